```python
import math
import jax, jax.numpy as jnp
from jax import lax
import numpy as np

D_MODEL = 1024
BATCH = 8
SEQ = 16384
DEPTH = 4

CHUNK = 64
MIX_A = D_MODEL // 2
MIX_B = D_MODEL // 2
MIX_C = D_MODEL // 2
MIX_D = D_MODEL // 2
S5_GROUP = 16
S5_GROUPS = MIX_A // S5_GROUP
S5_STATE = 64
CONV_W = 3
ATT_HEADS = 8
HEAD_DIM = MIX_C // ATT_HEADS
LEFT_CHUNKS = 8
BAND = (LEFT_CHUNKS + 1) * CHUNK
MAX_REL = 128
POOL_WINDOWS = (2, 4, 8, 16)
POOL_GROUP = MIX_D // len(POOL_WINDOWS)
D_FF = ((math.ceil(8 * D_MODEL / 3) + 255) // 256) * 256
D_PLE = 256
EV_IN = MIX_A + 3 * MIX_B
OD_IN = 3 * MIX_C + MIX_D
N_EVEN = (DEPTH + 1) // 2
N_ODD = DEPTH // 2
ALPHA = (2 * DEPTH) ** 0.25
BETA = (8 * DEPTH) ** -0.25
LN_EPS = 1e-5
NEG_INF = -1e30

kernel_name = "hybrid_s5_conv_chunkattn_pool_deepnorm"


def layer_norm(x, g, b):
    xf = x.astype(jnp.float32)
    mu = jnp.mean(xf, axis=-1, keepdims=True)
    var = jnp.mean(jnp.square(xf - mu), axis=-1, keepdims=True)
    y = (xf - mu) * lax.rsqrt(var + LN_EPS)
    return (y * g.astype(jnp.float32) + b.astype(jnp.float32)).astype(x.dtype)


def _complex_affine_combine(e1, e2):
    a1r, a1i, b1r, b1i = e1
    a2r, a2i, b2r, b2i = e2
    ar = a2r * a1r - a2i * a1i
    ai = a2r * a1i + a2i * a1r
    br = a2r * b1r - a2i * b1i + b2r
    bi = a2r * b1i + a2i * b1r + b2i
    return (ar, ai, br, bi)


def s5_mixer(u, lam_re, lam_im, log_dt, b_re, b_im, c_re, c_im, d_skip, w_glu, b_glu):
    f32 = jnp.float32
    bsz, L, _ = u.shape
    lre = lam_re.astype(f32)
    lim = lam_im.astype(f32)
    dt = jnp.exp(log_dt.astype(f32))[:, None]
    mag = jnp.exp(lre * dt)
    ang = lim * dt
    lb_re = mag * jnp.cos(ang)
    lb_im = mag * jnp.sin(ang)
    den = lre * lre + lim * lim
    nr = lb_re - 1.0
    ni = lb_im
    r_re = (nr * lre + ni * lim) / den
    r_im = (ni * lre - nr * lim) / den
    br = b_re.astype(f32)
    bi = b_im.astype(f32)
    bb_re = r_re[..., None] * br - r_im[..., None] * bi
    bb_im = r_re[..., None] * bi + r_im[..., None] * br
    uf = u.astype(f32).reshape(bsz, L, S5_GROUPS, S5_GROUP)
    bu_re = jnp.einsum('blgh,gph->blgp', uf, bb_re)
    bu_im = jnp.einsum('blgh,gph->blgp', uf, bb_im)
    a_re = jnp.broadcast_to(lb_re, bu_re.shape)
    a_im = jnp.broadcast_to(lb_im, bu_im.shape)
    _, _, xr, xi = lax.associative_scan(_complex_affine_combine, (a_re, a_im, bu_re, bu_im), axis=1)
    y = (jnp.einsum('ghp,blgp->blgh', c_re.astype(f32), xr)
         - jnp.einsum('ghp,blgp->blgh', c_im.astype(f32), xi)
         + d_skip.astype(f32).reshape(S5_GROUPS, S5_GROUP) * uf)
    y = y.reshape(bsz, L, MIX_A)
    g = jax.nn.gelu(y)
    out = g * jax.nn.sigmoid(g @ w_glu.astype(f32) + b_glu.astype(f32))
    return out.astype(u.dtype)


def short_conv_mixer(b_gate, c_gate, x_in, conv_w):
    L = x_in.shape[1]
    z = c_gate * x_in
    zp = jnp.pad(z, ((0, 0), (CONV_W - 1, 0), (0, 0)))
    y = conv_w[0] * zp[:, 0:L]
    for k in range(1, CONV_W):
        y = y + conv_w[k] * zp[:, k:k + L]
    return b_gate * y


def chunk_attention(q, k, v, rel_bias):
    bsz, L, _ = q.shape
    nc = L // CHUNK
    q = q.reshape(bsz, nc, CHUNK, ATT_HEADS, HEAD_DIM) * (HEAD_DIM ** -0.5)
    k = k.reshape(bsz, nc, CHUNK, ATT_HEADS, HEAD_DIM)
    v = v.reshape(bsz, nc, CHUNK, ATT_HEADS, HEAD_DIM)
    pad = ((0, 0), (LEFT_CHUNKS, 0), (0, 0), (0, 0), (0, 0))
    kp = jnp.pad(k, pad)
    vp = jnp.pad(v, pad)
    kb = jnp.concatenate([kp[:, j:j + nc] for j in range(LEFT_CHUNKS + 1)], axis=2)
    vb = jnp.concatenate([vp[:, j:j + nc] for j in range(LEFT_CHUNKS + 1)], axis=2)
    s = jnp.einsum('bcqhd,bckhd->bchqk', q, kb).astype(jnp.float32)
    qi = jnp.arange(CHUNK)[:, None]
    kj = jnp.arange(BAND)[None, :]
    rel = jnp.clip(qi + LEFT_CHUNKS * CHUNK - kj, -MAX_REL, MAX_REL) + MAX_REL
    bias = rel_bias.astype(jnp.float32)[:, rel]
    key_chunk = (jnp.arange(BAND) // CHUNK)[None, :] - LEFT_CHUNKS
    valid = (jnp.arange(nc)[:, None] + key_chunk) >= 0
    s = jnp.where(valid[None, :, None, None, :], s + bias[None, None], NEG_INF)
    pr = jax.nn.softmax(s, axis=-1).astype(vb.dtype)
    o = jnp.einsum('bchqk,bckhd->bcqhd', pr, vb)
    return o.reshape(bsz, L, MIX_C)


def pool_mixer(z, pool_w, pool_scale):
    bsz, L, _ = z.shape
    zf = z.astype(jnp.float32)
    cs = jnp.cumsum(zf, axis=1)
    t = jnp.arange(L)
    outs = []
    for gi, w in enumerate(POOL_WINDOWS):
        lo, hi = gi * POOL_GROUP, (gi + 1) * POOL_GROUP
        csg = cs[..., lo:hi]
        lagged = jnp.pad(csg[:, :L - w], ((0, 0), (w, 0), (0, 0)))
        count = jnp.minimum(t + 1, w).astype(jnp.float32)[None, :, None]
        outs.append((csg - lagged) / count - zf[..., lo:hi])
    pooled = jnp.stack(outs, axis=2)
    mixed = jnp.einsum('blgc,gcd->blgd', pooled, pool_w.astype(jnp.float32)).reshape(bsz, L, MIX_D)
    return (mixed * pool_scale.astype(jnp.float32)).astype(z.dtype)


def swiglu(x, w_up, w_down):
    h = x @ w_up
    g, u = jnp.split(h, 2, axis=-1)
    return (jax.nn.silu(g) * u) @ w_down


def _fwd_setup_inputs(seed: int = 0) -> dict:
    key = jax.random.key(seed)
    ks = jax.random.split(key, 32)
    f32 = jnp.float32

    def nrm(k, shape, scale):
        return scale * jax.random.normal(k, shape, f32)

    x = nrm(ks[0], (BATCH, SEQ, D_MODEL), 1.0)
    p = nrm(ks[1], (DEPTH, BATCH, SEQ, D_PLE), 1.0)
    ev_w_in = nrm(ks[2], (N_EVEN, D_MODEL, EV_IN), D_MODEL ** -0.5)
    n_idx = jnp.arange(S5_STATE, dtype=f32)
    ev_lambda_re = -0.5 + nrm(ks[3], (N_EVEN, S5_GROUPS, S5_STATE), 0.01)
    ev_lambda_im = math.pi * n_idx + nrm(ks[4], (N_EVEN, S5_GROUPS, S5_STATE), 0.01)
    ev_log_dt = jax.random.uniform(ks[5], (N_EVEN, S5_GROUPS), f32, math.log(1e-3), math.log(1e-1))
    ev_b_re = nrm(ks[6], (N_EVEN, S5_GROUPS, S5_STATE, S5_GROUP), (2 * S5_GROUP) ** -0.5)
    ev_b_im = nrm(ks[7], (N_EVEN, S5_GROUPS, S5_STATE, S5_GROUP), (2 * S5_GROUP) ** -0.5)
    ev_c_re = nrm(ks[8], (N_EVEN, S5_GROUPS, S5_GROUP, S5_STATE), S5_STATE ** -0.5)
    ev_c_im = nrm(ks[9], (N_EVEN, S5_GROUPS, S5_GROUP, S5_STATE), S5_STATE ** -0.5)
    ev_d = nrm(ks[10], (N_EVEN, MIX_A), 1.0)
    ev_w_glu = nrm(ks[11], (N_EVEN, MIX_A, MIX_A), MIX_A ** -0.5)
    ev_b_glu = nrm(ks[12], (N_EVEN, MIX_A), 0.02)
    ev_conv_w = nrm(ks[13], (N_EVEN, CONV_W, MIX_B), CONV_W ** -0.5)
    ev_w_out = nrm(ks[14], (N_EVEN, D_MODEL, D_MODEL), BETA * D_MODEL ** -0.5)
    od_w_in = nrm(ks[15], (N_ODD, D_MODEL, OD_IN), D_MODEL ** -0.5)
    od_rel_bias = nrm(ks[16], (N_ODD, ATT_HEADS, 2 * MAX_REL + 1), 0.1)
    od_pool_w = nrm(ks[17], (N_ODD, len(POOL_WINDOWS), POOL_GROUP, POOL_GROUP), POOL_GROUP ** -0.5)
    od_pool_scale = 1.0 + nrm(ks[18], (N_ODD, MIX_D), 0.1)
    od_w_out = nrm(ks[19], (N_ODD, D_MODEL, D_MODEL), BETA * D_MODEL ** -0.5)
    ln_mix_g = 1.0 + nrm(ks[20], (DEPTH, D_MODEL), 0.01)
    ln_mix_b = nrm(ks[21], (DEPTH, D_MODEL), 0.01)
    ln_ffn_g = 1.0 + nrm(ks[22], (DEPTH, D_MODEL), 0.01)
    ln_ffn_b = nrm(ks[23], (DEPTH, D_MODEL), 0.01)
    ffn_w_up = nrm(ks[24], (DEPTH, D_MODEL, 2 * D_FF), D_MODEL ** -0.5)
    ffn_w_down = nrm(ks[25], (DEPTH, D_FF, D_MODEL), BETA * D_FF ** -0.5)
    ple_w_proj = nrm(ks[26], (DEPTH, D_PLE, D_MODEL), D_PLE ** -0.5)
    ple_w_gate = nrm(ks[27], (DEPTH, D_MODEL, D_MODEL), D_MODEL ** -0.5)
    ple_b_gate = nrm(ks[28], (DEPTH, D_MODEL), 0.01)
    return {
        "x": x, "p": p,
        "ev_w_in": ev_w_in, "ev_lambda_re": ev_lambda_re, "ev_lambda_im": ev_lambda_im,
        "ev_log_dt": ev_log_dt, "ev_b_re": ev_b_re, "ev_b_im": ev_b_im,
        "ev_c_re": ev_c_re, "ev_c_im": ev_c_im, "ev_d": ev_d,
        "ev_w_glu": ev_w_glu, "ev_b_glu": ev_b_glu, "ev_conv_w": ev_conv_w, "ev_w_out": ev_w_out,
        "od_w_in": od_w_in, "od_rel_bias": od_rel_bias, "od_pool_w": od_pool_w,
        "od_pool_scale": od_pool_scale, "od_w_out": od_w_out,
        "ln_mix_g": ln_mix_g, "ln_mix_b": ln_mix_b, "ln_ffn_g": ln_ffn_g, "ln_ffn_b": ln_ffn_b,
        "ffn_w_up": ffn_w_up, "ffn_w_down": ffn_w_down,
        "ple_w_proj": ple_w_proj, "ple_w_gate": ple_w_gate, "ple_b_gate": ple_b_gate,
    }


def _fwd_reference(x, p, ev_w_in, ev_lambda_re, ev_lambda_im, ev_log_dt, ev_b_re, ev_b_im,
              ev_c_re, ev_c_im, ev_d, ev_w_glu, ev_b_glu, ev_conv_w, ev_w_out,
              od_w_in, od_rel_bias, od_pool_w, od_pool_scale, od_w_out,
              ln_mix_g, ln_mix_b, ln_ffn_g, ln_ffn_b, ffn_w_up, ffn_w_down,
              ple_w_proj, ple_w_gate, ple_b_gate):
    for i in range(DEPTH):
        if i % 2 == 0:
            e = i // 2
            h = x @ ev_w_in[e]
            u_a, b_g, c_g, x_b = jnp.split(h, [MIX_A, MIX_A + MIX_B, MIX_A + 2 * MIX_B], axis=-1)
            y_a = s5_mixer(u_a, ev_lambda_re[e], ev_lambda_im[e], ev_log_dt[e], ev_b_re[e], ev_b_im[e],
                           ev_c_re[e], ev_c_im[e], ev_d[e], ev_w_glu[e], ev_b_glu[e])
            y_b = short_conv_mixer(b_g, c_g, x_b, ev_conv_w[e])
            mix = jnp.concatenate([y_a, y_b], axis=-1) @ ev_w_out[e]
        else:
            o = i // 2
            h = x @ od_w_in[o]
            q, k, v, z = jnp.split(h, [MIX_C, 2 * MIX_C, 3 * MIX_C], axis=-1)
            y_c = chunk_attention(q, k, v, od_rel_bias[o])
            y_d = pool_mixer(z, od_pool_w[o], od_pool_scale[o])
            mix = jnp.concatenate([y_c, y_d], axis=-1) @ od_w_out[o]
        x = layer_norm(ALPHA * x + mix, ln_mix_g[i], ln_mix_b[i])
        x = layer_norm(ALPHA * x + swiglu(x, ffn_w_up[i], ffn_w_down[i]), ln_ffn_g[i], ln_ffn_b[i])
        x = x + jax.nn.sigmoid(x @ ple_w_gate[i] + ple_b_gate[i]) * (p[i] @ ple_w_proj[i])
    return x


import jax as _jax
import jax.numpy as _jnp

TWIN_FORMAT = 'train_step'
FWD_PARAMS = ['x', 'p', 'ev_w_in', 'ev_lambda_re', 'ev_lambda_im', 'ev_log_dt', 'ev_b_re', 'ev_b_im', 'ev_c_re', 'ev_c_im', 'ev_d', 'ev_w_glu', 'ev_b_glu', 'ev_conv_w', 'ev_w_out', 'od_w_in', 'od_rel_bias', 'od_pool_w', 'od_pool_scale', 'od_w_out', 'ln_mix_g', 'ln_mix_b', 'ln_ffn_g', 'ln_ffn_b', 'ffn_w_up', 'ffn_w_down', 'ple_w_proj', 'ple_w_gate', 'ple_b_gate']
TWIN_WEIGHTS = ['ev_w_in', 'ev_lambda_re', 'ev_lambda_im', 'ev_log_dt', 'ev_b_re', 'ev_b_im', 'ev_c_re', 'ev_c_im', 'ev_d', 'ev_w_glu', 'ev_b_glu', 'ev_conv_w', 'ev_w_out', 'od_w_in', 'od_rel_bias', 'od_pool_w', 'od_pool_scale', 'od_w_out', 'ln_mix_g', 'ln_mix_b', 'ln_ffn_g', 'ln_ffn_b', 'ffn_w_up', 'ffn_w_down', 'ple_w_proj', 'ple_w_gate', 'ple_b_gate']
TWIN_DIFF_INPUT = 'x'
TWIN_INPUTS = ['x', 'p', 'ev_w_in', 'ev_lambda_re', 'ev_lambda_im', 'ev_log_dt', 'ev_b_re', 'ev_b_im', 'ev_c_re', 'ev_c_im', 'ev_d', 'ev_w_glu', 'ev_b_glu', 'ev_conv_w', 'ev_w_out', 'od_w_in', 'od_rel_bias', 'od_pool_w', 'od_pool_scale', 'od_w_out', 'ln_mix_g', 'ln_mix_b', 'ln_ffn_g', 'ln_ffn_b', 'ffn_w_up', 'ffn_w_down', 'ple_w_proj', 'ple_w_gate', 'ple_b_gate', 'loss_target', 'm_ev_w_in', 'm_ev_lambda_re', 'm_ev_lambda_im', 'm_ev_log_dt', 'm_ev_b_re', 'm_ev_b_im', 'm_ev_c_re', 'm_ev_c_im', 'm_ev_d', 'm_ev_w_glu', 'm_ev_b_glu', 'm_ev_conv_w', 'm_ev_w_out', 'm_od_w_in', 'm_od_rel_bias', 'm_od_pool_w', 'm_od_pool_scale', 'm_od_w_out', 'm_ln_mix_g', 'm_ln_mix_b', 'm_ln_ffn_g', 'm_ln_ffn_b', 'm_ffn_w_up', 'm_ffn_w_down', 'm_ple_w_proj', 'm_ple_w_gate', 'm_ple_b_gate', 'v_ev_w_in', 'v_ev_lambda_re', 'v_ev_lambda_im', 'v_ev_log_dt', 'v_ev_b_re', 'v_ev_b_im', 'v_ev_c_re', 'v_ev_c_im', 'v_ev_d', 'v_ev_w_glu', 'v_ev_b_glu', 'v_ev_conv_w', 'v_ev_w_out', 'v_od_w_in', 'v_od_rel_bias', 'v_od_pool_w', 'v_od_pool_scale', 'v_od_w_out', 'v_ln_mix_g', 'v_ln_mix_b', 'v_ln_ffn_g', 'v_ln_ffn_b', 'v_ffn_w_up', 'v_ffn_w_down', 'v_ple_w_proj', 'v_ple_w_gate', 'v_ple_b_gate']
TWIN_OUTPUTS = ['loss', 'grad_x', 'grad_ev_w_in', 'grad_ev_lambda_re', 'grad_ev_lambda_im', 'grad_ev_log_dt', 'grad_ev_b_re', 'grad_ev_b_im', 'grad_ev_c_re', 'grad_ev_c_im', 'grad_ev_d', 'grad_ev_w_glu', 'grad_ev_b_glu', 'grad_ev_conv_w', 'grad_ev_w_out', 'grad_od_w_in', 'grad_od_rel_bias', 'grad_od_pool_w', 'grad_od_pool_scale', 'grad_od_w_out', 'grad_ln_mix_g', 'grad_ln_mix_b', 'grad_ln_ffn_g', 'grad_ln_ffn_b', 'grad_ffn_w_up', 'grad_ffn_w_down', 'grad_ple_w_proj', 'grad_ple_w_gate', 'grad_ple_b_gate', 'delta_ev_w_in', 'delta_ev_lambda_re', 'delta_ev_lambda_im', 'delta_ev_log_dt', 'delta_ev_b_re', 'delta_ev_b_im', 'delta_ev_c_re', 'delta_ev_c_im', 'delta_ev_d', 'delta_ev_w_glu', 'delta_ev_b_glu', 'delta_ev_conv_w', 'delta_ev_w_out', 'delta_od_w_in', 'delta_od_rel_bias', 'delta_od_pool_w', 'delta_od_pool_scale', 'delta_od_w_out', 'delta_ln_mix_g', 'delta_ln_mix_b', 'delta_ln_ffn_g', 'delta_ln_ffn_b', 'delta_ffn_w_up', 'delta_ffn_w_down', 'delta_ple_w_proj', 'delta_ple_w_gate', 'delta_ple_b_gate', 'new_m_ev_w_in', 'new_m_ev_lambda_re', 'new_m_ev_lambda_im', 'new_m_ev_log_dt', 'new_m_ev_b_re', 'new_m_ev_b_im', 'new_m_ev_c_re', 'new_m_ev_c_im', 'new_m_ev_d', 'new_m_ev_w_glu', 'new_m_ev_b_glu', 'new_m_ev_conv_w', 'new_m_ev_w_out', 'new_m_od_w_in', 'new_m_od_rel_bias', 'new_m_od_pool_w', 'new_m_od_pool_scale', 'new_m_od_w_out', 'new_m_ln_mix_g', 'new_m_ln_mix_b', 'new_m_ln_ffn_g', 'new_m_ln_ffn_b', 'new_m_ffn_w_up', 'new_m_ffn_w_down', 'new_m_ple_w_proj', 'new_m_ple_w_gate', 'new_m_ple_b_gate', 'new_v_ev_w_in', 'new_v_ev_lambda_re', 'new_v_ev_lambda_im', 'new_v_ev_log_dt', 'new_v_ev_b_re', 'new_v_ev_b_im', 'new_v_ev_c_re', 'new_v_ev_c_im', 'new_v_ev_d', 'new_v_ev_w_glu', 'new_v_ev_b_glu', 'new_v_ev_conv_w', 'new_v_ev_w_out', 'new_v_od_w_in', 'new_v_od_rel_bias', 'new_v_od_pool_w', 'new_v_od_pool_scale', 'new_v_od_w_out', 'new_v_ln_mix_g', 'new_v_ln_mix_b', 'new_v_ln_ffn_g', 'new_v_ln_ffn_b', 'new_v_ffn_w_up', 'new_v_ffn_w_down', 'new_v_ple_w_proj', 'new_v_ple_w_gate', 'new_v_ple_b_gate']
TWIN_LEAF_KINDS = {'loss': 'loss', 'grad_x': 'grad_x', 'grad_ev_w_in': 'grad_w', 'grad_ev_lambda_re': 'grad_w', 'grad_ev_lambda_im': 'grad_w', 'grad_ev_log_dt': 'grad_w', 'grad_ev_b_re': 'grad_w', 'grad_ev_b_im': 'grad_w', 'grad_ev_c_re': 'grad_w', 'grad_ev_c_im': 'grad_w', 'grad_ev_d': 'grad_w', 'grad_ev_w_glu': 'grad_w', 'grad_ev_b_glu': 'grad_w', 'grad_ev_conv_w': 'grad_w', 'grad_ev_w_out': 'grad_w', 'grad_od_w_in': 'grad_w', 'grad_od_rel_bias': 'grad_w', 'grad_od_pool_w': 'grad_w', 'grad_od_pool_scale': 'grad_w', 'grad_od_w_out': 'grad_w', 'grad_ln_mix_g': 'grad_w', 'grad_ln_mix_b': 'grad_w', 'grad_ln_ffn_g': 'grad_w', 'grad_ln_ffn_b': 'grad_w', 'grad_ffn_w_up': 'grad_w', 'grad_ffn_w_down': 'grad_w', 'grad_ple_w_proj': 'grad_w', 'grad_ple_w_gate': 'grad_w', 'grad_ple_b_gate': 'grad_w', 'delta_ev_w_in': 'delta_w', 'delta_ev_lambda_re': 'delta_w', 'delta_ev_lambda_im': 'delta_w', 'delta_ev_log_dt': 'delta_w', 'delta_ev_b_re': 'delta_w', 'delta_ev_b_im': 'delta_w', 'delta_ev_c_re': 'delta_w', 'delta_ev_c_im': 'delta_w', 'delta_ev_d': 'delta_w', 'delta_ev_w_glu': 'delta_w', 'delta_ev_b_glu': 'delta_w', 'delta_ev_conv_w': 'delta_w', 'delta_ev_w_out': 'delta_w', 'delta_od_w_in': 'delta_w', 'delta_od_rel_bias': 'delta_w', 'delta_od_pool_w': 'delta_w', 'delta_od_pool_scale': 'delta_w', 'delta_od_w_out': 'delta_w', 'delta_ln_mix_g': 'delta_w', 'delta_ln_mix_b': 'delta_w', 'delta_ln_ffn_g': 'delta_w', 'delta_ln_ffn_b': 'delta_w', 'delta_ffn_w_up': 'delta_w', 'delta_ffn_w_down': 'delta_w', 'delta_ple_w_proj': 'delta_w', 'delta_ple_w_gate': 'delta_w', 'delta_ple_b_gate': 'delta_w', 'new_m_ev_w_in': 'new_m', 'new_m_ev_lambda_re': 'new_m', 'new_m_ev_lambda_im': 'new_m', 'new_m_ev_log_dt': 'new_m', 'new_m_ev_b_re': 'new_m', 'new_m_ev_b_im': 'new_m', 'new_m_ev_c_re': 'new_m', 'new_m_ev_c_im': 'new_m', 'new_m_ev_d': 'new_m', 'new_m_ev_w_glu': 'new_m', 'new_m_ev_b_glu': 'new_m', 'new_m_ev_conv_w': 'new_m', 'new_m_ev_w_out': 'new_m', 'new_m_od_w_in': 'new_m', 'new_m_od_rel_bias': 'new_m', 'new_m_od_pool_w': 'new_m', 'new_m_od_pool_scale': 'new_m', 'new_m_od_w_out': 'new_m', 'new_m_ln_mix_g': 'new_m', 'new_m_ln_mix_b': 'new_m', 'new_m_ln_ffn_g': 'new_m', 'new_m_ln_ffn_b': 'new_m', 'new_m_ffn_w_up': 'new_m', 'new_m_ffn_w_down': 'new_m', 'new_m_ple_w_proj': 'new_m', 'new_m_ple_w_gate': 'new_m', 'new_m_ple_b_gate': 'new_m', 'new_v_ev_w_in': 'new_v', 'new_v_ev_lambda_re': 'new_v', 'new_v_ev_lambda_im': 'new_v', 'new_v_ev_log_dt': 'new_v', 'new_v_ev_b_re': 'new_v', 'new_v_ev_b_im': 'new_v', 'new_v_ev_c_re': 'new_v', 'new_v_ev_c_im': 'new_v', 'new_v_ev_d': 'new_v', 'new_v_ev_w_glu': 'new_v', 'new_v_ev_b_glu': 'new_v', 'new_v_ev_conv_w': 'new_v', 'new_v_ev_w_out': 'new_v', 'new_v_od_w_in': 'new_v', 'new_v_od_rel_bias': 'new_v', 'new_v_od_pool_w': 'new_v', 'new_v_od_pool_scale': 'new_v', 'new_v_od_w_out': 'new_v', 'new_v_ln_mix_g': 'new_v', 'new_v_ln_mix_b': 'new_v', 'new_v_ln_ffn_g': 'new_v', 'new_v_ln_ffn_b': 'new_v', 'new_v_ffn_w_up': 'new_v', 'new_v_ffn_w_down': 'new_v', 'new_v_ple_w_proj': 'new_v', 'new_v_ple_w_gate': 'new_v', 'new_v_ple_b_gate': 'new_v'}


def _forward(args):
    return _fwd_reference(*[args[k] for k in FWD_PARAMS])


def _output_shape():
    def fwd():
        inp = _fwd_setup_inputs(0)
        return _fwd_reference(*[inp[k] for k in FWD_PARAMS])
    out = _jax.eval_shape(fwd)
    return out.shape, out.dtype

N_MICROBATCH = 1
ADAM_LR = 0.001
ADAM_B1 = 0.9
ADAM_B2 = 0.999
ADAM_EPS = 1e-08
ADAM_WD = 0.01
ADAM_STEP = 10
PER_EXAMPLE_BATCH_AXIS = {'x': 0, 'p': 1, 'loss_target': 0}
SHARED_INPUTS = []
_WEIGHT_DTYPES = {'ev_w_in': _jnp.float32, 'ev_lambda_re': _jnp.float32, 'ev_lambda_im': _jnp.float32, 'ev_log_dt': _jnp.float32, 'ev_b_re': _jnp.float32, 'ev_b_im': _jnp.float32, 'ev_c_re': _jnp.float32, 'ev_c_im': _jnp.float32, 'ev_d': _jnp.float32, 'ev_w_glu': _jnp.float32, 'ev_b_glu': _jnp.float32, 'ev_conv_w': _jnp.float32, 'ev_w_out': _jnp.float32, 'od_w_in': _jnp.float32, 'od_rel_bias': _jnp.float32, 'od_pool_w': _jnp.float32, 'od_pool_scale': _jnp.float32, 'od_w_out': _jnp.float32, 'ln_mix_g': _jnp.float32, 'ln_mix_b': _jnp.float32, 'ln_ffn_g': _jnp.float32, 'ln_ffn_b': _jnp.float32, 'ffn_w_up': _jnp.float32, 'ffn_w_down': _jnp.float32, 'ple_w_proj': _jnp.float32, 'ple_w_gate': _jnp.float32, 'ple_b_gate': _jnp.float32}
MOMENT_SCALE = {'ev_w_in': 9.220044e-02, 'ev_lambda_re': 4.482884e-03, 'ev_lambda_im': 3.291285e-03, 'ev_log_dt': 1.942772e+00, 'ev_b_re': 1.931142e-03, 'ev_b_im': 1.964821e-03, 'ev_c_re': 2.690280e-03, 'ev_c_im': 2.977412e-03, 'ev_d': 3.747523e-01, 'ev_w_glu': 4.617960e-02, 'ev_b_glu': 1.642637e-01, 'ev_conv_w': 1.046140e-01, 'ev_w_out': 3.856175e-01, 'od_w_in': 4.758792e-02, 'od_rel_bias': 8.241222e-03, 'od_pool_w': 8.721284e-02, 'od_pool_scale': 8.618505e-02, 'od_w_out': 1.602685e-01, 'ln_mix_g': 9.818609e-01, 'ln_mix_b': 9.982069e+00, 'ln_ffn_g': 6.572548e+01, 'ln_ffn_b': 1.020824e+01, 'ffn_w_up': 3.593754e-02, 'ffn_w_down': 1.407332e-01, 'ple_w_proj': 6.975619e-01, 'ple_w_gate': 1.538678e-01, 'ple_b_gate': 6.509741e+00}


def _to_microbatches(a, axis):
    t = _jnp.moveaxis(a, axis, 0)
    t = t.reshape((N_MICROBATCH, t.shape[0] // N_MICROBATCH) + t.shape[1:])
    return _jnp.moveaxis(t, 1, axis + 1)


def setup_inputs(seed: int = 0) -> dict:
    inp = _fwd_setup_inputs(seed)
    key = _jax.random.fold_in(_jax.random.key(seed), 7919)
    shape, _ = _output_shape()
    out = dict(inp)
    out["loss_target"] = _jax.random.normal(_jax.random.fold_in(key, 0), shape, _jnp.float32)
    for i, name in enumerate(TWIN_WEIGHTS):
        w = inp[name].astype(_jnp.float32)
        if MOMENT_SCALE is None:
            s = _jnp.sqrt(_jnp.mean(_jnp.square(w)) + 1e-30)
        else:
            s = MOMENT_SCALE[name]
        km, kv = _jax.random.split(_jax.random.fold_in(key, i + 1))
        out[name] = w
        out["m_" + name] = s * _jax.random.normal(km, w.shape, _jnp.float32)
        out["v_" + name] = (s * s) * _jax.random.uniform(kv, w.shape, _jnp.float32, 0.5, 1.5)
    if N_MICROBATCH > 1:
        for name, axis in PER_EXAMPLE_BATCH_AXIS.items():
            out[name] = _to_microbatches(out[name], axis)
    return {'x': out['x'], 'p': out['p'], 'ev_w_in': out['ev_w_in'], 'ev_lambda_re': out['ev_lambda_re'], 'ev_lambda_im': out['ev_lambda_im'], 'ev_log_dt': out['ev_log_dt'], 'ev_b_re': out['ev_b_re'], 'ev_b_im': out['ev_b_im'], 'ev_c_re': out['ev_c_re'], 'ev_c_im': out['ev_c_im'], 'ev_d': out['ev_d'], 'ev_w_glu': out['ev_w_glu'], 'ev_b_glu': out['ev_b_glu'], 'ev_conv_w': out['ev_conv_w'], 'ev_w_out': out['ev_w_out'], 'od_w_in': out['od_w_in'], 'od_rel_bias': out['od_rel_bias'], 'od_pool_w': out['od_pool_w'], 'od_pool_scale': out['od_pool_scale'], 'od_w_out': out['od_w_out'], 'ln_mix_g': out['ln_mix_g'], 'ln_mix_b': out['ln_mix_b'], 'ln_ffn_g': out['ln_ffn_g'], 'ln_ffn_b': out['ln_ffn_b'], 'ffn_w_up': out['ffn_w_up'], 'ffn_w_down': out['ffn_w_down'], 'ple_w_proj': out['ple_w_proj'], 'ple_w_gate': out['ple_w_gate'], 'ple_b_gate': out['ple_b_gate'], 'loss_target': out['loss_target'], 'm_ev_w_in': out['m_ev_w_in'], 'm_ev_lambda_re': out['m_ev_lambda_re'], 'm_ev_lambda_im': out['m_ev_lambda_im'], 'm_ev_log_dt': out['m_ev_log_dt'], 'm_ev_b_re': out['m_ev_b_re'], 'm_ev_b_im': out['m_ev_b_im'], 'm_ev_c_re': out['m_ev_c_re'], 'm_ev_c_im': out['m_ev_c_im'], 'm_ev_d': out['m_ev_d'], 'm_ev_w_glu': out['m_ev_w_glu'], 'm_ev_b_glu': out['m_ev_b_glu'], 'm_ev_conv_w': out['m_ev_conv_w'], 'm_ev_w_out': out['m_ev_w_out'], 'm_od_w_in': out['m_od_w_in'], 'm_od_rel_bias': out['m_od_rel_bias'], 'm_od_pool_w': out['m_od_pool_w'], 'm_od_pool_scale': out['m_od_pool_scale'], 'm_od_w_out': out['m_od_w_out'], 'm_ln_mix_g': out['m_ln_mix_g'], 'm_ln_mix_b': out['m_ln_mix_b'], 'm_ln_ffn_g': out['m_ln_ffn_g'], 'm_ln_ffn_b': out['m_ln_ffn_b'], 'm_ffn_w_up': out['m_ffn_w_up'], 'm_ffn_w_down': out['m_ffn_w_down'], 'm_ple_w_proj': out['m_ple_w_proj'], 'm_ple_w_gate': out['m_ple_w_gate'], 'm_ple_b_gate': out['m_ple_b_gate'], 'v_ev_w_in': out['v_ev_w_in'], 'v_ev_lambda_re': out['v_ev_lambda_re'], 'v_ev_lambda_im': out['v_ev_lambda_im'], 'v_ev_log_dt': out['v_ev_log_dt'], 'v_ev_b_re': out['v_ev_b_re'], 'v_ev_b_im': out['v_ev_b_im'], 'v_ev_c_re': out['v_ev_c_re'], 'v_ev_c_im': out['v_ev_c_im'], 'v_ev_d': out['v_ev_d'], 'v_ev_w_glu': out['v_ev_w_glu'], 'v_ev_b_glu': out['v_ev_b_glu'], 'v_ev_conv_w': out['v_ev_conv_w'], 'v_ev_w_out': out['v_ev_w_out'], 'v_od_w_in': out['v_od_w_in'], 'v_od_rel_bias': out['v_od_rel_bias'], 'v_od_pool_w': out['v_od_pool_w'], 'v_od_pool_scale': out['v_od_pool_scale'], 'v_od_w_out': out['v_od_w_out'], 'v_ln_mix_g': out['v_ln_mix_g'], 'v_ln_mix_b': out['v_ln_mix_b'], 'v_ln_ffn_g': out['v_ln_ffn_g'], 'v_ln_ffn_b': out['v_ln_ffn_b'], 'v_ffn_w_up': out['v_ffn_w_up'], 'v_ffn_w_down': out['v_ffn_w_down'], 'v_ple_w_proj': out['v_ple_w_proj'], 'v_ple_w_gate': out['v_ple_w_gate'], 'v_ple_b_gate': out['v_ple_b_gate']}


def _loss(weights, diff, rest, loss_target):
    with _jax.named_scope("forward"):
        args = {**rest, TWIN_DIFF_INPUT: diff, **{k: w.astype(_WEIGHT_DTYPES[k]) for k, w in weights.items()}}
        y = _forward(args)
    with _jax.named_scope("loss_head"):
        err = _jnp.square(y.astype(_jnp.float32) - loss_target)
        return 0.5 * _jnp.sum(_jnp.mean(err, axis=-1)) if err.ndim else 0.5 * err


def _adamw(w, g, m, v):
    m = ADAM_B1 * m + (1.0 - ADAM_B1) * g
    v = ADAM_B2 * v + (1.0 - ADAM_B2) * _jnp.square(g)
    m_hat = m / (1.0 - ADAM_B1 ** ADAM_STEP)
    v_hat = v / (1.0 - ADAM_B2 ** ADAM_STEP)
    delta = -ADAM_LR * (m_hat / (_jnp.sqrt(v_hat) + ADAM_EPS) + ADAM_WD * w)
    return delta, m, v


def reference(x, p, ev_w_in, ev_lambda_re, ev_lambda_im, ev_log_dt, ev_b_re, ev_b_im, ev_c_re, ev_c_im, ev_d, ev_w_glu, ev_b_glu, ev_conv_w, ev_w_out, od_w_in, od_rel_bias, od_pool_w, od_pool_scale, od_w_out, ln_mix_g, ln_mix_b, ln_ffn_g, ln_ffn_b, ffn_w_up, ffn_w_down, ple_w_proj, ple_w_gate, ple_b_gate, loss_target, m_ev_w_in, m_ev_lambda_re, m_ev_lambda_im, m_ev_log_dt, m_ev_b_re, m_ev_b_im, m_ev_c_re, m_ev_c_im, m_ev_d, m_ev_w_glu, m_ev_b_glu, m_ev_conv_w, m_ev_w_out, m_od_w_in, m_od_rel_bias, m_od_pool_w, m_od_pool_scale, m_od_w_out, m_ln_mix_g, m_ln_mix_b, m_ln_ffn_g, m_ln_ffn_b, m_ffn_w_up, m_ffn_w_down, m_ple_w_proj, m_ple_w_gate, m_ple_b_gate, v_ev_w_in, v_ev_lambda_re, v_ev_lambda_im, v_ev_log_dt, v_ev_b_re, v_ev_b_im, v_ev_c_re, v_ev_c_im, v_ev_d, v_ev_w_glu, v_ev_b_glu, v_ev_conv_w, v_ev_w_out, v_od_w_in, v_od_rel_bias, v_od_pool_w, v_od_pool_scale, v_od_w_out, v_ln_mix_g, v_ln_mix_b, v_ln_ffn_g, v_ln_ffn_b, v_ffn_w_up, v_ffn_w_down, v_ple_w_proj, v_ple_w_gate, v_ple_b_gate):
    given = dict(x=x, p=p, ev_w_in=ev_w_in, ev_lambda_re=ev_lambda_re, ev_lambda_im=ev_lambda_im, ev_log_dt=ev_log_dt, ev_b_re=ev_b_re, ev_b_im=ev_b_im, ev_c_re=ev_c_re, ev_c_im=ev_c_im, ev_d=ev_d, ev_w_glu=ev_w_glu, ev_b_glu=ev_b_glu, ev_conv_w=ev_conv_w, ev_w_out=ev_w_out, od_w_in=od_w_in, od_rel_bias=od_rel_bias, od_pool_w=od_pool_w, od_pool_scale=od_pool_scale, od_w_out=od_w_out, ln_mix_g=ln_mix_g, ln_mix_b=ln_mix_b, ln_ffn_g=ln_ffn_g, ln_ffn_b=ln_ffn_b, ffn_w_up=ffn_w_up, ffn_w_down=ffn_w_down, ple_w_proj=ple_w_proj, ple_w_gate=ple_w_gate, ple_b_gate=ple_b_gate, loss_target=loss_target, m_ev_w_in=m_ev_w_in, m_ev_lambda_re=m_ev_lambda_re, m_ev_lambda_im=m_ev_lambda_im, m_ev_log_dt=m_ev_log_dt, m_ev_b_re=m_ev_b_re, m_ev_b_im=m_ev_b_im, m_ev_c_re=m_ev_c_re, m_ev_c_im=m_ev_c_im, m_ev_d=m_ev_d, m_ev_w_glu=m_ev_w_glu, m_ev_b_glu=m_ev_b_glu, m_ev_conv_w=m_ev_conv_w, m_ev_w_out=m_ev_w_out, m_od_w_in=m_od_w_in, m_od_rel_bias=m_od_rel_bias, m_od_pool_w=m_od_pool_w, m_od_pool_scale=m_od_pool_scale, m_od_w_out=m_od_w_out, m_ln_mix_g=m_ln_mix_g, m_ln_mix_b=m_ln_mix_b, m_ln_ffn_g=m_ln_ffn_g, m_ln_ffn_b=m_ln_ffn_b, m_ffn_w_up=m_ffn_w_up, m_ffn_w_down=m_ffn_w_down, m_ple_w_proj=m_ple_w_proj, m_ple_w_gate=m_ple_w_gate, m_ple_b_gate=m_ple_b_gate, v_ev_w_in=v_ev_w_in, v_ev_lambda_re=v_ev_lambda_re, v_ev_lambda_im=v_ev_lambda_im, v_ev_log_dt=v_ev_log_dt, v_ev_b_re=v_ev_b_re, v_ev_b_im=v_ev_b_im, v_ev_c_re=v_ev_c_re, v_ev_c_im=v_ev_c_im, v_ev_d=v_ev_d, v_ev_w_glu=v_ev_w_glu, v_ev_b_glu=v_ev_b_glu, v_ev_conv_w=v_ev_conv_w, v_ev_w_out=v_ev_w_out, v_od_w_in=v_od_w_in, v_od_rel_bias=v_od_rel_bias, v_od_pool_w=v_od_pool_w, v_od_pool_scale=v_od_pool_scale, v_od_w_out=v_od_w_out, v_ln_mix_g=v_ln_mix_g, v_ln_mix_b=v_ln_mix_b, v_ln_ffn_g=v_ln_ffn_g, v_ln_ffn_b=v_ln_ffn_b, v_ffn_w_up=v_ffn_w_up, v_ffn_w_down=v_ffn_w_down, v_ple_w_proj=v_ple_w_proj, v_ple_w_gate=v_ple_w_gate, v_ple_b_gate=v_ple_b_gate)
    weights = {n: given[n] for n in TWIN_WEIGHTS}
    shared = {n: given[n] for n in SHARED_INPUTS}
    per_example = {n: given[n] for n in ['x', 'p']}
    grad_fn = _jax.value_and_grad(_loss, argnums=(0, 1))

    def one_microbatch(ex, loss_target):
        ex = dict(ex)
        diff = ex.pop(TWIN_DIFF_INPUT)
        return grad_fn(weights, diff, {**shared, **ex}, loss_target)

    if N_MICROBATCH == 1:
        loss, (grad_w, grad_x) = one_microbatch(per_example, given["loss_target"])
    else:
        def body(carry, xs):
            loss_sum, grad_sum = carry
            l_k, (gw_k, gx_k) = one_microbatch(xs[0], xs[1])
            with _jax.named_scope("update"):
                return (loss_sum + l_k, _jax.tree.map(_jnp.add, grad_sum, gw_k)), gx_k

        init = (_jnp.zeros((), _jnp.float32), _jax.tree.map(_jnp.zeros_like, weights))
        (loss, grad_w), grad_x = _jax.lax.scan(body, init, (per_example, given["loss_target"]))
    with _jax.named_scope("update"):
        delta_w, new_m, new_v = {}, {}, {}
        for n in TWIN_WEIGHTS:
            delta_w[n], new_m[n], new_v[n] = _adamw(weights[n], grad_w[n], given["m_" + n], given["v_" + n])
    return (loss, grad_x, *[grad_w[n] for n in TWIN_WEIGHTS], *[delta_w[n] for n in TWIN_WEIGHTS],
            *[new_m[n] for n in TWIN_WEIGHTS], *[new_v[n] for n in TWIN_WEIGHTS])
```

```python
import functools
import math

import jax
import jax.numpy as jnp
import numpy as np
from jax import lax
from jax.experimental import pallas as pl
from jax.experimental.pallas import tpu as pltpu

F32 = jnp.float32
BF16 = jnp.bfloat16
MESH = pl.DeviceIdType.MESH
HIGHEST = lax.Precision.HIGHEST

D_MODEL = 1024
DEPTH = 4
MIX = 512
S5_GROUPS = 32
S5_GROUP = 16
S5_STATE = 64
S5_N = S5_GROUPS * S5_STATE
CHUNK = 64
LEFT_CHUNKS = 8
MAX_REL = 128
ATT_HEADS = 8
HEAD_DIM = 64
POOL_WINDOWS = (2, 4, 8, 16)
POOL_GROUP = 128
D_FF = 2816
D_PLE = 256
ALPHA = (2 * DEPTH) ** 0.25
LN_EPS = 1e-5
NEG_INF = -1e30
N_CHIPS = 4
N_DEV = 8

ADAM_LR = 0.001
ADAM_B1 = 0.9
ADAM_B2 = 0.999
ADAM_EPS = 1e-08
ADAM_WD = 0.01
ADAM_STEP = 10

TM = 512
T_S5 = 256
S5_LEVELS = 8
T_ATT = 512
VMEM_BIG = 56 * 1024 * 1024


VMEM_DEFAULT = 48 * 1024 * 1024


def _cparams(sem, vmem=None):
    return pltpu.CompilerParams(dimension_semantics=sem, vmem_limit_bytes=vmem or VMEM_DEFAULT)


def _sigmoid(x):
    return 1.0 / (1.0 + jnp.exp(-x))


def _mm(a_parts, b, *, name, trans_b=False, out_dtype=F32, tm=TM, tn=1024, vmem=None):
    m = a_parts[0][0].shape[0]
    n = b.shape[0] if trans_b else b.shape[1]
    kk = b.shape[1] if trans_b else b.shape[0]
    tn = min(tn, n)
    widths = [w for _, _, w in a_parts]
    assert sum(widths) == kk and m % tm == 0 and n % tn == 0
    na = len(a_parts)

    def body(*refs):
        b_ref, o_ref = refs[na], refs[na + 1]
        acc = None
        k0 = 0
        for ar, w in zip(refs[:na], widths):
            a = ar[...].astype(BF16)
            if trans_b:
                part = lax.dot_general(a, b_ref[:, k0:k0 + w], (((1,), (1,)), ((), ())), preferred_element_type=F32)
            else:
                part = jnp.dot(a, b_ref[k0:k0 + w, :], preferred_element_type=F32)
            acc = part if acc is None else acc + part
            k0 += w
        o_ref[...] = acc.astype(o_ref.dtype)

    in_specs = [pl.BlockSpec((tm, w), functools.partial(lambda j, i, cb: (i, cb), cb=cb)) for _, cb, w in a_parts]
    if trans_b:
        in_specs.append(pl.BlockSpec((tn, kk), lambda j, i: (j, 0)))
    else:
        in_specs.append(pl.BlockSpec((kk, tn), lambda j, i: (0, j)))
    return pl.pallas_call(
        body, name=name, grid=(n // tn, m // tm), in_specs=in_specs,
        out_specs=pl.BlockSpec((tm, tn), lambda j, i: (i, j)),
        out_shape=jax.ShapeDtypeStruct((m, n), out_dtype),
        compiler_params=_cparams(("parallel", "parallel"), vmem),
    )(*[a for a, _, _ in a_parts], b)


def _mm_tn(a, a_cb, ka, b, *, name, tk=1024, tn=1024, tmr=TM, vmem=None):
    m = a.shape[0]
    n = b.shape[1]
    tk = min(tk, ka)
    tn = min(tn, n)
    assert ka % tk == 0 and n % tn == 0 and m % tmr == 0
    kb = ka // tk

    def body(a_ref, b_ref, o_ref):
        @pl.when(pl.program_id(2) == 0)
        def _():
            o_ref[...] = jnp.zeros_like(o_ref)

        o_ref[...] += lax.dot_general(a_ref[...].astype(BF16), b_ref[...].astype(BF16), (((0,), (0,)), ((), ())),
                                      preferred_element_type=F32)

    return pl.pallas_call(
        body, name=name, grid=(kb, n // tn, m // tmr),
        in_specs=[pl.BlockSpec((tmr, tk), lambda k, j, r: (r, a_cb * kb + k)),
                  pl.BlockSpec((tmr, tn), lambda k, j, r: (r, j))],
        out_specs=pl.BlockSpec((tk, tn), lambda k, j, r: (k, j)),
        out_shape=jax.ShapeDtypeStruct((ka, n), F32),
        compiler_params=_cparams(("parallel", "parallel", "arbitrary"), vmem),
    )(a, b)


def _ln_stats(r):
    mu = jnp.mean(r, axis=-1, keepdims=True)
    xc = r - mu
    var = jnp.mean(xc * xc, axis=-1, keepdims=True)
    rstd = lax.rsqrt(var + LN_EPS)
    return xc * rstd, rstd


def _ln_fwd(xa, mix, g, b, *, name):
    m, n = xa.shape

    def body(xa_ref, mix_ref, g_ref, b_ref, r_ref, y_ref):
        r = ALPHA * xa_ref[...] + mix_ref[...]
        xhat, _ = _ln_stats(r)
        r_ref[...] = r
        y_ref[...] = xhat * g_ref[...] + b_ref[...]

    row = pl.BlockSpec((TM, n), lambda i: (i, 0))
    vec = pl.BlockSpec((1, n), lambda i: (0, 0))
    return pl.pallas_call(
        body, name=name, grid=(m // TM,), in_specs=[row, row, vec, vec], out_specs=[row, row],
        out_shape=[jax.ShapeDtypeStruct((m, n), F32)] * 2,
        compiler_params=_cparams(("parallel",)),
    )(xa, mix, g, b)


def _ln_bwd(r, da, db, ca, g, *, name):
    m, n = r.shape

    def body(r_ref, da_ref, db_ref, g_ref, dr_ref, dg_ref, dbias_ref):
        @pl.when(pl.program_id(0) == 0)
        def _():
            dg_ref[...] = jnp.zeros_like(dg_ref)
            dbias_ref[...] = jnp.zeros_like(dbias_ref)

        dy = ca * da_ref[...] + db_ref[...]
        xhat, rstd = _ln_stats(r_ref[...])
        dg_ref[...] += jnp.sum(dy * xhat, axis=0, keepdims=True)
        dbias_ref[...] += jnp.sum(dy, axis=0, keepdims=True)
        dxh = dy * g_ref[...]
        m1 = jnp.mean(dxh, axis=-1, keepdims=True)
        m2 = jnp.mean(dxh * xhat, axis=-1, keepdims=True)
        dr_ref[...] = rstd * (dxh - m1 - xhat * m2)

    row = pl.BlockSpec((TM, n), lambda i: (i, 0))
    vec = pl.BlockSpec((1, n), lambda i: (0, 0))
    return pl.pallas_call(
        body, name=name, grid=(m // TM,), in_specs=[row, row, row, vec], out_specs=[row, vec, vec],
        out_shape=[jax.ShapeDtypeStruct((m, n), F32), jax.ShapeDtypeStruct((1, n), F32), jax.ShapeDtypeStruct((1, n), F32)],
        compiler_params=_cparams(("arbitrary",)),
    )(r, da, db, g)


def _ffn_up(x1, wup, *, name):
    m = x1.shape[0]
    tn = D_FF // 2

    def body(x_ref, wg_ref, wu_ref, a_ref, g_ref, u_ref):
        x = x_ref[...].astype(BF16)
        g = jnp.dot(x, wg_ref[...], preferred_element_type=F32)
        u = jnp.dot(x, wu_ref[...], preferred_element_type=F32)
        a_ref[...] = (g * _sigmoid(g) * u).astype(BF16)
        g_ref[...] = g.astype(BF16)
        u_ref[...] = u.astype(BF16)

    out = pl.BlockSpec((TM, tn), lambda j, i: (i, j))
    return pl.pallas_call(
        body, name=name, grid=(2, m // TM),
        in_specs=[pl.BlockSpec((TM, D_MODEL), lambda j, i: (i, 0)),
                  pl.BlockSpec((D_MODEL, tn), lambda j, i: (0, j)),
                  pl.BlockSpec((D_MODEL, tn), lambda j, i: (0, j + 2))],
        out_specs=[out, out, out], out_shape=[jax.ShapeDtypeStruct((m, D_FF), BF16)] * 3,
        compiler_params=_cparams(("parallel", "parallel")),
    )(x1, wup, wup)


def _ffn_down_bwd(df, wdown, g, u, *, name):
    m = df.shape[0]
    tm = 256

    def body(df_ref, w_ref, g_ref, u_ref, o_ref):
        da = lax.dot_general(df_ref[...].astype(BF16), w_ref[...], (((1,), (1,)), ((), ())), preferred_element_type=F32)
        gg = g_ref[...].astype(F32)
        sg = _sigmoid(gg)
        o_ref[:, :D_FF] = (da * u_ref[...].astype(F32) * (sg * (1.0 + gg * (1.0 - sg)))).astype(BF16)
        o_ref[:, D_FF:] = (da * (gg * sg)).astype(BF16)

    return pl.pallas_call(
        body, name=name, grid=(m // tm,),
        in_specs=[pl.BlockSpec((tm, D_MODEL), lambda i: (i, 0)), pl.BlockSpec((D_FF, D_MODEL), lambda i: (0, 0)),
                  pl.BlockSpec((tm, D_FF), lambda i: (i, 0)), pl.BlockSpec((tm, D_FF), lambda i: (i, 0))],
        out_specs=pl.BlockSpec((tm, 2 * D_FF), lambda i: (i, 0)),
        out_shape=jax.ShapeDtypeStruct((m, 2 * D_FF), BF16),
        compiler_params=_cparams(("parallel",), VMEM_BIG),
    )(df, wdown, g, u)


def _ple_fwd(x2, zg, pp, bg, *, name):
    m, n = x2.shape

    def body(x_ref, zg_ref, pp_ref, bg_ref, x3_ref, gate_ref, ppb_ref):
        gate = _sigmoid(zg_ref[...] + bg_ref[...])
        pp = pp_ref[...]
        x3_ref[...] = x_ref[...] + gate * pp
        gate_ref[...] = gate.astype(BF16)
        ppb_ref[...] = pp.astype(BF16)

    row = pl.BlockSpec((TM, n), lambda i: (i, 0))
    return pl.pallas_call(
        body, name=name, grid=(m // TM,), in_specs=[row, row, row, pl.BlockSpec((1, n), lambda i: (0, 0))],
        out_specs=[row, row, row],
        out_shape=[jax.ShapeDtypeStruct((m, n), F32), jax.ShapeDtypeStruct((m, n), BF16), jax.ShapeDtypeStruct((m, n), BF16)],
        compiler_params=_cparams(("parallel",)),
    )(x2, zg, pp, bg)


def _ple_bwd(da, db, gate, pp, *, name):
    m, n = da.shape
    two = db is not None

    def body(*refs):
        if two:
            da_ref, db_ref, gate_ref, pp_ref, dx_ref, dz_ref, dpp_ref, dbg_ref = refs
            dx = ALPHA * da_ref[...] + db_ref[...]
        else:
            da_ref, gate_ref, pp_ref, dx_ref, dz_ref, dpp_ref, dbg_ref = refs
            dx = da_ref[...]

        @pl.when(pl.program_id(0) == 0)
        def _():
            dbg_ref[...] = jnp.zeros_like(dbg_ref)

        gate = gate_ref[...].astype(F32)
        dz = dx * pp_ref[...].astype(F32) * gate * (1.0 - gate)
        dx_ref[...] = dx
        dz_ref[...] = dz.astype(BF16)
        dpp_ref[...] = (dx * gate).astype(BF16)
        dbg_ref[...] += jnp.sum(dz, axis=0, keepdims=True)

    row = pl.BlockSpec((TM, n), lambda i: (i, 0))
    vec = pl.BlockSpec((1, n), lambda i: (0, 0))
    ins = [da, db, gate, pp] if two else [da, gate, pp]
    return pl.pallas_call(
        body, name=name, grid=(m // TM,), in_specs=[row] * len(ins), out_specs=[row, row, row, vec],
        out_shape=[jax.ShapeDtypeStruct((m, n), F32), jax.ShapeDtypeStruct((m, n), BF16), jax.ShapeDtypeStruct((m, n), BF16),
                   jax.ShapeDtypeStruct((1, n), F32)],
        compiler_params=_cparams(("arbitrary",)),
    )(*ins)


def _loss_head(y, target, *, name):
    m, n = y.shape

    def body(y_ref, t_ref, loss_ref, dy_ref):
        @pl.when(pl.program_id(0) == 0)
        def _():
            loss_ref[...] = jnp.zeros_like(loss_ref)

        err = y_ref[...] - t_ref[...]
        dy_ref[...] = err * (1.0 / n)
        per_tok = jnp.mean(err * err, axis=-1, keepdims=True)
        loss_ref[...] += 0.5 * jnp.sum(per_tok, axis=0, keepdims=True)

    row = pl.BlockSpec((TM, n), lambda i: (i, 0))
    return pl.pallas_call(
        body, name=name, grid=(m // TM,), in_specs=[row, row],
        out_specs=[pl.BlockSpec((1, 1), lambda i: (0, 0)), row],
        out_shape=[jax.ShapeDtypeStruct((1, 1), F32), jax.ShapeDtypeStruct((m, n), F32)],
        compiler_params=_cparams(("arbitrary",)),
    )(y, target)


def _gelu(y):
    c = math.sqrt(2.0 / math.pi)
    return 0.5 * y * (1.0 + jnp.tanh(c * (y + 0.044715 * y * y * y)))


def _gelu_grad(y):
    c = math.sqrt(2.0 / math.pi)
    t = jnp.tanh(c * (y + 0.044715 * y * y * y))
    return 0.5 * (1.0 + t) + 0.5 * y * (1.0 - t * t) * c * (1.0 + 3.0 * 0.044715 * y * y)


def _shift_rows(x, s, row, down):
    t = x.shape[0]
    if s % 8 == 0:
        z = jnp.zeros((s, x.shape[1]), x.dtype)
        return jnp.concatenate([z, x[:t - s]], axis=0) if down else jnp.concatenate([x[s:], z], axis=0)
    if down:
        return jnp.where(row >= s, pltpu.roll(x, s, 0), 0.0)
    return jnp.where(row < t - s, pltpu.roll(x, t - s, 0), 0.0)


def _scan_block(xr, xi, pwr_ref, pwi_ref, off, row, down, conj):
    for k in range(S5_LEVELS):
        s = 1 << k
        pr = pwr_ref[k:k + 1, pl.ds(off, 128)]
        pi_ = pwi_ref[k:k + 1, pl.ds(off, 128)]
        if conj:
            pi_ = -pi_
        sr = _shift_rows(xr, s, row, down)
        si = _shift_rows(xi, s, row, down)
        xr, xi = xr + pr * sr - pi_ * si, xi + pr * si + pi_ * sr
    return xr, xi


def _s5_fwd(h, bmat, cmat, dvec, wglu, bglu, lam, pwr, pwi, *, name):
    m = h.shape[0]
    t = T_S5
    nb = m // t

    def body(u_ref, bmat_ref, cmat_ref, d_ref, wglu_ref, bglu_ref, lam_ref, pwr_ref, pwi_ref,
             out_ref, y_ref, hb_ref, bu_ref, carry_ref):
        @pl.when(pl.program_id(0) == 0)
        def _():
            carry_ref[...] = jnp.zeros_like(carry_ref)

        hb_ref[0] = carry_ref[...]
        u = u_ref[...]
        bu_ref[...] = jnp.dot(u.astype(BF16), bmat_ref[...], preferred_element_type=F32)
        row = lax.broadcasted_iota(jnp.int32, (t, 128), 0)

        def strip(j, c):
            off = pl.multiple_of(j * 128, 128)
            offi = pl.multiple_of(S5_N + j * 128, 128)
            lr = lam_ref[0:1, pl.ds(off, 128)]
            li = lam_ref[1:2, pl.ds(off, 128)]
            hr = carry_ref[0:1, pl.ds(off, 128)]
            hi = carry_ref[0:1, pl.ds(offi, 128)]
            xr = bu_ref[:, pl.ds(off, 128)] + jnp.where(row == 0, lr * hr - li * hi, 0.0)
            xi = bu_ref[:, pl.ds(offi, 128)] + jnp.where(row == 0, lr * hi + li * hr, 0.0)
            xr, xi = _scan_block(xr, xi, pwr_ref, pwi_ref, off, row, True, False)
            bu_ref[:, pl.ds(off, 128)] = xr
            bu_ref[:, pl.ds(offi, 128)] = xi
            carry_ref[0:1, pl.ds(off, 128)] = xr[t - 1:t, :]
            carry_ref[0:1, pl.ds(offi, 128)] = xi[t - 1:t, :]
            return c

        lax.fori_loop(0, S5_N // 128, strip, 0)
        y = jnp.dot(bu_ref[...].astype(BF16), cmat_ref[...], preferred_element_type=F32) + d_ref[...] * u
        y_ref[...] = y
        g = _gelu(y)
        zz = jnp.dot(g.astype(BF16), wglu_ref[...], preferred_element_type=F32) + bglu_ref[...]
        out_ref[...] = (g * _sigmoid(zz)).astype(BF16)

    const = lambda shape: pl.BlockSpec(shape, lambda i: (0,) * len(shape))
    row_spec = pl.BlockSpec((t, MIX), lambda i: (i, 0))
    return pl.pallas_call(
        body, name=name, grid=(nb,),
        in_specs=[row_spec, const((MIX, 2 * S5_N)), const((2 * S5_N, MIX)), const((1, MIX)), const((MIX, MIX)),
                  const((1, MIX)), const((2, S5_N)), const((S5_LEVELS, S5_N)), const((S5_LEVELS, S5_N))],
        out_specs=[row_spec, row_spec, pl.BlockSpec((1, 1, 2 * S5_N), lambda i: (i, 0, 0))],
        out_shape=[jax.ShapeDtypeStruct((m, MIX), BF16), jax.ShapeDtypeStruct((m, MIX), F32),
                   jax.ShapeDtypeStruct((nb, 1, 2 * S5_N), F32)],
        scratch_shapes=[pltpu.VMEM((t, 2 * S5_N), F32), pltpu.VMEM((1, 2 * S5_N), F32)],
        compiler_params=_cparams(("arbitrary",), VMEM_BIG),
    )(h, bmat, cmat, dvec, wglu, bglu, lam, pwr, pwi)


def _s5_bwd(dcat, ypre, h, hb, bmat, cmat, dvec, wglu, bglu, lam, pwr, pwi, *, name):
    m = h.shape[0]
    t = T_S5
    nb = m // t

    def body(dya_ref, y_ref, u_ref, hb_ref, bmat_ref, cmat_ref, d_ref, wglu_ref, bglu_ref, lam_ref, pwr_ref, pwi_ref,
             du_ref, xb_ref, gb_ref, gq_ref, dzz_ref, dyq_ref, dlam_ref, dbglu_ref, dd_ref,
             bu_ref, dx_ref, gcarry_ref):
        @pl.when(pl.program_id(0) == 0)
        def _():
            gcarry_ref[...] = jnp.zeros_like(gcarry_ref)
            dlam_ref[...] = jnp.zeros_like(dlam_ref)
            dbglu_ref[...] = jnp.zeros_like(dbglu_ref)
            dd_ref[...] = jnp.zeros_like(dd_ref)

        u = u_ref[...]
        y = y_ref[...]
        g = _gelu(y)
        gq = g.astype(BF16)
        sg = _sigmoid(jnp.dot(gq, wglu_ref[...], preferred_element_type=F32) + bglu_ref[...])
        dout = dya_ref[...]
        dzz = dout * g * sg * (1.0 - sg)
        dzzq = dzz.astype(BF16)
        dg = dout * sg + lax.dot_general(dzzq, wglu_ref[...], (((1,), (1,)), ((), ())), preferred_element_type=F32)
        dy = dg * _gelu_grad(y)
        dyq = dy.astype(BF16)
        gq_ref[...] = gq
        dzz_ref[...] = dzzq
        dyq_ref[...] = dyq
        dbglu_ref[...] += jnp.sum(dzz, axis=0, keepdims=True)
        dd_ref[...] += jnp.sum(dy * u, axis=0, keepdims=True)

        dx_ref[...] = lax.dot_general(dyq, cmat_ref[...], (((1,), (1,)), ((), ())), preferred_element_type=F32)
        bu_ref[...] = jnp.dot(u.astype(BF16), bmat_ref[...], preferred_element_type=F32)
        row = lax.broadcasted_iota(jnp.int32, (t, 128), 0)

        def strip(j, c):
            off = pl.multiple_of(j * 128, 128)
            offi = pl.multiple_of(S5_N + j * 128, 128)
            lr = lam_ref[0:1, pl.ds(off, 128)]
            li = lam_ref[1:2, pl.ds(off, 128)]
            hr = hb_ref[0, 0:1, pl.ds(off, 128)]
            hi = hb_ref[0, 0:1, pl.ds(offi, 128)]
            xr = bu_ref[:, pl.ds(off, 128)] + jnp.where(row == 0, lr * hr - li * hi, 0.0)
            xi = bu_ref[:, pl.ds(offi, 128)] + jnp.where(row == 0, lr * hi + li * hr, 0.0)
            xr, xi = _scan_block(xr, xi, pwr_ref, pwi_ref, off, row, True, False)
            xb_ref[:, pl.ds(off, 128)] = xr.astype(BF16)
            xb_ref[:, pl.ds(offi, 128)] = xi.astype(BF16)
            pr_ = jnp.where(row == 0, hr, pltpu.roll(xr, 1, 0))
            pi_ = jnp.where(row == 0, hi, pltpu.roll(xi, 1, 0))

            cr = gcarry_ref[0:1, pl.ds(off, 128)]
            ci = gcarry_ref[0:1, pl.ds(offi, 128)]
            gr = dx_ref[:, pl.ds(off, 128)] + jnp.where(row == t - 1, lr * cr + li * ci, 0.0)
            gi = dx_ref[:, pl.ds(offi, 128)] + jnp.where(row == t - 1, lr * ci - li * cr, 0.0)
            gr, gi = _scan_block(gr, gi, pwr_ref, pwi_ref, off, row, False, True)
            gb_ref[:, pl.ds(off, 128)] = gr.astype(BF16)
            gb_ref[:, pl.ds(offi, 128)] = gi.astype(BF16)
            gcarry_ref[0:1, pl.ds(off, 128)] = gr[0:1, :]
            gcarry_ref[0:1, pl.ds(offi, 128)] = gi[0:1, :]
            dlam_ref[0:1, pl.ds(off, 128)] += jnp.sum(pr_ * gr + pi_ * gi, axis=0, keepdims=True)
            dlam_ref[1:2, pl.ds(off, 128)] += jnp.sum(pr_ * gi - pi_ * gr, axis=0, keepdims=True)
            return c

        lax.fori_loop(0, S5_N // 128, strip, 0)
        du_ref[...] = dy * d_ref[...] + lax.dot_general(gb_ref[...], bmat_ref[...], (((1,), (1,)), ((), ())),
                                                        preferred_element_type=F32)

    const = lambda shape: pl.BlockSpec(shape, lambda i: (0,) * len(shape))
    rev = lambda i: (nb - 1 - i, 0)
    row_spec = pl.BlockSpec((t, MIX), rev)
    wide = pl.BlockSpec((t, 2 * S5_N), rev)
    return pl.pallas_call(
        body, name=name, grid=(nb,),
        in_specs=[row_spec, row_spec, row_spec, pl.BlockSpec((1, 1, 2 * S5_N), lambda i: (nb - 1 - i, 0, 0)),
                  const((MIX, 2 * S5_N)), const((2 * S5_N, MIX)), const((1, MIX)), const((MIX, MIX)), const((1, MIX)),
                  const((2, S5_N)), const((S5_LEVELS, S5_N)), const((S5_LEVELS, S5_N))],
        out_specs=[row_spec, wide, wide, row_spec, row_spec, row_spec, const((2, S5_N)), const((1, MIX)), const((1, MIX))],
        out_shape=[jax.ShapeDtypeStruct((m, MIX), F32), jax.ShapeDtypeStruct((m, 2 * S5_N), BF16),
                   jax.ShapeDtypeStruct((m, 2 * S5_N), BF16), jax.ShapeDtypeStruct((m, MIX), BF16),
                   jax.ShapeDtypeStruct((m, MIX), BF16), jax.ShapeDtypeStruct((m, MIX), BF16),
                   jax.ShapeDtypeStruct((2, S5_N), F32), jax.ShapeDtypeStruct((1, MIX), F32), jax.ShapeDtypeStruct((1, MIX), F32)],
        scratch_shapes=[pltpu.VMEM((t, 2 * S5_N), F32), pltpu.VMEM((t, 2 * S5_N), F32), pltpu.VMEM((1, 2 * S5_N), F32)],
        compiler_params=_cparams(("arbitrary",), VMEM_BIG),
    )(dcat, ypre, h, hb, bmat, cmat, dvec, wglu, bglu, lam, pwr, pwi)


HALO = 8


def _taps_down(zext, t):
    return pltpu.roll(zext, 1, 0)[HALO:HALO + t], pltpu.roll(zext, 2, 0)[HALO:HALO + t]


def _conv_z(c_ref, x_ref, cp_ref, xp_ref, first, t):
    z = c_ref[...] * x_ref[...]
    zp = jnp.where(first, 0.0, cp_ref[t - HALO:t, :] * xp_ref[t - HALO:t, :])
    z1, z2 = _taps_down(jnp.concatenate([zp, z], axis=0), t)
    return z, z1, z2


def _conv_fwd(h, cw, *, name):
    m = h.shape[0]
    t = TM
    nb = m // t

    def body(b_ref, c_ref, x_ref, cp_ref, xp_ref, w_ref, o_ref):
        z, z1, z2 = _conv_z(c_ref, x_ref, cp_ref, xp_ref, pl.program_id(0) == 0, t)
        o_ref[...] = (b_ref[...] * (w_ref[0:1, :] * z2 + w_ref[1:2, :] * z1 + w_ref[2:3, :] * z)).astype(BF16)

    cur = lambda cb: pl.BlockSpec((t, MIX), lambda i: (i, cb))
    prev = lambda cb: pl.BlockSpec((t, MIX), lambda i: (jnp.maximum(i - 1, 0), cb))
    return pl.pallas_call(
        body, name=name, grid=(nb,),
        in_specs=[cur(1), cur(2), cur(3), prev(2), prev(3), pl.BlockSpec((3, MIX), lambda i: (0, 0))],
        out_specs=pl.BlockSpec((t, MIX), lambda i: (i, 0)),
        out_shape=jax.ShapeDtypeStruct((m, MIX), BF16),
        compiler_params=_cparams(("parallel",)),
    )(h, h, h, h, h, cw)


def _conv_bwd(dcat, h, cw, *, name):
    m = h.shape[0]
    t = TM
    nb = m // t

    def body(dy_ref, dyn_ref, b_ref, c_ref, x_ref, cp_ref, xp_ref, bn_ref, w_ref, o_ref, dw_ref):
        i = pl.program_id(0)

        @pl.when(i == 0)
        def _():
            dw_ref[...] = jnp.zeros_like(dw_ref)

        z, z1, z2 = _conv_z(c_ref, x_ref, cp_ref, xp_ref, i == 0, t)
        w0, w1, w2 = w_ref[0:1, :], w_ref[1:2, :], w_ref[2:3, :]
        dy = dy_ref[...]
        dconv = dy * b_ref[...]
        dnext = jnp.where(i == nb - 1, 0.0, dyn_ref[0:HALO, :] * bn_ref[0:HALO, :])
        dext = jnp.concatenate([dconv, dnext], axis=0)
        d1 = pltpu.roll(dext, t + HALO - 1, 0)[0:t]
        d2 = pltpu.roll(dext, t + HALO - 2, 0)[0:t]
        dz = w2 * dconv + w1 * d1 + w0 * d2
        o_ref[:, 0:MIX] = dy * (w0 * z2 + w1 * z1 + w2 * z)
        o_ref[:, MIX:2 * MIX] = dz * x_ref[...]
        o_ref[:, 2 * MIX:3 * MIX] = dz * c_ref[...]
        dw_ref[0:1, :] += jnp.sum(dconv * z2, axis=0, keepdims=True)
        dw_ref[1:2, :] += jnp.sum(dconv * z1, axis=0, keepdims=True)
        dw_ref[2:3, :] += jnp.sum(dconv * z, axis=0, keepdims=True)

    cur = lambda cb: pl.BlockSpec((t, MIX), lambda i: (i, cb))
    prev = lambda cb: pl.BlockSpec((t, MIX), lambda i: (jnp.maximum(i - 1, 0), cb))
    nxt = lambda cb: pl.BlockSpec((t, MIX), lambda i: (jnp.minimum(i + 1, nb - 1), cb))
    return pl.pallas_call(
        body, name=name, grid=(nb,),
        in_specs=[cur(1), nxt(1), cur(1), cur(2), cur(3), prev(2), prev(3), nxt(1), pl.BlockSpec((3, MIX), lambda i: (0, 0))],
        out_specs=[pl.BlockSpec((t, 3 * MIX), lambda i: (i, 0)), pl.BlockSpec((8, MIX), lambda i: (0, 0))],
        out_shape=[jax.ShapeDtypeStruct((m, 3 * MIX), F32), jax.ShapeDtypeStruct((8, MIX), F32)],
        compiler_params=_cparams(("arbitrary",)),
    )(dcat, dcat, h, h, h, h, h, h, cw)


PHALO = 16


def _pool_pooled(z_ref, zp_ref, i, t):
    z = z_ref[...]
    zp = jnp.where(i == 0, 0.0, zp_ref[t - PHALO:t, :])
    s = jnp.concatenate([zp, z], axis=0)
    sums = {}
    width = 1
    while width < PHALO:
        s = s + pltpu.roll(s, width, 0)
        width *= 2
        sums[width] = s[PHALO:PHALO + t]
    tpos = i * t + lax.broadcasted_iota(jnp.int32, (t, 1), 0)
    outs = []
    for gi, w in enumerate(POOL_WINDOWS):
        lo = gi * POOL_GROUP
        count = jnp.minimum(tpos + 1, w).astype(F32)
        outs.append(sums[w][:, lo:lo + POOL_GROUP] / count - z[:, lo:lo + POOL_GROUP])
    return outs


def _pool_fwd(h, pw, ps, *, name):
    m = h.shape[0]
    t = TM
    nb = m // t

    def body(z_ref, zp_ref, pw_ref, ps_ref, o_ref):
        pooled = _pool_pooled(z_ref, zp_ref, pl.program_id(0), t)
        for gi in range(len(POOL_WINDOWS)):
            lo = gi * POOL_GROUP
            mixed = jnp.dot(pooled[gi].astype(BF16), pw_ref[gi], preferred_element_type=F32)
            o_ref[:, lo:lo + POOL_GROUP] = (mixed * ps_ref[:, lo:lo + POOL_GROUP]).astype(BF16)

    return pl.pallas_call(
        body, name=name, grid=(nb,),
        in_specs=[pl.BlockSpec((t, MIX), lambda i: (i, 3)), pl.BlockSpec((t, MIX), lambda i: (jnp.maximum(i - 1, 0), 3)),
                  pl.BlockSpec((4, POOL_GROUP, POOL_GROUP), lambda i: (0, 0, 0)), pl.BlockSpec((1, MIX), lambda i: (0, 0))],
        out_specs=pl.BlockSpec((t, MIX), lambda i: (i, 0)),
        out_shape=jax.ShapeDtypeStruct((m, MIX), BF16),
        compiler_params=_cparams(("parallel",)),
    )(h, h, pw, ps)


def _pool_bwd(dcat, h, pw, ps, *, name):
    m = h.shape[0]
    t = TM
    nb = m // t

    def body(dy_ref, dyn_ref, z_ref, zp_ref, pw_ref, ps_ref, dz_ref, dpw_ref, dps_ref):
        i = pl.program_id(0)

        @pl.when(i == 0)
        def _():
            dpw_ref[...] = jnp.zeros_like(dpw_ref)
            dps_ref[...] = jnp.zeros_like(dps_ref)

        pooled = _pool_pooled(z_ref, zp_ref, i, t)
        dy = dy_ref[...]
        tpos = i * t + lax.broadcasted_iota(jnp.int32, (t, 1), 0)
        for gi, w in enumerate(POOL_WINDOWS):
            lo = gi * POOL_GROUP
            sl = slice(lo, lo + POOL_GROUP)
            pq = pooled[gi].astype(BF16)
            mixed = jnp.dot(pq, pw_ref[gi], preferred_element_type=F32)
            dps_ref[:, sl] += jnp.sum(dy[:, sl] * mixed, axis=0, keepdims=True)
            dmix = (dy[:, sl] * ps_ref[:, sl]).astype(BF16)
            dpw_ref[gi] += lax.dot_general(pq, dmix, (((0,), (0,)), ((), ())), preferred_element_type=F32)
            dpool = lax.dot_general(dmix, pw_ref[gi], (((1,), (1,)), ((), ())), preferred_element_type=F32)
            dmix_n = (dyn_ref[0:PHALO, sl] * ps_ref[:, sl]).astype(BF16)
            dpool_n = lax.dot_general(dmix_n, pw_ref[gi], (((1,), (1,)), ((), ())), preferred_element_type=F32)
            e = dpool / jnp.minimum(tpos + 1, w).astype(F32)
            e_n = jnp.where(i == nb - 1, 0.0, dpool_n * (1.0 / w))
            f = jnp.concatenate([e, e_n], axis=0)
            width = 1
            while width < w:
                f = f + pltpu.roll(f, t + PHALO - width, 0)
                width *= 2
            dz_ref[:, sl] = f[0:t] - dpool

    return pl.pallas_call(
        body, name=name, grid=(nb,),
        in_specs=[pl.BlockSpec((t, MIX), lambda i: (i, 1)), pl.BlockSpec((t, MIX), lambda i: (jnp.minimum(i + 1, nb - 1), 1)),
                  pl.BlockSpec((t, MIX), lambda i: (i, 3)), pl.BlockSpec((t, MIX), lambda i: (jnp.maximum(i - 1, 0), 3)),
                  pl.BlockSpec((4, POOL_GROUP, POOL_GROUP), lambda i: (0, 0, 0)), pl.BlockSpec((1, MIX), lambda i: (0, 0))],
        out_specs=[pl.BlockSpec((t, MIX), lambda i: (i, 0)), pl.BlockSpec((4, POOL_GROUP, POOL_GROUP), lambda i: (0, 0, 0)),
                   pl.BlockSpec((1, MIX), lambda i: (0, 0))],
        out_shape=[jax.ShapeDtypeStruct((m, MIX), F32), jax.ShapeDtypeStruct((4, POOL_GROUP, POOL_GROUP), F32),
                   jax.ShapeDtypeStruct((1, MIX), F32)],
        compiler_params=_cparams(("arbitrary",)),
    )(dcat, dcat, h, h, pw, ps)


NKEY = 2 * T_ATT


def _band_mask():
    qc = np.arange(T_ATT)[:, None] // CHUNK
    kc = np.arange(NKEY)[None, :] // CHUNK - LEFT_CHUNKS
    return np.where((kc <= qc) & (kc >= qc - LEFT_CHUNKS), 0.0, NEG_INF).astype(np.float32)


def _diag_index():
    c = np.arange(NKEY)
    d = np.where(c <= NKEY // 2 + CHUNK, T_ATT - c, T_ATT + NKEY - c)
    return np.clip(d, -MAX_REL, MAX_REL) + MAX_REL


def _bias_tile(vd_ref, mask_ref, tile_ref):
    for hh in range(2):
        v = vd_ref[0, hh:hh + 1, :]
        base = jnp.concatenate([v if s == 0 else pltpu.roll(v, s, 1) for s in range(8)], axis=0)
        for mrow in range(T_ATT // 8):
            rows = slice(8 * mrow, 8 * mrow + 8)
            blk = base if mrow == 0 else pltpu.roll(base, 8 * mrow, 1)
            tile_ref[hh, rows, :] = blk + mask_ref[rows, :]


def _attn_probs(q, kc, tile, first):
    s = lax.dot_general(q, kc, (((1,), (1,)), ((), ())), preferred_element_type=F32) + tile
    col = lax.broadcasted_iota(jnp.int32, s.shape, 1)
    s = jnp.where(jnp.logical_and(first, col < T_ATT), NEG_INF, s)
    p = jnp.exp(s - jnp.max(s, axis=-1, keepdims=True))
    return p * (1.0 / jnp.sum(p, axis=-1, keepdims=True))


def _attn_specs(nb):
    cur = lambda base: pl.BlockSpec((T_ATT, 128), lambda hp, i: (i, base + hp))
    prev = lambda base: pl.BlockSpec((T_ATT, 128), lambda hp, i: (jnp.maximum(i - 1, 0), base + hp))
    return [cur(0), cur(4), prev(4), cur(8), prev(8),
            pl.BlockSpec((1, 2, NKEY), lambda hp, i: (hp, 0, 0)), pl.BlockSpec((T_ATT, NKEY), lambda hp, i: (0, 0))]


def _attn_fwd(h, vdiag, mask, *, name):
    m = h.shape[0]
    nb = m // T_ATT

    def body(q_ref, k_ref, kp_ref, v_ref, vp_ref, vd_ref, mask_ref, o_ref, tile_ref):
        i = pl.program_id(1)

        @pl.when(i == 0)
        def _():
            _bias_tile(vd_ref, mask_ref, tile_ref)

        outs = []
        for hh in range(2):
            sl = slice(hh * HEAD_DIM, (hh + 1) * HEAD_DIM)
            q = (q_ref[:, sl] * (HEAD_DIM ** -0.5)).astype(BF16)
            kc = jnp.concatenate([kp_ref[:, sl], k_ref[:, sl]], axis=0).astype(BF16)
            vc = jnp.concatenate([vp_ref[:, sl], v_ref[:, sl]], axis=0).astype(BF16)
            p = _attn_probs(q, kc, tile_ref[hh], i == 0)
            outs.append(jnp.dot(p.astype(BF16), vc, preferred_element_type=F32))
        o_ref[...] = jnp.concatenate(outs, axis=1).astype(BF16)

    return pl.pallas_call(
        body, name=name, grid=(ATT_HEADS // 2, nb), in_specs=_attn_specs(nb),
        out_specs=pl.BlockSpec((T_ATT, 128), lambda hp, i: (i, hp)),
        out_shape=jax.ShapeDtypeStruct((m, MIX), BF16),
        scratch_shapes=[pltpu.VMEM((2, T_ATT, NKEY), F32)],
        compiler_params=_cparams(("parallel", "arbitrary"), VMEM_BIG),
    )(h, h, h, h, h, vdiag, mask)


def _attn_bwd(dcat, h, vdiag, mask, *, name):
    m = h.shape[0]
    nb = m // T_ATT

    def body(do_ref, q_ref, k_ref, kp_ref, v_ref, vp_ref, vd_ref, mask_ref,
             dq_ref, dk_ref, dkp_ref, dv_ref, dvp_ref, dvd_ref, tile_ref, acc_ref):
        i = pl.program_id(1)

        @pl.when(i == 0)
        def _():
            _bias_tile(vd_ref, mask_ref, tile_ref)
            acc_ref[...] = jnp.zeros_like(acc_ref)

        scale = HEAD_DIM ** -0.5
        dqs, dks, dkps, dvs, dvps = [], [], [], [], []
        for hh in range(2):
            sl = slice(hh * HEAD_DIM, (hh + 1) * HEAD_DIM)
            q = (q_ref[:, sl] * scale).astype(BF16)
            kc = jnp.concatenate([kp_ref[:, sl], k_ref[:, sl]], axis=0).astype(BF16)
            vc = jnp.concatenate([vp_ref[:, sl], v_ref[:, sl]], axis=0).astype(BF16)
            do = do_ref[:, sl].astype(BF16)
            p = _attn_probs(q, kc, tile_ref[hh], i == 0)
            dv = lax.dot_general(p.astype(BF16), do, (((0,), (0,)), ((), ())), preferred_element_type=F32)
            dp = lax.dot_general(do, vc, (((1,), (1,)), ((), ())), preferred_element_type=F32)
            ds = p * (dp - jnp.sum(dp * p, axis=-1, keepdims=True))
            acc_ref[hh] += ds
            dsq = ds.astype(BF16)
            dqs.append(jnp.dot(dsq, kc, preferred_element_type=F32) * scale)
            dk = lax.dot_general(dsq, q, (((0,), (0,)), ((), ())), preferred_element_type=F32)
            dkps.append(dk[:T_ATT])
            dks.append(dk[T_ATT:])
            dvps.append(dv[:T_ATT])
            dvs.append(dv[T_ATT:])
        dq_ref[...] = jnp.concatenate(dqs, axis=1)
        dk_ref[...] = jnp.concatenate(dks, axis=1)
        dkp_ref[...] = jnp.concatenate(dkps, axis=1)
        dv_ref[...] = jnp.concatenate(dvs, axis=1)
        dvp_ref[...] = jnp.concatenate(dvps, axis=1)

        @pl.when(i == nb - 1)
        def _():
            for hh in range(2):
                r8 = acc_ref[hh, 0:8, :]
                for mrow in range(1, T_ATT // 8):
                    r8 = r8 + pltpu.roll(acc_ref[hh, 8 * mrow:8 * mrow + 8, :], NKEY - 8 * mrow, 1)
                tot = r8[0:1, :]
                for s in range(1, 8):
                    tot = tot + pltpu.roll(r8[s:s + 1, :], NKEY - s, 1)
                dvd_ref[0, hh:hh + 1, :] = tot

    out = pl.BlockSpec((T_ATT, 128), lambda hp, i: (i, hp))
    return pl.pallas_call(
        body, name=name, grid=(ATT_HEADS // 2, nb),
        in_specs=[pl.BlockSpec((T_ATT, 128), lambda hp, i: (i, hp))] + _attn_specs(nb),
        out_specs=[out, out, out, out, out, pl.BlockSpec((1, 2, NKEY), lambda hp, i: (hp, 0, 0))],
        out_shape=[jax.ShapeDtypeStruct((m, MIX), F32)] * 5 + [jax.ShapeDtypeStruct((ATT_HEADS // 2, 2, NKEY), F32)],
        scratch_shapes=[pltpu.VMEM((2, T_ATT, NKEY), F32), pltpu.VMEM((2, T_ATT, NKEY), F32)],
        compiler_params=_cparams(("parallel", "arbitrary"), VMEM_BIG),
    )(dcat, h, h, h, h, h, vdiag, mask)


def _row_tile(rows):
    for t in (512, 256, 128, 64, 32, 16, 8):
        if rows % t == 0:
            return t
    return rows


def _add_n(arrs, coefs, *, name):
    rows, cols = arrs[0].shape
    t = _row_tile(rows)

    def body(*refs):
        acc = None
        for r, cf in zip(refs[:-1], coefs):
            v = r[...] if cf == 1.0 else cf * r[...]
            acc = v if acc is None else acc + v
        refs[-1][...] = acc

    spec = pl.BlockSpec((t, cols), lambda i: (i, 0))
    return pl.pallas_call(
        body, name=name, grid=(rows // t,), in_specs=[spec] * len(arrs), out_specs=spec,
        out_shape=jax.ShapeDtypeStruct((rows, cols), F32), compiler_params=_cparams(("parallel",)),
    )(*arrs)


def _adamw(w, g, mom, var, *, name):
    rows, cols = w.shape
    t = _row_tile(rows)

    def body(w_ref, g_ref, m_ref, v_ref, d_ref, mo_ref, vo_ref):
        g_ = g_ref[...]
        m_ = ADAM_B1 * m_ref[...] + (1.0 - ADAM_B1) * g_
        v_ = ADAM_B2 * v_ref[...] + (1.0 - ADAM_B2) * (g_ * g_)
        m_hat = m_ / (1.0 - ADAM_B1 ** ADAM_STEP)
        v_hat = v_ / (1.0 - ADAM_B2 ** ADAM_STEP)
        d_ref[...] = -ADAM_LR * (m_hat / (jnp.sqrt(v_hat) + ADAM_EPS) + ADAM_WD * w_ref[...])
        mo_ref[...] = m_
        vo_ref[...] = v_

    spec = pl.BlockSpec((t, cols), lambda i: (i, 0))
    return pl.pallas_call(
        body, name=name, grid=(rows // t,), in_specs=[spec] * 4, out_specs=[spec] * 3,
        out_shape=[jax.ShapeDtypeStruct((rows, cols), F32)] * 3, compiler_params=_cparams(("parallel",)),
    )(w, g, mom, var)


ANY = pl.BlockSpec(memory_space=pl.ANY)


def _place():
    x, y, c = lax.axis_index("x"), lax.axis_index("y"), lax.axis_index("c")
    chips = [(1 - x, y), (x, 1 - y), (1 - x, 1 - y)]
    return x, y, c, chips


def _gather_chips(ws, *, name):
    n = len(ws)

    def body(*refs):
        ins, outs = refs[:n], refs[n:2 * n]
        send_sems, recv_sems, local_sems = refs[2 * n:]
        x, y, c, chips = _place()
        me = 2 * x + y
        sibling = (x, y, 1 - c)

        def remote(k, j, chip_index, rows, to):
            region = outs[k].at[chip_index, rows]
            return pltpu.make_async_remote_copy(
                src_ref=region, dst_ref=region, send_sem=send_sems.at[6 * k + j], recv_sem=recv_sems.at[6 * k + j],
                device_id=to, device_id_type=MESH)

        local, sent = [], []
        for k in range(n):
            half = ins[k].shape[0] // 2
            mine = pl.ds(c * half, half)
            local.append(pltpu.make_async_copy(ins[k], outs[k].at[me], local_sems.at[k]))
            local[-1].start()
            for j, chip in enumerate(chips):
                sent.append(pltpu.make_async_remote_copy(
                    src_ref=ins[k].at[mine], dst_ref=outs[k].at[me, mine], send_sem=send_sems.at[6 * k + j],
                    recv_sem=recv_sems.at[6 * k + j], device_id=(*chip, c), device_id_type=MESH))
                sent[-1].start()
        for k in range(n):
            half = ins[k].shape[0] // 2
            mine = pl.ds(c * half, half)
            for j, chip in enumerate(chips):
                remote(k, j, 2 * chip[0] + chip[1], mine, (*chip, c)).wait_recv()
                sent.append(remote(k, 3 + j, 2 * chip[0] + chip[1], mine, sibling))
                sent[-1].start()
        for k in range(n):
            half = ins[k].shape[0] // 2
            theirs = pl.ds((1 - c) * half, half)
            for j, chip in enumerate(chips):
                remote(k, 3 + j, 2 * chip[0] + chip[1], theirs, sibling).wait_recv()
        for cp in sent:
            cp.wait_send()
        for cp in local:
            cp.wait()

    return pl.pallas_call(
        body, name=name, in_specs=[ANY] * n, out_specs=[ANY] * n,
        out_shape=[jax.ShapeDtypeStruct((N_CHIPS,) + w.shape, w.dtype) for w in ws],
        scratch_shapes=[pltpu.SemaphoreType.DMA((6 * n,)), pltpu.SemaphoreType.DMA((6 * n,)), pltpu.SemaphoreType.DMA((n,))],
    )(*ws)


def _swap_sibling(arrs, *, name):
    n = len(arrs)

    def body(*refs):
        ins, outs = refs[:n], refs[n:2 * n]
        send_sems, recv_sems = refs[2 * n:]
        x, y, c, _ = _place()
        cps = [pltpu.make_async_remote_copy(src_ref=ins[k], dst_ref=outs[k], send_sem=send_sems.at[k], recv_sem=recv_sems.at[k],
                                            device_id=(x, y, 1 - c), device_id_type=MESH) for k in range(n)]
        for cp in cps:
            cp.start()
        for cp in cps:
            cp.wait_recv()
        for cp in cps:
            cp.wait_send()

    return pl.pallas_call(
        body, name=name, in_specs=[ANY] * n, out_specs=[ANY] * n,
        out_shape=[jax.ShapeDtypeStruct(a.shape, a.dtype) for a in arrs],
        scratch_shapes=[pltpu.SemaphoreType.DMA((n,)), pltpu.SemaphoreType.DMA((n,))],
    )(*arrs)


def _scatter_chips(ps, *, name):
    n = len(ps)

    def body(*refs):
        ins, outs = refs[:n], refs[n:2 * n]
        send_sems, recv_sems = refs[2 * n:]
        x, y, c, chips = _place()
        cps = []
        for k in range(n):
            for j, chip in enumerate(chips):
                cps.append(pltpu.make_async_remote_copy(
                    src_ref=ins[k].at[2 * chip[0] + chip[1]], dst_ref=outs[k].at[j], send_sem=send_sems.at[3 * k + j],
                    recv_sem=recv_sems.at[3 * k + j], device_id=(*chip, c), device_id_type=MESH))
        for cp in cps:
            cp.start()
        for cp in cps:
            cp.wait_recv()
        for cp in cps:
            cp.wait_send()

    return pl.pallas_call(
        body, name=name, in_specs=[ANY] * n, out_specs=[ANY] * n,
        out_shape=[jax.ShapeDtypeStruct((3,) + p.shape[1:], p.dtype) for p in ps],
        scratch_shapes=[pltpu.SemaphoreType.DMA((3 * n,)), pltpu.SemaphoreType.DMA((3 * n,))],
    )(*ps)


def _all_reduce_small(buf, *, name):
    rows = buf.shape[0]

    def body(x_ref, sum_ref, all_ref, send_sems, recv_sems, local_sem):
        x, y, c, chips = _place()
        me, sibling = (x, y, c), (x, y, 1 - c)

        def slab(px, py, pc):
            return all_ref.at[pl.ds((4 * px + 2 * py + pc) * rows, rows), :]

        def copy(k, block, to, src=None):
            return pltpu.make_async_remote_copy(
                src_ref=slab(*block) if src is None else src, dst_ref=slab(*block), send_sem=send_sems.at[k],
                recv_sem=recv_sems.at[k], device_id=to, device_id_type=MESH)

        mine = pltpu.make_async_copy(x_ref, slab(*me), local_sem)
        mine.start()
        first = [copy(0, me, sibling, src=x_ref)]
        first += [copy(1 + j, me, (*chip, c), src=x_ref) for j, chip in enumerate(chips)]
        for cp in first:
            cp.start()
        passed = [copy(4 + j, (*chip, c), sibling) for j, chip in enumerate(chips)]
        for j, chip in enumerate(chips):
            copy(1 + j, (*chip, c), me).wait_recv()
            passed[j].start()
        copy(0, sibling, me).wait_recv()
        for j, chip in enumerate(chips):
            copy(4 + j, (*chip, 1 - c), me).wait_recv()
        for cp in first + passed:
            cp.wait_send()
        mine.wait()
        acc = all_ref[0:rows, :]
        for d in range(1, N_DEV):
            acc = acc + all_ref[d * rows:(d + 1) * rows, :]
        sum_ref[...] = acc

    vmem = pl.BlockSpec(memory_space=pltpu.VMEM)
    return pl.pallas_call(
        body, name=name, in_specs=[vmem], out_specs=[vmem, vmem],
        out_shape=[jax.ShapeDtypeStruct((rows, 128), F32), jax.ShapeDtypeStruct((N_DEV * rows, 128), F32)],
        scratch_shapes=[pltpu.SemaphoreType.DMA((7,)), pltpu.SemaphoreType.DMA((7,)), pltpu.SemaphoreType.DMA],
        compiler_params=pltpu.CompilerParams(vmem_limit_bytes=VMEM_BIG),
    )(buf)[0]


WEIGHTS = ['ev_w_in', 'ev_lambda_re', 'ev_lambda_im', 'ev_log_dt', 'ev_b_re', 'ev_b_im', 'ev_c_re', 'ev_c_im', 'ev_d',
           'ev_w_glu', 'ev_b_glu', 'ev_conv_w', 'ev_w_out', 'od_w_in', 'od_rel_bias', 'od_pool_w', 'od_pool_scale',
           'od_w_out', 'ln_mix_g', 'ln_mix_b', 'ln_ffn_g', 'ln_ffn_b', 'ffn_w_up', 'ffn_w_down', 'ple_w_proj',
           'ple_w_gate', 'ple_b_gate']
INPUTS = ['x', 'p'] + WEIGHTS + ['loss_target'] + ['m_' + n for n in WEIGHTS] + ['v_' + n for n in WEIGHTS]

BIG = {
    'ev_w_in': (2, (2, 1024, 2048)), 'ev_w_glu': (1, (2, 512, 512)), 'ev_w_out': (1, (2, 1024, 1024)),
    'od_w_in': (2, (2, 1024, 2048)), 'od_w_out': (1, (2, 1024, 1024)), 'ffn_w_up': (2, (4, 1024, 5632)),
    'ffn_w_down': (1, (4, 2816, 1024)), 'ple_w_proj': (2, (4, 256, 1024)), 'ple_w_gate': (1, (4, 1024, 1024)),
}
SMALL_SHARDED = {'ev_conv_w': (2, 3, 512), 'od_pool_scale': (2, 512)}
REPLICATED = [n for n in WEIGHTS if n not in BIG and n not in SMALL_SHARDED]


def _shard_rows(name):
    axis, (nl, k, n) = BIG[name]
    return (nl * k, n // N_CHIPS) if axis == 2 else (nl * k // N_CHIPS, n)


def _unstack(name, st):
    axis, (nl, k, n) = BIG[name]
    if axis == 2:
        return st.reshape(N_CHIPS, nl, k, n // N_CHIPS).transpose(1, 2, 0, 3).reshape(nl, k, n)
    return st.reshape(N_CHIPS, nl, k // N_CHIPS, n).transpose(1, 0, 2, 3).reshape(nl, k, n)


def _stack(name, full):
    axis, (nl, k, n) = BIG[name]
    rows, cols = _shard_rows(name)
    if axis == 2:
        return full.reshape(nl, k, N_CHIPS, n // N_CHIPS).transpose(2, 0, 1, 3).reshape(N_CHIPS, rows, cols)
    return full.reshape(nl, N_CHIPS, k // N_CHIPS, n).transpose(1, 0, 2, 3).reshape(N_CHIPS, rows, cols)


def _pack(arrs):
    flat = jnp.concatenate([a.reshape(-1) for a in arrs])
    total = flat.shape[0]
    padded = -(-total // 1024) * 1024
    return jnp.pad(flat, (0, padded - total)).reshape(padded // 128, 128)


def _unpack(buf, shapes):
    flat = buf.reshape(-1)
    out, pos = [], 0
    for s in shapes:
        size = int(np.prod(s))
        out.append(flat[pos:pos + size].reshape(s))
        pos += size
    return out


def _s5_params(lam_re, lam_im, log_dt, b_re, b_im, c_re, c_im):
    dt = jnp.exp(log_dt)[:, None]
    mag = jnp.exp(lam_re * dt)
    ang = lam_im * dt
    lb_re = mag * jnp.cos(ang)
    lb_im = mag * jnp.sin(ang)
    den = lam_re * lam_re + lam_im * lam_im
    nr = lb_re - 1.0
    ni = lb_im
    r_re = (nr * lam_re + ni * lam_im) / den
    r_im = (ni * lam_re - nr * lam_im) / den
    bb_re = r_re[..., None] * b_re - r_im[..., None] * b_im
    bb_im = r_re[..., None] * b_im + r_im[..., None] * b_re
    eye = jnp.eye(S5_GROUPS, dtype=F32)

    def block_diag(a):
        g, r, c = a.shape
        return (a[:, :, None, :] * eye[:, None, :, None]).reshape(g * r, g * c)

    bmat = jnp.concatenate([block_diag(bb_re.transpose(0, 2, 1)), block_diag(bb_im.transpose(0, 2, 1))], axis=1)
    cmat = jnp.concatenate([block_diag(c_re.transpose(0, 2, 1)), block_diag(-c_im.transpose(0, 2, 1))], axis=0)
    lam = jnp.stack([lb_re.reshape(S5_N), lb_im.reshape(S5_N)])
    return lam, bmat, cmat


def _lam_powers(lam):
    pr, pi_ = lam[0], lam[1]
    res, ims = [], []
    for _ in range(S5_LEVELS):
        res.append(pr)
        ims.append(pi_)
        pr, pi_ = pr * pr - pi_ * pi_, 2.0 * pr * pi_
    return jnp.stack(res), jnp.stack(ims)


def _local_step(x, p, target, w):
    mask = jnp.asarray(_band_mask())
    diag_idx = _diag_index()
    onehot = jnp.asarray(np.eye(2 * MAX_REL + 1, dtype=np.float32)[diag_idx])
    saved = []
    for i in range(DEPTH):
        li = i // 2
        s = {'x0': x}
        if i % 2 == 0:
            (lam, bmat, cmat), s5_vjp = jax.vjp(
                _s5_params, w['ev_lambda_re'][li], w['ev_lambda_im'][li], w['ev_log_dt'][li], w['ev_b_re'][li],
                w['ev_b_im'][li], w['ev_c_re'][li], w['ev_c_im'][li])
            pwr, pwi = _lam_powers(lam)
            s5c = (bmat.astype(BF16), cmat.astype(BF16), w['ev_d'][li].reshape(1, MIX), w['ev_w_glu'][li],
                   w['ev_b_glu'][li].reshape(1, MIX), lam, pwr, pwi)
            h = _mm([(x, 0, D_MODEL)], w['ev_w_in'][li], name=f"in_proj")
            ya, ypre, hb = _s5_fwd(h, *s5c, name=f"s5_fwd")
            yb = _conv_fwd(h, w['ev_conv_w'][li], name=f"conv_fwd")
            wout = w['ev_w_out'][li]
            s.update(s5_vjp=s5_vjp, s5c=s5c, ypre=ypre, hb=hb)
        else:
            vdiag = jnp.dot(w['od_rel_bias'][li], onehot.T, precision=HIGHEST).reshape(ATT_HEADS // 2, 2, NKEY)
            pw = w['od_pool_w'][li].astype(BF16)
            ps = w['od_pool_scale'][li].reshape(1, MIX)
            h = _mm([(x, 0, D_MODEL)], w['od_w_in'][li], name=f"in_proj")
            ya = _attn_fwd(h, vdiag, mask, name=f"attn_fwd")
            yb = _pool_fwd(h, pw, ps, name=f"pool_fwd")
            wout = w['od_w_out'][li]
            s.update(vdiag=vdiag, pw=pw, ps=ps)
        mix = _mm([(ya, 0, MIX), (yb, 0, MIX)], wout, name=f"out_proj")
        r1, x1 = _ln_fwd(x, mix, w['ln_mix_g'][i].reshape(1, -1), w['ln_mix_b'][i].reshape(1, -1), name=f"ln_mix")
        a, gg, uu = _ffn_up(x1, w['ffn_w_up'][i], name=f"ffn_up")
        f = _mm([(a, 0, D_FF)], w['ffn_w_down'][i], name=f"ffn_down")
        r2, x2 = _ln_fwd(x1, f, w['ln_ffn_g'][i].reshape(1, -1), w['ln_ffn_b'][i].reshape(1, -1), name=f"ln_ffn")
        zg = _mm([(x2, 0, D_MODEL)], w['ple_w_gate'][i], name=f"ple_gate")
        pp = _mm([(p[i], 0, D_PLE)], w['ple_w_proj'][i], name=f"ple_proj")
        x3, gate, ppb = _ple_fwd(x2, zg, pp, w['ple_b_gate'][i].reshape(1, -1), name=f"ple")
        s.update(h=h, ya=ya, yb=yb, wout=wout, r1=r1, x1=x1, a=a, gg=gg, uu=uu, r2=r2, x2=x2, gate=gate, ppb=ppb)
        saved.append(s)
        x = x3

    loss, da = _loss_head(x, target, name="loss_head")
    db = None
    grads = {n: [None] * (DEPTH if n.startswith(('ln_', 'ffn_', 'ple_')) else DEPTH // 2) for n in WEIGHTS}
    for i in reversed(range(DEPTH)):
        li = i // 2
        s = saved[i]
        dx3, dz, dpp, dbg = _ple_bwd(da, db, s['gate'], s['ppb'], name=f"ple_bwd")
        grads['ple_b_gate'][i] = dbg.reshape(-1)
        grads['ple_w_gate'][i] = _mm_tn(s['x2'], 0, D_MODEL, dz, name=f"d_ple_gate")
        grads['ple_w_proj'][i] = _mm_tn(p[i], 0, D_PLE, dpp, name=f"d_ple_proj")
        dx2 = _mm([(dz, 0, D_MODEL)], w['ple_w_gate'][i], trans_b=True, name=f"ple_gate_bwd")
        dr2, dg2, db2 = _ln_bwd(s['r2'], dx3, dx2, 1.0, w['ln_ffn_g'][i].reshape(1, -1), name=f"ln_ffn_bwd")
        grads['ln_ffn_g'][i] = dg2.reshape(-1)
        grads['ln_ffn_b'][i] = db2.reshape(-1)
        dhh = _ffn_down_bwd(dr2, w['ffn_w_down'][i], s['gg'], s['uu'], name=f"ffn_down_bwd")
        grads['ffn_w_down'][i] = _mm_tn(s['a'], 0, D_FF, dr2, tk=D_FF // 2, name=f"d_ffn_down")
        grads['ffn_w_up'][i] = _mm_tn(s['x1'], 0, D_MODEL, dhh, tn=D_FF // 2, name=f"d_ffn_up")
        dx1 = _mm([(dhh, 0, 2 * D_FF)], w['ffn_w_up'][i], trans_b=True, tm=256, vmem=VMEM_BIG, name=f"ffn_up_bwd")
        dr1, dg1, db1 = _ln_bwd(s['r1'], dr2, dx1, ALPHA, w['ln_mix_g'][i].reshape(1, -1), name=f"ln_mix_bwd")
        grads['ln_mix_g'][i] = dg1.reshape(-1)
        grads['ln_mix_b'][i] = db1.reshape(-1)
        dcat = _mm([(dr1, 0, D_MODEL)], s['wout'], trans_b=True, name=f"out_proj_bwd")
        dwout = jnp.concatenate([_mm_tn(s['ya'], 0, MIX, dr1, name=f"d_out_a"),
                                 _mm_tn(s['yb'], 0, MIX, dr1, name=f"d_out_b")], axis=0)
        h = s['h']
        if i % 2 == 0:
            s5c = s['s5c']
            du, xb, gb, gq, dzzq, dyq, dlam, dbglu, dd = _s5_bwd(dcat, s['ypre'], h, s['hb'], *s5c, name=f"s5_bwd")
            dbmat = _mm_tn(h, 0, MIX, gb, name=f"d_s5_b")
            dcmat = _mm_tn(xb, 0, 2 * S5_N, dyq, name=f"d_s5_c")
            s5g = s['s5_vjp']((dlam, dbmat, dcmat))
            for n, g_ in zip(['ev_lambda_re', 'ev_lambda_im', 'ev_log_dt', 'ev_b_re', 'ev_b_im', 'ev_c_re', 'ev_c_im'], s5g):
                grads[n][li] = g_
            grads['ev_w_glu'][li] = _mm_tn(gq, 0, MIX, dzzq, name=f"d_glu")
            grads['ev_b_glu'][li] = dbglu.reshape(-1)
            grads['ev_d'][li] = dd.reshape(-1)
            d3, dcw = _conv_bwd(dcat, h, w['ev_conv_w'][li], name=f"conv_bwd")
            grads['ev_conv_w'][li] = dcw[0:3]
            grads['ev_w_out'][li] = dwout
            grads['ev_w_in'][li] = jnp.concatenate(
                [_mm_tn(s['x0'], 0, D_MODEL, du, name=f"d_in_a"), _mm_tn(s['x0'], 0, D_MODEL, d3, tn=3 * MIX, name=f"d_in_b")],
                axis=1)
            db = _mm([(du, 0, MIX), (d3, 0, 3 * MIX)], w['ev_w_in'][li], trans_b=True, name=f"in_proj_bwd")
        else:
            dq, dk, dkp, dv, dvp, dvd = _attn_bwd(dcat, h, s['vdiag'], mask, name=f"attn_bwd")
            dzp, dpw, dps = _pool_bwd(dcat, h, s['pw'], s['ps'], name=f"pool_bwd")
            zeros = jnp.zeros((T_ATT, MIX), F32)
            dh = jnp.concatenate([dq, dk + jnp.concatenate([dkp[T_ATT:], zeros], axis=0),
                                  dv + jnp.concatenate([dvp[T_ATT:], zeros], axis=0), dzp], axis=1)
            grads['od_rel_bias'][li] = jnp.dot(dvd.reshape(ATT_HEADS, NKEY), onehot, precision=HIGHEST)
            grads['od_pool_w'][li] = dpw
            grads['od_pool_scale'][li] = dps.reshape(-1)
            grads['od_w_out'][li] = dwout
            grads['od_w_in'][li] = _mm_tn(s['x0'], 0, D_MODEL, dh, name=f"d_in")
            db = _mm([(dh, 0, 4 * MIX)], w['od_w_in'][li], trans_b=True, name=f"in_proj_bwd")
        da = dr1
    grad_x = _add_n([da, db], [ALPHA, 1.0], name="grad_x")
    return loss, grad_x, {n: jnp.stack(g) for n, g in grads.items()}


def _reduce_big(grads, c, me):
    names = list(BIG)
    mine, other = [], []
    for n in names:
        st = _stack(n, grads[n])
        half = st.shape[1] // 2
        mine.append(lax.dynamic_slice_in_dim(st, c * half, half, axis=1))
        other.append(lax.dynamic_slice_in_dim(st, (1 - c) * half, half, axis=1))
    from_sibling = _swap_sibling(other, name="grad_pair_swap")
    chip_sums = []
    for n, a, b in zip(names, mine, from_sibling):
        k, half, cols = a.shape
        chip_sums.append(_add_n([a.reshape(k * half, cols), b.reshape(k * half, cols)], [1.0, 1.0],
                                name=f"grad_pair_add_{n}").reshape(k, half, cols))
    from_chips = _scatter_chips(chip_sums, name="grad_chip_scatter")
    halves = []
    for n, own, got in zip(names, chip_sums, from_chips):
        halves.append(_add_n([lax.dynamic_index_in_dim(own, me, 0, keepdims=False), got[0], got[1], got[2]], [1.0] * 4,
                             name=f"grad_chip_add_{n}"))
    from_sibling = _swap_sibling(halves, name="grad_half_swap")
    out = {}
    for n, a, b in zip(names, halves, from_sibling):
        out[n] = jnp.where(c == 0, jnp.concatenate([a, b], axis=0), jnp.concatenate([b, a], axis=0))
    return out


def kernel(x, p, ev_w_in, ev_lambda_re, ev_lambda_im, ev_log_dt, ev_b_re, ev_b_im, ev_c_re, ev_c_im, ev_d, ev_w_glu, ev_b_glu, ev_conv_w, ev_w_out, od_w_in, od_rel_bias, od_pool_w, od_pool_scale, od_w_out, ln_mix_g, ln_mix_b, ln_ffn_g, ln_ffn_b, ffn_w_up, ffn_w_down, ple_w_proj, ple_w_gate, ple_b_gate, loss_target, m_ev_w_in, m_ev_lambda_re, m_ev_lambda_im, m_ev_log_dt, m_ev_b_re, m_ev_b_im, m_ev_c_re, m_ev_c_im, m_ev_d, m_ev_w_glu, m_ev_b_glu, m_ev_conv_w, m_ev_w_out, m_od_w_in, m_od_rel_bias, m_od_pool_w, m_od_pool_scale, m_od_w_out, m_ln_mix_g, m_ln_mix_b, m_ln_ffn_g, m_ln_ffn_b, m_ffn_w_up, m_ffn_w_down, m_ple_w_proj, m_ple_w_gate, m_ple_b_gate, v_ev_w_in, v_ev_lambda_re, v_ev_lambda_im, v_ev_log_dt, v_ev_b_re, v_ev_b_im, v_ev_c_re, v_ev_c_im, v_ev_d, v_ev_w_glu, v_ev_b_glu, v_ev_conv_w, v_ev_w_out, v_od_w_in, v_od_rel_bias, v_od_pool_w, v_od_pool_scale, v_od_w_out, v_ln_mix_g, v_ln_mix_b, v_ln_ffn_g, v_ln_ffn_b, v_ffn_w_up, v_ffn_w_down, v_ple_w_proj, v_ple_w_gate, v_ple_b_gate):
    given = locals()
    a = {n: given[n] for n in INPUTS}
    x, y, c = lax.axis_index("x"), lax.axis_index("y"), lax.axis_index("c")
    me = 2 * x + y

    misc = jnp.concatenate([a['ev_conv_w'].reshape(6, 128), a['od_pool_scale'], jnp.zeros((8, 128), F32)], axis=0)
    gathered = _gather_chips([a[n].astype(BF16).reshape(_shard_rows(n)) for n in BIG] + [misc], name="weight_gather")
    w = {n: _unstack(n, g) for n, g in zip(BIG, gathered)}
    gm = gathered[-1]
    w['ev_conv_w'] = gm[:, 0:6].reshape(N_CHIPS, 2, 3, 128).transpose(1, 2, 0, 3).reshape(2, 3, 512)
    w['od_pool_scale'] = gm[:, 6:8].transpose(1, 0, 2).reshape(2, 512)
    for n in REPLICATED:
        w[n] = a[n]

    loss, grad_x, grads = _local_step(a['x'][0], a['p'][:, 0], a['loss_target'][0], w)
    loss = lax.psum(loss[0, 0], ("x", "y", "c"))

    small_names = REPLICATED + list(SMALL_SHARDED)
    small = _all_reduce_small(_pack([grads[n] for n in small_names]), name="small_grad_all_reduce")
    small = dict(zip(small_names, _unpack(small, [grads[n].shape for n in small_names])))
    for n in SMALL_SHARDED:
        small[n] = lax.dynamic_slice_in_dim(small[n], me * 128, 128, axis=small[n].ndim - 1)
    big = _reduce_big(grads, c, me)

    res = {}
    for n in BIG:
        shape = a[n].shape
        d, m_, v_ = _adamw(a[n].reshape(big[n].shape), big[n], a['m_' + n].reshape(big[n].shape),
                           a['v_' + n].reshape(big[n].shape), name=f"adamw_{n}")
        res[n] = (big[n].reshape(shape), d.reshape(shape), m_.reshape(shape), v_.reshape(shape))
    shapes = [a[n].shape for n in small_names]
    d, m_, v_ = _adamw(_pack([a[n] for n in small_names]), _pack([small[n] for n in small_names]),
                       _pack([a['m_' + n] for n in small_names]), _pack([a['v_' + n] for n in small_names]), name="adamw_small")
    for n, dd, mm, vv in zip(small_names, _unpack(d, shapes), _unpack(m_, shapes), _unpack(v_, shapes)):
        res[n] = (small[n], dd, mm, vv)

    outs = [loss, grad_x[None]]
    for part in range(4):
        outs += [res[n][part] for n in WEIGHTS]
    return tuple(outs)
```

```python
import functools
import math

import jax
import jax.numpy as jnp
import numpy as np
from jax import lax
from jax.experimental import pallas as pl
from jax.experimental.pallas import tpu as pltpu

F32 = jnp.float32
BF16 = jnp.bfloat16
MESH = pl.DeviceIdType.MESH
HIGHEST = lax.Precision.HIGHEST

D_MODEL = 1024
DEPTH = 4
MIX = 512
S5_GROUPS = 32
S5_GROUP = 16
S5_STATE = 64
S5_N = S5_GROUPS * S5_STATE
CHUNK = 64
LEFT_CHUNKS = 8
MAX_REL = 128
ATT_HEADS = 8
HEAD_DIM = 64
POOL_WINDOWS = (2, 4, 8, 16)
POOL_GROUP = 128
D_FF = 2816
D_PLE = 256
ALPHA = (2 * DEPTH) ** 0.25
LN_EPS = 1e-5
NEG_INF = -1e30
N_CHIPS = 4
N_DEV = 8

ADAM_LR = 0.001
ADAM_B1 = 0.9
ADAM_B2 = 0.999
ADAM_EPS = 1e-08
ADAM_WD = 0.01
ADAM_STEP = 10

TM = 512
T_S5 = 256
T_ATT = 512
VMEM_BIG = 56 * 1024 * 1024


VMEM_DEFAULT = 48 * 1024 * 1024


def _cparams(sem, vmem=None):
    return pltpu.CompilerParams(dimension_semantics=sem, vmem_limit_bytes=vmem or VMEM_DEFAULT)


def _sigmoid(x):
    return 1.0 / (1.0 + jnp.exp(-x))


def _mm(a_parts, b, *, name, trans_b=False, out_dtype=F32, tm=TM, tn=1024, vmem=None):
    m = a_parts[0][0].shape[0]
    n = b.shape[0] if trans_b else b.shape[1]
    kk = b.shape[1] if trans_b else b.shape[0]
    tn = min(tn, n)
    widths = [w for _, _, w in a_parts]
    assert sum(widths) == kk and m % tm == 0 and n % tn == 0
    na = len(a_parts)

    def body(*refs):
        b_ref, o_ref = refs[na], refs[na + 1]
        acc = None
        k0 = 0
        for ar, w in zip(refs[:na], widths):
            a = ar[...].astype(BF16)
            if trans_b:
                part = lax.dot_general(a, b_ref[:, k0:k0 + w], (((1,), (1,)), ((), ())), preferred_element_type=F32)
            else:
                part = jnp.dot(a, b_ref[k0:k0 + w, :], preferred_element_type=F32)
            acc = part if acc is None else acc + part
            k0 += w
        o_ref[...] = acc.astype(o_ref.dtype)

    in_specs = [pl.BlockSpec((tm, w), functools.partial(lambda j, i, cb: (i, cb), cb=cb)) for _, cb, w in a_parts]
    if trans_b:
        in_specs.append(pl.BlockSpec((tn, kk), lambda j, i: (j, 0)))
    else:
        in_specs.append(pl.BlockSpec((kk, tn), lambda j, i: (0, j)))
    return pl.pallas_call(
        body, name=name, grid=(n // tn, m // tm), in_specs=in_specs,
        out_specs=pl.BlockSpec((tm, tn), lambda j, i: (i, j)),
        out_shape=jax.ShapeDtypeStruct((m, n), out_dtype),
        compiler_params=_cparams(("parallel", "parallel"), vmem),
    )(*[a for a, _, _ in a_parts], b)


def _mm_tn(a, a_cb, ka, b, *, name, tk=1024, tn=1024, tmr=TM, vmem=None):
    m = a.shape[0]
    n = b.shape[1]
    tk = min(tk, ka)
    tn = min(tn, n)
    assert ka % tk == 0 and n % tn == 0 and m % tmr == 0
    kb = ka // tk

    def body(a_ref, b_ref, o_ref):
        @pl.when(pl.program_id(2) == 0)
        def _():
            o_ref[...] = jnp.zeros_like(o_ref)

        o_ref[...] += lax.dot_general(a_ref[...].astype(BF16), b_ref[...].astype(BF16), (((0,), (0,)), ((), ())),
                                      preferred_element_type=F32)

    return pl.pallas_call(
        body, name=name, grid=(kb, n // tn, m // tmr),
        in_specs=[pl.BlockSpec((tmr, tk), lambda k, j, r: (r, a_cb * kb + k)),
                  pl.BlockSpec((tmr, tn), lambda k, j, r: (r, j))],
        out_specs=pl.BlockSpec((tk, tn), lambda k, j, r: (k, j)),
        out_shape=jax.ShapeDtypeStruct((ka, n), F32),
        compiler_params=_cparams(("parallel", "parallel", "arbitrary"), vmem),
    )(a, b)


def _ln_stats(r):
    mu = jnp.mean(r, axis=-1, keepdims=True)
    xc = r - mu
    var = jnp.mean(xc * xc, axis=-1, keepdims=True)
    rstd = lax.rsqrt(var + LN_EPS)
    return xc * rstd, rstd


def _ln_fwd(xa, mix, g, b, *, name):
    m, n = xa.shape

    def body(xa_ref, mix_ref, g_ref, b_ref, r_ref, y_ref):
        r = ALPHA * xa_ref[...] + mix_ref[...]
        xhat, _ = _ln_stats(r)
        r_ref[...] = r
        y_ref[...] = xhat * g_ref[...] + b_ref[...]

    row = pl.BlockSpec((TM, n), lambda i: (i, 0))
    vec = pl.BlockSpec((1, n), lambda i: (0, 0))
    return pl.pallas_call(
        body, name=name, grid=(m // TM,), in_specs=[row, row, vec, vec], out_specs=[row, row],
        out_shape=[jax.ShapeDtypeStruct((m, n), F32)] * 2,
        compiler_params=_cparams(("parallel",)),
    )(xa, mix, g, b)


def _ln_bwd(r, da, db, ca, g, *, name):
    m, n = r.shape

    def body(r_ref, da_ref, db_ref, g_ref, dr_ref, dg_ref, dbias_ref):
        @pl.when(pl.program_id(0) == 0)
        def _():
            dg_ref[...] = jnp.zeros_like(dg_ref)
            dbias_ref[...] = jnp.zeros_like(dbias_ref)

        dy = ca * da_ref[...] + db_ref[...]
        xhat, rstd = _ln_stats(r_ref[...])
        dg_ref[...] += jnp.sum(dy * xhat, axis=0, keepdims=True)
        dbias_ref[...] += jnp.sum(dy, axis=0, keepdims=True)
        dxh = dy * g_ref[...]
        m1 = jnp.mean(dxh, axis=-1, keepdims=True)
        m2 = jnp.mean(dxh * xhat, axis=-1, keepdims=True)
        dr_ref[...] = rstd * (dxh - m1 - xhat * m2)

    row = pl.BlockSpec((TM, n), lambda i: (i, 0))
    vec = pl.BlockSpec((1, n), lambda i: (0, 0))
    return pl.pallas_call(
        body, name=name, grid=(m // TM,), in_specs=[row, row, row, vec], out_specs=[row, vec, vec],
        out_shape=[jax.ShapeDtypeStruct((m, n), F32), jax.ShapeDtypeStruct((1, n), F32), jax.ShapeDtypeStruct((1, n), F32)],
        compiler_params=_cparams(("arbitrary",)),
    )(r, da, db, g)


def _ffn_up(x1, wup, *, name):
    m = x1.shape[0]
    tn = D_FF // 2

    def body(x_ref, wg_ref, wu_ref, a_ref, g_ref, u_ref):
        x = x_ref[...].astype(BF16)
        g = jnp.dot(x, wg_ref[...], preferred_element_type=F32)
        u = jnp.dot(x, wu_ref[...], preferred_element_type=F32)
        a_ref[...] = (g * _sigmoid(g) * u).astype(BF16)
        g_ref[...] = g.astype(BF16)
        u_ref[...] = u.astype(BF16)

    out = pl.BlockSpec((TM, tn), lambda j, i: (i, j))
    return pl.pallas_call(
        body, name=name, grid=(2, m // TM),
        in_specs=[pl.BlockSpec((TM, D_MODEL), lambda j, i: (i, 0)),
                  pl.BlockSpec((D_MODEL, tn), lambda j, i: (0, j)),
                  pl.BlockSpec((D_MODEL, tn), lambda j, i: (0, j + 2))],
        out_specs=[out, out, out], out_shape=[jax.ShapeDtypeStruct((m, D_FF), BF16)] * 3,
        compiler_params=_cparams(("parallel", "parallel")),
    )(x1, wup, wup)


def _ffn_down_bwd(df, wdown, g, u, *, name):
    m = df.shape[0]
    tm = 256

    def body(df_ref, w_ref, g_ref, u_ref, o_ref):
        da = lax.dot_general(df_ref[...].astype(BF16), w_ref[...], (((1,), (1,)), ((), ())), preferred_element_type=F32)
        gg = g_ref[...].astype(F32)
        sg = _sigmoid(gg)
        o_ref[:, :D_FF] = (da * u_ref[...].astype(F32) * (sg * (1.0 + gg * (1.0 - sg)))).astype(BF16)
        o_ref[:, D_FF:] = (da * (gg * sg)).astype(BF16)

    return pl.pallas_call(
        body, name=name, grid=(m // tm,),
        in_specs=[pl.BlockSpec((tm, D_MODEL), lambda i: (i, 0)), pl.BlockSpec((D_FF, D_MODEL), lambda i: (0, 0)),
                  pl.BlockSpec((tm, D_FF), lambda i: (i, 0)), pl.BlockSpec((tm, D_FF), lambda i: (i, 0))],
        out_specs=pl.BlockSpec((tm, 2 * D_FF), lambda i: (i, 0)),
        out_shape=jax.ShapeDtypeStruct((m, 2 * D_FF), BF16),
        compiler_params=_cparams(("parallel",), VMEM_BIG),
    )(df, wdown, g, u)


def _ple_fwd(x2, zg, pp, bg, *, name):
    m, n = x2.shape

    def body(x_ref, zg_ref, pp_ref, bg_ref, x3_ref, gate_ref, ppb_ref):
        gate = _sigmoid(zg_ref[...] + bg_ref[...])
        pp = pp_ref[...]
        x3_ref[...] = x_ref[...] + gate * pp
        gate_ref[...] = gate.astype(BF16)
        ppb_ref[...] = pp.astype(BF16)

    row = pl.BlockSpec((TM, n), lambda i: (i, 0))
    return pl.pallas_call(
        body, name=name, grid=(m // TM,), in_specs=[row, row, row, pl.BlockSpec((1, n), lambda i: (0, 0))],
        out_specs=[row, row, row],
        out_shape=[jax.ShapeDtypeStruct((m, n), F32), jax.ShapeDtypeStruct((m, n), BF16), jax.ShapeDtypeStruct((m, n), BF16)],
        compiler_params=_cparams(("parallel",)),
    )(x2, zg, pp, bg)


def _ple_bwd(da, db, gate, pp, *, name):
    m, n = da.shape
    two = db is not None

    def body(*refs):
        if two:
            da_ref, db_ref, gate_ref, pp_ref, dx_ref, dz_ref, dpp_ref, dbg_ref = refs
            dx = ALPHA * da_ref[...] + db_ref[...]
        else:
            da_ref, gate_ref, pp_ref, dx_ref, dz_ref, dpp_ref, dbg_ref = refs
            dx = da_ref[...]

        @pl.when(pl.program_id(0) == 0)
        def _():
            dbg_ref[...] = jnp.zeros_like(dbg_ref)

        gate = gate_ref[...].astype(F32)
        dz = dx * pp_ref[...].astype(F32) * gate * (1.0 - gate)
        dx_ref[...] = dx
        dz_ref[...] = dz.astype(BF16)
        dpp_ref[...] = (dx * gate).astype(BF16)
        dbg_ref[...] += jnp.sum(dz, axis=0, keepdims=True)

    row = pl.BlockSpec((TM, n), lambda i: (i, 0))
    vec = pl.BlockSpec((1, n), lambda i: (0, 0))
    ins = [da, db, gate, pp] if two else [da, gate, pp]
    return pl.pallas_call(
        body, name=name, grid=(m // TM,), in_specs=[row] * len(ins), out_specs=[row, row, row, vec],
        out_shape=[jax.ShapeDtypeStruct((m, n), F32), jax.ShapeDtypeStruct((m, n), BF16), jax.ShapeDtypeStruct((m, n), BF16),
                   jax.ShapeDtypeStruct((1, n), F32)],
        compiler_params=_cparams(("arbitrary",)),
    )(*ins)


def _loss_head(y, target, *, name):
    m, n = y.shape

    def body(y_ref, t_ref, loss_ref, dy_ref):
        @pl.when(pl.program_id(0) == 0)
        def _():
            loss_ref[...] = jnp.zeros_like(loss_ref)

        err = y_ref[...] - t_ref[...]
        dy_ref[...] = err * (1.0 / n)
        per_tok = jnp.mean(err * err, axis=-1, keepdims=True)
        loss_ref[...] += 0.5 * jnp.sum(per_tok, axis=0, keepdims=True)

    row = pl.BlockSpec((TM, n), lambda i: (i, 0))
    return pl.pallas_call(
        body, name=name, grid=(m // TM,), in_specs=[row, row],
        out_specs=[pl.BlockSpec((1, 1), lambda i: (0, 0)), row],
        out_shape=[jax.ShapeDtypeStruct((1, 1), F32), jax.ShapeDtypeStruct((m, n), F32)],
        compiler_params=_cparams(("arbitrary",)),
    )(y, target)


def _gelu(y):
    c = math.sqrt(2.0 / math.pi)
    return 0.5 * y * (1.0 + jnp.tanh(c * (y + 0.044715 * y * y * y)))


def _gelu_grad(y):
    c = math.sqrt(2.0 / math.pi)
    t = jnp.tanh(c * (y + 0.044715 * y * y * y))
    return 0.5 * (1.0 + t) + 0.5 * y * (1.0 - t * t) * c * (1.0 + 3.0 * 0.044715 * y * y)


STRIP = 256


def _scan_strip(xr, xi, cr, ci, ptab_ref, off, rowmod, down):
    t = xr.shape[0]
    base = 0 if down else 16
    p_r = ptab_ref[base:base + 8, pl.ds(off, STRIP)]
    p_i = ptab_ref[base + 8:base + 16, pl.ds(off, STRIP)]
    if not down:
        p_i = -p_i
    for k in range(3):
        s = 1 << k
        idx = s - 1 if down else 8 - s
        pr, pi_ = p_r[idx:idx + 1], p_i[idx:idx + 1]
        if down:
            sr = jnp.where(rowmod >= s, pltpu.roll(xr, s, 0), 0.0)
            si = jnp.where(rowmod >= s, pltpu.roll(xi, s, 0), 0.0)
        else:
            sr = jnp.where(rowmod < 8 - s, pltpu.roll(xr, t - s, 0), 0.0)
            si = jnp.where(rowmod < 8 - s, pltpu.roll(xi, t - s, 0), 0.0)
        xr, xi = xr + pr * sr - pi_ * si, xi + pr * si + pi_ * sr
    ng = t // 8
    out_r, out_i = [None] * ng, [None] * ng
    for g in (range(ng) if down else reversed(range(ng))):
        cbr = jnp.broadcast_to(cr, (8, STRIP))
        cbi = jnp.broadcast_to(ci, (8, STRIP))
        br = xr[8 * g:8 * g + 8] + p_r * cbr - p_i * cbi
        bi = xi[8 * g:8 * g + 8] + p_r * cbi + p_i * cbr
        cr, ci = (br[7:8], bi[7:8]) if down else (br[0:1], bi[0:1])
        out_r[g], out_i[g] = br, bi
    return jnp.concatenate(out_r, axis=0), jnp.concatenate(out_i, axis=0)


def _s5_fwd(h, bmat, cmat, dvec, wglu, bglu, ptab, *, name):
    m = h.shape[0]
    t = T_S5
    nb = m // t

    def body(u_ref, bmat_ref, cmat_ref, d_ref, wglu_ref, bglu_ref, ptab_ref,
             out_ref, y_ref, hb_ref, bu_ref, carry_ref):
        @pl.when(pl.program_id(0) == 0)
        def _():
            carry_ref[...] = jnp.zeros_like(carry_ref)

        hb_ref[0] = carry_ref[...]
        u = u_ref[...]
        bu_ref[...] = jnp.dot(u.astype(BF16), bmat_ref[...], preferred_element_type=F32)
        rowmod = lax.broadcasted_iota(jnp.int32, (t, STRIP), 0) & 7

        def strip(j, c):
            off = pl.multiple_of(j * STRIP, STRIP)
            offi = pl.multiple_of(S5_N + j * STRIP, STRIP)
            xr, xi = _scan_strip(bu_ref[:, pl.ds(off, STRIP)], bu_ref[:, pl.ds(offi, STRIP)],
                                 carry_ref[0:1, pl.ds(off, STRIP)], carry_ref[0:1, pl.ds(offi, STRIP)],
                                 ptab_ref, off, rowmod, True)
            bu_ref[:, pl.ds(off, STRIP)] = xr
            bu_ref[:, pl.ds(offi, STRIP)] = xi
            carry_ref[0:1, pl.ds(off, STRIP)] = xr[t - 1:t, :]
            carry_ref[0:1, pl.ds(offi, STRIP)] = xi[t - 1:t, :]
            return c

        lax.fori_loop(0, S5_N // STRIP, strip, 0)
        y = jnp.dot(bu_ref[...].astype(BF16), cmat_ref[...], preferred_element_type=F32) + d_ref[...] * u
        y_ref[...] = y
        g = _gelu(y)
        zz = jnp.dot(g.astype(BF16), wglu_ref[...], preferred_element_type=F32) + bglu_ref[...]
        out_ref[...] = (g * _sigmoid(zz)).astype(BF16)

    const = lambda shape: pl.BlockSpec(shape, lambda i: (0,) * len(shape))
    row_spec = pl.BlockSpec((t, MIX), lambda i: (i, 0))
    return pl.pallas_call(
        body, name=name, grid=(nb,),
        in_specs=[row_spec, const((MIX, 2 * S5_N)), const((2 * S5_N, MIX)), const((1, MIX)), const((MIX, MIX)),
                  const((1, MIX)), const((32, S5_N))],
        out_specs=[row_spec, row_spec, pl.BlockSpec((1, 1, 2 * S5_N), lambda i: (i, 0, 0))],
        out_shape=[jax.ShapeDtypeStruct((m, MIX), BF16), jax.ShapeDtypeStruct((m, MIX), F32),
                   jax.ShapeDtypeStruct((nb, 1, 2 * S5_N), F32)],
        scratch_shapes=[pltpu.VMEM((t, 2 * S5_N), F32), pltpu.VMEM((1, 2 * S5_N), F32)],
        compiler_params=_cparams(("arbitrary",), VMEM_BIG),
    )(h, bmat, cmat, dvec, wglu, bglu, ptab)


def _s5_bwd(dcat, ypre, h, hb, bmat, cmat, dvec, wglu, bglu, ptab, *, name):
    m = h.shape[0]
    t = T_S5
    nb = m // t

    def body(dya_ref, y_ref, u_ref, hb_ref, bmat_ref, cmat_ref, d_ref, wglu_ref, bglu_ref, ptab_ref,
             du_ref, xb_ref, gb_ref, gq_ref, dzz_ref, dyq_ref, dlam_ref, dbglu_ref, dd_ref,
             bu_ref, dx_ref, gcarry_ref):
        @pl.when(pl.program_id(0) == 0)
        def _():
            gcarry_ref[...] = jnp.zeros_like(gcarry_ref)
            dlam_ref[...] = jnp.zeros_like(dlam_ref)
            dbglu_ref[...] = jnp.zeros_like(dbglu_ref)
            dd_ref[...] = jnp.zeros_like(dd_ref)

        u = u_ref[...]
        y = y_ref[...]
        g = _gelu(y)
        gq = g.astype(BF16)
        sg = _sigmoid(jnp.dot(gq, wglu_ref[...], preferred_element_type=F32) + bglu_ref[...])
        dout = dya_ref[...]
        dzz = dout * g * sg * (1.0 - sg)
        dzzq = dzz.astype(BF16)
        dg = dout * sg + lax.dot_general(dzzq, wglu_ref[...], (((1,), (1,)), ((), ())), preferred_element_type=F32)
        dy = dg * _gelu_grad(y)
        dyq = dy.astype(BF16)
        gq_ref[...] = gq
        dzz_ref[...] = dzzq
        dyq_ref[...] = dyq
        dbglu_ref[...] += jnp.sum(dzz, axis=0, keepdims=True)
        dd_ref[...] += jnp.sum(dy * u, axis=0, keepdims=True)

        dx_ref[...] = lax.dot_general(dyq, cmat_ref[...], (((1,), (1,)), ((), ())), preferred_element_type=F32)
        bu_ref[...] = jnp.dot(u.astype(BF16), bmat_ref[...], preferred_element_type=F32)
        row = lax.broadcasted_iota(jnp.int32, (t, STRIP), 0)
        rowmod = row & 7

        def strip(j, c):
            off = pl.multiple_of(j * STRIP, STRIP)
            offi = pl.multiple_of(S5_N + j * STRIP, STRIP)
            hr = hb_ref[0, 0:1, pl.ds(off, STRIP)]
            hi = hb_ref[0, 0:1, pl.ds(offi, STRIP)]
            xr, xi = _scan_strip(bu_ref[:, pl.ds(off, STRIP)], bu_ref[:, pl.ds(offi, STRIP)], hr, hi,
                                 ptab_ref, off, rowmod, True)
            xb_ref[:, pl.ds(off, STRIP)] = xr.astype(BF16)
            xb_ref[:, pl.ds(offi, STRIP)] = xi.astype(BF16)
            pr_ = jnp.where(row == 0, hr, pltpu.roll(xr, 1, 0))
            pi_ = jnp.where(row == 0, hi, pltpu.roll(xi, 1, 0))

            gr, gi = _scan_strip(dx_ref[:, pl.ds(off, STRIP)], dx_ref[:, pl.ds(offi, STRIP)],
                                 gcarry_ref[0:1, pl.ds(off, STRIP)], gcarry_ref[0:1, pl.ds(offi, STRIP)],
                                 ptab_ref, off, rowmod, False)
            gb_ref[:, pl.ds(off, STRIP)] = gr.astype(BF16)
            gb_ref[:, pl.ds(offi, STRIP)] = gi.astype(BF16)
            gcarry_ref[0:1, pl.ds(off, STRIP)] = gr[0:1, :]
            gcarry_ref[0:1, pl.ds(offi, STRIP)] = gi[0:1, :]
            dlam_ref[0:1, pl.ds(off, STRIP)] += jnp.sum(pr_ * gr + pi_ * gi, axis=0, keepdims=True)
            dlam_ref[1:2, pl.ds(off, STRIP)] += jnp.sum(pr_ * gi - pi_ * gr, axis=0, keepdims=True)
            return c

        lax.fori_loop(0, S5_N // STRIP, strip, 0)
        du_ref[...] = dy * d_ref[...] + lax.dot_general(gb_ref[...], bmat_ref[...], (((1,), (1,)), ((), ())),
                                                        preferred_element_type=F32)

    const = lambda shape: pl.BlockSpec(shape, lambda i: (0,) * len(shape))
    rev = lambda i: (nb - 1 - i, 0)
    row_spec = pl.BlockSpec((t, MIX), rev)
    wide = pl.BlockSpec((t, 2 * S5_N), rev)
    return pl.pallas_call(
        body, name=name, grid=(nb,),
        in_specs=[row_spec, row_spec, row_spec, pl.BlockSpec((1, 1, 2 * S5_N), lambda i: (nb - 1 - i, 0, 0)),
                  const((MIX, 2 * S5_N)), const((2 * S5_N, MIX)), const((1, MIX)), const((MIX, MIX)), const((1, MIX)),
                  const((32, S5_N))],
        out_specs=[row_spec, wide, wide, row_spec, row_spec, row_spec, const((2, S5_N)), const((1, MIX)), const((1, MIX))],
        out_shape=[jax.ShapeDtypeStruct((m, MIX), F32), jax.ShapeDtypeStruct((m, 2 * S5_N), BF16),
                   jax.ShapeDtypeStruct((m, 2 * S5_N), BF16), jax.ShapeDtypeStruct((m, MIX), BF16),
                   jax.ShapeDtypeStruct((m, MIX), BF16), jax.ShapeDtypeStruct((m, MIX), BF16),
                   jax.ShapeDtypeStruct((2, S5_N), F32), jax.ShapeDtypeStruct((1, MIX), F32), jax.ShapeDtypeStruct((1, MIX), F32)],
        scratch_shapes=[pltpu.VMEM((t, 2 * S5_N), F32), pltpu.VMEM((t, 2 * S5_N), F32), pltpu.VMEM((1, 2 * S5_N), F32)],
        compiler_params=_cparams(("arbitrary",), VMEM_BIG),
    )(dcat, ypre, h, hb, bmat, cmat, dvec, wglu, bglu, ptab)


HALO = 8


def _taps_down(zext, t):
    return pltpu.roll(zext, 1, 0)[HALO:HALO + t], pltpu.roll(zext, 2, 0)[HALO:HALO + t]


def _conv_z(c_ref, x_ref, cp_ref, xp_ref, first, t):
    z = c_ref[...] * x_ref[...]
    zp = jnp.where(first, 0.0, cp_ref[t - HALO:t, :] * xp_ref[t - HALO:t, :])
    z1, z2 = _taps_down(jnp.concatenate([zp, z], axis=0), t)
    return z, z1, z2


def _conv_fwd(h, cw, *, name):
    m = h.shape[0]
    t = TM
    nb = m // t

    def body(b_ref, c_ref, x_ref, cp_ref, xp_ref, w_ref, o_ref):
        z, z1, z2 = _conv_z(c_ref, x_ref, cp_ref, xp_ref, pl.program_id(0) == 0, t)
        o_ref[...] = (b_ref[...] * (w_ref[0:1, :] * z2 + w_ref[1:2, :] * z1 + w_ref[2:3, :] * z)).astype(BF16)

    cur = lambda cb: pl.BlockSpec((t, MIX), lambda i: (i, cb))
    prev = lambda cb: pl.BlockSpec((t, MIX), lambda i: (jnp.maximum(i - 1, 0), cb))
    return pl.pallas_call(
        body, name=name, grid=(nb,),
        in_specs=[cur(1), cur(2), cur(3), prev(2), prev(3), pl.BlockSpec((3, MIX), lambda i: (0, 0))],
        out_specs=pl.BlockSpec((t, MIX), lambda i: (i, 0)),
        out_shape=jax.ShapeDtypeStruct((m, MIX), BF16),
        compiler_params=_cparams(("parallel",)),
    )(h, h, h, h, h, cw)


def _conv_bwd(dcat, h, cw, *, name):
    m = h.shape[0]
    t = TM
    nb = m // t

    def body(dy_ref, dyn_ref, b_ref, c_ref, x_ref, cp_ref, xp_ref, bn_ref, w_ref, o_ref, dw_ref):
        i = pl.program_id(0)

        @pl.when(i == 0)
        def _():
            dw_ref[...] = jnp.zeros_like(dw_ref)

        z, z1, z2 = _conv_z(c_ref, x_ref, cp_ref, xp_ref, i == 0, t)
        w0, w1, w2 = w_ref[0:1, :], w_ref[1:2, :], w_ref[2:3, :]
        dy = dy_ref[...]
        dconv = dy * b_ref[...]
        dnext = jnp.where(i == nb - 1, 0.0, dyn_ref[0:HALO, :] * bn_ref[0:HALO, :])
        dext = jnp.concatenate([dconv, dnext], axis=0)
        d1 = pltpu.roll(dext, t + HALO - 1, 0)[0:t]
        d2 = pltpu.roll(dext, t + HALO - 2, 0)[0:t]
        dz = w2 * dconv + w1 * d1 + w0 * d2
        o_ref[:, 0:MIX] = dy * (w0 * z2 + w1 * z1 + w2 * z)
        o_ref[:, MIX:2 * MIX] = dz * x_ref[...]
        o_ref[:, 2 * MIX:3 * MIX] = dz * c_ref[...]
        dw_ref[0:1, :] += jnp.sum(dconv * z2, axis=0, keepdims=True)
        dw_ref[1:2, :] += jnp.sum(dconv * z1, axis=0, keepdims=True)
        dw_ref[2:3, :] += jnp.sum(dconv * z, axis=0, keepdims=True)

    cur = lambda cb: pl.BlockSpec((t, MIX), lambda i: (i, cb))
    prev = lambda cb: pl.BlockSpec((t, MIX), lambda i: (jnp.maximum(i - 1, 0), cb))
    nxt = lambda cb: pl.BlockSpec((t, MIX), lambda i: (jnp.minimum(i + 1, nb - 1), cb))
    return pl.pallas_call(
        body, name=name, grid=(nb,),
        in_specs=[cur(1), nxt(1), cur(1), cur(2), cur(3), prev(2), prev(3), nxt(1), pl.BlockSpec((3, MIX), lambda i: (0, 0))],
        out_specs=[pl.BlockSpec((t, 3 * MIX), lambda i: (i, 0)), pl.BlockSpec((8, MIX), lambda i: (0, 0))],
        out_shape=[jax.ShapeDtypeStruct((m, 3 * MIX), F32), jax.ShapeDtypeStruct((8, MIX), F32)],
        compiler_params=_cparams(("arbitrary",)),
    )(dcat, dcat, h, h, h, h, h, h, cw)


PHALO = 16


def _pool_pooled(z_ref, zp_ref, i, t):
    z = z_ref[...]
    zp = jnp.where(i == 0, 0.0, zp_ref[t - PHALO:t, :])
    s = jnp.concatenate([zp, z], axis=0)
    sums = {}
    width = 1
    while width < PHALO:
        s = s + pltpu.roll(s, width, 0)
        width *= 2
        sums[width] = s[PHALO:PHALO + t]
    tpos = i * t + lax.broadcasted_iota(jnp.int32, (t, 1), 0)
    outs = []
    for gi, w in enumerate(POOL_WINDOWS):
        lo = gi * POOL_GROUP
        count = jnp.minimum(tpos + 1, w).astype(F32)
        outs.append(sums[w][:, lo:lo + POOL_GROUP] / count - z[:, lo:lo + POOL_GROUP])
    return outs


def _pool_fwd(h, pw, ps, *, name):
    m = h.shape[0]
    t = TM
    nb = m // t

    def body(z_ref, zp_ref, pw_ref, ps_ref, o_ref):
        pooled = _pool_pooled(z_ref, zp_ref, pl.program_id(0), t)
        for gi in range(len(POOL_WINDOWS)):
            lo = gi * POOL_GROUP
            mixed = jnp.dot(pooled[gi].astype(BF16), pw_ref[gi], preferred_element_type=F32)
            o_ref[:, lo:lo + POOL_GROUP] = (mixed * ps_ref[:, lo:lo + POOL_GROUP]).astype(BF16)

    return pl.pallas_call(
        body, name=name, grid=(nb,),
        in_specs=[pl.BlockSpec((t, MIX), lambda i: (i, 3)), pl.BlockSpec((t, MIX), lambda i: (jnp.maximum(i - 1, 0), 3)),
                  pl.BlockSpec((4, POOL_GROUP, POOL_GROUP), lambda i: (0, 0, 0)), pl.BlockSpec((1, MIX), lambda i: (0, 0))],
        out_specs=pl.BlockSpec((t, MIX), lambda i: (i, 0)),
        out_shape=jax.ShapeDtypeStruct((m, MIX), BF16),
        compiler_params=_cparams(("parallel",)),
    )(h, h, pw, ps)


def _pool_bwd(dcat, h, pw, ps, *, name):
    m = h.shape[0]
    t = TM
    nb = m // t

    def body(dy_ref, dyn_ref, z_ref, zp_ref, pw_ref, ps_ref, dz_ref, dpw_ref, dps_ref):
        i = pl.program_id(0)

        @pl.when(i == 0)
        def _():
            dpw_ref[...] = jnp.zeros_like(dpw_ref)
            dps_ref[...] = jnp.zeros_like(dps_ref)

        pooled = _pool_pooled(z_ref, zp_ref, i, t)
        dy = dy_ref[...]
        tpos = i * t + lax.broadcasted_iota(jnp.int32, (t, 1), 0)
        for gi, w in enumerate(POOL_WINDOWS):
            lo = gi * POOL_GROUP
            sl = slice(lo, lo + POOL_GROUP)
            pq = pooled[gi].astype(BF16)
            mixed = jnp.dot(pq, pw_ref[gi], preferred_element_type=F32)
            dps_ref[:, sl] += jnp.sum(dy[:, sl] * mixed, axis=0, keepdims=True)
            dmix = (dy[:, sl] * ps_ref[:, sl]).astype(BF16)
            dpw_ref[gi] += lax.dot_general(pq, dmix, (((0,), (0,)), ((), ())), preferred_element_type=F32)
            dpool = lax.dot_general(dmix, pw_ref[gi], (((1,), (1,)), ((), ())), preferred_element_type=F32)
            dmix_n = (dyn_ref[0:PHALO, sl] * ps_ref[:, sl]).astype(BF16)
            dpool_n = lax.dot_general(dmix_n, pw_ref[gi], (((1,), (1,)), ((), ())), preferred_element_type=F32)
            e = dpool / jnp.minimum(tpos + 1, w).astype(F32)
            e_n = jnp.where(i == nb - 1, 0.0, dpool_n * (1.0 / w))
            f = jnp.concatenate([e, e_n], axis=0)
            width = 1
            while width < w:
                f = f + pltpu.roll(f, t + PHALO - width, 0)
                width *= 2
            dz_ref[:, sl] = f[0:t] - dpool

    return pl.pallas_call(
        body, name=name, grid=(nb,),
        in_specs=[pl.BlockSpec((t, MIX), lambda i: (i, 1)), pl.BlockSpec((t, MIX), lambda i: (jnp.minimum(i + 1, nb - 1), 1)),
                  pl.BlockSpec((t, MIX), lambda i: (i, 3)), pl.BlockSpec((t, MIX), lambda i: (jnp.maximum(i - 1, 0), 3)),
                  pl.BlockSpec((4, POOL_GROUP, POOL_GROUP), lambda i: (0, 0, 0)), pl.BlockSpec((1, MIX), lambda i: (0, 0))],
        out_specs=[pl.BlockSpec((t, MIX), lambda i: (i, 0)), pl.BlockSpec((4, POOL_GROUP, POOL_GROUP), lambda i: (0, 0, 0)),
                   pl.BlockSpec((1, MIX), lambda i: (0, 0))],
        out_shape=[jax.ShapeDtypeStruct((m, MIX), F32), jax.ShapeDtypeStruct((4, POOL_GROUP, POOL_GROUP), F32),
                   jax.ShapeDtypeStruct((1, MIX), F32)],
        compiler_params=_cparams(("arbitrary",)),
    )(dcat, dcat, h, h, pw, ps)


NKEY = 2 * T_ATT
SUB = 2 * CHUNK
KW = (LEFT_CHUNKS + 2) * CHUNK
NSUB = T_ATT // SUB


def _band_mask():
    qc = np.arange(SUB)[:, None] // CHUNK
    kc = np.arange(KW)[None, :] // CHUNK - LEFT_CHUNKS
    return np.where((kc <= qc) & (kc >= qc - LEFT_CHUNKS), 0.0, NEG_INF).astype(np.float32)


def _diag_index():
    c = np.arange(NKEY)
    d = np.where(c <= NKEY // 2 + CHUNK, T_ATT - c, T_ATT + NKEY - c)
    return np.clip(d, -MAX_REL, MAX_REL) + MAX_REL


def _bias_tile(vd_ref, mask_ref, tile_ref):
    for hh in range(2):
        v = vd_ref[0, hh:hh + 1, :]
        base = jnp.concatenate([v if s == 0 else pltpu.roll(v, s, 1) for s in range(8)], axis=0)
        for mrow in range(SUB // 8):
            rows = slice(8 * mrow, 8 * mrow + 8)
            blk = base if mrow == 0 else pltpu.roll(base, 8 * mrow, 1)
            tile_ref[hh, rows, :] = blk[:, :KW] + mask_ref[rows, :]


def _attn_probs(q, kc, tile, absent):
    s = lax.dot_general(q, kc, (((1,), (1,)), ((), ())), preferred_element_type=F32) + tile
    col = lax.broadcasted_iota(jnp.int32, s.shape, 1)
    s = jnp.where(col < absent, NEG_INF, s)
    p = jnp.exp(s - jnp.max(s, axis=-1, keepdims=True))
    return p * (1.0 / jnp.sum(p, axis=-1, keepdims=True))


def _attn_specs(nb):
    cur = lambda base: pl.BlockSpec((T_ATT, 128), lambda hp, i: (i, base + hp))
    prev = lambda base: pl.BlockSpec((T_ATT, 128), lambda hp, i: (jnp.maximum(i - 1, 0), base + hp))
    return [cur(0), cur(4), prev(4), cur(8), prev(8),
            pl.BlockSpec((1, 2, NKEY), lambda hp, i: (hp, 0, 0)), pl.BlockSpec((SUB, KW), lambda hp, i: (0, 0))]


def _attn_fwd(h, vdiag, mask, *, name):
    m = h.shape[0]
    nb = m // T_ATT

    def body(q_ref, k_ref, kp_ref, v_ref, vp_ref, vd_ref, mask_ref, o_ref, tile_ref):
        i = pl.program_id(1)

        @pl.when(i == 0)
        def _():
            _bias_tile(vd_ref, mask_ref, tile_ref)

        outs = []
        for hh in range(2):
            sl = slice(hh * HEAD_DIM, (hh + 1) * HEAD_DIM)
            kc = jnp.concatenate([kp_ref[:, sl], k_ref[:, sl]], axis=0).astype(BF16)
            vc = jnp.concatenate([vp_ref[:, sl], v_ref[:, sl]], axis=0).astype(BF16)
            sub = []
            for j in range(NSUB):
                q = (q_ref[SUB * j:SUB * (j + 1), sl] * (HEAD_DIM ** -0.5)).astype(BF16)
                p = _attn_probs(q, kc[SUB * j:SUB * j + KW], tile_ref[hh], jnp.where(i == 0, T_ATT - SUB * j, 0))
                sub.append(jnp.dot(p.astype(BF16), vc[SUB * j:SUB * j + KW], preferred_element_type=F32))
            outs.append(jnp.concatenate(sub, axis=0))
        o_ref[...] = jnp.concatenate(outs, axis=1).astype(BF16)

    return pl.pallas_call(
        body, name=name, grid=(ATT_HEADS // 2, nb), in_specs=_attn_specs(nb),
        out_specs=pl.BlockSpec((T_ATT, 128), lambda hp, i: (i, hp)),
        out_shape=jax.ShapeDtypeStruct((m, MIX), BF16),
        scratch_shapes=[pltpu.VMEM((2, SUB, KW), F32)],
        compiler_params=_cparams(("parallel", "arbitrary")),
    )(h, h, h, h, h, vdiag, mask)


def _attn_bwd(dcat, h, vdiag, mask, *, name):
    m = h.shape[0]
    nb = m // T_ATT

    def body(do_ref, q_ref, k_ref, kp_ref, v_ref, vp_ref, vd_ref, mask_ref,
             dq_ref, dk_ref, dkp_ref, dv_ref, dvp_ref, dvd_ref, tile_ref, acc_ref):
        i = pl.program_id(1)

        @pl.when(i == 0)
        def _():
            _bias_tile(vd_ref, mask_ref, tile_ref)
            acc_ref[...] = jnp.zeros_like(acc_ref)

        scale = HEAD_DIM ** -0.5

        def placed(part, j):
            pieces = [part]
            if j > 0:
                pieces.insert(0, jnp.zeros((SUB * j, HEAD_DIM), F32))
            if NKEY - KW - SUB * j > 0:
                pieces.append(jnp.zeros((NKEY - KW - SUB * j, HEAD_DIM), F32))
            return jnp.concatenate(pieces, axis=0)

        dqs, dks, dvs = [], [], []
        for hh in range(2):
            sl = slice(hh * HEAD_DIM, (hh + 1) * HEAD_DIM)
            kcat = jnp.concatenate([kp_ref[:, sl], k_ref[:, sl]], axis=0).astype(BF16)
            vcat = jnp.concatenate([vp_ref[:, sl], v_ref[:, sl]], axis=0).astype(BF16)
            dq_sub, dk, dv = [], None, None
            for j in range(NSUB):
                rows = slice(SUB * j, SUB * (j + 1))
                q = (q_ref[rows, sl] * scale).astype(BF16)
                kc, vc = kcat[SUB * j:SUB * j + KW], vcat[SUB * j:SUB * j + KW]
                do = do_ref[rows, sl].astype(BF16)
                p = _attn_probs(q, kc, tile_ref[hh], jnp.where(i == 0, T_ATT - SUB * j, 0))
                dv_j = lax.dot_general(p.astype(BF16), do, (((0,), (0,)), ((), ())), preferred_element_type=F32)
                dp = lax.dot_general(do, vc, (((1,), (1,)), ((), ())), preferred_element_type=F32)
                ds = p * (dp - jnp.sum(dp * p, axis=-1, keepdims=True))
                acc_ref[hh] += ds
                dsq = ds.astype(BF16)
                dq_sub.append(jnp.dot(dsq, kc, preferred_element_type=F32) * scale)
                dk_j = lax.dot_general(dsq, q, (((0,), (0,)), ((), ())), preferred_element_type=F32)
                dk = placed(dk_j, j) if dk is None else dk + placed(dk_j, j)
                dv = placed(dv_j, j) if dv is None else dv + placed(dv_j, j)
            dqs.append(jnp.concatenate(dq_sub, axis=0))
            dks.append(dk)
            dvs.append(dv)
        dq_ref[...] = jnp.concatenate(dqs, axis=1)
        dk_all = jnp.concatenate(dks, axis=1)
        dv_all = jnp.concatenate(dvs, axis=1)
        dkp_ref[...] = dk_all[:T_ATT]
        dk_ref[...] = dk_all[T_ATT:]
        dvp_ref[...] = dv_all[:T_ATT]
        dv_ref[...] = dv_all[T_ATT:]

        @pl.when(i == nb - 1)
        def _():
            pad = jnp.zeros((8, NKEY - KW), F32)
            for hh in range(2):
                r8 = jnp.concatenate([acc_ref[hh, 0:8, :], pad], axis=1)
                for mrow in range(1, SUB // 8):
                    blk = jnp.concatenate([acc_ref[hh, 8 * mrow:8 * mrow + 8, :], pad], axis=1)
                    r8 = r8 + pltpu.roll(blk, NKEY - 8 * mrow, 1)
                tot = r8[0:1, :]
                for s in range(1, 8):
                    tot = tot + pltpu.roll(r8[s:s + 1, :], NKEY - s, 1)
                dvd_ref[0, hh:hh + 1, :] = tot

    out = pl.BlockSpec((T_ATT, 128), lambda hp, i: (i, hp))
    return pl.pallas_call(
        body, name=name, grid=(ATT_HEADS // 2, nb),
        in_specs=[pl.BlockSpec((T_ATT, 128), lambda hp, i: (i, hp))] + _attn_specs(nb),
        out_specs=[out, out, out, out, out, pl.BlockSpec((1, 2, NKEY), lambda hp, i: (hp, 0, 0))],
        out_shape=[jax.ShapeDtypeStruct((m, MIX), F32)] * 5 + [jax.ShapeDtypeStruct((ATT_HEADS // 2, 2, NKEY), F32)],
        scratch_shapes=[pltpu.VMEM((2, SUB, KW), F32), pltpu.VMEM((2, SUB, KW), F32)],
        compiler_params=_cparams(("parallel", "arbitrary")),
    )(dcat, h, h, h, h, h, vdiag, mask)


def _row_tile(rows):
    for t in (512, 256, 128, 64, 32, 16, 8):
        if rows % t == 0:
            return t
    return rows


def _add_n(arrs, coefs, *, name, also_bf16=False):
    rows, cols = arrs[0].shape
    t = _row_tile(rows)
    n = len(arrs)

    def body(*refs):
        acc = None
        for r, cf in zip(refs[:n], coefs):
            v = r[...].astype(F32)
            v = v if cf == 1.0 else cf * v
            acc = v if acc is None else acc + v
        refs[n][...] = acc
        if also_bf16:
            refs[n + 1][...] = acc.astype(BF16)

    spec = pl.BlockSpec((t, cols), lambda i: (i, 0))
    f32 = jax.ShapeDtypeStruct((rows, cols), F32)
    return pl.pallas_call(
        body, name=name, grid=(rows // t,), in_specs=[spec] * n,
        out_specs=[spec, spec] if also_bf16 else spec,
        out_shape=[f32, jax.ShapeDtypeStruct((rows, cols), BF16)] if also_bf16 else f32,
        compiler_params=_cparams(("parallel",)),
    )(*arrs)


def _adamw(w, g, mom, var, *, name):
    rows, cols = w.shape
    t = _row_tile(rows)

    def body(w_ref, g_ref, m_ref, v_ref, d_ref, mo_ref, vo_ref):
        g_ = g_ref[...]
        m_ = ADAM_B1 * m_ref[...] + (1.0 - ADAM_B1) * g_
        v_ = ADAM_B2 * v_ref[...] + (1.0 - ADAM_B2) * (g_ * g_)
        m_hat = m_ / (1.0 - ADAM_B1 ** ADAM_STEP)
        v_hat = v_ / (1.0 - ADAM_B2 ** ADAM_STEP)
        d_ref[...] = -ADAM_LR * (m_hat / (jnp.sqrt(v_hat) + ADAM_EPS) + ADAM_WD * w_ref[...])
        mo_ref[...] = m_
        vo_ref[...] = v_

    spec = pl.BlockSpec((t, cols), lambda i: (i, 0))
    return pl.pallas_call(
        body, name=name, grid=(rows // t,), in_specs=[spec] * 4, out_specs=[spec] * 3,
        out_shape=[jax.ShapeDtypeStruct((rows, cols), F32)] * 3, compiler_params=_cparams(("parallel",)),
    )(w, g, mom, var)


ANY = pl.BlockSpec(memory_space=pl.ANY)


def _place():
    x, y, c = lax.axis_index("x"), lax.axis_index("y"), lax.axis_index("c")
    chips = [(1 - x, y), (x, 1 - y), (1 - x, 1 - y)]
    return x, y, c, chips


def _gather_chips(ws, *, name):
    n = len(ws)

    def body(*refs):
        ins, outs = refs[:n], refs[n:2 * n]
        send_sems, recv_sems, local_sems = refs[2 * n:]
        x, y, c, chips = _place()
        me = 2 * x + y
        sibling = (x, y, 1 - c)

        def remote(k, j, chip_index, rows, to):
            region = outs[k].at[chip_index, rows]
            return pltpu.make_async_remote_copy(
                src_ref=region, dst_ref=region, send_sem=send_sems.at[6 * k + j], recv_sem=recv_sems.at[6 * k + j],
                device_id=to, device_id_type=MESH)

        local, sent = [], []
        for k in range(n):
            half = ins[k].shape[0] // 2
            mine = pl.ds(c * half, half)
            local.append(pltpu.make_async_copy(ins[k], outs[k].at[me], local_sems.at[k]))
            local[-1].start()
            for j, chip in enumerate(chips):
                sent.append(pltpu.make_async_remote_copy(
                    src_ref=ins[k].at[mine], dst_ref=outs[k].at[me, mine], send_sem=send_sems.at[6 * k + j],
                    recv_sem=recv_sems.at[6 * k + j], device_id=(*chip, c), device_id_type=MESH))
                sent[-1].start()
        for k in range(n):
            half = ins[k].shape[0] // 2
            mine = pl.ds(c * half, half)
            for j, chip in enumerate(chips):
                remote(k, j, 2 * chip[0] + chip[1], mine, (*chip, c)).wait_recv()
                sent.append(remote(k, 3 + j, 2 * chip[0] + chip[1], mine, sibling))
                sent[-1].start()
        for k in range(n):
            half = ins[k].shape[0] // 2
            theirs = pl.ds((1 - c) * half, half)
            for j, chip in enumerate(chips):
                remote(k, 3 + j, 2 * chip[0] + chip[1], theirs, sibling).wait_recv()
        for cp in sent:
            cp.wait_send()
        for cp in local:
            cp.wait()

    return pl.pallas_call(
        body, name=name, in_specs=[ANY] * n, out_specs=[ANY] * n,
        out_shape=[jax.ShapeDtypeStruct((N_CHIPS,) + w.shape, w.dtype) for w in ws],
        scratch_shapes=[pltpu.SemaphoreType.DMA((6 * n,)), pltpu.SemaphoreType.DMA((6 * n,)), pltpu.SemaphoreType.DMA((n,))],
    )(*ws)


def _swap_sibling(arrs, *, name):
    n = len(arrs)

    def body(*refs):
        ins, outs = refs[:n], refs[n:2 * n]
        send_sems, recv_sems = refs[2 * n:]
        x, y, c, _ = _place()
        cps = [pltpu.make_async_remote_copy(src_ref=ins[k], dst_ref=outs[k], send_sem=send_sems.at[k], recv_sem=recv_sems.at[k],
                                            device_id=(x, y, 1 - c), device_id_type=MESH) for k in range(n)]
        for cp in cps:
            cp.start()
        for cp in cps:
            cp.wait_recv()
        for cp in cps:
            cp.wait_send()

    return pl.pallas_call(
        body, name=name, in_specs=[ANY] * n, out_specs=[ANY] * n,
        out_shape=[jax.ShapeDtypeStruct(a.shape, a.dtype) for a in arrs],
        scratch_shapes=[pltpu.SemaphoreType.DMA((n,)), pltpu.SemaphoreType.DMA((n,))],
    )(*arrs)


def _scatter_chips(ps, *, name):
    n = len(ps)

    def body(*refs):
        ins, outs = refs[:n], refs[n:2 * n]
        send_sems, recv_sems = refs[2 * n:]
        x, y, c, chips = _place()
        cps = []
        for k in range(n):
            for j, chip in enumerate(chips):
                cps.append(pltpu.make_async_remote_copy(
                    src_ref=ins[k].at[2 * chip[0] + chip[1]], dst_ref=outs[k].at[j], send_sem=send_sems.at[3 * k + j],
                    recv_sem=recv_sems.at[3 * k + j], device_id=(*chip, c), device_id_type=MESH))
        for cp in cps:
            cp.start()
        for cp in cps:
            cp.wait_recv()
        for cp in cps:
            cp.wait_send()

    return pl.pallas_call(
        body, name=name, in_specs=[ANY] * n, out_specs=[ANY] * n,
        out_shape=[jax.ShapeDtypeStruct((3,) + p.shape[1:], p.dtype) for p in ps],
        scratch_shapes=[pltpu.SemaphoreType.DMA((3 * n,)), pltpu.SemaphoreType.DMA((3 * n,))],
    )(*ps)


def _all_reduce_small(buf, *, name):
    rows = buf.shape[0]

    def body(x_ref, sum_ref, all_ref, send_sems, recv_sems, local_sem):
        x, y, c, chips = _place()
        me, sibling = (x, y, c), (x, y, 1 - c)

        def slab(px, py, pc):
            return all_ref.at[pl.ds((4 * px + 2 * py + pc) * rows, rows), :]

        def copy(k, block, to, src=None):
            return pltpu.make_async_remote_copy(
                src_ref=slab(*block) if src is None else src, dst_ref=slab(*block), send_sem=send_sems.at[k],
                recv_sem=recv_sems.at[k], device_id=to, device_id_type=MESH)

        mine = pltpu.make_async_copy(x_ref, slab(*me), local_sem)
        mine.start()
        first = [copy(0, me, sibling, src=x_ref)]
        first += [copy(1 + j, me, (*chip, c), src=x_ref) for j, chip in enumerate(chips)]
        for cp in first:
            cp.start()
        passed = [copy(4 + j, (*chip, c), sibling) for j, chip in enumerate(chips)]
        for j, chip in enumerate(chips):
            copy(1 + j, (*chip, c), me).wait_recv()
            passed[j].start()
        copy(0, sibling, me).wait_recv()
        for j, chip in enumerate(chips):
            copy(4 + j, (*chip, 1 - c), me).wait_recv()
        for cp in first + passed:
            cp.wait_send()
        mine.wait()
        acc = all_ref[0:rows, :]
        for d in range(1, N_DEV):
            acc = acc + all_ref[d * rows:(d + 1) * rows, :]
        sum_ref[...] = acc

    vmem = pl.BlockSpec(memory_space=pltpu.VMEM)
    return pl.pallas_call(
        body, name=name, in_specs=[vmem], out_specs=[vmem, vmem],
        out_shape=[jax.ShapeDtypeStruct((rows, 128), F32), jax.ShapeDtypeStruct((N_DEV * rows, 128), F32)],
        scratch_shapes=[pltpu.SemaphoreType.DMA((7,)), pltpu.SemaphoreType.DMA((7,)), pltpu.SemaphoreType.DMA],
        compiler_params=pltpu.CompilerParams(vmem_limit_bytes=VMEM_BIG),
    )(buf)[0]


WEIGHTS = ['ev_w_in', 'ev_lambda_re', 'ev_lambda_im', 'ev_log_dt', 'ev_b_re', 'ev_b_im', 'ev_c_re', 'ev_c_im', 'ev_d',
           'ev_w_glu', 'ev_b_glu', 'ev_conv_w', 'ev_w_out', 'od_w_in', 'od_rel_bias', 'od_pool_w', 'od_pool_scale',
           'od_w_out', 'ln_mix_g', 'ln_mix_b', 'ln_ffn_g', 'ln_ffn_b', 'ffn_w_up', 'ffn_w_down', 'ple_w_proj',
           'ple_w_gate', 'ple_b_gate']
INPUTS = ['x', 'p'] + WEIGHTS + ['loss_target'] + ['m_' + n for n in WEIGHTS] + ['v_' + n for n in WEIGHTS]

BIG = {
    'ev_w_in': (2, (2, 1024, 2048)), 'ev_w_glu': (1, (2, 512, 512)), 'ev_w_out': (1, (2, 1024, 1024)),
    'od_w_in': (2, (2, 1024, 2048)), 'od_w_out': (1, (2, 1024, 1024)), 'ffn_w_up': (2, (4, 1024, 5632)),
    'ffn_w_down': (1, (4, 2816, 1024)), 'ple_w_proj': (2, (4, 256, 1024)), 'ple_w_gate': (1, (4, 1024, 1024)),
}
SMALL_SHARDED = {'ev_conv_w': (2, 3, 512), 'od_pool_scale': (2, 512)}
REPLICATED = [n for n in WEIGHTS if n not in BIG and n not in SMALL_SHARDED]


def _shard_rows(name):
    axis, (nl, k, n) = BIG[name]
    return (nl * k, n // N_CHIPS) if axis == 2 else (nl * k // N_CHIPS, n)


def _unstack(name, st):
    axis, (nl, k, n) = BIG[name]
    if axis == 2:
        return st.reshape(N_CHIPS, nl, k, n // N_CHIPS).transpose(1, 2, 0, 3).reshape(nl, k, n)
    return st.reshape(N_CHIPS, nl, k // N_CHIPS, n).transpose(1, 0, 2, 3).reshape(nl, k, n)


def _stack(name, full):
    axis, (nl, k, n) = BIG[name]
    rows, cols = _shard_rows(name)
    if axis == 2:
        return full.reshape(nl, k, N_CHIPS, n // N_CHIPS).transpose(2, 0, 1, 3).reshape(N_CHIPS, rows, cols)
    return full.reshape(nl, N_CHIPS, k // N_CHIPS, n).transpose(1, 0, 2, 3).reshape(N_CHIPS, rows, cols)


def _pack(arrs):
    flat = jnp.concatenate([a.reshape(-1) for a in arrs])
    total = flat.shape[0]
    padded = -(-total // 1024) * 1024
    return jnp.pad(flat, (0, padded - total)).reshape(padded // 128, 128)


def _unpack(buf, shapes):
    flat = buf.reshape(-1)
    out, pos = [], 0
    for s in shapes:
        size = int(np.prod(s))
        out.append(flat[pos:pos + size].reshape(s))
        pos += size
    return out


def _s5_params(lam_re, lam_im, log_dt, b_re, b_im, c_re, c_im):
    dt = jnp.exp(log_dt)[:, None]
    mag = jnp.exp(lam_re * dt)
    ang = lam_im * dt
    lb_re = mag * jnp.cos(ang)
    lb_im = mag * jnp.sin(ang)
    den = lam_re * lam_re + lam_im * lam_im
    nr = lb_re - 1.0
    ni = lb_im
    r_re = (nr * lam_re + ni * lam_im) / den
    r_im = (ni * lam_re - nr * lam_im) / den
    bb_re = r_re[..., None] * b_re - r_im[..., None] * b_im
    bb_im = r_re[..., None] * b_im + r_im[..., None] * b_re
    eye = jnp.eye(S5_GROUPS, dtype=F32)

    def block_diag(a):
        g, r, c = a.shape
        return (a[:, :, None, :] * eye[:, None, :, None]).reshape(g * r, g * c)

    bmat = jnp.concatenate([block_diag(bb_re.transpose(0, 2, 1)), block_diag(bb_im.transpose(0, 2, 1))], axis=1)
    cmat = jnp.concatenate([block_diag(c_re.transpose(0, 2, 1)), block_diag(-c_im.transpose(0, 2, 1))], axis=0)
    lam = jnp.stack([lb_re.reshape(S5_N), lb_im.reshape(S5_N)])
    return lam, bmat, cmat


def _lam_powers(lam):
    res, ims = [lam[0]], [lam[1]]
    for _ in range(7):
        res, ims = res + [res[-1] * lam[0] - ims[-1] * lam[1]], ims + [res[-1] * lam[1] + ims[-1] * lam[0]]
    return jnp.stack(res + ims + res[::-1] + ims[::-1])


def _local_step(x, p, target, w):
    mask = jnp.asarray(_band_mask())
    diag_idx = _diag_index()
    onehot = jnp.asarray(np.eye(2 * MAX_REL + 1, dtype=np.float32)[diag_idx])
    saved = []
    for i in range(DEPTH):
        li = i // 2
        s = {'x0': x}
        if i % 2 == 0:
            (lam, bmat, cmat), s5_vjp = jax.vjp(
                _s5_params, w['ev_lambda_re'][li], w['ev_lambda_im'][li], w['ev_log_dt'][li], w['ev_b_re'][li],
                w['ev_b_im'][li], w['ev_c_re'][li], w['ev_c_im'][li])
            s5c = (bmat.astype(BF16), cmat.astype(BF16), w['ev_d'][li].reshape(1, MIX), w['ev_w_glu'][li],
                   w['ev_b_glu'][li].reshape(1, MIX), _lam_powers(lam))
            h = _mm([(x, 0, D_MODEL)], w['ev_w_in'][li], name=f"in_proj")
            ya, ypre, hb = _s5_fwd(h, *s5c, name=f"s5_fwd")
            yb = _conv_fwd(h, w['ev_conv_w'][li], name=f"conv_fwd")
            wout = w['ev_w_out'][li]
            s.update(s5_vjp=s5_vjp, s5c=s5c, ypre=ypre, hb=hb)
        else:
            vdiag = jnp.dot(w['od_rel_bias'][li], onehot.T, precision=HIGHEST).reshape(ATT_HEADS // 2, 2, NKEY)
            pw = w['od_pool_w'][li].astype(BF16)
            ps = w['od_pool_scale'][li].reshape(1, MIX)
            h = _mm([(x, 0, D_MODEL)], w['od_w_in'][li], name=f"in_proj")
            ya = _attn_fwd(h, vdiag, mask, name=f"attn_fwd")
            yb = _pool_fwd(h, pw, ps, name=f"pool_fwd")
            wout = w['od_w_out'][li]
            s.update(vdiag=vdiag, pw=pw, ps=ps)
        mix = _mm([(ya, 0, MIX), (yb, 0, MIX)], wout, name=f"out_proj")
        r1, x1 = _ln_fwd(x, mix, w['ln_mix_g'][i].reshape(1, -1), w['ln_mix_b'][i].reshape(1, -1), name=f"ln_mix")
        a, gg, uu = _ffn_up(x1, w['ffn_w_up'][i], name=f"ffn_up")
        f = _mm([(a, 0, D_FF)], w['ffn_w_down'][i], name=f"ffn_down")
        r2, x2 = _ln_fwd(x1, f, w['ln_ffn_g'][i].reshape(1, -1), w['ln_ffn_b'][i].reshape(1, -1), name=f"ln_ffn")
        zg = _mm([(x2, 0, D_MODEL)], w['ple_w_gate'][i], name=f"ple_gate")
        pp = _mm([(p[i], 0, D_PLE)], w['ple_w_proj'][i], name=f"ple_proj")
        x3, gate, ppb = _ple_fwd(x2, zg, pp, w['ple_b_gate'][i].reshape(1, -1), name=f"ple")
        s.update(h=h, ya=ya, yb=yb, wout=wout, r1=r1, x1=x1, a=a, gg=gg, uu=uu, r2=r2, x2=x2, gate=gate, ppb=ppb)
        saved.append(s)
        x = x3

    loss, da = _loss_head(x, target, name="loss_head")
    db = None
    grads = {n: [None] * (DEPTH if n.startswith(('ln_', 'ffn_', 'ple_')) else DEPTH // 2) for n in WEIGHTS}
    for i in reversed(range(DEPTH)):
        li = i // 2
        s = saved[i]
        dx3, dz, dpp, dbg = _ple_bwd(da, db, s['gate'], s['ppb'], name=f"ple_bwd")
        grads['ple_b_gate'][i] = dbg.reshape(-1)
        grads['ple_w_gate'][i] = _mm_tn(s['x2'], 0, D_MODEL, dz, name=f"d_ple_gate")
        grads['ple_w_proj'][i] = _mm_tn(p[i], 0, D_PLE, dpp, name=f"d_ple_proj")
        dx2 = _mm([(dz, 0, D_MODEL)], w['ple_w_gate'][i], trans_b=True, name=f"ple_gate_bwd")
        dr2, dg2, db2 = _ln_bwd(s['r2'], dx3, dx2, 1.0, w['ln_ffn_g'][i].reshape(1, -1), name=f"ln_ffn_bwd")
        grads['ln_ffn_g'][i] = dg2.reshape(-1)
        grads['ln_ffn_b'][i] = db2.reshape(-1)
        dhh = _ffn_down_bwd(dr2, w['ffn_w_down'][i], s['gg'], s['uu'], name=f"ffn_down_bwd")
        grads['ffn_w_down'][i] = _mm_tn(s['a'], 0, D_FF, dr2, tk=D_FF // 2, name=f"d_ffn_down")
        grads['ffn_w_up'][i] = _mm_tn(s['x1'], 0, D_MODEL, dhh, tn=D_FF // 2, name=f"d_ffn_up")
        dx1 = _mm([(dhh, 0, 2 * D_FF)], w['ffn_w_up'][i], trans_b=True, tm=256, vmem=VMEM_BIG, name=f"ffn_up_bwd")
        dr1, dg1, db1 = _ln_bwd(s['r1'], dr2, dx1, ALPHA, w['ln_mix_g'][i].reshape(1, -1), name=f"ln_mix_bwd")
        grads['ln_mix_g'][i] = dg1.reshape(-1)
        grads['ln_mix_b'][i] = db1.reshape(-1)
        dcat = _mm([(dr1, 0, D_MODEL)], s['wout'], trans_b=True, name=f"out_proj_bwd")
        dwout = jnp.concatenate([_mm_tn(s['ya'], 0, MIX, dr1, name=f"d_out_a"),
                                 _mm_tn(s['yb'], 0, MIX, dr1, name=f"d_out_b")], axis=0)
        h = s['h']
        if i % 2 == 0:
            s5c = s['s5c']
            du, xb, gb, gq, dzzq, dyq, dlam, dbglu, dd = _s5_bwd(dcat, s['ypre'], h, s['hb'], *s5c, name=f"s5_bwd")
            dbmat = _mm_tn(h, 0, MIX, gb, name=f"d_s5_b")
            dcmat = _mm_tn(xb, 0, 2 * S5_N, dyq, name=f"d_s5_c")
            s5g = s['s5_vjp']((dlam, dbmat, dcmat))
            for n, g_ in zip(['ev_lambda_re', 'ev_lambda_im', 'ev_log_dt', 'ev_b_re', 'ev_b_im', 'ev_c_re', 'ev_c_im'], s5g):
                grads[n][li] = g_
            grads['ev_w_glu'][li] = _mm_tn(gq, 0, MIX, dzzq, name=f"d_glu")
            grads['ev_b_glu'][li] = dbglu.reshape(-1)
            grads['ev_d'][li] = dd.reshape(-1)
            d3, dcw = _conv_bwd(dcat, h, w['ev_conv_w'][li], name=f"conv_bwd")
            grads['ev_conv_w'][li] = dcw[0:3]
            grads['ev_w_out'][li] = dwout
            grads['ev_w_in'][li] = jnp.concatenate(
                [_mm_tn(s['x0'], 0, D_MODEL, du, name=f"d_in_a"), _mm_tn(s['x0'], 0, D_MODEL, d3, tn=3 * MIX, name=f"d_in_b")],
                axis=1)
            db = _mm([(du, 0, MIX), (d3, 0, 3 * MIX)], w['ev_w_in'][li], trans_b=True, name=f"in_proj_bwd")
        else:
            dq, dk, dkp, dv, dvp, dvd = _attn_bwd(dcat, h, s['vdiag'], mask, name=f"attn_bwd")
            dzp, dpw, dps = _pool_bwd(dcat, h, s['pw'], s['ps'], name=f"pool_bwd")
            zeros = jnp.zeros((T_ATT, MIX), F32)
            dh = jnp.concatenate([dq, dk + jnp.concatenate([dkp[T_ATT:], zeros], axis=0),
                                  dv + jnp.concatenate([dvp[T_ATT:], zeros], axis=0), dzp], axis=1)
            grads['od_rel_bias'][li] = jnp.dot(dvd.reshape(ATT_HEADS, NKEY), onehot, precision=HIGHEST)
            grads['od_pool_w'][li] = dpw
            grads['od_pool_scale'][li] = dps.reshape(-1)
            grads['od_w_out'][li] = dwout
            grads['od_w_in'][li] = _mm_tn(s['x0'], 0, D_MODEL, dh, name=f"d_in")
            db = _mm([(dh, 0, 4 * MIX)], w['od_w_in'][li], trans_b=True, name=f"in_proj_bwd")
        da = dr1
    grad_x = _add_n([da, db], [ALPHA, 1.0], name="grad_x")
    return loss, grad_x, {n: jnp.stack(g) for n, g in grads.items()}


def _reduce_big(grads, c, me):
    names = list(BIG)
    mine, other = [], []
    for n in names:
        st = _stack(n, grads[n])
        half = st.shape[1] // 2
        mine.append(lax.dynamic_slice_in_dim(st, c * half, half, axis=1))
        other.append(lax.dynamic_slice_in_dim(st, (1 - c) * half, half, axis=1))
    from_sibling = _swap_sibling(other, name="grad_pair_swap")
    chip_sums, to_send = [], []
    for n, a, b in zip(names, mine, from_sibling):
        k, half, cols = a.shape
        s32, s16 = _add_n([a.reshape(k * half, cols), b.reshape(k * half, cols)], [1.0, 1.0], also_bf16=True,
                          name=f"grad_pair_add_{n}")
        chip_sums.append(s32.reshape(k, half, cols))
        to_send.append(s16.reshape(k, half, cols))
    from_chips = _scatter_chips(to_send, name="grad_chip_scatter")
    halves = []
    for n, own, got in zip(names, chip_sums, from_chips):
        halves.append(_add_n([lax.dynamic_index_in_dim(own, me, 0, keepdims=False), got[0], got[1], got[2]], [1.0] * 4,
                             name=f"grad_chip_add_{n}"))
    from_sibling = _swap_sibling(halves, name="grad_half_swap")
    out = {}
    for n, a, b in zip(names, halves, from_sibling):
        out[n] = jnp.where(c == 0, jnp.concatenate([a, b], axis=0), jnp.concatenate([b, a], axis=0))
    return out


def kernel(x, p, ev_w_in, ev_lambda_re, ev_lambda_im, ev_log_dt, ev_b_re, ev_b_im, ev_c_re, ev_c_im, ev_d, ev_w_glu, ev_b_glu, ev_conv_w, ev_w_out, od_w_in, od_rel_bias, od_pool_w, od_pool_scale, od_w_out, ln_mix_g, ln_mix_b, ln_ffn_g, ln_ffn_b, ffn_w_up, ffn_w_down, ple_w_proj, ple_w_gate, ple_b_gate, loss_target, m_ev_w_in, m_ev_lambda_re, m_ev_lambda_im, m_ev_log_dt, m_ev_b_re, m_ev_b_im, m_ev_c_re, m_ev_c_im, m_ev_d, m_ev_w_glu, m_ev_b_glu, m_ev_conv_w, m_ev_w_out, m_od_w_in, m_od_rel_bias, m_od_pool_w, m_od_pool_scale, m_od_w_out, m_ln_mix_g, m_ln_mix_b, m_ln_ffn_g, m_ln_ffn_b, m_ffn_w_up, m_ffn_w_down, m_ple_w_proj, m_ple_w_gate, m_ple_b_gate, v_ev_w_in, v_ev_lambda_re, v_ev_lambda_im, v_ev_log_dt, v_ev_b_re, v_ev_b_im, v_ev_c_re, v_ev_c_im, v_ev_d, v_ev_w_glu, v_ev_b_glu, v_ev_conv_w, v_ev_w_out, v_od_w_in, v_od_rel_bias, v_od_pool_w, v_od_pool_scale, v_od_w_out, v_ln_mix_g, v_ln_mix_b, v_ln_ffn_g, v_ln_ffn_b, v_ffn_w_up, v_ffn_w_down, v_ple_w_proj, v_ple_w_gate, v_ple_b_gate):
    given = locals()
    a = {n: given[n] for n in INPUTS}
    x, y, c = lax.axis_index("x"), lax.axis_index("y"), lax.axis_index("c")
    me = 2 * x + y

    misc = jnp.concatenate([a['ev_conv_w'].reshape(6, 128), a['od_pool_scale'], jnp.zeros((8, 128), F32)], axis=0)
    gathered = _gather_chips([a[n].astype(BF16).reshape(_shard_rows(n)) for n in BIG] + [misc], name="weight_gather")
    w = {n: _unstack(n, g) for n, g in zip(BIG, gathered)}
    gm = gathered[-1]
    w['ev_conv_w'] = gm[:, 0:6].reshape(N_CHIPS, 2, 3, 128).transpose(1, 2, 0, 3).reshape(2, 3, 512)
    w['od_pool_scale'] = gm[:, 6:8].transpose(1, 0, 2).reshape(2, 512)
    for n in REPLICATED:
        w[n] = a[n]

    loss, grad_x, grads = _local_step(a['x'][0], a['p'][:, 0], a['loss_target'][0], w)
    loss = lax.psum(loss[0, 0], ("x", "y", "c"))

    small_names = REPLICATED + list(SMALL_SHARDED)
    small = _all_reduce_small(_pack([grads[n] for n in small_names]), name="small_grad_all_reduce")
    small = dict(zip(small_names, _unpack(small, [grads[n].shape for n in small_names])))
    for n in SMALL_SHARDED:
        small[n] = lax.dynamic_slice_in_dim(small[n], me * 128, 128, axis=small[n].ndim - 1)
    big = _reduce_big(grads, c, me)

    res = {}
    for n in BIG:
        shape = a[n].shape
        d, m_, v_ = _adamw(a[n].reshape(big[n].shape), big[n], a['m_' + n].reshape(big[n].shape),
                           a['v_' + n].reshape(big[n].shape), name=f"adamw_{n}")
        res[n] = (big[n].reshape(shape), d.reshape(shape), m_.reshape(shape), v_.reshape(shape))
    shapes = [a[n].shape for n in small_names]
    d, m_, v_ = _adamw(_pack([a[n] for n in small_names]), _pack([small[n] for n in small_names]),
                       _pack([a['m_' + n] for n in small_names]), _pack([a['v_' + n] for n in small_names]), name="adamw_small")
    for n, dd, mm, vv in zip(small_names, _unpack(d, shapes), _unpack(m_, shapes), _unpack(v_, shapes)):
        res[n] = (small[n], dd, mm, vv)

    outs = [loss, grad_x[None]]
    for part in range(4):
        outs += [res[n][part] for n in WEIGHTS]
    return tuple(outs)
```

```python
import functools
import math

import jax
import jax.numpy as jnp
import numpy as np
from jax import lax
from jax.experimental import pallas as pl
from jax.experimental.pallas import tpu as pltpu

F32 = jnp.float32
BF16 = jnp.bfloat16
MESH = pl.DeviceIdType.MESH
HIGHEST = lax.Precision.HIGHEST

D_MODEL = 1024
DEPTH = 4
MIX = 512
S5_GROUPS = 32
S5_GROUP = 16
S5_STATE = 64
S5_N = S5_GROUPS * S5_STATE
CHUNK = 64
LEFT_CHUNKS = 8
MAX_REL = 128
ATT_HEADS = 8
HEAD_DIM = 64
POOL_WINDOWS = (2, 4, 8, 16)
POOL_GROUP = 128
D_FF = 2816
D_PLE = 256
ALPHA = (2 * DEPTH) ** 0.25
LN_EPS = 1e-5
NEG_INF = -1e30
N_CHIPS = 4
N_DEV = 8

ADAM_LR = 0.001
ADAM_B1 = 0.9
ADAM_B2 = 0.999
ADAM_EPS = 1e-08
ADAM_WD = 0.01
ADAM_STEP = 10

TM = 512
T_S5 = 256
T_ATT = 512
VMEM_BIG = 56 * 1024 * 1024


VMEM_DEFAULT = 48 * 1024 * 1024


def _cparams(sem, vmem=None):
    return pltpu.CompilerParams(dimension_semantics=sem, vmem_limit_bytes=vmem or VMEM_DEFAULT)


def _sigmoid(x):
    return 1.0 / (1.0 + jnp.exp(-x))


def _mm(a_parts, b, *, name, trans_b=False, out_dtype=F32, tm=TM, tn=1024, vmem=None):
    m = a_parts[0][0].shape[0]
    n = b.shape[0] if trans_b else b.shape[1]
    kk = b.shape[1] if trans_b else b.shape[0]
    tn = min(tn, n)
    widths = [w for _, _, w in a_parts]
    assert sum(widths) == kk and m % tm == 0 and n % tn == 0
    na = len(a_parts)

    def body(*refs):
        b_ref, o_ref = refs[na], refs[na + 1]
        acc = None
        k0 = 0
        for ar, w in zip(refs[:na], widths):
            a = ar[...].astype(BF16)
            if trans_b:
                part = lax.dot_general(a, b_ref[:, k0:k0 + w], (((1,), (1,)), ((), ())), preferred_element_type=F32)
            else:
                part = jnp.dot(a, b_ref[k0:k0 + w, :], preferred_element_type=F32)
            acc = part if acc is None else acc + part
            k0 += w
        o_ref[...] = acc.astype(o_ref.dtype)

    in_specs = [pl.BlockSpec((tm, w), functools.partial(lambda j, i, cb: (i, cb), cb=cb)) for _, cb, w in a_parts]
    if trans_b:
        in_specs.append(pl.BlockSpec((tn, kk), lambda j, i: (j, 0)))
    else:
        in_specs.append(pl.BlockSpec((kk, tn), lambda j, i: (0, j)))
    return pl.pallas_call(
        body, name=name, grid=(n // tn, m // tm), in_specs=in_specs,
        out_specs=pl.BlockSpec((tm, tn), lambda j, i: (i, j)),
        out_shape=jax.ShapeDtypeStruct((m, n), out_dtype),
        compiler_params=_cparams(("parallel", "parallel"), vmem),
    )(*[a for a, _, _ in a_parts], b)


def _mm_tn(a, a_cb, ka, b, *, name, tk=1024, tn=1024, tmr=2 * TM, vmem=None):
    m = a.shape[0]
    n = b.shape[1]
    tk = min(tk, ka)
    tn = min(tn, n)
    assert ka % tk == 0 and n % tn == 0 and m % tmr == 0
    kb = ka // tk

    def body(a_ref, b_ref, o_ref):
        @pl.when(pl.program_id(2) == 0)
        def _():
            o_ref[...] = jnp.zeros_like(o_ref)

        o_ref[...] += lax.dot_general(a_ref[...].astype(BF16), b_ref[...].astype(BF16), (((0,), (0,)), ((), ())),
                                      preferred_element_type=F32)

    return pl.pallas_call(
        body, name=name, grid=(kb, n // tn, m // tmr),
        in_specs=[pl.BlockSpec((tmr, tk), lambda k, j, r: (r, a_cb * kb + k)),
                  pl.BlockSpec((tmr, tn), lambda k, j, r: (r, j))],
        out_specs=pl.BlockSpec((tk, tn), lambda k, j, r: (k, j)),
        out_shape=jax.ShapeDtypeStruct((ka, n), F32),
        compiler_params=_cparams(("parallel", "parallel", "arbitrary"), vmem),
    )(a, b)


def _ln_stats(r):
    mu = jnp.mean(r, axis=-1, keepdims=True)
    xc = r - mu
    var = jnp.mean(xc * xc, axis=-1, keepdims=True)
    rstd = lax.rsqrt(var + LN_EPS)
    return xc * rstd, rstd


def _ln_apply(r, g, b):
    xhat, _ = _ln_stats(r)
    return xhat * g + b


def _ln_grad(r, dy, g):
    xhat, rstd = _ln_stats(r)
    dxh = dy * g
    m1 = jnp.mean(dxh, axis=-1, keepdims=True)
    m2 = jnp.mean(dxh * xhat, axis=-1, keepdims=True)
    return (rstd * (dxh - m1 - xhat * m2), jnp.sum(dy * xhat, axis=0, keepdims=True), jnp.sum(dy, axis=0, keepdims=True))


def _mm_rows(matmuls, rows_in, vecs_in, out_rows, acc_widths, fn, *, name, tm=TM, vmem=None):
    m = rows_in[0].shape[0]
    assert m % tm == 0
    flat, in_specs, layout = [], [], []
    for a_parts, b, trans_b in matmuls:
        for arr, cb, w in a_parts:
            flat.append(arr)
            in_specs.append(pl.BlockSpec((tm, w), functools.partial(lambda i, cb: (i, cb), cb=cb)))
        flat.append(b)
        in_specs.append(pl.BlockSpec(b.shape, lambda i: (0, 0)))
        layout.append(([w for _, _, w in a_parts], trans_b))
    for r in rows_in:
        flat.append(r)
        in_specs.append(pl.BlockSpec((tm, r.shape[1]), lambda i: (i, 0)))
    for v in vecs_in:
        flat.append(v)
        in_specs.append(pl.BlockSpec(v.shape, lambda i: (0, 0)))
    n_in = len(flat)
    n_rows_out = len(out_rows)

    def body(*refs):
        pos = 0
        products = []
        for widths, trans_b in layout:
            b_ref = refs[pos + len(widths)]
            acc, k0 = None, 0
            for ar, w in zip(refs[pos:pos + len(widths)], widths):
                a = ar[...].astype(BF16)
                if trans_b:
                    part = lax.dot_general(a, b_ref[:, k0:k0 + w], (((1,), (1,)), ((), ())), preferred_element_type=F32)
                else:
                    part = jnp.dot(a, b_ref[k0:k0 + w, :], preferred_element_type=F32)
                acc = part if acc is None else acc + part
                k0 += w
            products.append(acc)
            pos += len(widths) + 1
        rows = [r[...] for r in refs[pos:pos + len(rows_in)]]
        pos += len(rows_in)
        vecs = [v[...] for v in refs[pos:n_in]]
        outs, sums = fn(products, rows, vecs)
        for o_ref, o in zip(refs[n_in:n_in + n_rows_out], outs):
            o_ref[...] = o.astype(o_ref.dtype)
        if acc_widths:
            acc_refs = refs[n_in + n_rows_out:]

            @pl.when(pl.program_id(0) == 0)
            def _():
                for a_ref in acc_refs:
                    a_ref[...] = jnp.zeros_like(a_ref)

            for a_ref, s_ in zip(acc_refs, sums):
                a_ref[...] += s_

    out_specs = [pl.BlockSpec((tm, n), lambda i: (i, 0)) for n, _ in out_rows]
    out_specs += [pl.BlockSpec((1, wd), lambda i: (0, 0)) for wd in acc_widths]
    out_shape = [jax.ShapeDtypeStruct((m, n), dt) for n, dt in out_rows]
    out_shape += [jax.ShapeDtypeStruct((1, wd), F32) for wd in acc_widths]
    return pl.pallas_call(
        body, name=name, grid=(m // tm,), in_specs=in_specs, out_specs=out_specs, out_shape=out_shape,
        compiler_params=_cparams(("arbitrary",) if acc_widths else ("parallel",), vmem),
    )(*flat)


def _ffn_up(x1, wup, *, name):
    m = x1.shape[0]
    tn = D_FF // 2

    def body(x_ref, wg_ref, wu_ref, a_ref, g_ref, u_ref):
        x = x_ref[...].astype(BF16)
        g = jnp.dot(x, wg_ref[...], preferred_element_type=F32)
        u = jnp.dot(x, wu_ref[...], preferred_element_type=F32)
        a_ref[...] = (g * _sigmoid(g) * u).astype(BF16)
        g_ref[...] = g.astype(BF16)
        u_ref[...] = u.astype(BF16)

    out = pl.BlockSpec((TM, tn), lambda j, i: (i, j))
    return pl.pallas_call(
        body, name=name, grid=(2, m // TM),
        in_specs=[pl.BlockSpec((TM, D_MODEL), lambda j, i: (i, 0)),
                  pl.BlockSpec((D_MODEL, tn), lambda j, i: (0, j)),
                  pl.BlockSpec((D_MODEL, tn), lambda j, i: (0, j + 2))],
        out_specs=[out, out, out], out_shape=[jax.ShapeDtypeStruct((m, D_FF), BF16)] * 3,
        compiler_params=_cparams(("parallel", "parallel")),
    )(x1, wup, wup)


def _ffn_down_bwd(df, wdown, g, u, *, name):
    m = df.shape[0]
    tm = 256

    def body(df_ref, w_ref, g_ref, u_ref, o_ref):
        da = lax.dot_general(df_ref[...].astype(BF16), w_ref[...], (((1,), (1,)), ((), ())), preferred_element_type=F32)
        gg = g_ref[...].astype(F32)
        sg = _sigmoid(gg)
        o_ref[:, :D_FF] = (da * u_ref[...].astype(F32) * (sg * (1.0 + gg * (1.0 - sg)))).astype(BF16)
        o_ref[:, D_FF:] = (da * (gg * sg)).astype(BF16)

    return pl.pallas_call(
        body, name=name, grid=(m // tm,),
        in_specs=[pl.BlockSpec((tm, D_MODEL), lambda i: (i, 0)), pl.BlockSpec((D_FF, D_MODEL), lambda i: (0, 0)),
                  pl.BlockSpec((tm, D_FF), lambda i: (i, 0)), pl.BlockSpec((tm, D_FF), lambda i: (i, 0))],
        out_specs=pl.BlockSpec((tm, 2 * D_FF), lambda i: (i, 0)),
        out_shape=jax.ShapeDtypeStruct((m, 2 * D_FF), BF16),
        compiler_params=_cparams(("parallel",), VMEM_BIG),
    )(df, wdown, g, u)


def _ple_ln_bwd(da, db, gate, pp, r2, wgate, g2, *, name):
    m, n = da.shape
    two = db is not None
    n_in = 7 if two else 6

    def body(*refs):
        if two:
            da_ref, db_ref, gate_ref, pp_ref, r_ref, w_ref, g_ref = refs[:n_in]
            dx3 = ALPHA * da_ref[...] + db_ref[...]
        else:
            da_ref, gate_ref, pp_ref, r_ref, w_ref, g_ref = refs[:n_in]
            dx3 = da_ref[...]
        dz_ref, dpp_ref, dr_ref, dbg_ref, dg_ref, dbias_ref = refs[n_in:]

        @pl.when(pl.program_id(0) == 0)
        def _():
            dbg_ref[...] = jnp.zeros_like(dbg_ref)
            dg_ref[...] = jnp.zeros_like(dg_ref)
            dbias_ref[...] = jnp.zeros_like(dbias_ref)

        gate = gate_ref[...].astype(F32)
        dz = dx3 * pp_ref[...].astype(F32) * gate * (1.0 - gate)
        dzq = dz.astype(BF16)
        dz_ref[...] = dzq
        dpp_ref[...] = (dx3 * gate).astype(BF16)
        dbg_ref[...] += jnp.sum(dz, axis=0, keepdims=True)
        dx2 = dx3 + lax.dot_general(dzq, w_ref[...], (((1,), (1,)), ((), ())), preferred_element_type=F32)
        dr, dg, dbias = _ln_grad(r_ref[...], dx2, g_ref[...])
        dr_ref[...] = dr
        dg_ref[...] += dg
        dbias_ref[...] += dbias

    row = pl.BlockSpec((TM, n), lambda i: (i, 0))
    vec = pl.BlockSpec((1, n), lambda i: (0, 0))
    ins = ([da, db] if two else [da]) + [gate, pp, r2, wgate, g2]
    in_specs = [row] * (n_in - 2) + [pl.BlockSpec(wgate.shape, lambda i: (0, 0)), vec]
    return pl.pallas_call(
        body, name=name, grid=(m // TM,), in_specs=in_specs, out_specs=[row, row, row, vec, vec, vec],
        out_shape=[jax.ShapeDtypeStruct((m, n), BF16), jax.ShapeDtypeStruct((m, n), BF16), jax.ShapeDtypeStruct((m, n), F32)]
        + [jax.ShapeDtypeStruct((1, n), F32)] * 3,
        compiler_params=_cparams(("arbitrary",)),
    )(*ins)


def _loss_head(y, target, *, name):
    m, n = y.shape

    def body(y_ref, t_ref, loss_ref, dy_ref):
        @pl.when(pl.program_id(0) == 0)
        def _():
            loss_ref[...] = jnp.zeros_like(loss_ref)

        err = y_ref[...] - t_ref[...]
        dy_ref[...] = err * (1.0 / n)
        per_tok = jnp.mean(err * err, axis=-1, keepdims=True)
        loss_ref[...] += 0.5 * jnp.sum(per_tok, axis=0, keepdims=True)

    row = pl.BlockSpec((TM, n), lambda i: (i, 0))
    return pl.pallas_call(
        body, name=name, grid=(m // TM,), in_specs=[row, row],
        out_specs=[pl.BlockSpec((1, 1), lambda i: (0, 0)), row],
        out_shape=[jax.ShapeDtypeStruct((1, 1), F32), jax.ShapeDtypeStruct((m, n), F32)],
        compiler_params=_cparams(("arbitrary",)),
    )(y, target)


def _gelu(y):
    c = math.sqrt(2.0 / math.pi)
    return 0.5 * y * (1.0 + jnp.tanh(c * (y + 0.044715 * y * y * y)))


def _gelu_grad(y):
    c = math.sqrt(2.0 / math.pi)
    t = jnp.tanh(c * (y + 0.044715 * y * y * y))
    return 0.5 * (1.0 + t) + 0.5 * y * (1.0 - t * t) * c * (1.0 + 3.0 * 0.044715 * y * y)


STRIP = 256


def _scan_strip(xr, xi, cr, ci, ptab_ref, off, rowmod, down):
    t = xr.shape[0]
    base = 0 if down else 16
    p_r = ptab_ref[base:base + 8, pl.ds(off, STRIP)]
    p_i = ptab_ref[base + 8:base + 16, pl.ds(off, STRIP)]
    if not down:
        p_i = -p_i
    for k in range(3):
        s = 1 << k
        idx = s - 1 if down else 8 - s
        pr, pi_ = p_r[idx:idx + 1], p_i[idx:idx + 1]
        if down:
            sr = jnp.where(rowmod >= s, pltpu.roll(xr, s, 0), 0.0)
            si = jnp.where(rowmod >= s, pltpu.roll(xi, s, 0), 0.0)
        else:
            sr = jnp.where(rowmod < 8 - s, pltpu.roll(xr, t - s, 0), 0.0)
            si = jnp.where(rowmod < 8 - s, pltpu.roll(xi, t - s, 0), 0.0)
        xr, xi = xr + pr * sr - pi_ * si, xi + pr * si + pi_ * sr
    ng = t // 8
    out_r, out_i = [None] * ng, [None] * ng
    for g in (range(ng) if down else reversed(range(ng))):
        cbr = jnp.broadcast_to(cr, (8, STRIP))
        cbi = jnp.broadcast_to(ci, (8, STRIP))
        br = xr[8 * g:8 * g + 8] + p_r * cbr - p_i * cbi
        bi = xi[8 * g:8 * g + 8] + p_r * cbi + p_i * cbr
        cr, ci = (br[7:8], bi[7:8]) if down else (br[0:1], bi[0:1])
        out_r[g], out_i[g] = br, bi
    return jnp.concatenate(out_r, axis=0), jnp.concatenate(out_i, axis=0)


def _s5_fwd(h, bmat, cmat, dvec, wglu, bglu, ptab, *, name):
    m = h.shape[0]
    t = T_S5
    nb = m // t

    def body(u_ref, bmat_ref, cmat_ref, d_ref, wglu_ref, bglu_ref, ptab_ref,
             out_ref, y_ref, hb_ref, bu_ref, carry_ref):
        @pl.when(pl.program_id(0) == 0)
        def _():
            carry_ref[...] = jnp.zeros_like(carry_ref)

        hb_ref[0] = carry_ref[...]
        u = u_ref[...]
        bu_ref[...] = jnp.dot(u.astype(BF16), bmat_ref[...], preferred_element_type=F32)
        rowmod = lax.broadcasted_iota(jnp.int32, (t, STRIP), 0) & 7

        def strip(j, c):
            off = pl.multiple_of(j * STRIP, STRIP)
            offi = pl.multiple_of(S5_N + j * STRIP, STRIP)
            xr, xi = _scan_strip(bu_ref[:, pl.ds(off, STRIP)], bu_ref[:, pl.ds(offi, STRIP)],
                                 carry_ref[0:1, pl.ds(off, STRIP)], carry_ref[0:1, pl.ds(offi, STRIP)],
                                 ptab_ref, off, rowmod, True)
            bu_ref[:, pl.ds(off, STRIP)] = xr
            bu_ref[:, pl.ds(offi, STRIP)] = xi
            carry_ref[0:1, pl.ds(off, STRIP)] = xr[t - 1:t, :]
            carry_ref[0:1, pl.ds(offi, STRIP)] = xi[t - 1:t, :]
            return c

        lax.fori_loop(0, S5_N // STRIP, strip, 0)
        y = jnp.dot(bu_ref[...].astype(BF16), cmat_ref[...], preferred_element_type=F32) + d_ref[...] * u
        y_ref[...] = y
        g = _gelu(y)
        zz = jnp.dot(g.astype(BF16), wglu_ref[...], preferred_element_type=F32) + bglu_ref[...]
        out_ref[...] = (g * _sigmoid(zz)).astype(BF16)

    const = lambda shape: pl.BlockSpec(shape, lambda i: (0,) * len(shape))
    row_spec = pl.BlockSpec((t, MIX), lambda i: (i, 0))
    return pl.pallas_call(
        body, name=name, grid=(nb,),
        in_specs=[row_spec, const((MIX, 2 * S5_N)), const((2 * S5_N, MIX)), const((1, MIX)), const((MIX, MIX)),
                  const((1, MIX)), const((32, S5_N))],
        out_specs=[row_spec, row_spec, pl.BlockSpec((1, 1, 2 * S5_N), lambda i: (i, 0, 0))],
        out_shape=[jax.ShapeDtypeStruct((m, MIX), BF16), jax.ShapeDtypeStruct((m, MIX), F32),
                   jax.ShapeDtypeStruct((nb, 1, 2 * S5_N), F32)],
        scratch_shapes=[pltpu.VMEM((t, 2 * S5_N), F32), pltpu.VMEM((1, 2 * S5_N), F32)],
        compiler_params=_cparams(("arbitrary",), VMEM_BIG),
    )(h, bmat, cmat, dvec, wglu, bglu, ptab)


def _s5_bwd(dcat, ypre, h, hb, bmat, cmat, dvec, wglu, bglu, ptab, *, name):
    m = h.shape[0]
    t = T_S5
    nb = m // t

    def body(dya_ref, y_ref, u_ref, hb_ref, bmat_ref, cmat_ref, d_ref, wglu_ref, bglu_ref, ptab_ref,
             du_ref, xb_ref, gb_ref, gq_ref, dzz_ref, dyq_ref, dlam_ref, dbglu_ref, dd_ref,
             bu_ref, dx_ref, gcarry_ref):
        @pl.when(pl.program_id(0) == 0)
        def _():
            gcarry_ref[...] = jnp.zeros_like(gcarry_ref)
            dlam_ref[...] = jnp.zeros_like(dlam_ref)
            dbglu_ref[...] = jnp.zeros_like(dbglu_ref)
            dd_ref[...] = jnp.zeros_like(dd_ref)

        u = u_ref[...]
        y = y_ref[...]
        g = _gelu(y)
        gq = g.astype(BF16)
        sg = _sigmoid(jnp.dot(gq, wglu_ref[...], preferred_element_type=F32) + bglu_ref[...])
        dout = dya_ref[...]
        dzz = dout * g * sg * (1.0 - sg)
        dzzq = dzz.astype(BF16)
        dg = dout * sg + lax.dot_general(dzzq, wglu_ref[...], (((1,), (1,)), ((), ())), preferred_element_type=F32)
        dy = dg * _gelu_grad(y)
        dyq = dy.astype(BF16)
        gq_ref[...] = gq
        dzz_ref[...] = dzzq
        dyq_ref[...] = dyq
        dbglu_ref[...] += jnp.sum(dzz, axis=0, keepdims=True)
        dd_ref[...] += jnp.sum(dy * u, axis=0, keepdims=True)

        dx_ref[...] = lax.dot_general(dyq, cmat_ref[...], (((1,), (1,)), ((), ())), preferred_element_type=F32)
        bu_ref[...] = jnp.dot(u.astype(BF16), bmat_ref[...], preferred_element_type=F32)
        row = lax.broadcasted_iota(jnp.int32, (t, STRIP), 0)
        rowmod = row & 7

        def strip(j, c):
            off = pl.multiple_of(j * STRIP, STRIP)
            offi = pl.multiple_of(S5_N + j * STRIP, STRIP)
            hr = hb_ref[0, 0:1, pl.ds(off, STRIP)]
            hi = hb_ref[0, 0:1, pl.ds(offi, STRIP)]
            xr, xi = _scan_strip(bu_ref[:, pl.ds(off, STRIP)], bu_ref[:, pl.ds(offi, STRIP)], hr, hi,
                                 ptab_ref, off, rowmod, True)
            xb_ref[:, pl.ds(off, STRIP)] = xr.astype(BF16)
            xb_ref[:, pl.ds(offi, STRIP)] = xi.astype(BF16)
            pr_ = jnp.where(row == 0, hr, pltpu.roll(xr, 1, 0))
            pi_ = jnp.where(row == 0, hi, pltpu.roll(xi, 1, 0))

            gr, gi = _scan_strip(dx_ref[:, pl.ds(off, STRIP)], dx_ref[:, pl.ds(offi, STRIP)],
                                 gcarry_ref[0:1, pl.ds(off, STRIP)], gcarry_ref[0:1, pl.ds(offi, STRIP)],
                                 ptab_ref, off, rowmod, False)
            gb_ref[:, pl.ds(off, STRIP)] = gr.astype(BF16)
            gb_ref[:, pl.ds(offi, STRIP)] = gi.astype(BF16)
            gcarry_ref[0:1, pl.ds(off, STRIP)] = gr[0:1, :]
            gcarry_ref[0:1, pl.ds(offi, STRIP)] = gi[0:1, :]
            dlam_ref[0:1, pl.ds(off, STRIP)] += jnp.sum(pr_ * gr + pi_ * gi, axis=0, keepdims=True)
            dlam_ref[1:2, pl.ds(off, STRIP)] += jnp.sum(pr_ * gi - pi_ * gr, axis=0, keepdims=True)
            return c

        lax.fori_loop(0, S5_N // STRIP, strip, 0)
        du_ref[...] = dy * d_ref[...] + lax.dot_general(gb_ref[...], bmat_ref[...], (((1,), (1,)), ((), ())),
                                                        preferred_element_type=F32)

    const = lambda shape: pl.BlockSpec(shape, lambda i: (0,) * len(shape))
    rev = lambda i: (nb - 1 - i, 0)
    row_spec = pl.BlockSpec((t, MIX), rev)
    wide = pl.BlockSpec((t, 2 * S5_N), rev)
    return pl.pallas_call(
        body, name=name, grid=(nb,),
        in_specs=[row_spec, row_spec, row_spec, pl.BlockSpec((1, 1, 2 * S5_N), lambda i: (nb - 1 - i, 0, 0)),
                  const((MIX, 2 * S5_N)), const((2 * S5_N, MIX)), const((1, MIX)), const((MIX, MIX)), const((1, MIX)),
                  const((32, S5_N))],
        out_specs=[row_spec, wide, wide, row_spec, row_spec, row_spec, const((2, S5_N)), const((1, MIX)), const((1, MIX))],
        out_shape=[jax.ShapeDtypeStruct((m, MIX), F32), jax.ShapeDtypeStruct((m, 2 * S5_N), BF16),
                   jax.ShapeDtypeStruct((m, 2 * S5_N), BF16), jax.ShapeDtypeStruct((m, MIX), BF16),
                   jax.ShapeDtypeStruct((m, MIX), BF16), jax.ShapeDtypeStruct((m, MIX), BF16),
                   jax.ShapeDtypeStruct((2, S5_N), F32), jax.ShapeDtypeStruct((1, MIX), F32), jax.ShapeDtypeStruct((1, MIX), F32)],
        scratch_shapes=[pltpu.VMEM((t, 2 * S5_N), F32), pltpu.VMEM((t, 2 * S5_N), F32), pltpu.VMEM((1, 2 * S5_N), F32)],
        compiler_params=_cparams(("arbitrary",), VMEM_BIG),
    )(dcat, ypre, h, hb, bmat, cmat, dvec, wglu, bglu, ptab)


HALO = 8


def _taps_down(zext, t):
    return pltpu.roll(zext, 1, 0)[HALO:HALO + t], pltpu.roll(zext, 2, 0)[HALO:HALO + t]


def _conv_z(c_ref, x_ref, cp_ref, xp_ref, first, t):
    z = c_ref[...] * x_ref[...]
    zp = jnp.where(first, 0.0, cp_ref[t - HALO:t, :] * xp_ref[t - HALO:t, :])
    z1, z2 = _taps_down(jnp.concatenate([zp, z], axis=0), t)
    return z, z1, z2


def _conv_fwd(h, cw, *, name):
    m = h.shape[0]
    t = TM
    nb = m // t

    def body(b_ref, c_ref, x_ref, cp_ref, xp_ref, w_ref, o_ref):
        z, z1, z2 = _conv_z(c_ref, x_ref, cp_ref, xp_ref, pl.program_id(0) == 0, t)
        o_ref[...] = (b_ref[...] * (w_ref[0:1, :] * z2 + w_ref[1:2, :] * z1 + w_ref[2:3, :] * z)).astype(BF16)

    cur = lambda cb: pl.BlockSpec((t, MIX), lambda i: (i, cb))
    prev = lambda cb: pl.BlockSpec((t, MIX), lambda i: (jnp.maximum(i - 1, 0), cb))
    return pl.pallas_call(
        body, name=name, grid=(nb,),
        in_specs=[cur(1), cur(2), cur(3), prev(2), prev(3), pl.BlockSpec((3, MIX), lambda i: (0, 0))],
        out_specs=pl.BlockSpec((t, MIX), lambda i: (i, 0)),
        out_shape=jax.ShapeDtypeStruct((m, MIX), BF16),
        compiler_params=_cparams(("parallel",)),
    )(h, h, h, h, h, cw)


def _conv_bwd(dcat, h, cw, *, name):
    m = h.shape[0]
    t = TM
    nb = m // t

    def body(dy_ref, dyn_ref, b_ref, c_ref, x_ref, cp_ref, xp_ref, bn_ref, w_ref, o_ref, dw_ref):
        i = pl.program_id(0)

        @pl.when(i == 0)
        def _():
            dw_ref[...] = jnp.zeros_like(dw_ref)

        z, z1, z2 = _conv_z(c_ref, x_ref, cp_ref, xp_ref, i == 0, t)
        w0, w1, w2 = w_ref[0:1, :], w_ref[1:2, :], w_ref[2:3, :]
        dy = dy_ref[...]
        dconv = dy * b_ref[...]
        dnext = jnp.where(i == nb - 1, 0.0, dyn_ref[0:HALO, :] * bn_ref[0:HALO, :])
        dext = jnp.concatenate([dconv, dnext], axis=0)
        d1 = pltpu.roll(dext, t + HALO - 1, 0)[0:t]
        d2 = pltpu.roll(dext, t + HALO - 2, 0)[0:t]
        dz = w2 * dconv + w1 * d1 + w0 * d2
        o_ref[:, 0:MIX] = dy * (w0 * z2 + w1 * z1 + w2 * z)
        o_ref[:, MIX:2 * MIX] = dz * x_ref[...]
        o_ref[:, 2 * MIX:3 * MIX] = dz * c_ref[...]
        dw_ref[0:1, :] += jnp.sum(dconv * z2, axis=0, keepdims=True)
        dw_ref[1:2, :] += jnp.sum(dconv * z1, axis=0, keepdims=True)
        dw_ref[2:3, :] += jnp.sum(dconv * z, axis=0, keepdims=True)

    cur = lambda cb: pl.BlockSpec((t, MIX), lambda i: (i, cb))
    prev = lambda cb: pl.BlockSpec((t, MIX), lambda i: (jnp.maximum(i - 1, 0), cb))
    nxt = lambda cb: pl.BlockSpec((t, MIX), lambda i: (jnp.minimum(i + 1, nb - 1), cb))
    return pl.pallas_call(
        body, name=name, grid=(nb,),
        in_specs=[cur(1), nxt(1), cur(1), cur(2), cur(3), prev(2), prev(3), nxt(1), pl.BlockSpec((3, MIX), lambda i: (0, 0))],
        out_specs=[pl.BlockSpec((t, 3 * MIX), lambda i: (i, 0)), pl.BlockSpec((8, MIX), lambda i: (0, 0))],
        out_shape=[jax.ShapeDtypeStruct((m, 3 * MIX), F32), jax.ShapeDtypeStruct((8, MIX), F32)],
        compiler_params=_cparams(("arbitrary",)),
    )(dcat, dcat, h, h, h, h, h, h, cw)


PHALO = 16


def _pool_pooled(z_ref, zp_ref, i, t):
    z = z_ref[...]
    zp = jnp.where(i == 0, 0.0, zp_ref[t - PHALO:t, :])
    s = jnp.concatenate([zp, z], axis=0)
    sums = {}
    width = 1
    while width < PHALO:
        s = s + pltpu.roll(s, width, 0)
        width *= 2
        sums[width] = s[PHALO:PHALO + t]
    tpos = i * t + lax.broadcasted_iota(jnp.int32, (t, 1), 0)
    outs = []
    for gi, w in enumerate(POOL_WINDOWS):
        lo = gi * POOL_GROUP
        count = jnp.minimum(tpos + 1, w).astype(F32)
        outs.append(sums[w][:, lo:lo + POOL_GROUP] / count - z[:, lo:lo + POOL_GROUP])
    return outs


def _pool_fwd(h, pw, ps, *, name):
    m = h.shape[0]
    t = TM
    nb = m // t

    def body(z_ref, zp_ref, pw_ref, ps_ref, o_ref):
        pooled = _pool_pooled(z_ref, zp_ref, pl.program_id(0), t)
        for gi in range(len(POOL_WINDOWS)):
            lo = gi * POOL_GROUP
            mixed = jnp.dot(pooled[gi].astype(BF16), pw_ref[gi], preferred_element_type=F32)
            o_ref[:, lo:lo + POOL_GROUP] = (mixed * ps_ref[:, lo:lo + POOL_GROUP]).astype(BF16)

    return pl.pallas_call(
        body, name=name, grid=(nb,),
        in_specs=[pl.BlockSpec((t, MIX), lambda i: (i, 3)), pl.BlockSpec((t, MIX), lambda i: (jnp.maximum(i - 1, 0), 3)),
                  pl.BlockSpec((4, POOL_GROUP, POOL_GROUP), lambda i: (0, 0, 0)), pl.BlockSpec((1, MIX), lambda i: (0, 0))],
        out_specs=pl.BlockSpec((t, MIX), lambda i: (i, 0)),
        out_shape=jax.ShapeDtypeStruct((m, MIX), BF16),
        compiler_params=_cparams(("parallel",)),
    )(h, h, pw, ps)


def _pool_bwd(dcat, h, pw, ps, *, name):
    m = h.shape[0]
    t = TM
    nb = m // t

    def body(dy_ref, dyn_ref, z_ref, zp_ref, pw_ref, ps_ref, dz_ref, dpw_ref, dps_ref):
        i = pl.program_id(0)

        @pl.when(i == 0)
        def _():
            dpw_ref[...] = jnp.zeros_like(dpw_ref)
            dps_ref[...] = jnp.zeros_like(dps_ref)

        pooled = _pool_pooled(z_ref, zp_ref, i, t)
        dy = dy_ref[...]
        tpos = i * t + lax.broadcasted_iota(jnp.int32, (t, 1), 0)
        for gi, w in enumerate(POOL_WINDOWS):
            lo = gi * POOL_GROUP
            sl = slice(lo, lo + POOL_GROUP)
            pq = pooled[gi].astype(BF16)
            mixed = jnp.dot(pq, pw_ref[gi], preferred_element_type=F32)
            dps_ref[:, sl] += jnp.sum(dy[:, sl] * mixed, axis=0, keepdims=True)
            dmix = (dy[:, sl] * ps_ref[:, sl]).astype(BF16)
            dpw_ref[gi] += lax.dot_general(pq, dmix, (((0,), (0,)), ((), ())), preferred_element_type=F32)
            dpool = lax.dot_general(dmix, pw_ref[gi], (((1,), (1,)), ((), ())), preferred_element_type=F32)
            dmix_n = (dyn_ref[0:PHALO, sl] * ps_ref[:, sl]).astype(BF16)
            dpool_n = lax.dot_general(dmix_n, pw_ref[gi], (((1,), (1,)), ((), ())), preferred_element_type=F32)
            e = dpool / jnp.minimum(tpos + 1, w).astype(F32)
            e_n = jnp.where(i == nb - 1, 0.0, dpool_n * (1.0 / w))
            f = jnp.concatenate([e, e_n], axis=0)
            width = 1
            while width < w:
                f = f + pltpu.roll(f, t + PHALO - width, 0)
                width *= 2
            dz_ref[:, sl] = f[0:t] - dpool

    return pl.pallas_call(
        body, name=name, grid=(nb,),
        in_specs=[pl.BlockSpec((t, MIX), lambda i: (i, 1)), pl.BlockSpec((t, MIX), lambda i: (jnp.minimum(i + 1, nb - 1), 1)),
                  pl.BlockSpec((t, MIX), lambda i: (i, 3)), pl.BlockSpec((t, MIX), lambda i: (jnp.maximum(i - 1, 0), 3)),
                  pl.BlockSpec((4, POOL_GROUP, POOL_GROUP), lambda i: (0, 0, 0)), pl.BlockSpec((1, MIX), lambda i: (0, 0))],
        out_specs=[pl.BlockSpec((t, MIX), lambda i: (i, 0)), pl.BlockSpec((4, POOL_GROUP, POOL_GROUP), lambda i: (0, 0, 0)),
                   pl.BlockSpec((1, MIX), lambda i: (0, 0))],
        out_shape=[jax.ShapeDtypeStruct((m, MIX), F32), jax.ShapeDtypeStruct((4, POOL_GROUP, POOL_GROUP), F32),
                   jax.ShapeDtypeStruct((1, MIX), F32)],
        compiler_params=_cparams(("arbitrary",)),
    )(dcat, dcat, h, h, pw, ps)


NKEY = 2 * T_ATT


def _band_mask():
    qc = np.arange(T_ATT)[:, None] // CHUNK
    kc = np.arange(NKEY)[None, :] // CHUNK - LEFT_CHUNKS
    return np.where((kc <= qc) & (kc >= qc - LEFT_CHUNKS), 0.0, NEG_INF).astype(np.float32)


def _diag_index():
    c = np.arange(NKEY)
    d = np.where(c <= NKEY // 2 + CHUNK, T_ATT - c, T_ATT + NKEY - c)
    return np.clip(d, -MAX_REL, MAX_REL) + MAX_REL


def _bias_tile(vd_ref, mask_ref, tile_ref):
    col = lax.broadcasted_iota(jnp.int32, (8, NKEY), 1)
    no_prev = jnp.where(col < T_ATT, NEG_INF, 0.0)
    for hh in range(2):
        v = vd_ref[0, hh:hh + 1, :]
        base = jnp.concatenate([v if s == 0 else pltpu.roll(v, s, 1) for s in range(8)], axis=0)
        for mrow in range(T_ATT // 8):
            rows = slice(8 * mrow, 8 * mrow + 8)
            blk = (base if mrow == 0 else pltpu.roll(base, 8 * mrow, 1)) + mask_ref[rows, :]
            tile_ref[hh, rows, :] = blk
            tile_ref[2 + hh, rows, :] = blk + no_prev


def _attn_probs(q, kc, tile):
    s = lax.dot_general(q, kc, (((1,), (1,)), ((), ())), preferred_element_type=F32) + tile
    p = jnp.exp(s - jnp.max(s, axis=-1, keepdims=True))
    return p * (1.0 / jnp.sum(p, axis=-1, keepdims=True))


def _attn_specs(block):
    cur = lambda base: pl.BlockSpec((T_ATT, 128), lambda hp, i: (block(i), base + hp))
    prev = lambda base: pl.BlockSpec((T_ATT, 128), lambda hp, i: (jnp.maximum(block(i) - 1, 0), base + hp))
    return [cur(0), cur(4), prev(4), cur(8), prev(8),
            pl.BlockSpec((1, 2, NKEY), lambda hp, i: (hp, 0, 0)), pl.BlockSpec((T_ATT, NKEY), lambda hp, i: (0, 0))]


def _attn_fwd(h, vdiag, mask, *, name):
    m = h.shape[0]
    nb = m // T_ATT

    def body(q_ref, k_ref, kp_ref, v_ref, vp_ref, vd_ref, mask_ref, o_ref, tile_ref):
        i = pl.program_id(1)

        @pl.when(i == 0)
        def _():
            _bias_tile(vd_ref, mask_ref, tile_ref)

        first = jnp.where(i == 0, 2, 0)
        outs = []
        for hh in range(2):
            sl = slice(hh * HEAD_DIM, (hh + 1) * HEAD_DIM)
            q = (q_ref[:, sl] * (HEAD_DIM ** -0.5)).astype(BF16)
            kc = jnp.concatenate([kp_ref[:, sl], k_ref[:, sl]], axis=0).astype(BF16)
            vc = jnp.concatenate([vp_ref[:, sl], v_ref[:, sl]], axis=0).astype(BF16)
            p = _attn_probs(q, kc, tile_ref[first + hh])
            outs.append(jnp.dot(p.astype(BF16), vc, preferred_element_type=F32))
        o_ref[...] = jnp.concatenate(outs, axis=1).astype(BF16)

    return pl.pallas_call(
        body, name=name, grid=(ATT_HEADS // 2, nb), in_specs=_attn_specs(lambda i: i),
        out_specs=pl.BlockSpec((T_ATT, 128), lambda hp, i: (i, hp)),
        out_shape=jax.ShapeDtypeStruct((m, MIX), BF16),
        scratch_shapes=[pltpu.VMEM((4, T_ATT, NKEY), F32)],
        compiler_params=_cparams(("parallel", "arbitrary"), VMEM_BIG),
    )(h, h, h, h, h, vdiag, mask)


def _attn_bwd(dcat, h, vdiag, mask, *, name):
    m = h.shape[0]
    nb = m // T_ATT

    def body(do_ref, q_ref, k_ref, kp_ref, v_ref, vp_ref, vd_ref, mask_ref,
             dq_ref, dk_ref, dv_ref, dvd_ref, tile_ref, acc_ref, carry_ref):
        i = pl.program_id(1)

        @pl.when(i == 0)
        def _():
            _bias_tile(vd_ref, mask_ref, tile_ref)
            acc_ref[...] = jnp.zeros_like(acc_ref)

            carry_ref[...] = jnp.zeros_like(carry_ref)

        scale = HEAD_DIM ** -0.5
        first = jnp.where(i == nb - 1, 2, 0)
        dqs, dks, dvs = [], [], []
        for hh in range(2):
            sl = slice(hh * HEAD_DIM, (hh + 1) * HEAD_DIM)
            q = (q_ref[:, sl] * scale).astype(BF16)
            kc = jnp.concatenate([kp_ref[:, sl], k_ref[:, sl]], axis=0).astype(BF16)
            vc = jnp.concatenate([vp_ref[:, sl], v_ref[:, sl]], axis=0).astype(BF16)
            do = do_ref[:, sl].astype(BF16)
            p = _attn_probs(q, kc, tile_ref[first + hh])
            dvs.append(lax.dot_general(p.astype(BF16), do, (((0,), (0,)), ((), ())), preferred_element_type=F32))
            dp = lax.dot_general(do, vc, (((1,), (1,)), ((), ())), preferred_element_type=F32)
            ds = p * (dp - jnp.sum(dp * p, axis=-1, keepdims=True))
            acc_ref[hh] += ds
            dsq = ds.astype(BF16)
            dqs.append(jnp.dot(dsq, kc, preferred_element_type=F32) * scale)
            dks.append(lax.dot_general(dsq, q, (((0,), (0,)), ((), ())), preferred_element_type=F32))
        dq_ref[...] = jnp.concatenate(dqs, axis=1)
        dk = jnp.concatenate(dks, axis=1)
        dv = jnp.concatenate(dvs, axis=1)
        dk_ref[...] = dk[T_ATT:] + carry_ref[0]
        dv_ref[...] = dv[T_ATT:] + carry_ref[1]
        carry_ref[0] = dk[:T_ATT]
        carry_ref[1] = dv[:T_ATT]

        @pl.when(i == nb - 1)
        def _():
            for hh in range(2):
                r8 = acc_ref[hh, 0:8, :]
                for mrow in range(1, T_ATT // 8):
                    r8 = r8 + pltpu.roll(acc_ref[hh, 8 * mrow:8 * mrow + 8, :], NKEY - 8 * mrow, 1)
                tot = r8[0:1, :]
                for s in range(1, 8):
                    tot = tot + pltpu.roll(r8[s:s + 1, :], NKEY - s, 1)
                dvd_ref[0, hh:hh + 1, :] = tot

    block = lambda i: nb - 1 - i
    out = pl.BlockSpec((T_ATT, 128), lambda hp, i: (block(i), hp))
    return pl.pallas_call(
        body, name=name, grid=(ATT_HEADS // 2, nb),
        in_specs=[out] + _attn_specs(block),
        out_specs=[out, out, out, pl.BlockSpec((1, 2, NKEY), lambda hp, i: (hp, 0, 0))],
        out_shape=[jax.ShapeDtypeStruct((m, MIX), F32)] * 3 + [jax.ShapeDtypeStruct((ATT_HEADS // 2, 2, NKEY), F32)],
        scratch_shapes=[pltpu.VMEM((4, T_ATT, NKEY), F32), pltpu.VMEM((2, T_ATT, NKEY), F32), pltpu.VMEM((2, T_ATT, 128), F32)],
        compiler_params=_cparams(("parallel", "arbitrary"), VMEM_BIG),
    )(dcat, h, h, h, h, h, vdiag, mask)


def _row_tile(rows):
    for t in (512, 256, 128, 64, 32, 16, 8):
        if rows % t == 0:
            return t
    return rows


def _add_n(arrs, coefs, *, name, also_bf16=False):
    rows, cols = arrs[0].shape
    t = _row_tile(rows)
    n = len(arrs)

    def body(*refs):
        acc = None
        for r, cf in zip(refs[:n], coefs):
            v = r[...].astype(F32)
            v = v if cf == 1.0 else cf * v
            acc = v if acc is None else acc + v
        refs[n][...] = acc
        if also_bf16:
            refs[n + 1][...] = acc.astype(BF16)

    spec = pl.BlockSpec((t, cols), lambda i: (i, 0))
    f32 = jax.ShapeDtypeStruct((rows, cols), F32)
    return pl.pallas_call(
        body, name=name, grid=(rows // t,), in_specs=[spec] * n,
        out_specs=[spec, spec] if also_bf16 else spec,
        out_shape=[f32, jax.ShapeDtypeStruct((rows, cols), BF16)] if also_bf16 else f32,
        compiler_params=_cparams(("parallel",)),
    )(*arrs)


def _adamw(w, g, mom, var, *, name):
    rows, cols = w.shape
    t = _row_tile(rows)

    def body(w_ref, g_ref, m_ref, v_ref, d_ref, mo_ref, vo_ref):
        g_ = g_ref[...]
        m_ = ADAM_B1 * m_ref[...] + (1.0 - ADAM_B1) * g_
        v_ = ADAM_B2 * v_ref[...] + (1.0 - ADAM_B2) * (g_ * g_)
        m_hat = m_ / (1.0 - ADAM_B1 ** ADAM_STEP)
        v_hat = v_ / (1.0 - ADAM_B2 ** ADAM_STEP)
        d_ref[...] = -ADAM_LR * (m_hat / (jnp.sqrt(v_hat) + ADAM_EPS) + ADAM_WD * w_ref[...])
        mo_ref[...] = m_
        vo_ref[...] = v_

    spec = pl.BlockSpec((t, cols), lambda i: (i, 0))
    return pl.pallas_call(
        body, name=name, grid=(rows // t,), in_specs=[spec] * 4, out_specs=[spec] * 3,
        out_shape=[jax.ShapeDtypeStruct((rows, cols), F32)] * 3, compiler_params=_cparams(("parallel",)),
    )(w, g, mom, var)


ANY = pl.BlockSpec(memory_space=pl.ANY)


def _place():
    x, y, c = lax.axis_index("x"), lax.axis_index("y"), lax.axis_index("c")
    chips = [(1 - x, y), (x, 1 - y), (1 - x, 1 - y)]
    return x, y, c, chips


def _gather_chips(ws, *, name):
    n = len(ws)

    def body(*refs):
        ins, outs = refs[:n], refs[n:2 * n]
        send_sems, recv_sems, local_sems = refs[2 * n:]
        x, y, c, chips = _place()
        me = 2 * x + y
        sibling = (x, y, 1 - c)

        def remote(k, j, chip_index, rows, to):
            region = outs[k].at[chip_index, rows]
            return pltpu.make_async_remote_copy(
                src_ref=region, dst_ref=region, send_sem=send_sems.at[6 * k + j], recv_sem=recv_sems.at[6 * k + j],
                device_id=to, device_id_type=MESH)

        local, sent = [], []
        for k in range(n):
            half = ins[k].shape[0] // 2
            mine = pl.ds(c * half, half)
            local.append(pltpu.make_async_copy(ins[k], outs[k].at[me], local_sems.at[k]))
            local[-1].start()
            for j, chip in enumerate(chips):
                sent.append(pltpu.make_async_remote_copy(
                    src_ref=ins[k].at[mine], dst_ref=outs[k].at[me, mine], send_sem=send_sems.at[6 * k + j],
                    recv_sem=recv_sems.at[6 * k + j], device_id=(*chip, c), device_id_type=MESH))
                sent[-1].start()
        for k in range(n):
            half = ins[k].shape[0] // 2
            mine = pl.ds(c * half, half)
            for j, chip in enumerate(chips):
                remote(k, j, 2 * chip[0] + chip[1], mine, (*chip, c)).wait_recv()
                sent.append(remote(k, 3 + j, 2 * chip[0] + chip[1], mine, sibling))
                sent[-1].start()
        for k in range(n):
            half = ins[k].shape[0] // 2
            theirs = pl.ds((1 - c) * half, half)
            for j, chip in enumerate(chips):
                remote(k, 3 + j, 2 * chip[0] + chip[1], theirs, sibling).wait_recv()
        for cp in sent:
            cp.wait_send()
        for cp in local:
            cp.wait()

    return pl.pallas_call(
        body, name=name, in_specs=[ANY] * n, out_specs=[ANY] * n,
        out_shape=[jax.ShapeDtypeStruct((N_CHIPS,) + w.shape, w.dtype) for w in ws],
        scratch_shapes=[pltpu.SemaphoreType.DMA((6 * n,)), pltpu.SemaphoreType.DMA((6 * n,)), pltpu.SemaphoreType.DMA((n,))],
    )(*ws)


def _swap_sibling(arrs, *, name):
    n = len(arrs)

    def body(*refs):
        ins, outs = refs[:n], refs[n:2 * n]
        send_sems, recv_sems = refs[2 * n:]
        x, y, c, _ = _place()
        cps = [pltpu.make_async_remote_copy(src_ref=ins[k], dst_ref=outs[k], send_sem=send_sems.at[k], recv_sem=recv_sems.at[k],
                                            device_id=(x, y, 1 - c), device_id_type=MESH) for k in range(n)]
        for cp in cps:
            cp.start()
        for cp in cps:
            cp.wait_recv()
        for cp in cps:
            cp.wait_send()

    return pl.pallas_call(
        body, name=name, in_specs=[ANY] * n, out_specs=[ANY] * n,
        out_shape=[jax.ShapeDtypeStruct(a.shape, a.dtype) for a in arrs],
        scratch_shapes=[pltpu.SemaphoreType.DMA((n,)), pltpu.SemaphoreType.DMA((n,))],
    )(*arrs)


def _scatter_chips(ps, *, name):
    n = len(ps)

    def body(*refs):
        ins, outs = refs[:n], refs[n:2 * n]
        send_sems, recv_sems = refs[2 * n:]
        x, y, c, chips = _place()
        cps = []
        for k in range(n):
            for j, chip in enumerate(chips):
                cps.append(pltpu.make_async_remote_copy(
                    src_ref=ins[k].at[2 * chip[0] + chip[1]], dst_ref=outs[k].at[j], send_sem=send_sems.at[3 * k + j],
                    recv_sem=recv_sems.at[3 * k + j], device_id=(*chip, c), device_id_type=MESH))
        for cp in cps:
            cp.start()
        for cp in cps:
            cp.wait_recv()
        for cp in cps:
            cp.wait_send()

    return pl.pallas_call(
        body, name=name, in_specs=[ANY] * n, out_specs=[ANY] * n,
        out_shape=[jax.ShapeDtypeStruct((3,) + p.shape[1:], p.dtype) for p in ps],
        scratch_shapes=[pltpu.SemaphoreType.DMA((3 * n,)), pltpu.SemaphoreType.DMA((3 * n,))],
    )(*ps)


def _all_reduce_small(buf, *, name):
    rows = buf.shape[0]

    def body(x_ref, sum_ref, all_ref, send_sems, recv_sems, local_sem):
        x, y, c, chips = _place()
        me, sibling = (x, y, c), (x, y, 1 - c)

        def slab(px, py, pc):
            return all_ref.at[pl.ds((4 * px + 2 * py + pc) * rows, rows), :]

        def copy(k, block, to, src=None):
            return pltpu.make_async_remote_copy(
                src_ref=slab(*block) if src is None else src, dst_ref=slab(*block), send_sem=send_sems.at[k],
                recv_sem=recv_sems.at[k], device_id=to, device_id_type=MESH)

        mine = pltpu.make_async_copy(x_ref, slab(*me), local_sem)
        mine.start()
        first = [copy(0, me, sibling, src=x_ref)]
        first += [copy(1 + j, me, (*chip, c), src=x_ref) for j, chip in enumerate(chips)]
        for cp in first:
            cp.start()
        passed = [copy(4 + j, (*chip, c), sibling) for j, chip in enumerate(chips)]
        for j, chip in enumerate(chips):
            copy(1 + j, (*chip, c), me).wait_recv()
            passed[j].start()
        copy(0, sibling, me).wait_recv()
        for j, chip in enumerate(chips):
            copy(4 + j, (*chip, 1 - c), me).wait_recv()
        for cp in first + passed:
            cp.wait_send()
        mine.wait()
        acc = all_ref[0:rows, :]
        for d in range(1, N_DEV):
            acc = acc + all_ref[d * rows:(d + 1) * rows, :]
        sum_ref[...] = acc

    vmem = pl.BlockSpec(memory_space=pltpu.VMEM)
    return pl.pallas_call(
        body, name=name, in_specs=[vmem], out_specs=[vmem, vmem],
        out_shape=[jax.ShapeDtypeStruct((rows, 128), F32), jax.ShapeDtypeStruct((N_DEV * rows, 128), F32)],
        scratch_shapes=[pltpu.SemaphoreType.DMA((7,)), pltpu.SemaphoreType.DMA((7,)), pltpu.SemaphoreType.DMA],
        compiler_params=pltpu.CompilerParams(vmem_limit_bytes=VMEM_BIG),
    )(buf)[0]


WEIGHTS = ['ev_w_in', 'ev_lambda_re', 'ev_lambda_im', 'ev_log_dt', 'ev_b_re', 'ev_b_im', 'ev_c_re', 'ev_c_im', 'ev_d',
           'ev_w_glu', 'ev_b_glu', 'ev_conv_w', 'ev_w_out', 'od_w_in', 'od_rel_bias', 'od_pool_w', 'od_pool_scale',
           'od_w_out', 'ln_mix_g', 'ln_mix_b', 'ln_ffn_g', 'ln_ffn_b', 'ffn_w_up', 'ffn_w_down', 'ple_w_proj',
           'ple_w_gate', 'ple_b_gate']
INPUTS = ['x', 'p'] + WEIGHTS + ['loss_target'] + ['m_' + n for n in WEIGHTS] + ['v_' + n for n in WEIGHTS]

BIG = {
    'ev_w_in': (2, (2, 1024, 2048)), 'ev_w_glu': (1, (2, 512, 512)), 'ev_w_out': (1, (2, 1024, 1024)),
    'od_w_in': (2, (2, 1024, 2048)), 'od_w_out': (1, (2, 1024, 1024)), 'ffn_w_up': (2, (4, 1024, 5632)),
    'ffn_w_down': (1, (4, 2816, 1024)), 'ple_w_proj': (2, (4, 256, 1024)), 'ple_w_gate': (1, (4, 1024, 1024)),
}
SMALL_SHARDED = {'ev_conv_w': (2, 3, 512), 'od_pool_scale': (2, 512)}
REPLICATED = [n for n in WEIGHTS if n not in BIG and n not in SMALL_SHARDED]


def _shard_rows(name):
    axis, (nl, k, n) = BIG[name]
    return (nl * k, n // N_CHIPS) if axis == 2 else (nl * k // N_CHIPS, n)


def _unstack(name, st):
    axis, (nl, k, n) = BIG[name]
    if axis == 2:
        return st.reshape(N_CHIPS, nl, k, n // N_CHIPS).transpose(1, 2, 0, 3).reshape(nl, k, n)
    return st.reshape(N_CHIPS, nl, k // N_CHIPS, n).transpose(1, 0, 2, 3).reshape(nl, k, n)


def _stack(name, full):
    axis, (nl, k, n) = BIG[name]
    rows, cols = _shard_rows(name)
    if axis == 2:
        return full.reshape(nl, k, N_CHIPS, n // N_CHIPS).transpose(2, 0, 1, 3).reshape(N_CHIPS, rows, cols)
    return full.reshape(nl, N_CHIPS, k // N_CHIPS, n).transpose(1, 0, 2, 3).reshape(N_CHIPS, rows, cols)


def _pack(arrs):
    flat = jnp.concatenate([a.reshape(-1) for a in arrs])
    total = flat.shape[0]
    padded = -(-total // 1024) * 1024
    return jnp.pad(flat, (0, padded - total)).reshape(padded // 128, 128)


def _unpack(buf, shapes):
    flat = buf.reshape(-1)
    out, pos = [], 0
    for s in shapes:
        size = int(np.prod(s))
        out.append(flat[pos:pos + size].reshape(s))
        pos += size
    return out


def _s5_params(lam_re, lam_im, log_dt, b_re, b_im, c_re, c_im):
    dt = jnp.exp(log_dt)[:, None]
    mag = jnp.exp(lam_re * dt)
    ang = lam_im * dt
    lb_re = mag * jnp.cos(ang)
    lb_im = mag * jnp.sin(ang)
    den = lam_re * lam_re + lam_im * lam_im
    nr = lb_re - 1.0
    ni = lb_im
    r_re = (nr * lam_re + ni * lam_im) / den
    r_im = (ni * lam_re - nr * lam_im) / den
    bb_re = r_re[..., None] * b_re - r_im[..., None] * b_im
    bb_im = r_re[..., None] * b_im + r_im[..., None] * b_re
    eye = jnp.eye(S5_GROUPS, dtype=F32)

    def block_diag(a):
        g, r, c = a.shape
        return (a[:, :, None, :] * eye[:, None, :, None]).reshape(g * r, g * c)

    bmat = jnp.concatenate([block_diag(bb_re.transpose(0, 2, 1)), block_diag(bb_im.transpose(0, 2, 1))], axis=1)
    cmat = jnp.concatenate([block_diag(c_re.transpose(0, 2, 1)), block_diag(-c_im.transpose(0, 2, 1))], axis=0)
    lam = jnp.stack([lb_re.reshape(S5_N), lb_im.reshape(S5_N)])
    return lam, bmat, cmat


def _lam_powers(lam):
    res, ims = [lam[0]], [lam[1]]
    for _ in range(7):
        res, ims = res + [res[-1] * lam[0] - ims[-1] * lam[1]], ims + [res[-1] * lam[1] + ims[-1] * lam[0]]
    return jnp.stack(res + ims + res[::-1] + ims[::-1])


def _local_step(x, p, target, w):
    mask = jnp.asarray(_band_mask())
    diag_idx = _diag_index()
    onehot = jnp.asarray(np.eye(2 * MAX_REL + 1, dtype=np.float32)[diag_idx])
    saved = []
    for i in range(DEPTH):
        li = i // 2
        s = {'x0': x}
        if i % 2 == 0:
            (lam, bmat, cmat), s5_vjp = jax.vjp(
                _s5_params, w['ev_lambda_re'][li], w['ev_lambda_im'][li], w['ev_log_dt'][li], w['ev_b_re'][li],
                w['ev_b_im'][li], w['ev_c_re'][li], w['ev_c_im'][li])
            s5c = (bmat.astype(BF16), cmat.astype(BF16), w['ev_d'][li].reshape(1, MIX), w['ev_w_glu'][li],
                   w['ev_b_glu'][li].reshape(1, MIX), _lam_powers(lam))
            h = _mm([(x, 0, D_MODEL)], w['ev_w_in'][li], name=f"in_proj")
            ya, ypre, hb = _s5_fwd(h, *s5c, name=f"s5_fwd")
            yb = _conv_fwd(h, w['ev_conv_w'][li], name=f"conv_fwd")
            wout = w['ev_w_out'][li]
            s.update(s5_vjp=s5_vjp, s5c=s5c, ypre=ypre, hb=hb)
        else:
            vdiag = jnp.dot(w['od_rel_bias'][li], onehot.T, precision=HIGHEST).reshape(ATT_HEADS // 2, 2, NKEY)
            pw = w['od_pool_w'][li].astype(BF16)
            ps = w['od_pool_scale'][li].reshape(1, MIX)
            h = _mm([(x, 0, D_MODEL)], w['od_w_in'][li], name=f"in_proj")
            ya = _attn_fwd(h, vdiag, mask, name=f"attn_fwd")
            yb = _pool_fwd(h, pw, ps, name=f"pool_fwd")
            wout = w['od_w_out'][li]
            s.update(vdiag=vdiag, pw=pw, ps=ps)
        vec = lambda n: w[n][i].reshape(1, -1)

        def residual_ln(products, rows, vecs):
            r = ALPHA * rows[0] + products[0]
            return (r, _ln_apply(r, vecs[0], vecs[1])), ()

        def embed_gate(products, rows, vecs):
            gate = _sigmoid(products[0] + vecs[0])
            return (rows[0] + gate * products[1], gate, products[1]), ()

        two_f32 = [(D_MODEL, F32), (D_MODEL, F32)]
        r1, x1 = _mm_rows([([(ya, 0, MIX), (yb, 0, MIX)], wout, False)], [x], [vec('ln_mix_g'), vec('ln_mix_b')],
                          two_f32, [], residual_ln, name="out_proj_ln")
        a, gg, uu = _ffn_up(x1, w['ffn_w_up'][i], name=f"ffn_up")
        r2, x2 = _mm_rows([([(a, 0, D_FF)], w['ffn_w_down'][i], False)], [x1], [vec('ln_ffn_g'), vec('ln_ffn_b')],
                          two_f32, [], residual_ln, name="ffn_down_ln")
        x3, gate, ppb = _mm_rows(
            [([(x2, 0, D_MODEL)], w['ple_w_gate'][i], False), ([(p[i], 0, D_PLE)], w['ple_w_proj'][i], False)],
            [x2], [vec('ple_b_gate')], [(D_MODEL, F32), (D_MODEL, BF16), (D_MODEL, BF16)], [], embed_gate, name="ple")
        s.update(h=h, ya=ya, yb=yb, wout=wout, r1=r1, x1=x1, a=a, gg=gg, uu=uu, r2=r2, x2=x2, gate=gate, ppb=ppb)
        saved.append(s)
        x = x3

    loss, da = _loss_head(x, target, name="loss_head")
    db = None
    grads = {n: [None] * (DEPTH if n.startswith(('ln_', 'ffn_', 'ple_')) else DEPTH // 2) for n in WEIGHTS}
    for i in reversed(range(DEPTH)):
        li = i // 2
        s = saved[i]
        dz, dpp, dr2, dbg, dg2, db2 = _ple_ln_bwd(da, db, s['gate'], s['ppb'], s['r2'], w['ple_w_gate'][i],
                                                  w['ln_ffn_g'][i].reshape(1, -1), name="ple_ln_bwd")
        grads['ple_b_gate'][i] = dbg.reshape(-1)
        grads['ple_w_gate'][i] = _mm_tn(s['x2'], 0, D_MODEL, dz, name=f"d_ple_gate")
        grads['ple_w_proj'][i] = _mm_tn(p[i], 0, D_PLE, dpp, name=f"d_ple_proj")
        grads['ln_ffn_g'][i] = dg2.reshape(-1)
        grads['ln_ffn_b'][i] = db2.reshape(-1)
        dhh = _ffn_down_bwd(dr2, w['ffn_w_down'][i], s['gg'], s['uu'], name=f"ffn_down_bwd")
        grads['ffn_w_down'][i] = _mm_tn(s['a'], 0, D_FF, dr2, tk=D_FF // 2, name=f"d_ffn_down")
        grads['ffn_w_up'][i] = _mm_tn(s['x1'], 0, D_MODEL, dhh, tn=D_FF // 2, name=f"d_ffn_up")

        def ln_mix_grad(products, rows, vecs):
            dr, dg, dbias = _ln_grad(rows[0], ALPHA * rows[1] + products[0], vecs[0])
            return (dr,), (dg, dbias)

        dr1, dg1, db1 = _mm_rows([([(dhh, 0, 2 * D_FF)], w['ffn_w_up'][i], True)], [s['r1'], dr2],
                                 [w['ln_mix_g'][i].reshape(1, -1)], [(D_MODEL, F32)], [D_MODEL, D_MODEL], ln_mix_grad,
                                 tm=256, vmem=VMEM_BIG, name="ffn_up_ln_bwd")
        grads['ln_mix_g'][i] = dg1.reshape(-1)
        grads['ln_mix_b'][i] = db1.reshape(-1)
        dcat = _mm([(dr1, 0, D_MODEL)], s['wout'], trans_b=True, name=f"out_proj_bwd")
        dwout = jnp.concatenate([_mm_tn(s['ya'], 0, MIX, dr1, name=f"d_out_a"),
                                 _mm_tn(s['yb'], 0, MIX, dr1, name=f"d_out_b")], axis=0)
        h = s['h']
        if i % 2 == 0:
            s5c = s['s5c']
            du, xb, gb, gq, dzzq, dyq, dlam, dbglu, dd = _s5_bwd(dcat, s['ypre'], h, s['hb'], *s5c, name=f"s5_bwd")
            dbmat = _mm_tn(h, 0, MIX, gb, name=f"d_s5_b")
            dcmat = _mm_tn(xb, 0, 2 * S5_N, dyq, name=f"d_s5_c")
            s5g = s['s5_vjp']((dlam, dbmat, dcmat))
            for n, g_ in zip(['ev_lambda_re', 'ev_lambda_im', 'ev_log_dt', 'ev_b_re', 'ev_b_im', 'ev_c_re', 'ev_c_im'], s5g):
                grads[n][li] = g_
            grads['ev_w_glu'][li] = _mm_tn(gq, 0, MIX, dzzq, name=f"d_glu")
            grads['ev_b_glu'][li] = dbglu.reshape(-1)
            grads['ev_d'][li] = dd.reshape(-1)
            d3, dcw = _conv_bwd(dcat, h, w['ev_conv_w'][li], name=f"conv_bwd")
            grads['ev_conv_w'][li] = dcw[0:3]
            grads['ev_w_out'][li] = dwout
            grads['ev_w_in'][li] = jnp.concatenate(
                [_mm_tn(s['x0'], 0, D_MODEL, du, name=f"d_in_a"), _mm_tn(s['x0'], 0, D_MODEL, d3, tn=3 * MIX, name=f"d_in_b")],
                axis=1)
            db = _mm([(du, 0, MIX), (d3, 0, 3 * MIX)], w['ev_w_in'][li], trans_b=True, name=f"in_proj_bwd")
        else:
            dq, dk, dv, dvd = _attn_bwd(dcat, h, s['vdiag'], mask, name=f"attn_bwd")
            dzp, dpw, dps = _pool_bwd(dcat, h, s['pw'], s['ps'], name=f"pool_bwd")
            parts = [dq, dk, dv, dzp]
            grads['od_rel_bias'][li] = jnp.dot(dvd.reshape(ATT_HEADS, NKEY), onehot, precision=HIGHEST)
            grads['od_pool_w'][li] = dpw
            grads['od_pool_scale'][li] = dps.reshape(-1)
            grads['od_w_out'][li] = dwout
            grads['od_w_in'][li] = jnp.concatenate([_mm_tn(s['x0'], 0, D_MODEL, d_, name=f"d_in_a") for d_ in parts], axis=1)
            db = _mm([(d_, 0, MIX) for d_ in parts], w['od_w_in'][li], trans_b=True, name=f"in_proj_bwd")
        da = dr1
    grad_x = _add_n([da, db], [ALPHA, 1.0], name="grad_x")
    return loss, grad_x, {n: jnp.stack(g) for n, g in grads.items()}


def _reduce_big(grads, c, me):
    names = list(BIG)
    mine, other = [], []
    for n in names:
        st = _stack(n, grads[n])
        half = st.shape[1] // 2
        mine.append(lax.dynamic_slice_in_dim(st, c * half, half, axis=1))
        other.append(lax.dynamic_slice_in_dim(st, (1 - c) * half, half, axis=1))
    from_sibling = _swap_sibling(other, name="grad_pair_swap")
    chip_sums, to_send = [], []
    for n, a, b in zip(names, mine, from_sibling):
        k, half, cols = a.shape
        s32, s16 = _add_n([a.reshape(k * half, cols), b.reshape(k * half, cols)], [1.0, 1.0], also_bf16=True,
                          name=f"grad_pair_add_{n}")
        chip_sums.append(s32.reshape(k, half, cols))
        to_send.append(s16.reshape(k, half, cols))
    from_chips = _scatter_chips(to_send, name="grad_chip_scatter")
    halves = []
    for n, own, got in zip(names, chip_sums, from_chips):
        halves.append(_add_n([lax.dynamic_index_in_dim(own, me, 0, keepdims=False), got[0], got[1], got[2]], [1.0] * 4,
                             name=f"grad_chip_add_{n}"))
    from_sibling = _swap_sibling(halves, name="grad_half_swap")
    out = {}
    for n, a, b in zip(names, halves, from_sibling):
        out[n] = jnp.where(c == 0, jnp.concatenate([a, b], axis=0), jnp.concatenate([b, a], axis=0))
    return out


def kernel(x, p, ev_w_in, ev_lambda_re, ev_lambda_im, ev_log_dt, ev_b_re, ev_b_im, ev_c_re, ev_c_im, ev_d, ev_w_glu, ev_b_glu, ev_conv_w, ev_w_out, od_w_in, od_rel_bias, od_pool_w, od_pool_scale, od_w_out, ln_mix_g, ln_mix_b, ln_ffn_g, ln_ffn_b, ffn_w_up, ffn_w_down, ple_w_proj, ple_w_gate, ple_b_gate, loss_target, m_ev_w_in, m_ev_lambda_re, m_ev_lambda_im, m_ev_log_dt, m_ev_b_re, m_ev_b_im, m_ev_c_re, m_ev_c_im, m_ev_d, m_ev_w_glu, m_ev_b_glu, m_ev_conv_w, m_ev_w_out, m_od_w_in, m_od_rel_bias, m_od_pool_w, m_od_pool_scale, m_od_w_out, m_ln_mix_g, m_ln_mix_b, m_ln_ffn_g, m_ln_ffn_b, m_ffn_w_up, m_ffn_w_down, m_ple_w_proj, m_ple_w_gate, m_ple_b_gate, v_ev_w_in, v_ev_lambda_re, v_ev_lambda_im, v_ev_log_dt, v_ev_b_re, v_ev_b_im, v_ev_c_re, v_ev_c_im, v_ev_d, v_ev_w_glu, v_ev_b_glu, v_ev_conv_w, v_ev_w_out, v_od_w_in, v_od_rel_bias, v_od_pool_w, v_od_pool_scale, v_od_w_out, v_ln_mix_g, v_ln_mix_b, v_ln_ffn_g, v_ln_ffn_b, v_ffn_w_up, v_ffn_w_down, v_ple_w_proj, v_ple_w_gate, v_ple_b_gate):
    given = locals()
    a = {n: given[n] for n in INPUTS}
    x, y, c = lax.axis_index("x"), lax.axis_index("y"), lax.axis_index("c")
    me = 2 * x + y

    misc = jnp.concatenate([a['ev_conv_w'].reshape(6, 128), a['od_pool_scale'], jnp.zeros((8, 128), F32)], axis=0)
    gathered = _gather_chips([a[n].astype(BF16).reshape(_shard_rows(n)) for n in BIG] + [misc], name="weight_gather")
    w = {n: _unstack(n, g) for n, g in zip(BIG, gathered)}
    gm = gathered[-1]
    w['ev_conv_w'] = gm[:, 0:6].reshape(N_CHIPS, 2, 3, 128).transpose(1, 2, 0, 3).reshape(2, 3, 512)
    w['od_pool_scale'] = gm[:, 6:8].transpose(1, 0, 2).reshape(2, 512)
    for n in REPLICATED:
        w[n] = a[n]

    loss, grad_x, grads = _local_step(a['x'][0], a['p'][:, 0], a['loss_target'][0], w)
    loss = lax.psum(loss[0, 0], ("x", "y", "c"))

    small_names = REPLICATED + list(SMALL_SHARDED)
    small = _all_reduce_small(_pack([grads[n] for n in small_names]), name="small_grad_all_reduce")
    small = dict(zip(small_names, _unpack(small, [grads[n].shape for n in small_names])))
    for n in SMALL_SHARDED:
        small[n] = lax.dynamic_slice_in_dim(small[n], me * 128, 128, axis=small[n].ndim - 1)
    big = _reduce_big(grads, c, me)

    res = {}
    for n in BIG:
        shape = a[n].shape
        d, m_, v_ = _adamw(a[n].reshape(big[n].shape), big[n], a['m_' + n].reshape(big[n].shape),
                           a['v_' + n].reshape(big[n].shape), name=f"adamw_{n}")
        res[n] = (big[n].reshape(shape), d.reshape(shape), m_.reshape(shape), v_.reshape(shape))
    shapes = [a[n].shape for n in small_names]
    d, m_, v_ = _adamw(_pack([a[n] for n in small_names]), _pack([small[n] for n in small_names]),
                       _pack([a['m_' + n] for n in small_names]), _pack([a['v_' + n] for n in small_names]), name="adamw_small")
    for n, dd, mm, vv in zip(small_names, _unpack(d, shapes), _unpack(m_, shapes), _unpack(v_, shapes)):
        res[n] = (small[n], dd, mm, vv)

    outs = [loss, grad_x[None]]
    for part in range(4):
        outs += [res[n][part] for n in WEIGHTS]
    return tuple(outs)
```

```python
import functools
import math

import jax
import jax.numpy as jnp
import numpy as np
from jax import lax
from jax.experimental import pallas as pl
from jax.experimental.pallas import tpu as pltpu

F32 = jnp.float32
BF16 = jnp.bfloat16
MESH = pl.DeviceIdType.MESH
HIGHEST = lax.Precision.HIGHEST

D_MODEL = 1024
DEPTH = 4
MIX = 512
S5_GROUPS = 32
S5_GROUP = 16
S5_STATE = 64
S5_N = S5_GROUPS * S5_STATE
CHUNK = 64
LEFT_CHUNKS = 8
MAX_REL = 128
ATT_HEADS = 8
HEAD_DIM = 64
POOL_WINDOWS = (2, 4, 8, 16)
POOL_GROUP = 128
D_FF = 2816
D_PLE = 256
ALPHA = (2 * DEPTH) ** 0.25
LN_EPS = 1e-5
NEG_INF = -1e30
N_CHIPS = 4
N_DEV = 8

ADAM_LR = 0.001
ADAM_B1 = 0.9
ADAM_B2 = 0.999
ADAM_EPS = 1e-08
ADAM_WD = 0.01
ADAM_STEP = 10

TM = 512
T_S5 = 256
T_ATT = 512
VMEM_BIG = 56 * 1024 * 1024


VMEM_DEFAULT = 48 * 1024 * 1024


def _cparams(sem, vmem=None):
    return pltpu.CompilerParams(dimension_semantics=sem, vmem_limit_bytes=vmem or VMEM_DEFAULT)


def _sigmoid(x):
    return 1.0 / (1.0 + jnp.exp(-x))


def _mm(a_parts, b, *, name, trans_b=False, out_dtype=F32, tm=TM, tn=1024, vmem=None):
    m = a_parts[0][0].shape[0]
    n = b.shape[0] if trans_b else b.shape[1]
    kk = b.shape[1] if trans_b else b.shape[0]
    tn = min(tn, n)
    widths = [w for _, _, w in a_parts]
    assert sum(widths) == kk and m % tm == 0 and n % tn == 0
    na = len(a_parts)

    def body(*refs):
        b_ref, o_ref = refs[na], refs[na + 1]
        acc = None
        k0 = 0
        for ar, w in zip(refs[:na], widths):
            a = ar[...].astype(BF16)
            if trans_b:
                part = lax.dot_general(a, b_ref[:, k0:k0 + w], (((1,), (1,)), ((), ())), preferred_element_type=F32)
            else:
                part = jnp.dot(a, b_ref[k0:k0 + w, :], preferred_element_type=F32)
            acc = part if acc is None else acc + part
            k0 += w
        o_ref[...] = acc.astype(o_ref.dtype)

    in_specs = [pl.BlockSpec((tm, w), functools.partial(lambda j, i, cb: (i, cb), cb=cb)) for _, cb, w in a_parts]
    if trans_b:
        in_specs.append(pl.BlockSpec((tn, kk), lambda j, i: (j, 0)))
    else:
        in_specs.append(pl.BlockSpec((kk, tn), lambda j, i: (0, j)))
    return pl.pallas_call(
        body, name=name, grid=(n // tn, m // tm), in_specs=in_specs,
        out_specs=pl.BlockSpec((tm, tn), lambda j, i: (i, j)),
        out_shape=jax.ShapeDtypeStruct((m, n), out_dtype),
        compiler_params=_cparams(("parallel", "parallel"), vmem),
    )(*[a for a, _, _ in a_parts], b)


def _mm_tn(a, a_cb, ka, b, *, name, tk=1024, tn=1024, tmr=2 * TM, vmem=None):
    m = a.shape[0]
    n = b.shape[1]
    tk = min(tk, ka)
    tn = min(tn, n)
    assert ka % tk == 0 and n % tn == 0 and m % tmr == 0
    kb = ka // tk

    def body(a_ref, b_ref, o_ref):
        @pl.when(pl.program_id(2) == 0)
        def _():
            o_ref[...] = jnp.zeros_like(o_ref)

        o_ref[...] += lax.dot_general(a_ref[...].astype(BF16), b_ref[...].astype(BF16), (((0,), (0,)), ((), ())),
                                      preferred_element_type=F32)

    return pl.pallas_call(
        body, name=name, grid=(kb, n // tn, m // tmr),
        in_specs=[pl.BlockSpec((tmr, tk), lambda k, j, r: (r, a_cb * kb + k)),
                  pl.BlockSpec((tmr, tn), lambda k, j, r: (r, j))],
        out_specs=pl.BlockSpec((tk, tn), lambda k, j, r: (k, j)),
        out_shape=jax.ShapeDtypeStruct((ka, n), F32),
        compiler_params=_cparams(("parallel", "parallel", "arbitrary"), vmem),
    )(a, b)


def _mm_tn_slabs(a, ka, b, nbw, nslab, *, name, tmr=2 * TM):
    m = a.shape[0]
    assert m % tmr == 0

    def body(a_ref, b_ref, o_ref):
        @pl.when(pl.program_id(1) == 0)
        def _():
            o_ref[...] = jnp.zeros_like(o_ref)

        o_ref[0] += lax.dot_general(a_ref[...].astype(BF16), b_ref[...].astype(BF16), (((0,), (0,)), ((), ())),
                                    preferred_element_type=F32)

    return pl.pallas_call(
        body, name=name, grid=(nslab, m // tmr),
        in_specs=[pl.BlockSpec((tmr, ka), lambda s, r: (r, s)), pl.BlockSpec((tmr, nbw), lambda s, r: (r, s))],
        out_specs=pl.BlockSpec((1, ka, nbw), lambda s, r: (s, 0, 0)),
        out_shape=jax.ShapeDtypeStruct((nslab, ka, nbw), F32),
        compiler_params=_cparams(("parallel", "arbitrary")),
    )(a, b)


def _ln_stats(r):
    mu = jnp.mean(r, axis=-1, keepdims=True)
    xc = r - mu
    var = jnp.mean(xc * xc, axis=-1, keepdims=True)
    rstd = lax.rsqrt(var + LN_EPS)
    return xc * rstd, rstd


def _ln_apply(r, g, b):
    xhat, _ = _ln_stats(r)
    return xhat * g + b


def _ln_grad(r, dy, g):
    xhat, rstd = _ln_stats(r)
    dxh = dy * g
    m1 = jnp.mean(dxh, axis=-1, keepdims=True)
    m2 = jnp.mean(dxh * xhat, axis=-1, keepdims=True)
    return (rstd * (dxh - m1 - xhat * m2), jnp.sum(dy * xhat, axis=0, keepdims=True), jnp.sum(dy, axis=0, keepdims=True))


def _mm_rows(matmuls, rows_in, vecs_in, out_rows, acc_widths, fn, *, name, tm=TM, vmem=None):
    m = rows_in[0].shape[0]
    assert m % tm == 0
    flat, in_specs, layout = [], [], []
    for a_parts, b, trans_b in matmuls:
        for arr, cb, w in a_parts:
            flat.append(arr)
            in_specs.append(pl.BlockSpec((tm, w), functools.partial(lambda i, cb: (i, cb), cb=cb)))
        flat.append(b)
        in_specs.append(pl.BlockSpec(b.shape, lambda i: (0, 0)))
        layout.append(([w for _, _, w in a_parts], trans_b))
    for r in rows_in:
        flat.append(r)
        in_specs.append(pl.BlockSpec((tm, r.shape[1]), lambda i: (i, 0)))
    for v in vecs_in:
        flat.append(v)
        in_specs.append(pl.BlockSpec(v.shape, lambda i: (0, 0)))
    n_in = len(flat)
    n_rows_out = len(out_rows)

    def body(*refs):
        pos = 0
        products = []
        for widths, trans_b in layout:
            b_ref = refs[pos + len(widths)]
            acc, k0 = None, 0
            for ar, w in zip(refs[pos:pos + len(widths)], widths):
                a = ar[...].astype(BF16)
                if trans_b:
                    part = lax.dot_general(a, b_ref[:, k0:k0 + w], (((1,), (1,)), ((), ())), preferred_element_type=F32)
                else:
                    part = jnp.dot(a, b_ref[k0:k0 + w, :], preferred_element_type=F32)
                acc = part if acc is None else acc + part
                k0 += w
            products.append(acc)
            pos += len(widths) + 1
        rows = [r[...] for r in refs[pos:pos + len(rows_in)]]
        pos += len(rows_in)
        vecs = [v[...] for v in refs[pos:n_in]]
        outs, sums = fn(products, rows, vecs)
        for o_ref, o in zip(refs[n_in:n_in + n_rows_out], outs):
            o_ref[...] = o.astype(o_ref.dtype)
        if acc_widths:
            acc_refs = refs[n_in + n_rows_out:]

            @pl.when(pl.program_id(0) == 0)
            def _():
                for a_ref in acc_refs:
                    a_ref[...] = jnp.zeros_like(a_ref)

            for a_ref, s_ in zip(acc_refs, sums):
                a_ref[...] += s_

    out_specs = [pl.BlockSpec((tm, n), lambda i: (i, 0)) for n, _ in out_rows]
    out_specs += [pl.BlockSpec((1, wd), lambda i: (0, 0)) for wd in acc_widths]
    out_shape = [jax.ShapeDtypeStruct((m, n), dt) for n, dt in out_rows]
    out_shape += [jax.ShapeDtypeStruct((1, wd), F32) for wd in acc_widths]
    return pl.pallas_call(
        body, name=name, grid=(m // tm,), in_specs=in_specs, out_specs=out_specs, out_shape=out_shape,
        compiler_params=_cparams(("arbitrary",) if acc_widths else ("parallel",), vmem),
    )(*flat)


def _ffn_up(x1, wup, *, name):
    m = x1.shape[0]
    tn = D_FF // 2

    def body(x_ref, wg_ref, wu_ref, a_ref, g_ref, u_ref):
        x = x_ref[...].astype(BF16)
        g = jnp.dot(x, wg_ref[...], preferred_element_type=F32)
        u = jnp.dot(x, wu_ref[...], preferred_element_type=F32)
        a_ref[...] = (g * _sigmoid(g) * u).astype(BF16)
        g_ref[...] = g.astype(BF16)
        u_ref[...] = u.astype(BF16)

    out = pl.BlockSpec((TM, tn), lambda j, i: (i, j))
    return pl.pallas_call(
        body, name=name, grid=(2, m // TM),
        in_specs=[pl.BlockSpec((TM, D_MODEL), lambda j, i: (i, 0)),
                  pl.BlockSpec((D_MODEL, tn), lambda j, i: (0, j)),
                  pl.BlockSpec((D_MODEL, tn), lambda j, i: (0, j + 2))],
        out_specs=[out, out, out], out_shape=[jax.ShapeDtypeStruct((m, D_FF), BF16)] * 3,
        compiler_params=_cparams(("parallel", "parallel")),
    )(x1, wup, wup)


def _ffn_down_bwd(df, wdown, g, u, *, name):
    m = df.shape[0]
    tm = 256

    def body(df_ref, w_ref, g_ref, u_ref, o_ref):
        da = lax.dot_general(df_ref[...].astype(BF16), w_ref[...], (((1,), (1,)), ((), ())), preferred_element_type=F32)
        gg = g_ref[...].astype(F32)
        sg = _sigmoid(gg)
        o_ref[:, :D_FF] = (da * u_ref[...].astype(F32) * (sg * (1.0 + gg * (1.0 - sg)))).astype(BF16)
        o_ref[:, D_FF:] = (da * (gg * sg)).astype(BF16)

    return pl.pallas_call(
        body, name=name, grid=(m // tm,),
        in_specs=[pl.BlockSpec((tm, D_MODEL), lambda i: (i, 0)), pl.BlockSpec((D_FF, D_MODEL), lambda i: (0, 0)),
                  pl.BlockSpec((tm, D_FF), lambda i: (i, 0)), pl.BlockSpec((tm, D_FF), lambda i: (i, 0))],
        out_specs=pl.BlockSpec((tm, 2 * D_FF), lambda i: (i, 0)),
        out_shape=jax.ShapeDtypeStruct((m, 2 * D_FF), BF16),
        compiler_params=_cparams(("parallel",), VMEM_BIG),
    )(df, wdown, g, u)


def _ple_ln_bwd(da, db, gate, pp, r2, wgate, g2, *, name):
    m, n = da.shape
    two = db is not None
    n_in = 7 if two else 6

    def body(*refs):
        if two:
            da_ref, db_ref, gate_ref, pp_ref, r_ref, w_ref, g_ref = refs[:n_in]
            dx3 = ALPHA * da_ref[...] + db_ref[...]
        else:
            da_ref, gate_ref, pp_ref, r_ref, w_ref, g_ref = refs[:n_in]
            dx3 = da_ref[...]
        dz_ref, dpp_ref, dr_ref, dbg_ref, dg_ref, dbias_ref = refs[n_in:]

        @pl.when(pl.program_id(0) == 0)
        def _():
            dbg_ref[...] = jnp.zeros_like(dbg_ref)
            dg_ref[...] = jnp.zeros_like(dg_ref)
            dbias_ref[...] = jnp.zeros_like(dbias_ref)

        gate = gate_ref[...].astype(F32)
        dz = dx3 * pp_ref[...].astype(F32) * gate * (1.0 - gate)
        dzq = dz.astype(BF16)
        dz_ref[...] = dzq
        dpp_ref[...] = (dx3 * gate).astype(BF16)
        dbg_ref[...] += jnp.sum(dz, axis=0, keepdims=True)
        dx2 = dx3 + lax.dot_general(dzq, w_ref[...], (((1,), (1,)), ((), ())), preferred_element_type=F32)
        dr, dg, dbias = _ln_grad(r_ref[...], dx2, g_ref[...])
        dr_ref[...] = dr
        dg_ref[...] += dg
        dbias_ref[...] += dbias

    row = pl.BlockSpec((TM, n), lambda i: (i, 0))
    vec = pl.BlockSpec((1, n), lambda i: (0, 0))
    ins = ([da, db] if two else [da]) + [gate, pp, r2, wgate, g2]
    in_specs = [row] * (n_in - 2) + [pl.BlockSpec(wgate.shape, lambda i: (0, 0)), vec]
    return pl.pallas_call(
        body, name=name, grid=(m // TM,), in_specs=in_specs, out_specs=[row, row, row, vec, vec, vec],
        out_shape=[jax.ShapeDtypeStruct((m, n), BF16), jax.ShapeDtypeStruct((m, n), BF16), jax.ShapeDtypeStruct((m, n), F32)]
        + [jax.ShapeDtypeStruct((1, n), F32)] * 3,
        compiler_params=_cparams(("arbitrary",)),
    )(*ins)


def _loss_head(y, target, *, name):
    m, n = y.shape

    def body(y_ref, t_ref, loss_ref, dy_ref):
        @pl.when(pl.program_id(0) == 0)
        def _():
            loss_ref[...] = jnp.zeros_like(loss_ref)

        err = y_ref[...] - t_ref[...]
        dy_ref[...] = err * (1.0 / n)
        per_tok = jnp.mean(err * err, axis=-1, keepdims=True)
        loss_ref[...] += 0.5 * jnp.sum(per_tok, axis=0, keepdims=True)

    row = pl.BlockSpec((TM, n), lambda i: (i, 0))
    return pl.pallas_call(
        body, name=name, grid=(m // TM,), in_specs=[row, row],
        out_specs=[pl.BlockSpec((1, 1), lambda i: (0, 0)), row],
        out_shape=[jax.ShapeDtypeStruct((1, 1), F32), jax.ShapeDtypeStruct((m, n), F32)],
        compiler_params=_cparams(("arbitrary",)),
    )(y, target)


def _gelu(y):
    c = math.sqrt(2.0 / math.pi)
    return 0.5 * y * (1.0 + jnp.tanh(c * (y + 0.044715 * y * y * y)))


def _gelu_grad(y):
    c = math.sqrt(2.0 / math.pi)
    t = jnp.tanh(c * (y + 0.044715 * y * y * y))
    return 0.5 * (1.0 + t) + 0.5 * y * (1.0 - t * t) * c * (1.0 + 3.0 * 0.044715 * y * y)


STRIP = 256
S5_SLABS = 4
SLAB_COLS = 2 * S5_N // S5_SLABS


def _strip_cols(j):
    off = pl.multiple_of(j * STRIP, STRIP)
    col = pl.multiple_of(j * STRIP + (j // 2) * (SLAB_COLS // 2), STRIP)
    return off, col, pl.multiple_of(col + SLAB_COLS // 2, STRIP)


def _scan_strip(xr, xi, cr, ci, ptab_ref, off, rowmod, down):
    t = xr.shape[0]
    base = 0 if down else 16
    p_r = ptab_ref[base:base + 8, pl.ds(off, STRIP)]
    p_i = ptab_ref[base + 8:base + 16, pl.ds(off, STRIP)]
    if not down:
        p_i = -p_i
    for k in range(3):
        s = 1 << k
        idx = s - 1 if down else 8 - s
        pr, pi_ = p_r[idx:idx + 1], p_i[idx:idx + 1]
        if down:
            sr = jnp.where(rowmod >= s, pltpu.roll(xr, s, 0), 0.0)
            si = jnp.where(rowmod >= s, pltpu.roll(xi, s, 0), 0.0)
        else:
            sr = jnp.where(rowmod < 8 - s, pltpu.roll(xr, t - s, 0), 0.0)
            si = jnp.where(rowmod < 8 - s, pltpu.roll(xi, t - s, 0), 0.0)
        xr, xi = xr + pr * sr - pi_ * si, xi + pr * si + pi_ * sr
    ng = t // 8
    out_r, out_i = [None] * ng, [None] * ng
    for g in (range(ng) if down else reversed(range(ng))):
        cbr = jnp.broadcast_to(cr, (8, STRIP))
        cbi = jnp.broadcast_to(ci, (8, STRIP))
        br = xr[8 * g:8 * g + 8] + p_r * cbr - p_i * cbi
        bi = xi[8 * g:8 * g + 8] + p_r * cbi + p_i * cbr
        cr, ci = (br[7:8], bi[7:8]) if down else (br[0:1], bi[0:1])
        out_r[g], out_i[g] = br, bi
    return jnp.concatenate(out_r, axis=0), jnp.concatenate(out_i, axis=0)


def _s5_fwd(h, bmat, cmat, dvec, wglu, bglu, ptab, *, name):
    m = h.shape[0]
    t = T_S5
    nb = m // t

    def body(u_ref, bmat_ref, cmat_ref, d_ref, wglu_ref, bglu_ref, ptab_ref,
             out_ref, y_ref, hb_ref, bu_ref, carry_ref):
        @pl.when(pl.program_id(0) == 0)
        def _():
            carry_ref[...] = jnp.zeros_like(carry_ref)

        hb_ref[0] = carry_ref[...]
        u = u_ref[...]
        ub = u.astype(BF16)
        for s in range(S5_SLABS):
            bu_ref[:, SLAB_COLS * s:SLAB_COLS * (s + 1)] = jnp.dot(ub[:, 128 * s:128 * (s + 1)], bmat_ref[s],
                                                                  preferred_element_type=F32)
        rowmod = lax.broadcasted_iota(jnp.int32, (t, STRIP), 0) & 7

        def strip(j, c):
            off, col, coli = _strip_cols(j)
            xr, xi = _scan_strip(bu_ref[:, pl.ds(col, STRIP)], bu_ref[:, pl.ds(coli, STRIP)],
                                 carry_ref[0:1, pl.ds(col, STRIP)], carry_ref[0:1, pl.ds(coli, STRIP)],
                                 ptab_ref, off, rowmod, True)
            bu_ref[:, pl.ds(col, STRIP)] = xr
            bu_ref[:, pl.ds(coli, STRIP)] = xi
            carry_ref[0:1, pl.ds(col, STRIP)] = xr[t - 1:t, :]
            carry_ref[0:1, pl.ds(coli, STRIP)] = xi[t - 1:t, :]
            return c

        lax.fori_loop(0, S5_N // STRIP, strip, 0)
        y = jnp.concatenate(
            [jnp.dot(bu_ref[:, SLAB_COLS * s:SLAB_COLS * (s + 1)].astype(BF16), cmat_ref[s], preferred_element_type=F32)
             for s in range(S5_SLABS)], axis=1) + d_ref[...] * u
        y_ref[...] = y
        g = _gelu(y)
        zz = jnp.dot(g.astype(BF16), wglu_ref[...], preferred_element_type=F32) + bglu_ref[...]
        out_ref[...] = (g * _sigmoid(zz)).astype(BF16)

    const = lambda shape: pl.BlockSpec(shape, lambda i: (0,) * len(shape))
    row_spec = pl.BlockSpec((t, MIX), lambda i: (i, 0))
    return pl.pallas_call(
        body, name=name, grid=(nb,),
        in_specs=[row_spec, const((S5_SLABS, 128, SLAB_COLS)), const((S5_SLABS, SLAB_COLS, 128)), const((1, MIX)),
                  const((MIX, MIX)), const((1, MIX)), const((32, S5_N))],
        out_specs=[row_spec, row_spec, pl.BlockSpec((1, 1, 2 * S5_N), lambda i: (i, 0, 0))],
        out_shape=[jax.ShapeDtypeStruct((m, MIX), BF16), jax.ShapeDtypeStruct((m, MIX), F32),
                   jax.ShapeDtypeStruct((nb, 1, 2 * S5_N), F32)],
        scratch_shapes=[pltpu.VMEM((t, 2 * S5_N), F32), pltpu.VMEM((1, 2 * S5_N), F32)],
        compiler_params=_cparams(("arbitrary",), VMEM_BIG),
    )(h, bmat, cmat, dvec, wglu, bglu, ptab)


def _s5_bwd(dcat, ypre, h, hb, bmat, cmat, dvec, wglu, bglu, ptab, *, name):
    m = h.shape[0]
    t = T_S5
    nb = m // t

    def body(dya_ref, y_ref, u_ref, hb_ref, bmat_ref, cmat_ref, d_ref, wglu_ref, bglu_ref, ptab_ref,
             du_ref, xb_ref, gb_ref, gq_ref, dzz_ref, dyq_ref, dlam_ref, dbglu_ref, dd_ref,
             bu_ref, dx_ref, gcarry_ref):
        @pl.when(pl.program_id(0) == 0)
        def _():
            gcarry_ref[...] = jnp.zeros_like(gcarry_ref)
            dlam_ref[...] = jnp.zeros_like(dlam_ref)
            dbglu_ref[...] = jnp.zeros_like(dbglu_ref)
            dd_ref[...] = jnp.zeros_like(dd_ref)

        u = u_ref[...]
        y = y_ref[...]
        g = _gelu(y)
        gq = g.astype(BF16)
        sg = _sigmoid(jnp.dot(gq, wglu_ref[...], preferred_element_type=F32) + bglu_ref[...])
        dout = dya_ref[...]
        dzz = dout * g * sg * (1.0 - sg)
        dzzq = dzz.astype(BF16)
        dg = dout * sg + lax.dot_general(dzzq, wglu_ref[...], (((1,), (1,)), ((), ())), preferred_element_type=F32)
        dy = dg * _gelu_grad(y)
        dyq = dy.astype(BF16)
        gq_ref[...] = gq
        dzz_ref[...] = dzzq
        dyq_ref[...] = dyq
        dbglu_ref[...] += jnp.sum(dzz, axis=0, keepdims=True)
        dd_ref[...] += jnp.sum(dy * u, axis=0, keepdims=True)

        ub = u.astype(BF16)
        nt = (((1,), (1,)), ((), ()))
        for s in range(S5_SLABS):
            cols = slice(SLAB_COLS * s, SLAB_COLS * (s + 1))
            dx_ref[:, cols] = lax.dot_general(dyq[:, 128 * s:128 * (s + 1)], cmat_ref[s], nt, preferred_element_type=F32)
            bu_ref[:, cols] = jnp.dot(ub[:, 128 * s:128 * (s + 1)], bmat_ref[s], preferred_element_type=F32)
        row = lax.broadcasted_iota(jnp.int32, (t, STRIP), 0)
        rowmod = row & 7

        def strip(j, c):
            off, col, coli = _strip_cols(j)
            hr = hb_ref[0, 0:1, pl.ds(col, STRIP)]
            hi = hb_ref[0, 0:1, pl.ds(coli, STRIP)]
            xr, xi = _scan_strip(bu_ref[:, pl.ds(col, STRIP)], bu_ref[:, pl.ds(coli, STRIP)], hr, hi,
                                 ptab_ref, off, rowmod, True)
            xb_ref[:, pl.ds(col, STRIP)] = xr.astype(BF16)
            xb_ref[:, pl.ds(coli, STRIP)] = xi.astype(BF16)
            pr_ = jnp.where(row == 0, hr, pltpu.roll(xr, 1, 0))
            pi_ = jnp.where(row == 0, hi, pltpu.roll(xi, 1, 0))

            gr, gi = _scan_strip(dx_ref[:, pl.ds(col, STRIP)], dx_ref[:, pl.ds(coli, STRIP)],
                                 gcarry_ref[0:1, pl.ds(col, STRIP)], gcarry_ref[0:1, pl.ds(coli, STRIP)],
                                 ptab_ref, off, rowmod, False)
            gb_ref[:, pl.ds(col, STRIP)] = gr.astype(BF16)
            gb_ref[:, pl.ds(coli, STRIP)] = gi.astype(BF16)
            gcarry_ref[0:1, pl.ds(col, STRIP)] = gr[0:1, :]
            gcarry_ref[0:1, pl.ds(coli, STRIP)] = gi[0:1, :]
            dlam_ref[0:1, pl.ds(off, STRIP)] += jnp.sum(pr_ * gr + pi_ * gi, axis=0, keepdims=True)
            dlam_ref[1:2, pl.ds(off, STRIP)] += jnp.sum(pr_ * gi - pi_ * gr, axis=0, keepdims=True)
            return c

        lax.fori_loop(0, S5_N // STRIP, strip, 0)
        du_ref[...] = dy * d_ref[...] + jnp.concatenate(
            [lax.dot_general(gb_ref[:, SLAB_COLS * s:SLAB_COLS * (s + 1)], bmat_ref[s], nt, preferred_element_type=F32)
             for s in range(S5_SLABS)], axis=1)

    const = lambda shape: pl.BlockSpec(shape, lambda i: (0,) * len(shape))
    rev = lambda i: (nb - 1 - i, 0)
    row_spec = pl.BlockSpec((t, MIX), rev)
    wide = pl.BlockSpec((t, 2 * S5_N), rev)
    return pl.pallas_call(
        body, name=name, grid=(nb,),
        in_specs=[row_spec, row_spec, row_spec, pl.BlockSpec((1, 1, 2 * S5_N), lambda i: (nb - 1 - i, 0, 0)),
                  const((S5_SLABS, 128, SLAB_COLS)), const((S5_SLABS, SLAB_COLS, 128)), const((1, MIX)), const((MIX, MIX)),
                  const((1, MIX)), const((32, S5_N))],
        out_specs=[row_spec, wide, wide, row_spec, row_spec, row_spec, const((2, S5_N)), const((1, MIX)), const((1, MIX))],
        out_shape=[jax.ShapeDtypeStruct((m, MIX), F32), jax.ShapeDtypeStruct((m, 2 * S5_N), BF16),
                   jax.ShapeDtypeStruct((m, 2 * S5_N), BF16), jax.ShapeDtypeStruct((m, MIX), BF16),
                   jax.ShapeDtypeStruct((m, MIX), BF16), jax.ShapeDtypeStruct((m, MIX), BF16),
                   jax.ShapeDtypeStruct((2, S5_N), F32), jax.ShapeDtypeStruct((1, MIX), F32), jax.ShapeDtypeStruct((1, MIX), F32)],
        scratch_shapes=[pltpu.VMEM((t, 2 * S5_N), F32), pltpu.VMEM((t, 2 * S5_N), F32), pltpu.VMEM((1, 2 * S5_N), F32)],
        compiler_params=_cparams(("arbitrary",), VMEM_BIG),
    )(dcat, ypre, h, hb, bmat, cmat, dvec, wglu, bglu, ptab)


HALO = 8


def _taps_down(zext, t):
    return pltpu.roll(zext, 1, 0)[HALO:HALO + t], pltpu.roll(zext, 2, 0)[HALO:HALO + t]


def _conv_z(c_ref, x_ref, cp_ref, xp_ref, first, t):
    z = c_ref[...] * x_ref[...]
    zp = jnp.where(first, 0.0, cp_ref[t - HALO:t, :] * xp_ref[t - HALO:t, :])
    z1, z2 = _taps_down(jnp.concatenate([zp, z], axis=0), t)
    return z, z1, z2


def _conv_fwd(h, cw, *, name):
    m = h.shape[0]
    t = TM
    nb = m // t

    def body(b_ref, c_ref, x_ref, cp_ref, xp_ref, w_ref, o_ref):
        z, z1, z2 = _conv_z(c_ref, x_ref, cp_ref, xp_ref, pl.program_id(0) == 0, t)
        o_ref[...] = (b_ref[...] * (w_ref[0:1, :] * z2 + w_ref[1:2, :] * z1 + w_ref[2:3, :] * z)).astype(BF16)

    cur = lambda cb: pl.BlockSpec((t, MIX), lambda i: (i, cb))
    prev = lambda cb: pl.BlockSpec((t, MIX), lambda i: (jnp.maximum(i - 1, 0), cb))
    return pl.pallas_call(
        body, name=name, grid=(nb,),
        in_specs=[cur(1), cur(2), cur(3), prev(2), prev(3), pl.BlockSpec((3, MIX), lambda i: (0, 0))],
        out_specs=pl.BlockSpec((t, MIX), lambda i: (i, 0)),
        out_shape=jax.ShapeDtypeStruct((m, MIX), BF16),
        compiler_params=_cparams(("parallel",)),
    )(h, h, h, h, h, cw)


def _conv_bwd(dcat, h, cw, *, name):
    m = h.shape[0]
    t = TM
    nb = m // t

    def body(dy_ref, dyn_ref, b_ref, c_ref, x_ref, cp_ref, xp_ref, bn_ref, w_ref, o_ref, dw_ref):
        i = pl.program_id(0)

        @pl.when(i == 0)
        def _():
            dw_ref[...] = jnp.zeros_like(dw_ref)

        z, z1, z2 = _conv_z(c_ref, x_ref, cp_ref, xp_ref, i == 0, t)
        w0, w1, w2 = w_ref[0:1, :], w_ref[1:2, :], w_ref[2:3, :]
        dy = dy_ref[...]
        dconv = dy * b_ref[...]
        dnext = jnp.where(i == nb - 1, 0.0, dyn_ref[0:HALO, :] * bn_ref[0:HALO, :])
        dext = jnp.concatenate([dconv, dnext], axis=0)
        d1 = pltpu.roll(dext, t + HALO - 1, 0)[0:t]
        d2 = pltpu.roll(dext, t + HALO - 2, 0)[0:t]
        dz = w2 * dconv + w1 * d1 + w0 * d2
        o_ref[:, 0:MIX] = dy * (w0 * z2 + w1 * z1 + w2 * z)
        o_ref[:, MIX:2 * MIX] = dz * x_ref[...]
        o_ref[:, 2 * MIX:3 * MIX] = dz * c_ref[...]
        dw_ref[0:1, :] += jnp.sum(dconv * z2, axis=0, keepdims=True)
        dw_ref[1:2, :] += jnp.sum(dconv * z1, axis=0, keepdims=True)
        dw_ref[2:3, :] += jnp.sum(dconv * z, axis=0, keepdims=True)

    cur = lambda cb: pl.BlockSpec((t, MIX), lambda i: (i, cb))
    prev = lambda cb: pl.BlockSpec((t, MIX), lambda i: (jnp.maximum(i - 1, 0), cb))
    nxt = lambda cb: pl.BlockSpec((t, MIX), lambda i: (jnp.minimum(i + 1, nb - 1), cb))
    return pl.pallas_call(
        body, name=name, grid=(nb,),
        in_specs=[cur(1), nxt(1), cur(1), cur(2), cur(3), prev(2), prev(3), nxt(1), pl.BlockSpec((3, MIX), lambda i: (0, 0))],
        out_specs=[pl.BlockSpec((t, 3 * MIX), lambda i: (i, 0)), pl.BlockSpec((8, MIX), lambda i: (0, 0))],
        out_shape=[jax.ShapeDtypeStruct((m, 3 * MIX), F32), jax.ShapeDtypeStruct((8, MIX), F32)],
        compiler_params=_cparams(("arbitrary",)),
    )(dcat, dcat, h, h, h, h, h, h, cw)


PHALO = 16


def _pool_pooled(z_ref, zp_ref, i, t):
    z = z_ref[...]
    zp = jnp.where(i == 0, 0.0, zp_ref[t - PHALO:t, :])
    s = jnp.concatenate([zp, z], axis=0)
    sums = {}
    width = 1
    while width < PHALO:
        s = s + pltpu.roll(s, width, 0)
        width *= 2
        sums[width] = s[PHALO:PHALO + t]
    tpos = i * t + lax.broadcasted_iota(jnp.int32, (t, 1), 0)
    outs = []
    for gi, w in enumerate(POOL_WINDOWS):
        lo = gi * POOL_GROUP
        count = jnp.minimum(tpos + 1, w).astype(F32)
        outs.append(sums[w][:, lo:lo + POOL_GROUP] / count - z[:, lo:lo + POOL_GROUP])
    return outs


def _pool_fwd(h, pw, ps, *, name):
    m = h.shape[0]
    t = TM
    nb = m // t

    def body(z_ref, zp_ref, pw_ref, ps_ref, o_ref):
        pooled = _pool_pooled(z_ref, zp_ref, pl.program_id(0), t)
        for gi in range(len(POOL_WINDOWS)):
            lo = gi * POOL_GROUP
            mixed = jnp.dot(pooled[gi].astype(BF16), pw_ref[gi], preferred_element_type=F32)
            o_ref[:, lo:lo + POOL_GROUP] = (mixed * ps_ref[:, lo:lo + POOL_GROUP]).astype(BF16)

    return pl.pallas_call(
        body, name=name, grid=(nb,),
        in_specs=[pl.BlockSpec((t, MIX), lambda i: (i, 3)), pl.BlockSpec((t, MIX), lambda i: (jnp.maximum(i - 1, 0), 3)),
                  pl.BlockSpec((4, POOL_GROUP, POOL_GROUP), lambda i: (0, 0, 0)), pl.BlockSpec((1, MIX), lambda i: (0, 0))],
        out_specs=pl.BlockSpec((t, MIX), lambda i: (i, 0)),
        out_shape=jax.ShapeDtypeStruct((m, MIX), BF16),
        compiler_params=_cparams(("parallel",)),
    )(h, h, pw, ps)


def _pool_bwd(dcat, h, pw, ps, *, name):
    m = h.shape[0]
    t = TM
    nb = m // t

    def body(dy_ref, dyn_ref, z_ref, zp_ref, pw_ref, ps_ref, dz_ref, dpw_ref, dps_ref):
        i = pl.program_id(0)

        @pl.when(i == 0)
        def _():
            dpw_ref[...] = jnp.zeros_like(dpw_ref)
            dps_ref[...] = jnp.zeros_like(dps_ref)

        pooled = _pool_pooled(z_ref, zp_ref, i, t)
        dy = dy_ref[...]
        tpos = i * t + lax.broadcasted_iota(jnp.int32, (t, 1), 0)
        for gi, w in enumerate(POOL_WINDOWS):
            lo = gi * POOL_GROUP
            sl = slice(lo, lo + POOL_GROUP)
            pq = pooled[gi].astype(BF16)
            mixed = jnp.dot(pq, pw_ref[gi], preferred_element_type=F32)
            dps_ref[:, sl] += jnp.sum(dy[:, sl] * mixed, axis=0, keepdims=True)
            dmix = (dy[:, sl] * ps_ref[:, sl]).astype(BF16)
            dpw_ref[gi] += lax.dot_general(pq, dmix, (((0,), (0,)), ((), ())), preferred_element_type=F32)
            dpool = lax.dot_general(dmix, pw_ref[gi], (((1,), (1,)), ((), ())), preferred_element_type=F32)
            dmix_n = (dyn_ref[0:PHALO, sl] * ps_ref[:, sl]).astype(BF16)
            dpool_n = lax.dot_general(dmix_n, pw_ref[gi], (((1,), (1,)), ((), ())), preferred_element_type=F32)
            e = dpool / jnp.minimum(tpos + 1, w).astype(F32)
            e_n = jnp.where(i == nb - 1, 0.0, dpool_n * (1.0 / w))
            f = jnp.concatenate([e, e_n], axis=0)
            width = 1
            while width < w:
                f = f + pltpu.roll(f, t + PHALO - width, 0)
                width *= 2
            dz_ref[:, sl] = f[0:t] - dpool

    return pl.pallas_call(
        body, name=name, grid=(nb,),
        in_specs=[pl.BlockSpec((t, MIX), lambda i: (i, 1)), pl.BlockSpec((t, MIX), lambda i: (jnp.minimum(i + 1, nb - 1), 1)),
                  pl.BlockSpec((t, MIX), lambda i: (i, 3)), pl.BlockSpec((t, MIX), lambda i: (jnp.maximum(i - 1, 0), 3)),
                  pl.BlockSpec((4, POOL_GROUP, POOL_GROUP), lambda i: (0, 0, 0)), pl.BlockSpec((1, MIX), lambda i: (0, 0))],
        out_specs=[pl.BlockSpec((t, MIX), lambda i: (i, 0)), pl.BlockSpec((4, POOL_GROUP, POOL_GROUP), lambda i: (0, 0, 0)),
                   pl.BlockSpec((1, MIX), lambda i: (0, 0))],
        out_shape=[jax.ShapeDtypeStruct((m, MIX), F32), jax.ShapeDtypeStruct((4, POOL_GROUP, POOL_GROUP), F32),
                   jax.ShapeDtypeStruct((1, MIX), F32)],
        compiler_params=_cparams(("arbitrary",)),
    )(dcat, dcat, h, h, pw, ps)


NKEY = 2 * T_ATT


def _band_mask():
    qc = np.arange(T_ATT)[:, None] // CHUNK
    kc = np.arange(NKEY)[None, :] // CHUNK - LEFT_CHUNKS
    return np.where((kc <= qc) & (kc >= qc - LEFT_CHUNKS), 0.0, NEG_INF).astype(np.float32)


def _diag_index():
    c = np.arange(NKEY)
    d = np.where(c <= NKEY // 2 + CHUNK, T_ATT - c, T_ATT + NKEY - c)
    return np.clip(d, -MAX_REL, MAX_REL) + MAX_REL


def _bias_tile(vd_ref, mask_ref, tile_ref):
    col = lax.broadcasted_iota(jnp.int32, (8, NKEY), 1)
    no_prev = jnp.where(col < T_ATT, NEG_INF, 0.0)
    for hh in range(2):
        v = vd_ref[0, hh:hh + 1, :]
        base = jnp.concatenate([v if s == 0 else pltpu.roll(v, s, 1) for s in range(8)], axis=0)
        for mrow in range(T_ATT // 8):
            rows = slice(8 * mrow, 8 * mrow + 8)
            blk = (base if mrow == 0 else pltpu.roll(base, 8 * mrow, 1)) + mask_ref[rows, :]
            tile_ref[hh, rows, :] = blk
            tile_ref[2 + hh, rows, :] = blk + no_prev


BAND_ROWS = 2 * CHUNK
BAND_COLS = (LEFT_CHUNKS + 2) * CHUNK
N_BANDS = T_ATT // BAND_ROWS


def _band(x, r):
    return x[BAND_ROWS * r:BAND_ROWS * (r + 1), BAND_ROWS * r:BAND_ROWS * r + BAND_COLS]


def _from_bands(parts):
    rows = []
    for r, part in enumerate(parts):
        right = NKEY - BAND_COLS - BAND_ROWS * r
        pieces = ([jnp.zeros((BAND_ROWS, BAND_ROWS * r), part.dtype)] if r else []) + [part]
        pieces += [jnp.zeros((BAND_ROWS, right), part.dtype)] if right else []
        rows.append(jnp.concatenate(pieces, axis=1))
    return jnp.concatenate(rows, axis=0)


def _attn_probs(q, kc, tile_ref, idx):
    s = lax.dot_general(q, kc, (((1,), (1,)), ((), ())), preferred_element_type=F32)
    parts = []
    for r in range(N_BANDS):
        sb = _band(s, r) + tile_ref[idx, BAND_ROWS * r:BAND_ROWS * (r + 1), BAND_ROWS * r:BAND_ROWS * r + BAND_COLS]
        p = jnp.exp(sb - jnp.max(sb, axis=-1, keepdims=True))
        parts.append(p * (1.0 / jnp.sum(p, axis=-1, keepdims=True)))
    return parts


def _attn_specs(block):
    cur = lambda base: pl.BlockSpec((T_ATT, 128), lambda hp, i: (block(i), base + hp))
    prev = lambda base: pl.BlockSpec((T_ATT, 128), lambda hp, i: (jnp.maximum(block(i) - 1, 0), base + hp))
    return [cur(0), cur(4), prev(4), cur(8), prev(8),
            pl.BlockSpec((1, 2, NKEY), lambda hp, i: (hp, 0, 0)), pl.BlockSpec((T_ATT, NKEY), lambda hp, i: (0, 0))]


def _attn_fwd(h, vdiag, mask, *, name):
    m = h.shape[0]
    nb = m // T_ATT

    def body(q_ref, k_ref, kp_ref, v_ref, vp_ref, vd_ref, mask_ref, o_ref, tile_ref):
        i = pl.program_id(1)

        @pl.when(i == 0)
        def _():
            _bias_tile(vd_ref, mask_ref, tile_ref)

        first = jnp.where(i == 0, 2, 0)
        outs = []
        for hh in range(2):
            sl = slice(hh * HEAD_DIM, (hh + 1) * HEAD_DIM)
            q = (q_ref[:, sl] * (HEAD_DIM ** -0.5)).astype(BF16)
            kc = jnp.concatenate([kp_ref[:, sl], k_ref[:, sl]], axis=0).astype(BF16)
            vc = jnp.concatenate([vp_ref[:, sl], v_ref[:, sl]], axis=0).astype(BF16)
            p = _from_bands([b.astype(BF16) for b in _attn_probs(q, kc, tile_ref, first + hh)])
            outs.append(jnp.dot(p, vc, preferred_element_type=F32))
        o_ref[...] = jnp.concatenate(outs, axis=1).astype(BF16)

    return pl.pallas_call(
        body, name=name, grid=(ATT_HEADS // 2, nb), in_specs=_attn_specs(lambda i: i),
        out_specs=pl.BlockSpec((T_ATT, 128), lambda hp, i: (i, hp)),
        out_shape=jax.ShapeDtypeStruct((m, MIX), BF16),
        scratch_shapes=[pltpu.VMEM((4, T_ATT, NKEY), F32)],
        compiler_params=_cparams(("parallel", "arbitrary"), VMEM_BIG),
    )(h, h, h, h, h, vdiag, mask)


def _attn_bwd(dcat, h, vdiag, mask, *, name):
    m = h.shape[0]
    nb = m // T_ATT

    def body(do_ref, q_ref, k_ref, kp_ref, v_ref, vp_ref, vd_ref, mask_ref,
             dq_ref, dk_ref, dv_ref, dvd_ref, tile_ref, acc_ref, carry_ref):
        i = pl.program_id(1)

        @pl.when(i == 0)
        def _():
            _bias_tile(vd_ref, mask_ref, tile_ref)
            acc_ref[...] = jnp.zeros_like(acc_ref)

            carry_ref[...] = jnp.zeros_like(carry_ref)

        scale = HEAD_DIM ** -0.5
        first = jnp.where(i == nb - 1, 2, 0)
        dqs, dks, dvs = [], [], []
        for hh in range(2):
            sl = slice(hh * HEAD_DIM, (hh + 1) * HEAD_DIM)
            q = (q_ref[:, sl] * scale).astype(BF16)
            kc = jnp.concatenate([kp_ref[:, sl], k_ref[:, sl]], axis=0).astype(BF16)
            vc = jnp.concatenate([vp_ref[:, sl], v_ref[:, sl]], axis=0).astype(BF16)
            do = do_ref[:, sl].astype(BF16)
            bands = _attn_probs(q, kc, tile_ref, first + hh)
            p = _from_bands([b.astype(BF16) for b in bands])
            dvs.append(lax.dot_general(p, do, (((0,), (0,)), ((), ())), preferred_element_type=F32))
            dp = lax.dot_general(do, vc, (((1,), (1,)), ((), ())), preferred_element_type=F32)
            ds_bands = []
            for r, pb in enumerate(bands):
                dpb = _band(dp, r)
                dsb = pb * (dpb - jnp.sum(dpb * pb, axis=-1, keepdims=True))
                acc_ref[hh, BAND_ROWS * r:BAND_ROWS * (r + 1), BAND_ROWS * r:BAND_ROWS * r + BAND_COLS] += dsb
                ds_bands.append(dsb.astype(BF16))
            dsq = _from_bands(ds_bands)
            dqs.append(jnp.dot(dsq, kc, preferred_element_type=F32) * scale)
            dks.append(lax.dot_general(dsq, q, (((0,), (0,)), ((), ())), preferred_element_type=F32))
        dq_ref[...] = jnp.concatenate(dqs, axis=1)
        dk = jnp.concatenate(dks, axis=1)
        dv = jnp.concatenate(dvs, axis=1)
        dk_ref[...] = dk[T_ATT:] + carry_ref[0]
        dv_ref[...] = dv[T_ATT:] + carry_ref[1]
        carry_ref[0] = dk[:T_ATT]
        carry_ref[1] = dv[:T_ATT]

        @pl.when(i == nb - 1)
        def _():
            for hh in range(2):
                r8 = acc_ref[hh, 0:8, :]
                for mrow in range(1, T_ATT // 8):
                    r8 = r8 + pltpu.roll(acc_ref[hh, 8 * mrow:8 * mrow + 8, :], NKEY - 8 * mrow, 1)
                tot = r8[0:1, :]
                for s in range(1, 8):
                    tot = tot + pltpu.roll(r8[s:s + 1, :], NKEY - s, 1)
                dvd_ref[0, hh:hh + 1, :] = tot

    block = lambda i: nb - 1 - i
    out = pl.BlockSpec((T_ATT, 128), lambda hp, i: (block(i), hp))
    return pl.pallas_call(
        body, name=name, grid=(ATT_HEADS // 2, nb),
        in_specs=[out] + _attn_specs(block),
        out_specs=[out, out, out, pl.BlockSpec((1, 2, NKEY), lambda hp, i: (hp, 0, 0))],
        out_shape=[jax.ShapeDtypeStruct((m, MIX), F32)] * 3 + [jax.ShapeDtypeStruct((ATT_HEADS // 2, 2, NKEY), F32)],
        scratch_shapes=[pltpu.VMEM((4, T_ATT, NKEY), F32), pltpu.VMEM((2, T_ATT, NKEY), F32), pltpu.VMEM((2, T_ATT, 128), F32)],
        compiler_params=_cparams(("parallel", "arbitrary"), VMEM_BIG),
    )(dcat, h, h, h, h, h, vdiag, mask)


def _row_tile(rows):
    for t in (512, 256, 128, 64, 32, 16, 8):
        if rows % t == 0:
            return t
    return rows


def _add_n(arrs, coefs, *, name, also_bf16=False):
    rows, cols = arrs[0].shape
    t = _row_tile(rows)
    n = len(arrs)

    def body(*refs):
        acc = None
        for r, cf in zip(refs[:n], coefs):
            v = r[...].astype(F32)
            v = v if cf == 1.0 else cf * v
            acc = v if acc is None else acc + v
        refs[n][...] = acc
        if also_bf16:
            refs[n + 1][...] = acc.astype(BF16)

    spec = pl.BlockSpec((t, cols), lambda i: (i, 0))
    f32 = jax.ShapeDtypeStruct((rows, cols), F32)
    return pl.pallas_call(
        body, name=name, grid=(rows // t,), in_specs=[spec] * n,
        out_specs=[spec, spec] if also_bf16 else spec,
        out_shape=[f32, jax.ShapeDtypeStruct((rows, cols), BF16)] if also_bf16 else f32,
        compiler_params=_cparams(("parallel",)),
    )(*arrs)


def _adamw(w, g, mom, var, *, name):
    rows, cols = w.shape
    t = _row_tile(rows)

    def body(w_ref, g_ref, m_ref, v_ref, d_ref, mo_ref, vo_ref):
        g_ = g_ref[...]
        m_ = ADAM_B1 * m_ref[...] + (1.0 - ADAM_B1) * g_
        v_ = ADAM_B2 * v_ref[...] + (1.0 - ADAM_B2) * (g_ * g_)
        m_hat = m_ / (1.0 - ADAM_B1 ** ADAM_STEP)
        v_hat = v_ / (1.0 - ADAM_B2 ** ADAM_STEP)
        d_ref[...] = -ADAM_LR * (m_hat / (jnp.sqrt(v_hat) + ADAM_EPS) + ADAM_WD * w_ref[...])
        mo_ref[...] = m_
        vo_ref[...] = v_

    spec = pl.BlockSpec((t, cols), lambda i: (i, 0))
    return pl.pallas_call(
        body, name=name, grid=(rows // t,), in_specs=[spec] * 4, out_specs=[spec] * 3,
        out_shape=[jax.ShapeDtypeStruct((rows, cols), F32)] * 3, compiler_params=_cparams(("parallel",)),
    )(w, g, mom, var)


ANY = pl.BlockSpec(memory_space=pl.ANY)


def _place():
    x, y, c = lax.axis_index("x"), lax.axis_index("y"), lax.axis_index("c")
    chips = [(1 - x, y), (x, 1 - y), (1 - x, 1 - y)]
    return x, y, c, chips


def _gather_chips(ws, *, name):
    n = len(ws)

    def body(*refs):
        ins, outs = refs[:n], refs[n:2 * n]
        send_sems, recv_sems, local_sems = refs[2 * n:]
        x, y, c, chips = _place()
        me = 2 * x + y
        sibling = (x, y, 1 - c)

        def remote(k, j, chip_index, rows, to):
            region = outs[k].at[chip_index, rows]
            return pltpu.make_async_remote_copy(
                src_ref=region, dst_ref=region, send_sem=send_sems.at[6 * k + j], recv_sem=recv_sems.at[6 * k + j],
                device_id=to, device_id_type=MESH)

        local, sent = [], []
        for k in range(n):
            half = ins[k].shape[0] // 2
            mine = pl.ds(c * half, half)
            local.append(pltpu.make_async_copy(ins[k], outs[k].at[me], local_sems.at[k]))
            local[-1].start()
            for j, chip in enumerate(chips):
                sent.append(pltpu.make_async_remote_copy(
                    src_ref=ins[k].at[mine], dst_ref=outs[k].at[me, mine], send_sem=send_sems.at[6 * k + j],
                    recv_sem=recv_sems.at[6 * k + j], device_id=(*chip, c), device_id_type=MESH))
                sent[-1].start()
        for k in range(n):
            half = ins[k].shape[0] // 2
            mine = pl.ds(c * half, half)
            for j, chip in enumerate(chips):
                remote(k, j, 2 * chip[0] + chip[1], mine, (*chip, c)).wait_recv()
                sent.append(remote(k, 3 + j, 2 * chip[0] + chip[1], mine, sibling))
                sent[-1].start()
        for k in range(n):
            half = ins[k].shape[0] // 2
            theirs = pl.ds((1 - c) * half, half)
            for j, chip in enumerate(chips):
                remote(k, 3 + j, 2 * chip[0] + chip[1], theirs, sibling).wait_recv()
        for cp in sent:
            cp.wait_send()
        for cp in local:
            cp.wait()

    return pl.pallas_call(
        body, name=name, in_specs=[ANY] * n, out_specs=[ANY] * n,
        out_shape=[jax.ShapeDtypeStruct((N_CHIPS,) + w.shape, w.dtype) for w in ws],
        scratch_shapes=[pltpu.SemaphoreType.DMA((6 * n,)), pltpu.SemaphoreType.DMA((6 * n,)), pltpu.SemaphoreType.DMA((n,))],
    )(*ws)


def _swap_sibling(arrs, *, name):
    n = len(arrs)

    def body(*refs):
        ins, outs = refs[:n], refs[n:2 * n]
        send_sems, recv_sems = refs[2 * n:]
        x, y, c, _ = _place()
        cps = [pltpu.make_async_remote_copy(src_ref=ins[k], dst_ref=outs[k], send_sem=send_sems.at[k], recv_sem=recv_sems.at[k],
                                            device_id=(x, y, 1 - c), device_id_type=MESH) for k in range(n)]
        for cp in cps:
            cp.start()
        for cp in cps:
            cp.wait_recv()
        for cp in cps:
            cp.wait_send()

    return pl.pallas_call(
        body, name=name, in_specs=[ANY] * n, out_specs=[ANY] * n,
        out_shape=[jax.ShapeDtypeStruct(a.shape, a.dtype) for a in arrs],
        scratch_shapes=[pltpu.SemaphoreType.DMA((n,)), pltpu.SemaphoreType.DMA((n,))],
    )(*arrs)


def _scatter_chips(ps, *, name):
    n = len(ps)

    def body(*refs):
        ins, outs = refs[:n], refs[n:2 * n]
        send_sems, recv_sems = refs[2 * n:]
        x, y, c, chips = _place()
        cps = []
        for k in range(n):
            for j, chip in enumerate(chips):
                cps.append(pltpu.make_async_remote_copy(
                    src_ref=ins[k].at[2 * chip[0] + chip[1]], dst_ref=outs[k].at[j], send_sem=send_sems.at[3 * k + j],
                    recv_sem=recv_sems.at[3 * k + j], device_id=(*chip, c), device_id_type=MESH))
        for cp in cps:
            cp.start()
        for cp in cps:
            cp.wait_recv()
        for cp in cps:
            cp.wait_send()

    return pl.pallas_call(
        body, name=name, in_specs=[ANY] * n, out_specs=[ANY] * n,
        out_shape=[jax.ShapeDtypeStruct((3,) + p.shape[1:], p.dtype) for p in ps],
        scratch_shapes=[pltpu.SemaphoreType.DMA((3 * n,)), pltpu.SemaphoreType.DMA((3 * n,))],
    )(*ps)


def _all_reduce_small(buf, *, name):
    rows = buf.shape[0]

    def body(x_ref, sum_ref, all_ref, send_sems, recv_sems, local_sem):
        x, y, c, chips = _place()
        me, sibling = (x, y, c), (x, y, 1 - c)

        def slab(px, py, pc):
            return all_ref.at[pl.ds((4 * px + 2 * py + pc) * rows, rows), :]

        def copy(k, block, to, src=None):
            return pltpu.make_async_remote_copy(
                src_ref=slab(*block) if src is None else src, dst_ref=slab(*block), send_sem=send_sems.at[k],
                recv_sem=recv_sems.at[k], device_id=to, device_id_type=MESH)

        mine = pltpu.make_async_copy(x_ref, slab(*me), local_sem)
        mine.start()
        first = [copy(0, me, sibling, src=x_ref)]
        first += [copy(1 + j, me, (*chip, c), src=x_ref) for j, chip in enumerate(chips)]
        for cp in first:
            cp.start()
        passed = [copy(4 + j, (*chip, c), sibling) for j, chip in enumerate(chips)]
        for j, chip in enumerate(chips):
            copy(1 + j, (*chip, c), me).wait_recv()
            passed[j].start()
        copy(0, sibling, me).wait_recv()
        for j, chip in enumerate(chips):
            copy(4 + j, (*chip, 1 - c), me).wait_recv()
        for cp in first + passed:
            cp.wait_send()
        mine.wait()
        acc = all_ref[0:rows, :]
        for d in range(1, N_DEV):
            acc = acc + all_ref[d * rows:(d + 1) * rows, :]
        sum_ref[...] = acc

    vmem = pl.BlockSpec(memory_space=pltpu.VMEM)
    return pl.pallas_call(
        body, name=name, in_specs=[vmem], out_specs=[vmem, vmem],
        out_shape=[jax.ShapeDtypeStruct((rows, 128), F32), jax.ShapeDtypeStruct((N_DEV * rows, 128), F32)],
        scratch_shapes=[pltpu.SemaphoreType.DMA((7,)), pltpu.SemaphoreType.DMA((7,)), pltpu.SemaphoreType.DMA],
        compiler_params=pltpu.CompilerParams(vmem_limit_bytes=VMEM_BIG),
    )(buf)[0]


WEIGHTS = ['ev_w_in', 'ev_lambda_re', 'ev_lambda_im', 'ev_log_dt', 'ev_b_re', 'ev_b_im', 'ev_c_re', 'ev_c_im', 'ev_d',
           'ev_w_glu', 'ev_b_glu', 'ev_conv_w', 'ev_w_out', 'od_w_in', 'od_rel_bias', 'od_pool_w', 'od_pool_scale',
           'od_w_out', 'ln_mix_g', 'ln_mix_b', 'ln_ffn_g', 'ln_ffn_b', 'ffn_w_up', 'ffn_w_down', 'ple_w_proj',
           'ple_w_gate', 'ple_b_gate']
INPUTS = ['x', 'p'] + WEIGHTS + ['loss_target'] + ['m_' + n for n in WEIGHTS] + ['v_' + n for n in WEIGHTS]

BIG = {
    'ev_w_in': (2, (2, 1024, 2048)), 'ev_w_glu': (1, (2, 512, 512)), 'ev_w_out': (1, (2, 1024, 1024)),
    'od_w_in': (2, (2, 1024, 2048)), 'od_w_out': (1, (2, 1024, 1024)), 'ffn_w_up': (2, (4, 1024, 5632)),
    'ffn_w_down': (1, (4, 2816, 1024)), 'ple_w_proj': (2, (4, 256, 1024)), 'ple_w_gate': (1, (4, 1024, 1024)),
}
SMALL_SHARDED = {'ev_conv_w': (2, 3, 512), 'od_pool_scale': (2, 512)}
REPLICATED = [n for n in WEIGHTS if n not in BIG and n not in SMALL_SHARDED]


def _shard_rows(name):
    axis, (nl, k, n) = BIG[name]
    return (nl * k, n // N_CHIPS) if axis == 2 else (nl * k // N_CHIPS, n)


def _unstack(name, st):
    axis, (nl, k, n) = BIG[name]
    if axis == 2:
        return st.reshape(N_CHIPS, nl, k, n // N_CHIPS).transpose(1, 2, 0, 3).reshape(nl, k, n)
    return st.reshape(N_CHIPS, nl, k // N_CHIPS, n).transpose(1, 0, 2, 3).reshape(nl, k, n)


def _stack(name, full):
    axis, (nl, k, n) = BIG[name]
    rows, cols = _shard_rows(name)
    if axis == 2:
        return full.reshape(nl, k, N_CHIPS, n // N_CHIPS).transpose(2, 0, 1, 3).reshape(N_CHIPS, rows, cols)
    return full.reshape(nl, N_CHIPS, k // N_CHIPS, n).transpose(1, 0, 2, 3).reshape(N_CHIPS, rows, cols)


def _pack(arrs):
    flat = jnp.concatenate([a.reshape(-1) for a in arrs])
    total = flat.shape[0]
    padded = -(-total // 1024) * 1024
    return jnp.pad(flat, (0, padded - total)).reshape(padded // 128, 128)


def _unpack(buf, shapes):
    flat = buf.reshape(-1)
    out, pos = [], 0
    for s in shapes:
        size = int(np.prod(s))
        out.append(flat[pos:pos + size].reshape(s))
        pos += size
    return out


def _s5_params(lam_re, lam_im, log_dt, b_re, b_im, c_re, c_im):
    dt = jnp.exp(log_dt)[:, None]
    mag = jnp.exp(lam_re * dt)
    ang = lam_im * dt
    lb_re = mag * jnp.cos(ang)
    lb_im = mag * jnp.sin(ang)
    den = lam_re * lam_re + lam_im * lam_im
    nr = lb_re - 1.0
    ni = lb_im
    r_re = (nr * lam_re + ni * lam_im) / den
    r_im = (ni * lam_re - nr * lam_im) / den
    bb_re = r_re[..., None] * b_re - r_im[..., None] * b_im
    bb_im = r_re[..., None] * b_im + r_im[..., None] * b_re
    per = S5_GROUPS // S5_SLABS
    eye = jnp.eye(per, dtype=F32)

    def block_diag(a):
        _, r, c = a.shape
        a = a.reshape(S5_SLABS, per, r, c)
        return (a[:, :, :, None, :] * eye[None, :, None, :, None]).reshape(S5_SLABS, per * r, per * c)

    bmat = jnp.concatenate([block_diag(bb_re.transpose(0, 2, 1)), block_diag(bb_im.transpose(0, 2, 1))], axis=2)
    cmat = jnp.concatenate([block_diag(c_re.transpose(0, 2, 1)), block_diag(-c_im.transpose(0, 2, 1))], axis=1)
    lam = jnp.stack([lb_re.reshape(S5_N), lb_im.reshape(S5_N)])
    return lam, bmat, cmat


def _lam_powers(lam):
    res, ims = [lam[0]], [lam[1]]
    for _ in range(7):
        res, ims = res + [res[-1] * lam[0] - ims[-1] * lam[1]], ims + [res[-1] * lam[1] + ims[-1] * lam[0]]
    return jnp.stack(res + ims + res[::-1] + ims[::-1])


def _local_step(x, p, target, w):
    mask = jnp.asarray(_band_mask())
    diag_idx = _diag_index()
    onehot = jnp.asarray(np.eye(2 * MAX_REL + 1, dtype=np.float32)[diag_idx])
    saved = []
    for i in range(DEPTH):
        li = i // 2
        s = {'x0': x}
        if i % 2 == 0:
            (lam, bmat, cmat), s5_vjp = jax.vjp(
                _s5_params, w['ev_lambda_re'][li], w['ev_lambda_im'][li], w['ev_log_dt'][li], w['ev_b_re'][li],
                w['ev_b_im'][li], w['ev_c_re'][li], w['ev_c_im'][li])
            s5c = (bmat.astype(BF16), cmat.astype(BF16), w['ev_d'][li].reshape(1, MIX), w['ev_w_glu'][li],
                   w['ev_b_glu'][li].reshape(1, MIX), _lam_powers(lam))
            h = _mm([(x, 0, D_MODEL)], w['ev_w_in'][li], name=f"in_proj")
            ya, ypre, hb = _s5_fwd(h, *s5c, name=f"s5_fwd")
            yb = _conv_fwd(h, w['ev_conv_w'][li], name=f"conv_fwd")
            wout = w['ev_w_out'][li]
            s.update(s5_vjp=s5_vjp, s5c=s5c, ypre=ypre, hb=hb)
        else:
            vdiag = jnp.dot(w['od_rel_bias'][li], onehot.T, precision=HIGHEST).reshape(ATT_HEADS // 2, 2, NKEY)
            pw = w['od_pool_w'][li].astype(BF16)
            ps = w['od_pool_scale'][li].reshape(1, MIX)
            h = _mm([(x, 0, D_MODEL)], w['od_w_in'][li], name=f"in_proj")
            ya = _attn_fwd(h, vdiag, mask, name=f"attn_fwd")
            yb = _pool_fwd(h, pw, ps, name=f"pool_fwd")
            wout = w['od_w_out'][li]
            s.update(vdiag=vdiag, pw=pw, ps=ps)
        vec = lambda n: w[n][i].reshape(1, -1)

        def residual_ln(products, rows, vecs):
            r = ALPHA * rows[0] + products[0]
            return (r, _ln_apply(r, vecs[0], vecs[1])), ()

        def embed_gate(products, rows, vecs):
            gate = _sigmoid(products[0] + vecs[0])
            return (rows[0] + gate * products[1], gate, products[1]), ()

        two_f32 = [(D_MODEL, F32), (D_MODEL, F32)]
        r1, x1 = _mm_rows([([(ya, 0, MIX), (yb, 0, MIX)], wout, False)], [x], [vec('ln_mix_g'), vec('ln_mix_b')],
                          two_f32, [], residual_ln, name="out_proj_ln")
        a, gg, uu = _ffn_up(x1, w['ffn_w_up'][i], name=f"ffn_up")
        r2, x2 = _mm_rows([([(a, 0, D_FF)], w['ffn_w_down'][i], False)], [x1], [vec('ln_ffn_g'), vec('ln_ffn_b')],
                          two_f32, [], residual_ln, name="ffn_down_ln")
        x3, gate, ppb = _mm_rows(
            [([(x2, 0, D_MODEL)], w['ple_w_gate'][i], False), ([(p[i], 0, D_PLE)], w['ple_w_proj'][i], False)],
            [x2], [vec('ple_b_gate')], [(D_MODEL, F32), (D_MODEL, BF16), (D_MODEL, BF16)], [], embed_gate, name="ple")
        s.update(h=h, ya=ya, yb=yb, wout=wout, r1=r1, x1=x1, a=a, gg=gg, uu=uu, r2=r2, x2=x2, gate=gate, ppb=ppb)
        saved.append(s)
        x = x3

    loss, da = _loss_head(x, target, name="loss_head")
    db = None
    grads = {n: [None] * (DEPTH if n.startswith(('ln_', 'ffn_', 'ple_')) else DEPTH // 2) for n in WEIGHTS}
    for i in reversed(range(DEPTH)):
        li = i // 2
        s = saved[i]
        dz, dpp, dr2, dbg, dg2, db2 = _ple_ln_bwd(da, db, s['gate'], s['ppb'], s['r2'], w['ple_w_gate'][i],
                                                  w['ln_ffn_g'][i].reshape(1, -1), name="ple_ln_bwd")
        grads['ple_b_gate'][i] = dbg.reshape(-1)
        grads['ple_w_gate'][i] = _mm_tn(s['x2'], 0, D_MODEL, dz, name=f"d_ple_gate")
        grads['ple_w_proj'][i] = _mm_tn(p[i], 0, D_PLE, dpp, name=f"d_ple_proj")
        grads['ln_ffn_g'][i] = dg2.reshape(-1)
        grads['ln_ffn_b'][i] = db2.reshape(-1)
        dhh = _ffn_down_bwd(dr2, w['ffn_w_down'][i], s['gg'], s['uu'], name=f"ffn_down_bwd")
        grads['ffn_w_down'][i] = _mm_tn(s['a'], 0, D_FF, dr2, tk=D_FF // 2, name=f"d_ffn_down")
        grads['ffn_w_up'][i] = _mm_tn(s['x1'], 0, D_MODEL, dhh, tn=D_FF // 2, name=f"d_ffn_up")

        def ln_mix_grad(products, rows, vecs):
            dr, dg, dbias = _ln_grad(rows[0], ALPHA * rows[1] + products[0], vecs[0])
            return (dr,), (dg, dbias)

        dr1, dg1, db1 = _mm_rows([([(dhh, 0, 2 * D_FF)], w['ffn_w_up'][i], True)], [s['r1'], dr2],
                                 [w['ln_mix_g'][i].reshape(1, -1)], [(D_MODEL, F32)], [D_MODEL, D_MODEL], ln_mix_grad,
                                 tm=256, vmem=VMEM_BIG, name="ffn_up_ln_bwd")
        grads['ln_mix_g'][i] = dg1.reshape(-1)
        grads['ln_mix_b'][i] = db1.reshape(-1)
        dcat = _mm([(dr1, 0, D_MODEL)], s['wout'], trans_b=True, name=f"out_proj_bwd")
        dwout = jnp.concatenate([_mm_tn(s['ya'], 0, MIX, dr1, name=f"d_out_a"),
                                 _mm_tn(s['yb'], 0, MIX, dr1, name=f"d_out_b")], axis=0)
        h = s['h']
        if i % 2 == 0:
            s5c = s['s5c']
            du, xb, gb, gq, dzzq, dyq, dlam, dbglu, dd = _s5_bwd(dcat, s['ypre'], h, s['hb'], *s5c, name=f"s5_bwd")
            dbmat = _mm_tn_slabs(h, 128, gb, SLAB_COLS, S5_SLABS, name=f"d_s5_b")
            dcmat = _mm_tn_slabs(xb, SLAB_COLS, dyq, 128, S5_SLABS, name=f"d_s5_c")
            s5g = s['s5_vjp']((dlam, dbmat, dcmat))
            for n, g_ in zip(['ev_lambda_re', 'ev_lambda_im', 'ev_log_dt', 'ev_b_re', 'ev_b_im', 'ev_c_re', 'ev_c_im'], s5g):
                grads[n][li] = g_
            grads['ev_w_glu'][li] = _mm_tn(gq, 0, MIX, dzzq, name=f"d_glu")
            grads['ev_b_glu'][li] = dbglu.reshape(-1)
            grads['ev_d'][li] = dd.reshape(-1)
            d3, dcw = _conv_bwd(dcat, h, w['ev_conv_w'][li], name=f"conv_bwd")
            grads['ev_conv_w'][li] = dcw[0:3]
            grads['ev_w_out'][li] = dwout
            grads['ev_w_in'][li] = jnp.concatenate(
                [_mm_tn(s['x0'], 0, D_MODEL, du, name=f"d_in_a"), _mm_tn(s['x0'], 0, D_MODEL, d3, tn=3 * MIX, name=f"d_in_b")],
                axis=1)
            db = _mm([(du, 0, MIX), (d3, 0, 3 * MIX)], w['ev_w_in'][li], trans_b=True, name=f"in_proj_bwd")
        else:
            dq, dk, dv, dvd = _attn_bwd(dcat, h, s['vdiag'], mask, name=f"attn_bwd")
            dzp, dpw, dps = _pool_bwd(dcat, h, s['pw'], s['ps'], name=f"pool_bwd")
            parts = [dq, dk, dv, dzp]
            grads['od_rel_bias'][li] = jnp.dot(dvd.reshape(ATT_HEADS, NKEY), onehot, precision=HIGHEST)
            grads['od_pool_w'][li] = dpw
            grads['od_pool_scale'][li] = dps.reshape(-1)
            grads['od_w_out'][li] = dwout
            grads['od_w_in'][li] = jnp.concatenate([_mm_tn(s['x0'], 0, D_MODEL, d_, name=f"d_in_a") for d_ in parts], axis=1)
            db = _mm([(d_, 0, MIX) for d_ in parts], w['od_w_in'][li], trans_b=True, name=f"in_proj_bwd")
        da = dr1
    grad_x = _add_n([da, db], [ALPHA, 1.0], name="grad_x")
    return loss, grad_x, {n: jnp.stack(g) for n, g in grads.items()}


def _reduce_big(grads, c, me):
    names = list(BIG)
    mine, other = [], []
    for n in names:
        st = _stack(n, grads[n])
        half = st.shape[1] // 2
        mine.append(lax.dynamic_slice_in_dim(st, c * half, half, axis=1))
        other.append(lax.dynamic_slice_in_dim(st, (1 - c) * half, half, axis=1))
    from_sibling = _swap_sibling(other, name="grad_pair_swap")
    chip_sums, to_send = [], []
    for n, a, b in zip(names, mine, from_sibling):
        k, half, cols = a.shape
        s32, s16 = _add_n([a.reshape(k * half, cols), b.reshape(k * half, cols)], [1.0, 1.0], also_bf16=True,
                          name=f"grad_pair_add_{n}")
        chip_sums.append(s32.reshape(k, half, cols))
        to_send.append(s16.reshape(k, half, cols))
    from_chips = _scatter_chips(to_send, name="grad_chip_scatter")
    halves = []
    for n, own, got in zip(names, chip_sums, from_chips):
        halves.append(_add_n([lax.dynamic_index_in_dim(own, me, 0, keepdims=False), got[0], got[1], got[2]], [1.0] * 4,
                             name=f"grad_chip_add_{n}"))
    from_sibling = _swap_sibling(halves, name="grad_half_swap")
    out = {}
    for n, a, b in zip(names, halves, from_sibling):
        out[n] = jnp.where(c == 0, jnp.concatenate([a, b], axis=0), jnp.concatenate([b, a], axis=0))
    return out


def kernel(x, p, ev_w_in, ev_lambda_re, ev_lambda_im, ev_log_dt, ev_b_re, ev_b_im, ev_c_re, ev_c_im, ev_d, ev_w_glu, ev_b_glu, ev_conv_w, ev_w_out, od_w_in, od_rel_bias, od_pool_w, od_pool_scale, od_w_out, ln_mix_g, ln_mix_b, ln_ffn_g, ln_ffn_b, ffn_w_up, ffn_w_down, ple_w_proj, ple_w_gate, ple_b_gate, loss_target, m_ev_w_in, m_ev_lambda_re, m_ev_lambda_im, m_ev_log_dt, m_ev_b_re, m_ev_b_im, m_ev_c_re, m_ev_c_im, m_ev_d, m_ev_w_glu, m_ev_b_glu, m_ev_conv_w, m_ev_w_out, m_od_w_in, m_od_rel_bias, m_od_pool_w, m_od_pool_scale, m_od_w_out, m_ln_mix_g, m_ln_mix_b, m_ln_ffn_g, m_ln_ffn_b, m_ffn_w_up, m_ffn_w_down, m_ple_w_proj, m_ple_w_gate, m_ple_b_gate, v_ev_w_in, v_ev_lambda_re, v_ev_lambda_im, v_ev_log_dt, v_ev_b_re, v_ev_b_im, v_ev_c_re, v_ev_c_im, v_ev_d, v_ev_w_glu, v_ev_b_glu, v_ev_conv_w, v_ev_w_out, v_od_w_in, v_od_rel_bias, v_od_pool_w, v_od_pool_scale, v_od_w_out, v_ln_mix_g, v_ln_mix_b, v_ln_ffn_g, v_ln_ffn_b, v_ffn_w_up, v_ffn_w_down, v_ple_w_proj, v_ple_w_gate, v_ple_b_gate):
    given = locals()
    a = {n: given[n] for n in INPUTS}
    x, y, c = lax.axis_index("x"), lax.axis_index("y"), lax.axis_index("c")
    me = 2 * x + y

    misc = jnp.concatenate([a['ev_conv_w'].reshape(6, 128), a['od_pool_scale'], jnp.zeros((8, 128), F32)], axis=0)
    gathered = _gather_chips([a[n].astype(BF16).reshape(_shard_rows(n)) for n in BIG] + [misc], name="weight_gather")
    w = {n: _unstack(n, g) for n, g in zip(BIG, gathered)}
    gm = gathered[-1]
    w['ev_conv_w'] = gm[:, 0:6].reshape(N_CHIPS, 2, 3, 128).transpose(1, 2, 0, 3).reshape(2, 3, 512)
    w['od_pool_scale'] = gm[:, 6:8].transpose(1, 0, 2).reshape(2, 512)
    for n in REPLICATED:
        w[n] = a[n]

    loss, grad_x, grads = _local_step(a['x'][0], a['p'][:, 0], a['loss_target'][0], w)
    loss = lax.psum(loss[0, 0], ("x", "y", "c"))

    small_names = REPLICATED + list(SMALL_SHARDED)
    small = _all_reduce_small(_pack([grads[n] for n in small_names]), name="small_grad_all_reduce")
    small = dict(zip(small_names, _unpack(small, [grads[n].shape for n in small_names])))
    for n in SMALL_SHARDED:
        small[n] = lax.dynamic_slice_in_dim(small[n], me * 128, 128, axis=small[n].ndim - 1)
    big = _reduce_big(grads, c, me)

    res = {}
    for n in BIG:
        shape = a[n].shape
        d, m_, v_ = _adamw(a[n].reshape(big[n].shape), big[n], a['m_' + n].reshape(big[n].shape),
                           a['v_' + n].reshape(big[n].shape), name=f"adamw_{n}")
        res[n] = (big[n].reshape(shape), d.reshape(shape), m_.reshape(shape), v_.reshape(shape))
    shapes = [a[n].shape for n in small_names]
    d, m_, v_ = _adamw(_pack([a[n] for n in small_names]), _pack([small[n] for n in small_names]),
                       _pack([a['m_' + n] for n in small_names]), _pack([a['v_' + n] for n in small_names]), name="adamw_small")
    for n, dd, mm, vv in zip(small_names, _unpack(d, shapes), _unpack(m_, shapes), _unpack(v_, shapes)):
        res[n] = (small[n], dd, mm, vv)

    outs = [loss, grad_x[None]]
    for part in range(4):
        outs += [res[n][part] for n in WEIGHTS]
    return tuple(outs)
```

```python
import functools
import math

import jax
import jax.numpy as jnp
import numpy as np
from jax import lax
from jax.experimental import pallas as pl
from jax.experimental.pallas import tpu as pltpu

F32 = jnp.float32
BF16 = jnp.bfloat16
MESH = pl.DeviceIdType.MESH
HIGHEST = lax.Precision.HIGHEST

D_MODEL = 1024
DEPTH = 4
MIX = 512
S5_GROUPS = 32
S5_GROUP = 16
S5_STATE = 64
S5_N = S5_GROUPS * S5_STATE
CHUNK = 64
LEFT_CHUNKS = 8
MAX_REL = 128
ATT_HEADS = 8
HEAD_DIM = 64
POOL_WINDOWS = (2, 4, 8, 16)
POOL_GROUP = 128
D_FF = 2816
D_PLE = 256
ALPHA = (2 * DEPTH) ** 0.25
LN_EPS = 1e-5
NEG_INF = -1e30
N_CHIPS = 4
N_DEV = 8

ADAM_LR = 0.001
ADAM_B1 = 0.9
ADAM_B2 = 0.999
ADAM_EPS = 1e-08
ADAM_WD = 0.01
ADAM_STEP = 10

TM = 512
T_S5 = 256
T_ATT = 512
VMEM_BIG = 56 * 1024 * 1024


VMEM_DEFAULT = 48 * 1024 * 1024


def _cparams(sem, vmem=None):
    return pltpu.CompilerParams(dimension_semantics=sem, vmem_limit_bytes=vmem or VMEM_DEFAULT)


def _sigmoid(x):
    return 1.0 / (1.0 + jnp.exp(-x))


def _mm(a_parts, b, *, name, trans_b=False, out_dtype=F32, tm=TM, tn=1024, vmem=None):
    m = a_parts[0][0].shape[0]
    n = b.shape[0] if trans_b else b.shape[1]
    kk = b.shape[1] if trans_b else b.shape[0]
    tn = min(tn, n)
    widths = [w for _, _, w in a_parts]
    assert sum(widths) == kk and m % tm == 0 and n % tn == 0
    na = len(a_parts)

    def body(*refs):
        b_ref, o_ref = refs[na], refs[na + 1]
        acc = None
        k0 = 0
        for ar, w in zip(refs[:na], widths):
            a = ar[...].astype(BF16)
            if trans_b:
                part = lax.dot_general(a, b_ref[:, k0:k0 + w], (((1,), (1,)), ((), ())), preferred_element_type=F32)
            else:
                part = jnp.dot(a, b_ref[k0:k0 + w, :], preferred_element_type=F32)
            acc = part if acc is None else acc + part
            k0 += w
        o_ref[...] = acc.astype(o_ref.dtype)

    in_specs = [pl.BlockSpec((tm, w), functools.partial(lambda j, i, cb: (i, cb), cb=cb)) for _, cb, w in a_parts]
    if trans_b:
        in_specs.append(pl.BlockSpec((tn, kk), lambda j, i: (j, 0)))
    else:
        in_specs.append(pl.BlockSpec((kk, tn), lambda j, i: (0, j)))
    return pl.pallas_call(
        body, name=name, grid=(n // tn, m // tm), in_specs=in_specs,
        out_specs=pl.BlockSpec((tm, tn), lambda j, i: (i, j)),
        out_shape=jax.ShapeDtypeStruct((m, n), out_dtype),
        compiler_params=_cparams(("parallel", "parallel"), vmem),
    )(*[a for a, _, _ in a_parts], b)


def _host_parts(exchange):
    if exchange is None:
        return [], [], [], [], []
    any_space = pl.BlockSpec(memory_space=pl.ANY)
    return (exchange.ins, [any_space] * len(exchange.ins), [any_space] * len(exchange.out_shapes),
            list(exchange.out_shapes), list(exchange.sems))


def _host_run(exchange, refs, first, last):
    if exchange is None:
        return
    n_in, n_out = len(exchange.ins), len(exchange.out_shapes)
    ins, outs, sems = refs[:n_in], refs[n_in:n_in + n_out], refs[n_in + n_out:]

    @pl.when(first)
    def _():
        exchange.start(ins, outs, sems)

    @pl.when(last)
    def _():
        exchange.finish(ins, outs, sems)


def _mm_tn(a, a_cb, ka, b, *, name, tk=1024, tn=1024, tmr=2 * TM, vmem=None, also_bf16=False, exchange=None):
    m = a.shape[0]
    n = b.shape[1]
    tk = min(tk, ka)
    tn = min(tn, n)
    assert ka % tk == 0 and n % tn == 0 and m % tmr == 0
    kb = ka // tk
    grid = (kb, n // tn, m // tmr)
    ex_ops, ex_in_specs, ex_out_specs, ex_out_shapes, ex_scratch = _host_parts(exchange)
    n_own_out = 2 if also_bf16 else 1

    def body(*refs):
        a_ref, b_ref = refs[:2]
        hosted_in = refs[2:2 + len(ex_ops)]
        outs = refs[2 + len(ex_ops):]
        o_ref = outs[0]
        k, j, r = pl.program_id(0), pl.program_id(1), pl.program_id(2)
        _host_run(exchange, list(hosted_in) + list(outs[n_own_out:]),
                  (k == 0) & (j == 0) & (r == 0), (k == grid[0] - 1) & (j == grid[1] - 1) & (r == grid[2] - 1))

        @pl.when(r == 0)
        def _():
            o_ref[...] = jnp.zeros_like(o_ref)

        o_ref[...] += lax.dot_general(a_ref[...].astype(BF16), b_ref[...].astype(BF16), (((0,), (0,)), ((), ())),
                                      preferred_element_type=F32)
        if also_bf16:
            @pl.when(r == grid[2] - 1)
            def _():
                outs[1][...] = o_ref[...].astype(BF16)

    tile = pl.BlockSpec((tk, tn), lambda k, j, r: (k, j))
    res = pl.pallas_call(
        body, name=name, grid=grid,
        in_specs=[pl.BlockSpec((tmr, tk), lambda k, j, r: (r, a_cb * kb + k)),
                  pl.BlockSpec((tmr, tn), lambda k, j, r: (r, j))] + ex_in_specs,
        out_specs=[tile] * n_own_out + ex_out_specs,
        out_shape=[jax.ShapeDtypeStruct((ka, n), F32)] + ([jax.ShapeDtypeStruct((ka, n), BF16)] if also_bf16 else [])
        + ex_out_shapes,
        scratch_shapes=ex_scratch,
        compiler_params=_cparams(("arbitrary",) * 3 if exchange is not None else ("parallel", "parallel", "arbitrary"), vmem),
    )(a, b, *ex_ops)
    if exchange is None:
        return tuple(res) if also_bf16 else res[0]
    own = tuple(res[:n_own_out]) if also_bf16 else res[0]
    return own, list(res[n_own_out:])


def _mm_tn_slabs(a, ka, b, nbw, nslab, *, name, tmr=2 * TM):
    m = a.shape[0]
    assert m % tmr == 0

    def body(a_ref, b_ref, o_ref):
        @pl.when(pl.program_id(1) == 0)
        def _():
            o_ref[...] = jnp.zeros_like(o_ref)

        o_ref[0] += lax.dot_general(a_ref[...].astype(BF16), b_ref[...].astype(BF16), (((0,), (0,)), ((), ())),
                                    preferred_element_type=F32)

    return pl.pallas_call(
        body, name=name, grid=(nslab, m // tmr),
        in_specs=[pl.BlockSpec((tmr, ka), lambda s, r: (r, s)), pl.BlockSpec((tmr, nbw), lambda s, r: (r, s))],
        out_specs=pl.BlockSpec((1, ka, nbw), lambda s, r: (s, 0, 0)),
        out_shape=jax.ShapeDtypeStruct((nslab, ka, nbw), F32),
        compiler_params=_cparams(("parallel", "arbitrary")),
    )(a, b)


def _ln_stats(r):
    mu = jnp.mean(r, axis=-1, keepdims=True)
    xc = r - mu
    var = jnp.mean(xc * xc, axis=-1, keepdims=True)
    rstd = lax.rsqrt(var + LN_EPS)
    return xc * rstd, rstd


def _ln_apply(r, g, b):
    xhat, _ = _ln_stats(r)
    return xhat * g + b


def _ln_grad(r, dy, g):
    xhat, rstd = _ln_stats(r)
    dxh = dy * g
    m1 = jnp.mean(dxh, axis=-1, keepdims=True)
    m2 = jnp.mean(dxh * xhat, axis=-1, keepdims=True)
    return (rstd * (dxh - m1 - xhat * m2), jnp.sum(dy * xhat, axis=0, keepdims=True), jnp.sum(dy, axis=0, keepdims=True))


def _mm_rows(matmuls, rows_in, vecs_in, out_rows, acc_widths, fn, *, name, tm=TM, vmem=None):
    m = rows_in[0].shape[0]
    assert m % tm == 0
    flat, in_specs, layout = [], [], []
    for a_parts, b, trans_b in matmuls:
        for arr, cb, w in a_parts:
            flat.append(arr)
            in_specs.append(pl.BlockSpec((tm, w), functools.partial(lambda i, cb: (i, cb), cb=cb)))
        flat.append(b)
        in_specs.append(pl.BlockSpec(b.shape, lambda i: (0, 0)))
        layout.append(([w for _, _, w in a_parts], trans_b))
    for r in rows_in:
        flat.append(r)
        in_specs.append(pl.BlockSpec((tm, r.shape[1]), lambda i: (i, 0)))
    for v in vecs_in:
        flat.append(v)
        in_specs.append(pl.BlockSpec(v.shape, lambda i: (0, 0)))
    n_in = len(flat)
    n_rows_out = len(out_rows)

    def body(*refs):
        pos = 0
        products = []
        for widths, trans_b in layout:
            b_ref = refs[pos + len(widths)]
            acc, k0 = None, 0
            for ar, w in zip(refs[pos:pos + len(widths)], widths):
                a = ar[...].astype(BF16)
                if trans_b:
                    part = lax.dot_general(a, b_ref[:, k0:k0 + w], (((1,), (1,)), ((), ())), preferred_element_type=F32)
                else:
                    part = jnp.dot(a, b_ref[k0:k0 + w, :], preferred_element_type=F32)
                acc = part if acc is None else acc + part
                k0 += w
            products.append(acc)
            pos += len(widths) + 1
        rows = [r[...] for r in refs[pos:pos + len(rows_in)]]
        pos += len(rows_in)
        vecs = [v[...] for v in refs[pos:n_in]]
        outs, sums = fn(products, rows, vecs)
        for o_ref, o in zip(refs[n_in:n_in + n_rows_out], outs):
            o_ref[...] = o.astype(o_ref.dtype)
        if acc_widths:
            acc_refs = refs[n_in + n_rows_out:]

            @pl.when(pl.program_id(0) == 0)
            def _():
                for a_ref in acc_refs:
                    a_ref[...] = jnp.zeros_like(a_ref)

            for a_ref, s_ in zip(acc_refs, sums):
                a_ref[...] += s_

    out_specs = [pl.BlockSpec((tm, n), lambda i: (i, 0)) for n, _ in out_rows]
    out_specs += [pl.BlockSpec((1, wd), lambda i: (0, 0)) for wd in acc_widths]
    out_shape = [jax.ShapeDtypeStruct((m, n), dt) for n, dt in out_rows]
    out_shape += [jax.ShapeDtypeStruct((1, wd), F32) for wd in acc_widths]
    return pl.pallas_call(
        body, name=name, grid=(m // tm,), in_specs=in_specs, out_specs=out_specs, out_shape=out_shape,
        compiler_params=_cparams(("arbitrary",) if acc_widths else ("parallel",), vmem),
    )(*flat)


def _ffn_up(x1, wup, *, name, exchange=None):
    m = x1.shape[0]
    tn = D_FF // 2
    grid = (2, m // TM)
    ex_ops, ex_in_specs, ex_out_specs, ex_out_shapes, ex_scratch = _host_parts(exchange)

    def body(*refs):
        x_ref, wg_ref, wu_ref = refs[:3]
        hosted_in = refs[3:3 + len(ex_ops)]
        a_ref, g_ref, u_ref = refs[3 + len(ex_ops):6 + len(ex_ops)]
        j, i = pl.program_id(0), pl.program_id(1)
        _host_run(exchange, list(hosted_in) + list(refs[6 + len(ex_ops):]),
                  (j == 0) & (i == 0), (j == grid[0] - 1) & (i == grid[1] - 1))
        x = x_ref[...].astype(BF16)
        g = jnp.dot(x, wg_ref[...], preferred_element_type=F32)
        u = jnp.dot(x, wu_ref[...], preferred_element_type=F32)
        a_ref[...] = (g * _sigmoid(g) * u).astype(BF16)
        g_ref[...] = g.astype(BF16)
        u_ref[...] = u.astype(BF16)

    out = pl.BlockSpec((TM, tn), lambda j, i: (i, j))
    res = pl.pallas_call(
        body, name=name, grid=grid,
        in_specs=[pl.BlockSpec((TM, D_MODEL), lambda j, i: (i, 0)),
                  pl.BlockSpec((D_MODEL, tn), lambda j, i: (0, j)),
                  pl.BlockSpec((D_MODEL, tn), lambda j, i: (0, j + 2))] + ex_in_specs,
        out_specs=[out, out, out] + ex_out_specs,
        out_shape=[jax.ShapeDtypeStruct((m, D_FF), BF16)] * 3 + ex_out_shapes,
        scratch_shapes=ex_scratch,
        compiler_params=_cparams(("arbitrary", "arbitrary") if exchange is not None else ("parallel", "parallel")),
    )(x1, wup, wup, *ex_ops)
    return (res[0], res[1], res[2]) if exchange is None else ((res[0], res[1], res[2]), list(res[3:]))


def _ffn_down_bwd(df, wdown, g, u, *, name):
    m = df.shape[0]
    tm = 256

    def body(df_ref, w_ref, g_ref, u_ref, o_ref):
        da = lax.dot_general(df_ref[...].astype(BF16), w_ref[...], (((1,), (1,)), ((), ())), preferred_element_type=F32)
        gg = g_ref[...].astype(F32)
        sg = _sigmoid(gg)
        o_ref[:, :D_FF] = (da * u_ref[...].astype(F32) * (sg * (1.0 + gg * (1.0 - sg)))).astype(BF16)
        o_ref[:, D_FF:] = (da * (gg * sg)).astype(BF16)

    return pl.pallas_call(
        body, name=name, grid=(m // tm,),
        in_specs=[pl.BlockSpec((tm, D_MODEL), lambda i: (i, 0)), pl.BlockSpec((D_FF, D_MODEL), lambda i: (0, 0)),
                  pl.BlockSpec((tm, D_FF), lambda i: (i, 0)), pl.BlockSpec((tm, D_FF), lambda i: (i, 0))],
        out_specs=pl.BlockSpec((tm, 2 * D_FF), lambda i: (i, 0)),
        out_shape=jax.ShapeDtypeStruct((m, 2 * D_FF), BF16),
        compiler_params=_cparams(("parallel",), VMEM_BIG),
    )(df, wdown, g, u)


def _ple_ln_bwd(da, db, gate, pp, r2, wgate, g2, *, name):
    m, n = da.shape
    two = db is not None
    n_in = 7 if two else 6

    def body(*refs):
        if two:
            da_ref, db_ref, gate_ref, pp_ref, r_ref, w_ref, g_ref = refs[:n_in]
            dx3 = ALPHA * da_ref[...] + db_ref[...]
        else:
            da_ref, gate_ref, pp_ref, r_ref, w_ref, g_ref = refs[:n_in]
            dx3 = da_ref[...]
        dz_ref, dpp_ref, dr_ref, dbg_ref, dg_ref, dbias_ref = refs[n_in:]

        @pl.when(pl.program_id(0) == 0)
        def _():
            dbg_ref[...] = jnp.zeros_like(dbg_ref)
            dg_ref[...] = jnp.zeros_like(dg_ref)
            dbias_ref[...] = jnp.zeros_like(dbias_ref)

        gate = gate_ref[...].astype(F32)
        dz = dx3 * pp_ref[...].astype(F32) * gate * (1.0 - gate)
        dzq = dz.astype(BF16)
        dz_ref[...] = dzq
        dpp_ref[...] = (dx3 * gate).astype(BF16)
        dbg_ref[...] += jnp.sum(dz, axis=0, keepdims=True)
        dx2 = dx3 + lax.dot_general(dzq, w_ref[...], (((1,), (1,)), ((), ())), preferred_element_type=F32)
        dr, dg, dbias = _ln_grad(r_ref[...], dx2, g_ref[...])
        dr_ref[...] = dr
        dg_ref[...] += dg
        dbias_ref[...] += dbias

    row = pl.BlockSpec((TM, n), lambda i: (i, 0))
    vec = pl.BlockSpec((1, n), lambda i: (0, 0))
    ins = ([da, db] if two else [da]) + [gate, pp, r2, wgate, g2]
    in_specs = [row] * (n_in - 2) + [pl.BlockSpec(wgate.shape, lambda i: (0, 0)), vec]
    return pl.pallas_call(
        body, name=name, grid=(m // TM,), in_specs=in_specs, out_specs=[row, row, row, vec, vec, vec],
        out_shape=[jax.ShapeDtypeStruct((m, n), BF16), jax.ShapeDtypeStruct((m, n), BF16), jax.ShapeDtypeStruct((m, n), F32)]
        + [jax.ShapeDtypeStruct((1, n), F32)] * 3,
        compiler_params=_cparams(("arbitrary",)),
    )(*ins)


def _loss_head(y, target, *, name):
    m, n = y.shape

    def body(y_ref, t_ref, loss_ref, dy_ref):
        @pl.when(pl.program_id(0) == 0)
        def _():
            loss_ref[...] = jnp.zeros_like(loss_ref)

        err = y_ref[...] - t_ref[...]
        dy_ref[...] = err * (1.0 / n)
        per_tok = jnp.mean(err * err, axis=-1, keepdims=True)
        loss_ref[...] += 0.5 * jnp.sum(per_tok, axis=0, keepdims=True)

    row = pl.BlockSpec((TM, n), lambda i: (i, 0))
    return pl.pallas_call(
        body, name=name, grid=(m // TM,), in_specs=[row, row],
        out_specs=[pl.BlockSpec((1, 1), lambda i: (0, 0)), row],
        out_shape=[jax.ShapeDtypeStruct((1, 1), F32), jax.ShapeDtypeStruct((m, n), F32)],
        compiler_params=_cparams(("arbitrary",)),
    )(y, target)


def _gelu(y):
    c = math.sqrt(2.0 / math.pi)
    return 0.5 * y * (1.0 + jnp.tanh(c * (y + 0.044715 * y * y * y)))


def _gelu_grad(y):
    c = math.sqrt(2.0 / math.pi)
    t = jnp.tanh(c * (y + 0.044715 * y * y * y))
    return 0.5 * (1.0 + t) + 0.5 * y * (1.0 - t * t) * c * (1.0 + 3.0 * 0.044715 * y * y)


STRIP = 256
S5_SLABS = 4
SLAB_COLS = 2 * S5_N // S5_SLABS


def _strip_cols(j):
    off = pl.multiple_of(j * STRIP, STRIP)
    col = pl.multiple_of(j * STRIP + (j // 2) * (SLAB_COLS // 2), STRIP)
    return off, col, pl.multiple_of(col + SLAB_COLS // 2, STRIP)


def _scan_strip(xr, xi, cr, ci, ptab_ref, off, rowmod, down):
    t = xr.shape[0]
    base = 0 if down else 16
    p_r = ptab_ref[base:base + 8, pl.ds(off, STRIP)]
    p_i = ptab_ref[base + 8:base + 16, pl.ds(off, STRIP)]
    if not down:
        p_i = -p_i
    for k in range(3):
        s = 1 << k
        idx = s - 1 if down else 8 - s
        pr, pi_ = p_r[idx:idx + 1], p_i[idx:idx + 1]
        if down:
            sr = jnp.where(rowmod >= s, pltpu.roll(xr, s, 0), 0.0)
            si = jnp.where(rowmod >= s, pltpu.roll(xi, s, 0), 0.0)
        else:
            sr = jnp.where(rowmod < 8 - s, pltpu.roll(xr, t - s, 0), 0.0)
            si = jnp.where(rowmod < 8 - s, pltpu.roll(xi, t - s, 0), 0.0)
        xr, xi = xr + pr * sr - pi_ * si, xi + pr * si + pi_ * sr
    ng = t // 8
    out_r, out_i = [None] * ng, [None] * ng
    for g in (range(ng) if down else reversed(range(ng))):
        cbr = jnp.broadcast_to(cr, (8, STRIP))
        cbi = jnp.broadcast_to(ci, (8, STRIP))
        br = xr[8 * g:8 * g + 8] + p_r * cbr - p_i * cbi
        bi = xi[8 * g:8 * g + 8] + p_r * cbi + p_i * cbr
        cr, ci = (br[7:8], bi[7:8]) if down else (br[0:1], bi[0:1])
        out_r[g], out_i[g] = br, bi
    return jnp.concatenate(out_r, axis=0), jnp.concatenate(out_i, axis=0)


def _s5_fwd(h, bmat, cmat, dvec, wglu, bglu, ptab, *, name):
    m = h.shape[0]
    t = T_S5
    nb = m // t

    def body(u_ref, bmat_ref, cmat_ref, d_ref, wglu_ref, bglu_ref, ptab_ref,
             out_ref, y_ref, hb_ref, bu_ref, carry_ref):
        @pl.when(pl.program_id(0) == 0)
        def _():
            carry_ref[...] = jnp.zeros_like(carry_ref)

        hb_ref[0] = carry_ref[...]
        u = u_ref[...]
        ub = u.astype(BF16)
        for s in range(S5_SLABS):
            bu_ref[:, SLAB_COLS * s:SLAB_COLS * (s + 1)] = jnp.dot(ub[:, 128 * s:128 * (s + 1)], bmat_ref[s],
                                                                  preferred_element_type=F32)
        rowmod = lax.broadcasted_iota(jnp.int32, (t, STRIP), 0) & 7

        def strip(j, c):
            off, col, coli = _strip_cols(j)
            xr, xi = _scan_strip(bu_ref[:, pl.ds(col, STRIP)], bu_ref[:, pl.ds(coli, STRIP)],
                                 carry_ref[0:1, pl.ds(col, STRIP)], carry_ref[0:1, pl.ds(coli, STRIP)],
                                 ptab_ref, off, rowmod, True)
            bu_ref[:, pl.ds(col, STRIP)] = xr
            bu_ref[:, pl.ds(coli, STRIP)] = xi
            carry_ref[0:1, pl.ds(col, STRIP)] = xr[t - 1:t, :]
            carry_ref[0:1, pl.ds(coli, STRIP)] = xi[t - 1:t, :]
            return c

        lax.fori_loop(0, S5_N // STRIP, strip, 0)
        y = jnp.concatenate(
            [jnp.dot(bu_ref[:, SLAB_COLS * s:SLAB_COLS * (s + 1)].astype(BF16), cmat_ref[s], preferred_element_type=F32)
             for s in range(S5_SLABS)], axis=1) + d_ref[...] * u
        y_ref[...] = y
        g = _gelu(y)
        zz = jnp.dot(g.astype(BF16), wglu_ref[...], preferred_element_type=F32) + bglu_ref[...]
        out_ref[...] = (g * _sigmoid(zz)).astype(BF16)

    const = lambda shape: pl.BlockSpec(shape, lambda i: (0,) * len(shape))
    row_spec = pl.BlockSpec((t, MIX), lambda i: (i, 0))
    return pl.pallas_call(
        body, name=name, grid=(nb,),
        in_specs=[row_spec, const((S5_SLABS, 128, SLAB_COLS)), const((S5_SLABS, SLAB_COLS, 128)), const((1, MIX)),
                  const((MIX, MIX)), const((1, MIX)), const((32, S5_N))],
        out_specs=[row_spec, row_spec, pl.BlockSpec((1, 1, 2 * S5_N), lambda i: (i, 0, 0))],
        out_shape=[jax.ShapeDtypeStruct((m, MIX), BF16), jax.ShapeDtypeStruct((m, MIX), F32),
                   jax.ShapeDtypeStruct((nb, 1, 2 * S5_N), F32)],
        scratch_shapes=[pltpu.VMEM((t, 2 * S5_N), F32), pltpu.VMEM((1, 2 * S5_N), F32)],
        compiler_params=_cparams(("arbitrary",), VMEM_BIG),
    )(h, bmat, cmat, dvec, wglu, bglu, ptab)


def _s5_bwd(dcat, ypre, h, hb, bmat, cmat, dvec, wglu, bglu, ptab, *, name):
    m = h.shape[0]
    t = T_S5
    nb = m // t

    def body(dya_ref, y_ref, u_ref, hb_ref, bmat_ref, cmat_ref, d_ref, wglu_ref, bglu_ref, ptab_ref,
             du_ref, xb_ref, gb_ref, gq_ref, dzz_ref, dyq_ref, dlam_ref, dbglu_ref, dd_ref,
             bu_ref, dx_ref, gcarry_ref):
        @pl.when(pl.program_id(0) == 0)
        def _():
            gcarry_ref[...] = jnp.zeros_like(gcarry_ref)
            dlam_ref[...] = jnp.zeros_like(dlam_ref)
            dbglu_ref[...] = jnp.zeros_like(dbglu_ref)
            dd_ref[...] = jnp.zeros_like(dd_ref)

        u = u_ref[...]
        y = y_ref[...]
        g = _gelu(y)
        gq = g.astype(BF16)
        sg = _sigmoid(jnp.dot(gq, wglu_ref[...], preferred_element_type=F32) + bglu_ref[...])
        dout = dya_ref[...]
        dzz = dout * g * sg * (1.0 - sg)
        dzzq = dzz.astype(BF16)
        dg = dout * sg + lax.dot_general(dzzq, wglu_ref[...], (((1,), (1,)), ((), ())), preferred_element_type=F32)
        dy = dg * _gelu_grad(y)
        dyq = dy.astype(BF16)
        gq_ref[...] = gq
        dzz_ref[...] = dzzq
        dyq_ref[...] = dyq
        dbglu_ref[...] += jnp.sum(dzz, axis=0, keepdims=True)
        dd_ref[...] += jnp.sum(dy * u, axis=0, keepdims=True)

        ub = u.astype(BF16)
        nt = (((1,), (1,)), ((), ()))
        for s in range(S5_SLABS):
            cols = slice(SLAB_COLS * s, SLAB_COLS * (s + 1))
            dx_ref[:, cols] = lax.dot_general(dyq[:, 128 * s:128 * (s + 1)], cmat_ref[s], nt, preferred_element_type=F32)
            bu_ref[:, cols] = jnp.dot(ub[:, 128 * s:128 * (s + 1)], bmat_ref[s], preferred_element_type=F32)
        row = lax.broadcasted_iota(jnp.int32, (t, STRIP), 0)
        rowmod = row & 7

        def strip(j, c):
            off, col, coli = _strip_cols(j)
            hr = hb_ref[0, 0:1, pl.ds(col, STRIP)]
            hi = hb_ref[0, 0:1, pl.ds(coli, STRIP)]
            xr, xi = _scan_strip(bu_ref[:, pl.ds(col, STRIP)], bu_ref[:, pl.ds(coli, STRIP)], hr, hi,
                                 ptab_ref, off, rowmod, True)
            xb_ref[:, pl.ds(col, STRIP)] = xr.astype(BF16)
            xb_ref[:, pl.ds(coli, STRIP)] = xi.astype(BF16)
            pr_ = jnp.where(row == 0, hr, pltpu.roll(xr, 1, 0))
            pi_ = jnp.where(row == 0, hi, pltpu.roll(xi, 1, 0))

            gr, gi = _scan_strip(dx_ref[:, pl.ds(col, STRIP)], dx_ref[:, pl.ds(coli, STRIP)],
                                 gcarry_ref[0:1, pl.ds(col, STRIP)], gcarry_ref[0:1, pl.ds(coli, STRIP)],
                                 ptab_ref, off, rowmod, False)
            gb_ref[:, pl.ds(col, STRIP)] = gr.astype(BF16)
            gb_ref[:, pl.ds(coli, STRIP)] = gi.astype(BF16)
            gcarry_ref[0:1, pl.ds(col, STRIP)] = gr[0:1, :]
            gcarry_ref[0:1, pl.ds(coli, STRIP)] = gi[0:1, :]
            dlam_ref[0:1, pl.ds(off, STRIP)] += jnp.sum(pr_ * gr + pi_ * gi, axis=0, keepdims=True)
            dlam_ref[1:2, pl.ds(off, STRIP)] += jnp.sum(pr_ * gi - pi_ * gr, axis=0, keepdims=True)
            return c

        lax.fori_loop(0, S5_N // STRIP, strip, 0)
        du_ref[...] = dy * d_ref[...] + jnp.concatenate(
            [lax.dot_general(gb_ref[:, SLAB_COLS * s:SLAB_COLS * (s + 1)], bmat_ref[s], nt, preferred_element_type=F32)
             for s in range(S5_SLABS)], axis=1)

    const = lambda shape: pl.BlockSpec(shape, lambda i: (0,) * len(shape))
    rev = lambda i: (nb - 1 - i, 0)
    row_spec = pl.BlockSpec((t, MIX), rev)
    wide = pl.BlockSpec((t, 2 * S5_N), rev)
    return pl.pallas_call(
        body, name=name, grid=(nb,),
        in_specs=[row_spec, row_spec, row_spec, pl.BlockSpec((1, 1, 2 * S5_N), lambda i: (nb - 1 - i, 0, 0)),
                  const((S5_SLABS, 128, SLAB_COLS)), const((S5_SLABS, SLAB_COLS, 128)), const((1, MIX)), const((MIX, MIX)),
                  const((1, MIX)), const((32, S5_N))],
        out_specs=[row_spec, wide, wide, row_spec, row_spec, row_spec, const((2, S5_N)), const((1, MIX)), const((1, MIX))],
        out_shape=[jax.ShapeDtypeStruct((m, MIX), F32), jax.ShapeDtypeStruct((m, 2 * S5_N), BF16),
                   jax.ShapeDtypeStruct((m, 2 * S5_N), BF16), jax.ShapeDtypeStruct((m, MIX), BF16),
                   jax.ShapeDtypeStruct((m, MIX), BF16), jax.ShapeDtypeStruct((m, MIX), BF16),
                   jax.ShapeDtypeStruct((2, S5_N), F32), jax.ShapeDtypeStruct((1, MIX), F32), jax.ShapeDtypeStruct((1, MIX), F32)],
        scratch_shapes=[pltpu.VMEM((t, 2 * S5_N), F32), pltpu.VMEM((t, 2 * S5_N), F32), pltpu.VMEM((1, 2 * S5_N), F32)],
        compiler_params=_cparams(("arbitrary",), VMEM_BIG),
    )(dcat, ypre, h, hb, bmat, cmat, dvec, wglu, bglu, ptab)


HALO = 8


def _taps_down(zext, t):
    return pltpu.roll(zext, 1, 0)[HALO:HALO + t], pltpu.roll(zext, 2, 0)[HALO:HALO + t]


def _conv_z(c_ref, x_ref, cp_ref, xp_ref, first, t):
    z = c_ref[...] * x_ref[...]
    zp = jnp.where(first, 0.0, cp_ref[t - HALO:t, :] * xp_ref[t - HALO:t, :])
    z1, z2 = _taps_down(jnp.concatenate([zp, z], axis=0), t)
    return z, z1, z2


def _conv_fwd(h, cw, *, name):
    m = h.shape[0]
    t = TM
    nb = m // t

    def body(b_ref, c_ref, x_ref, cp_ref, xp_ref, w_ref, o_ref):
        z, z1, z2 = _conv_z(c_ref, x_ref, cp_ref, xp_ref, pl.program_id(0) == 0, t)
        o_ref[...] = (b_ref[...] * (w_ref[0:1, :] * z2 + w_ref[1:2, :] * z1 + w_ref[2:3, :] * z)).astype(BF16)

    cur = lambda cb: pl.BlockSpec((t, MIX), lambda i: (i, cb))
    prev = lambda cb: pl.BlockSpec((t, MIX), lambda i: (jnp.maximum(i - 1, 0), cb))
    return pl.pallas_call(
        body, name=name, grid=(nb,),
        in_specs=[cur(1), cur(2), cur(3), prev(2), prev(3), pl.BlockSpec((3, MIX), lambda i: (0, 0))],
        out_specs=pl.BlockSpec((t, MIX), lambda i: (i, 0)),
        out_shape=jax.ShapeDtypeStruct((m, MIX), BF16),
        compiler_params=_cparams(("parallel",)),
    )(h, h, h, h, h, cw)


def _conv_bwd(dcat, h, cw, *, name):
    m = h.shape[0]
    t = TM
    nb = m // t

    def body(dy_ref, dyn_ref, b_ref, c_ref, x_ref, cp_ref, xp_ref, bn_ref, w_ref, o_ref, dw_ref):
        i = pl.program_id(0)

        @pl.when(i == 0)
        def _():
            dw_ref[...] = jnp.zeros_like(dw_ref)

        z, z1, z2 = _conv_z(c_ref, x_ref, cp_ref, xp_ref, i == 0, t)
        w0, w1, w2 = w_ref[0:1, :], w_ref[1:2, :], w_ref[2:3, :]
        dy = dy_ref[...]
        dconv = dy * b_ref[...]
        dnext = jnp.where(i == nb - 1, 0.0, dyn_ref[0:HALO, :] * bn_ref[0:HALO, :])
        dext = jnp.concatenate([dconv, dnext], axis=0)
        d1 = pltpu.roll(dext, t + HALO - 1, 0)[0:t]
        d2 = pltpu.roll(dext, t + HALO - 2, 0)[0:t]
        dz = w2 * dconv + w1 * d1 + w0 * d2
        o_ref[:, 0:MIX] = dy * (w0 * z2 + w1 * z1 + w2 * z)
        o_ref[:, MIX:2 * MIX] = dz * x_ref[...]
        o_ref[:, 2 * MIX:3 * MIX] = dz * c_ref[...]
        dw_ref[0:1, :] += jnp.sum(dconv * z2, axis=0, keepdims=True)
        dw_ref[1:2, :] += jnp.sum(dconv * z1, axis=0, keepdims=True)
        dw_ref[2:3, :] += jnp.sum(dconv * z, axis=0, keepdims=True)

    cur = lambda cb: pl.BlockSpec((t, MIX), lambda i: (i, cb))
    prev = lambda cb: pl.BlockSpec((t, MIX), lambda i: (jnp.maximum(i - 1, 0), cb))
    nxt = lambda cb: pl.BlockSpec((t, MIX), lambda i: (jnp.minimum(i + 1, nb - 1), cb))
    return pl.pallas_call(
        body, name=name, grid=(nb,),
        in_specs=[cur(1), nxt(1), cur(1), cur(2), cur(3), prev(2), prev(3), nxt(1), pl.BlockSpec((3, MIX), lambda i: (0, 0))],
        out_specs=[pl.BlockSpec((t, 3 * MIX), lambda i: (i, 0)), pl.BlockSpec((8, MIX), lambda i: (0, 0))],
        out_shape=[jax.ShapeDtypeStruct((m, 3 * MIX), F32), jax.ShapeDtypeStruct((8, MIX), F32)],
        compiler_params=_cparams(("arbitrary",)),
    )(dcat, dcat, h, h, h, h, h, h, cw)


PHALO = 16


def _pool_pooled(z_ref, zp_ref, i, t):
    z = z_ref[...]
    zp = jnp.where(i == 0, 0.0, zp_ref[t - PHALO:t, :])
    s = jnp.concatenate([zp, z], axis=0)
    sums = {}
    width = 1
    while width < PHALO:
        s = s + pltpu.roll(s, width, 0)
        width *= 2
        sums[width] = s[PHALO:PHALO + t]
    tpos = i * t + lax.broadcasted_iota(jnp.int32, (t, 1), 0)
    outs = []
    for gi, w in enumerate(POOL_WINDOWS):
        lo = gi * POOL_GROUP
        count = jnp.minimum(tpos + 1, w).astype(F32)
        outs.append(sums[w][:, lo:lo + POOL_GROUP] / count - z[:, lo:lo + POOL_GROUP])
    return outs


def _pool_fwd(h, pw, ps, *, name):
    m = h.shape[0]
    t = TM
    nb = m // t

    def body(z_ref, zp_ref, pw_ref, ps_ref, o_ref):
        pooled = _pool_pooled(z_ref, zp_ref, pl.program_id(0), t)
        for gi in range(len(POOL_WINDOWS)):
            lo = gi * POOL_GROUP
            mixed = jnp.dot(pooled[gi].astype(BF16), pw_ref[gi], preferred_element_type=F32)
            o_ref[:, lo:lo + POOL_GROUP] = (mixed * ps_ref[:, lo:lo + POOL_GROUP]).astype(BF16)

    return pl.pallas_call(
        body, name=name, grid=(nb,),
        in_specs=[pl.BlockSpec((t, MIX), lambda i: (i, 3)), pl.BlockSpec((t, MIX), lambda i: (jnp.maximum(i - 1, 0), 3)),
                  pl.BlockSpec((4, POOL_GROUP, POOL_GROUP), lambda i: (0, 0, 0)), pl.BlockSpec((1, MIX), lambda i: (0, 0))],
        out_specs=pl.BlockSpec((t, MIX), lambda i: (i, 0)),
        out_shape=jax.ShapeDtypeStruct((m, MIX), BF16),
        compiler_params=_cparams(("parallel",)),
    )(h, h, pw, ps)


def _pool_bwd(dcat, h, pw, ps, *, name):
    m = h.shape[0]
    t = TM
    nb = m // t

    def body(dy_ref, dyn_ref, z_ref, zp_ref, pw_ref, ps_ref, dz_ref, dpw_ref, dps_ref):
        i = pl.program_id(0)

        @pl.when(i == 0)
        def _():
            dpw_ref[...] = jnp.zeros_like(dpw_ref)
            dps_ref[...] = jnp.zeros_like(dps_ref)

        pooled = _pool_pooled(z_ref, zp_ref, i, t)
        dy = dy_ref[...]
        tpos = i * t + lax.broadcasted_iota(jnp.int32, (t, 1), 0)
        for gi, w in enumerate(POOL_WINDOWS):
            lo = gi * POOL_GROUP
            sl = slice(lo, lo + POOL_GROUP)
            pq = pooled[gi].astype(BF16)
            mixed = jnp.dot(pq, pw_ref[gi], preferred_element_type=F32)
            dps_ref[:, sl] += jnp.sum(dy[:, sl] * mixed, axis=0, keepdims=True)
            dmix = (dy[:, sl] * ps_ref[:, sl]).astype(BF16)
            dpw_ref[gi] += lax.dot_general(pq, dmix, (((0,), (0,)), ((), ())), preferred_element_type=F32)
            dpool = lax.dot_general(dmix, pw_ref[gi], (((1,), (1,)), ((), ())), preferred_element_type=F32)
            dmix_n = (dyn_ref[0:PHALO, sl] * ps_ref[:, sl]).astype(BF16)
            dpool_n = lax.dot_general(dmix_n, pw_ref[gi], (((1,), (1,)), ((), ())), preferred_element_type=F32)
            e = dpool / jnp.minimum(tpos + 1, w).astype(F32)
            e_n = jnp.where(i == nb - 1, 0.0, dpool_n * (1.0 / w))
            f = jnp.concatenate([e, e_n], axis=0)
            width = 1
            while width < w:
                f = f + pltpu.roll(f, t + PHALO - width, 0)
                width *= 2
            dz_ref[:, sl] = f[0:t] - dpool

    return pl.pallas_call(
        body, name=name, grid=(nb,),
        in_specs=[pl.BlockSpec((t, MIX), lambda i: (i, 1)), pl.BlockSpec((t, MIX), lambda i: (jnp.minimum(i + 1, nb - 1), 1)),
                  pl.BlockSpec((t, MIX), lambda i: (i, 3)), pl.BlockSpec((t, MIX), lambda i: (jnp.maximum(i - 1, 0), 3)),
                  pl.BlockSpec((4, POOL_GROUP, POOL_GROUP), lambda i: (0, 0, 0)), pl.BlockSpec((1, MIX), lambda i: (0, 0))],
        out_specs=[pl.BlockSpec((t, MIX), lambda i: (i, 0)), pl.BlockSpec((4, POOL_GROUP, POOL_GROUP), lambda i: (0, 0, 0)),
                   pl.BlockSpec((1, MIX), lambda i: (0, 0))],
        out_shape=[jax.ShapeDtypeStruct((m, MIX), F32), jax.ShapeDtypeStruct((4, POOL_GROUP, POOL_GROUP), F32),
                   jax.ShapeDtypeStruct((1, MIX), F32)],
        compiler_params=_cparams(("arbitrary",)),
    )(dcat, dcat, h, h, pw, ps)


NKEY = 2 * T_ATT


def _band_mask():
    qc = np.arange(T_ATT)[:, None] // CHUNK
    kc = np.arange(NKEY)[None, :] // CHUNK - LEFT_CHUNKS
    return np.where((kc <= qc) & (kc >= qc - LEFT_CHUNKS), 0.0, NEG_INF).astype(np.float32)


def _diag_index():
    c = np.arange(NKEY)
    d = np.where(c <= NKEY // 2 + CHUNK, T_ATT - c, T_ATT + NKEY - c)
    return np.clip(d, -MAX_REL, MAX_REL) + MAX_REL


def _bias_tile(vd_ref, mask_ref, tile_ref):
    col = lax.broadcasted_iota(jnp.int32, (8, NKEY), 1)
    no_prev = jnp.where(col < T_ATT, NEG_INF, 0.0)
    for hh in range(2):
        v = vd_ref[0, hh:hh + 1, :]
        base = jnp.concatenate([v if s == 0 else pltpu.roll(v, s, 1) for s in range(8)], axis=0)
        for mrow in range(T_ATT // 8):
            rows = slice(8 * mrow, 8 * mrow + 8)
            blk = (base if mrow == 0 else pltpu.roll(base, 8 * mrow, 1)) + mask_ref[rows, :]
            tile_ref[hh, rows, :] = blk
            tile_ref[2 + hh, rows, :] = blk + no_prev


BAND_ROWS = 2 * CHUNK
BAND_COLS = (LEFT_CHUNKS + 2) * CHUNK
N_BANDS = T_ATT // BAND_ROWS


def _band(x, r):
    return x[BAND_ROWS * r:BAND_ROWS * (r + 1), BAND_ROWS * r:BAND_ROWS * r + BAND_COLS]


def _from_bands(parts):
    rows = []
    for r, part in enumerate(parts):
        right = NKEY - BAND_COLS - BAND_ROWS * r
        pieces = ([jnp.zeros((BAND_ROWS, BAND_ROWS * r), part.dtype)] if r else []) + [part]
        pieces += [jnp.zeros((BAND_ROWS, right), part.dtype)] if right else []
        rows.append(jnp.concatenate(pieces, axis=1))
    return jnp.concatenate(rows, axis=0)


def _attn_probs(q, kc, tile_ref, idx):
    s = lax.dot_general(q, kc, (((1,), (1,)), ((), ())), preferred_element_type=F32)
    parts = []
    for r in range(N_BANDS):
        sb = _band(s, r) + tile_ref[idx, BAND_ROWS * r:BAND_ROWS * (r + 1), BAND_ROWS * r:BAND_ROWS * r + BAND_COLS]
        p = jnp.exp(sb - jnp.max(sb, axis=-1, keepdims=True))
        parts.append(p * (1.0 / jnp.sum(p, axis=-1, keepdims=True)))
    return parts


def _attn_specs(block):
    cur = lambda base: pl.BlockSpec((T_ATT, 128), lambda hp, i: (block(i), base + hp))
    prev = lambda base: pl.BlockSpec((T_ATT, 128), lambda hp, i: (jnp.maximum(block(i) - 1, 0), base + hp))
    return [cur(0), cur(4), prev(4), cur(8), prev(8),
            pl.BlockSpec((1, 2, NKEY), lambda hp, i: (hp, 0, 0)), pl.BlockSpec((T_ATT, NKEY), lambda hp, i: (0, 0))]


def _attn_fwd(h, vdiag, mask, *, name):
    m = h.shape[0]
    nb = m // T_ATT

    def body(q_ref, k_ref, kp_ref, v_ref, vp_ref, vd_ref, mask_ref, o_ref, tile_ref):
        i = pl.program_id(1)

        @pl.when(i == 0)
        def _():
            _bias_tile(vd_ref, mask_ref, tile_ref)

        first = jnp.where(i == 0, 2, 0)
        outs = []
        for hh in range(2):
            sl = slice(hh * HEAD_DIM, (hh + 1) * HEAD_DIM)
            q = (q_ref[:, sl] * (HEAD_DIM ** -0.5)).astype(BF16)
            kc = jnp.concatenate([kp_ref[:, sl], k_ref[:, sl]], axis=0).astype(BF16)
            vc = jnp.concatenate([vp_ref[:, sl], v_ref[:, sl]], axis=0).astype(BF16)
            p = _from_bands([b.astype(BF16) for b in _attn_probs(q, kc, tile_ref, first + hh)])
            outs.append(jnp.dot(p, vc, preferred_element_type=F32))
        o_ref[...] = jnp.concatenate(outs, axis=1).astype(BF16)

    return pl.pallas_call(
        body, name=name, grid=(ATT_HEADS // 2, nb), in_specs=_attn_specs(lambda i: i),
        out_specs=pl.BlockSpec((T_ATT, 128), lambda hp, i: (i, hp)),
        out_shape=jax.ShapeDtypeStruct((m, MIX), BF16),
        scratch_shapes=[pltpu.VMEM((4, T_ATT, NKEY), F32)],
        compiler_params=_cparams(("parallel", "arbitrary"), VMEM_BIG),
    )(h, h, h, h, h, vdiag, mask)


def _attn_bwd(dcat, h, vdiag, mask, *, name):
    m = h.shape[0]
    nb = m // T_ATT

    def body(do_ref, q_ref, k_ref, kp_ref, v_ref, vp_ref, vd_ref, mask_ref,
             dq_ref, dk_ref, dv_ref, dvd_ref, tile_ref, acc_ref, carry_ref):
        i = pl.program_id(1)

        @pl.when(i == 0)
        def _():
            _bias_tile(vd_ref, mask_ref, tile_ref)
            acc_ref[...] = jnp.zeros_like(acc_ref)

            carry_ref[...] = jnp.zeros_like(carry_ref)

        scale = HEAD_DIM ** -0.5
        first = jnp.where(i == nb - 1, 2, 0)
        dqs, dks, dvs = [], [], []
        for hh in range(2):
            sl = slice(hh * HEAD_DIM, (hh + 1) * HEAD_DIM)
            q = (q_ref[:, sl] * scale).astype(BF16)
            kc = jnp.concatenate([kp_ref[:, sl], k_ref[:, sl]], axis=0).astype(BF16)
            vc = jnp.concatenate([vp_ref[:, sl], v_ref[:, sl]], axis=0).astype(BF16)
            do = do_ref[:, sl].astype(BF16)
            bands = _attn_probs(q, kc, tile_ref, first + hh)
            p = _from_bands([b.astype(BF16) for b in bands])
            dvs.append(lax.dot_general(p, do, (((0,), (0,)), ((), ())), preferred_element_type=F32))
            dp = lax.dot_general(do, vc, (((1,), (1,)), ((), ())), preferred_element_type=F32)
            ds_bands = []
            for r, pb in enumerate(bands):
                dpb = _band(dp, r)
                dsb = pb * (dpb - jnp.sum(dpb * pb, axis=-1, keepdims=True))
                acc_ref[hh, BAND_ROWS * r:BAND_ROWS * (r + 1), BAND_ROWS * r:BAND_ROWS * r + BAND_COLS] += dsb
                ds_bands.append(dsb.astype(BF16))
            dsq = _from_bands(ds_bands)
            dqs.append(jnp.dot(dsq, kc, preferred_element_type=F32) * scale)
            dks.append(lax.dot_general(dsq, q, (((0,), (0,)), ((), ())), preferred_element_type=F32))
        dq_ref[...] = jnp.concatenate(dqs, axis=1)
        dk = jnp.concatenate(dks, axis=1)
        dv = jnp.concatenate(dvs, axis=1)
        dk_ref[...] = dk[T_ATT:] + carry_ref[0]
        dv_ref[...] = dv[T_ATT:] + carry_ref[1]
        carry_ref[0] = dk[:T_ATT]
        carry_ref[1] = dv[:T_ATT]

        @pl.when(i == nb - 1)
        def _():
            for hh in range(2):
                r8 = acc_ref[hh, 0:8, :]
                for mrow in range(1, T_ATT // 8):
                    r8 = r8 + pltpu.roll(acc_ref[hh, 8 * mrow:8 * mrow + 8, :], NKEY - 8 * mrow, 1)
                tot = r8[0:1, :]
                for s in range(1, 8):
                    tot = tot + pltpu.roll(r8[s:s + 1, :], NKEY - s, 1)
                dvd_ref[0, hh:hh + 1, :] = tot

    block = lambda i: nb - 1 - i
    out = pl.BlockSpec((T_ATT, 128), lambda hp, i: (block(i), hp))
    return pl.pallas_call(
        body, name=name, grid=(ATT_HEADS // 2, nb),
        in_specs=[out] + _attn_specs(block),
        out_specs=[out, out, out, pl.BlockSpec((1, 2, NKEY), lambda hp, i: (hp, 0, 0))],
        out_shape=[jax.ShapeDtypeStruct((m, MIX), F32)] * 3 + [jax.ShapeDtypeStruct((ATT_HEADS // 2, 2, NKEY), F32)],
        scratch_shapes=[pltpu.VMEM((4, T_ATT, NKEY), F32), pltpu.VMEM((2, T_ATT, NKEY), F32), pltpu.VMEM((2, T_ATT, 128), F32)],
        compiler_params=_cparams(("parallel", "arbitrary"), VMEM_BIG),
    )(dcat, h, h, h, h, h, vdiag, mask)


def _row_tile(rows):
    for t in (512, 256, 128, 64, 32, 16, 8):
        if rows % t == 0:
            return t
    return rows


def _add_n(arrs, coefs, *, name):
    rows, cols = arrs[0].shape
    t = _row_tile(rows)
    n = len(arrs)

    def body(*refs):
        acc = None
        for r, cf in zip(refs[:n], coefs):
            v = r[...] if cf == 1.0 else cf * r[...]
            acc = v if acc is None else acc + v
        refs[n][...] = acc

    spec = pl.BlockSpec((t, cols), lambda i: (i, 0))
    return pl.pallas_call(
        body, name=name, grid=(rows // t,), in_specs=[spec] * n, out_specs=spec,
        out_shape=jax.ShapeDtypeStruct((rows, cols), F32), compiler_params=_cparams(("parallel",)),
    )(*arrs)


def _adamw(w, g, mom, var, *, name):
    rows, cols = w.shape
    t = _row_tile(rows)

    def body(w_ref, g_ref, m_ref, v_ref, d_ref, mo_ref, vo_ref):
        g_ = g_ref[...]
        m_ = ADAM_B1 * m_ref[...] + (1.0 - ADAM_B1) * g_
        v_ = ADAM_B2 * v_ref[...] + (1.0 - ADAM_B2) * (g_ * g_)
        m_hat = m_ / (1.0 - ADAM_B1 ** ADAM_STEP)
        v_hat = v_ / (1.0 - ADAM_B2 ** ADAM_STEP)
        d_ref[...] = -ADAM_LR * (m_hat / (jnp.sqrt(v_hat) + ADAM_EPS) + ADAM_WD * w_ref[...])
        mo_ref[...] = m_
        vo_ref[...] = v_

    spec = pl.BlockSpec((t, cols), lambda i: (i, 0))
    return pl.pallas_call(
        body, name=name, grid=(rows // t,), in_specs=[spec] * 4, out_specs=[spec] * 3,
        out_shape=[jax.ShapeDtypeStruct((rows, cols), F32)] * 3, compiler_params=_cparams(("parallel",)),
    )(w, g, mom, var)


ANY = pl.BlockSpec(memory_space=pl.ANY)


def _place():
    x, y, c = lax.axis_index("x"), lax.axis_index("y"), lax.axis_index("c")
    chips = [(1 - x, y), (x, 1 - y), (1 - x, 1 - y)]
    return x, y, c, chips


class _GatherExchange:
    def __init__(self, ws):
        n = len(ws)
        self.ins = list(ws)
        self.out_shapes = [jax.ShapeDtypeStruct((N_CHIPS,) + w.shape, w.dtype) for w in ws]
        self.sems = [pltpu.SemaphoreType.DMA((6 * n,)), pltpu.SemaphoreType.DMA((6 * n,)), pltpu.SemaphoreType.DMA((n,))]

    def _copies(self, ins, outs, sems, onward=True):
        send_sems, recv_sems, local_sems = sems
        x, y, c, chips = _place()
        me = 2 * x + y

        def region(k, j, chip_index, rows, to):
            ref = outs[k].at[chip_index, rows]
            return pltpu.make_async_remote_copy(
                src_ref=ref, dst_ref=ref, send_sem=send_sems.at[6 * k + j], recv_sem=recv_sems.at[6 * k + j],
                device_id=to, device_id_type=MESH)

        local, first, landed, passed, handed = [], [], [], [], []
        for k in range(len(ins)):
            half = ins[k].shape[0] // 2
            mine, theirs = pl.ds(c * half, half), pl.ds((1 - c) * half, half)
            local.append(pltpu.make_async_copy(ins[k], outs[k].at[me], local_sems.at[k]))
            for j, chip in enumerate(chips):
                first.append(pltpu.make_async_remote_copy(
                    src_ref=ins[k].at[mine], dst_ref=outs[k].at[me, mine], send_sem=send_sems.at[6 * k + j],
                    recv_sem=recv_sems.at[6 * k + j], device_id=(*chip, c), device_id_type=MESH))
                if onward:
                    landed.append(region(k, j, 2 * chip[0] + chip[1], mine, (*chip, c)))
                    passed.append(region(k, 3 + j, 2 * chip[0] + chip[1], mine, (x, y, 1 - c)))
                    handed.append(region(k, 3 + j, 2 * chip[0] + chip[1], theirs, (x, y, 1 - c)))
        return local, first, landed, passed, handed

    def start(self, ins, outs, sems):
        local, first, _, _, _ = self._copies(ins, outs, sems, onward=False)
        for cp in local + first:
            cp.start()

    def finish(self, ins, outs, sems):
        local, first, landed, passed, handed = self._copies(ins, outs, sems)
        for arrived, onward in zip(landed, passed):
            arrived.wait_recv()
            onward.start()
        for cp in handed:
            cp.wait_recv()
        for cp in first + passed:
            cp.wait_send()
        for cp in local:
            cp.wait()


class _ReduceExchange:
    def __init__(self, grads, axes):
        self.ins = list(grads)
        self.axes = list(axes)
        n = len(grads)
        self.out_shapes = [jax.ShapeDtypeStruct((N_DEV - 1,) + self._block(g, a), g.dtype) for g, a in zip(grads, axes)]
        self.sems = [pltpu.SemaphoreType.DMA((7 * n,)), pltpu.SemaphoreType.DMA((7 * n,))]

    @staticmethod
    def _block(g, axis):
        k, n = g.shape
        return (k // 2, n // N_CHIPS) if axis == 2 else (k // N_DEV, n)

    def _copies(self, ins, outs, sems):
        send_sems, recv_sems = sems
        x, y, c, _ = _place()
        cps = []
        for w, (g, axis) in enumerate(zip(ins, self.axes)):
            rows, cols = self._block(g, axis)
            for k in range(1, N_DEV):
                tx, ty, tc = (1 - x if k & 4 else x), (1 - y if k & 2 else y), (1 - c if k & 1 else c)
                chip = 2 * tx + ty
                if axis == 2:
                    src = g.at[pl.ds(tc * rows, rows), pl.ds(chip * cols, cols)]
                else:
                    src = g.at[pl.ds((2 * chip + tc) * rows, rows), :]
                cps.append(pltpu.make_async_remote_copy(
                    src_ref=src, dst_ref=outs[w].at[k - 1], send_sem=send_sems.at[7 * w + k - 1],
                    recv_sem=recv_sems.at[7 * w + k - 1], device_id=(tx, ty, tc), device_id_type=MESH))
        return cps

    def start(self, ins, outs, sems):
        for cp in self._copies(ins, outs, sems):
            cp.start()

    def finish(self, ins, outs, sems):
        cps = self._copies(ins, outs, sems)
        for cp in cps:
            cp.wait_recv()
        for cp in cps:
            cp.wait_send()


def _run_exchange(ex, *, name):
    n_in, n_out = len(ex.ins), len(ex.out_shapes)

    def body(*refs):
        ins, outs, sems = refs[:n_in], refs[n_in:n_in + n_out], refs[n_in + n_out:]
        ex.start(ins, outs, sems)
        ex.finish(ins, outs, sems)

    return pl.pallas_call(body, name=name, in_specs=[ANY] * n_in, out_specs=[ANY] * n_out, out_shape=ex.out_shapes,
                          scratch_shapes=ex.sems)(*ex.ins)


def _all_reduce_small(buf, *, name):
    rows = buf.shape[0]

    def body(x_ref, sum_ref, all_ref, send_sems, recv_sems, local_sem):
        x, y, c, chips = _place()
        me, sibling = (x, y, c), (x, y, 1 - c)

        def slab(px, py, pc):
            return all_ref.at[pl.ds((4 * px + 2 * py + pc) * rows, rows), :]

        def copy(k, block, to, src=None):
            return pltpu.make_async_remote_copy(
                src_ref=slab(*block) if src is None else src, dst_ref=slab(*block), send_sem=send_sems.at[k],
                recv_sem=recv_sems.at[k], device_id=to, device_id_type=MESH)

        mine = pltpu.make_async_copy(x_ref, slab(*me), local_sem)
        mine.start()
        first = [copy(0, me, sibling, src=x_ref)]
        first += [copy(1 + j, me, (*chip, c), src=x_ref) for j, chip in enumerate(chips)]
        for cp in first:
            cp.start()
        passed = [copy(4 + j, (*chip, c), sibling) for j, chip in enumerate(chips)]
        for j, chip in enumerate(chips):
            copy(1 + j, (*chip, c), me).wait_recv()
            passed[j].start()
        copy(0, sibling, me).wait_recv()
        for j, chip in enumerate(chips):
            copy(4 + j, (*chip, 1 - c), me).wait_recv()
        for cp in first + passed:
            cp.wait_send()
        mine.wait()
        acc = all_ref[0:rows, :]
        for d in range(1, N_DEV):
            acc = acc + all_ref[d * rows:(d + 1) * rows, :]
        sum_ref[...] = acc

    vmem = pl.BlockSpec(memory_space=pltpu.VMEM)
    return pl.pallas_call(
        body, name=name, in_specs=[vmem], out_specs=[vmem, vmem],
        out_shape=[jax.ShapeDtypeStruct((rows, 128), F32), jax.ShapeDtypeStruct((N_DEV * rows, 128), F32)],
        scratch_shapes=[pltpu.SemaphoreType.DMA((7,)), pltpu.SemaphoreType.DMA((7,)), pltpu.SemaphoreType.DMA],
        compiler_params=pltpu.CompilerParams(vmem_limit_bytes=VMEM_BIG),
    )(buf)[0]


WEIGHTS = ['ev_w_in', 'ev_lambda_re', 'ev_lambda_im', 'ev_log_dt', 'ev_b_re', 'ev_b_im', 'ev_c_re', 'ev_c_im', 'ev_d',
           'ev_w_glu', 'ev_b_glu', 'ev_conv_w', 'ev_w_out', 'od_w_in', 'od_rel_bias', 'od_pool_w', 'od_pool_scale',
           'od_w_out', 'ln_mix_g', 'ln_mix_b', 'ln_ffn_g', 'ln_ffn_b', 'ffn_w_up', 'ffn_w_down', 'ple_w_proj',
           'ple_w_gate', 'ple_b_gate']
INPUTS = ['x', 'p'] + WEIGHTS + ['loss_target'] + ['m_' + n for n in WEIGHTS] + ['v_' + n for n in WEIGHTS]

BIG = {
    'ev_w_in': (2, (2, 1024, 2048)), 'ev_w_glu': (1, (2, 512, 512)), 'ev_w_out': (1, (2, 1024, 1024)),
    'od_w_in': (2, (2, 1024, 2048)), 'od_w_out': (1, (2, 1024, 1024)), 'ffn_w_up': (2, (4, 1024, 5632)),
    'ffn_w_down': (1, (4, 2816, 1024)), 'ple_w_proj': (2, (4, 256, 1024)), 'ple_w_gate': (1, (4, 1024, 1024)),
}
SMALL_SHARDED = {'ev_conv_w': (2, 3, 512), 'od_pool_scale': (2, 512)}
REPLICATED = [n for n in WEIGHTS if n not in BIG and n not in SMALL_SHARDED]


def _shard_rows(name):
    axis, (nl, k, n) = BIG[name]
    return (nl * k, n // N_CHIPS) if axis == 2 else (nl * k // N_CHIPS, n)


def _pack(arrs):
    flat = jnp.concatenate([a.reshape(-1) for a in arrs])
    total = flat.shape[0]
    padded = -(-total // 1024) * 1024
    return jnp.pad(flat, (0, padded - total)).reshape(padded // 128, 128)


def _unpack(buf, shapes):
    flat = buf.reshape(-1)
    out, pos = [], 0
    for s in shapes:
        size = int(np.prod(s))
        out.append(flat[pos:pos + size].reshape(s))
        pos += size
    return out


def _s5_params(lam_re, lam_im, log_dt, b_re, b_im, c_re, c_im):
    dt = jnp.exp(log_dt)[:, None]
    mag = jnp.exp(lam_re * dt)
    ang = lam_im * dt
    lb_re = mag * jnp.cos(ang)
    lb_im = mag * jnp.sin(ang)
    den = lam_re * lam_re + lam_im * lam_im
    nr = lb_re - 1.0
    ni = lb_im
    r_re = (nr * lam_re + ni * lam_im) / den
    r_im = (ni * lam_re - nr * lam_im) / den
    bb_re = r_re[..., None] * b_re - r_im[..., None] * b_im
    bb_im = r_re[..., None] * b_im + r_im[..., None] * b_re
    per = S5_GROUPS // S5_SLABS
    eye = jnp.eye(per, dtype=F32)

    def block_diag(a):
        _, r, c = a.shape
        a = a.reshape(S5_SLABS, per, r, c)
        return (a[:, :, :, None, :] * eye[None, :, None, :, None]).reshape(S5_SLABS, per * r, per * c)

    bmat = jnp.concatenate([block_diag(bb_re.transpose(0, 2, 1)), block_diag(bb_im.transpose(0, 2, 1))], axis=2)
    cmat = jnp.concatenate([block_diag(c_re.transpose(0, 2, 1)), block_diag(-c_im.transpose(0, 2, 1))], axis=1)
    lam = jnp.stack([lb_re.reshape(S5_N), lb_im.reshape(S5_N)])
    return lam, bmat, cmat


def _lam_powers(lam):
    res, ims = [lam[0]], [lam[1]]
    for _ in range(7):
        res, ims = res + [res[-1] * lam[0] - ims[-1] * lam[1]], ims + [res[-1] * lam[1] + ims[-1] * lam[0]]
    return jnp.stack(res + ims + res[::-1] + ims[::-1])


def _layer_big(i):
    mixer = [('w_in', 'ev_w_in'), ('w_glu', 'ev_w_glu'), ('w_out', 'ev_w_out')] if i % 2 == 0 else \
        [('w_in', 'od_w_in'), ('w_out', 'od_w_out')]
    ffn = [('w_up', 'ffn_w_up'), ('w_down', 'ffn_w_down'), ('w_proj', 'ple_w_proj'), ('w_gate', 'ple_w_gate')]
    return [(k, n, i // 2) for k, n in mixer] + [(k, n, i) for k, n in ffn]


class _WholePlan:
    def __init__(self, whole):
        self.whole = whole
        self.grads = {n: {} for n in BIG}

    def layer_weights(self, i):
        return {k: self.whole[n][l] for k, n, l in _layer_big(i)}

    def forward_host(self, i):
        return None

    def backward_host(self, i):
        return None

    def layer_grads(self, i, g):
        for k, n, l in _layer_big(i):
            self.grads[n][l] = g[k][0]


def _local_step(x, p, target, w, plan):
    mask = jnp.asarray(_band_mask())
    diag_idx = _diag_index()
    onehot = jnp.asarray(np.eye(2 * MAX_REL + 1, dtype=np.float32)[diag_idx])
    saved = []
    for i in range(DEPTH):
        li = i // 2
        lw = plan.layer_weights(i)
        s = {'x0': x, 'lw': lw}
        h = _mm([(x, 0, D_MODEL)], lw['w_in'], name=f"in_proj")
        if i % 2 == 0:
            (lam, bmat, cmat), s5_vjp = jax.vjp(
                _s5_params, w['ev_lambda_re'][li], w['ev_lambda_im'][li], w['ev_log_dt'][li], w['ev_b_re'][li],
                w['ev_b_im'][li], w['ev_c_re'][li], w['ev_c_im'][li])
            s5c = (bmat.astype(BF16), cmat.astype(BF16), w['ev_d'][li].reshape(1, MIX), lw['w_glu'],
                   w['ev_b_glu'][li].reshape(1, MIX), _lam_powers(lam))
            ya, ypre, hb = _s5_fwd(h, *s5c, name=f"s5_fwd")
            yb = _conv_fwd(h, w['ev_conv_w'][li], name=f"conv_fwd")
            s.update(s5_vjp=s5_vjp, s5c=s5c, ypre=ypre, hb=hb)
        else:
            vdiag = jnp.dot(w['od_rel_bias'][li], onehot.T, precision=HIGHEST).reshape(ATT_HEADS // 2, 2, NKEY)
            pw = w['od_pool_w'][li].astype(BF16)
            ps = w['od_pool_scale'][li].reshape(1, MIX)
            ya = _attn_fwd(h, vdiag, mask, name=f"attn_fwd")
            yb = _pool_fwd(h, pw, ps, name=f"pool_fwd")
            s.update(vdiag=vdiag, pw=pw, ps=ps)
        wout = lw['w_out']
        vec = lambda n: w[n][i].reshape(1, -1)

        def residual_ln(products, rows, vecs):
            r = ALPHA * rows[0] + products[0]
            return (r, _ln_apply(r, vecs[0], vecs[1])), ()

        def embed_gate(products, rows, vecs):
            gate = _sigmoid(products[0] + vecs[0])
            return (rows[0] + gate * products[1], gate, products[1]), ()

        two_f32 = [(D_MODEL, F32), (D_MODEL, F32)]
        r1, x1 = _mm_rows([([(ya, 0, MIX), (yb, 0, MIX)], wout, False)], [x], [vec('ln_mix_g'), vec('ln_mix_b')],
                          two_f32, [], residual_ln, name="out_proj_ln")
        hosted = plan.forward_host(i)
        if hosted is None:
            a, gg, uu = _ffn_up(x1, lw['w_up'], name=f"ffn_up")
        else:
            (a, gg, uu), arrived = _ffn_up(x1, lw['w_up'], exchange=hosted, name=f"ffn_up_gather")
            plan.forward_hosted(i, arrived)
        r2, x2 = _mm_rows([([(a, 0, D_FF)], lw['w_down'], False)], [x1], [vec('ln_ffn_g'), vec('ln_ffn_b')],
                          two_f32, [], residual_ln, name="ffn_down_ln")
        x3, gate, ppb = _mm_rows(
            [([(x2, 0, D_MODEL)], lw['w_gate'], False), ([(p[i], 0, D_PLE)], lw['w_proj'], False)],
            [x2], [vec('ple_b_gate')], [(D_MODEL, F32), (D_MODEL, BF16), (D_MODEL, BF16)], [], embed_gate, name="ple")
        s.update(h=h, ya=ya, yb=yb, r1=r1, x1=x1, a=a, gg=gg, uu=uu, r2=r2, x2=x2, gate=gate, ppb=ppb)
        saved.append(s)
        x = x3

    loss, da = _loss_head(x, target, name="loss_head")
    db = None
    grads = {n: [None] * (DEPTH if n.startswith(('ln_', 'ple_')) else DEPTH // 2) for n in WEIGHTS if n not in BIG}

    def both(pieces, axis):
        return tuple(jnp.concatenate([pc[k] for pc in pieces], axis=axis) for k in range(2))

    for i in reversed(range(DEPTH)):
        li = i // 2
        s = saved[i]
        lw = s['lw']
        big = {}
        dz, dpp, dr2, dbg, dg2, db2 = _ple_ln_bwd(da, db, s['gate'], s['ppb'], s['r2'], lw['w_gate'],
                                                  w['ln_ffn_g'][i].reshape(1, -1), name="ple_ln_bwd")
        grads['ple_b_gate'][i] = dbg.reshape(-1)
        big['w_gate'] = _mm_tn(s['x2'], 0, D_MODEL, dz, also_bf16=True, name=f"d_ple_gate")
        big['w_proj'] = _mm_tn(p[i], 0, D_PLE, dpp, also_bf16=True, name=f"d_ple_proj")
        grads['ln_ffn_g'][i] = dg2.reshape(-1)
        grads['ln_ffn_b'][i] = db2.reshape(-1)
        dhh = _ffn_down_bwd(dr2, lw['w_down'], s['gg'], s['uu'], name=f"ffn_down_bwd")
        big['w_down'] = _mm_tn(s['a'], 0, D_FF, dr2, tk=D_FF // 2, also_bf16=True, name=f"d_ffn_down")
        hosted = plan.backward_host(i)
        if hosted is None:
            big['w_up'] = _mm_tn(s['x1'], 0, D_MODEL, dhh, tn=D_FF // 2, also_bf16=True, name=f"d_ffn_up")
        else:
            big['w_up'], arrived = _mm_tn(s['x1'], 0, D_MODEL, dhh, tn=D_FF // 2, also_bf16=True, exchange=hosted,
                                          name=f"d_ffn_up_reduce_{i % 2}")
            plan.backward_hosted(i, arrived)

        def ln_mix_grad(products, rows, vecs):
            dr, dg, dbias = _ln_grad(rows[0], ALPHA * rows[1] + products[0], vecs[0])
            return (dr,), (dg, dbias)

        dr1, dg1, db1 = _mm_rows([([(dhh, 0, 2 * D_FF)], lw['w_up'], True)], [s['r1'], dr2],
                                 [w['ln_mix_g'][i].reshape(1, -1)], [(D_MODEL, F32)], [D_MODEL, D_MODEL], ln_mix_grad,
                                 tm=256, vmem=VMEM_BIG, name="ffn_up_ln_bwd")
        grads['ln_mix_g'][i] = dg1.reshape(-1)
        grads['ln_mix_b'][i] = db1.reshape(-1)
        dcat = _mm([(dr1, 0, D_MODEL)], lw['w_out'], trans_b=True, name=f"out_proj_bwd")
        big['w_out'] = both([_mm_tn(s['ya'], 0, MIX, dr1, also_bf16=True, name=f"d_out_a"),
                             _mm_tn(s['yb'], 0, MIX, dr1, also_bf16=True, name=f"d_out_b")], 0)
        h = s['h']
        if i % 2 == 0:
            s5c = s['s5c']
            du, xb, gb, gq, dzzq, dyq, dlam, dbglu, dd = _s5_bwd(dcat, s['ypre'], h, s['hb'], *s5c, name=f"s5_bwd")
            dbmat = _mm_tn_slabs(h, 128, gb, SLAB_COLS, S5_SLABS, name=f"d_s5_b")
            dcmat = _mm_tn_slabs(xb, SLAB_COLS, dyq, 128, S5_SLABS, name=f"d_s5_c")
            s5g = s['s5_vjp']((dlam, dbmat, dcmat))
            for n, g_ in zip(['ev_lambda_re', 'ev_lambda_im', 'ev_log_dt', 'ev_b_re', 'ev_b_im', 'ev_c_re', 'ev_c_im'], s5g):
                grads[n][li] = g_
            big['w_glu'] = _mm_tn(gq, 0, MIX, dzzq, also_bf16=True, name=f"d_glu")
            grads['ev_b_glu'][li] = dbglu.reshape(-1)
            grads['ev_d'][li] = dd.reshape(-1)
            d3, dcw = _conv_bwd(dcat, h, w['ev_conv_w'][li], name=f"conv_bwd")
            grads['ev_conv_w'][li] = dcw[0:3]
            big['w_in'] = both([_mm_tn(s['x0'], 0, D_MODEL, du, also_bf16=True, name=f"d_in_a"),
                                _mm_tn(s['x0'], 0, D_MODEL, d3, tn=3 * MIX, also_bf16=True, name=f"d_in_b")], 1)
            db = _mm([(du, 0, MIX), (d3, 0, 3 * MIX)], lw['w_in'], trans_b=True, name=f"in_proj_bwd")
        else:
            dq, dk, dv, dvd = _attn_bwd(dcat, h, s['vdiag'], mask, name=f"attn_bwd")
            dzp, dpw, dps = _pool_bwd(dcat, h, s['pw'], s['ps'], name=f"pool_bwd")
            parts = [dq, dk, dv, dzp]
            grads['od_rel_bias'][li] = jnp.dot(dvd.reshape(ATT_HEADS, NKEY), onehot, precision=HIGHEST)
            grads['od_pool_w'][li] = dpw
            grads['od_pool_scale'][li] = dps.reshape(-1)
            big['w_in'] = both([_mm_tn(s['x0'], 0, D_MODEL, d_, also_bf16=True, name=f"d_in_a") for d_ in parts], 1)
            db = _mm([(d_, 0, MIX) for d_ in parts], lw['w_in'], trans_b=True, name=f"in_proj_bwd")
        plan.layer_grads(i, big)
        da = dr1
    grad_x = _add_n([da, db], [ALPHA, 1.0], name="grad_x")
    return loss, grad_x, {n: jnp.stack(g) for n, g in grads.items()}


def _sum_blocks(own, others, *, name):
    rows, cols = own.shape
    t = _row_tile(rows)

    def body(own_ref, others_ref, o_ref):
        acc = own_ref[...]
        for k in range(N_DEV - 1):
            acc = acc + others_ref[k].astype(F32)
        o_ref[...] = acc

    return pl.pallas_call(
        body, name=name, grid=(rows // t,),
        in_specs=[pl.BlockSpec((t, cols), lambda i: (i, 0)), pl.BlockSpec((N_DEV - 1, t, cols), lambda i: (0, i, 0))],
        out_specs=pl.BlockSpec((t, cols), lambda i: (i, 0)), out_shape=jax.ShapeDtypeStruct((rows, cols), F32),
        compiler_params=_cparams(("parallel",)),
    )(own, others)


def _share_halves(items, out_shapes, *, name):
    n = len(items)

    def body(*refs):
        ins, outs = refs[:n], refs[n:n + len(out_shapes)]
        send_sems, recv_sems, local_sems = refs[n + len(out_shapes):]
        x, y, c, _ = _place()
        local, remote, arriving = [], [], []
        for k, (half, dst, layer) in enumerate(items):
            rows = half.shape[0]
            mine = outs[dst].at[layer, pl.ds(c * rows, rows)]
            theirs = outs[dst].at[layer, pl.ds((1 - c) * rows, rows)]
            local.append(pltpu.make_async_copy(ins[k], mine, local_sems.at[k]))
            remote.append(pltpu.make_async_remote_copy(src_ref=ins[k], dst_ref=mine, send_sem=send_sems.at[k],
                                                       recv_sem=recv_sems.at[k], device_id=(x, y, 1 - c), device_id_type=MESH))
            arriving.append(pltpu.make_async_remote_copy(src_ref=ins[k], dst_ref=theirs, send_sem=send_sems.at[k],
                                                         recv_sem=recv_sems.at[k], device_id=(x, y, 1 - c), device_id_type=MESH))
        for cp in local + remote:
            cp.start()
        for cp in arriving:
            cp.wait_recv()
        for cp in remote:
            cp.wait_send()
        for cp in local:
            cp.wait()

    return pl.pallas_call(
        body, name=name, in_specs=[ANY] * n, out_specs=[ANY] * len(out_shapes), out_shape=out_shapes,
        scratch_shapes=[pltpu.SemaphoreType.DMA((n,)), pltpu.SemaphoreType.DMA((n,)), pltpu.SemaphoreType.DMA((n,))],
    )(*[half for half, _, _ in items])


class _ShardedPlan:
    def __init__(self, a, c, me):
        self.a, self.c, self.me = a, c, me
        self.weights, self.pending, self.own, self.arrived = {}, None, {}, {}

    def _shards(self, i):
        return [self.a[n][l].astype(BF16) for _, n, l in _layer_big(i)]

    def _set_weights(self, i, gathered):
        lw = {}
        for (k, n, _), g in zip(_layer_big(i), gathered):
            _, rows, cols = g.shape
            lw[k] = g.transpose(1, 0, 2).reshape(rows, N_CHIPS * cols) if BIG[n][0] == 2 else g.reshape(N_CHIPS * rows, cols)
        self.weights[i] = lw

    def gather_first(self, misc):
        gathered = _run_exchange(_GatherExchange(self._shards(0) + [misc]), name="weight_gather_0")
        self._set_weights(0, gathered[:-1])
        return gathered[-1]

    def layer_weights(self, i):
        return self.weights.pop(i)

    def forward_host(self, i):
        return _GatherExchange(self._shards(i + 1)) if i + 1 < DEPTH else None

    def forward_hosted(self, i, arrived):
        self._set_weights(i + 1, arrived)

    def _reduce_exchange(self):
        i, g = self.pending
        return _ReduceExchange([g[k][1] for k, _, _ in _layer_big(i)], [BIG[n][0] for _, n, _ in _layer_big(i)])

    def layer_grads(self, i, g):
        for k, n, l in _layer_big(i):
            full = g[k][0]
            kk, nn = full.shape
            if BIG[n][0] == 2:
                self.own[n, l] = lax.dynamic_slice(full, (self.c * (kk // 2), self.me * (nn // N_CHIPS)), (kk // 2, nn // N_CHIPS))
            else:
                self.own[n, l] = lax.dynamic_slice_in_dim(full, (2 * self.me + self.c) * (kk // N_DEV), kk // N_DEV, axis=0)
        self.pending = (i, g)

    def backward_host(self, i):
        return self._reduce_exchange() if i + 1 < DEPTH else None

    def backward_hosted(self, i, arrived):
        for (_, n, l), r in zip(_layer_big(i + 1), arrived):
            self.arrived[n, l] = r

    def reduced(self):
        for (_, n, l), r in zip(_layer_big(0), _run_exchange(self._reduce_exchange(), name="grad_reduce_0")):
            self.arrived[n, l] = r
        names = list(BIG)
        items = []
        for n in names:
            for l in range(BIG[n][1][0]):
                items.append((_sum_blocks(self.own[n, l], self.arrived[n, l], name=f"grad_sum_{n}"), names.index(n), l))
        shared = _share_halves(items, [jax.ShapeDtypeStruct(self.a[n].shape, F32) for n in names], name="grad_half_share")
        return dict(zip(names, shared))


def kernel(x, p, ev_w_in, ev_lambda_re, ev_lambda_im, ev_log_dt, ev_b_re, ev_b_im, ev_c_re, ev_c_im, ev_d, ev_w_glu, ev_b_glu, ev_conv_w, ev_w_out, od_w_in, od_rel_bias, od_pool_w, od_pool_scale, od_w_out, ln_mix_g, ln_mix_b, ln_ffn_g, ln_ffn_b, ffn_w_up, ffn_w_down, ple_w_proj, ple_w_gate, ple_b_gate, loss_target, m_ev_w_in, m_ev_lambda_re, m_ev_lambda_im, m_ev_log_dt, m_ev_b_re, m_ev_b_im, m_ev_c_re, m_ev_c_im, m_ev_d, m_ev_w_glu, m_ev_b_glu, m_ev_conv_w, m_ev_w_out, m_od_w_in, m_od_rel_bias, m_od_pool_w, m_od_pool_scale, m_od_w_out, m_ln_mix_g, m_ln_mix_b, m_ln_ffn_g, m_ln_ffn_b, m_ffn_w_up, m_ffn_w_down, m_ple_w_proj, m_ple_w_gate, m_ple_b_gate, v_ev_w_in, v_ev_lambda_re, v_ev_lambda_im, v_ev_log_dt, v_ev_b_re, v_ev_b_im, v_ev_c_re, v_ev_c_im, v_ev_d, v_ev_w_glu, v_ev_b_glu, v_ev_conv_w, v_ev_w_out, v_od_w_in, v_od_rel_bias, v_od_pool_w, v_od_pool_scale, v_od_w_out, v_ln_mix_g, v_ln_mix_b, v_ln_ffn_g, v_ln_ffn_b, v_ffn_w_up, v_ffn_w_down, v_ple_w_proj, v_ple_w_gate, v_ple_b_gate):
    given = locals()
    a = {n: given[n] for n in INPUTS}
    x, y, c = lax.axis_index("x"), lax.axis_index("y"), lax.axis_index("c")
    me = 2 * x + y

    plan = _ShardedPlan(a, c, me)
    misc = jnp.concatenate([a['ev_conv_w'].reshape(6, 128), a['od_pool_scale'], jnp.zeros((8, 128), F32)], axis=0)
    gm = plan.gather_first(misc)
    w = {n: a[n] for n in REPLICATED}
    w['ev_conv_w'] = gm[:, 0:6].reshape(N_CHIPS, 2, 3, 128).transpose(1, 2, 0, 3).reshape(2, 3, 512)
    w['od_pool_scale'] = gm[:, 6:8].transpose(1, 0, 2).reshape(2, 512)

    loss, grad_x, grads = _local_step(a['x'][0], a['p'][:, 0], a['loss_target'][0], w, plan)
    loss = lax.psum(loss[0, 0], ("x", "y", "c"))

    small_names = REPLICATED + list(SMALL_SHARDED)
    small = _all_reduce_small(_pack([grads[n] for n in small_names]), name="small_grad_all_reduce")
    small = dict(zip(small_names, _unpack(small, [grads[n].shape for n in small_names])))
    for n in SMALL_SHARDED:
        small[n] = lax.dynamic_slice_in_dim(small[n], me * 128, 128, axis=small[n].ndim - 1)
    big = plan.reduced()

    res = {}
    for n in BIG:
        shape = a[n].shape
        flat = _shard_rows(n)
        d, m_, v_ = _adamw(a[n].reshape(flat), big[n].reshape(flat), a['m_' + n].reshape(flat), a['v_' + n].reshape(flat),
                           name=f"adamw_{n}")
        res[n] = (big[n], d.reshape(shape), m_.reshape(shape), v_.reshape(shape))
    shapes = [a[n].shape for n in small_names]
    d, m_, v_ = _adamw(_pack([a[n] for n in small_names]), _pack([small[n] for n in small_names]),
                       _pack([a['m_' + n] for n in small_names]), _pack([a['v_' + n] for n in small_names]), name="adamw_small")
    for n, dd, mm, vv in zip(small_names, _unpack(d, shapes), _unpack(m_, shapes), _unpack(v_, shapes)):
        res[n] = (small[n], dd, mm, vv)

    outs = [loss, grad_x[None]]
    for part in range(4):
        outs += [res[n][part] for n in WEIGHTS]
    return tuple(outs)
```

```python
import functools
import math

import jax
import jax.numpy as jnp
import numpy as np
from jax import lax
from jax.experimental import pallas as pl
from jax.experimental.pallas import tpu as pltpu

F32 = jnp.float32
BF16 = jnp.bfloat16
MESH = pl.DeviceIdType.MESH
HIGHEST = lax.Precision.HIGHEST

D_MODEL = 1024
DEPTH = 4
MIX = 512
S5_GROUPS = 32
S5_GROUP = 16
S5_STATE = 64
S5_N = S5_GROUPS * S5_STATE
CHUNK = 64
LEFT_CHUNKS = 8
MAX_REL = 128
ATT_HEADS = 8
HEAD_DIM = 64
POOL_WINDOWS = (2, 4, 8, 16)
POOL_GROUP = 128
D_FF = 2816
D_PLE = 256
ALPHA = (2 * DEPTH) ** 0.25
LN_EPS = 1e-5
NEG_INF = -1e30
N_CHIPS = 4
N_DEV = 8

ADAM_LR = 0.001
ADAM_B1 = 0.9
ADAM_B2 = 0.999
ADAM_EPS = 1e-08
ADAM_WD = 0.01
ADAM_STEP = 10

TM = 512
T_S5 = 256
T_ATT = 512
VMEM_BIG = 56 * 1024 * 1024


VMEM_DEFAULT = 48 * 1024 * 1024


def _cparams(sem, vmem=None):
    return pltpu.CompilerParams(dimension_semantics=sem, vmem_limit_bytes=vmem or VMEM_DEFAULT)


def _sigmoid(x):
    return 1.0 / (1.0 + jnp.exp(-x))


def _mm(a_parts, b, *, name, trans_b=False, out_dtype=F32, tm=TM, tn=1024, vmem=None):
    m = a_parts[0][0].shape[0]
    n = b.shape[0] if trans_b else b.shape[1]
    kk = b.shape[1] if trans_b else b.shape[0]
    tn = min(tn, n)
    widths = [w for _, _, w in a_parts]
    assert sum(widths) == kk and m % tm == 0 and n % tn == 0
    na = len(a_parts)

    def body(*refs):
        b_ref, o_ref = refs[na], refs[na + 1]
        acc = None
        k0 = 0
        for ar, w in zip(refs[:na], widths):
            a = ar[...].astype(BF16)
            if trans_b:
                part = lax.dot_general(a, b_ref[:, k0:k0 + w], (((1,), (1,)), ((), ())), preferred_element_type=F32)
            else:
                part = jnp.dot(a, b_ref[k0:k0 + w, :], preferred_element_type=F32)
            acc = part if acc is None else acc + part
            k0 += w
        o_ref[...] = acc.astype(o_ref.dtype)

    in_specs = [pl.BlockSpec((tm, w), functools.partial(lambda j, i, cb: (i, cb), cb=cb)) for _, cb, w in a_parts]
    if trans_b:
        in_specs.append(pl.BlockSpec((tn, kk), lambda j, i: (j, 0)))
    else:
        in_specs.append(pl.BlockSpec((kk, tn), lambda j, i: (0, j)))
    return pl.pallas_call(
        body, name=name, grid=(n // tn, m // tm), in_specs=in_specs,
        out_specs=pl.BlockSpec((tm, tn), lambda j, i: (i, j)),
        out_shape=jax.ShapeDtypeStruct((m, n), out_dtype),
        compiler_params=_cparams(("parallel", "parallel"), vmem),
    )(*[a for a, _, _ in a_parts], b)


def _host_parts(exchange):
    if exchange is None:
        return [], [], [], [], []
    any_space = pl.BlockSpec(memory_space=pl.ANY)
    return (exchange.ins, [any_space] * len(exchange.ins), [any_space] * len(exchange.out_shapes),
            list(exchange.out_shapes), list(exchange.sems))


def _host_run(exchange, refs, first, last):
    if exchange is None:
        return
    n_in, n_out = len(exchange.ins), len(exchange.out_shapes)
    ins, outs, sems = refs[:n_in], refs[n_in:n_in + n_out], refs[n_in + n_out:]

    @pl.when(first)
    def _():
        exchange.start(ins, outs, sems)

    @pl.when(last)
    def _():
        exchange.finish(ins, outs, sems)


def _mm_tn(a, a_cb, ka, b, *, name, tk=1024, tn=1024, tmr=2 * TM, vmem=None, also_bf16=False, exchange=None):
    m = a.shape[0]
    n = b.shape[1]
    tk = min(tk, ka)
    tn = min(tn, n)
    assert ka % tk == 0 and n % tn == 0 and m % tmr == 0
    kb = ka // tk
    grid = (kb, n // tn, m // tmr)
    ex_ops, ex_in_specs, ex_out_specs, ex_out_shapes, ex_scratch = _host_parts(exchange)
    n_own_out = 2 if also_bf16 else 1

    def body(*refs):
        a_ref, b_ref = refs[:2]
        hosted_in = refs[2:2 + len(ex_ops)]
        outs = refs[2 + len(ex_ops):]
        o_ref = outs[0]
        k, j, r = pl.program_id(0), pl.program_id(1), pl.program_id(2)
        _host_run(exchange, list(hosted_in) + list(outs[n_own_out:]),
                  (k == 0) & (j == 0) & (r == 0), (k == grid[0] - 1) & (j == grid[1] - 1) & (r == grid[2] - 1))

        @pl.when(r == 0)
        def _():
            o_ref[...] = jnp.zeros_like(o_ref)

        o_ref[...] += lax.dot_general(a_ref[...].astype(BF16), b_ref[...].astype(BF16), (((0,), (0,)), ((), ())),
                                      preferred_element_type=F32)
        if also_bf16:
            @pl.when(r == grid[2] - 1)
            def _():
                outs[1][...] = o_ref[...].astype(BF16)

    tile = pl.BlockSpec((tk, tn), lambda k, j, r: (k, j))
    res = pl.pallas_call(
        body, name=name, grid=grid,
        in_specs=[pl.BlockSpec((tmr, tk), lambda k, j, r: (r, a_cb * kb + k)),
                  pl.BlockSpec((tmr, tn), lambda k, j, r: (r, j))] + ex_in_specs,
        out_specs=[tile] * n_own_out + ex_out_specs,
        out_shape=[jax.ShapeDtypeStruct((ka, n), F32)] + ([jax.ShapeDtypeStruct((ka, n), BF16)] if also_bf16 else [])
        + ex_out_shapes,
        scratch_shapes=ex_scratch,
        compiler_params=_cparams(("arbitrary",) * 3 if exchange is not None else ("parallel", "parallel", "arbitrary"), vmem),
    )(a, b, *ex_ops)
    if exchange is None:
        return tuple(res) if also_bf16 else res[0]
    own = tuple(res[:n_own_out]) if also_bf16 else res[0]
    return own, list(res[n_own_out:])


def _mm_tn_slabs(a, ka, b, nbw, nslab, *, name, tmr=2 * TM):
    m = a.shape[0]
    assert m % tmr == 0

    def body(a_ref, b_ref, o_ref):
        @pl.when(pl.program_id(1) == 0)
        def _():
            o_ref[...] = jnp.zeros_like(o_ref)

        o_ref[0] += lax.dot_general(a_ref[...].astype(BF16), b_ref[...].astype(BF16), (((0,), (0,)), ((), ())),
                                    preferred_element_type=F32)

    return pl.pallas_call(
        body, name=name, grid=(nslab, m // tmr),
        in_specs=[pl.BlockSpec((tmr, ka), lambda s, r: (r, s)), pl.BlockSpec((tmr, nbw), lambda s, r: (r, s))],
        out_specs=pl.BlockSpec((1, ka, nbw), lambda s, r: (s, 0, 0)),
        out_shape=jax.ShapeDtypeStruct((nslab, ka, nbw), F32),
        compiler_params=_cparams(("parallel", "arbitrary")),
    )(a, b)


def _ln_stats(r):
    mu = jnp.mean(r, axis=-1, keepdims=True)
    xc = r - mu
    var = jnp.mean(xc * xc, axis=-1, keepdims=True)
    rstd = lax.rsqrt(var + LN_EPS)
    return xc * rstd, rstd


def _ln_apply(r, g, b):
    xhat, _ = _ln_stats(r)
    return xhat * g + b


def _ln_grad(r, dy, g):
    xhat, rstd = _ln_stats(r)
    dxh = dy * g
    m1 = jnp.mean(dxh, axis=-1, keepdims=True)
    m2 = jnp.mean(dxh * xhat, axis=-1, keepdims=True)
    return (rstd * (dxh - m1 - xhat * m2), jnp.sum(dy * xhat, axis=0, keepdims=True), jnp.sum(dy, axis=0, keepdims=True))


def _mm_rows(matmuls, rows_in, vecs_in, out_rows, acc_widths, fn, *, name, tm=TM, vmem=None):
    m = rows_in[0].shape[0]
    assert m % tm == 0
    flat, in_specs, layout = [], [], []
    for a_parts, b, trans_b in matmuls:
        for arr, cb, w in a_parts:
            flat.append(arr)
            in_specs.append(pl.BlockSpec((tm, w), functools.partial(lambda i, cb: (i, cb), cb=cb)))
        flat.append(b)
        in_specs.append(pl.BlockSpec(b.shape, lambda i: (0, 0)))
        layout.append(([w for _, _, w in a_parts], trans_b))
    for r in rows_in:
        flat.append(r)
        in_specs.append(pl.BlockSpec((tm, r.shape[1]), lambda i: (i, 0)))
    for v in vecs_in:
        flat.append(v)
        in_specs.append(pl.BlockSpec(v.shape, lambda i: (0, 0)))
    n_in = len(flat)
    n_rows_out = len(out_rows)

    def body(*refs):
        pos = 0
        products = []
        for widths, trans_b in layout:
            b_ref = refs[pos + len(widths)]
            acc, k0 = None, 0
            for ar, w in zip(refs[pos:pos + len(widths)], widths):
                a = ar[...].astype(BF16)
                if trans_b:
                    part = lax.dot_general(a, b_ref[:, k0:k0 + w], (((1,), (1,)), ((), ())), preferred_element_type=F32)
                else:
                    part = jnp.dot(a, b_ref[k0:k0 + w, :], preferred_element_type=F32)
                acc = part if acc is None else acc + part
                k0 += w
            products.append(acc)
            pos += len(widths) + 1
        rows = [r[...] for r in refs[pos:pos + len(rows_in)]]
        pos += len(rows_in)
        vecs = [v[...] for v in refs[pos:n_in]]
        outs, sums = fn(products, rows, vecs)
        for o_ref, o in zip(refs[n_in:n_in + n_rows_out], outs):
            o_ref[...] = o.astype(o_ref.dtype)
        if acc_widths:
            acc_refs = refs[n_in + n_rows_out:]

            @pl.when(pl.program_id(0) == 0)
            def _():
                for a_ref in acc_refs:
                    a_ref[...] = jnp.zeros_like(a_ref)

            for a_ref, s_ in zip(acc_refs, sums):
                a_ref[...] += s_

    out_specs = [pl.BlockSpec((tm, n), lambda i: (i, 0)) for n, _ in out_rows]
    out_specs += [pl.BlockSpec((1, wd), lambda i: (0, 0)) for wd in acc_widths]
    out_shape = [jax.ShapeDtypeStruct((m, n), dt) for n, dt in out_rows]
    out_shape += [jax.ShapeDtypeStruct((1, wd), F32) for wd in acc_widths]
    return pl.pallas_call(
        body, name=name, grid=(m // tm,), in_specs=in_specs, out_specs=out_specs, out_shape=out_shape,
        compiler_params=_cparams(("arbitrary",) if acc_widths else ("parallel",), vmem),
    )(*flat)


def _ffn_up(x1, wup, *, name, exchange=None):
    m = x1.shape[0]
    tn = D_FF // 2
    grid = (2, m // TM)
    ex_ops, ex_in_specs, ex_out_specs, ex_out_shapes, ex_scratch = _host_parts(exchange)

    def body(*refs):
        x_ref, wg_ref, wu_ref = refs[:3]
        hosted_in = refs[3:3 + len(ex_ops)]
        a_ref, g_ref, u_ref = refs[3 + len(ex_ops):6 + len(ex_ops)]
        j, i = pl.program_id(0), pl.program_id(1)
        _host_run(exchange, list(hosted_in) + list(refs[6 + len(ex_ops):]),
                  (j == 0) & (i == 0), (j == grid[0] - 1) & (i == grid[1] - 1))
        x = x_ref[...].astype(BF16)
        g = jnp.dot(x, wg_ref[...], preferred_element_type=F32)
        u = jnp.dot(x, wu_ref[...], preferred_element_type=F32)
        a_ref[...] = (g * _sigmoid(g) * u).astype(BF16)
        g_ref[...] = g.astype(BF16)
        u_ref[...] = u.astype(BF16)

    out = pl.BlockSpec((TM, tn), lambda j, i: (i, j))
    res = pl.pallas_call(
        body, name=name, grid=grid,
        in_specs=[pl.BlockSpec((TM, D_MODEL), lambda j, i: (i, 0)),
                  pl.BlockSpec((D_MODEL, tn), lambda j, i: (0, j)),
                  pl.BlockSpec((D_MODEL, tn), lambda j, i: (0, j + 2))] + ex_in_specs,
        out_specs=[out, out, out] + ex_out_specs,
        out_shape=[jax.ShapeDtypeStruct((m, D_FF), BF16)] * 3 + ex_out_shapes,
        scratch_shapes=ex_scratch,
        compiler_params=_cparams(("arbitrary", "arbitrary") if exchange is not None else ("parallel", "parallel")),
    )(x1, wup, wup, *ex_ops)
    return (res[0], res[1], res[2]) if exchange is None else ((res[0], res[1], res[2]), list(res[3:]))


def _ffn_down_bwd(df, wdown, g, u, *, name):
    m = df.shape[0]
    tm = 256

    def body(df_ref, w_ref, g_ref, u_ref, o_ref):
        da = lax.dot_general(df_ref[...].astype(BF16), w_ref[...], (((1,), (1,)), ((), ())), preferred_element_type=F32)
        gg = g_ref[...].astype(F32)
        sg = _sigmoid(gg)
        o_ref[:, :D_FF] = (da * u_ref[...].astype(F32) * (sg * (1.0 + gg * (1.0 - sg)))).astype(BF16)
        o_ref[:, D_FF:] = (da * (gg * sg)).astype(BF16)

    return pl.pallas_call(
        body, name=name, grid=(m // tm,),
        in_specs=[pl.BlockSpec((tm, D_MODEL), lambda i: (i, 0)), pl.BlockSpec((D_FF, D_MODEL), lambda i: (0, 0)),
                  pl.BlockSpec((tm, D_FF), lambda i: (i, 0)), pl.BlockSpec((tm, D_FF), lambda i: (i, 0))],
        out_specs=pl.BlockSpec((tm, 2 * D_FF), lambda i: (i, 0)),
        out_shape=jax.ShapeDtypeStruct((m, 2 * D_FF), BF16),
        compiler_params=_cparams(("parallel",), VMEM_BIG),
    )(df, wdown, g, u)


def _ple_ln_bwd(da, db, gate, pp, r2, wgate, g2, *, name):
    m, n = da.shape
    two = db is not None
    n_in = 7 if two else 6

    def body(*refs):
        if two:
            da_ref, db_ref, gate_ref, pp_ref, r_ref, w_ref, g_ref = refs[:n_in]
            dx3 = ALPHA * da_ref[...] + db_ref[...]
        else:
            da_ref, gate_ref, pp_ref, r_ref, w_ref, g_ref = refs[:n_in]
            dx3 = da_ref[...]
        dz_ref, dpp_ref, dr_ref, dbg_ref, dg_ref, dbias_ref = refs[n_in:]

        @pl.when(pl.program_id(0) == 0)
        def _():
            dbg_ref[...] = jnp.zeros_like(dbg_ref)
            dg_ref[...] = jnp.zeros_like(dg_ref)
            dbias_ref[...] = jnp.zeros_like(dbias_ref)

        gate = gate_ref[...].astype(F32)
        dz = dx3 * pp_ref[...].astype(F32) * gate * (1.0 - gate)
        dzq = dz.astype(BF16)
        dz_ref[...] = dzq
        dpp_ref[...] = (dx3 * gate).astype(BF16)
        dbg_ref[...] += jnp.sum(dz, axis=0, keepdims=True)
        dx2 = dx3 + lax.dot_general(dzq, w_ref[...], (((1,), (1,)), ((), ())), preferred_element_type=F32)
        dr, dg, dbias = _ln_grad(r_ref[...], dx2, g_ref[...])
        dr_ref[...] = dr
        dg_ref[...] += dg
        dbias_ref[...] += dbias

    row = pl.BlockSpec((TM, n), lambda i: (i, 0))
    vec = pl.BlockSpec((1, n), lambda i: (0, 0))
    ins = ([da, db] if two else [da]) + [gate, pp, r2, wgate, g2]
    in_specs = [row] * (n_in - 2) + [pl.BlockSpec(wgate.shape, lambda i: (0, 0)), vec]
    return pl.pallas_call(
        body, name=name, grid=(m // TM,), in_specs=in_specs, out_specs=[row, row, row, vec, vec, vec],
        out_shape=[jax.ShapeDtypeStruct((m, n), BF16), jax.ShapeDtypeStruct((m, n), BF16), jax.ShapeDtypeStruct((m, n), F32)]
        + [jax.ShapeDtypeStruct((1, n), F32)] * 3,
        compiler_params=_cparams(("arbitrary",)),
    )(*ins)


def _loss_head(y, target, *, name):
    m, n = y.shape

    def body(y_ref, t_ref, loss_ref, dy_ref):
        @pl.when(pl.program_id(0) == 0)
        def _():
            loss_ref[...] = jnp.zeros_like(loss_ref)

        err = y_ref[...] - t_ref[...]
        dy_ref[...] = err * (1.0 / n)
        per_tok = jnp.mean(err * err, axis=-1, keepdims=True)
        loss_ref[...] += 0.5 * jnp.sum(per_tok, axis=0, keepdims=True)

    row = pl.BlockSpec((TM, n), lambda i: (i, 0))
    return pl.pallas_call(
        body, name=name, grid=(m // TM,), in_specs=[row, row],
        out_specs=[pl.BlockSpec((1, 1), lambda i: (0, 0)), row],
        out_shape=[jax.ShapeDtypeStruct((1, 1), F32), jax.ShapeDtypeStruct((m, n), F32)],
        compiler_params=_cparams(("arbitrary",)),
    )(y, target)


def _gelu(y):
    c = math.sqrt(2.0 / math.pi)
    return 0.5 * y * (1.0 + jnp.tanh(c * (y + 0.044715 * y * y * y)))


def _gelu_grad(y):
    c = math.sqrt(2.0 / math.pi)
    t = jnp.tanh(c * (y + 0.044715 * y * y * y))
    return 0.5 * (1.0 + t) + 0.5 * y * (1.0 - t * t) * c * (1.0 + 3.0 * 0.044715 * y * y)


STRIP = 256
S5_SLABS = 4
SLAB_COLS = 2 * S5_N // S5_SLABS


def _strip_cols(j):
    off = pl.multiple_of(j * STRIP, STRIP)
    col = pl.multiple_of(j * STRIP + (j // 2) * (SLAB_COLS // 2), STRIP)
    return off, col, pl.multiple_of(col + SLAB_COLS // 2, STRIP)


def _scan_strip(xr, xi, cr, ci, ptab_ref, off, rowmod, down):
    t = xr.shape[0]
    base = 0 if down else 16
    p_r = ptab_ref[base:base + 8, pl.ds(off, STRIP)]
    p_i = ptab_ref[base + 8:base + 16, pl.ds(off, STRIP)]
    if not down:
        p_i = -p_i
    for k in range(3):
        s = 1 << k
        idx = s - 1 if down else 8 - s
        pr, pi_ = p_r[idx:idx + 1], p_i[idx:idx + 1]
        if down:
            sr = jnp.where(rowmod >= s, pltpu.roll(xr, s, 0), 0.0)
            si = jnp.where(rowmod >= s, pltpu.roll(xi, s, 0), 0.0)
        else:
            sr = jnp.where(rowmod < 8 - s, pltpu.roll(xr, t - s, 0), 0.0)
            si = jnp.where(rowmod < 8 - s, pltpu.roll(xi, t - s, 0), 0.0)
        xr, xi = xr + pr * sr - pi_ * si, xi + pr * si + pi_ * sr
    ng = t // 8
    out_r, out_i = [None] * ng, [None] * ng
    for g in (range(ng) if down else reversed(range(ng))):
        cbr = jnp.broadcast_to(cr, (8, STRIP))
        cbi = jnp.broadcast_to(ci, (8, STRIP))
        br = xr[8 * g:8 * g + 8] + p_r * cbr - p_i * cbi
        bi = xi[8 * g:8 * g + 8] + p_r * cbi + p_i * cbr
        cr, ci = (br[7:8], bi[7:8]) if down else (br[0:1], bi[0:1])
        out_r[g], out_i[g] = br, bi
    return jnp.concatenate(out_r, axis=0), jnp.concatenate(out_i, axis=0)


def _s5_fwd(h, bmat, cmat, dvec, wglu, bglu, ptab, *, name):
    m = h.shape[0]
    t = T_S5
    nb = m // t

    def body(u_ref, bmat_ref, cmat_ref, d_ref, wglu_ref, bglu_ref, ptab_ref,
             out_ref, y_ref, hb_ref, bu_ref, carry_ref):
        @pl.when(pl.program_id(0) == 0)
        def _():
            carry_ref[...] = jnp.zeros_like(carry_ref)

        hb_ref[0] = carry_ref[...]
        u = u_ref[...]
        ub = u.astype(BF16)
        for s in range(S5_SLABS):
            bu_ref[:, SLAB_COLS * s:SLAB_COLS * (s + 1)] = jnp.dot(ub[:, 128 * s:128 * (s + 1)], bmat_ref[s],
                                                                  preferred_element_type=F32)
        rowmod = lax.broadcasted_iota(jnp.int32, (t, STRIP), 0) & 7

        def strip(j, c):
            off, col, coli = _strip_cols(j)
            xr, xi = _scan_strip(bu_ref[:, pl.ds(col, STRIP)], bu_ref[:, pl.ds(coli, STRIP)],
                                 carry_ref[0:1, pl.ds(col, STRIP)], carry_ref[0:1, pl.ds(coli, STRIP)],
                                 ptab_ref, off, rowmod, True)
            bu_ref[:, pl.ds(col, STRIP)] = xr
            bu_ref[:, pl.ds(coli, STRIP)] = xi
            carry_ref[0:1, pl.ds(col, STRIP)] = xr[t - 1:t, :]
            carry_ref[0:1, pl.ds(coli, STRIP)] = xi[t - 1:t, :]
            return c

        lax.fori_loop(0, S5_N // STRIP, strip, 0)
        y = jnp.concatenate(
            [jnp.dot(bu_ref[:, SLAB_COLS * s:SLAB_COLS * (s + 1)].astype(BF16), cmat_ref[s], preferred_element_type=F32)
             for s in range(S5_SLABS)], axis=1) + d_ref[...] * u
        y_ref[...] = y
        g = _gelu(y)
        zz = jnp.dot(g.astype(BF16), wglu_ref[...], preferred_element_type=F32) + bglu_ref[...]
        out_ref[...] = (g * _sigmoid(zz)).astype(BF16)

    const = lambda shape: pl.BlockSpec(shape, lambda i: (0,) * len(shape))
    row_spec = pl.BlockSpec((t, MIX), lambda i: (i, 0))
    return pl.pallas_call(
        body, name=name, grid=(nb,),
        in_specs=[row_spec, const((S5_SLABS, 128, SLAB_COLS)), const((S5_SLABS, SLAB_COLS, 128)), const((1, MIX)),
                  const((MIX, MIX)), const((1, MIX)), const((32, S5_N))],
        out_specs=[row_spec, row_spec, pl.BlockSpec((1, 1, 2 * S5_N), lambda i: (i, 0, 0))],
        out_shape=[jax.ShapeDtypeStruct((m, MIX), BF16), jax.ShapeDtypeStruct((m, MIX), F32),
                   jax.ShapeDtypeStruct((nb, 1, 2 * S5_N), F32)],
        scratch_shapes=[pltpu.VMEM((t, 2 * S5_N), F32), pltpu.VMEM((1, 2 * S5_N), F32)],
        compiler_params=_cparams(("arbitrary",), VMEM_BIG),
    )(h, bmat, cmat, dvec, wglu, bglu, ptab)


def _s5_bwd(dcat, ypre, h, hb, bmat, cmat, dvec, wglu, bglu, ptab, *, name):
    m = h.shape[0]
    t = T_S5
    nb = m // t

    def body(dya_ref, y_ref, u_ref, hb_ref, bmat_ref, cmat_ref, d_ref, wglu_ref, bglu_ref, ptab_ref,
             du_ref, xb_ref, gb_ref, gq_ref, dzz_ref, dyq_ref, dlam_ref, dbglu_ref, dd_ref,
             bu_ref, dx_ref, gcarry_ref):
        @pl.when(pl.program_id(0) == 0)
        def _():
            gcarry_ref[...] = jnp.zeros_like(gcarry_ref)
            dlam_ref[...] = jnp.zeros_like(dlam_ref)
            dbglu_ref[...] = jnp.zeros_like(dbglu_ref)
            dd_ref[...] = jnp.zeros_like(dd_ref)

        u = u_ref[...]
        y = y_ref[...]
        g = _gelu(y)
        gq = g.astype(BF16)
        sg = _sigmoid(jnp.dot(gq, wglu_ref[...], preferred_element_type=F32) + bglu_ref[...])
        dout = dya_ref[...]
        dzz = dout * g * sg * (1.0 - sg)
        dzzq = dzz.astype(BF16)
        dg = dout * sg + lax.dot_general(dzzq, wglu_ref[...], (((1,), (1,)), ((), ())), preferred_element_type=F32)
        dy = dg * _gelu_grad(y)
        dyq = dy.astype(BF16)
        gq_ref[...] = gq
        dzz_ref[...] = dzzq
        dyq_ref[...] = dyq
        dbglu_ref[...] += jnp.sum(dzz, axis=0, keepdims=True)
        dd_ref[...] += jnp.sum(dy * u, axis=0, keepdims=True)

        ub = u.astype(BF16)
        nt = (((1,), (1,)), ((), ()))
        for s in range(S5_SLABS):
            cols = slice(SLAB_COLS * s, SLAB_COLS * (s + 1))
            dx_ref[:, cols] = lax.dot_general(dyq[:, 128 * s:128 * (s + 1)], cmat_ref[s], nt, preferred_element_type=F32)
            bu_ref[:, cols] = jnp.dot(ub[:, 128 * s:128 * (s + 1)], bmat_ref[s], preferred_element_type=F32)
        row = lax.broadcasted_iota(jnp.int32, (t, STRIP), 0)
        rowmod = row & 7

        def strip(j, c):
            off, col, coli = _strip_cols(j)
            hr = hb_ref[0, 0:1, pl.ds(col, STRIP)]
            hi = hb_ref[0, 0:1, pl.ds(coli, STRIP)]
            xr, xi = _scan_strip(bu_ref[:, pl.ds(col, STRIP)], bu_ref[:, pl.ds(coli, STRIP)], hr, hi,
                                 ptab_ref, off, rowmod, True)
            xb_ref[:, pl.ds(col, STRIP)] = xr.astype(BF16)
            xb_ref[:, pl.ds(coli, STRIP)] = xi.astype(BF16)
            pr_ = jnp.where(row == 0, hr, pltpu.roll(xr, 1, 0))
            pi_ = jnp.where(row == 0, hi, pltpu.roll(xi, 1, 0))

            gr, gi = _scan_strip(dx_ref[:, pl.ds(col, STRIP)], dx_ref[:, pl.ds(coli, STRIP)],
                                 gcarry_ref[0:1, pl.ds(col, STRIP)], gcarry_ref[0:1, pl.ds(coli, STRIP)],
                                 ptab_ref, off, rowmod, False)
            gb_ref[:, pl.ds(col, STRIP)] = gr.astype(BF16)
            gb_ref[:, pl.ds(coli, STRIP)] = gi.astype(BF16)
            gcarry_ref[0:1, pl.ds(col, STRIP)] = gr[0:1, :]
            gcarry_ref[0:1, pl.ds(coli, STRIP)] = gi[0:1, :]
            dlam_ref[0:1, pl.ds(off, STRIP)] += jnp.sum(pr_ * gr + pi_ * gi, axis=0, keepdims=True)
            dlam_ref[1:2, pl.ds(off, STRIP)] += jnp.sum(pr_ * gi - pi_ * gr, axis=0, keepdims=True)
            return c

        lax.fori_loop(0, S5_N // STRIP, strip, 0)
        du_ref[...] = dy * d_ref[...] + jnp.concatenate(
            [lax.dot_general(gb_ref[:, SLAB_COLS * s:SLAB_COLS * (s + 1)], bmat_ref[s], nt, preferred_element_type=F32)
             for s in range(S5_SLABS)], axis=1)

    const = lambda shape: pl.BlockSpec(shape, lambda i: (0,) * len(shape))
    rev = lambda i: (nb - 1 - i, 0)
    row_spec = pl.BlockSpec((t, MIX), rev)
    wide = pl.BlockSpec((t, 2 * S5_N), rev)
    return pl.pallas_call(
        body, name=name, grid=(nb,),
        in_specs=[row_spec, row_spec, row_spec, pl.BlockSpec((1, 1, 2 * S5_N), lambda i: (nb - 1 - i, 0, 0)),
                  const((S5_SLABS, 128, SLAB_COLS)), const((S5_SLABS, SLAB_COLS, 128)), const((1, MIX)), const((MIX, MIX)),
                  const((1, MIX)), const((32, S5_N))],
        out_specs=[row_spec, wide, wide, row_spec, row_spec, row_spec, const((2, S5_N)), const((1, MIX)), const((1, MIX))],
        out_shape=[jax.ShapeDtypeStruct((m, MIX), F32), jax.ShapeDtypeStruct((m, 2 * S5_N), BF16),
                   jax.ShapeDtypeStruct((m, 2 * S5_N), BF16), jax.ShapeDtypeStruct((m, MIX), BF16),
                   jax.ShapeDtypeStruct((m, MIX), BF16), jax.ShapeDtypeStruct((m, MIX), BF16),
                   jax.ShapeDtypeStruct((2, S5_N), F32), jax.ShapeDtypeStruct((1, MIX), F32), jax.ShapeDtypeStruct((1, MIX), F32)],
        scratch_shapes=[pltpu.VMEM((t, 2 * S5_N), F32), pltpu.VMEM((t, 2 * S5_N), F32), pltpu.VMEM((1, 2 * S5_N), F32)],
        compiler_params=_cparams(("arbitrary",), VMEM_BIG),
    )(dcat, ypre, h, hb, bmat, cmat, dvec, wglu, bglu, ptab)


HALO = 8


def _taps_down(zext, t):
    return pltpu.roll(zext, 1, 0)[HALO:HALO + t], pltpu.roll(zext, 2, 0)[HALO:HALO + t]


def _conv_z(c_ref, x_ref, cp_ref, xp_ref, first, t):
    z = c_ref[...] * x_ref[...]
    zp = jnp.where(first, 0.0, cp_ref[t - HALO:t, :] * xp_ref[t - HALO:t, :])
    z1, z2 = _taps_down(jnp.concatenate([zp, z], axis=0), t)
    return z, z1, z2


def _conv_fwd(h, cw, *, name):
    m = h.shape[0]
    t = TM
    nb = m // t

    def body(b_ref, c_ref, x_ref, cp_ref, xp_ref, w_ref, o_ref):
        z, z1, z2 = _conv_z(c_ref, x_ref, cp_ref, xp_ref, pl.program_id(0) == 0, t)
        o_ref[...] = (b_ref[...] * (w_ref[0:1, :] * z2 + w_ref[1:2, :] * z1 + w_ref[2:3, :] * z)).astype(BF16)

    cur = lambda cb: pl.BlockSpec((t, MIX), lambda i: (i, cb))
    prev = lambda cb: pl.BlockSpec((t, MIX), lambda i: (jnp.maximum(i - 1, 0), cb))
    return pl.pallas_call(
        body, name=name, grid=(nb,),
        in_specs=[cur(1), cur(2), cur(3), prev(2), prev(3), pl.BlockSpec((3, MIX), lambda i: (0, 0))],
        out_specs=pl.BlockSpec((t, MIX), lambda i: (i, 0)),
        out_shape=jax.ShapeDtypeStruct((m, MIX), BF16),
        compiler_params=_cparams(("parallel",)),
    )(h, h, h, h, h, cw)


def _conv_bwd(dcat, h, cw, *, name):
    m = h.shape[0]
    t = TM
    nb = m // t

    def body(dy_ref, dyn_ref, b_ref, c_ref, x_ref, cp_ref, xp_ref, bn_ref, w_ref, o_ref, dw_ref):
        i = pl.program_id(0)

        @pl.when(i == 0)
        def _():
            dw_ref[...] = jnp.zeros_like(dw_ref)

        z, z1, z2 = _conv_z(c_ref, x_ref, cp_ref, xp_ref, i == 0, t)
        w0, w1, w2 = w_ref[0:1, :], w_ref[1:2, :], w_ref[2:3, :]
        dy = dy_ref[...]
        dconv = dy * b_ref[...]
        dnext = jnp.where(i == nb - 1, 0.0, dyn_ref[0:HALO, :] * bn_ref[0:HALO, :])
        dext = jnp.concatenate([dconv, dnext], axis=0)
        d1 = pltpu.roll(dext, t + HALO - 1, 0)[0:t]
        d2 = pltpu.roll(dext, t + HALO - 2, 0)[0:t]
        dz = w2 * dconv + w1 * d1 + w0 * d2
        o_ref[:, 0:MIX] = dy * (w0 * z2 + w1 * z1 + w2 * z)
        o_ref[:, MIX:2 * MIX] = dz * x_ref[...]
        o_ref[:, 2 * MIX:3 * MIX] = dz * c_ref[...]
        dw_ref[0:1, :] += jnp.sum(dconv * z2, axis=0, keepdims=True)
        dw_ref[1:2, :] += jnp.sum(dconv * z1, axis=0, keepdims=True)
        dw_ref[2:3, :] += jnp.sum(dconv * z, axis=0, keepdims=True)

    cur = lambda cb: pl.BlockSpec((t, MIX), lambda i: (i, cb))
    prev = lambda cb: pl.BlockSpec((t, MIX), lambda i: (jnp.maximum(i - 1, 0), cb))
    nxt = lambda cb: pl.BlockSpec((t, MIX), lambda i: (jnp.minimum(i + 1, nb - 1), cb))
    return pl.pallas_call(
        body, name=name, grid=(nb,),
        in_specs=[cur(1), nxt(1), cur(1), cur(2), cur(3), prev(2), prev(3), nxt(1), pl.BlockSpec((3, MIX), lambda i: (0, 0))],
        out_specs=[pl.BlockSpec((t, 3 * MIX), lambda i: (i, 0)), pl.BlockSpec((8, MIX), lambda i: (0, 0))],
        out_shape=[jax.ShapeDtypeStruct((m, 3 * MIX), F32), jax.ShapeDtypeStruct((8, MIX), F32)],
        compiler_params=_cparams(("arbitrary",)),
    )(dcat, dcat, h, h, h, h, h, h, cw)


PHALO = 16


def _pool_pooled(z_ref, zp_ref, i, t):
    z = z_ref[...]
    zp = jnp.where(i == 0, 0.0, zp_ref[t - PHALO:t, :])
    s = jnp.concatenate([zp, z], axis=0)
    sums = {}
    width = 1
    while width < PHALO:
        s = s + pltpu.roll(s, width, 0)
        width *= 2
        sums[width] = s[PHALO:PHALO + t]
    tpos = i * t + lax.broadcasted_iota(jnp.int32, (t, 1), 0)
    outs = []
    for gi, w in enumerate(POOL_WINDOWS):
        lo = gi * POOL_GROUP
        count = jnp.minimum(tpos + 1, w).astype(F32)
        outs.append(sums[w][:, lo:lo + POOL_GROUP] / count - z[:, lo:lo + POOL_GROUP])
    return outs


def _pool_fwd(h, pw, ps, *, name):
    m = h.shape[0]
    t = TM
    nb = m // t

    def body(z_ref, zp_ref, pw_ref, ps_ref, o_ref):
        pooled = _pool_pooled(z_ref, zp_ref, pl.program_id(0), t)
        for gi in range(len(POOL_WINDOWS)):
            lo = gi * POOL_GROUP
            mixed = jnp.dot(pooled[gi].astype(BF16), pw_ref[gi], preferred_element_type=F32)
            o_ref[:, lo:lo + POOL_GROUP] = (mixed * ps_ref[:, lo:lo + POOL_GROUP]).astype(BF16)

    return pl.pallas_call(
        body, name=name, grid=(nb,),
        in_specs=[pl.BlockSpec((t, MIX), lambda i: (i, 3)), pl.BlockSpec((t, MIX), lambda i: (jnp.maximum(i - 1, 0), 3)),
                  pl.BlockSpec((4, POOL_GROUP, POOL_GROUP), lambda i: (0, 0, 0)), pl.BlockSpec((1, MIX), lambda i: (0, 0))],
        out_specs=pl.BlockSpec((t, MIX), lambda i: (i, 0)),
        out_shape=jax.ShapeDtypeStruct((m, MIX), BF16),
        compiler_params=_cparams(("parallel",)),
    )(h, h, pw, ps)


def _pool_bwd(dcat, h, pw, ps, *, name):
    m = h.shape[0]
    t = TM
    nb = m // t

    def body(dy_ref, dyn_ref, z_ref, zp_ref, pw_ref, ps_ref, dz_ref, dpw_ref, dps_ref):
        i = pl.program_id(0)

        @pl.when(i == 0)
        def _():
            dpw_ref[...] = jnp.zeros_like(dpw_ref)
            dps_ref[...] = jnp.zeros_like(dps_ref)

        pooled = _pool_pooled(z_ref, zp_ref, i, t)
        dy = dy_ref[...]
        tpos = i * t + lax.broadcasted_iota(jnp.int32, (t, 1), 0)
        for gi, w in enumerate(POOL_WINDOWS):
            lo = gi * POOL_GROUP
            sl = slice(lo, lo + POOL_GROUP)
            pq = pooled[gi].astype(BF16)
            mixed = jnp.dot(pq, pw_ref[gi], preferred_element_type=F32)
            dps_ref[:, sl] += jnp.sum(dy[:, sl] * mixed, axis=0, keepdims=True)
            dmix = (dy[:, sl] * ps_ref[:, sl]).astype(BF16)
            dpw_ref[gi] += lax.dot_general(pq, dmix, (((0,), (0,)), ((), ())), preferred_element_type=F32)
            dpool = lax.dot_general(dmix, pw_ref[gi], (((1,), (1,)), ((), ())), preferred_element_type=F32)
            dmix_n = (dyn_ref[0:PHALO, sl] * ps_ref[:, sl]).astype(BF16)
            dpool_n = lax.dot_general(dmix_n, pw_ref[gi], (((1,), (1,)), ((), ())), preferred_element_type=F32)
            e = dpool / jnp.minimum(tpos + 1, w).astype(F32)
            e_n = jnp.where(i == nb - 1, 0.0, dpool_n * (1.0 / w))
            f = jnp.concatenate([e, e_n], axis=0)
            width = 1
            while width < w:
                f = f + pltpu.roll(f, t + PHALO - width, 0)
                width *= 2
            dz_ref[:, sl] = f[0:t] - dpool

    return pl.pallas_call(
        body, name=name, grid=(nb,),
        in_specs=[pl.BlockSpec((t, MIX), lambda i: (i, 1)), pl.BlockSpec((t, MIX), lambda i: (jnp.minimum(i + 1, nb - 1), 1)),
                  pl.BlockSpec((t, MIX), lambda i: (i, 3)), pl.BlockSpec((t, MIX), lambda i: (jnp.maximum(i - 1, 0), 3)),
                  pl.BlockSpec((4, POOL_GROUP, POOL_GROUP), lambda i: (0, 0, 0)), pl.BlockSpec((1, MIX), lambda i: (0, 0))],
        out_specs=[pl.BlockSpec((t, MIX), lambda i: (i, 0)), pl.BlockSpec((4, POOL_GROUP, POOL_GROUP), lambda i: (0, 0, 0)),
                   pl.BlockSpec((1, MIX), lambda i: (0, 0))],
        out_shape=[jax.ShapeDtypeStruct((m, MIX), F32), jax.ShapeDtypeStruct((4, POOL_GROUP, POOL_GROUP), F32),
                   jax.ShapeDtypeStruct((1, MIX), F32)],
        compiler_params=_cparams(("arbitrary",)),
    )(dcat, dcat, h, h, pw, ps)


NKEY = 2 * T_ATT


def _band_mask():
    qc = np.arange(T_ATT)[:, None] // CHUNK
    kc = np.arange(NKEY)[None, :] // CHUNK - LEFT_CHUNKS
    return np.where((kc <= qc) & (kc >= qc - LEFT_CHUNKS), 0.0, NEG_INF).astype(np.float32)


def _diag_index():
    c = np.arange(NKEY)
    d = np.where(c <= NKEY // 2 + CHUNK, T_ATT - c, T_ATT + NKEY - c)
    return np.clip(d, -MAX_REL, MAX_REL) + MAX_REL


def _bias_tile(vd_ref, mask_ref, tile_ref):
    col = lax.broadcasted_iota(jnp.int32, (8, NKEY), 1)
    no_prev = jnp.where(col < T_ATT, NEG_INF, 0.0)
    for hh in range(2):
        v = vd_ref[0, hh:hh + 1, :]
        base = jnp.concatenate([v if s == 0 else pltpu.roll(v, s, 1) for s in range(8)], axis=0)
        for mrow in range(T_ATT // 8):
            rows = slice(8 * mrow, 8 * mrow + 8)
            blk = (base if mrow == 0 else pltpu.roll(base, 8 * mrow, 1)) + mask_ref[rows, :]
            tile_ref[hh, rows, :] = blk
            tile_ref[2 + hh, rows, :] = blk + no_prev


BAND_ROWS = 2 * CHUNK
BAND_COLS = (LEFT_CHUNKS + 2) * CHUNK
N_BANDS = T_ATT // BAND_ROWS


def _band(x, r):
    return x[BAND_ROWS * r:BAND_ROWS * (r + 1), BAND_ROWS * r:BAND_ROWS * r + BAND_COLS]


def _from_bands(parts):
    rows = []
    for r, part in enumerate(parts):
        right = NKEY - BAND_COLS - BAND_ROWS * r
        pieces = ([jnp.zeros((BAND_ROWS, BAND_ROWS * r), part.dtype)] if r else []) + [part]
        pieces += [jnp.zeros((BAND_ROWS, right), part.dtype)] if right else []
        rows.append(jnp.concatenate(pieces, axis=1))
    return jnp.concatenate(rows, axis=0)


def _attn_probs(q, kc, tile_ref, idx):
    s = lax.dot_general(q, kc, (((1,), (1,)), ((), ())), preferred_element_type=F32)
    parts = []
    for r in range(N_BANDS):
        sb = _band(s, r) + tile_ref[idx, BAND_ROWS * r:BAND_ROWS * (r + 1), BAND_ROWS * r:BAND_ROWS * r + BAND_COLS]
        p = jnp.exp(sb - jnp.max(sb, axis=-1, keepdims=True))
        parts.append(p * (1.0 / jnp.sum(p, axis=-1, keepdims=True)))
    return parts


def _attn_specs(block):
    cur = lambda base: pl.BlockSpec((T_ATT, 128), lambda hp, i: (block(i), base + hp))
    prev = lambda base: pl.BlockSpec((T_ATT, 128), lambda hp, i: (jnp.maximum(block(i) - 1, 0), base + hp))
    return [cur(0), cur(4), prev(4), cur(8), prev(8),
            pl.BlockSpec((1, 2, NKEY), lambda hp, i: (hp, 0, 0)), pl.BlockSpec((T_ATT, NKEY), lambda hp, i: (0, 0))]


def _attn_fwd(h, vdiag, mask, *, name):
    m = h.shape[0]
    nb = m // T_ATT

    def body(q_ref, k_ref, kp_ref, v_ref, vp_ref, vd_ref, mask_ref, o_ref, tile_ref):
        i = pl.program_id(1)

        @pl.when(i == 0)
        def _():
            _bias_tile(vd_ref, mask_ref, tile_ref)

        first = jnp.where(i == 0, 2, 0)
        outs = []
        for hh in range(2):
            sl = slice(hh * HEAD_DIM, (hh + 1) * HEAD_DIM)
            q = (q_ref[:, sl] * (HEAD_DIM ** -0.5)).astype(BF16)
            kc = jnp.concatenate([kp_ref[:, sl], k_ref[:, sl]], axis=0).astype(BF16)
            vc = jnp.concatenate([vp_ref[:, sl], v_ref[:, sl]], axis=0).astype(BF16)
            p = _from_bands([b.astype(BF16) for b in _attn_probs(q, kc, tile_ref, first + hh)])
            outs.append(jnp.dot(p, vc, preferred_element_type=F32))
        o_ref[...] = jnp.concatenate(outs, axis=1).astype(BF16)

    return pl.pallas_call(
        body, name=name, grid=(ATT_HEADS // 2, nb), in_specs=_attn_specs(lambda i: i),
        out_specs=pl.BlockSpec((T_ATT, 128), lambda hp, i: (i, hp)),
        out_shape=jax.ShapeDtypeStruct((m, MIX), BF16),
        scratch_shapes=[pltpu.VMEM((4, T_ATT, NKEY), F32)],
        compiler_params=_cparams(("parallel", "arbitrary"), VMEM_BIG),
    )(h, h, h, h, h, vdiag, mask)


def _attn_bwd(dcat, h, vdiag, mask, *, name):
    m = h.shape[0]
    nb = m // T_ATT

    def body(do_ref, q_ref, k_ref, kp_ref, v_ref, vp_ref, vd_ref, mask_ref,
             dq_ref, dk_ref, dv_ref, dvd_ref, tile_ref, acc_ref, carry_ref):
        i = pl.program_id(1)

        @pl.when(i == 0)
        def _():
            _bias_tile(vd_ref, mask_ref, tile_ref)
            acc_ref[...] = jnp.zeros_like(acc_ref)

            carry_ref[...] = jnp.zeros_like(carry_ref)

        scale = HEAD_DIM ** -0.5
        first = jnp.where(i == nb - 1, 2, 0)
        dqs, dks, dvs = [], [], []
        for hh in range(2):
            sl = slice(hh * HEAD_DIM, (hh + 1) * HEAD_DIM)
            q = (q_ref[:, sl] * scale).astype(BF16)
            kc = jnp.concatenate([kp_ref[:, sl], k_ref[:, sl]], axis=0).astype(BF16)
            vc = jnp.concatenate([vp_ref[:, sl], v_ref[:, sl]], axis=0).astype(BF16)
            do = do_ref[:, sl].astype(BF16)
            bands = _attn_probs(q, kc, tile_ref, first + hh)
            p = _from_bands([b.astype(BF16) for b in bands])
            dvs.append(lax.dot_general(p, do, (((0,), (0,)), ((), ())), preferred_element_type=F32))
            dp = lax.dot_general(do, vc, (((1,), (1,)), ((), ())), preferred_element_type=F32)
            ds_bands = []
            for r, pb in enumerate(bands):
                dpb = _band(dp, r)
                dsb = pb * (dpb - jnp.sum(dpb * pb, axis=-1, keepdims=True))
                acc_ref[hh, BAND_ROWS * r:BAND_ROWS * (r + 1), BAND_ROWS * r:BAND_ROWS * r + BAND_COLS] += dsb
                ds_bands.append(dsb.astype(BF16))
            dsq = _from_bands(ds_bands)
            dqs.append(jnp.dot(dsq, kc, preferred_element_type=F32) * scale)
            dks.append(lax.dot_general(dsq, q, (((0,), (0,)), ((), ())), preferred_element_type=F32))
        dq_ref[...] = jnp.concatenate(dqs, axis=1)
        dk = jnp.concatenate(dks, axis=1)
        dv = jnp.concatenate(dvs, axis=1)
        dk_ref[...] = dk[T_ATT:] + carry_ref[0]
        dv_ref[...] = dv[T_ATT:] + carry_ref[1]
        carry_ref[0] = dk[:T_ATT]
        carry_ref[1] = dv[:T_ATT]

        @pl.when(i == nb - 1)
        def _():
            for hh in range(2):
                r8 = acc_ref[hh, 0:8, :]
                for mrow in range(1, T_ATT // 8):
                    r8 = r8 + pltpu.roll(acc_ref[hh, 8 * mrow:8 * mrow + 8, :], NKEY - 8 * mrow, 1)
                tot = r8[0:1, :]
                for s in range(1, 8):
                    tot = tot + pltpu.roll(r8[s:s + 1, :], NKEY - s, 1)
                dvd_ref[0, hh:hh + 1, :] = tot

    block = lambda i: nb - 1 - i
    out = pl.BlockSpec((T_ATT, 128), lambda hp, i: (block(i), hp))
    return pl.pallas_call(
        body, name=name, grid=(ATT_HEADS // 2, nb),
        in_specs=[out] + _attn_specs(block),
        out_specs=[out, out, out, pl.BlockSpec((1, 2, NKEY), lambda hp, i: (hp, 0, 0))],
        out_shape=[jax.ShapeDtypeStruct((m, MIX), F32)] * 3 + [jax.ShapeDtypeStruct((ATT_HEADS // 2, 2, NKEY), F32)],
        scratch_shapes=[pltpu.VMEM((4, T_ATT, NKEY), F32), pltpu.VMEM((2, T_ATT, NKEY), F32), pltpu.VMEM((2, T_ATT, 128), F32)],
        compiler_params=_cparams(("parallel", "arbitrary"), VMEM_BIG),
    )(dcat, h, h, h, h, h, vdiag, mask)


def _row_tile(rows):
    for t in (512, 256, 128, 64, 32, 16, 8):
        if rows % t == 0:
            return t
    return rows


def _add_n(arrs, coefs, *, name):
    rows, cols = arrs[0].shape
    t = _row_tile(rows)
    n = len(arrs)

    def body(*refs):
        acc = None
        for r, cf in zip(refs[:n], coefs):
            v = r[...] if cf == 1.0 else cf * r[...]
            acc = v if acc is None else acc + v
        refs[n][...] = acc

    spec = pl.BlockSpec((t, cols), lambda i: (i, 0))
    return pl.pallas_call(
        body, name=name, grid=(rows // t,), in_specs=[spec] * n, out_specs=spec,
        out_shape=jax.ShapeDtypeStruct((rows, cols), F32), compiler_params=_cparams(("parallel",)),
    )(*arrs)


def _adamw(w, g, mom, var, *, name):
    rows, cols = w.shape
    t = _row_tile(rows)

    def body(w_ref, g_ref, m_ref, v_ref, d_ref, mo_ref, vo_ref):
        g_ = g_ref[...]
        m_ = ADAM_B1 * m_ref[...] + (1.0 - ADAM_B1) * g_
        v_ = ADAM_B2 * v_ref[...] + (1.0 - ADAM_B2) * (g_ * g_)
        m_hat = m_ / (1.0 - ADAM_B1 ** ADAM_STEP)
        v_hat = v_ / (1.0 - ADAM_B2 ** ADAM_STEP)
        d_ref[...] = -ADAM_LR * (m_hat / (jnp.sqrt(v_hat) + ADAM_EPS) + ADAM_WD * w_ref[...])
        mo_ref[...] = m_
        vo_ref[...] = v_

    spec = pl.BlockSpec((t, cols), lambda i: (i, 0))
    return pl.pallas_call(
        body, name=name, grid=(rows // t,), in_specs=[spec] * 4, out_specs=[spec] * 3,
        out_shape=[jax.ShapeDtypeStruct((rows, cols), F32)] * 3, compiler_params=_cparams(("parallel",)),
    )(w, g, mom, var)


ANY = pl.BlockSpec(memory_space=pl.ANY)


def _place():
    x, y, c = lax.axis_index("x"), lax.axis_index("y"), lax.axis_index("c")
    chips = [(1 - x, y), (x, 1 - y), (1 - x, 1 - y)]
    return x, y, c, chips


class _GatherExchange:
    def __init__(self, ws):
        n = len(ws)
        self.ins = list(ws)
        self.out_shapes = [jax.ShapeDtypeStruct((N_CHIPS,) + w.shape, w.dtype) for w in ws]
        self.sems = [pltpu.SemaphoreType.DMA((6 * n,)), pltpu.SemaphoreType.DMA((6 * n,))]

    def _copies(self, ins, outs, sems, onward=True):
        send_sems, recv_sems = sems
        x, y, c, chips = _place()
        me = 2 * x + y

        def region(k, j, chip_index, rows, to):
            ref = outs[k].at[chip_index, rows]
            return pltpu.make_async_remote_copy(
                src_ref=ref, dst_ref=ref, send_sem=send_sems.at[6 * k + j], recv_sem=recv_sems.at[6 * k + j],
                device_id=to, device_id_type=MESH)

        first, landed, passed, handed = [], [], [], []
        for k in range(len(ins)):
            half = ins[k].shape[0] // 2
            mine, theirs = pl.ds(c * half, half), pl.ds((1 - c) * half, half)
            for j, chip in enumerate(chips):
                first.append(pltpu.make_async_remote_copy(
                    src_ref=ins[k].at[mine], dst_ref=outs[k].at[me, mine], send_sem=send_sems.at[6 * k + j],
                    recv_sem=recv_sems.at[6 * k + j], device_id=(*chip, c), device_id_type=MESH))
                if onward:
                    landed.append(region(k, j, 2 * chip[0] + chip[1], mine, (*chip, c)))
                    passed.append(region(k, 3 + j, 2 * chip[0] + chip[1], mine, (x, y, 1 - c)))
                    handed.append(region(k, 3 + j, 2 * chip[0] + chip[1], theirs, (x, y, 1 - c)))
        return first, landed, passed, handed

    def start(self, ins, outs, sems):
        for cp in self._copies(ins, outs, sems, onward=False)[0]:
            cp.start()

    def finish(self, ins, outs, sems):
        first, landed, passed, handed = self._copies(ins, outs, sems)
        for arrived, onward in zip(landed, passed):
            arrived.wait_recv()
            onward.start()
        for cp in handed:
            cp.wait_recv()
        for cp in first + passed:
            cp.wait_send()


class _ReduceExchange:
    def __init__(self, grads, axes):
        self.ins = list(grads)
        self.axes = list(axes)
        n = len(grads)
        self.out_shapes = [jax.ShapeDtypeStruct((N_DEV - 1,) + self._block(g, a), g.dtype) for g, a in zip(grads, axes)]
        self.sems = [pltpu.SemaphoreType.DMA((7 * n,)), pltpu.SemaphoreType.DMA((7 * n,))]

    @staticmethod
    def _block(g, axis):
        k, n = g.shape
        return (k // 2, n // N_CHIPS) if axis == 2 else (k // N_DEV, n)

    def _copies(self, ins, outs, sems):
        send_sems, recv_sems = sems
        x, y, c, _ = _place()
        cps = []
        for w, (g, axis) in enumerate(zip(ins, self.axes)):
            rows, cols = self._block(g, axis)
            for k in range(1, N_DEV):
                tx, ty, tc = (1 - x if k & 4 else x), (1 - y if k & 2 else y), (1 - c if k & 1 else c)
                chip = 2 * tx + ty
                if axis == 2:
                    src = g.at[pl.ds(tc * rows, rows), pl.ds(chip * cols, cols)]
                else:
                    src = g.at[pl.ds((2 * chip + tc) * rows, rows), :]
                cps.append(pltpu.make_async_remote_copy(
                    src_ref=src, dst_ref=outs[w].at[k - 1], send_sem=send_sems.at[7 * w + k - 1],
                    recv_sem=recv_sems.at[7 * w + k - 1], device_id=(tx, ty, tc), device_id_type=MESH))
        return cps

    def start(self, ins, outs, sems):
        for cp in self._copies(ins, outs, sems):
            cp.start()

    def finish(self, ins, outs, sems):
        cps = self._copies(ins, outs, sems)
        for cp in cps:
            cp.wait_recv()
        for cp in cps:
            cp.wait_send()


def _run_exchange(ex, *, name):
    n_in, n_out = len(ex.ins), len(ex.out_shapes)

    def body(*refs):
        ins, outs, sems = refs[:n_in], refs[n_in:n_in + n_out], refs[n_in + n_out:]
        ex.start(ins, outs, sems)
        ex.finish(ins, outs, sems)

    return pl.pallas_call(body, name=name, in_specs=[ANY] * n_in, out_specs=[ANY] * n_out, out_shape=ex.out_shapes,
                          scratch_shapes=ex.sems)(*ex.ins)


def _all_reduce_small(buf, *, name):
    rows = buf.shape[0]

    def body(x_ref, sum_ref, all_ref, send_sems, recv_sems, local_sem):
        x, y, c, chips = _place()
        me, sibling = (x, y, c), (x, y, 1 - c)

        def slab(px, py, pc):
            return all_ref.at[pl.ds((4 * px + 2 * py + pc) * rows, rows), :]

        def copy(k, block, to, src=None):
            return pltpu.make_async_remote_copy(
                src_ref=slab(*block) if src is None else src, dst_ref=slab(*block), send_sem=send_sems.at[k],
                recv_sem=recv_sems.at[k], device_id=to, device_id_type=MESH)

        mine = pltpu.make_async_copy(x_ref, slab(*me), local_sem)
        mine.start()
        first = [copy(0, me, sibling, src=x_ref)]
        first += [copy(1 + j, me, (*chip, c), src=x_ref) for j, chip in enumerate(chips)]
        for cp in first:
            cp.start()
        passed = [copy(4 + j, (*chip, c), sibling) for j, chip in enumerate(chips)]
        for j, chip in enumerate(chips):
            copy(1 + j, (*chip, c), me).wait_recv()
            passed[j].start()
        copy(0, sibling, me).wait_recv()
        for j, chip in enumerate(chips):
            copy(4 + j, (*chip, 1 - c), me).wait_recv()
        for cp in first + passed:
            cp.wait_send()
        mine.wait()
        acc = all_ref[0:rows, :]
        for d in range(1, N_DEV):
            acc = acc + all_ref[d * rows:(d + 1) * rows, :]
        sum_ref[...] = acc

    vmem = pl.BlockSpec(memory_space=pltpu.VMEM)
    return pl.pallas_call(
        body, name=name, in_specs=[vmem], out_specs=[vmem, vmem],
        out_shape=[jax.ShapeDtypeStruct((rows, 128), F32), jax.ShapeDtypeStruct((N_DEV * rows, 128), F32)],
        scratch_shapes=[pltpu.SemaphoreType.DMA((7,)), pltpu.SemaphoreType.DMA((7,)), pltpu.SemaphoreType.DMA],
        compiler_params=pltpu.CompilerParams(vmem_limit_bytes=VMEM_BIG),
    )(buf)[0]


WEIGHTS = ['ev_w_in', 'ev_lambda_re', 'ev_lambda_im', 'ev_log_dt', 'ev_b_re', 'ev_b_im', 'ev_c_re', 'ev_c_im', 'ev_d',
           'ev_w_glu', 'ev_b_glu', 'ev_conv_w', 'ev_w_out', 'od_w_in', 'od_rel_bias', 'od_pool_w', 'od_pool_scale',
           'od_w_out', 'ln_mix_g', 'ln_mix_b', 'ln_ffn_g', 'ln_ffn_b', 'ffn_w_up', 'ffn_w_down', 'ple_w_proj',
           'ple_w_gate', 'ple_b_gate']
INPUTS = ['x', 'p'] + WEIGHTS + ['loss_target'] + ['m_' + n for n in WEIGHTS] + ['v_' + n for n in WEIGHTS]

BIG = {
    'ev_w_in': (2, (2, 1024, 2048)), 'ev_w_glu': (1, (2, 512, 512)), 'ev_w_out': (1, (2, 1024, 1024)),
    'od_w_in': (2, (2, 1024, 2048)), 'od_w_out': (1, (2, 1024, 1024)), 'ffn_w_up': (2, (4, 1024, 5632)),
    'ffn_w_down': (1, (4, 2816, 1024)), 'ple_w_proj': (2, (4, 256, 1024)), 'ple_w_gate': (1, (4, 1024, 1024)),
}
SMALL_SHARDED = {'ev_conv_w': (2, 3, 512), 'od_pool_scale': (2, 512)}
REPLICATED = [n for n in WEIGHTS if n not in BIG and n not in SMALL_SHARDED]


def _shard_rows(name):
    axis, (nl, k, n) = BIG[name]
    return (nl * k, n // N_CHIPS) if axis == 2 else (nl * k // N_CHIPS, n)


def _pack(arrs):
    flat = jnp.concatenate([a.reshape(-1) for a in arrs])
    total = flat.shape[0]
    padded = -(-total // 1024) * 1024
    return jnp.pad(flat, (0, padded - total)).reshape(padded // 128, 128)


def _unpack(buf, shapes):
    flat = buf.reshape(-1)
    out, pos = [], 0
    for s in shapes:
        size = int(np.prod(s))
        out.append(flat[pos:pos + size].reshape(s))
        pos += size
    return out


def _s5_params(lam_re, lam_im, log_dt, b_re, b_im, c_re, c_im):
    dt = jnp.exp(log_dt)[:, None]
    mag = jnp.exp(lam_re * dt)
    ang = lam_im * dt
    lb_re = mag * jnp.cos(ang)
    lb_im = mag * jnp.sin(ang)
    den = lam_re * lam_re + lam_im * lam_im
    nr = lb_re - 1.0
    ni = lb_im
    r_re = (nr * lam_re + ni * lam_im) / den
    r_im = (ni * lam_re - nr * lam_im) / den
    bb_re = r_re[..., None] * b_re - r_im[..., None] * b_im
    bb_im = r_re[..., None] * b_im + r_im[..., None] * b_re
    per = S5_GROUPS // S5_SLABS
    eye = jnp.eye(per, dtype=F32)

    def block_diag(a):
        _, r, c = a.shape
        a = a.reshape(S5_SLABS, per, r, c)
        return (a[:, :, :, None, :] * eye[None, :, None, :, None]).reshape(S5_SLABS, per * r, per * c)

    bmat = jnp.concatenate([block_diag(bb_re.transpose(0, 2, 1)), block_diag(bb_im.transpose(0, 2, 1))], axis=2)
    cmat = jnp.concatenate([block_diag(c_re.transpose(0, 2, 1)), block_diag(-c_im.transpose(0, 2, 1))], axis=1)
    lam = jnp.stack([lb_re.reshape(S5_N), lb_im.reshape(S5_N)])
    return lam, bmat, cmat


def _lam_powers(lam):
    res, ims = [lam[0]], [lam[1]]
    for _ in range(7):
        res, ims = res + [res[-1] * lam[0] - ims[-1] * lam[1]], ims + [res[-1] * lam[1] + ims[-1] * lam[0]]
    return jnp.stack(res + ims + res[::-1] + ims[::-1])


def _layer_big(i):
    mixer = [('w_in', 'ev_w_in'), ('w_glu', 'ev_w_glu'), ('w_out', 'ev_w_out')] if i % 2 == 0 else \
        [('w_in', 'od_w_in'), ('w_out', 'od_w_out')]
    ffn = [('w_up', 'ffn_w_up'), ('w_down', 'ffn_w_down'), ('w_proj', 'ple_w_proj'), ('w_gate', 'ple_w_gate')]
    return [(k, n, i // 2) for k, n in mixer] + [(k, n, i) for k, n in ffn]


class _WholePlan:
    def __init__(self, whole):
        self.whole = whole
        self.grads = {n: {} for n in BIG}

    def layer_weights(self, i):
        return {k: self.whole[n][l] for k, n, l in _layer_big(i)}

    def forward_host(self, i):
        return None

    def backward_host(self, i):
        return None

    def layer_grads(self, i, g):
        for k, n, l in _layer_big(i):
            self.grads[n][l] = g[k][0]


def _local_step(x, p, target, w, plan):
    mask = jnp.asarray(_band_mask())
    diag_idx = _diag_index()
    onehot = jnp.asarray(np.eye(2 * MAX_REL + 1, dtype=np.float32)[diag_idx])
    saved = []
    for i in range(DEPTH):
        li = i // 2
        lw = plan.layer_weights(i)
        s = {'x0': x, 'lw': lw}
        h = _mm([(x, 0, D_MODEL)], lw['w_in'], name=f"in_proj")
        if i % 2 == 0:
            (lam, bmat, cmat), s5_vjp = jax.vjp(
                _s5_params, w['ev_lambda_re'][li], w['ev_lambda_im'][li], w['ev_log_dt'][li], w['ev_b_re'][li],
                w['ev_b_im'][li], w['ev_c_re'][li], w['ev_c_im'][li])
            s5c = (bmat.astype(BF16), cmat.astype(BF16), w['ev_d'][li].reshape(1, MIX), lw['w_glu'],
                   w['ev_b_glu'][li].reshape(1, MIX), _lam_powers(lam))
            ya, ypre, hb = _s5_fwd(h, *s5c, name=f"s5_fwd")
            yb = _conv_fwd(h, w['ev_conv_w'][li], name=f"conv_fwd")
            s.update(s5_vjp=s5_vjp, s5c=s5c, ypre=ypre, hb=hb)
        else:
            vdiag = jnp.dot(w['od_rel_bias'][li], onehot.T, precision=HIGHEST).reshape(ATT_HEADS // 2, 2, NKEY)
            pw = w['od_pool_w'][li].astype(BF16)
            ps = w['od_pool_scale'][li].reshape(1, MIX)
            ya = _attn_fwd(h, vdiag, mask, name=f"attn_fwd")
            yb = _pool_fwd(h, pw, ps, name=f"pool_fwd")
            s.update(vdiag=vdiag, pw=pw, ps=ps)
        wout = lw['w_out']
        vec = lambda n: w[n][i].reshape(1, -1)

        def residual_ln(products, rows, vecs):
            r = ALPHA * rows[0] + products[0]
            return (r, _ln_apply(r, vecs[0], vecs[1])), ()

        def embed_gate(products, rows, vecs):
            gate = _sigmoid(products[0] + vecs[0])
            return (rows[0] + gate * products[1], gate, products[1]), ()

        two_f32 = [(D_MODEL, F32), (D_MODEL, F32)]
        r1, x1 = _mm_rows([([(ya, 0, MIX), (yb, 0, MIX)], wout, False)], [x], [vec('ln_mix_g'), vec('ln_mix_b')],
                          two_f32, [], residual_ln, name="out_proj_ln")
        hosted = plan.forward_host(i)
        if hosted is None:
            a, gg, uu = _ffn_up(x1, lw['w_up'], name=f"ffn_up")
        else:
            (a, gg, uu), arrived = _ffn_up(x1, lw['w_up'], exchange=hosted, name=f"ffn_up_gather")
            plan.forward_hosted(i, arrived)
        r2, x2 = _mm_rows([([(a, 0, D_FF)], lw['w_down'], False)], [x1], [vec('ln_ffn_g'), vec('ln_ffn_b')],
                          two_f32, [], residual_ln, name="ffn_down_ln")
        x3, gate, ppb = _mm_rows(
            [([(x2, 0, D_MODEL)], lw['w_gate'], False), ([(p[i], 0, D_PLE)], lw['w_proj'], False)],
            [x2], [vec('ple_b_gate')], [(D_MODEL, F32), (D_MODEL, BF16), (D_MODEL, BF16)], [], embed_gate, name="ple")
        s.update(h=h, ya=ya, yb=yb, r1=r1, x1=x1, a=a, gg=gg, uu=uu, r2=r2, x2=x2, gate=gate, ppb=ppb)
        saved.append(s)
        x = x3

    loss, da = _loss_head(x, target, name="loss_head")
    db = None
    grads = {n: [None] * (DEPTH if n.startswith(('ln_', 'ple_')) else DEPTH // 2) for n in WEIGHTS if n not in BIG}

    def both(pieces, axis):
        return tuple(jnp.concatenate([pc[k] for pc in pieces], axis=axis) for k in range(2))

    for i in reversed(range(DEPTH)):
        li = i // 2
        s = saved[i]
        lw = s['lw']
        big = {}
        dz, dpp, dr2, dbg, dg2, db2 = _ple_ln_bwd(da, db, s['gate'], s['ppb'], s['r2'], lw['w_gate'],
                                                  w['ln_ffn_g'][i].reshape(1, -1), name="ple_ln_bwd")
        grads['ple_b_gate'][i] = dbg.reshape(-1)
        big['w_gate'] = _mm_tn(s['x2'], 0, D_MODEL, dz, also_bf16=True, name=f"d_ple_gate")
        big['w_proj'] = _mm_tn(p[i], 0, D_PLE, dpp, also_bf16=True, name=f"d_ple_proj")
        grads['ln_ffn_g'][i] = dg2.reshape(-1)
        grads['ln_ffn_b'][i] = db2.reshape(-1)
        dhh = _ffn_down_bwd(dr2, lw['w_down'], s['gg'], s['uu'], name=f"ffn_down_bwd")
        big['w_down'] = _mm_tn(s['a'], 0, D_FF, dr2, tk=D_FF // 2, also_bf16=True, name=f"d_ffn_down")
        hosted = plan.backward_host(i)
        if hosted is None:
            big['w_up'] = _mm_tn(s['x1'], 0, D_MODEL, dhh, tn=D_FF // 2, also_bf16=True, name=f"d_ffn_up")
        else:
            big['w_up'], arrived = _mm_tn(s['x1'], 0, D_MODEL, dhh, tn=D_FF // 2, also_bf16=True, exchange=hosted,
                                          name=f"d_ffn_up_reduce_{i % 2}")
            plan.backward_hosted(i, arrived)

        def ln_mix_grad(products, rows, vecs):
            dr, dg, dbias = _ln_grad(rows[0], ALPHA * rows[1] + products[0], vecs[0])
            return (dr,), (dg, dbias)

        dr1, dg1, db1 = _mm_rows([([(dhh, 0, 2 * D_FF)], lw['w_up'], True)], [s['r1'], dr2],
                                 [w['ln_mix_g'][i].reshape(1, -1)], [(D_MODEL, F32)], [D_MODEL, D_MODEL], ln_mix_grad,
                                 tm=256, vmem=VMEM_BIG, name="ffn_up_ln_bwd")
        grads['ln_mix_g'][i] = dg1.reshape(-1)
        grads['ln_mix_b'][i] = db1.reshape(-1)
        dcat = _mm([(dr1, 0, D_MODEL)], lw['w_out'], trans_b=True, name=f"out_proj_bwd")
        big['w_out'] = both([_mm_tn(s['ya'], 0, MIX, dr1, also_bf16=True, name=f"d_out_a"),
                             _mm_tn(s['yb'], 0, MIX, dr1, also_bf16=True, name=f"d_out_b")], 0)
        h = s['h']
        if i % 2 == 0:
            s5c = s['s5c']
            du, xb, gb, gq, dzzq, dyq, dlam, dbglu, dd = _s5_bwd(dcat, s['ypre'], h, s['hb'], *s5c, name=f"s5_bwd")
            dbmat = _mm_tn_slabs(h, 128, gb, SLAB_COLS, S5_SLABS, name=f"d_s5_b")
            dcmat = _mm_tn_slabs(xb, SLAB_COLS, dyq, 128, S5_SLABS, name=f"d_s5_c")
            s5g = s['s5_vjp']((dlam, dbmat, dcmat))
            for n, g_ in zip(['ev_lambda_re', 'ev_lambda_im', 'ev_log_dt', 'ev_b_re', 'ev_b_im', 'ev_c_re', 'ev_c_im'], s5g):
                grads[n][li] = g_
            big['w_glu'] = _mm_tn(gq, 0, MIX, dzzq, also_bf16=True, name=f"d_glu")
            grads['ev_b_glu'][li] = dbglu.reshape(-1)
            grads['ev_d'][li] = dd.reshape(-1)
            d3, dcw = _conv_bwd(dcat, h, w['ev_conv_w'][li], name=f"conv_bwd")
            grads['ev_conv_w'][li] = dcw[0:3]
            big['w_in'] = both([_mm_tn(s['x0'], 0, D_MODEL, du, also_bf16=True, name=f"d_in_a"),
                                _mm_tn(s['x0'], 0, D_MODEL, d3, tn=3 * MIX, also_bf16=True, name=f"d_in_b")], 1)
            db = _mm([(du, 0, MIX), (d3, 0, 3 * MIX)], lw['w_in'], trans_b=True, name=f"in_proj_bwd")
        else:
            dq, dk, dv, dvd = _attn_bwd(dcat, h, s['vdiag'], mask, name=f"attn_bwd")
            dzp, dpw, dps = _pool_bwd(dcat, h, s['pw'], s['ps'], name=f"pool_bwd")
            parts = [dq, dk, dv, dzp]
            grads['od_rel_bias'][li] = jnp.dot(dvd.reshape(ATT_HEADS, NKEY), onehot, precision=HIGHEST)
            grads['od_pool_w'][li] = dpw
            grads['od_pool_scale'][li] = dps.reshape(-1)
            big['w_in'] = both([_mm_tn(s['x0'], 0, D_MODEL, d_, also_bf16=True, name=f"d_in_a") for d_ in parts], 1)
            db = _mm([(d_, 0, MIX) for d_ in parts], lw['w_in'], trans_b=True, name=f"in_proj_bwd")
        plan.layer_grads(i, big)
        da = dr1
    grad_x = _add_n([da, db], [ALPHA, 1.0], name="grad_x")
    return loss, grad_x, {n: jnp.stack(g) for n, g in grads.items()}


def _sum_blocks(own, others, *, name):
    rows, cols = own.shape
    t = _row_tile(rows)

    def body(own_ref, others_ref, o_ref):
        acc = own_ref[...]
        for k in range(N_DEV - 1):
            acc = acc + others_ref[k].astype(F32)
        o_ref[...] = acc

    return pl.pallas_call(
        body, name=name, grid=(rows // t,),
        in_specs=[pl.BlockSpec((t, cols), lambda i: (i, 0)), pl.BlockSpec((N_DEV - 1, t, cols), lambda i: (0, i, 0))],
        out_specs=pl.BlockSpec((t, cols), lambda i: (i, 0)), out_shape=jax.ShapeDtypeStruct((rows, cols), F32),
        compiler_params=_cparams(("parallel",)),
    )(own, others)


def _swap_sibling(arrs, *, name):
    n = len(arrs)

    def body(*refs):
        ins, outs = refs[:n], refs[n:2 * n]
        send_sems, recv_sems = refs[2 * n:]
        x, y, c, _ = _place()
        cps = [pltpu.make_async_remote_copy(src_ref=ins[k], dst_ref=outs[k], send_sem=send_sems.at[k], recv_sem=recv_sems.at[k],
                                            device_id=(x, y, 1 - c), device_id_type=MESH) for k in range(n)]
        for cp in cps:
            cp.start()
        for cp in cps:
            cp.wait_recv()
        for cp in cps:
            cp.wait_send()

    return pl.pallas_call(
        body, name=name, in_specs=[ANY] * n, out_specs=[ANY] * n,
        out_shape=[jax.ShapeDtypeStruct(a.shape, a.dtype) for a in arrs],
        scratch_shapes=[pltpu.SemaphoreType.DMA((n,)), pltpu.SemaphoreType.DMA((n,))],
    )(*arrs)


class _ShardedPlan:
    def __init__(self, a, c, me):
        self.a, self.c, self.me = a, c, me
        self.weights, self.pending, self.own, self.arrived = {}, None, {}, {}

    def _shards(self, i):
        return [self.a[n][l].astype(BF16) for _, n, l in _layer_big(i)]

    def _with_own(self, gathered, own):
        return lax.dynamic_update_index_in_dim(gathered, own, self.me, 0)

    def _set_weights(self, i, gathered):
        lw = {}
        for (k, n, _), g, own in zip(_layer_big(i), gathered, self._shards(i)):
            _, rows, cols = g.shape
            g = self._with_own(g, own)
            lw[k] = g.transpose(1, 0, 2).reshape(rows, N_CHIPS * cols) if BIG[n][0] == 2 else g.reshape(N_CHIPS * rows, cols)
        self.weights[i] = lw

    def gather_first(self, misc):
        gathered = _run_exchange(_GatherExchange(self._shards(0) + [misc]), name="weight_gather_0")
        self._set_weights(0, gathered[:-1])
        return self._with_own(gathered[-1], misc)

    def layer_weights(self, i):
        return self.weights.pop(i)

    def forward_host(self, i):
        return _GatherExchange(self._shards(i + 1)) if i + 1 < DEPTH else None

    def forward_hosted(self, i, arrived):
        self._set_weights(i + 1, arrived)

    def _reduce_exchange(self):
        i, g = self.pending
        return _ReduceExchange([g[k][1] for k, _, _ in _layer_big(i)], [BIG[n][0] for _, n, _ in _layer_big(i)])

    def layer_grads(self, i, g):
        for k, n, l in _layer_big(i):
            full = g[k][0]
            kk, nn = full.shape
            if BIG[n][0] == 2:
                self.own[n, l] = lax.dynamic_slice(full, (self.c * (kk // 2), self.me * (nn // N_CHIPS)), (kk // 2, nn // N_CHIPS))
            else:
                self.own[n, l] = lax.dynamic_slice_in_dim(full, (2 * self.me + self.c) * (kk // N_DEV), kk // N_DEV, axis=0)
        self.pending = (i, g)

    def backward_host(self, i):
        return self._reduce_exchange() if i + 1 < DEPTH else None

    def backward_hosted(self, i, arrived):
        for (_, n, l), r in zip(_layer_big(i + 1), arrived):
            self.arrived[n, l] = r

    def reduced(self):
        for (_, n, l), r in zip(_layer_big(0), _run_exchange(self._reduce_exchange(), name="grad_reduce_0")):
            self.arrived[n, l] = r
        keys = [(n, l) for n in BIG for l in range(BIG[n][1][0])]
        mine = [_sum_blocks(self.own[k], self.arrived[k], name=f"grad_sum_{k[0]}") for k in keys]
        theirs = _swap_sibling(mine, name="grad_half_swap")
        out = {}
        for n in BIG:
            layers = []
            for l in range(BIG[n][1][0]):
                a_, b_ = mine[keys.index((n, l))], theirs[keys.index((n, l))]
                layers.append(jnp.where(self.c == 0, jnp.concatenate([a_, b_], axis=0), jnp.concatenate([b_, a_], axis=0)))
            out[n] = jnp.stack(layers)
        return out


def kernel(x, p, ev_w_in, ev_lambda_re, ev_lambda_im, ev_log_dt, ev_b_re, ev_b_im, ev_c_re, ev_c_im, ev_d, ev_w_glu, ev_b_glu, ev_conv_w, ev_w_out, od_w_in, od_rel_bias, od_pool_w, od_pool_scale, od_w_out, ln_mix_g, ln_mix_b, ln_ffn_g, ln_ffn_b, ffn_w_up, ffn_w_down, ple_w_proj, ple_w_gate, ple_b_gate, loss_target, m_ev_w_in, m_ev_lambda_re, m_ev_lambda_im, m_ev_log_dt, m_ev_b_re, m_ev_b_im, m_ev_c_re, m_ev_c_im, m_ev_d, m_ev_w_glu, m_ev_b_glu, m_ev_conv_w, m_ev_w_out, m_od_w_in, m_od_rel_bias, m_od_pool_w, m_od_pool_scale, m_od_w_out, m_ln_mix_g, m_ln_mix_b, m_ln_ffn_g, m_ln_ffn_b, m_ffn_w_up, m_ffn_w_down, m_ple_w_proj, m_ple_w_gate, m_ple_b_gate, v_ev_w_in, v_ev_lambda_re, v_ev_lambda_im, v_ev_log_dt, v_ev_b_re, v_ev_b_im, v_ev_c_re, v_ev_c_im, v_ev_d, v_ev_w_glu, v_ev_b_glu, v_ev_conv_w, v_ev_w_out, v_od_w_in, v_od_rel_bias, v_od_pool_w, v_od_pool_scale, v_od_w_out, v_ln_mix_g, v_ln_mix_b, v_ln_ffn_g, v_ln_ffn_b, v_ffn_w_up, v_ffn_w_down, v_ple_w_proj, v_ple_w_gate, v_ple_b_gate):
    given = locals()
    a = {n: given[n] for n in INPUTS}
    x, y, c = lax.axis_index("x"), lax.axis_index("y"), lax.axis_index("c")
    me = 2 * x + y

    plan = _ShardedPlan(a, c, me)
    misc = jnp.concatenate([a['ev_conv_w'].reshape(6, 128), a['od_pool_scale'], jnp.zeros((8, 128), F32)], axis=0)
    gm = plan.gather_first(misc)
    w = {n: a[n] for n in REPLICATED}
    w['ev_conv_w'] = gm[:, 0:6].reshape(N_CHIPS, 2, 3, 128).transpose(1, 2, 0, 3).reshape(2, 3, 512)
    w['od_pool_scale'] = gm[:, 6:8].transpose(1, 0, 2).reshape(2, 512)

    loss, grad_x, grads = _local_step(a['x'][0], a['p'][:, 0], a['loss_target'][0], w, plan)
    loss = lax.psum(loss[0, 0], ("x", "y", "c"))

    small_names = REPLICATED + list(SMALL_SHARDED)
    small = _all_reduce_small(_pack([grads[n] for n in small_names]), name="small_grad_all_reduce")
    small = dict(zip(small_names, _unpack(small, [grads[n].shape for n in small_names])))
    for n in SMALL_SHARDED:
        small[n] = lax.dynamic_slice_in_dim(small[n], me * 128, 128, axis=small[n].ndim - 1)
    big = plan.reduced()

    res = {}
    for n in BIG:
        shape = a[n].shape
        flat = _shard_rows(n)
        d, m_, v_ = _adamw(a[n].reshape(flat), big[n].reshape(flat), a['m_' + n].reshape(flat), a['v_' + n].reshape(flat),
                           name=f"adamw_{n}")
        res[n] = (big[n], d.reshape(shape), m_.reshape(shape), v_.reshape(shape))
    shapes = [a[n].shape for n in small_names]
    d, m_, v_ = _adamw(_pack([a[n] for n in small_names]), _pack([small[n] for n in small_names]),
                       _pack([a['m_' + n] for n in small_names]), _pack([a['v_' + n] for n in small_names]), name="adamw_small")
    for n, dd, mm, vv in zip(small_names, _unpack(d, shapes), _unpack(m_, shapes), _unpack(v_, shapes)):
        res[n] = (small[n], dd, mm, vv)

    outs = [loss, grad_x[None]]
    for part in range(4):
        outs += [res[n][part] for n in WEIGHTS]
    return tuple(outs)
```

```python
import functools
import math

import jax
import jax.numpy as jnp
import numpy as np
from jax import lax
from jax.experimental import pallas as pl
from jax.experimental.pallas import tpu as pltpu

F32 = jnp.float32
BF16 = jnp.bfloat16
MESH = pl.DeviceIdType.MESH
HIGHEST = lax.Precision.HIGHEST

D_MODEL = 1024
DEPTH = 4
MIX = 512
S5_GROUPS = 32
S5_GROUP = 16
S5_STATE = 64
S5_N = S5_GROUPS * S5_STATE
CHUNK = 64
LEFT_CHUNKS = 8
MAX_REL = 128
ATT_HEADS = 8
HEAD_DIM = 64
POOL_WINDOWS = (2, 4, 8, 16)
POOL_GROUP = 128
D_FF = 2816
D_PLE = 256
ALPHA = (2 * DEPTH) ** 0.25
LN_EPS = 1e-5
NEG_INF = -1e30
N_CHIPS = 4
N_DEV = 8

ADAM_LR = 0.001
ADAM_B1 = 0.9
ADAM_B2 = 0.999
ADAM_EPS = 1e-08
ADAM_WD = 0.01
ADAM_STEP = 10

TM = 512
T_S5 = 256
T_ATT = 512
VMEM_BIG = 56 * 1024 * 1024


VMEM_DEFAULT = 48 * 1024 * 1024


def _cparams(sem, vmem=None):
    return pltpu.CompilerParams(dimension_semantics=sem, vmem_limit_bytes=vmem or VMEM_DEFAULT)


def _sigmoid(x):
    return 1.0 / (1.0 + jnp.exp(-x))


def _mm(a_parts, b, *, name, trans_b=False, out_dtype=F32, tm=TM, tn=1024, vmem=None):
    m = a_parts[0][0].shape[0]
    n = b.shape[0] if trans_b else b.shape[1]
    kk = b.shape[1] if trans_b else b.shape[0]
    tn = min(tn, n)
    widths = [w for _, _, w in a_parts]
    assert sum(widths) == kk and m % tm == 0 and n % tn == 0
    na = len(a_parts)

    def body(*refs):
        b_ref, o_ref = refs[na], refs[na + 1]
        acc = None
        k0 = 0
        for ar, w in zip(refs[:na], widths):
            a = ar[...].astype(BF16)
            if trans_b:
                part = lax.dot_general(a, b_ref[:, k0:k0 + w], (((1,), (1,)), ((), ())), preferred_element_type=F32)
            else:
                part = jnp.dot(a, b_ref[k0:k0 + w, :], preferred_element_type=F32)
            acc = part if acc is None else acc + part
            k0 += w
        o_ref[...] = acc.astype(o_ref.dtype)

    in_specs = [pl.BlockSpec((tm, w), functools.partial(lambda j, i, cb: (i, cb), cb=cb)) for _, cb, w in a_parts]
    if trans_b:
        in_specs.append(pl.BlockSpec((tn, kk), lambda j, i: (j, 0)))
    else:
        in_specs.append(pl.BlockSpec((kk, tn), lambda j, i: (0, j)))
    return pl.pallas_call(
        body, name=name, grid=(n // tn, m // tm), in_specs=in_specs,
        out_specs=pl.BlockSpec((tm, tn), lambda j, i: (i, j)),
        out_shape=jax.ShapeDtypeStruct((m, n), out_dtype),
        compiler_params=_cparams(("parallel", "parallel"), vmem),
    )(*[a for a, _, _ in a_parts], b)


def _host_parts(exchange):
    if exchange is None:
        return [], [], [], [], []
    any_space = pl.BlockSpec(memory_space=pl.ANY)
    return (exchange.ins, [any_space] * len(exchange.ins), [any_space] * len(exchange.out_shapes),
            list(exchange.out_shapes), list(exchange.sems))


def _host_run(exchange, refs, first, last):
    if exchange is None:
        return
    n_in, n_out = len(exchange.ins), len(exchange.out_shapes)
    ins, outs, sems = refs[:n_in], refs[n_in:n_in + n_out], refs[n_in + n_out:]

    @pl.when(first)
    def _():
        exchange.start(ins, outs, sems)

    @pl.when(last)
    def _():
        exchange.finish(ins, outs, sems)


def _mm_tn(a, a_cb, ka, b, *, name, tk=1024, tn=1024, tmr=2 * TM, vmem=None, also_bf16=False, exchange=None):
    m = a.shape[0]
    n = b.shape[1]
    tk = min(tk, ka)
    tn = min(tn, n)
    assert ka % tk == 0 and n % tn == 0 and m % tmr == 0
    kb = ka // tk
    grid = (kb, n // tn, m // tmr)
    ex_ops, ex_in_specs, ex_out_specs, ex_out_shapes, ex_scratch = _host_parts(exchange)
    n_own_out = 2 if also_bf16 else 1

    def body(*refs):
        a_ref, b_ref = refs[:2]
        hosted_in = refs[2:2 + len(ex_ops)]
        outs = refs[2 + len(ex_ops):]
        o_ref = outs[0]
        k, j, r = pl.program_id(0), pl.program_id(1), pl.program_id(2)
        _host_run(exchange, list(hosted_in) + list(outs[n_own_out:]),
                  (k == 0) & (j == 0) & (r == 0), (k == grid[0] - 1) & (j == grid[1] - 1) & (r == grid[2] - 1))

        @pl.when(r == 0)
        def _():
            o_ref[...] = jnp.zeros_like(o_ref)

        o_ref[...] += lax.dot_general(a_ref[...].astype(BF16), b_ref[...].astype(BF16), (((0,), (0,)), ((), ())),
                                      preferred_element_type=F32)
        if also_bf16:
            @pl.when(r == grid[2] - 1)
            def _():
                outs[1][...] = o_ref[...].astype(BF16)

    tile = pl.BlockSpec((tk, tn), lambda k, j, r: (k, j))
    res = pl.pallas_call(
        body, name=name, grid=grid,
        in_specs=[pl.BlockSpec((tmr, tk), lambda k, j, r: (r, a_cb * kb + k)),
                  pl.BlockSpec((tmr, tn), lambda k, j, r: (r, j))] + ex_in_specs,
        out_specs=[tile] * n_own_out + ex_out_specs,
        out_shape=[jax.ShapeDtypeStruct((ka, n), F32)] + ([jax.ShapeDtypeStruct((ka, n), BF16)] if also_bf16 else [])
        + ex_out_shapes,
        scratch_shapes=ex_scratch,
        compiler_params=_cparams(("arbitrary",) * 3 if exchange is not None else ("parallel", "parallel", "arbitrary"), vmem),
    )(a, b, *ex_ops)
    if exchange is None:
        return tuple(res) if also_bf16 else res[0]
    own = tuple(res[:n_own_out]) if also_bf16 else res[0]
    return own, list(res[n_own_out:])


def _mm_tn_slabs(a, ka, b, nbw, nslab, *, name, tmr=2 * TM):
    m = a.shape[0]
    assert m % tmr == 0

    def body(a_ref, b_ref, o_ref):
        @pl.when(pl.program_id(1) == 0)
        def _():
            o_ref[...] = jnp.zeros_like(o_ref)

        o_ref[0] += lax.dot_general(a_ref[...].astype(BF16), b_ref[...].astype(BF16), (((0,), (0,)), ((), ())),
                                    preferred_element_type=F32)

    return pl.pallas_call(
        body, name=name, grid=(nslab, m // tmr),
        in_specs=[pl.BlockSpec((tmr, ka), lambda s, r: (r, s)), pl.BlockSpec((tmr, nbw), lambda s, r: (r, s))],
        out_specs=pl.BlockSpec((1, ka, nbw), lambda s, r: (s, 0, 0)),
        out_shape=jax.ShapeDtypeStruct((nslab, ka, nbw), F32),
        compiler_params=_cparams(("parallel", "arbitrary")),
    )(a, b)


def _ln_stats(r):
    mu = jnp.mean(r, axis=-1, keepdims=True)
    xc = r - mu
    var = jnp.mean(xc * xc, axis=-1, keepdims=True)
    rstd = lax.rsqrt(var + LN_EPS)
    return xc * rstd, rstd


def _ln_apply(r, g, b):
    xhat, _ = _ln_stats(r)
    return xhat * g + b


def _ln_grad(r, dy, g):
    xhat, rstd = _ln_stats(r)
    dxh = dy * g
    m1 = jnp.mean(dxh, axis=-1, keepdims=True)
    m2 = jnp.mean(dxh * xhat, axis=-1, keepdims=True)
    return (rstd * (dxh - m1 - xhat * m2), jnp.sum(dy * xhat, axis=0, keepdims=True), jnp.sum(dy, axis=0, keepdims=True))


def _mm_rows(matmuls, rows_in, vecs_in, out_rows, acc_widths, fn, *, name, tm=TM, vmem=None):
    m = rows_in[0].shape[0]
    assert m % tm == 0
    flat, in_specs, layout = [], [], []
    for a_parts, b, trans_b in matmuls:
        for arr, cb, w in a_parts:
            flat.append(arr)
            in_specs.append(pl.BlockSpec((tm, w), functools.partial(lambda i, cb: (i, cb), cb=cb)))
        flat.append(b)
        in_specs.append(pl.BlockSpec(b.shape, lambda i: (0, 0)))
        layout.append(([w for _, _, w in a_parts], trans_b))
    for r in rows_in:
        flat.append(r)
        in_specs.append(pl.BlockSpec((tm, r.shape[1]), lambda i: (i, 0)))
    for v in vecs_in:
        flat.append(v)
        in_specs.append(pl.BlockSpec(v.shape, lambda i: (0, 0)))
    n_in = len(flat)
    n_rows_out = len(out_rows)

    def body(*refs):
        pos = 0
        products = []
        for widths, trans_b in layout:
            b_ref = refs[pos + len(widths)]
            acc, k0 = None, 0
            for ar, w in zip(refs[pos:pos + len(widths)], widths):
                a = ar[...].astype(BF16)
                if trans_b:
                    part = lax.dot_general(a, b_ref[:, k0:k0 + w], (((1,), (1,)), ((), ())), preferred_element_type=F32)
                else:
                    part = jnp.dot(a, b_ref[k0:k0 + w, :], preferred_element_type=F32)
                acc = part if acc is None else acc + part
                k0 += w
            products.append(acc)
            pos += len(widths) + 1
        rows = [r[...] for r in refs[pos:pos + len(rows_in)]]
        pos += len(rows_in)
        vecs = [v[...] for v in refs[pos:n_in]]
        outs, sums = fn(products, rows, vecs)
        for o_ref, o in zip(refs[n_in:n_in + n_rows_out], outs):
            o_ref[...] = o.astype(o_ref.dtype)
        if acc_widths:
            acc_refs = refs[n_in + n_rows_out:]

            @pl.when(pl.program_id(0) == 0)
            def _():
                for a_ref in acc_refs:
                    a_ref[...] = jnp.zeros_like(a_ref)

            for a_ref, s_ in zip(acc_refs, sums):
                a_ref[...] += s_

    out_specs = [pl.BlockSpec((tm, n), lambda i: (i, 0)) for n, _ in out_rows]
    out_specs += [pl.BlockSpec((1, wd), lambda i: (0, 0)) for wd in acc_widths]
    out_shape = [jax.ShapeDtypeStruct((m, n), dt) for n, dt in out_rows]
    out_shape += [jax.ShapeDtypeStruct((1, wd), F32) for wd in acc_widths]
    return pl.pallas_call(
        body, name=name, grid=(m // tm,), in_specs=in_specs, out_specs=out_specs, out_shape=out_shape,
        compiler_params=_cparams(("arbitrary",) if acc_widths else ("parallel",), vmem),
    )(*flat)


def _ffn_up(x1, wup, *, name, exchange=None):
    m = x1.shape[0]
    tn = D_FF // 2
    grid = (2, m // TM)
    ex_ops, ex_in_specs, ex_out_specs, ex_out_shapes, ex_scratch = _host_parts(exchange)

    def body(*refs):
        x_ref, wg_ref, wu_ref = refs[:3]
        hosted_in = refs[3:3 + len(ex_ops)]
        a_ref, g_ref, u_ref = refs[3 + len(ex_ops):6 + len(ex_ops)]
        j, i = pl.program_id(0), pl.program_id(1)
        _host_run(exchange, list(hosted_in) + list(refs[6 + len(ex_ops):]),
                  (j == 0) & (i == 0), (j == grid[0] - 1) & (i == grid[1] - 1))
        x = x_ref[...].astype(BF16)
        g = jnp.dot(x, wg_ref[...], preferred_element_type=F32)
        u = jnp.dot(x, wu_ref[...], preferred_element_type=F32)
        a_ref[...] = (g * _sigmoid(g) * u).astype(BF16)
        g_ref[...] = g.astype(BF16)
        u_ref[...] = u.astype(BF16)

    out = pl.BlockSpec((TM, tn), lambda j, i: (i, j))
    res = pl.pallas_call(
        body, name=name, grid=grid,
        in_specs=[pl.BlockSpec((TM, D_MODEL), lambda j, i: (i, 0)),
                  pl.BlockSpec((D_MODEL, tn), lambda j, i: (0, j)),
                  pl.BlockSpec((D_MODEL, tn), lambda j, i: (0, j + 2))] + ex_in_specs,
        out_specs=[out, out, out] + ex_out_specs,
        out_shape=[jax.ShapeDtypeStruct((m, D_FF), BF16)] * 3 + ex_out_shapes,
        scratch_shapes=ex_scratch,
        compiler_params=_cparams(("arbitrary", "arbitrary") if exchange is not None else ("parallel", "parallel")),
    )(x1, wup, wup, *ex_ops)
    return (res[0], res[1], res[2]) if exchange is None else ((res[0], res[1], res[2]), list(res[3:]))


def _ffn_down_bwd(df, wdown, g, u, *, name):
    m = df.shape[0]
    tm = 256

    def body(df_ref, w_ref, g_ref, u_ref, o_ref):
        df = df_ref[...].astype(BF16)
        half = D_FF // 2
        for part in range(2):
            cols = slice(half * part, half * (part + 1))
            da = lax.dot_general(df, w_ref[cols, :], (((1,), (1,)), ((), ())), preferred_element_type=F32)
            gg = g_ref[:, cols].astype(F32)
            sg = _sigmoid(gg)
            o_ref[:, cols] = (da * u_ref[:, cols].astype(F32) * (sg * (1.0 + gg * (1.0 - sg)))).astype(BF16)
            o_ref[:, D_FF + half * part:D_FF + half * (part + 1)] = (da * (gg * sg)).astype(BF16)

    return pl.pallas_call(
        body, name=name, grid=(m // tm,),
        in_specs=[pl.BlockSpec((tm, D_MODEL), lambda i: (i, 0)), pl.BlockSpec((D_FF, D_MODEL), lambda i: (0, 0)),
                  pl.BlockSpec((tm, D_FF), lambda i: (i, 0)), pl.BlockSpec((tm, D_FF), lambda i: (i, 0))],
        out_specs=pl.BlockSpec((tm, 2 * D_FF), lambda i: (i, 0)),
        out_shape=jax.ShapeDtypeStruct((m, 2 * D_FF), BF16),
        compiler_params=_cparams(("parallel",), VMEM_BIG),
    )(df, wdown, g, u)


def _ple_ln_bwd(da, db, gate, pp, r2, wgate, g2, *, name):
    m, n = da.shape
    two = db is not None
    n_in = 7 if two else 6

    def body(*refs):
        if two:
            da_ref, db_ref, gate_ref, pp_ref, r_ref, w_ref, g_ref = refs[:n_in]
            dx3 = ALPHA * da_ref[...] + db_ref[...]
        else:
            da_ref, gate_ref, pp_ref, r_ref, w_ref, g_ref = refs[:n_in]
            dx3 = da_ref[...]
        dz_ref, dpp_ref, dr_ref, dbg_ref, dg_ref, dbias_ref = refs[n_in:]

        @pl.when(pl.program_id(0) == 0)
        def _():
            dbg_ref[...] = jnp.zeros_like(dbg_ref)
            dg_ref[...] = jnp.zeros_like(dg_ref)
            dbias_ref[...] = jnp.zeros_like(dbias_ref)

        gate = gate_ref[...].astype(F32)
        dz = dx3 * pp_ref[...].astype(F32) * gate * (1.0 - gate)
        dzq = dz.astype(BF16)
        dz_ref[...] = dzq
        dpp_ref[...] = (dx3 * gate).astype(BF16)
        dbg_ref[...] += jnp.sum(dz, axis=0, keepdims=True)
        dx2 = dx3 + lax.dot_general(dzq, w_ref[...], (((1,), (1,)), ((), ())), preferred_element_type=F32)
        dr, dg, dbias = _ln_grad(r_ref[...], dx2, g_ref[...])
        dr_ref[...] = dr
        dg_ref[...] += dg
        dbias_ref[...] += dbias

    row = pl.BlockSpec((TM, n), lambda i: (i, 0))
    vec = pl.BlockSpec((1, n), lambda i: (0, 0))
    ins = ([da, db] if two else [da]) + [gate, pp, r2, wgate, g2]
    in_specs = [row] * (n_in - 2) + [pl.BlockSpec(wgate.shape, lambda i: (0, 0)), vec]
    return pl.pallas_call(
        body, name=name, grid=(m // TM,), in_specs=in_specs, out_specs=[row, row, row, vec, vec, vec],
        out_shape=[jax.ShapeDtypeStruct((m, n), BF16), jax.ShapeDtypeStruct((m, n), BF16), jax.ShapeDtypeStruct((m, n), F32)]
        + [jax.ShapeDtypeStruct((1, n), F32)] * 3,
        compiler_params=_cparams(("arbitrary",)),
    )(*ins)


def _loss_head(y, target, *, name):
    m, n = y.shape

    def body(y_ref, t_ref, loss_ref, dy_ref):
        @pl.when(pl.program_id(0) == 0)
        def _():
            loss_ref[...] = jnp.zeros_like(loss_ref)

        err = y_ref[...] - t_ref[...]
        dy_ref[...] = err * (1.0 / n)
        per_tok = jnp.mean(err * err, axis=-1, keepdims=True)
        loss_ref[...] += 0.5 * jnp.sum(per_tok, axis=0, keepdims=True)

    row = pl.BlockSpec((TM, n), lambda i: (i, 0))
    return pl.pallas_call(
        body, name=name, grid=(m // TM,), in_specs=[row, row],
        out_specs=[pl.BlockSpec((1, 1), lambda i: (0, 0)), row],
        out_shape=[jax.ShapeDtypeStruct((1, 1), F32), jax.ShapeDtypeStruct((m, n), F32)],
        compiler_params=_cparams(("arbitrary",)),
    )(y, target)


def _gelu(y):
    c = math.sqrt(2.0 / math.pi)
    return 0.5 * y * (1.0 + jnp.tanh(c * (y + 0.044715 * y * y * y)))


def _gelu_grad(y):
    c = math.sqrt(2.0 / math.pi)
    t = jnp.tanh(c * (y + 0.044715 * y * y * y))
    return 0.5 * (1.0 + t) + 0.5 * y * (1.0 - t * t) * c * (1.0 + 3.0 * 0.044715 * y * y)


STRIP = 128
S5_SLABS = 4
SLAB_COLS = 2 * S5_N // S5_SLABS
N_TILES = 2 * S5_N // STRIP
SLAB_TILES = SLAB_COLS // STRIP


def _strip_tiles(j):
    re_tile = (j // (SLAB_TILES // 2)) * SLAB_TILES + j % (SLAB_TILES // 2)
    return pl.multiple_of(j * STRIP, STRIP), re_tile, re_tile + SLAB_TILES // 2


def _store_tiles(ref, first_tile, value):
    for k in range(value.shape[1] // STRIP):
        ref[first_tile + k] = value[:, STRIP * k:STRIP * (k + 1)]


def _load_tiles(ref, first_tile, count):
    return jnp.concatenate([ref[first_tile + k] for k in range(count)], axis=1)


PTAB_ROWS = 40
GROUPS = T_S5 // 8


def _scan_cols(ref, hr, hi, ptab_ref, off, down, visit=None):
    sign = 1.0 if down else -1.0
    ref_r, ref_i = ref
    cols_p = pl.ds(off, STRIP)

    def power(row, im_offset=8):
        return ptab_ref[row:row + 1, cols_p], sign * ptab_ref[row + im_offset:row + im_offset + 1, cols_p]

    def rows(r):
        return pl.ds(r, GROUPS, stride=8)

    def mul_add(br, bi, qr, qi, vr, vi):
        return br + qr * vr - qi * vi, bi + qr * vi + qi * vr

    order = list(range(8)) if down else list(range(7, -1, -1))
    lam_r, lam_i = power(0)
    vr, vi = ref_r[rows(order[0]), :], ref_i[rows(order[0]), :]
    for r in order[1:]:
        vr, vi = mul_add(ref_r[rows(r), :], ref_i[rows(r), :], lam_r, lam_i, vr, vi)
        ref_r[rows(r), :] = vr
        ref_i[rows(r), :] = vi
    grow = lax.broadcasted_iota(jnp.int32, (GROUPS, STRIP), 0)
    edge = 0 if down else GROUPS - 1
    l8r, l8i = power(7)
    er = vr + jnp.where(grow == edge, l8r * hr - l8i * hi, 0.0)
    ei = vi + jnp.where(grow == edge, l8r * hi + l8i * hr, 0.0)
    k, step = 0, 1
    while step < GROUPS:
        qr, qi = (l8r, l8i) if k == 0 else power(32 + k - 1, 4)
        if down:
            sr = jnp.where(grow >= step, pltpu.roll(er, step, 0), 0.0)
            si = jnp.where(grow >= step, pltpu.roll(ei, step, 0), 0.0)
        else:
            sr = jnp.where(grow < GROUPS - step, pltpu.roll(er, GROUPS - step, 0), 0.0)
            si = jnp.where(grow < GROUPS - step, pltpu.roll(ei, GROUPS - step, 0), 0.0)
        er, ei = mul_add(er, ei, qr, qi, sr, si)
        k, step = k + 1, 2 * step
    if down:
        cr = jnp.where(grow == 0, hr, pltpu.roll(er, 1, 0))
        ci = jnp.where(grow == 0, hi, pltpu.roll(ei, 1, 0))
    else:
        cr = jnp.where(grow == GROUPS - 1, hr, pltpu.roll(er, GROUPS - 1, 0))
        ci = jnp.where(grow == GROUPS - 1, hi, pltpu.roll(ei, GROUPS - 1, 0))
    for r in range(8):
        qr, qi = power(r if down else 16 + r)
        xr, xi = mul_add(ref_r[rows(r), :], ref_i[rows(r), :], qr, qi, cr, ci)
        ref_r[rows(r), :] = xr
        ref_i[rows(r), :] = xi
        if visit is not None:
            visit(r, xr, xi)
    last = GROUPS - 1 if down else 0
    return er[last:last + 1], ei[last:last + 1]


def _s5_fwd(h, bmat, cmat, dvec, wglu, bglu, ptab, *, name):
    m = h.shape[0]
    t = T_S5
    nb = m // t

    def body(u_ref, bmat_ref, cmat_ref, d_ref, wglu_ref, bglu_ref, ptab_ref,
             out_ref, y_ref, hb_ref, bu_ref, carry_ref):
        @pl.when(pl.program_id(0) == 0)
        def _():
            carry_ref[...] = jnp.zeros_like(carry_ref)

        hb_ref[0] = carry_ref[...]
        u = u_ref[...]
        ub = u.astype(BF16)
        for s in range(S5_SLABS):
            _store_tiles(bu_ref, SLAB_TILES * s,
                         jnp.dot(ub[:, 128 * s:128 * (s + 1)], bmat_ref[s], preferred_element_type=F32))

        def strip(j, c):
            off, tr, ti = _strip_tiles(j)
            cols_r, cols_i = pl.ds(pl.multiple_of(tr * STRIP, STRIP), STRIP), pl.ds(pl.multiple_of(ti * STRIP, STRIP), STRIP)
            er, ei = _scan_cols((bu_ref.at[tr], bu_ref.at[ti]), carry_ref[0:1, cols_r], carry_ref[0:1, cols_i],
                                ptab_ref, off, True)
            carry_ref[0:1, cols_r] = er
            carry_ref[0:1, cols_i] = ei
            return c

        lax.fori_loop(0, S5_N // STRIP, strip, 0)
        y = jnp.concatenate(
            [jnp.dot(_load_tiles(bu_ref, SLAB_TILES * s, SLAB_TILES).astype(BF16), cmat_ref[s], preferred_element_type=F32)
             for s in range(S5_SLABS)], axis=1) + d_ref[...] * u
        y_ref[...] = y
        g = _gelu(y)
        zz = jnp.dot(g.astype(BF16), wglu_ref[...], preferred_element_type=F32) + bglu_ref[...]
        out_ref[...] = (g * _sigmoid(zz)).astype(BF16)

    const = lambda shape: pl.BlockSpec(shape, lambda i: (0,) * len(shape))
    row_spec = pl.BlockSpec((t, MIX), lambda i: (i, 0))
    return pl.pallas_call(
        body, name=name, grid=(nb,),
        in_specs=[row_spec, const((S5_SLABS, 128, SLAB_COLS)), const((S5_SLABS, SLAB_COLS, 128)), const((1, MIX)),
                  const((MIX, MIX)), const((1, MIX)), const((PTAB_ROWS, S5_N))],
        out_specs=[row_spec, row_spec, pl.BlockSpec((1, 1, 2 * S5_N), lambda i: (i, 0, 0))],
        out_shape=[jax.ShapeDtypeStruct((m, MIX), BF16), jax.ShapeDtypeStruct((m, MIX), F32),
                   jax.ShapeDtypeStruct((nb, 1, 2 * S5_N), F32)],
        scratch_shapes=[pltpu.VMEM((N_TILES, t, STRIP), F32), pltpu.VMEM((1, 2 * S5_N), F32)],
        compiler_params=_cparams(("arbitrary",), VMEM_BIG),
    )(h, bmat, cmat, dvec, wglu, bglu, ptab)


def _s5_bwd(dcat, ypre, h, hb, bmat, cmat, dvec, wglu, bglu, ptab, *, name):
    m = h.shape[0]
    t = T_S5
    nb = m // t

    def body(dya_ref, y_ref, u_ref, hb_ref, bmat_ref, cmat_ref, d_ref, wglu_ref, bglu_ref, ptab_ref,
             du_ref, xb_ref, gb_ref, gq_ref, dzz_ref, dyq_ref, dlam_ref, dbglu_ref, dd_ref,
             bu_ref, dx_ref, gcarry_ref):
        @pl.when(pl.program_id(0) == 0)
        def _():
            gcarry_ref[...] = jnp.zeros_like(gcarry_ref)
            dlam_ref[...] = jnp.zeros_like(dlam_ref)
            dbglu_ref[...] = jnp.zeros_like(dbglu_ref)
            dd_ref[...] = jnp.zeros_like(dd_ref)

        u = u_ref[...]
        y = y_ref[...]
        g = _gelu(y)
        gq = g.astype(BF16)
        sg = _sigmoid(jnp.dot(gq, wglu_ref[...], preferred_element_type=F32) + bglu_ref[...])
        dout = dya_ref[...]
        dzz = dout * g * sg * (1.0 - sg)
        dzzq = dzz.astype(BF16)
        dg = dout * sg + lax.dot_general(dzzq, wglu_ref[...], (((1,), (1,)), ((), ())), preferred_element_type=F32)
        dy = dg * _gelu_grad(y)
        dyq = dy.astype(BF16)
        gq_ref[...] = gq
        dzz_ref[...] = dzzq
        dyq_ref[...] = dyq
        dbglu_ref[...] += jnp.sum(dzz, axis=0, keepdims=True)
        dd_ref[...] += jnp.sum(dy * u, axis=0, keepdims=True)

        ub = u.astype(BF16)
        nt = (((1,), (1,)), ((), ()))
        for s in range(S5_SLABS):
            _store_tiles(dx_ref, SLAB_TILES * s,
                         lax.dot_general(dyq[:, 128 * s:128 * (s + 1)], cmat_ref[s], nt, preferred_element_type=F32))
            _store_tiles(bu_ref, SLAB_TILES * s,
                         jnp.dot(ub[:, 128 * s:128 * (s + 1)], bmat_ref[s], preferred_element_type=F32))
        grow = lax.broadcasted_iota(jnp.int32, (GROUPS, STRIP), 0)

        def strip(j, c):
            off, tr, ti = _strip_tiles(j)
            cols_r, cols_i = pl.ds(pl.multiple_of(tr * STRIP, STRIP), STRIP), pl.ds(pl.multiple_of(ti * STRIP, STRIP), STRIP)
            x_r, x_i = bu_ref.at[tr], bu_ref.at[ti]
            hr = hb_ref[0, 0:1, cols_r]
            hi = hb_ref[0, 0:1, cols_i]
            _scan_cols((x_r, x_i), hr, hi, ptab_ref, off, True)
            xb_ref[:, cols_r] = x_r[...].astype(BF16)
            xb_ref[:, cols_i] = x_i[...].astype(BF16)
            sums = [jnp.zeros((1, STRIP), F32), jnp.zeros((1, STRIP), F32)]

            def d_lam(r, gr, gi):
                if r == 0:
                    pr_ = jnp.where(grow == 0, hr, pltpu.roll(x_r[pl.ds(7, GROUPS, stride=8), :], 1, 0))
                    pi_ = jnp.where(grow == 0, hi, pltpu.roll(x_i[pl.ds(7, GROUPS, stride=8), :], 1, 0))
                else:
                    pr_ = x_r[pl.ds(r - 1, GROUPS, stride=8), :]
                    pi_ = x_i[pl.ds(r - 1, GROUPS, stride=8), :]
                sums[0] = sums[0] + jnp.sum(pr_ * gr + pi_ * gi, axis=0, keepdims=True)
                sums[1] = sums[1] + jnp.sum(pr_ * gi - pi_ * gr, axis=0, keepdims=True)

            g_r, g_i = dx_ref.at[tr], dx_ref.at[ti]
            gr0, gi0 = _scan_cols((g_r, g_i), gcarry_ref[0:1, cols_r], gcarry_ref[0:1, cols_i], ptab_ref, off, False, d_lam)
            gb_ref[:, cols_r] = g_r[...].astype(BF16)
            gb_ref[:, cols_i] = g_i[...].astype(BF16)
            gcarry_ref[0:1, cols_r] = gr0
            gcarry_ref[0:1, cols_i] = gi0
            dlam_ref[0:1, pl.ds(off, STRIP)] += sums[0]
            dlam_ref[1:2, pl.ds(off, STRIP)] += sums[1]
            return c

        lax.fori_loop(0, S5_N // STRIP, strip, 0)
        du_ref[...] = dy * d_ref[...] + jnp.concatenate(
            [lax.dot_general(gb_ref[:, SLAB_COLS * s:SLAB_COLS * (s + 1)], bmat_ref[s], nt, preferred_element_type=F32)
             for s in range(S5_SLABS)], axis=1)

    const = lambda shape: pl.BlockSpec(shape, lambda i: (0,) * len(shape))
    rev = lambda i: (nb - 1 - i, 0)
    row_spec = pl.BlockSpec((t, MIX), rev)
    wide = pl.BlockSpec((t, 2 * S5_N), rev)
    return pl.pallas_call(
        body, name=name, grid=(nb,),
        in_specs=[row_spec, row_spec, row_spec, pl.BlockSpec((1, 1, 2 * S5_N), lambda i: (nb - 1 - i, 0, 0)),
                  const((S5_SLABS, 128, SLAB_COLS)), const((S5_SLABS, SLAB_COLS, 128)), const((1, MIX)), const((MIX, MIX)),
                  const((1, MIX)), const((PTAB_ROWS, S5_N))],
        out_specs=[row_spec, wide, wide, row_spec, row_spec, row_spec, const((2, S5_N)), const((1, MIX)), const((1, MIX))],
        out_shape=[jax.ShapeDtypeStruct((m, MIX), F32), jax.ShapeDtypeStruct((m, 2 * S5_N), BF16),
                   jax.ShapeDtypeStruct((m, 2 * S5_N), BF16), jax.ShapeDtypeStruct((m, MIX), BF16),
                   jax.ShapeDtypeStruct((m, MIX), BF16), jax.ShapeDtypeStruct((m, MIX), BF16),
                   jax.ShapeDtypeStruct((2, S5_N), F32), jax.ShapeDtypeStruct((1, MIX), F32), jax.ShapeDtypeStruct((1, MIX), F32)],
        scratch_shapes=[pltpu.VMEM((N_TILES, t, STRIP), F32), pltpu.VMEM((N_TILES, t, STRIP), F32),
                        pltpu.VMEM((1, 2 * S5_N), F32)],
        compiler_params=_cparams(("arbitrary",), VMEM_BIG),
    )(dcat, ypre, h, hb, bmat, cmat, dvec, wglu, bglu, ptab)


HALO = 8


def _taps_down(zext, t):
    return pltpu.roll(zext, 1, 0)[HALO:HALO + t], pltpu.roll(zext, 2, 0)[HALO:HALO + t]


def _conv_z(c_ref, x_ref, cp_ref, xp_ref, first, t):
    z = c_ref[...] * x_ref[...]
    zp = jnp.where(first, 0.0, cp_ref[t - HALO:t, :] * xp_ref[t - HALO:t, :])
    z1, z2 = _taps_down(jnp.concatenate([zp, z], axis=0), t)
    return z, z1, z2


def _conv_fwd(h, cw, *, name):
    m = h.shape[0]
    t = TM
    nb = m // t

    def body(b_ref, c_ref, x_ref, cp_ref, xp_ref, w_ref, o_ref):
        z, z1, z2 = _conv_z(c_ref, x_ref, cp_ref, xp_ref, pl.program_id(0) == 0, t)
        o_ref[...] = (b_ref[...] * (w_ref[0:1, :] * z2 + w_ref[1:2, :] * z1 + w_ref[2:3, :] * z)).astype(BF16)

    cur = lambda cb: pl.BlockSpec((t, MIX), lambda i: (i, cb))
    prev = lambda cb: pl.BlockSpec((t, MIX), lambda i: (jnp.maximum(i - 1, 0), cb))
    return pl.pallas_call(
        body, name=name, grid=(nb,),
        in_specs=[cur(1), cur(2), cur(3), prev(2), prev(3), pl.BlockSpec((3, MIX), lambda i: (0, 0))],
        out_specs=pl.BlockSpec((t, MIX), lambda i: (i, 0)),
        out_shape=jax.ShapeDtypeStruct((m, MIX), BF16),
        compiler_params=_cparams(("parallel",)),
    )(h, h, h, h, h, cw)


def _conv_bwd(dcat, h, cw, *, name):
    m = h.shape[0]
    t = TM
    nb = m // t

    def body(dy_ref, dyn_ref, b_ref, c_ref, x_ref, cp_ref, xp_ref, bn_ref, w_ref, o_ref, dw_ref):
        i = pl.program_id(0)

        @pl.when(i == 0)
        def _():
            dw_ref[...] = jnp.zeros_like(dw_ref)

        z, z1, z2 = _conv_z(c_ref, x_ref, cp_ref, xp_ref, i == 0, t)
        w0, w1, w2 = w_ref[0:1, :], w_ref[1:2, :], w_ref[2:3, :]
        dy = dy_ref[...]
        dconv = dy * b_ref[...]
        dnext = jnp.where(i == nb - 1, 0.0, dyn_ref[0:HALO, :] * bn_ref[0:HALO, :])
        dext = jnp.concatenate([dconv, dnext], axis=0)
        d1 = pltpu.roll(dext, t + HALO - 1, 0)[0:t]
        d2 = pltpu.roll(dext, t + HALO - 2, 0)[0:t]
        dz = w2 * dconv + w1 * d1 + w0 * d2
        o_ref[:, 0:MIX] = dy * (w0 * z2 + w1 * z1 + w2 * z)
        o_ref[:, MIX:2 * MIX] = dz * x_ref[...]
        o_ref[:, 2 * MIX:3 * MIX] = dz * c_ref[...]
        dw_ref[0:1, :] += jnp.sum(dconv * z2, axis=0, keepdims=True)
        dw_ref[1:2, :] += jnp.sum(dconv * z1, axis=0, keepdims=True)
        dw_ref[2:3, :] += jnp.sum(dconv * z, axis=0, keepdims=True)

    cur = lambda cb: pl.BlockSpec((t, MIX), lambda i: (i, cb))
    prev = lambda cb: pl.BlockSpec((t, MIX), lambda i: (jnp.maximum(i - 1, 0), cb))
    nxt = lambda cb: pl.BlockSpec((t, MIX), lambda i: (jnp.minimum(i + 1, nb - 1), cb))
    return pl.pallas_call(
        body, name=name, grid=(nb,),
        in_specs=[cur(1), nxt(1), cur(1), cur(2), cur(3), prev(2), prev(3), nxt(1), pl.BlockSpec((3, MIX), lambda i: (0, 0))],
        out_specs=[pl.BlockSpec((t, 3 * MIX), lambda i: (i, 0)), pl.BlockSpec((8, MIX), lambda i: (0, 0))],
        out_shape=[jax.ShapeDtypeStruct((m, 3 * MIX), F32), jax.ShapeDtypeStruct((8, MIX), F32)],
        compiler_params=_cparams(("arbitrary",)),
    )(dcat, dcat, h, h, h, h, h, h, cw)


PHALO = 16


def _pool_pooled(z_ref, zp_ref, i, t):
    z = z_ref[...]
    zp = jnp.where(i == 0, 0.0, zp_ref[t - PHALO:t, :])
    s = jnp.concatenate([zp, z], axis=0)
    sums = {}
    width = 1
    while width < PHALO:
        s = s + pltpu.roll(s, width, 0)
        width *= 2
        sums[width] = s[PHALO:PHALO + t]
    tpos = i * t + lax.broadcasted_iota(jnp.int32, (t, 1), 0)
    outs = []
    for gi, w in enumerate(POOL_WINDOWS):
        lo = gi * POOL_GROUP
        count = jnp.minimum(tpos + 1, w).astype(F32)
        outs.append(sums[w][:, lo:lo + POOL_GROUP] / count - z[:, lo:lo + POOL_GROUP])
    return outs


def _pool_fwd(h, pw, ps, *, name):
    m = h.shape[0]
    t = TM
    nb = m // t

    def body(z_ref, zp_ref, pw_ref, ps_ref, o_ref):
        pooled = _pool_pooled(z_ref, zp_ref, pl.program_id(0), t)
        for gi in range(len(POOL_WINDOWS)):
            lo = gi * POOL_GROUP
            mixed = jnp.dot(pooled[gi].astype(BF16), pw_ref[gi], preferred_element_type=F32)
            o_ref[:, lo:lo + POOL_GROUP] = (mixed * ps_ref[:, lo:lo + POOL_GROUP]).astype(BF16)

    return pl.pallas_call(
        body, name=name, grid=(nb,),
        in_specs=[pl.BlockSpec((t, MIX), lambda i: (i, 3)), pl.BlockSpec((t, MIX), lambda i: (jnp.maximum(i - 1, 0), 3)),
                  pl.BlockSpec((4, POOL_GROUP, POOL_GROUP), lambda i: (0, 0, 0)), pl.BlockSpec((1, MIX), lambda i: (0, 0))],
        out_specs=pl.BlockSpec((t, MIX), lambda i: (i, 0)),
        out_shape=jax.ShapeDtypeStruct((m, MIX), BF16),
        compiler_params=_cparams(("parallel",)),
    )(h, h, pw, ps)


def _pool_bwd(dcat, h, pw, ps, *, name):
    m = h.shape[0]
    t = TM
    nb = m // t

    def body(dy_ref, dyn_ref, z_ref, zp_ref, pw_ref, ps_ref, dz_ref, dpw_ref, dps_ref):
        i = pl.program_id(0)

        @pl.when(i == 0)
        def _():
            dpw_ref[...] = jnp.zeros_like(dpw_ref)
            dps_ref[...] = jnp.zeros_like(dps_ref)

        pooled = _pool_pooled(z_ref, zp_ref, i, t)
        dy = dy_ref[...]
        tpos = i * t + lax.broadcasted_iota(jnp.int32, (t, 1), 0)
        for gi, w in enumerate(POOL_WINDOWS):
            lo = gi * POOL_GROUP
            sl = slice(lo, lo + POOL_GROUP)
            pq = pooled[gi].astype(BF16)
            mixed = jnp.dot(pq, pw_ref[gi], preferred_element_type=F32)
            dps_ref[:, sl] += jnp.sum(dy[:, sl] * mixed, axis=0, keepdims=True)
            dmix = (dy[:, sl] * ps_ref[:, sl]).astype(BF16)
            dpw_ref[gi] += lax.dot_general(pq, dmix, (((0,), (0,)), ((), ())), preferred_element_type=F32)
            dpool = lax.dot_general(dmix, pw_ref[gi], (((1,), (1,)), ((), ())), preferred_element_type=F32)
            dmix_n = (dyn_ref[0:PHALO, sl] * ps_ref[:, sl]).astype(BF16)
            dpool_n = lax.dot_general(dmix_n, pw_ref[gi], (((1,), (1,)), ((), ())), preferred_element_type=F32)
            e = dpool / jnp.minimum(tpos + 1, w).astype(F32)
            e_n = jnp.where(i == nb - 1, 0.0, dpool_n * (1.0 / w))
            f = jnp.concatenate([e, e_n], axis=0)
            width = 1
            while width < w:
                f = f + pltpu.roll(f, t + PHALO - width, 0)
                width *= 2
            dz_ref[:, sl] = f[0:t] - dpool

    return pl.pallas_call(
        body, name=name, grid=(nb,),
        in_specs=[pl.BlockSpec((t, MIX), lambda i: (i, 1)), pl.BlockSpec((t, MIX), lambda i: (jnp.minimum(i + 1, nb - 1), 1)),
                  pl.BlockSpec((t, MIX), lambda i: (i, 3)), pl.BlockSpec((t, MIX), lambda i: (jnp.maximum(i - 1, 0), 3)),
                  pl.BlockSpec((4, POOL_GROUP, POOL_GROUP), lambda i: (0, 0, 0)), pl.BlockSpec((1, MIX), lambda i: (0, 0))],
        out_specs=[pl.BlockSpec((t, MIX), lambda i: (i, 0)), pl.BlockSpec((4, POOL_GROUP, POOL_GROUP), lambda i: (0, 0, 0)),
                   pl.BlockSpec((1, MIX), lambda i: (0, 0))],
        out_shape=[jax.ShapeDtypeStruct((m, MIX), F32), jax.ShapeDtypeStruct((4, POOL_GROUP, POOL_GROUP), F32),
                   jax.ShapeDtypeStruct((1, MIX), F32)],
        compiler_params=_cparams(("arbitrary",)),
    )(dcat, dcat, h, h, pw, ps)


NKEY = 2 * T_ATT


def _band_mask():
    qc = np.arange(T_ATT)[:, None] // CHUNK
    kc = np.arange(NKEY)[None, :] // CHUNK - LEFT_CHUNKS
    return np.where((kc <= qc) & (kc >= qc - LEFT_CHUNKS), 0.0, NEG_INF).astype(np.float32)


def _diag_index():
    c = np.arange(NKEY)
    d = np.where(c <= NKEY // 2 + CHUNK, T_ATT - c, T_ATT + NKEY - c)
    return np.clip(d, -MAX_REL, MAX_REL) + MAX_REL


def _bias_tile(vd_ref, mask_ref, tile_ref):
    col = lax.broadcasted_iota(jnp.int32, (8, NKEY), 1)
    no_prev = jnp.where(col < T_ATT, NEG_INF, 0.0)
    for hh in range(2):
        v = vd_ref[0, hh:hh + 1, :]
        base = jnp.concatenate([v if s == 0 else pltpu.roll(v, s, 1) for s in range(8)], axis=0)
        for mrow in range(T_ATT // 8):
            rows = slice(8 * mrow, 8 * mrow + 8)
            blk = (base if mrow == 0 else pltpu.roll(base, 8 * mrow, 1)) + mask_ref[rows, :]
            tile_ref[hh, rows, :] = blk
            tile_ref[2 + hh, rows, :] = blk + no_prev


BAND_ROWS = 2 * CHUNK
BAND_COLS = (LEFT_CHUNKS + 2) * CHUNK
N_BANDS = T_ATT // BAND_ROWS


def _band(x, r):
    return x[BAND_ROWS * r:BAND_ROWS * (r + 1), BAND_ROWS * r:BAND_ROWS * r + BAND_COLS]


def _from_bands(parts):
    rows = []
    for r, part in enumerate(parts):
        right = NKEY - BAND_COLS - BAND_ROWS * r
        pieces = ([jnp.zeros((BAND_ROWS, BAND_ROWS * r), part.dtype)] if r else []) + [part]
        pieces += [jnp.zeros((BAND_ROWS, right), part.dtype)] if right else []
        rows.append(jnp.concatenate(pieces, axis=1))
    return jnp.concatenate(rows, axis=0)


def _attn_probs(q, kc, tile_ref, idx):
    s = lax.dot_general(q, kc, (((1,), (1,)), ((), ())), preferred_element_type=F32)
    parts = []
    for r in range(N_BANDS):
        sb = _band(s, r) + tile_ref[idx, BAND_ROWS * r:BAND_ROWS * (r + 1), BAND_ROWS * r:BAND_ROWS * r + BAND_COLS]
        p = jnp.exp(sb - jnp.max(sb, axis=-1, keepdims=True))
        parts.append(p * (1.0 / jnp.sum(p, axis=-1, keepdims=True)))
    return parts


def _attn_specs(block):
    cur = lambda base: pl.BlockSpec((T_ATT, 128), lambda hp, i: (block(i), base + hp))
    prev = lambda base: pl.BlockSpec((T_ATT, 128), lambda hp, i: (jnp.maximum(block(i) - 1, 0), base + hp))
    return [cur(0), cur(4), prev(4), cur(8), prev(8),
            pl.BlockSpec((1, 2, NKEY), lambda hp, i: (hp, 0, 0)), pl.BlockSpec((T_ATT, NKEY), lambda hp, i: (0, 0))]


def _attn_fwd(h, vdiag, mask, *, name):
    m = h.shape[0]
    nb = m // T_ATT

    def body(q_ref, k_ref, kp_ref, v_ref, vp_ref, vd_ref, mask_ref, o_ref, tile_ref):
        i = pl.program_id(1)

        @pl.when(i == 0)
        def _():
            _bias_tile(vd_ref, mask_ref, tile_ref)

        first = jnp.where(i == 0, 2, 0)
        outs = []
        for hh in range(2):
            sl = slice(hh * HEAD_DIM, (hh + 1) * HEAD_DIM)
            q = (q_ref[:, sl] * (HEAD_DIM ** -0.5)).astype(BF16)
            kc = jnp.concatenate([kp_ref[:, sl], k_ref[:, sl]], axis=0).astype(BF16)
            vc = jnp.concatenate([vp_ref[:, sl], v_ref[:, sl]], axis=0).astype(BF16)
            p = _from_bands([b.astype(BF16) for b in _attn_probs(q, kc, tile_ref, first + hh)])
            outs.append(jnp.dot(p, vc, preferred_element_type=F32))
        o_ref[...] = jnp.concatenate(outs, axis=1).astype(BF16)

    return pl.pallas_call(
        body, name=name, grid=(ATT_HEADS // 2, nb), in_specs=_attn_specs(lambda i: i),
        out_specs=pl.BlockSpec((T_ATT, 128), lambda hp, i: (i, hp)),
        out_shape=jax.ShapeDtypeStruct((m, MIX), BF16),
        scratch_shapes=[pltpu.VMEM((4, T_ATT, NKEY), F32)],
        compiler_params=_cparams(("parallel", "arbitrary"), VMEM_BIG),
    )(h, h, h, h, h, vdiag, mask)


def _attn_bwd(dcat, h, vdiag, mask, *, name):
    m = h.shape[0]
    nb = m // T_ATT

    def body(do_ref, q_ref, k_ref, kp_ref, v_ref, vp_ref, vd_ref, mask_ref,
             dq_ref, dk_ref, dv_ref, dvd_ref, tile_ref, acc_ref, carry_ref):
        i = pl.program_id(1)

        @pl.when(i == 0)
        def _():
            _bias_tile(vd_ref, mask_ref, tile_ref)
            acc_ref[...] = jnp.zeros_like(acc_ref)

            carry_ref[...] = jnp.zeros_like(carry_ref)

        scale = HEAD_DIM ** -0.5
        first = jnp.where(i == nb - 1, 2, 0)
        dqs, dks, dvs = [], [], []
        for hh in range(2):
            sl = slice(hh * HEAD_DIM, (hh + 1) * HEAD_DIM)
            q = (q_ref[:, sl] * scale).astype(BF16)
            kc = jnp.concatenate([kp_ref[:, sl], k_ref[:, sl]], axis=0).astype(BF16)
            vc = jnp.concatenate([vp_ref[:, sl], v_ref[:, sl]], axis=0).astype(BF16)
            do = do_ref[:, sl].astype(BF16)
            bands = _attn_probs(q, kc, tile_ref, first + hh)
            p = _from_bands([b.astype(BF16) for b in bands])
            dvs.append(lax.dot_general(p, do, (((0,), (0,)), ((), ())), preferred_element_type=F32))
            dp = lax.dot_general(do, vc, (((1,), (1,)), ((), ())), preferred_element_type=F32)
            ds_bands = []
            for r, pb in enumerate(bands):
                dpb = _band(dp, r)
                dsb = pb * (dpb - jnp.sum(dpb * pb, axis=-1, keepdims=True))
                acc_ref[hh, BAND_ROWS * r:BAND_ROWS * (r + 1), BAND_ROWS * r:BAND_ROWS * r + BAND_COLS] += dsb
                ds_bands.append(dsb.astype(BF16))
            dsq = _from_bands(ds_bands)
            dqs.append(jnp.dot(dsq, kc, preferred_element_type=F32) * scale)
            dks.append(lax.dot_general(dsq, q, (((0,), (0,)), ((), ())), preferred_element_type=F32))
        dq_ref[...] = jnp.concatenate(dqs, axis=1)
        dk = jnp.concatenate(dks, axis=1)
        dv = jnp.concatenate(dvs, axis=1)
        dk_ref[...] = dk[T_ATT:] + carry_ref[0]
        dv_ref[...] = dv[T_ATT:] + carry_ref[1]
        carry_ref[0] = dk[:T_ATT]
        carry_ref[1] = dv[:T_ATT]

        @pl.when(i == nb - 1)
        def _():
            for hh in range(2):
                r8 = acc_ref[hh, 0:8, :]
                for mrow in range(1, T_ATT // 8):
                    r8 = r8 + pltpu.roll(acc_ref[hh, 8 * mrow:8 * mrow + 8, :], NKEY - 8 * mrow, 1)
                tot = r8[0:1, :]
                for s in range(1, 8):
                    tot = tot + pltpu.roll(r8[s:s + 1, :], NKEY - s, 1)
                dvd_ref[0, hh:hh + 1, :] = tot

    block = lambda i: nb - 1 - i
    out = pl.BlockSpec((T_ATT, 128), lambda hp, i: (block(i), hp))
    return pl.pallas_call(
        body, name=name, grid=(ATT_HEADS // 2, nb),
        in_specs=[out] + _attn_specs(block),
        out_specs=[out, out, out, pl.BlockSpec((1, 2, NKEY), lambda hp, i: (hp, 0, 0))],
        out_shape=[jax.ShapeDtypeStruct((m, MIX), F32)] * 3 + [jax.ShapeDtypeStruct((ATT_HEADS // 2, 2, NKEY), F32)],
        scratch_shapes=[pltpu.VMEM((4, T_ATT, NKEY), F32), pltpu.VMEM((2, T_ATT, NKEY), F32), pltpu.VMEM((2, T_ATT, 128), F32)],
        compiler_params=_cparams(("parallel", "arbitrary"), VMEM_BIG),
    )(dcat, h, h, h, h, h, vdiag, mask)


def _row_tile(rows):
    for t in (512, 256, 128, 64, 32, 16, 8):
        if rows % t == 0:
            return t
    return rows


def _add_n(arrs, coefs, *, name):
    rows, cols = arrs[0].shape
    t = _row_tile(rows)
    n = len(arrs)

    def body(*refs):
        acc = None
        for r, cf in zip(refs[:n], coefs):
            v = r[...] if cf == 1.0 else cf * r[...]
            acc = v if acc is None else acc + v
        refs[n][...] = acc

    spec = pl.BlockSpec((t, cols), lambda i: (i, 0))
    return pl.pallas_call(
        body, name=name, grid=(rows // t,), in_specs=[spec] * n, out_specs=spec,
        out_shape=jax.ShapeDtypeStruct((rows, cols), F32), compiler_params=_cparams(("parallel",)),
    )(*arrs)


def _adamw(w, g, mom, var, *, name):
    rows, cols = w.shape
    t = _row_tile(rows)

    def body(w_ref, g_ref, m_ref, v_ref, d_ref, mo_ref, vo_ref):
        g_ = g_ref[...]
        m_ = ADAM_B1 * m_ref[...] + (1.0 - ADAM_B1) * g_
        v_ = ADAM_B2 * v_ref[...] + (1.0 - ADAM_B2) * (g_ * g_)
        m_hat = m_ / (1.0 - ADAM_B1 ** ADAM_STEP)
        v_hat = v_ / (1.0 - ADAM_B2 ** ADAM_STEP)
        d_ref[...] = -ADAM_LR * (m_hat / (jnp.sqrt(v_hat) + ADAM_EPS) + ADAM_WD * w_ref[...])
        mo_ref[...] = m_
        vo_ref[...] = v_

    spec = pl.BlockSpec((t, cols), lambda i: (i, 0))
    return pl.pallas_call(
        body, name=name, grid=(rows // t,), in_specs=[spec] * 4, out_specs=[spec] * 3,
        out_shape=[jax.ShapeDtypeStruct((rows, cols), F32)] * 3, compiler_params=_cparams(("parallel",)),
    )(w, g, mom, var)


ANY = pl.BlockSpec(memory_space=pl.ANY)


def _place():
    x, y, c = lax.axis_index("x"), lax.axis_index("y"), lax.axis_index("c")
    chips = [(1 - x, y), (x, 1 - y), (1 - x, 1 - y)]
    return x, y, c, chips


class _GatherExchange:
    def __init__(self, ws):
        n = len(ws)
        self.ins = list(ws)
        self.out_shapes = [jax.ShapeDtypeStruct((N_CHIPS,) + w.shape, w.dtype) for w in ws]
        self.sems = [pltpu.SemaphoreType.DMA((6 * n,)), pltpu.SemaphoreType.DMA((6 * n,))]

    def _copies(self, ins, outs, sems, onward=True):
        send_sems, recv_sems = sems
        x, y, c, chips = _place()
        me = 2 * x + y

        def region(k, j, chip_index, rows, to):
            ref = outs[k].at[chip_index, rows]
            return pltpu.make_async_remote_copy(
                src_ref=ref, dst_ref=ref, send_sem=send_sems.at[6 * k + j], recv_sem=recv_sems.at[6 * k + j],
                device_id=to, device_id_type=MESH)

        first, landed, passed, handed = [], [], [], []
        for k in range(len(ins)):
            half = ins[k].shape[0] // 2
            mine, theirs = pl.ds(c * half, half), pl.ds((1 - c) * half, half)
            for j, chip in enumerate(chips):
                first.append(pltpu.make_async_remote_copy(
                    src_ref=ins[k].at[mine], dst_ref=outs[k].at[me, mine], send_sem=send_sems.at[6 * k + j],
                    recv_sem=recv_sems.at[6 * k + j], device_id=(*chip, c), device_id_type=MESH))
                if onward:
                    landed.append(region(k, j, 2 * chip[0] + chip[1], mine, (*chip, c)))
                    passed.append(region(k, 3 + j, 2 * chip[0] + chip[1], mine, (x, y, 1 - c)))
                    handed.append(region(k, 3 + j, 2 * chip[0] + chip[1], theirs, (x, y, 1 - c)))
        return first, landed, passed, handed

    def start(self, ins, outs, sems):
        for cp in self._copies(ins, outs, sems, onward=False)[0]:
            cp.start()

    def finish(self, ins, outs, sems):
        first, landed, passed, handed = self._copies(ins, outs, sems)
        for arrived, onward in zip(landed, passed):
            arrived.wait_recv()
            onward.start()
        for cp in handed:
            cp.wait_recv()
        for cp in first + passed:
            cp.wait_send()


class _ReduceExchange:
    def __init__(self, grads, axes):
        self.ins = list(grads)
        self.axes = list(axes)
        n = len(grads)
        self.out_shapes = [jax.ShapeDtypeStruct((N_DEV - 1,) + self._block(g, a), g.dtype) for g, a in zip(grads, axes)]
        self.sems = [pltpu.SemaphoreType.DMA((7 * n,)), pltpu.SemaphoreType.DMA((7 * n,))]

    @staticmethod
    def _block(g, axis):
        k, n = g.shape
        return (k // 2, n // N_CHIPS) if axis == 2 else (k // N_DEV, n)

    def _copies(self, ins, outs, sems):
        send_sems, recv_sems = sems
        x, y, c, _ = _place()
        cps = []
        for w, (g, axis) in enumerate(zip(ins, self.axes)):
            rows, cols = self._block(g, axis)
            for k in range(1, N_DEV):
                tx, ty, tc = (1 - x if k & 4 else x), (1 - y if k & 2 else y), (1 - c if k & 1 else c)
                chip = 2 * tx + ty
                if axis == 2:
                    src = g.at[pl.ds(tc * rows, rows), pl.ds(chip * cols, cols)]
                else:
                    src = g.at[pl.ds((2 * chip + tc) * rows, rows), :]
                cps.append(pltpu.make_async_remote_copy(
                    src_ref=src, dst_ref=outs[w].at[k - 1], send_sem=send_sems.at[7 * w + k - 1],
                    recv_sem=recv_sems.at[7 * w + k - 1], device_id=(tx, ty, tc), device_id_type=MESH))
        return cps

    def start(self, ins, outs, sems):
        for cp in self._copies(ins, outs, sems):
            cp.start()

    def finish(self, ins, outs, sems):
        cps = self._copies(ins, outs, sems)
        for cp in cps:
            cp.wait_recv()
        for cp in cps:
            cp.wait_send()


def _run_exchange(ex, *, name):
    n_in, n_out = len(ex.ins), len(ex.out_shapes)

    def body(*refs):
        ins, outs, sems = refs[:n_in], refs[n_in:n_in + n_out], refs[n_in + n_out:]
        ex.start(ins, outs, sems)
        ex.finish(ins, outs, sems)

    return pl.pallas_call(body, name=name, in_specs=[ANY] * n_in, out_specs=[ANY] * n_out, out_shape=ex.out_shapes,
                          scratch_shapes=ex.sems)(*ex.ins)


def _all_reduce_small(buf, *, name):
    rows = buf.shape[0]

    def body(x_ref, sum_ref, all_ref, send_sems, recv_sems, local_sem):
        x, y, c, chips = _place()
        me, sibling = (x, y, c), (x, y, 1 - c)

        def slab(px, py, pc):
            return all_ref.at[pl.ds((4 * px + 2 * py + pc) * rows, rows), :]

        def copy(k, block, to, src=None):
            return pltpu.make_async_remote_copy(
                src_ref=slab(*block) if src is None else src, dst_ref=slab(*block), send_sem=send_sems.at[k],
                recv_sem=recv_sems.at[k], device_id=to, device_id_type=MESH)

        mine = pltpu.make_async_copy(x_ref, slab(*me), local_sem)
        mine.start()
        first = [copy(0, me, sibling, src=x_ref)]
        first += [copy(1 + j, me, (*chip, c), src=x_ref) for j, chip in enumerate(chips)]
        for cp in first:
            cp.start()
        passed = [copy(4 + j, (*chip, c), sibling) for j, chip in enumerate(chips)]
        for j, chip in enumerate(chips):
            copy(1 + j, (*chip, c), me).wait_recv()
            passed[j].start()
        copy(0, sibling, me).wait_recv()
        for j, chip in enumerate(chips):
            copy(4 + j, (*chip, 1 - c), me).wait_recv()
        for cp in first + passed:
            cp.wait_send()
        mine.wait()
        acc = all_ref[0:rows, :]
        for d in range(1, N_DEV):
            acc = acc + all_ref[d * rows:(d + 1) * rows, :]
        sum_ref[...] = acc

    vmem = pl.BlockSpec(memory_space=pltpu.VMEM)
    return pl.pallas_call(
        body, name=name, in_specs=[vmem], out_specs=[vmem, vmem],
        out_shape=[jax.ShapeDtypeStruct((rows, 128), F32), jax.ShapeDtypeStruct((N_DEV * rows, 128), F32)],
        scratch_shapes=[pltpu.SemaphoreType.DMA((7,)), pltpu.SemaphoreType.DMA((7,)), pltpu.SemaphoreType.DMA],
        compiler_params=pltpu.CompilerParams(vmem_limit_bytes=VMEM_BIG),
    )(buf)[0]


WEIGHTS = ['ev_w_in', 'ev_lambda_re', 'ev_lambda_im', 'ev_log_dt', 'ev_b_re', 'ev_b_im', 'ev_c_re', 'ev_c_im', 'ev_d',
           'ev_w_glu', 'ev_b_glu', 'ev_conv_w', 'ev_w_out', 'od_w_in', 'od_rel_bias', 'od_pool_w', 'od_pool_scale',
           'od_w_out', 'ln_mix_g', 'ln_mix_b', 'ln_ffn_g', 'ln_ffn_b', 'ffn_w_up', 'ffn_w_down', 'ple_w_proj',
           'ple_w_gate', 'ple_b_gate']
INPUTS = ['x', 'p'] + WEIGHTS + ['loss_target'] + ['m_' + n for n in WEIGHTS] + ['v_' + n for n in WEIGHTS]

BIG = {
    'ev_w_in': (2, (2, 1024, 2048)), 'ev_w_glu': (1, (2, 512, 512)), 'ev_w_out': (1, (2, 1024, 1024)),
    'od_w_in': (2, (2, 1024, 2048)), 'od_w_out': (1, (2, 1024, 1024)), 'ffn_w_up': (2, (4, 1024, 5632)),
    'ffn_w_down': (1, (4, 2816, 1024)), 'ple_w_proj': (2, (4, 256, 1024)), 'ple_w_gate': (1, (4, 1024, 1024)),
}
SMALL_SHARDED = {'ev_conv_w': (2, 3, 512), 'od_pool_scale': (2, 512)}
REPLICATED = [n for n in WEIGHTS if n not in BIG and n not in SMALL_SHARDED]


def _shard_rows(name):
    axis, (nl, k, n) = BIG[name]
    return (nl * k, n // N_CHIPS) if axis == 2 else (nl * k // N_CHIPS, n)


def _pack(arrs):
    flat = jnp.concatenate([a.reshape(-1) for a in arrs])
    total = flat.shape[0]
    padded = -(-total // 1024) * 1024
    return jnp.pad(flat, (0, padded - total)).reshape(padded // 128, 128)


def _unpack(buf, shapes):
    flat = buf.reshape(-1)
    out, pos = [], 0
    for s in shapes:
        size = int(np.prod(s))
        out.append(flat[pos:pos + size].reshape(s))
        pos += size
    return out


def _s5_params(lam_re, lam_im, log_dt, b_re, b_im, c_re, c_im):
    dt = jnp.exp(log_dt)[:, None]
    mag = jnp.exp(lam_re * dt)
    ang = lam_im * dt
    lb_re = mag * jnp.cos(ang)
    lb_im = mag * jnp.sin(ang)
    den = lam_re * lam_re + lam_im * lam_im
    nr = lb_re - 1.0
    ni = lb_im
    r_re = (nr * lam_re + ni * lam_im) / den
    r_im = (ni * lam_re - nr * lam_im) / den
    bb_re = r_re[..., None] * b_re - r_im[..., None] * b_im
    bb_im = r_re[..., None] * b_im + r_im[..., None] * b_re
    per = S5_GROUPS // S5_SLABS
    eye = jnp.eye(per, dtype=F32)

    def block_diag(a):
        _, r, c = a.shape
        a = a.reshape(S5_SLABS, per, r, c)
        return (a[:, :, :, None, :] * eye[None, :, None, :, None]).reshape(S5_SLABS, per * r, per * c)

    bmat = jnp.concatenate([block_diag(bb_re.transpose(0, 2, 1)), block_diag(bb_im.transpose(0, 2, 1))], axis=2)
    cmat = jnp.concatenate([block_diag(c_re.transpose(0, 2, 1)), block_diag(-c_im.transpose(0, 2, 1))], axis=1)
    lam = jnp.stack([lb_re.reshape(S5_N), lb_im.reshape(S5_N)])
    return lam, bmat, cmat


def _lam_powers(lam):
    res, ims = [lam[0]], [lam[1]]
    for _ in range(7):
        res, ims = res + [res[-1] * lam[0] - ims[-1] * lam[1]], ims + [res[-1] * lam[1] + ims[-1] * lam[0]]
    sq_r, sq_i = [res[-1]], [ims[-1]]
    for _ in range(4):
        sq_r, sq_i = sq_r + [sq_r[-1] * sq_r[-1] - sq_i[-1] * sq_i[-1]], sq_i + [2.0 * sq_r[-1] * sq_i[-1]]
    return jnp.stack(res + ims + res[::-1] + ims[::-1] + sq_r[1:] + sq_i[1:])


def _layer_big(i):
    mixer = [('w_in', 'ev_w_in'), ('w_glu', 'ev_w_glu'), ('w_out', 'ev_w_out')] if i % 2 == 0 else \
        [('w_in', 'od_w_in'), ('w_out', 'od_w_out')]
    ffn = [('w_up', 'ffn_w_up'), ('w_down', 'ffn_w_down'), ('w_proj', 'ple_w_proj'), ('w_gate', 'ple_w_gate')]
    return [(k, n, i // 2) for k, n in mixer] + [(k, n, i) for k, n in ffn]


class _WholePlan:
    def __init__(self, whole):
        self.whole = whole
        self.grads = {n: {} for n in BIG}

    def layer_weights(self, i):
        return {k: self.whole[n][l] for k, n, l in _layer_big(i)}

    def forward_host(self, i):
        return None

    def backward_host(self, i):
        return None

    def layer_grads(self, i, g):
        for k, n, l in _layer_big(i):
            self.grads[n][l] = g[k][0]


def _local_step(x, p, target, w, plan):
    mask = jnp.asarray(_band_mask())
    diag_idx = _diag_index()
    onehot = jnp.asarray(np.eye(2 * MAX_REL + 1, dtype=np.float32)[diag_idx])
    saved = []
    for i in range(DEPTH):
        li = i // 2
        lw = plan.layer_weights(i)
        s = {'x0': x, 'lw': lw}
        h = _mm([(x, 0, D_MODEL)], lw['w_in'], name=f"in_proj")
        if i % 2 == 0:
            (lam, bmat, cmat), s5_vjp = jax.vjp(
                _s5_params, w['ev_lambda_re'][li], w['ev_lambda_im'][li], w['ev_log_dt'][li], w['ev_b_re'][li],
                w['ev_b_im'][li], w['ev_c_re'][li], w['ev_c_im'][li])
            s5c = (bmat.astype(BF16), cmat.astype(BF16), w['ev_d'][li].reshape(1, MIX), lw['w_glu'],
                   w['ev_b_glu'][li].reshape(1, MIX), _lam_powers(lam))
            ya, ypre, hb = _s5_fwd(h, *s5c, name=f"s5_fwd")
            yb = _conv_fwd(h, w['ev_conv_w'][li], name=f"conv_fwd")
            s.update(s5_vjp=s5_vjp, s5c=s5c, ypre=ypre, hb=hb)
        else:
            vdiag = jnp.dot(w['od_rel_bias'][li], onehot.T, precision=HIGHEST).reshape(ATT_HEADS // 2, 2, NKEY)
            pw = w['od_pool_w'][li].astype(BF16)
            ps = w['od_pool_scale'][li].reshape(1, MIX)
            ya = _attn_fwd(h, vdiag, mask, name=f"attn_fwd")
            yb = _pool_fwd(h, pw, ps, name=f"pool_fwd")
            s.update(vdiag=vdiag, pw=pw, ps=ps)
        wout = lw['w_out']
        vec = lambda n: w[n][i].reshape(1, -1)

        def residual_ln(products, rows, vecs):
            r = ALPHA * rows[0] + products[0]
            return (r, _ln_apply(r, vecs[0], vecs[1])), ()

        def embed_gate(products, rows, vecs):
            gate = _sigmoid(products[0] + vecs[0])
            return (rows[0] + gate * products[1], gate, products[1]), ()

        two_f32 = [(D_MODEL, F32), (D_MODEL, F32)]
        r1, x1 = _mm_rows([([(ya, 0, MIX), (yb, 0, MIX)], wout, False)], [x], [vec('ln_mix_g'), vec('ln_mix_b')],
                          two_f32, [], residual_ln, name="out_proj_ln")
        hosted = plan.forward_host(i)
        if hosted is None:
            a, gg, uu = _ffn_up(x1, lw['w_up'], name=f"ffn_up")
        else:
            (a, gg, uu), arrived = _ffn_up(x1, lw['w_up'], exchange=hosted, name=f"ffn_up_gather")
            plan.forward_hosted(i, arrived)
        r2, x2 = _mm_rows([([(a, 0, D_FF)], lw['w_down'], False)], [x1], [vec('ln_ffn_g'), vec('ln_ffn_b')],
                          two_f32, [], residual_ln, name="ffn_down_ln")
        x3, gate, ppb = _mm_rows(
            [([(x2, 0, D_MODEL)], lw['w_gate'], False), ([(p[i], 0, D_PLE)], lw['w_proj'], False)],
            [x2], [vec('ple_b_gate')], [(D_MODEL, F32), (D_MODEL, BF16), (D_MODEL, BF16)], [], embed_gate, name="ple")
        s.update(h=h, ya=ya, yb=yb, r1=r1, x1=x1, a=a, gg=gg, uu=uu, r2=r2, x2=x2, gate=gate, ppb=ppb)
        saved.append(s)
        x = x3

    loss, da = _loss_head(x, target, name="loss_head")
    db = None
    grads = {n: [None] * (DEPTH if n.startswith(('ln_', 'ple_')) else DEPTH // 2) for n in WEIGHTS if n not in BIG}

    def both(pieces, axis):
        return tuple(jnp.concatenate([pc[k] for pc in pieces], axis=axis) for k in range(2))

    for i in reversed(range(DEPTH)):
        li = i // 2
        s = saved[i]
        lw = s['lw']
        big = {}
        dz, dpp, dr2, dbg, dg2, db2 = _ple_ln_bwd(da, db, s['gate'], s['ppb'], s['r2'], lw['w_gate'],
                                                  w['ln_ffn_g'][i].reshape(1, -1), name="ple_ln_bwd")
        grads['ple_b_gate'][i] = dbg.reshape(-1)
        big['w_gate'] = _mm_tn(s['x2'], 0, D_MODEL, dz, also_bf16=True, name=f"d_ple_gate")
        big['w_proj'] = _mm_tn(p[i], 0, D_PLE, dpp, also_bf16=True, name=f"d_ple_proj")
        grads['ln_ffn_g'][i] = dg2.reshape(-1)
        grads['ln_ffn_b'][i] = db2.reshape(-1)
        dhh = _ffn_down_bwd(dr2, lw['w_down'], s['gg'], s['uu'], name=f"ffn_down_bwd")
        big['w_down'] = _mm_tn(s['a'], 0, D_FF, dr2, tk=D_FF // 2, also_bf16=True, name=f"d_ffn_down")
        hosted = plan.backward_host(i)
        if hosted is None:
            big['w_up'] = _mm_tn(s['x1'], 0, D_MODEL, dhh, tn=D_FF // 2, also_bf16=True, name=f"d_ffn_up")
        else:
            big['w_up'], arrived = _mm_tn(s['x1'], 0, D_MODEL, dhh, tn=D_FF // 2, also_bf16=True, exchange=hosted,
                                          name=f"d_ffn_up_reduce_{i % 2}")
            plan.backward_hosted(i, arrived)

        def ln_mix_grad(products, rows, vecs):
            dr, dg, dbias = _ln_grad(rows[0], ALPHA * rows[1] + products[0], vecs[0])
            return (dr,), (dg, dbias)

        dr1, dg1, db1 = _mm_rows([([(dhh, 0, 2 * D_FF)], lw['w_up'], True)], [s['r1'], dr2],
                                 [w['ln_mix_g'][i].reshape(1, -1)], [(D_MODEL, F32)], [D_MODEL, D_MODEL], ln_mix_grad,
                                 tm=256, vmem=VMEM_BIG, name="ffn_up_ln_bwd")
        grads['ln_mix_g'][i] = dg1.reshape(-1)
        grads['ln_mix_b'][i] = db1.reshape(-1)
        dcat = _mm([(dr1, 0, D_MODEL)], lw['w_out'], trans_b=True, name=f"out_proj_bwd")
        big['w_out'] = both([_mm_tn(s['ya'], 0, MIX, dr1, also_bf16=True, name=f"d_out_a"),
                             _mm_tn(s['yb'], 0, MIX, dr1, also_bf16=True, name=f"d_out_b")], 0)
        h = s['h']
        if i % 2 == 0:
            s5c = s['s5c']
            du, xb, gb, gq, dzzq, dyq, dlam, dbglu, dd = _s5_bwd(dcat, s['ypre'], h, s['hb'], *s5c, name=f"s5_bwd")
            dbmat = _mm_tn_slabs(h, 128, gb, SLAB_COLS, S5_SLABS, name=f"d_s5_b")
            dcmat = _mm_tn_slabs(xb, SLAB_COLS, dyq, 128, S5_SLABS, name=f"d_s5_c")
            s5g = s['s5_vjp']((dlam, dbmat, dcmat))
            for n, g_ in zip(['ev_lambda_re', 'ev_lambda_im', 'ev_log_dt', 'ev_b_re', 'ev_b_im', 'ev_c_re', 'ev_c_im'], s5g):
                grads[n][li] = g_
            big['w_glu'] = _mm_tn(gq, 0, MIX, dzzq, also_bf16=True, name=f"d_glu")
            grads['ev_b_glu'][li] = dbglu.reshape(-1)
            grads['ev_d'][li] = dd.reshape(-1)
            d3, dcw = _conv_bwd(dcat, h, w['ev_conv_w'][li], name=f"conv_bwd")
            grads['ev_conv_w'][li] = dcw[0:3]
            big['w_in'] = both([_mm_tn(s['x0'], 0, D_MODEL, du, also_bf16=True, name=f"d_in_a"),
                                _mm_tn(s['x0'], 0, D_MODEL, d3, tn=3 * MIX, also_bf16=True, name=f"d_in_b")], 1)
            db = _mm([(du, 0, MIX), (d3, 0, 3 * MIX)], lw['w_in'], trans_b=True, name=f"in_proj_bwd")
        else:
            dq, dk, dv, dvd = _attn_bwd(dcat, h, s['vdiag'], mask, name=f"attn_bwd")
            dzp, dpw, dps = _pool_bwd(dcat, h, s['pw'], s['ps'], name=f"pool_bwd")
            parts = [dq, dk, dv, dzp]
            grads['od_rel_bias'][li] = jnp.dot(dvd.reshape(ATT_HEADS, NKEY), onehot, precision=HIGHEST)
            grads['od_pool_w'][li] = dpw
            grads['od_pool_scale'][li] = dps.reshape(-1)
            big['w_in'] = both([_mm_tn(s['x0'], 0, D_MODEL, d_, also_bf16=True, name=f"d_in_a") for d_ in parts], 1)
            db = _mm([(d_, 0, MIX) for d_ in parts], lw['w_in'], trans_b=True, name=f"in_proj_bwd")
        plan.layer_grads(i, big)
        da = dr1
    grad_x = _add_n([da, db], [ALPHA, 1.0], name="grad_x")
    return loss, grad_x, {n: jnp.stack(g) for n, g in grads.items()}


def _sum_blocks(own, others, *, name):
    rows, cols = own.shape
    t = _row_tile(rows)

    def body(own_ref, others_ref, o_ref):
        acc = own_ref[...]
        for k in range(N_DEV - 1):
            acc = acc + others_ref[k].astype(F32)
        o_ref[...] = acc

    return pl.pallas_call(
        body, name=name, grid=(rows // t,),
        in_specs=[pl.BlockSpec((t, cols), lambda i: (i, 0)), pl.BlockSpec((N_DEV - 1, t, cols), lambda i: (0, i, 0))],
        out_specs=pl.BlockSpec((t, cols), lambda i: (i, 0)), out_shape=jax.ShapeDtypeStruct((rows, cols), F32),
        compiler_params=_cparams(("parallel",)),
    )(own, others)


def _swap_sibling(arrs, *, name):
    n = len(arrs)

    def body(*refs):
        ins, outs = refs[:n], refs[n:2 * n]
        send_sems, recv_sems = refs[2 * n:]
        x, y, c, _ = _place()
        cps = [pltpu.make_async_remote_copy(src_ref=ins[k], dst_ref=outs[k], send_sem=send_sems.at[k], recv_sem=recv_sems.at[k],
                                            device_id=(x, y, 1 - c), device_id_type=MESH) for k in range(n)]
        for cp in cps:
            cp.start()
        for cp in cps:
            cp.wait_recv()
        for cp in cps:
            cp.wait_send()

    return pl.pallas_call(
        body, name=name, in_specs=[ANY] * n, out_specs=[ANY] * n,
        out_shape=[jax.ShapeDtypeStruct(a.shape, a.dtype) for a in arrs],
        scratch_shapes=[pltpu.SemaphoreType.DMA((n,)), pltpu.SemaphoreType.DMA((n,))],
    )(*arrs)


class _ShardedPlan:
    def __init__(self, a, c, me):
        self.a, self.c, self.me = a, c, me
        self.weights, self.pending, self.own, self.arrived = {}, None, {}, {}

    def _shards(self, i):
        return [self.a[n][l].astype(BF16) for _, n, l in _layer_big(i)]

    def _with_own(self, gathered, own):
        return lax.dynamic_update_index_in_dim(gathered, own, self.me, 0)

    def _set_weights(self, i, gathered):
        lw = {}
        for (k, n, _), g, own in zip(_layer_big(i), gathered, self._shards(i)):
            _, rows, cols = g.shape
            g = self._with_own(g, own)
            lw[k] = g.transpose(1, 0, 2).reshape(rows, N_CHIPS * cols) if BIG[n][0] == 2 else g.reshape(N_CHIPS * rows, cols)
        self.weights[i] = lw

    def gather_first(self, misc):
        gathered = _run_exchange(_GatherExchange(self._shards(0) + [misc]), name="weight_gather_0")
        self._set_weights(0, gathered[:-1])
        return self._with_own(gathered[-1], misc)

    def layer_weights(self, i):
        return self.weights.pop(i)

    def forward_host(self, i):
        return _GatherExchange(self._shards(i + 1)) if i + 1 < DEPTH else None

    def forward_hosted(self, i, arrived):
        self._set_weights(i + 1, arrived)

    def _reduce_exchange(self):
        i, g = self.pending
        return _ReduceExchange([g[k][1] for k, _, _ in _layer_big(i)], [BIG[n][0] for _, n, _ in _layer_big(i)])

    def layer_grads(self, i, g):
        for k, n, l in _layer_big(i):
            full = g[k][0]
            kk, nn = full.shape
            if BIG[n][0] == 2:
                self.own[n, l] = lax.dynamic_slice(full, (self.c * (kk // 2), self.me * (nn // N_CHIPS)), (kk // 2, nn // N_CHIPS))
            else:
                self.own[n, l] = lax.dynamic_slice_in_dim(full, (2 * self.me + self.c) * (kk // N_DEV), kk // N_DEV, axis=0)
        self.pending = (i, g)

    def backward_host(self, i):
        return self._reduce_exchange() if i + 1 < DEPTH else None

    def backward_hosted(self, i, arrived):
        for (_, n, l), r in zip(_layer_big(i + 1), arrived):
            self.arrived[n, l] = r

    def reduced(self):
        for (_, n, l), r in zip(_layer_big(0), _run_exchange(self._reduce_exchange(), name="grad_reduce_0")):
            self.arrived[n, l] = r
        keys = [(n, l) for n in BIG for l in range(BIG[n][1][0])]
        mine = [_sum_blocks(self.own[k], self.arrived[k], name=f"grad_sum_{k[0]}") for k in keys]
        theirs = _swap_sibling(mine, name="grad_half_swap")
        out = {}
        for n in BIG:
            layers = []
            for l in range(BIG[n][1][0]):
                a_, b_ = mine[keys.index((n, l))], theirs[keys.index((n, l))]
                layers.append(jnp.where(self.c == 0, jnp.concatenate([a_, b_], axis=0), jnp.concatenate([b_, a_], axis=0)))
            out[n] = jnp.stack(layers)
        return out


def kernel(x, p, ev_w_in, ev_lambda_re, ev_lambda_im, ev_log_dt, ev_b_re, ev_b_im, ev_c_re, ev_c_im, ev_d, ev_w_glu, ev_b_glu, ev_conv_w, ev_w_out, od_w_in, od_rel_bias, od_pool_w, od_pool_scale, od_w_out, ln_mix_g, ln_mix_b, ln_ffn_g, ln_ffn_b, ffn_w_up, ffn_w_down, ple_w_proj, ple_w_gate, ple_b_gate, loss_target, m_ev_w_in, m_ev_lambda_re, m_ev_lambda_im, m_ev_log_dt, m_ev_b_re, m_ev_b_im, m_ev_c_re, m_ev_c_im, m_ev_d, m_ev_w_glu, m_ev_b_glu, m_ev_conv_w, m_ev_w_out, m_od_w_in, m_od_rel_bias, m_od_pool_w, m_od_pool_scale, m_od_w_out, m_ln_mix_g, m_ln_mix_b, m_ln_ffn_g, m_ln_ffn_b, m_ffn_w_up, m_ffn_w_down, m_ple_w_proj, m_ple_w_gate, m_ple_b_gate, v_ev_w_in, v_ev_lambda_re, v_ev_lambda_im, v_ev_log_dt, v_ev_b_re, v_ev_b_im, v_ev_c_re, v_ev_c_im, v_ev_d, v_ev_w_glu, v_ev_b_glu, v_ev_conv_w, v_ev_w_out, v_od_w_in, v_od_rel_bias, v_od_pool_w, v_od_pool_scale, v_od_w_out, v_ln_mix_g, v_ln_mix_b, v_ln_ffn_g, v_ln_ffn_b, v_ffn_w_up, v_ffn_w_down, v_ple_w_proj, v_ple_w_gate, v_ple_b_gate):
    given = locals()
    a = {n: given[n] for n in INPUTS}
    x, y, c = lax.axis_index("x"), lax.axis_index("y"), lax.axis_index("c")
    me = 2 * x + y

    plan = _ShardedPlan(a, c, me)
    misc = jnp.concatenate([a['ev_conv_w'].reshape(6, 128), a['od_pool_scale'], jnp.zeros((8, 128), F32)], axis=0)
    gm = plan.gather_first(misc)
    w = {n: a[n] for n in REPLICATED}
    w['ev_conv_w'] = gm[:, 0:6].reshape(N_CHIPS, 2, 3, 128).transpose(1, 2, 0, 3).reshape(2, 3, 512)
    w['od_pool_scale'] = gm[:, 6:8].transpose(1, 0, 2).reshape(2, 512)

    loss, grad_x, grads = _local_step(a['x'][0], a['p'][:, 0], a['loss_target'][0], w, plan)
    loss = lax.psum(loss[0, 0], ("x", "y", "c"))

    small_names = REPLICATED + list(SMALL_SHARDED)
    small = _all_reduce_small(_pack([grads[n] for n in small_names]), name="small_grad_all_reduce")
    small = dict(zip(small_names, _unpack(small, [grads[n].shape for n in small_names])))
    for n in SMALL_SHARDED:
        small[n] = lax.dynamic_slice_in_dim(small[n], me * 128, 128, axis=small[n].ndim - 1)
    big = plan.reduced()

    res = {}
    for n in BIG:
        shape = a[n].shape
        flat = _shard_rows(n)
        d, m_, v_ = _adamw(a[n].reshape(flat), big[n].reshape(flat), a['m_' + n].reshape(flat), a['v_' + n].reshape(flat),
                           name=f"adamw_{n}")
        res[n] = (big[n], d.reshape(shape), m_.reshape(shape), v_.reshape(shape))
    shapes = [a[n].shape for n in small_names]
    d, m_, v_ = _adamw(_pack([a[n] for n in small_names]), _pack([small[n] for n in small_names]),
                       _pack([a['m_' + n] for n in small_names]), _pack([a['v_' + n] for n in small_names]), name="adamw_small")
    for n, dd, mm, vv in zip(small_names, _unpack(d, shapes), _unpack(m_, shapes), _unpack(v_, shapes)):
        res[n] = (small[n], dd, mm, vv)

    outs = [loss, grad_x[None]]
    for part in range(4):
        outs += [res[n][part] for n in WEIGHTS]
    return tuple(outs)
```

```python
import functools
import math

import jax
import jax.numpy as jnp
import numpy as np
from jax import lax
from jax.experimental import pallas as pl
from jax.experimental.pallas import tpu as pltpu

F32 = jnp.float32
BF16 = jnp.bfloat16
MESH = pl.DeviceIdType.MESH
HIGHEST = lax.Precision.HIGHEST

D_MODEL = 1024
DEPTH = 4
MIX = 512
S5_GROUPS = 32
S5_GROUP = 16
S5_STATE = 64
S5_N = S5_GROUPS * S5_STATE
CHUNK = 64
LEFT_CHUNKS = 8
MAX_REL = 128
ATT_HEADS = 8
HEAD_DIM = 64
POOL_WINDOWS = (2, 4, 8, 16)
POOL_GROUP = 128
D_FF = 2816
D_PLE = 256
ALPHA = (2 * DEPTH) ** 0.25
LN_EPS = 1e-5
NEG_INF = -1e30
N_CHIPS = 4
N_DEV = 8

ADAM_LR = 0.001
ADAM_B1 = 0.9
ADAM_B2 = 0.999
ADAM_EPS = 1e-08
ADAM_WD = 0.01
ADAM_STEP = 10

TM = 512
T_S5 = 256
T_ATT = 512
VMEM_BIG = 56 * 1024 * 1024


VMEM_DEFAULT = 48 * 1024 * 1024


def _cparams(sem, vmem=None):
    return pltpu.CompilerParams(dimension_semantics=sem, vmem_limit_bytes=vmem or VMEM_DEFAULT)


def _sigmoid(x):
    return 1.0 / (1.0 + jnp.exp(-x))


def _mm(a_parts, b, *, name, trans_b=False, out_dtype=F32, tm=TM, tn=1024, vmem=None):
    m = a_parts[0][0].shape[0]
    n = b.shape[0] if trans_b else b.shape[1]
    kk = b.shape[1] if trans_b else b.shape[0]
    tn = min(tn, n)
    widths = [w for _, _, w in a_parts]
    assert sum(widths) == kk and m % tm == 0 and n % tn == 0
    na = len(a_parts)

    def body(*refs):
        b_ref, o_ref = refs[na], refs[na + 1]
        acc = None
        k0 = 0
        for ar, w in zip(refs[:na], widths):
            a = ar[...].astype(BF16)
            if trans_b:
                part = lax.dot_general(a, b_ref[:, k0:k0 + w], (((1,), (1,)), ((), ())), preferred_element_type=F32)
            else:
                part = jnp.dot(a, b_ref[k0:k0 + w, :], preferred_element_type=F32)
            acc = part if acc is None else acc + part
            k0 += w
        o_ref[...] = acc.astype(o_ref.dtype)

    in_specs = [pl.BlockSpec((tm, w), functools.partial(lambda j, i, cb: (i, cb), cb=cb)) for _, cb, w in a_parts]
    if trans_b:
        in_specs.append(pl.BlockSpec((tn, kk), lambda j, i: (j, 0)))
    else:
        in_specs.append(pl.BlockSpec((kk, tn), lambda j, i: (0, j)))
    return pl.pallas_call(
        body, name=name, grid=(n // tn, m // tm), in_specs=in_specs,
        out_specs=pl.BlockSpec((tm, tn), lambda j, i: (i, j)),
        out_shape=jax.ShapeDtypeStruct((m, n), out_dtype),
        compiler_params=_cparams(("parallel", "parallel"), vmem),
    )(*[a for a, _, _ in a_parts], b)


def _host_parts(exchange):
    if exchange is None:
        return [], [], [], [], []
    any_space = pl.BlockSpec(memory_space=pl.ANY)
    return (exchange.ins, [any_space] * len(exchange.ins), [any_space] * len(exchange.out_shapes),
            list(exchange.out_shapes), list(exchange.sems))


def _host_run(exchange, refs, first, last):
    if exchange is None:
        return
    n_in, n_out = len(exchange.ins), len(exchange.out_shapes)
    ins, outs, sems = refs[:n_in], refs[n_in:n_in + n_out], refs[n_in + n_out:]

    @pl.when(first)
    def _():
        exchange.start(ins, outs, sems)

    @pl.when(last)
    def _():
        exchange.finish(ins, outs, sems)


def _mm_tn(a, a_cb, ka, b, *, name, tk=1024, tn=1024, tmr=2 * TM, vmem=None, also_bf16=False, exchange=None):
    m = a.shape[0]
    n = b.shape[1]
    tk = min(tk, ka)
    tn = min(tn, n)
    assert ka % tk == 0 and n % tn == 0 and m % tmr == 0
    kb = ka // tk
    grid = (kb, n // tn, m // tmr)
    ex_ops, ex_in_specs, ex_out_specs, ex_out_shapes, ex_scratch = _host_parts(exchange)
    n_own_out = 2 if also_bf16 else 1

    def body(*refs):
        a_ref, b_ref = refs[:2]
        hosted_in = refs[2:2 + len(ex_ops)]
        outs = refs[2 + len(ex_ops):]
        o_ref = outs[0]
        k, j, r = pl.program_id(0), pl.program_id(1), pl.program_id(2)
        _host_run(exchange, list(hosted_in) + list(outs[n_own_out:]),
                  (k == 0) & (j == 0) & (r == 0), (k == grid[0] - 1) & (j == grid[1] - 1) & (r == grid[2] - 1))

        @pl.when(r == 0)
        def _():
            o_ref[...] = jnp.zeros_like(o_ref)

        o_ref[...] += lax.dot_general(a_ref[...].astype(BF16), b_ref[...].astype(BF16), (((0,), (0,)), ((), ())),
                                      preferred_element_type=F32)
        if also_bf16:
            @pl.when(r == grid[2] - 1)
            def _():
                outs[1][...] = o_ref[...].astype(BF16)

    tile = pl.BlockSpec((tk, tn), lambda k, j, r: (k, j))
    res = pl.pallas_call(
        body, name=name, grid=grid,
        in_specs=[pl.BlockSpec((tmr, tk), lambda k, j, r: (r, a_cb * kb + k)),
                  pl.BlockSpec((tmr, tn), lambda k, j, r: (r, j))] + ex_in_specs,
        out_specs=[tile] * n_own_out + ex_out_specs,
        out_shape=[jax.ShapeDtypeStruct((ka, n), F32)] + ([jax.ShapeDtypeStruct((ka, n), BF16)] if also_bf16 else [])
        + ex_out_shapes,
        scratch_shapes=ex_scratch,
        compiler_params=_cparams(("arbitrary",) * 3 if exchange is not None else ("parallel", "parallel", "arbitrary"), vmem),
    )(a, b, *ex_ops)
    if exchange is None:
        return tuple(res) if also_bf16 else res[0]
    own = tuple(res[:n_own_out]) if also_bf16 else res[0]
    return own, list(res[n_own_out:])


def _mm_tn_slabs(a, ka, b, nbw, nslab, *, name, tmr=2 * TM):
    m = a.shape[0]
    assert m % tmr == 0

    def body(a_ref, b_ref, o_ref):
        @pl.when(pl.program_id(1) == 0)
        def _():
            o_ref[...] = jnp.zeros_like(o_ref)

        o_ref[0] += lax.dot_general(a_ref[...].astype(BF16), b_ref[...].astype(BF16), (((0,), (0,)), ((), ())),
                                    preferred_element_type=F32)

    return pl.pallas_call(
        body, name=name, grid=(nslab, m // tmr),
        in_specs=[pl.BlockSpec((tmr, ka), lambda s, r: (r, s)), pl.BlockSpec((tmr, nbw), lambda s, r: (r, s))],
        out_specs=pl.BlockSpec((1, ka, nbw), lambda s, r: (s, 0, 0)),
        out_shape=jax.ShapeDtypeStruct((nslab, ka, nbw), F32),
        compiler_params=_cparams(("parallel", "arbitrary")),
    )(a, b)


def _ln_stats(r):
    mu = jnp.mean(r, axis=-1, keepdims=True)
    xc = r - mu
    var = jnp.mean(xc * xc, axis=-1, keepdims=True)
    rstd = lax.rsqrt(var + LN_EPS)
    return xc * rstd, rstd


def _ln_apply(r, g, b):
    xhat, _ = _ln_stats(r)
    return xhat * g + b


def _ln_grad(r, dy, g):
    xhat, rstd = _ln_stats(r)
    dxh = dy * g
    m1 = jnp.mean(dxh, axis=-1, keepdims=True)
    m2 = jnp.mean(dxh * xhat, axis=-1, keepdims=True)
    return (rstd * (dxh - m1 - xhat * m2), jnp.sum(dy * xhat, axis=0, keepdims=True), jnp.sum(dy, axis=0, keepdims=True))


def _mm_rows(matmuls, rows_in, vecs_in, out_rows, acc_widths, fn, *, name, tm=TM, vmem=None):
    m = rows_in[0].shape[0]
    assert m % tm == 0
    flat, in_specs, layout = [], [], []
    for a_parts, b, trans_b in matmuls:
        own = [(arr, cb, w) for arr, cb, w in a_parts if arr is not None]
        for arr, cb, w in own:
            flat.append(arr)
            in_specs.append(pl.BlockSpec((tm, w), functools.partial(lambda i, cb: (i, cb), cb=cb)))
        flat.append(b)
        in_specs.append(pl.BlockSpec(b.shape, lambda i: (0, 0)))
        layout.append(([(arr is None, cb, w) for arr, cb, w in a_parts], len(own), trans_b))
    first_row = len(flat)
    for r in rows_in:
        flat.append(r)
        in_specs.append(pl.BlockSpec((tm, r.shape[1]), lambda i: (i, 0)))
    for v in vecs_in:
        flat.append(v)
        in_specs.append(pl.BlockSpec(v.shape, lambda i: (0, 0)))
    n_in = len(flat)
    n_rows_out = len(out_rows)

    def body(*refs):
        rows = [r[...] for r in refs[first_row:first_row + len(rows_in)]]
        vecs = [v[...] for v in refs[first_row + len(rows_in):n_in]]
        pos = 0
        products = []
        for parts, n_own, trans_b in layout:
            b_ref = refs[pos + n_own]
            own_refs = iter(refs[pos:pos + n_own])
            acc, k0 = None, 0
            for is_row, cb, w in parts:
                a = (rows[cb] if is_row else next(own_refs)[...]).astype(BF16)
                if trans_b:
                    part = lax.dot_general(a, b_ref[:, k0:k0 + w], (((1,), (1,)), ((), ())), preferred_element_type=F32)
                else:
                    part = jnp.dot(a, b_ref[k0:k0 + w, :], preferred_element_type=F32)
                acc = part if acc is None else acc + part
                k0 += w
            products.append(acc)
            pos += n_own + 1
        outs, sums = fn(products, rows, vecs)
        for o_ref, o in zip(refs[n_in:n_in + n_rows_out], outs):
            o_ref[...] = o.astype(o_ref.dtype)
        if acc_widths:
            acc_refs = refs[n_in + n_rows_out:]

            @pl.when(pl.program_id(0) == 0)
            def _():
                for a_ref in acc_refs:
                    a_ref[...] = jnp.zeros_like(a_ref)

            for a_ref, s_ in zip(acc_refs, sums):
                a_ref[...] += s_

    out_specs = [pl.BlockSpec((tm, n), lambda i: (i, 0)) for n, _ in out_rows]
    out_specs += [pl.BlockSpec((1, wd), lambda i: (0, 0)) for wd in acc_widths]
    out_shape = [jax.ShapeDtypeStruct((m, n), dt) for n, dt in out_rows]
    out_shape += [jax.ShapeDtypeStruct((1, wd), F32) for wd in acc_widths]
    return pl.pallas_call(
        body, name=name, grid=(m // tm,), in_specs=in_specs, out_specs=out_specs, out_shape=out_shape,
        compiler_params=_cparams(("arbitrary",) if acc_widths else ("parallel",), vmem),
    )(*flat)


def _ffn_up(x1, wup, *, name, exchange=None):
    m = x1.shape[0]
    tn = D_FF // 2
    grid = (2, m // TM)
    ex_ops, ex_in_specs, ex_out_specs, ex_out_shapes, ex_scratch = _host_parts(exchange)

    def body(*refs):
        x_ref, wg_ref, wu_ref = refs[:3]
        hosted_in = refs[3:3 + len(ex_ops)]
        a_ref, g_ref, u_ref = refs[3 + len(ex_ops):6 + len(ex_ops)]
        j, i = pl.program_id(0), pl.program_id(1)
        _host_run(exchange, list(hosted_in) + list(refs[6 + len(ex_ops):]),
                  (j == 0) & (i == 0), (j == grid[0] - 1) & (i == grid[1] - 1))
        x = x_ref[...].astype(BF16)
        g = jnp.dot(x, wg_ref[...], preferred_element_type=F32)
        u = jnp.dot(x, wu_ref[...], preferred_element_type=F32)
        a_ref[...] = (g * _sigmoid(g) * u).astype(BF16)
        g_ref[...] = g.astype(BF16)
        u_ref[...] = u.astype(BF16)

    out = pl.BlockSpec((TM, tn), lambda j, i: (i, j))
    res = pl.pallas_call(
        body, name=name, grid=grid,
        in_specs=[pl.BlockSpec((TM, D_MODEL), lambda j, i: (i, 0)),
                  pl.BlockSpec((D_MODEL, tn), lambda j, i: (0, j)),
                  pl.BlockSpec((D_MODEL, tn), lambda j, i: (0, j + 2))] + ex_in_specs,
        out_specs=[out, out, out] + ex_out_specs,
        out_shape=[jax.ShapeDtypeStruct((m, D_FF), BF16)] * 3 + ex_out_shapes,
        scratch_shapes=ex_scratch,
        compiler_params=_cparams(("arbitrary", "arbitrary") if exchange is not None else ("parallel", "parallel")),
    )(x1, wup, wup, *ex_ops)
    return (res[0], res[1], res[2]) if exchange is None else ((res[0], res[1], res[2]), list(res[3:]))


def _ffn_down_bwd(df, wdown, g, u, *, name):
    m = df.shape[0]
    tm = TM
    chunk = 256

    def body(df_ref, w_ref, g_ref, u_ref, o_ref):
        df = df_ref[...].astype(BF16)
        for part in range(D_FF // chunk):
            cols = slice(chunk * part, chunk * (part + 1))
            da = lax.dot_general(df, w_ref[cols, :], (((1,), (1,)), ((), ())), preferred_element_type=F32)
            gg = g_ref[:, cols].astype(F32)
            sg = _sigmoid(gg)
            o_ref[:, cols] = (da * u_ref[:, cols].astype(F32) * (sg * (1.0 + gg * (1.0 - sg)))).astype(BF16)
            o_ref[:, D_FF + chunk * part:D_FF + chunk * (part + 1)] = (da * (gg * sg)).astype(BF16)

    return pl.pallas_call(
        body, name=name, grid=(m // tm,),
        in_specs=[pl.BlockSpec((tm, D_MODEL), lambda i: (i, 0)), pl.BlockSpec((D_FF, D_MODEL), lambda i: (0, 0)),
                  pl.BlockSpec((tm, D_FF), lambda i: (i, 0)), pl.BlockSpec((tm, D_FF), lambda i: (i, 0))],
        out_specs=pl.BlockSpec((tm, 2 * D_FF), lambda i: (i, 0)),
        out_shape=jax.ShapeDtypeStruct((m, 2 * D_FF), BF16),
        compiler_params=_cparams(("parallel",), VMEM_BIG),
    )(df, wdown, g, u)


def _ple_ln_bwd(dx3, gate, pp, r2, wgate, g2, *, name):
    m, n = dx3.shape

    def body(dx3_ref, gate_ref, pp_ref, r_ref, w_ref, g_ref, dz_ref, dpp_ref, dr_ref, dbg_ref, dg_ref, dbias_ref):
        dx3 = dx3_ref[...]

        @pl.when(pl.program_id(0) == 0)
        def _():
            dbg_ref[...] = jnp.zeros_like(dbg_ref)
            dg_ref[...] = jnp.zeros_like(dg_ref)
            dbias_ref[...] = jnp.zeros_like(dbias_ref)

        gate = gate_ref[...].astype(F32)
        dz = dx3 * pp_ref[...].astype(F32) * gate * (1.0 - gate)
        dzq = dz.astype(BF16)
        dz_ref[...] = dzq
        dpp_ref[...] = (dx3 * gate).astype(BF16)
        dbg_ref[...] += jnp.sum(dz, axis=0, keepdims=True)
        dx2 = dx3 + lax.dot_general(dzq, w_ref[...], (((1,), (1,)), ((), ())), preferred_element_type=F32)
        dr, dg, dbias = _ln_grad(r_ref[...], dx2, g_ref[...])
        dr_ref[...] = dr
        dg_ref[...] += dg
        dbias_ref[...] += dbias

    row = pl.BlockSpec((TM, n), lambda i: (i, 0))
    vec = pl.BlockSpec((1, n), lambda i: (0, 0))
    in_specs = [row] * 4 + [pl.BlockSpec(wgate.shape, lambda i: (0, 0)), vec]
    return pl.pallas_call(
        body, name=name, grid=(m // TM,), in_specs=in_specs, out_specs=[row, row, row, vec, vec, vec],
        out_shape=[jax.ShapeDtypeStruct((m, n), BF16), jax.ShapeDtypeStruct((m, n), BF16), jax.ShapeDtypeStruct((m, n), F32)]
        + [jax.ShapeDtypeStruct((1, n), F32)] * 3,
        compiler_params=_cparams(("arbitrary",)),
    )(dx3, gate, pp, r2, wgate, g2)


def _loss_head(y, target, *, name):
    m, n = y.shape

    def body(y_ref, t_ref, loss_ref, dy_ref):
        @pl.when(pl.program_id(0) == 0)
        def _():
            loss_ref[...] = jnp.zeros_like(loss_ref)

        err = y_ref[...] - t_ref[...]
        dy_ref[...] = err * (1.0 / n)
        per_tok = jnp.mean(err * err, axis=-1, keepdims=True)
        loss_ref[...] += 0.5 * jnp.sum(per_tok, axis=0, keepdims=True)

    row = pl.BlockSpec((TM, n), lambda i: (i, 0))
    return pl.pallas_call(
        body, name=name, grid=(m // TM,), in_specs=[row, row],
        out_specs=[pl.BlockSpec((1, 1), lambda i: (0, 0)), row],
        out_shape=[jax.ShapeDtypeStruct((1, 1), F32), jax.ShapeDtypeStruct((m, n), F32)],
        compiler_params=_cparams(("arbitrary",)),
    )(y, target)


def _gelu(y):
    c = math.sqrt(2.0 / math.pi)
    return 0.5 * y * (1.0 + jnp.tanh(c * (y + 0.044715 * y * y * y)))


def _gelu_grad(y):
    c = math.sqrt(2.0 / math.pi)
    t = jnp.tanh(c * (y + 0.044715 * y * y * y))
    return 0.5 * (1.0 + t) + 0.5 * y * (1.0 - t * t) * c * (1.0 + 3.0 * 0.044715 * y * y)


STRIP = 128
S5_SLABS = 4
SLAB_COLS = 2 * S5_N // S5_SLABS
N_TILES = 2 * S5_N // STRIP
SLAB_TILES = SLAB_COLS // STRIP


def _strip_tiles(j):
    re_tile = (j // (SLAB_TILES // 2)) * SLAB_TILES + j % (SLAB_TILES // 2)
    return pl.multiple_of(j * STRIP, STRIP), re_tile, re_tile + SLAB_TILES // 2


def _store_tiles(ref, first_tile, value):
    for k in range(value.shape[1] // STRIP):
        ref[first_tile + k] = value[:, STRIP * k:STRIP * (k + 1)]


def _load_tiles(ref, first_tile, count):
    return jnp.concatenate([ref[first_tile + k] for k in range(count)], axis=1)


PTAB_ROWS = 40
GROUPS = T_S5 // 8


def _scan_cols(ref, hr, hi, ptab_ref, off, down, visit=None):
    sign = 1.0 if down else -1.0
    ref_r, ref_i = ref
    cols_p = pl.ds(off, STRIP)

    def power(row, im_offset=8):
        return ptab_ref[row:row + 1, cols_p], sign * ptab_ref[row + im_offset:row + im_offset + 1, cols_p]

    def rows(r):
        return pl.ds(r, GROUPS, stride=8)

    def mul_add(br, bi, qr, qi, vr, vi):
        return br + qr * vr - qi * vi, bi + qr * vi + qi * vr

    order = list(range(8)) if down else list(range(7, -1, -1))
    lam_r, lam_i = power(0)
    vr, vi = ref_r[rows(order[0]), :], ref_i[rows(order[0]), :]
    for r in order[1:]:
        vr, vi = mul_add(ref_r[rows(r), :], ref_i[rows(r), :], lam_r, lam_i, vr, vi)
        ref_r[rows(r), :] = vr
        ref_i[rows(r), :] = vi
    grow = lax.broadcasted_iota(jnp.int32, (GROUPS, STRIP), 0)
    edge = 0 if down else GROUPS - 1
    l8r, l8i = power(7)
    er = vr + jnp.where(grow == edge, l8r * hr - l8i * hi, 0.0)
    ei = vi + jnp.where(grow == edge, l8r * hi + l8i * hr, 0.0)
    k, step = 0, 1
    while step < GROUPS:
        qr, qi = (l8r, l8i) if k == 0 else power(32 + k - 1, 4)
        if down:
            sr = jnp.where(grow >= step, pltpu.roll(er, step, 0), 0.0)
            si = jnp.where(grow >= step, pltpu.roll(ei, step, 0), 0.0)
        else:
            sr = jnp.where(grow < GROUPS - step, pltpu.roll(er, GROUPS - step, 0), 0.0)
            si = jnp.where(grow < GROUPS - step, pltpu.roll(ei, GROUPS - step, 0), 0.0)
        er, ei = mul_add(er, ei, qr, qi, sr, si)
        k, step = k + 1, 2 * step
    if down:
        cr = jnp.where(grow == 0, hr, pltpu.roll(er, 1, 0))
        ci = jnp.where(grow == 0, hi, pltpu.roll(ei, 1, 0))
    else:
        cr = jnp.where(grow == GROUPS - 1, hr, pltpu.roll(er, GROUPS - 1, 0))
        ci = jnp.where(grow == GROUPS - 1, hi, pltpu.roll(ei, GROUPS - 1, 0))
    for r in range(8):
        qr, qi = power(r if down else 16 + r)
        xr, xi = mul_add(ref_r[rows(r), :], ref_i[rows(r), :], qr, qi, cr, ci)
        ref_r[rows(r), :] = xr
        ref_i[rows(r), :] = xi
        if visit is not None:
            visit(r, xr, xi)
    last = GROUPS - 1 if down else 0
    return er[last:last + 1], ei[last:last + 1]


def _s5_fwd(h, bmat, cmat, dvec, wglu, bglu, ptab, *, name):
    m = h.shape[0]
    t = T_S5
    nb = m // t

    def body(u_ref, bmat_ref, cmat_ref, d_ref, wglu_ref, bglu_ref, ptab_ref,
             out_ref, y_ref, hb_ref, bu_ref, carry_ref):
        @pl.when(pl.program_id(0) == 0)
        def _():
            carry_ref[...] = jnp.zeros_like(carry_ref)

        hb_ref[0] = carry_ref[...]
        u = u_ref[...]
        ub = u.astype(BF16)
        for s in range(S5_SLABS):
            _store_tiles(bu_ref, SLAB_TILES * s,
                         jnp.dot(ub[:, 128 * s:128 * (s + 1)], bmat_ref[s], preferred_element_type=F32))

        def strip(j, c):
            off, tr, ti = _strip_tiles(j)
            cols_r, cols_i = pl.ds(pl.multiple_of(tr * STRIP, STRIP), STRIP), pl.ds(pl.multiple_of(ti * STRIP, STRIP), STRIP)
            er, ei = _scan_cols((bu_ref.at[tr], bu_ref.at[ti]), carry_ref[0:1, cols_r], carry_ref[0:1, cols_i],
                                ptab_ref, off, True)
            carry_ref[0:1, cols_r] = er
            carry_ref[0:1, cols_i] = ei
            return c

        lax.fori_loop(0, S5_N // STRIP, strip, 0)
        y = jnp.concatenate(
            [jnp.dot(_load_tiles(bu_ref, SLAB_TILES * s, SLAB_TILES).astype(BF16), cmat_ref[s], preferred_element_type=F32)
             for s in range(S5_SLABS)], axis=1) + d_ref[...] * u
        y_ref[...] = y
        g = _gelu(y)
        zz = jnp.dot(g.astype(BF16), wglu_ref[...], preferred_element_type=F32) + bglu_ref[...]
        out_ref[...] = (g * _sigmoid(zz)).astype(BF16)

    const = lambda shape: pl.BlockSpec(shape, lambda i: (0,) * len(shape))
    row_spec = pl.BlockSpec((t, MIX), lambda i: (i, 0))
    return pl.pallas_call(
        body, name=name, grid=(nb,),
        in_specs=[row_spec, const((S5_SLABS, 128, SLAB_COLS)), const((S5_SLABS, SLAB_COLS, 128)), const((1, MIX)),
                  const((MIX, MIX)), const((1, MIX)), const((PTAB_ROWS, S5_N))],
        out_specs=[row_spec, row_spec, pl.BlockSpec((1, 1, 2 * S5_N), lambda i: (i, 0, 0))],
        out_shape=[jax.ShapeDtypeStruct((m, MIX), BF16), jax.ShapeDtypeStruct((m, MIX), F32),
                   jax.ShapeDtypeStruct((nb, 1, 2 * S5_N), F32)],
        scratch_shapes=[pltpu.VMEM((N_TILES, t, STRIP), F32), pltpu.VMEM((1, 2 * S5_N), F32)],
        compiler_params=_cparams(("arbitrary",), VMEM_BIG),
    )(h, bmat, cmat, dvec, wglu, bglu, ptab)


def _s5_bwd(dcat, ypre, h, hb, bmat, cmat, dvec, wglu, bglu, ptab, *, name):
    m = h.shape[0]
    t = T_S5
    nb = m // t

    def body(dya_ref, y_ref, u_ref, hb_ref, bmat_ref, cmat_ref, d_ref, wglu_ref, bglu_ref, ptab_ref,
             du_ref, xb_ref, gb_ref, gq_ref, dzz_ref, dyq_ref, dlam_ref, dbglu_ref, dd_ref,
             bu_ref, dx_ref, gcarry_ref):
        @pl.when(pl.program_id(0) == 0)
        def _():
            gcarry_ref[...] = jnp.zeros_like(gcarry_ref)
            dlam_ref[...] = jnp.zeros_like(dlam_ref)
            dbglu_ref[...] = jnp.zeros_like(dbglu_ref)
            dd_ref[...] = jnp.zeros_like(dd_ref)

        u = u_ref[...]
        y = y_ref[...]
        g = _gelu(y)
        gq = g.astype(BF16)
        sg = _sigmoid(jnp.dot(gq, wglu_ref[...], preferred_element_type=F32) + bglu_ref[...])
        dout = dya_ref[...]
        dzz = dout * g * sg * (1.0 - sg)
        dzzq = dzz.astype(BF16)
        dg = dout * sg + lax.dot_general(dzzq, wglu_ref[...], (((1,), (1,)), ((), ())), preferred_element_type=F32)
        dy = dg * _gelu_grad(y)
        dyq = dy.astype(BF16)
        gq_ref[...] = gq
        dzz_ref[...] = dzzq
        dyq_ref[...] = dyq
        dbglu_ref[...] += jnp.sum(dzz, axis=0, keepdims=True)
        dd_ref[...] += jnp.sum(dy * u, axis=0, keepdims=True)

        ub = u.astype(BF16)
        nt = (((1,), (1,)), ((), ()))
        for s in range(S5_SLABS):
            _store_tiles(dx_ref, SLAB_TILES * s,
                         lax.dot_general(dyq[:, 128 * s:128 * (s + 1)], cmat_ref[s], nt, preferred_element_type=F32))
            _store_tiles(bu_ref, SLAB_TILES * s,
                         jnp.dot(ub[:, 128 * s:128 * (s + 1)], bmat_ref[s], preferred_element_type=F32))
        grow = lax.broadcasted_iota(jnp.int32, (GROUPS, STRIP), 0)

        def strip(j, c):
            off, tr, ti = _strip_tiles(j)
            cols_r, cols_i = pl.ds(pl.multiple_of(tr * STRIP, STRIP), STRIP), pl.ds(pl.multiple_of(ti * STRIP, STRIP), STRIP)
            x_r, x_i = bu_ref.at[tr], bu_ref.at[ti]
            hr = hb_ref[0, 0:1, cols_r]
            hi = hb_ref[0, 0:1, cols_i]
            _scan_cols((x_r, x_i), hr, hi, ptab_ref, off, True)
            xb_ref[:, cols_r] = x_r[...].astype(BF16)
            xb_ref[:, cols_i] = x_i[...].astype(BF16)
            sums = [jnp.zeros((1, STRIP), F32), jnp.zeros((1, STRIP), F32)]

            def d_lam(r, gr, gi):
                if r == 0:
                    pr_ = jnp.where(grow == 0, hr, pltpu.roll(x_r[pl.ds(7, GROUPS, stride=8), :], 1, 0))
                    pi_ = jnp.where(grow == 0, hi, pltpu.roll(x_i[pl.ds(7, GROUPS, stride=8), :], 1, 0))
                else:
                    pr_ = x_r[pl.ds(r - 1, GROUPS, stride=8), :]
                    pi_ = x_i[pl.ds(r - 1, GROUPS, stride=8), :]
                sums[0] = sums[0] + jnp.sum(pr_ * gr + pi_ * gi, axis=0, keepdims=True)
                sums[1] = sums[1] + jnp.sum(pr_ * gi - pi_ * gr, axis=0, keepdims=True)

            g_r, g_i = dx_ref.at[tr], dx_ref.at[ti]
            gr0, gi0 = _scan_cols((g_r, g_i), gcarry_ref[0:1, cols_r], gcarry_ref[0:1, cols_i], ptab_ref, off, False, d_lam)
            gb_ref[:, cols_r] = g_r[...].astype(BF16)
            gb_ref[:, cols_i] = g_i[...].astype(BF16)
            gcarry_ref[0:1, cols_r] = gr0
            gcarry_ref[0:1, cols_i] = gi0
            dlam_ref[0:1, pl.ds(off, STRIP)] += sums[0]
            dlam_ref[1:2, pl.ds(off, STRIP)] += sums[1]
            return c

        lax.fori_loop(0, S5_N // STRIP, strip, 0)
        du_ref[...] = dy * d_ref[...] + jnp.concatenate(
            [lax.dot_general(gb_ref[:, SLAB_COLS * s:SLAB_COLS * (s + 1)], bmat_ref[s], nt, preferred_element_type=F32)
             for s in range(S5_SLABS)], axis=1)

    const = lambda shape: pl.BlockSpec(shape, lambda i: (0,) * len(shape))
    rev = lambda i: (nb - 1 - i, 0)
    row_spec = pl.BlockSpec((t, MIX), rev)
    wide = pl.BlockSpec((t, 2 * S5_N), rev)
    return pl.pallas_call(
        body, name=name, grid=(nb,),
        in_specs=[row_spec, row_spec, row_spec, pl.BlockSpec((1, 1, 2 * S5_N), lambda i: (nb - 1 - i, 0, 0)),
                  const((S5_SLABS, 128, SLAB_COLS)), const((S5_SLABS, SLAB_COLS, 128)), const((1, MIX)), const((MIX, MIX)),
                  const((1, MIX)), const((PTAB_ROWS, S5_N))],
        out_specs=[row_spec, wide, wide, row_spec, row_spec, row_spec, const((2, S5_N)), const((1, MIX)), const((1, MIX))],
        out_shape=[jax.ShapeDtypeStruct((m, MIX), F32), jax.ShapeDtypeStruct((m, 2 * S5_N), BF16),
                   jax.ShapeDtypeStruct((m, 2 * S5_N), BF16), jax.ShapeDtypeStruct((m, MIX), BF16),
                   jax.ShapeDtypeStruct((m, MIX), BF16), jax.ShapeDtypeStruct((m, MIX), BF16),
                   jax.ShapeDtypeStruct((2, S5_N), F32), jax.ShapeDtypeStruct((1, MIX), F32), jax.ShapeDtypeStruct((1, MIX), F32)],
        scratch_shapes=[pltpu.VMEM((N_TILES, t, STRIP), F32), pltpu.VMEM((N_TILES, t, STRIP), F32),
                        pltpu.VMEM((1, 2 * S5_N), F32)],
        compiler_params=_cparams(("arbitrary",), VMEM_BIG),
    )(dcat, ypre, h, hb, bmat, cmat, dvec, wglu, bglu, ptab)


HALO = 8


def _taps_down(zext, t):
    return pltpu.roll(zext, 1, 0)[HALO:HALO + t], pltpu.roll(zext, 2, 0)[HALO:HALO + t]


def _conv_z(c_ref, x_ref, cp_ref, xp_ref, first, t):
    z = c_ref[...] * x_ref[...]
    zp = jnp.where(first, 0.0, cp_ref[t - HALO:t, :] * xp_ref[t - HALO:t, :])
    z1, z2 = _taps_down(jnp.concatenate([zp, z], axis=0), t)
    return z, z1, z2


def _conv_fwd(h, cw, *, name):
    m = h.shape[0]
    t = TM
    nb = m // t

    def body(b_ref, c_ref, x_ref, cp_ref, xp_ref, w_ref, o_ref):
        z, z1, z2 = _conv_z(c_ref, x_ref, cp_ref, xp_ref, pl.program_id(0) == 0, t)
        o_ref[...] = (b_ref[...] * (w_ref[0:1, :] * z2 + w_ref[1:2, :] * z1 + w_ref[2:3, :] * z)).astype(BF16)

    cur = lambda cb: pl.BlockSpec((t, MIX), lambda i: (i, cb))
    prev = lambda cb: pl.BlockSpec((t, MIX), lambda i: (jnp.maximum(i - 1, 0), cb))
    return pl.pallas_call(
        body, name=name, grid=(nb,),
        in_specs=[cur(1), cur(2), cur(3), prev(2), prev(3), pl.BlockSpec((3, MIX), lambda i: (0, 0))],
        out_specs=pl.BlockSpec((t, MIX), lambda i: (i, 0)),
        out_shape=jax.ShapeDtypeStruct((m, MIX), BF16),
        compiler_params=_cparams(("parallel",)),
    )(h, h, h, h, h, cw)


def _conv_bwd(dcat, h, cw, *, name):
    m = h.shape[0]
    t = TM
    nb = m // t

    def body(dy_ref, dyn_ref, b_ref, c_ref, x_ref, cp_ref, xp_ref, bn_ref, w_ref, o_ref, dw_ref):
        i = pl.program_id(0)

        @pl.when(i == 0)
        def _():
            dw_ref[...] = jnp.zeros_like(dw_ref)

        z, z1, z2 = _conv_z(c_ref, x_ref, cp_ref, xp_ref, i == 0, t)
        w0, w1, w2 = w_ref[0:1, :], w_ref[1:2, :], w_ref[2:3, :]
        dy = dy_ref[...]
        dconv = dy * b_ref[...]
        dnext = jnp.where(i == nb - 1, 0.0, dyn_ref[0:HALO, :] * bn_ref[0:HALO, :])
        dext = jnp.concatenate([dconv, dnext], axis=0)
        d1 = pltpu.roll(dext, t + HALO - 1, 0)[0:t]
        d2 = pltpu.roll(dext, t + HALO - 2, 0)[0:t]
        dz = w2 * dconv + w1 * d1 + w0 * d2
        o_ref[:, 0:MIX] = dy * (w0 * z2 + w1 * z1 + w2 * z)
        o_ref[:, MIX:2 * MIX] = dz * x_ref[...]
        o_ref[:, 2 * MIX:3 * MIX] = dz * c_ref[...]
        dw_ref[0:1, :] += jnp.sum(dconv * z2, axis=0, keepdims=True)
        dw_ref[1:2, :] += jnp.sum(dconv * z1, axis=0, keepdims=True)
        dw_ref[2:3, :] += jnp.sum(dconv * z, axis=0, keepdims=True)

    cur = lambda cb: pl.BlockSpec((t, MIX), lambda i: (i, cb))
    prev = lambda cb: pl.BlockSpec((t, MIX), lambda i: (jnp.maximum(i - 1, 0), cb))
    nxt = lambda cb: pl.BlockSpec((t, MIX), lambda i: (jnp.minimum(i + 1, nb - 1), cb))
    return pl.pallas_call(
        body, name=name, grid=(nb,),
        in_specs=[cur(1), nxt(1), cur(1), cur(2), cur(3), prev(2), prev(3), nxt(1), pl.BlockSpec((3, MIX), lambda i: (0, 0))],
        out_specs=[pl.BlockSpec((t, 3 * MIX), lambda i: (i, 0)), pl.BlockSpec((8, MIX), lambda i: (0, 0))],
        out_shape=[jax.ShapeDtypeStruct((m, 3 * MIX), F32), jax.ShapeDtypeStruct((8, MIX), F32)],
        compiler_params=_cparams(("arbitrary",)),
    )(dcat, dcat, h, h, h, h, h, h, cw)


PHALO = 16


def _pool_pooled(z_ref, zp_ref, i, t):
    z = z_ref[...]
    zp = jnp.where(i == 0, 0.0, zp_ref[t - PHALO:t, :])
    s = jnp.concatenate([zp, z], axis=0)
    sums = {}
    width = 1
    while width < PHALO:
        s = s + pltpu.roll(s, width, 0)
        width *= 2
        sums[width] = s[PHALO:PHALO + t]
    tpos = i * t + lax.broadcasted_iota(jnp.int32, (t, 1), 0)
    outs = []
    for gi, w in enumerate(POOL_WINDOWS):
        lo = gi * POOL_GROUP
        count = jnp.minimum(tpos + 1, w).astype(F32)
        outs.append(sums[w][:, lo:lo + POOL_GROUP] / count - z[:, lo:lo + POOL_GROUP])
    return outs


def _pool_fwd(h, pw, ps, *, name):
    m = h.shape[0]
    t = TM
    nb = m // t

    def body(z_ref, zp_ref, pw_ref, ps_ref, o_ref):
        pooled = _pool_pooled(z_ref, zp_ref, pl.program_id(0), t)
        for gi in range(len(POOL_WINDOWS)):
            lo = gi * POOL_GROUP
            mixed = jnp.dot(pooled[gi].astype(BF16), pw_ref[gi], preferred_element_type=F32)
            o_ref[:, lo:lo + POOL_GROUP] = (mixed * ps_ref[:, lo:lo + POOL_GROUP]).astype(BF16)

    return pl.pallas_call(
        body, name=name, grid=(nb,),
        in_specs=[pl.BlockSpec((t, MIX), lambda i: (i, 3)), pl.BlockSpec((t, MIX), lambda i: (jnp.maximum(i - 1, 0), 3)),
                  pl.BlockSpec((4, POOL_GROUP, POOL_GROUP), lambda i: (0, 0, 0)), pl.BlockSpec((1, MIX), lambda i: (0, 0))],
        out_specs=pl.BlockSpec((t, MIX), lambda i: (i, 0)),
        out_shape=jax.ShapeDtypeStruct((m, MIX), BF16),
        compiler_params=_cparams(("parallel",)),
    )(h, h, pw, ps)


def _pool_bwd(dcat, h, pw, ps, *, name):
    m = h.shape[0]
    t = TM
    nb = m // t

    def body(dy_ref, dyn_ref, z_ref, zp_ref, pw_ref, ps_ref, dz_ref, dpw_ref, dps_ref):
        i = pl.program_id(0)

        @pl.when(i == 0)
        def _():
            dpw_ref[...] = jnp.zeros_like(dpw_ref)
            dps_ref[...] = jnp.zeros_like(dps_ref)

        pooled = _pool_pooled(z_ref, zp_ref, i, t)
        dy = dy_ref[...]
        tpos = i * t + lax.broadcasted_iota(jnp.int32, (t, 1), 0)
        for gi, w in enumerate(POOL_WINDOWS):
            lo = gi * POOL_GROUP
            sl = slice(lo, lo + POOL_GROUP)
            pq = pooled[gi].astype(BF16)
            mixed = jnp.dot(pq, pw_ref[gi], preferred_element_type=F32)
            dps_ref[:, sl] += jnp.sum(dy[:, sl] * mixed, axis=0, keepdims=True)
            dmix = (dy[:, sl] * ps_ref[:, sl]).astype(BF16)
            dpw_ref[gi] += lax.dot_general(pq, dmix, (((0,), (0,)), ((), ())), preferred_element_type=F32)
            dpool = lax.dot_general(dmix, pw_ref[gi], (((1,), (1,)), ((), ())), preferred_element_type=F32)
            dmix_n = (dyn_ref[0:PHALO, sl] * ps_ref[:, sl]).astype(BF16)
            dpool_n = lax.dot_general(dmix_n, pw_ref[gi], (((1,), (1,)), ((), ())), preferred_element_type=F32)
            e = dpool / jnp.minimum(tpos + 1, w).astype(F32)
            e_n = jnp.where(i == nb - 1, 0.0, dpool_n * (1.0 / w))
            f = jnp.concatenate([e, e_n], axis=0)
            width = 1
            while width < w:
                f = f + pltpu.roll(f, t + PHALO - width, 0)
                width *= 2
            dz_ref[:, sl] = f[0:t] - dpool

    return pl.pallas_call(
        body, name=name, grid=(nb,),
        in_specs=[pl.BlockSpec((t, MIX), lambda i: (i, 1)), pl.BlockSpec((t, MIX), lambda i: (jnp.minimum(i + 1, nb - 1), 1)),
                  pl.BlockSpec((t, MIX), lambda i: (i, 3)), pl.BlockSpec((t, MIX), lambda i: (jnp.maximum(i - 1, 0), 3)),
                  pl.BlockSpec((4, POOL_GROUP, POOL_GROUP), lambda i: (0, 0, 0)), pl.BlockSpec((1, MIX), lambda i: (0, 0))],
        out_specs=[pl.BlockSpec((t, MIX), lambda i: (i, 0)), pl.BlockSpec((4, POOL_GROUP, POOL_GROUP), lambda i: (0, 0, 0)),
                   pl.BlockSpec((1, MIX), lambda i: (0, 0))],
        out_shape=[jax.ShapeDtypeStruct((m, MIX), F32), jax.ShapeDtypeStruct((4, POOL_GROUP, POOL_GROUP), F32),
                   jax.ShapeDtypeStruct((1, MIX), F32)],
        compiler_params=_cparams(("arbitrary",)),
    )(dcat, dcat, h, h, pw, ps)


NKEY = 2 * T_ATT


def _band_mask():
    qc = np.arange(T_ATT)[:, None] // CHUNK
    kc = np.arange(NKEY)[None, :] // CHUNK - LEFT_CHUNKS
    return np.where((kc <= qc) & (kc >= qc - LEFT_CHUNKS), 0.0, NEG_INF).astype(np.float32)


def _diag_index():
    c = np.arange(NKEY)
    d = np.where(c <= NKEY // 2 + CHUNK, T_ATT - c, T_ATT + NKEY - c)
    return np.clip(d, -MAX_REL, MAX_REL) + MAX_REL


def _bias_tile(vd_ref, mask_ref, tile_ref):
    col = lax.broadcasted_iota(jnp.int32, (8, NKEY), 1)
    no_prev = jnp.where(col < T_ATT, NEG_INF, 0.0)
    for hh in range(2):
        v = vd_ref[0, hh:hh + 1, :]
        base = jnp.concatenate([v if s == 0 else pltpu.roll(v, s, 1) for s in range(8)], axis=0)
        for mrow in range(T_ATT // 8):
            rows = slice(8 * mrow, 8 * mrow + 8)
            blk = (base if mrow == 0 else pltpu.roll(base, 8 * mrow, 1)) + mask_ref[rows, :]
            tile_ref[hh, rows, :] = blk
            tile_ref[2 + hh, rows, :] = blk + no_prev


BAND_ROWS = 2 * CHUNK
BAND_COLS = (LEFT_CHUNKS + 2) * CHUNK
N_BANDS = T_ATT // BAND_ROWS


def _band(x, r):
    return x[BAND_ROWS * r:BAND_ROWS * (r + 1), BAND_ROWS * r:BAND_ROWS * r + BAND_COLS]


def _from_bands(parts):
    rows = []
    for r, part in enumerate(parts):
        right = NKEY - BAND_COLS - BAND_ROWS * r
        pieces = ([jnp.zeros((BAND_ROWS, BAND_ROWS * r), part.dtype)] if r else []) + [part]
        pieces += [jnp.zeros((BAND_ROWS, right), part.dtype)] if right else []
        rows.append(jnp.concatenate(pieces, axis=1))
    return jnp.concatenate(rows, axis=0)


def _attn_probs(q, kc, tile_ref, idx):
    s = lax.dot_general(q, kc, (((1,), (1,)), ((), ())), preferred_element_type=F32)
    parts = []
    for r in range(N_BANDS):
        sb = _band(s, r) + tile_ref[idx, BAND_ROWS * r:BAND_ROWS * (r + 1), BAND_ROWS * r:BAND_ROWS * r + BAND_COLS]
        p = jnp.exp(sb - jnp.max(sb, axis=-1, keepdims=True))
        parts.append(p * (1.0 / jnp.sum(p, axis=-1, keepdims=True)))
    return parts


def _attn_specs(block):
    cur = lambda base: pl.BlockSpec((T_ATT, 128), lambda hp, i: (block(i), base + hp))
    prev = lambda base: pl.BlockSpec((T_ATT, 128), lambda hp, i: (jnp.maximum(block(i) - 1, 0), base + hp))
    return [cur(0), cur(4), prev(4), cur(8), prev(8),
            pl.BlockSpec((1, 2, NKEY), lambda hp, i: (hp, 0, 0)), pl.BlockSpec((T_ATT, NKEY), lambda hp, i: (0, 0))]


def _attn_fwd(h, vdiag, mask, *, name):
    m = h.shape[0]
    nb = m // T_ATT

    def body(q_ref, k_ref, kp_ref, v_ref, vp_ref, vd_ref, mask_ref, o_ref, tile_ref):
        i = pl.program_id(1)

        @pl.when(i == 0)
        def _():
            _bias_tile(vd_ref, mask_ref, tile_ref)

        first = jnp.where(i == 0, 2, 0)
        outs = []
        for hh in range(2):
            sl = slice(hh * HEAD_DIM, (hh + 1) * HEAD_DIM)
            q = (q_ref[:, sl] * (HEAD_DIM ** -0.5)).astype(BF16)
            kc = jnp.concatenate([kp_ref[:, sl], k_ref[:, sl]], axis=0).astype(BF16)
            vc = jnp.concatenate([vp_ref[:, sl], v_ref[:, sl]], axis=0).astype(BF16)
            p = _from_bands([b.astype(BF16) for b in _attn_probs(q, kc, tile_ref, first + hh)])
            outs.append(jnp.dot(p, vc, preferred_element_type=F32))
        o_ref[...] = jnp.concatenate(outs, axis=1).astype(BF16)

    return pl.pallas_call(
        body, name=name, grid=(ATT_HEADS // 2, nb), in_specs=_attn_specs(lambda i: i),
        out_specs=pl.BlockSpec((T_ATT, 128), lambda hp, i: (i, hp)),
        out_shape=jax.ShapeDtypeStruct((m, MIX), BF16),
        scratch_shapes=[pltpu.VMEM((4, T_ATT, NKEY), F32)],
        compiler_params=_cparams(("parallel", "arbitrary"), VMEM_BIG),
    )(h, h, h, h, h, vdiag, mask)


def _attn_bwd(dcat, h, vdiag, mask, *, name):
    m = h.shape[0]
    nb = m // T_ATT

    def body(do_ref, q_ref, k_ref, kp_ref, v_ref, vp_ref, vd_ref, mask_ref,
             dq_ref, dk_ref, dv_ref, dvd_ref, tile_ref, acc_ref, carry_ref):
        i = pl.program_id(1)

        @pl.when(i == 0)
        def _():
            _bias_tile(vd_ref, mask_ref, tile_ref)
            acc_ref[...] = jnp.zeros_like(acc_ref)

            carry_ref[...] = jnp.zeros_like(carry_ref)

        scale = HEAD_DIM ** -0.5
        first = jnp.where(i == nb - 1, 2, 0)
        dqs, dks, dvs = [], [], []
        for hh in range(2):
            sl = slice(hh * HEAD_DIM, (hh + 1) * HEAD_DIM)
            q = (q_ref[:, sl] * scale).astype(BF16)
            kc = jnp.concatenate([kp_ref[:, sl], k_ref[:, sl]], axis=0).astype(BF16)
            vc = jnp.concatenate([vp_ref[:, sl], v_ref[:, sl]], axis=0).astype(BF16)
            do = do_ref[:, sl].astype(BF16)
            bands = _attn_probs(q, kc, tile_ref, first + hh)
            p = _from_bands([b.astype(BF16) for b in bands])
            dvs.append(lax.dot_general(p, do, (((0,), (0,)), ((), ())), preferred_element_type=F32))
            dp = lax.dot_general(do, vc, (((1,), (1,)), ((), ())), preferred_element_type=F32)
            ds_bands = []
            for r, pb in enumerate(bands):
                dpb = _band(dp, r)
                dsb = pb * (dpb - jnp.sum(dpb * pb, axis=-1, keepdims=True))
                acc_ref[hh, BAND_ROWS * r:BAND_ROWS * (r + 1), BAND_ROWS * r:BAND_ROWS * r + BAND_COLS] += dsb
                ds_bands.append(dsb.astype(BF16))
            dsq = _from_bands(ds_bands)
            dqs.append(jnp.dot(dsq, kc, preferred_element_type=F32) * scale)
            dks.append(lax.dot_general(dsq, q, (((0,), (0,)), ((), ())), preferred_element_type=F32))
        dq_ref[...] = jnp.concatenate(dqs, axis=1)
        dk = jnp.concatenate(dks, axis=1)
        dv = jnp.concatenate(dvs, axis=1)
        dk_ref[...] = dk[T_ATT:] + carry_ref[0]
        dv_ref[...] = dv[T_ATT:] + carry_ref[1]
        carry_ref[0] = dk[:T_ATT]
        carry_ref[1] = dv[:T_ATT]

        @pl.when(i == nb - 1)
        def _():
            for hh in range(2):
                r8 = acc_ref[hh, 0:8, :]
                for mrow in range(1, T_ATT // 8):
                    r8 = r8 + pltpu.roll(acc_ref[hh, 8 * mrow:8 * mrow + 8, :], NKEY - 8 * mrow, 1)
                tot = r8[0:1, :]
                for s in range(1, 8):
                    tot = tot + pltpu.roll(r8[s:s + 1, :], NKEY - s, 1)
                dvd_ref[0, hh:hh + 1, :] = tot

    block = lambda i: nb - 1 - i
    out = pl.BlockSpec((T_ATT, 128), lambda hp, i: (block(i), hp))
    return pl.pallas_call(
        body, name=name, grid=(ATT_HEADS // 2, nb),
        in_specs=[out] + _attn_specs(block),
        out_specs=[out, out, out, pl.BlockSpec((1, 2, NKEY), lambda hp, i: (hp, 0, 0))],
        out_shape=[jax.ShapeDtypeStruct((m, MIX), F32)] * 3 + [jax.ShapeDtypeStruct((ATT_HEADS // 2, 2, NKEY), F32)],
        scratch_shapes=[pltpu.VMEM((4, T_ATT, NKEY), F32), pltpu.VMEM((2, T_ATT, NKEY), F32), pltpu.VMEM((2, T_ATT, 128), F32)],
        compiler_params=_cparams(("parallel", "arbitrary"), VMEM_BIG),
    )(dcat, h, h, h, h, h, vdiag, mask)


def _row_tile(rows):
    for t in (512, 256, 128, 64, 32, 16, 8):
        if rows % t == 0:
            return t
    return rows


def _adamw(w, g, mom, var, *, name):
    rows, cols = w.shape
    t = _row_tile(rows)

    def body(w_ref, g_ref, m_ref, v_ref, d_ref, mo_ref, vo_ref):
        g_ = g_ref[...]
        m_ = ADAM_B1 * m_ref[...] + (1.0 - ADAM_B1) * g_
        v_ = ADAM_B2 * v_ref[...] + (1.0 - ADAM_B2) * (g_ * g_)
        m_hat = m_ / (1.0 - ADAM_B1 ** ADAM_STEP)
        v_hat = v_ / (1.0 - ADAM_B2 ** ADAM_STEP)
        d_ref[...] = -ADAM_LR * (m_hat / (jnp.sqrt(v_hat) + ADAM_EPS) + ADAM_WD * w_ref[...])
        mo_ref[...] = m_
        vo_ref[...] = v_

    spec = pl.BlockSpec((t, cols), lambda i: (i, 0))
    return pl.pallas_call(
        body, name=name, grid=(rows // t,), in_specs=[spec] * 4, out_specs=[spec] * 3,
        out_shape=[jax.ShapeDtypeStruct((rows, cols), F32)] * 3, compiler_params=_cparams(("parallel",)),
    )(w, g, mom, var)


ANY = pl.BlockSpec(memory_space=pl.ANY)


def _place():
    x, y, c = lax.axis_index("x"), lax.axis_index("y"), lax.axis_index("c")
    chips = [(1 - x, y), (x, 1 - y), (1 - x, 1 - y)]
    return x, y, c, chips


class _GatherExchange:
    def __init__(self, ws):
        n = len(ws)
        self.ins = list(ws)
        self.out_shapes = [jax.ShapeDtypeStruct((N_CHIPS,) + w.shape, w.dtype) for w in ws]
        self.sems = [pltpu.SemaphoreType.DMA((6 * n,)), pltpu.SemaphoreType.DMA((6 * n,))]

    def _copies(self, ins, outs, sems, onward=True):
        send_sems, recv_sems = sems
        x, y, c, chips = _place()
        me = 2 * x + y

        def region(k, j, chip_index, rows, to):
            ref = outs[k].at[chip_index, rows]
            return pltpu.make_async_remote_copy(
                src_ref=ref, dst_ref=ref, send_sem=send_sems.at[6 * k + j], recv_sem=recv_sems.at[6 * k + j],
                device_id=to, device_id_type=MESH)

        first, landed, passed, handed = [], [], [], []
        for k in range(len(ins)):
            half = ins[k].shape[0] // 2
            mine, theirs = pl.ds(c * half, half), pl.ds((1 - c) * half, half)
            for j, chip in enumerate(chips):
                first.append(pltpu.make_async_remote_copy(
                    src_ref=ins[k].at[mine], dst_ref=outs[k].at[me, mine], send_sem=send_sems.at[6 * k + j],
                    recv_sem=recv_sems.at[6 * k + j], device_id=(*chip, c), device_id_type=MESH))
                if onward:
                    landed.append(region(k, j, 2 * chip[0] + chip[1], mine, (*chip, c)))
                    passed.append(region(k, 3 + j, 2 * chip[0] + chip[1], mine, (x, y, 1 - c)))
                    handed.append(region(k, 3 + j, 2 * chip[0] + chip[1], theirs, (x, y, 1 - c)))
        return first, landed, passed, handed

    def start(self, ins, outs, sems):
        for cp in self._copies(ins, outs, sems, onward=False)[0]:
            cp.start()

    def finish(self, ins, outs, sems):
        first, landed, passed, handed = self._copies(ins, outs, sems)
        for arrived, onward in zip(landed, passed):
            arrived.wait_recv()
            onward.start()
        for cp in handed:
            cp.wait_recv()
        for cp in first + passed:
            cp.wait_send()


class _ReduceExchange:
    def __init__(self, grads, axes):
        self.ins = list(grads)
        self.axes = list(axes)
        n = len(grads)
        self.out_shapes = [jax.ShapeDtypeStruct((N_DEV - 1,) + self._block(g, a), g.dtype) for g, a in zip(grads, axes)]
        self.sems = [pltpu.SemaphoreType.DMA((7 * n,)), pltpu.SemaphoreType.DMA((7 * n,))]

    @staticmethod
    def _block(g, axis):
        k, n = g.shape
        return (k // 2, n // N_CHIPS) if axis == 2 else (k // N_DEV, n)

    def _copies(self, ins, outs, sems):
        send_sems, recv_sems = sems
        x, y, c, _ = _place()
        cps = []
        for w, (g, axis) in enumerate(zip(ins, self.axes)):
            rows, cols = self._block(g, axis)
            for k in range(1, N_DEV):
                tx, ty, tc = (1 - x if k & 4 else x), (1 - y if k & 2 else y), (1 - c if k & 1 else c)
                chip = 2 * tx + ty
                if axis == 2:
                    src = g.at[pl.ds(tc * rows, rows), pl.ds(chip * cols, cols)]
                else:
                    src = g.at[pl.ds((2 * chip + tc) * rows, rows), :]
                cps.append(pltpu.make_async_remote_copy(
                    src_ref=src, dst_ref=outs[w].at[k - 1], send_sem=send_sems.at[7 * w + k - 1],
                    recv_sem=recv_sems.at[7 * w + k - 1], device_id=(tx, ty, tc), device_id_type=MESH))
        return cps

    def start(self, ins, outs, sems):
        for cp in self._copies(ins, outs, sems):
            cp.start()

    def finish(self, ins, outs, sems):
        cps = self._copies(ins, outs, sems)
        for cp in cps:
            cp.wait_recv()
        for cp in cps:
            cp.wait_send()


def _run_exchange(ex, *, name):
    n_in, n_out = len(ex.ins), len(ex.out_shapes)

    def body(*refs):
        ins, outs, sems = refs[:n_in], refs[n_in:n_in + n_out], refs[n_in + n_out:]
        ex.start(ins, outs, sems)
        ex.finish(ins, outs, sems)

    return pl.pallas_call(body, name=name, in_specs=[ANY] * n_in, out_specs=[ANY] * n_out, out_shape=ex.out_shapes,
                          scratch_shapes=ex.sems)(*ex.ins)


def _all_reduce_small(buf, *, name):
    rows = buf.shape[0]

    def body(x_ref, sum_ref, all_ref, send_sems, recv_sems, local_sem):
        x, y, c, chips = _place()
        me, sibling = (x, y, c), (x, y, 1 - c)

        def slab(px, py, pc):
            return all_ref.at[pl.ds((4 * px + 2 * py + pc) * rows, rows), :]

        def copy(k, block, to, src=None):
            return pltpu.make_async_remote_copy(
                src_ref=slab(*block) if src is None else src, dst_ref=slab(*block), send_sem=send_sems.at[k],
                recv_sem=recv_sems.at[k], device_id=to, device_id_type=MESH)

        mine = pltpu.make_async_copy(x_ref, slab(*me), local_sem)
        mine.start()
        first = [copy(0, me, sibling, src=x_ref)]
        first += [copy(1 + j, me, (*chip, c), src=x_ref) for j, chip in enumerate(chips)]
        for cp in first:
            cp.start()
        passed = [copy(4 + j, (*chip, c), sibling) for j, chip in enumerate(chips)]
        for j, chip in enumerate(chips):
            copy(1 + j, (*chip, c), me).wait_recv()
            passed[j].start()
        copy(0, sibling, me).wait_recv()
        for j, chip in enumerate(chips):
            copy(4 + j, (*chip, 1 - c), me).wait_recv()
        for cp in first + passed:
            cp.wait_send()
        mine.wait()
        acc = all_ref[0:rows, :]
        for d in range(1, N_DEV):
            acc = acc + all_ref[d * rows:(d + 1) * rows, :]
        sum_ref[...] = acc

    vmem = pl.BlockSpec(memory_space=pltpu.VMEM)
    return pl.pallas_call(
        body, name=name, in_specs=[vmem], out_specs=[vmem, vmem],
        out_shape=[jax.ShapeDtypeStruct((rows, 128), F32), jax.ShapeDtypeStruct((N_DEV * rows, 128), F32)],
        scratch_shapes=[pltpu.SemaphoreType.DMA((7,)), pltpu.SemaphoreType.DMA((7,)), pltpu.SemaphoreType.DMA],
        compiler_params=pltpu.CompilerParams(vmem_limit_bytes=VMEM_BIG),
    )(buf)[0]


WEIGHTS = ['ev_w_in', 'ev_lambda_re', 'ev_lambda_im', 'ev_log_dt', 'ev_b_re', 'ev_b_im', 'ev_c_re', 'ev_c_im', 'ev_d',
           'ev_w_glu', 'ev_b_glu', 'ev_conv_w', 'ev_w_out', 'od_w_in', 'od_rel_bias', 'od_pool_w', 'od_pool_scale',
           'od_w_out', 'ln_mix_g', 'ln_mix_b', 'ln_ffn_g', 'ln_ffn_b', 'ffn_w_up', 'ffn_w_down', 'ple_w_proj',
           'ple_w_gate', 'ple_b_gate']
INPUTS = ['x', 'p'] + WEIGHTS + ['loss_target'] + ['m_' + n for n in WEIGHTS] + ['v_' + n for n in WEIGHTS]

BIG = {
    'ev_w_in': (2, (2, 1024, 2048)), 'ev_w_glu': (1, (2, 512, 512)), 'ev_w_out': (1, (2, 1024, 1024)),
    'od_w_in': (2, (2, 1024, 2048)), 'od_w_out': (1, (2, 1024, 1024)), 'ffn_w_up': (2, (4, 1024, 5632)),
    'ffn_w_down': (1, (4, 2816, 1024)), 'ple_w_proj': (2, (4, 256, 1024)), 'ple_w_gate': (1, (4, 1024, 1024)),
}
SMALL_SHARDED = {'ev_conv_w': (2, 3, 512), 'od_pool_scale': (2, 512)}
REPLICATED = [n for n in WEIGHTS if n not in BIG and n not in SMALL_SHARDED]


def _shard_rows(name):
    axis, (nl, k, n) = BIG[name]
    return (nl * k, n // N_CHIPS) if axis == 2 else (nl * k // N_CHIPS, n)


def _pack(arrs):
    flat = jnp.concatenate([a.reshape(-1) for a in arrs])
    total = flat.shape[0]
    padded = -(-total // 1024) * 1024
    return jnp.pad(flat, (0, padded - total)).reshape(padded // 128, 128)


def _unpack(buf, shapes):
    flat = buf.reshape(-1)
    out, pos = [], 0
    for s in shapes:
        size = int(np.prod(s))
        out.append(flat[pos:pos + size].reshape(s))
        pos += size
    return out


def _s5_params(lam_re, lam_im, log_dt, b_re, b_im, c_re, c_im):
    dt = jnp.exp(log_dt)[:, None]
    mag = jnp.exp(lam_re * dt)
    ang = lam_im * dt
    lb_re = mag * jnp.cos(ang)
    lb_im = mag * jnp.sin(ang)
    den = lam_re * lam_re + lam_im * lam_im
    nr = lb_re - 1.0
    ni = lb_im
    r_re = (nr * lam_re + ni * lam_im) / den
    r_im = (ni * lam_re - nr * lam_im) / den
    bb_re = r_re[..., None] * b_re - r_im[..., None] * b_im
    bb_im = r_re[..., None] * b_im + r_im[..., None] * b_re
    per = S5_GROUPS // S5_SLABS
    eye = jnp.eye(per, dtype=F32)

    def block_diag(a):
        _, r, c = a.shape
        a = a.reshape(S5_SLABS, per, r, c)
        return (a[:, :, :, None, :] * eye[None, :, None, :, None]).reshape(S5_SLABS, per * r, per * c)

    bmat = jnp.concatenate([block_diag(bb_re.transpose(0, 2, 1)), block_diag(bb_im.transpose(0, 2, 1))], axis=2)
    cmat = jnp.concatenate([block_diag(c_re.transpose(0, 2, 1)), block_diag(-c_im.transpose(0, 2, 1))], axis=1)
    lam = jnp.stack([lb_re.reshape(S5_N), lb_im.reshape(S5_N)])
    return lam, bmat, cmat


def _lam_powers(lam):
    res, ims = [lam[0]], [lam[1]]
    for _ in range(7):
        res, ims = res + [res[-1] * lam[0] - ims[-1] * lam[1]], ims + [res[-1] * lam[1] + ims[-1] * lam[0]]
    sq_r, sq_i = [res[-1]], [ims[-1]]
    for _ in range(4):
        sq_r, sq_i = sq_r + [sq_r[-1] * sq_r[-1] - sq_i[-1] * sq_i[-1]], sq_i + [2.0 * sq_r[-1] * sq_i[-1]]
    return jnp.stack(res + ims + res[::-1] + ims[::-1] + sq_r[1:] + sq_i[1:])


def _layer_big(i):
    mixer = [('w_in', 'ev_w_in'), ('w_glu', 'ev_w_glu'), ('w_out', 'ev_w_out')] if i % 2 == 0 else \
        [('w_in', 'od_w_in'), ('w_out', 'od_w_out')]
    ffn = [('w_up', 'ffn_w_up'), ('w_down', 'ffn_w_down'), ('w_proj', 'ple_w_proj'), ('w_gate', 'ple_w_gate')]
    return [(k, n, i // 2) for k, n in mixer] + [(k, n, i) for k, n in ffn]


class _WholePlan:
    def __init__(self, whole):
        self.whole = whole
        self.grads = {n: {} for n in BIG}

    def layer_weights(self, i):
        return {k: self.whole[n][l] for k, n, l in _layer_big(i)}

    def forward_host(self, i):
        return None

    def backward_host(self, i):
        return None

    def layer_grads(self, i, g):
        for k, n, l in _layer_big(i):
            self.grads[n][l] = g[k][0]


def _local_step(x, p, target, w, plan):
    mask = jnp.asarray(_band_mask())
    diag_idx = _diag_index()
    onehot = jnp.asarray(np.eye(2 * MAX_REL + 1, dtype=np.float32)[diag_idx])
    saved = []
    for i in range(DEPTH):
        li = i // 2
        lw = plan.layer_weights(i)
        s = {'x0': x, 'lw': lw}
        h = _mm([(x, 0, D_MODEL)], lw['w_in'], name=f"in_proj")
        if i % 2 == 0:
            (lam, bmat, cmat), s5_vjp = jax.vjp(
                _s5_params, w['ev_lambda_re'][li], w['ev_lambda_im'][li], w['ev_log_dt'][li], w['ev_b_re'][li],
                w['ev_b_im'][li], w['ev_c_re'][li], w['ev_c_im'][li])
            s5c = (bmat.astype(BF16), cmat.astype(BF16), w['ev_d'][li].reshape(1, MIX), lw['w_glu'],
                   w['ev_b_glu'][li].reshape(1, MIX), _lam_powers(lam))
            ya, ypre, hb = _s5_fwd(h, *s5c, name=f"s5_fwd")
            yb = _conv_fwd(h, w['ev_conv_w'][li], name=f"conv_fwd")
            s.update(s5_vjp=s5_vjp, s5c=s5c, ypre=ypre, hb=hb)
        else:
            vdiag = jnp.dot(w['od_rel_bias'][li], onehot.T, precision=HIGHEST).reshape(ATT_HEADS // 2, 2, NKEY)
            pw = w['od_pool_w'][li].astype(BF16)
            ps = w['od_pool_scale'][li].reshape(1, MIX)
            ya = _attn_fwd(h, vdiag, mask, name=f"attn_fwd")
            yb = _pool_fwd(h, pw, ps, name=f"pool_fwd")
            s.update(vdiag=vdiag, pw=pw, ps=ps)
        wout = lw['w_out']
        vec = lambda n: w[n][i].reshape(1, -1)

        def residual_ln(products, rows, vecs):
            r = ALPHA * rows[0] + products[0]
            return (r, _ln_apply(r, vecs[0], vecs[1])), ()

        def embed_gate(products, rows, vecs):
            gate = _sigmoid(products[0] + vecs[0])
            return (rows[0] + gate * products[1], gate, products[1]), ()

        two_f32 = [(D_MODEL, F32), (D_MODEL, F32)]
        r1, x1 = _mm_rows([([(ya, 0, MIX), (yb, 0, MIX)], wout, False)], [x], [vec('ln_mix_g'), vec('ln_mix_b')],
                          two_f32, [], residual_ln, name="out_proj_ln")
        hosted = plan.forward_host(i)
        if hosted is None:
            a, gg, uu = _ffn_up(x1, lw['w_up'], name=f"ffn_up")
        else:
            (a, gg, uu), arrived = _ffn_up(x1, lw['w_up'], exchange=hosted, name=f"ffn_up_gather")
            plan.forward_hosted(i, arrived)
        r2, x2 = _mm_rows([([(a, 0, D_FF)], lw['w_down'], False)], [x1], [vec('ln_ffn_g'), vec('ln_ffn_b')],
                          two_f32, [], residual_ln, name="ffn_down_ln")
        x3, gate, ppb = _mm_rows(
            [([(None, 0, D_MODEL)], lw['w_gate'], False), ([(p[i], 0, D_PLE)], lw['w_proj'], False)],
            [x2], [vec('ple_b_gate')], [(D_MODEL, F32), (D_MODEL, BF16), (D_MODEL, BF16)], [], embed_gate, name="ple")
        s.update(h=h, ya=ya, yb=yb, r1=r1, x1=x1, a=a, gg=gg, uu=uu, r2=r2, x2=x2, gate=gate, ppb=ppb)
        saved.append(s)
        x = x3

    loss, da = _loss_head(x, target, name="loss_head")
    grads = {n: [None] * (DEPTH if n.startswith(('ln_', 'ple_')) else DEPTH // 2) for n in WEIGHTS if n not in BIG}

    def both(pieces, axis):
        return tuple(jnp.concatenate([pc[k] for pc in pieces], axis=axis) for k in range(2))

    for i in reversed(range(DEPTH)):
        li = i // 2
        s = saved[i]
        lw = s['lw']
        big = {}
        dz, dpp, dr2, dbg, dg2, db2 = _ple_ln_bwd(da, s['gate'], s['ppb'], s['r2'], lw['w_gate'],
                                                  w['ln_ffn_g'][i].reshape(1, -1), name="ple_ln_bwd")
        grads['ple_b_gate'][i] = dbg.reshape(-1)
        big['w_gate'] = _mm_tn(s['x2'], 0, D_MODEL, dz, also_bf16=True, name=f"d_ple_gate")
        big['w_proj'] = _mm_tn(p[i], 0, D_PLE, dpp, also_bf16=True, name=f"d_ple_proj")
        grads['ln_ffn_g'][i] = dg2.reshape(-1)
        grads['ln_ffn_b'][i] = db2.reshape(-1)
        dhh = _ffn_down_bwd(dr2, lw['w_down'], s['gg'], s['uu'], name=f"ffn_down_bwd")
        big['w_down'] = _mm_tn(s['a'], 0, D_FF, dr2, tk=D_FF // 2, also_bf16=True, name=f"d_ffn_down")
        hosted = plan.backward_host(i)
        if hosted is None:
            big['w_up'] = _mm_tn(s['x1'], 0, D_MODEL, dhh, tn=D_FF // 2, also_bf16=True, name=f"d_ffn_up")
        else:
            big['w_up'], arrived = _mm_tn(s['x1'], 0, D_MODEL, dhh, tn=D_FF // 2, also_bf16=True, exchange=hosted,
                                          name=f"d_ffn_up_reduce_{i % 2}")
            plan.backward_hosted(i, arrived)

        def ln_mix_grad(products, rows, vecs):
            dr, dg, dbias = _ln_grad(rows[0], ALPHA * rows[1] + products[0], vecs[0])
            return (dr,), (dg, dbias)

        dr1, dg1, db1 = _mm_rows([([(dhh, 0, 2 * D_FF)], lw['w_up'], True)], [s['r1'], dr2],
                                 [w['ln_mix_g'][i].reshape(1, -1)], [(D_MODEL, F32)], [D_MODEL, D_MODEL], ln_mix_grad,
                                 tm=TM, vmem=VMEM_BIG, name="ffn_up_ln_bwd")
        grads['ln_mix_g'][i] = dg1.reshape(-1)
        grads['ln_mix_b'][i] = db1.reshape(-1)
        dcat = _mm([(dr1, 0, D_MODEL)], lw['w_out'], trans_b=True, name=f"out_proj_bwd")
        big['w_out'] = both([_mm_tn(s['ya'], 0, MIX, dr1, also_bf16=True, name=f"d_out_a"),
                             _mm_tn(s['yb'], 0, MIX, dr1, also_bf16=True, name=f"d_out_b")], 0)
        h = s['h']
        if i % 2 == 0:
            s5c = s['s5c']
            du, xb, gb, gq, dzzq, dyq, dlam, dbglu, dd = _s5_bwd(dcat, s['ypre'], h, s['hb'], *s5c, name=f"s5_bwd")
            dbmat = _mm_tn_slabs(h, 128, gb, SLAB_COLS, S5_SLABS, name=f"d_s5_b")
            dcmat = _mm_tn_slabs(xb, SLAB_COLS, dyq, 128, S5_SLABS, name=f"d_s5_c")
            s5g = s['s5_vjp']((dlam, dbmat, dcmat))
            for n, g_ in zip(['ev_lambda_re', 'ev_lambda_im', 'ev_log_dt', 'ev_b_re', 'ev_b_im', 'ev_c_re', 'ev_c_im'], s5g):
                grads[n][li] = g_
            big['w_glu'] = _mm_tn(gq, 0, MIX, dzzq, also_bf16=True, name=f"d_glu")
            grads['ev_b_glu'][li] = dbglu.reshape(-1)
            grads['ev_d'][li] = dd.reshape(-1)
            d3, dcw = _conv_bwd(dcat, h, w['ev_conv_w'][li], name=f"conv_bwd")
            grads['ev_conv_w'][li] = dcw[0:3]
            big['w_in'] = both([_mm_tn(s['x0'], 0, D_MODEL, du, also_bf16=True, name=f"d_in_a"),
                                _mm_tn(s['x0'], 0, D_MODEL, d3, tn=3 * MIX, also_bf16=True, name=f"d_in_b")], 1)
            dh_parts = [(du, 0, MIX), (d3, 0, 3 * MIX)]
        else:
            dq, dk, dv, dvd = _attn_bwd(dcat, h, s['vdiag'], mask, name=f"attn_bwd")
            dzp, dpw, dps = _pool_bwd(dcat, h, s['pw'], s['ps'], name=f"pool_bwd")
            parts = [dq, dk, dv, dzp]
            grads['od_rel_bias'][li] = jnp.dot(dvd.reshape(ATT_HEADS, NKEY), onehot, precision=HIGHEST)
            grads['od_pool_w'][li] = dpw
            grads['od_pool_scale'][li] = dps.reshape(-1)
            big['w_in'] = both([_mm_tn(s['x0'], 0, D_MODEL, d_, also_bf16=True, name=f"d_in_a") for d_ in parts], 1)
            dh_parts = [(d_, 0, MIX) for d_ in parts]
        plan.layer_grads(i, big)

        def layer_input_grad(products, rows, vecs):
            return (ALPHA * rows[0] + products[0],), ()

        (da,) = _mm_rows([(dh_parts, lw['w_in'], True)], [dr1], [], [(D_MODEL, F32)], [], layer_input_grad,
                         name=f"in_proj_bwd_{i % 2}")
    return loss, da, {n: jnp.stack(g) for n, g in grads.items()}


def _sum_blocks(own, others, *, name):
    rows, cols = own.shape
    t = _row_tile(rows)

    def body(own_ref, others_ref, o_ref):
        acc = own_ref[...]
        for k in range(N_DEV - 1):
            acc = acc + others_ref[k].astype(F32)
        o_ref[...] = acc

    return pl.pallas_call(
        body, name=name, grid=(rows // t,),
        in_specs=[pl.BlockSpec((t, cols), lambda i: (i, 0)), pl.BlockSpec((N_DEV - 1, t, cols), lambda i: (0, i, 0))],
        out_specs=pl.BlockSpec((t, cols), lambda i: (i, 0)), out_shape=jax.ShapeDtypeStruct((rows, cols), F32),
        compiler_params=_cparams(("parallel",)),
    )(own, others)


def _swap_sibling(arrs, *, name):
    n = len(arrs)

    def body(*refs):
        ins, outs = refs[:n], refs[n:2 * n]
        send_sems, recv_sems = refs[2 * n:]
        x, y, c, _ = _place()
        cps = [pltpu.make_async_remote_copy(src_ref=ins[k], dst_ref=outs[k], send_sem=send_sems.at[k], recv_sem=recv_sems.at[k],
                                            device_id=(x, y, 1 - c), device_id_type=MESH) for k in range(n)]
        for cp in cps:
            cp.start()
        for cp in cps:
            cp.wait_recv()
        for cp in cps:
            cp.wait_send()

    return pl.pallas_call(
        body, name=name, in_specs=[ANY] * n, out_specs=[ANY] * n,
        out_shape=[jax.ShapeDtypeStruct(a.shape, a.dtype) for a in arrs],
        scratch_shapes=[pltpu.SemaphoreType.DMA((n,)), pltpu.SemaphoreType.DMA((n,))],
    )(*arrs)


class _ShardedPlan:
    def __init__(self, a, c, me):
        self.a, self.c, self.me = a, c, me
        self.weights, self.pending, self.own, self.arrived = {}, None, {}, {}

    def _shards(self, i):
        return [self.a[n][l].astype(BF16) for _, n, l in _layer_big(i)]

    def _with_own(self, gathered, own):
        return lax.dynamic_update_index_in_dim(gathered, own, self.me, 0)

    def _set_weights(self, i, gathered):
        lw = {}
        for (k, n, _), g, own in zip(_layer_big(i), gathered, self._shards(i)):
            _, rows, cols = g.shape
            g = self._with_own(g, own)
            lw[k] = g.transpose(1, 0, 2).reshape(rows, N_CHIPS * cols) if BIG[n][0] == 2 else g.reshape(N_CHIPS * rows, cols)
        self.weights[i] = lw

    def gather_first(self, misc):
        gathered = _run_exchange(_GatherExchange(self._shards(0) + [misc]), name="weight_gather_0")
        self._set_weights(0, gathered[:-1])
        return self._with_own(gathered[-1], misc)

    def layer_weights(self, i):
        return self.weights.pop(i)

    def forward_host(self, i):
        return _GatherExchange(self._shards(i + 1)) if i + 1 < DEPTH else None

    def forward_hosted(self, i, arrived):
        self._set_weights(i + 1, arrived)

    def _reduce_exchange(self):
        i, g = self.pending
        return _ReduceExchange([g[k][1] for k, _, _ in _layer_big(i)], [BIG[n][0] for _, n, _ in _layer_big(i)])

    def layer_grads(self, i, g):
        for k, n, l in _layer_big(i):
            full = g[k][0]
            kk, nn = full.shape
            if BIG[n][0] == 2:
                self.own[n, l] = lax.dynamic_slice(full, (self.c * (kk // 2), self.me * (nn // N_CHIPS)), (kk // 2, nn // N_CHIPS))
            else:
                self.own[n, l] = lax.dynamic_slice_in_dim(full, (2 * self.me + self.c) * (kk // N_DEV), kk // N_DEV, axis=0)
        self.pending = (i, g)

    def backward_host(self, i):
        return self._reduce_exchange() if i + 1 < DEPTH else None

    def backward_hosted(self, i, arrived):
        for (_, n, l), r in zip(_layer_big(i + 1), arrived):
            self.arrived[n, l] = r

    def reduced(self):
        for (_, n, l), r in zip(_layer_big(0), _run_exchange(self._reduce_exchange(), name="grad_reduce_0")):
            self.arrived[n, l] = r
        keys = [(n, l) for n in BIG for l in range(BIG[n][1][0])]
        mine = [_sum_blocks(self.own[k], self.arrived[k], name=f"grad_sum_{k[0]}") for k in keys]
        theirs = _swap_sibling(mine, name="grad_half_swap")
        out = {}
        for n in BIG:
            layers = []
            for l in range(BIG[n][1][0]):
                a_, b_ = mine[keys.index((n, l))], theirs[keys.index((n, l))]
                layers.append(jnp.where(self.c == 0, jnp.concatenate([a_, b_], axis=0), jnp.concatenate([b_, a_], axis=0)))
            out[n] = jnp.stack(layers)
        return out


def kernel(x, p, ev_w_in, ev_lambda_re, ev_lambda_im, ev_log_dt, ev_b_re, ev_b_im, ev_c_re, ev_c_im, ev_d, ev_w_glu, ev_b_glu, ev_conv_w, ev_w_out, od_w_in, od_rel_bias, od_pool_w, od_pool_scale, od_w_out, ln_mix_g, ln_mix_b, ln_ffn_g, ln_ffn_b, ffn_w_up, ffn_w_down, ple_w_proj, ple_w_gate, ple_b_gate, loss_target, m_ev_w_in, m_ev_lambda_re, m_ev_lambda_im, m_ev_log_dt, m_ev_b_re, m_ev_b_im, m_ev_c_re, m_ev_c_im, m_ev_d, m_ev_w_glu, m_ev_b_glu, m_ev_conv_w, m_ev_w_out, m_od_w_in, m_od_rel_bias, m_od_pool_w, m_od_pool_scale, m_od_w_out, m_ln_mix_g, m_ln_mix_b, m_ln_ffn_g, m_ln_ffn_b, m_ffn_w_up, m_ffn_w_down, m_ple_w_proj, m_ple_w_gate, m_ple_b_gate, v_ev_w_in, v_ev_lambda_re, v_ev_lambda_im, v_ev_log_dt, v_ev_b_re, v_ev_b_im, v_ev_c_re, v_ev_c_im, v_ev_d, v_ev_w_glu, v_ev_b_glu, v_ev_conv_w, v_ev_w_out, v_od_w_in, v_od_rel_bias, v_od_pool_w, v_od_pool_scale, v_od_w_out, v_ln_mix_g, v_ln_mix_b, v_ln_ffn_g, v_ln_ffn_b, v_ffn_w_up, v_ffn_w_down, v_ple_w_proj, v_ple_w_gate, v_ple_b_gate):
    given = locals()
    a = {n: given[n] for n in INPUTS}
    x, y, c = lax.axis_index("x"), lax.axis_index("y"), lax.axis_index("c")
    me = 2 * x + y

    plan = _ShardedPlan(a, c, me)
    misc = jnp.concatenate([a['ev_conv_w'].reshape(6, 128), a['od_pool_scale'], jnp.zeros((8, 128), F32)], axis=0)
    gm = plan.gather_first(misc)
    w = {n: a[n] for n in REPLICATED}
    w['ev_conv_w'] = gm[:, 0:6].reshape(N_CHIPS, 2, 3, 128).transpose(1, 2, 0, 3).reshape(2, 3, 512)
    w['od_pool_scale'] = gm[:, 6:8].transpose(1, 0, 2).reshape(2, 512)

    loss, grad_x, grads = _local_step(a['x'][0], a['p'][:, 0], a['loss_target'][0], w, plan)
    loss = lax.psum(loss[0, 0], ("x", "y", "c"))

    small_names = REPLICATED + list(SMALL_SHARDED)
    small = _all_reduce_small(_pack([grads[n] for n in small_names]), name="small_grad_all_reduce")
    small = dict(zip(small_names, _unpack(small, [grads[n].shape for n in small_names])))
    for n in SMALL_SHARDED:
        small[n] = lax.dynamic_slice_in_dim(small[n], me * 128, 128, axis=small[n].ndim - 1)
    big = plan.reduced()

    res = {}
    for n in BIG:
        shape = a[n].shape
        flat = _shard_rows(n)
        d, m_, v_ = _adamw(a[n].reshape(flat), big[n].reshape(flat), a['m_' + n].reshape(flat), a['v_' + n].reshape(flat),
                           name=f"adamw_{n}")
        res[n] = (big[n], d.reshape(shape), m_.reshape(shape), v_.reshape(shape))
    shapes = [a[n].shape for n in small_names]
    d, m_, v_ = _adamw(_pack([a[n] for n in small_names]), _pack([small[n] for n in small_names]),
                       _pack([a['m_' + n] for n in small_names]), _pack([a['v_' + n] for n in small_names]), name="adamw_small")
    for n, dd, mm, vv in zip(small_names, _unpack(d, shapes), _unpack(m_, shapes), _unpack(v_, shapes)):
        res[n] = (small[n], dd, mm, vv)

    outs = [loss, grad_x[None]]
    for part in range(4):
        outs += [res[n][part] for n in WEIGHTS]
    return tuple(outs)
```

```python
import functools
import math

import jax
import jax.numpy as jnp
import numpy as np
from jax import lax
from jax.experimental import pallas as pl
from jax.experimental.pallas import tpu as pltpu

F32 = jnp.float32
BF16 = jnp.bfloat16
MESH = pl.DeviceIdType.MESH
HIGHEST = lax.Precision.HIGHEST

D_MODEL = 1024
DEPTH = 4
MIX = 512
S5_GROUPS = 32
S5_GROUP = 16
S5_STATE = 64
S5_N = S5_GROUPS * S5_STATE
CHUNK = 64
LEFT_CHUNKS = 8
MAX_REL = 128
ATT_HEADS = 8
HEAD_DIM = 64
POOL_WINDOWS = (2, 4, 8, 16)
POOL_GROUP = 128
D_FF = 2816
D_PLE = 256
ALPHA = (2 * DEPTH) ** 0.25
LN_EPS = 1e-5
NEG_INF = -1e30
N_CHIPS = 4
N_DEV = 8

ADAM_LR = 0.001
ADAM_B1 = 0.9
ADAM_B2 = 0.999
ADAM_EPS = 1e-08
ADAM_WD = 0.01
ADAM_STEP = 10

TM = 512
T_S5 = 256
T_ATT = 512
VMEM_BIG = 56 * 1024 * 1024


VMEM_DEFAULT = 48 * 1024 * 1024


def _cparams(sem, vmem=None):
    return pltpu.CompilerParams(dimension_semantics=sem, vmem_limit_bytes=vmem or VMEM_DEFAULT)


def _sigmoid(x):
    return 0.5 + 0.5 * jnp.tanh(0.5 * x)


def _mm(a_parts, b, *, name, trans_b=False, out_dtype=F32, tm=TM, tn=1024, vmem=None):
    m = a_parts[0][0].shape[0]
    n = b.shape[0] if trans_b else b.shape[1]
    kk = b.shape[1] if trans_b else b.shape[0]
    tn = min(tn, n)
    widths = [w for _, _, w in a_parts]
    assert sum(widths) == kk and m % tm == 0 and n % tn == 0
    na = len(a_parts)

    def body(*refs):
        b_ref, o_ref = refs[na], refs[na + 1]
        acc = None
        k0 = 0
        for ar, w in zip(refs[:na], widths):
            a = ar[...].astype(BF16)
            if trans_b:
                part = lax.dot_general(a, b_ref[:, k0:k0 + w], (((1,), (1,)), ((), ())), preferred_element_type=F32)
            else:
                part = jnp.dot(a, b_ref[k0:k0 + w, :], preferred_element_type=F32)
            acc = part if acc is None else acc + part
            k0 += w
        o_ref[...] = acc.astype(o_ref.dtype)

    in_specs = [pl.BlockSpec((tm, w), functools.partial(lambda j, i, cb: (i, cb), cb=cb)) for _, cb, w in a_parts]
    if trans_b:
        in_specs.append(pl.BlockSpec((tn, kk), lambda j, i: (j, 0)))
    else:
        in_specs.append(pl.BlockSpec((kk, tn), lambda j, i: (0, j)))
    return pl.pallas_call(
        body, name=name, grid=(n // tn, m // tm), in_specs=in_specs,
        out_specs=pl.BlockSpec((tm, tn), lambda j, i: (i, j)),
        out_shape=jax.ShapeDtypeStruct((m, n), out_dtype),
        compiler_params=_cparams(("parallel", "parallel"), vmem),
    )(*[a for a, _, _ in a_parts], b)


def _host_parts(exchange):
    if exchange is None:
        return [], [], [], [], []
    any_space = pl.BlockSpec(memory_space=pl.ANY)
    return (exchange.ins, [any_space] * len(exchange.ins), [any_space] * len(exchange.out_shapes),
            list(exchange.out_shapes), list(exchange.sems))


def _host_run(exchange, refs, first, last):
    if exchange is None:
        return
    n_in, n_out = len(exchange.ins), len(exchange.out_shapes)
    ins, outs, sems = refs[:n_in], refs[n_in:n_in + n_out], refs[n_in + n_out:]

    @pl.when(first)
    def _():
        exchange.start(ins, outs, sems)

    @pl.when(last)
    def _():
        exchange.finish(ins, outs, sems)


def _mm_tn(a, a_cb, ka, b, *, name, tk=1024, tn=1024, tmr=2 * TM, vmem=None, also_bf16=False, exchange=None):
    m = a.shape[0]
    n = b.shape[1]
    tk = min(tk, ka)
    tn = min(tn, n)
    assert ka % tk == 0 and n % tn == 0 and m % tmr == 0
    kb = ka // tk
    grid = (kb, n // tn, m // tmr)
    ex_ops, ex_in_specs, ex_out_specs, ex_out_shapes, ex_scratch = _host_parts(exchange)
    n_own_out = 2 if also_bf16 else 1

    def body(*refs):
        a_ref, b_ref = refs[:2]
        hosted_in = refs[2:2 + len(ex_ops)]
        outs = refs[2 + len(ex_ops):]
        o_ref = outs[0]
        k, j, r = pl.program_id(0), pl.program_id(1), pl.program_id(2)
        _host_run(exchange, list(hosted_in) + list(outs[n_own_out:]),
                  (k == 0) & (j == 0) & (r == 0), (k == grid[0] - 1) & (j == grid[1] - 1) & (r == grid[2] - 1))

        @pl.when(r == 0)
        def _():
            o_ref[...] = jnp.zeros_like(o_ref)

        o_ref[...] += lax.dot_general(a_ref[...].astype(BF16), b_ref[...].astype(BF16), (((0,), (0,)), ((), ())),
                                      preferred_element_type=F32)
        if also_bf16:
            @pl.when(r == grid[2] - 1)
            def _():
                outs[1][...] = o_ref[...].astype(BF16)

    tile = pl.BlockSpec((tk, tn), lambda k, j, r: (k, j))
    res = pl.pallas_call(
        body, name=name, grid=grid,
        in_specs=[pl.BlockSpec((tmr, tk), lambda k, j, r: (r, a_cb * kb + k)),
                  pl.BlockSpec((tmr, tn), lambda k, j, r: (r, j))] + ex_in_specs,
        out_specs=[tile] * n_own_out + ex_out_specs,
        out_shape=[jax.ShapeDtypeStruct((ka, n), F32)] + ([jax.ShapeDtypeStruct((ka, n), BF16)] if also_bf16 else [])
        + ex_out_shapes,
        scratch_shapes=ex_scratch,
        compiler_params=_cparams(("arbitrary",) * 3 if exchange is not None else ("parallel", "parallel", "arbitrary"), vmem),
    )(a, b, *ex_ops)
    if exchange is None:
        return tuple(res) if also_bf16 else res[0]
    own = tuple(res[:n_own_out]) if also_bf16 else res[0]
    return own, list(res[n_own_out:])


def _mm_tn_slabs(a, ka, b, nbw, nslab, *, name, tmr=2 * TM):
    m = a.shape[0]
    assert m % tmr == 0

    def body(a_ref, b_ref, o_ref):
        @pl.when(pl.program_id(1) == 0)
        def _():
            o_ref[...] = jnp.zeros_like(o_ref)

        o_ref[0] += lax.dot_general(a_ref[...].astype(BF16), b_ref[...].astype(BF16), (((0,), (0,)), ((), ())),
                                    preferred_element_type=F32)

    return pl.pallas_call(
        body, name=name, grid=(nslab, m // tmr),
        in_specs=[pl.BlockSpec((tmr, ka), lambda s, r: (r, s)), pl.BlockSpec((tmr, nbw), lambda s, r: (r, s))],
        out_specs=pl.BlockSpec((1, ka, nbw), lambda s, r: (s, 0, 0)),
        out_shape=jax.ShapeDtypeStruct((nslab, ka, nbw), F32),
        compiler_params=_cparams(("parallel", "arbitrary")),
    )(a, b)


def _ln_stats(r):
    mu = jnp.mean(r, axis=-1, keepdims=True)
    xc = r - mu
    var = jnp.mean(xc * xc, axis=-1, keepdims=True)
    rstd = lax.rsqrt(var + LN_EPS)
    return xc * rstd, rstd


def _ln_apply(r, g, b):
    xhat, _ = _ln_stats(r)
    return xhat * g + b


def _ln_grad(r, dy, g):
    xhat, rstd = _ln_stats(r)
    dxh = dy * g
    m1 = jnp.mean(dxh, axis=-1, keepdims=True)
    m2 = jnp.mean(dxh * xhat, axis=-1, keepdims=True)
    return (rstd * (dxh - m1 - xhat * m2), jnp.sum(dy * xhat, axis=0, keepdims=True), jnp.sum(dy, axis=0, keepdims=True))


def _mm_rows(matmuls, rows_in, vecs_in, out_rows, acc_widths, fn, *, name, tm=TM, vmem=None):
    m = rows_in[0].shape[0]
    assert m % tm == 0
    flat, in_specs, layout = [], [], []
    for a_parts, b, trans_b in matmuls:
        own = [(arr, cb, w) for arr, cb, w in a_parts if arr is not None]
        for arr, cb, w in own:
            flat.append(arr)
            in_specs.append(pl.BlockSpec((tm, w), functools.partial(lambda i, cb: (i, cb), cb=cb)))
        flat.append(b)
        in_specs.append(pl.BlockSpec(b.shape, lambda i: (0, 0)))
        layout.append(([(arr is None, cb, w) for arr, cb, w in a_parts], len(own), trans_b))
    first_row = len(flat)
    for r in rows_in:
        flat.append(r)
        in_specs.append(pl.BlockSpec((tm, r.shape[1]), lambda i: (i, 0)))
    for v in vecs_in:
        flat.append(v)
        in_specs.append(pl.BlockSpec(v.shape, lambda i: (0, 0)))
    n_in = len(flat)
    n_rows_out = len(out_rows)

    def body(*refs):
        rows = [r[...] for r in refs[first_row:first_row + len(rows_in)]]
        vecs = [v[...] for v in refs[first_row + len(rows_in):n_in]]
        pos = 0
        products = []
        for parts, n_own, trans_b in layout:
            b_ref = refs[pos + n_own]
            own_refs = iter(refs[pos:pos + n_own])
            acc, k0 = None, 0
            for is_row, cb, w in parts:
                a = (rows[cb] if is_row else next(own_refs)[...]).astype(BF16)
                if trans_b:
                    part = lax.dot_general(a, b_ref[:, k0:k0 + w], (((1,), (1,)), ((), ())), preferred_element_type=F32)
                else:
                    part = jnp.dot(a, b_ref[k0:k0 + w, :], preferred_element_type=F32)
                acc = part if acc is None else acc + part
                k0 += w
            products.append(acc)
            pos += n_own + 1
        outs, sums = fn(products, rows, vecs)
        for o_ref, o in zip(refs[n_in:n_in + n_rows_out], outs):
            o_ref[...] = o.astype(o_ref.dtype)
        if acc_widths:
            acc_refs = refs[n_in + n_rows_out:]

            @pl.when(pl.program_id(0) == 0)
            def _():
                for a_ref in acc_refs:
                    a_ref[...] = jnp.zeros_like(a_ref)

            for a_ref, s_ in zip(acc_refs, sums):
                a_ref[...] += s_

    out_specs = [pl.BlockSpec((tm, n), lambda i: (i, 0)) for n, _ in out_rows]
    out_specs += [pl.BlockSpec((1, wd), lambda i: (0, 0)) for wd in acc_widths]
    out_shape = [jax.ShapeDtypeStruct((m, n), dt) for n, dt in out_rows]
    out_shape += [jax.ShapeDtypeStruct((1, wd), F32) for wd in acc_widths]
    return pl.pallas_call(
        body, name=name, grid=(m // tm,), in_specs=in_specs, out_specs=out_specs, out_shape=out_shape,
        compiler_params=_cparams(("arbitrary",) if acc_widths else ("parallel",), vmem),
    )(*flat)


def _ffn_up(x1, wup, *, name, exchange=None):
    m = x1.shape[0]
    tn = D_FF // 2
    grid = (2, m // TM)
    ex_ops, ex_in_specs, ex_out_specs, ex_out_shapes, ex_scratch = _host_parts(exchange)

    def body(*refs):
        x_ref, wg_ref, wu_ref = refs[:3]
        hosted_in = refs[3:3 + len(ex_ops)]
        a_ref, g_ref, u_ref = refs[3 + len(ex_ops):6 + len(ex_ops)]
        j, i = pl.program_id(0), pl.program_id(1)
        _host_run(exchange, list(hosted_in) + list(refs[6 + len(ex_ops):]),
                  (j == 0) & (i == 0), (j == grid[0] - 1) & (i == grid[1] - 1))
        x = x_ref[...].astype(BF16)
        g = jnp.dot(x, wg_ref[...], preferred_element_type=F32)
        u = jnp.dot(x, wu_ref[...], preferred_element_type=F32)
        a_ref[...] = (g * _sigmoid(g) * u).astype(BF16)
        g_ref[...] = g.astype(BF16)
        u_ref[...] = u.astype(BF16)

    out = pl.BlockSpec((TM, tn), lambda j, i: (i, j))
    res = pl.pallas_call(
        body, name=name, grid=grid,
        in_specs=[pl.BlockSpec((TM, D_MODEL), lambda j, i: (i, 0)),
                  pl.BlockSpec((D_MODEL, tn), lambda j, i: (0, j)),
                  pl.BlockSpec((D_MODEL, tn), lambda j, i: (0, j + 2))] + ex_in_specs,
        out_specs=[out, out, out] + ex_out_specs,
        out_shape=[jax.ShapeDtypeStruct((m, D_FF), BF16)] * 3 + ex_out_shapes,
        scratch_shapes=ex_scratch,
        compiler_params=_cparams(("arbitrary", "arbitrary") if exchange is not None else ("parallel", "parallel")),
    )(x1, wup, wup, *ex_ops)
    return (res[0], res[1], res[2]) if exchange is None else ((res[0], res[1], res[2]), list(res[3:]))


def _ffn_down_bwd(df, wdown, g, u, *, name):
    m = df.shape[0]
    tm = TM
    chunk = 256

    def body(df_ref, w_ref, g_ref, u_ref, o_ref):
        df = df_ref[...].astype(BF16)
        for part in range(D_FF // chunk):
            cols = slice(chunk * part, chunk * (part + 1))
            da = lax.dot_general(df, w_ref[cols, :], (((1,), (1,)), ((), ())), preferred_element_type=F32)
            gg = g_ref[:, cols].astype(F32)
            sg = _sigmoid(gg)
            o_ref[:, cols] = (da * u_ref[:, cols].astype(F32) * (sg * (1.0 + gg * (1.0 - sg)))).astype(BF16)
            o_ref[:, D_FF + chunk * part:D_FF + chunk * (part + 1)] = (da * (gg * sg)).astype(BF16)

    return pl.pallas_call(
        body, name=name, grid=(m // tm,),
        in_specs=[pl.BlockSpec((tm, D_MODEL), lambda i: (i, 0)), pl.BlockSpec((D_FF, D_MODEL), lambda i: (0, 0)),
                  pl.BlockSpec((tm, D_FF), lambda i: (i, 0)), pl.BlockSpec((tm, D_FF), lambda i: (i, 0))],
        out_specs=pl.BlockSpec((tm, 2 * D_FF), lambda i: (i, 0)),
        out_shape=jax.ShapeDtypeStruct((m, 2 * D_FF), BF16),
        compiler_params=_cparams(("parallel",), VMEM_BIG),
    )(df, wdown, g, u)


def _ple_ln_bwd(dx3, gate, pp, r2, wgate, g2, *, name):
    m, n = dx3.shape

    def body(dx3_ref, gate_ref, pp_ref, r_ref, w_ref, g_ref, dz_ref, dpp_ref, dr_ref, dbg_ref, dg_ref, dbias_ref):
        dx3 = dx3_ref[...]

        @pl.when(pl.program_id(0) == 0)
        def _():
            dbg_ref[...] = jnp.zeros_like(dbg_ref)
            dg_ref[...] = jnp.zeros_like(dg_ref)
            dbias_ref[...] = jnp.zeros_like(dbias_ref)

        gate = gate_ref[...].astype(F32)
        dz = dx3 * pp_ref[...].astype(F32) * gate * (1.0 - gate)
        dzq = dz.astype(BF16)
        dz_ref[...] = dzq
        dpp_ref[...] = (dx3 * gate).astype(BF16)
        dbg_ref[...] += jnp.sum(dz, axis=0, keepdims=True)
        dx2 = dx3 + lax.dot_general(dzq, w_ref[...], (((1,), (1,)), ((), ())), preferred_element_type=F32)
        dr, dg, dbias = _ln_grad(r_ref[...], dx2, g_ref[...])
        dr_ref[...] = dr
        dg_ref[...] += dg
        dbias_ref[...] += dbias

    row = pl.BlockSpec((TM, n), lambda i: (i, 0))
    vec = pl.BlockSpec((1, n), lambda i: (0, 0))
    in_specs = [row] * 4 + [pl.BlockSpec(wgate.shape, lambda i: (0, 0)), vec]
    return pl.pallas_call(
        body, name=name, grid=(m // TM,), in_specs=in_specs, out_specs=[row, row, row, vec, vec, vec],
        out_shape=[jax.ShapeDtypeStruct((m, n), BF16), jax.ShapeDtypeStruct((m, n), BF16), jax.ShapeDtypeStruct((m, n), F32)]
        + [jax.ShapeDtypeStruct((1, n), F32)] * 3,
        compiler_params=_cparams(("arbitrary",)),
    )(dx3, gate, pp, r2, wgate, g2)


def _loss_head(y, target, *, name):
    m, n = y.shape

    def body(y_ref, t_ref, loss_ref, dy_ref):
        @pl.when(pl.program_id(0) == 0)
        def _():
            loss_ref[...] = jnp.zeros_like(loss_ref)

        err = y_ref[...] - t_ref[...]
        dy_ref[...] = err * (1.0 / n)
        per_tok = jnp.mean(err * err, axis=-1, keepdims=True)
        loss_ref[...] += 0.5 * jnp.sum(per_tok, axis=0, keepdims=True)

    row = pl.BlockSpec((TM, n), lambda i: (i, 0))
    return pl.pallas_call(
        body, name=name, grid=(m // TM,), in_specs=[row, row],
        out_specs=[pl.BlockSpec((1, 1), lambda i: (0, 0)), row],
        out_shape=[jax.ShapeDtypeStruct((1, 1), F32), jax.ShapeDtypeStruct((m, n), F32)],
        compiler_params=_cparams(("arbitrary",)),
    )(y, target)


def _gelu(y):
    c = math.sqrt(2.0 / math.pi)
    return 0.5 * y * (1.0 + jnp.tanh(c * (y + 0.044715 * y * y * y)))


def _gelu_grad(y):
    c = math.sqrt(2.0 / math.pi)
    t = jnp.tanh(c * (y + 0.044715 * y * y * y))
    return 0.5 * (1.0 + t) + 0.5 * y * (1.0 - t * t) * c * (1.0 + 3.0 * 0.044715 * y * y)


STRIP = 128
S5_SLABS = 4
SLAB_COLS = 2 * S5_N // S5_SLABS
N_TILES = 2 * S5_N // STRIP
SLAB_TILES = SLAB_COLS // STRIP


def _strip_tiles(j):
    re_tile = (j // (SLAB_TILES // 2)) * SLAB_TILES + j % (SLAB_TILES // 2)
    return pl.multiple_of(j * STRIP, STRIP), re_tile, re_tile + SLAB_TILES // 2


def _store_tiles(ref, first_tile, value):
    for k in range(value.shape[1] // STRIP):
        ref[first_tile + k] = value[:, STRIP * k:STRIP * (k + 1)]


def _load_tiles(ref, first_tile, count):
    return jnp.concatenate([ref[first_tile + k] for k in range(count)], axis=1)


PTAB_ROWS = 40
GROUPS = T_S5 // 8


def _scan_cols(ref, hr, hi, ptab_ref, off, down, visit=None):
    sign = 1.0 if down else -1.0
    ref_r, ref_i = ref
    cols_p = pl.ds(off, STRIP)

    def power(row, im_offset=8):
        return ptab_ref[row:row + 1, cols_p], sign * ptab_ref[row + im_offset:row + im_offset + 1, cols_p]

    def rows(r):
        return pl.ds(r, GROUPS, stride=8)

    def mul_add(br, bi, qr, qi, vr, vi):
        return br + qr * vr - qi * vi, bi + qr * vi + qi * vr

    order = list(range(8)) if down else list(range(7, -1, -1))
    lam_r, lam_i = power(0)
    vr, vi = ref_r[rows(order[0]), :], ref_i[rows(order[0]), :]
    for r in order[1:]:
        vr, vi = mul_add(ref_r[rows(r), :], ref_i[rows(r), :], lam_r, lam_i, vr, vi)
        ref_r[rows(r), :] = vr
        ref_i[rows(r), :] = vi
    grow = lax.broadcasted_iota(jnp.int32, (GROUPS, STRIP), 0)
    edge = 0 if down else GROUPS - 1
    l8r, l8i = power(7)
    er = vr + jnp.where(grow == edge, l8r * hr - l8i * hi, 0.0)
    ei = vi + jnp.where(grow == edge, l8r * hi + l8i * hr, 0.0)
    k, step = 0, 1
    while step < GROUPS:
        qr, qi = (l8r, l8i) if k == 0 else power(32 + k - 1, 4)
        if down:
            sr = jnp.where(grow >= step, pltpu.roll(er, step, 0), 0.0)
            si = jnp.where(grow >= step, pltpu.roll(ei, step, 0), 0.0)
        else:
            sr = jnp.where(grow < GROUPS - step, pltpu.roll(er, GROUPS - step, 0), 0.0)
            si = jnp.where(grow < GROUPS - step, pltpu.roll(ei, GROUPS - step, 0), 0.0)
        er, ei = mul_add(er, ei, qr, qi, sr, si)
        k, step = k + 1, 2 * step
    if down:
        cr = jnp.where(grow == 0, hr, pltpu.roll(er, 1, 0))
        ci = jnp.where(grow == 0, hi, pltpu.roll(ei, 1, 0))
    else:
        cr = jnp.where(grow == GROUPS - 1, hr, pltpu.roll(er, GROUPS - 1, 0))
        ci = jnp.where(grow == GROUPS - 1, hi, pltpu.roll(ei, GROUPS - 1, 0))
    for r in range(8):
        qr, qi = power(r if down else 16 + r)
        xr, xi = mul_add(ref_r[rows(r), :], ref_i[rows(r), :], qr, qi, cr, ci)
        ref_r[rows(r), :] = xr
        ref_i[rows(r), :] = xi
        if visit is not None:
            visit(r, xr, xi)
    last = GROUPS - 1 if down else 0
    return er[last:last + 1], ei[last:last + 1]


def _s5_fwd(h, bmat, cmat, dvec, wglu, bglu, ptab, *, name):
    m = h.shape[0]
    t = T_S5
    nb = m // t

    def body(u_ref, bmat_ref, cmat_ref, d_ref, wglu_ref, bglu_ref, ptab_ref,
             out_ref, y_ref, hb_ref, bu_ref, carry_ref):
        @pl.when(pl.program_id(0) == 0)
        def _():
            carry_ref[...] = jnp.zeros_like(carry_ref)

        hb_ref[0] = carry_ref[...]
        u = u_ref[...]
        ub = u.astype(BF16)
        for s in range(S5_SLABS):
            _store_tiles(bu_ref, SLAB_TILES * s,
                         jnp.dot(ub[:, 128 * s:128 * (s + 1)], bmat_ref[s], preferred_element_type=F32))

        def strip(j, c):
            off, tr, ti = _strip_tiles(j)
            cols_r, cols_i = pl.ds(pl.multiple_of(tr * STRIP, STRIP), STRIP), pl.ds(pl.multiple_of(ti * STRIP, STRIP), STRIP)
            er, ei = _scan_cols((bu_ref.at[tr], bu_ref.at[ti]), carry_ref[0:1, cols_r], carry_ref[0:1, cols_i],
                                ptab_ref, off, True)
            carry_ref[0:1, cols_r] = er
            carry_ref[0:1, cols_i] = ei
            return c

        lax.fori_loop(0, S5_N // STRIP, strip, 0)
        y = jnp.concatenate(
            [jnp.dot(_load_tiles(bu_ref, SLAB_TILES * s, SLAB_TILES).astype(BF16), cmat_ref[s], preferred_element_type=F32)
             for s in range(S5_SLABS)], axis=1) + d_ref[...] * u
        y_ref[...] = y
        g = _gelu(y)
        zz = jnp.dot(g.astype(BF16), wglu_ref[...], preferred_element_type=F32) + bglu_ref[...]
        out_ref[...] = (g * _sigmoid(zz)).astype(BF16)

    const = lambda shape: pl.BlockSpec(shape, lambda i: (0,) * len(shape))
    row_spec = pl.BlockSpec((t, MIX), lambda i: (i, 0))
    return pl.pallas_call(
        body, name=name, grid=(nb,),
        in_specs=[row_spec, const((S5_SLABS, 128, SLAB_COLS)), const((S5_SLABS, SLAB_COLS, 128)), const((1, MIX)),
                  const((MIX, MIX)), const((1, MIX)), const((PTAB_ROWS, S5_N))],
        out_specs=[row_spec, row_spec, pl.BlockSpec((1, 1, 2 * S5_N), lambda i: (i, 0, 0))],
        out_shape=[jax.ShapeDtypeStruct((m, MIX), BF16), jax.ShapeDtypeStruct((m, MIX), F32),
                   jax.ShapeDtypeStruct((nb, 1, 2 * S5_N), F32)],
        scratch_shapes=[pltpu.VMEM((N_TILES, t, STRIP), F32), pltpu.VMEM((1, 2 * S5_N), F32)],
        compiler_params=_cparams(("arbitrary",), VMEM_BIG),
    )(h, bmat, cmat, dvec, wglu, bglu, ptab)


def _s5_bwd(dcat, ypre, h, hb, bmat, cmat, dvec, wglu, bglu, ptab, *, name, exchange=None):
    m = h.shape[0]
    t = T_S5
    nb = m // t
    ex_ops, ex_in_specs, ex_out_specs, ex_out_shapes, ex_scratch = _host_parts(exchange)
    n_ex = len(ex_ops)

    def body(*refs):
        dya_ref, y_ref, u_ref, hb_ref, bmat_ref, cmat_ref, d_ref, wglu_ref, bglu_ref, ptab_ref = refs[:10]
        du_ref, xb_ref, gb_ref, gq_ref, dzz_ref, dyq_ref, dlam_ref, dbglu_ref, dd_ref = refs[10 + n_ex:19 + n_ex]
        hosted_out = refs[19 + n_ex:19 + n_ex + len(ex_out_shapes)]
        bu_ref, dx_ref, gcarry_ref = refs[19 + n_ex + len(ex_out_shapes):22 + n_ex + len(ex_out_shapes)]
        _host_run(exchange, list(refs[10:10 + n_ex]) + list(hosted_out) + list(refs[22 + n_ex + len(ex_out_shapes):]),
                  pl.program_id(0) == 0, pl.program_id(0) == nb - 1)

        @pl.when(pl.program_id(0) == 0)
        def _():
            gcarry_ref[...] = jnp.zeros_like(gcarry_ref)
            dlam_ref[...] = jnp.zeros_like(dlam_ref)
            dbglu_ref[...] = jnp.zeros_like(dbglu_ref)
            dd_ref[...] = jnp.zeros_like(dd_ref)

        u = u_ref[...]
        y = y_ref[...]
        g = _gelu(y)
        gq = g.astype(BF16)
        sg = _sigmoid(jnp.dot(gq, wglu_ref[...], preferred_element_type=F32) + bglu_ref[...])
        dout = dya_ref[...]
        dzz = dout * g * sg * (1.0 - sg)
        dzzq = dzz.astype(BF16)
        dg = dout * sg + lax.dot_general(dzzq, wglu_ref[...], (((1,), (1,)), ((), ())), preferred_element_type=F32)
        dy = dg * _gelu_grad(y)
        dyq = dy.astype(BF16)
        gq_ref[...] = gq
        dzz_ref[...] = dzzq
        dyq_ref[...] = dyq
        dbglu_ref[...] += jnp.sum(dzz, axis=0, keepdims=True)
        dd_ref[...] += jnp.sum(dy * u, axis=0, keepdims=True)

        ub = u.astype(BF16)
        nt = (((1,), (1,)), ((), ()))
        for s in range(S5_SLABS):
            _store_tiles(dx_ref, SLAB_TILES * s,
                         lax.dot_general(dyq[:, 128 * s:128 * (s + 1)], cmat_ref[s], nt, preferred_element_type=F32))
            _store_tiles(bu_ref, SLAB_TILES * s,
                         jnp.dot(ub[:, 128 * s:128 * (s + 1)], bmat_ref[s], preferred_element_type=F32))
        grow = lax.broadcasted_iota(jnp.int32, (GROUPS, STRIP), 0)

        def strip(j, c):
            off, tr, ti = _strip_tiles(j)
            cols_r, cols_i = pl.ds(pl.multiple_of(tr * STRIP, STRIP), STRIP), pl.ds(pl.multiple_of(ti * STRIP, STRIP), STRIP)
            x_r, x_i = bu_ref.at[tr], bu_ref.at[ti]
            hr = hb_ref[0, 0:1, cols_r]
            hi = hb_ref[0, 0:1, cols_i]
            _scan_cols((x_r, x_i), hr, hi, ptab_ref, off, True)
            xb_ref[:, cols_r] = x_r[...].astype(BF16)
            xb_ref[:, cols_i] = x_i[...].astype(BF16)
            sums = [jnp.zeros((1, STRIP), F32), jnp.zeros((1, STRIP), F32)]

            def d_lam(r, gr, gi):
                if r == 0:
                    pr_ = jnp.where(grow == 0, hr, pltpu.roll(x_r[pl.ds(7, GROUPS, stride=8), :], 1, 0))
                    pi_ = jnp.where(grow == 0, hi, pltpu.roll(x_i[pl.ds(7, GROUPS, stride=8), :], 1, 0))
                else:
                    pr_ = x_r[pl.ds(r - 1, GROUPS, stride=8), :]
                    pi_ = x_i[pl.ds(r - 1, GROUPS, stride=8), :]
                sums[0] = sums[0] + jnp.sum(pr_ * gr + pi_ * gi, axis=0, keepdims=True)
                sums[1] = sums[1] + jnp.sum(pr_ * gi - pi_ * gr, axis=0, keepdims=True)

            g_r, g_i = dx_ref.at[tr], dx_ref.at[ti]
            gr0, gi0 = _scan_cols((g_r, g_i), gcarry_ref[0:1, cols_r], gcarry_ref[0:1, cols_i], ptab_ref, off, False, d_lam)
            gb_ref[:, cols_r] = g_r[...].astype(BF16)
            gb_ref[:, cols_i] = g_i[...].astype(BF16)
            gcarry_ref[0:1, cols_r] = gr0
            gcarry_ref[0:1, cols_i] = gi0
            dlam_ref[0:1, pl.ds(off, STRIP)] += sums[0]
            dlam_ref[1:2, pl.ds(off, STRIP)] += sums[1]
            return c

        lax.fori_loop(0, S5_N // STRIP, strip, 0)
        du_ref[...] = dy * d_ref[...] + jnp.concatenate(
            [lax.dot_general(gb_ref[:, SLAB_COLS * s:SLAB_COLS * (s + 1)], bmat_ref[s], nt, preferred_element_type=F32)
             for s in range(S5_SLABS)], axis=1)

    const = lambda shape: pl.BlockSpec(shape, lambda i: (0,) * len(shape))
    rev = lambda i: (nb - 1 - i, 0)
    row_spec = pl.BlockSpec((t, MIX), rev)
    wide = pl.BlockSpec((t, 2 * S5_N), rev)
    res = pl.pallas_call(
        body, name=name, grid=(nb,),
        in_specs=[row_spec, row_spec, row_spec, pl.BlockSpec((1, 1, 2 * S5_N), lambda i: (nb - 1 - i, 0, 0)),
                  const((S5_SLABS, 128, SLAB_COLS)), const((S5_SLABS, SLAB_COLS, 128)), const((1, MIX)), const((MIX, MIX)),
                  const((1, MIX)), const((PTAB_ROWS, S5_N))] + ex_in_specs,
        out_specs=[row_spec, wide, wide, row_spec, row_spec, row_spec, const((2, S5_N)), const((1, MIX)), const((1, MIX))]
        + ex_out_specs,
        out_shape=[jax.ShapeDtypeStruct((m, MIX), F32), jax.ShapeDtypeStruct((m, 2 * S5_N), BF16),
                   jax.ShapeDtypeStruct((m, 2 * S5_N), BF16), jax.ShapeDtypeStruct((m, MIX), BF16),
                   jax.ShapeDtypeStruct((m, MIX), BF16), jax.ShapeDtypeStruct((m, MIX), BF16),
                   jax.ShapeDtypeStruct((2, S5_N), F32), jax.ShapeDtypeStruct((1, MIX), F32), jax.ShapeDtypeStruct((1, MIX), F32)]
        + ex_out_shapes,
        scratch_shapes=[pltpu.VMEM((N_TILES, t, STRIP), F32), pltpu.VMEM((N_TILES, t, STRIP), F32),
                        pltpu.VMEM((1, 2 * S5_N), F32)] + ex_scratch,
        compiler_params=_cparams(("arbitrary",), VMEM_BIG),
    )(dcat, ypre, h, hb, bmat, cmat, dvec, wglu, bglu, ptab, *ex_ops)
    return res[:9] if exchange is None else (res[:9], list(res[9:]))


HALO = 8


def _taps_down(zext, t):
    return pltpu.roll(zext, 1, 0)[HALO:HALO + t], pltpu.roll(zext, 2, 0)[HALO:HALO + t]


def _conv_z(c_ref, x_ref, cp_ref, xp_ref, first, t):
    z = c_ref[...] * x_ref[...]
    zp = jnp.where(first, 0.0, cp_ref[t - HALO:t, :] * xp_ref[t - HALO:t, :])
    z1, z2 = _taps_down(jnp.concatenate([zp, z], axis=0), t)
    return z, z1, z2


def _conv_fwd(h, cw, *, name):
    m = h.shape[0]
    t = TM
    nb = m // t

    def body(b_ref, c_ref, x_ref, cp_ref, xp_ref, w_ref, o_ref):
        z, z1, z2 = _conv_z(c_ref, x_ref, cp_ref, xp_ref, pl.program_id(0) == 0, t)
        o_ref[...] = (b_ref[...] * (w_ref[0:1, :] * z2 + w_ref[1:2, :] * z1 + w_ref[2:3, :] * z)).astype(BF16)

    cur = lambda cb: pl.BlockSpec((t, MIX), lambda i: (i, cb))
    prev = lambda cb: pl.BlockSpec((t, MIX), lambda i: (jnp.maximum(i - 1, 0), cb))
    return pl.pallas_call(
        body, name=name, grid=(nb,),
        in_specs=[cur(1), cur(2), cur(3), prev(2), prev(3), pl.BlockSpec((3, MIX), lambda i: (0, 0))],
        out_specs=pl.BlockSpec((t, MIX), lambda i: (i, 0)),
        out_shape=jax.ShapeDtypeStruct((m, MIX), BF16),
        compiler_params=_cparams(("parallel",)),
    )(h, h, h, h, h, cw)


def _conv_bwd(dcat, h, cw, *, name):
    m = h.shape[0]
    t = TM
    nb = m // t

    def body(dy_ref, dyn_ref, b_ref, c_ref, x_ref, cp_ref, xp_ref, bn_ref, w_ref, o_ref, dw_ref):
        i = pl.program_id(0)

        @pl.when(i == 0)
        def _():
            dw_ref[...] = jnp.zeros_like(dw_ref)

        z, z1, z2 = _conv_z(c_ref, x_ref, cp_ref, xp_ref, i == 0, t)
        w0, w1, w2 = w_ref[0:1, :], w_ref[1:2, :], w_ref[2:3, :]
        dy = dy_ref[...]
        dconv = dy * b_ref[...]
        dnext = jnp.where(i == nb - 1, 0.0, dyn_ref[0:HALO, :] * bn_ref[0:HALO, :])
        dext = jnp.concatenate([dconv, dnext], axis=0)
        d1 = pltpu.roll(dext, t + HALO - 1, 0)[0:t]
        d2 = pltpu.roll(dext, t + HALO - 2, 0)[0:t]
        dz = w2 * dconv + w1 * d1 + w0 * d2
        o_ref[:, 0:MIX] = dy * (w0 * z2 + w1 * z1 + w2 * z)
        o_ref[:, MIX:2 * MIX] = dz * x_ref[...]
        o_ref[:, 2 * MIX:3 * MIX] = dz * c_ref[...]
        dw_ref[0:1, :] += jnp.sum(dconv * z2, axis=0, keepdims=True)
        dw_ref[1:2, :] += jnp.sum(dconv * z1, axis=0, keepdims=True)
        dw_ref[2:3, :] += jnp.sum(dconv * z, axis=0, keepdims=True)

    cur = lambda cb: pl.BlockSpec((t, MIX), lambda i: (i, cb))
    prev = lambda cb: pl.BlockSpec((t, MIX), lambda i: (jnp.maximum(i - 1, 0), cb))
    nxt = lambda cb: pl.BlockSpec((t, MIX), lambda i: (jnp.minimum(i + 1, nb - 1), cb))
    return pl.pallas_call(
        body, name=name, grid=(nb,),
        in_specs=[cur(1), nxt(1), cur(1), cur(2), cur(3), prev(2), prev(3), nxt(1), pl.BlockSpec((3, MIX), lambda i: (0, 0))],
        out_specs=[pl.BlockSpec((t, 3 * MIX), lambda i: (i, 0)), pl.BlockSpec((8, MIX), lambda i: (0, 0))],
        out_shape=[jax.ShapeDtypeStruct((m, 3 * MIX), F32), jax.ShapeDtypeStruct((8, MIX), F32)],
        compiler_params=_cparams(("arbitrary",)),
    )(dcat, dcat, h, h, h, h, h, h, cw)


PHALO = 16


def _pool_pooled(z_ref, zp_ref, i, t):
    z = z_ref[...]
    zp = jnp.where(i == 0, 0.0, zp_ref[t - PHALO:t, :])
    s = jnp.concatenate([zp, z], axis=0)
    sums = {}
    width = 1
    while width < PHALO:
        s = s + pltpu.roll(s, width, 0)
        width *= 2
        sums[width] = s[PHALO:PHALO + t]
    tpos = i * t + lax.broadcasted_iota(jnp.int32, (t, 1), 0)
    outs = []
    for gi, w in enumerate(POOL_WINDOWS):
        lo = gi * POOL_GROUP
        count = jnp.minimum(tpos + 1, w).astype(F32)
        outs.append(sums[w][:, lo:lo + POOL_GROUP] / count - z[:, lo:lo + POOL_GROUP])
    return outs


def _pool_fwd(h, pw, ps, *, name):
    m = h.shape[0]
    t = TM
    nb = m // t

    def body(z_ref, zp_ref, pw_ref, ps_ref, o_ref):
        pooled = _pool_pooled(z_ref, zp_ref, pl.program_id(0), t)
        for gi in range(len(POOL_WINDOWS)):
            lo = gi * POOL_GROUP
            mixed = jnp.dot(pooled[gi].astype(BF16), pw_ref[gi], preferred_element_type=F32)
            o_ref[:, lo:lo + POOL_GROUP] = (mixed * ps_ref[:, lo:lo + POOL_GROUP]).astype(BF16)

    return pl.pallas_call(
        body, name=name, grid=(nb,),
        in_specs=[pl.BlockSpec((t, MIX), lambda i: (i, 3)), pl.BlockSpec((t, MIX), lambda i: (jnp.maximum(i - 1, 0), 3)),
                  pl.BlockSpec((4, POOL_GROUP, POOL_GROUP), lambda i: (0, 0, 0)), pl.BlockSpec((1, MIX), lambda i: (0, 0))],
        out_specs=pl.BlockSpec((t, MIX), lambda i: (i, 0)),
        out_shape=jax.ShapeDtypeStruct((m, MIX), BF16),
        compiler_params=_cparams(("parallel",)),
    )(h, h, pw, ps)


def _pool_bwd(dcat, h, pw, ps, *, name):
    m = h.shape[0]
    t = TM
    nb = m // t

    def body(dy_ref, dyn_ref, z_ref, zp_ref, pw_ref, ps_ref, dz_ref, dpw_ref, dps_ref):
        i = pl.program_id(0)

        @pl.when(i == 0)
        def _():
            dpw_ref[...] = jnp.zeros_like(dpw_ref)
            dps_ref[...] = jnp.zeros_like(dps_ref)

        pooled = _pool_pooled(z_ref, zp_ref, i, t)
        dy = dy_ref[...]
        tpos = i * t + lax.broadcasted_iota(jnp.int32, (t, 1), 0)
        for gi, w in enumerate(POOL_WINDOWS):
            lo = gi * POOL_GROUP
            sl = slice(lo, lo + POOL_GROUP)
            pq = pooled[gi].astype(BF16)
            mixed = jnp.dot(pq, pw_ref[gi], preferred_element_type=F32)
            dps_ref[:, sl] += jnp.sum(dy[:, sl] * mixed, axis=0, keepdims=True)
            dmix = (dy[:, sl] * ps_ref[:, sl]).astype(BF16)
            dpw_ref[gi] += lax.dot_general(pq, dmix, (((0,), (0,)), ((), ())), preferred_element_type=F32)
            dpool = lax.dot_general(dmix, pw_ref[gi], (((1,), (1,)), ((), ())), preferred_element_type=F32)
            dmix_n = (dyn_ref[0:PHALO, sl] * ps_ref[:, sl]).astype(BF16)
            dpool_n = lax.dot_general(dmix_n, pw_ref[gi], (((1,), (1,)), ((), ())), preferred_element_type=F32)
            e = dpool / jnp.minimum(tpos + 1, w).astype(F32)
            e_n = jnp.where(i == nb - 1, 0.0, dpool_n * (1.0 / w))
            f = jnp.concatenate([e, e_n], axis=0)
            width = 1
            while width < w:
                f = f + pltpu.roll(f, t + PHALO - width, 0)
                width *= 2
            dz_ref[:, sl] = f[0:t] - dpool

    return pl.pallas_call(
        body, name=name, grid=(nb,),
        in_specs=[pl.BlockSpec((t, MIX), lambda i: (i, 1)), pl.BlockSpec((t, MIX), lambda i: (jnp.minimum(i + 1, nb - 1), 1)),
                  pl.BlockSpec((t, MIX), lambda i: (i, 3)), pl.BlockSpec((t, MIX), lambda i: (jnp.maximum(i - 1, 0), 3)),
                  pl.BlockSpec((4, POOL_GROUP, POOL_GROUP), lambda i: (0, 0, 0)), pl.BlockSpec((1, MIX), lambda i: (0, 0))],
        out_specs=[pl.BlockSpec((t, MIX), lambda i: (i, 0)), pl.BlockSpec((4, POOL_GROUP, POOL_GROUP), lambda i: (0, 0, 0)),
                   pl.BlockSpec((1, MIX), lambda i: (0, 0))],
        out_shape=[jax.ShapeDtypeStruct((m, MIX), F32), jax.ShapeDtypeStruct((4, POOL_GROUP, POOL_GROUP), F32),
                   jax.ShapeDtypeStruct((1, MIX), F32)],
        compiler_params=_cparams(("arbitrary",)),
    )(dcat, dcat, h, h, pw, ps)


NKEY = 2 * T_ATT


def _band_mask():
    qc = np.arange(T_ATT)[:, None] // CHUNK
    kc = np.arange(NKEY)[None, :] // CHUNK - LEFT_CHUNKS
    return np.where((kc <= qc) & (kc >= qc - LEFT_CHUNKS), 0.0, NEG_INF).astype(np.float32)


def _diag_index():
    c = np.arange(NKEY)
    d = np.where(c <= NKEY // 2 + CHUNK, T_ATT - c, T_ATT + NKEY - c)
    return np.clip(d, -MAX_REL, MAX_REL) + MAX_REL


def _bias_tile(vd_ref, mask_ref, tile_ref):
    col = lax.broadcasted_iota(jnp.int32, (8, NKEY), 1)
    no_prev = jnp.where(col < T_ATT, NEG_INF, 0.0)
    for hh in range(2):
        v = vd_ref[0, hh:hh + 1, :]
        base = jnp.concatenate([v if s == 0 else pltpu.roll(v, s, 1) for s in range(8)], axis=0)
        for mrow in range(T_ATT // 8):
            rows = slice(8 * mrow, 8 * mrow + 8)
            blk = (base if mrow == 0 else pltpu.roll(base, 8 * mrow, 1)) + mask_ref[rows, :]
            tile_ref[hh, rows, :] = blk
            tile_ref[2 + hh, rows, :] = blk + no_prev


BAND_ROWS = 2 * CHUNK
BAND_COLS = (LEFT_CHUNKS + 2) * CHUNK
N_BANDS = T_ATT // BAND_ROWS


def _band(x, r):
    return x[BAND_ROWS * r:BAND_ROWS * (r + 1), BAND_ROWS * r:BAND_ROWS * r + BAND_COLS]


def _from_bands(parts):
    rows = []
    for r, part in enumerate(parts):
        right = NKEY - BAND_COLS - BAND_ROWS * r
        pieces = ([jnp.zeros((BAND_ROWS, BAND_ROWS * r), part.dtype)] if r else []) + [part]
        pieces += [jnp.zeros((BAND_ROWS, right), part.dtype)] if right else []
        rows.append(jnp.concatenate(pieces, axis=1))
    return jnp.concatenate(rows, axis=0)


def _attn_probs(q, kc, tile_ref, idx):
    s = lax.dot_general(q, kc, (((1,), (1,)), ((), ())), preferred_element_type=F32)
    parts = []
    for r in range(N_BANDS):
        sb = _band(s, r) + tile_ref[idx, BAND_ROWS * r:BAND_ROWS * (r + 1), BAND_ROWS * r:BAND_ROWS * r + BAND_COLS]
        p = jnp.exp(sb - jnp.max(sb, axis=-1, keepdims=True))
        parts.append(p * (1.0 / jnp.sum(p, axis=-1, keepdims=True)))
    return parts


def _attn_specs(block):
    cur = lambda base: pl.BlockSpec((T_ATT, 128), lambda hp, i: (block(i), base + hp))
    prev = lambda base: pl.BlockSpec((T_ATT, 128), lambda hp, i: (jnp.maximum(block(i) - 1, 0), base + hp))
    return [cur(0), cur(4), prev(4), cur(8), prev(8),
            pl.BlockSpec((1, 2, NKEY), lambda hp, i: (hp, 0, 0)), pl.BlockSpec((T_ATT, NKEY), lambda hp, i: (0, 0))]


def _attn_fwd(h, vdiag, mask, *, name):
    m = h.shape[0]
    nb = m // T_ATT

    def body(q_ref, k_ref, kp_ref, v_ref, vp_ref, vd_ref, mask_ref, o_ref, tile_ref):
        i = pl.program_id(1)

        @pl.when(i == 0)
        def _():
            _bias_tile(vd_ref, mask_ref, tile_ref)

        first = jnp.where(i == 0, 2, 0)
        outs = []
        for hh in range(2):
            sl = slice(hh * HEAD_DIM, (hh + 1) * HEAD_DIM)
            q = (q_ref[:, sl] * (HEAD_DIM ** -0.5)).astype(BF16)
            kc = jnp.concatenate([kp_ref[:, sl], k_ref[:, sl]], axis=0).astype(BF16)
            vc = jnp.concatenate([vp_ref[:, sl], v_ref[:, sl]], axis=0).astype(BF16)
            p = _from_bands([b.astype(BF16) for b in _attn_probs(q, kc, tile_ref, first + hh)])
            outs.append(jnp.dot(p, vc, preferred_element_type=F32))
        o_ref[...] = jnp.concatenate(outs, axis=1).astype(BF16)

    return pl.pallas_call(
        body, name=name, grid=(ATT_HEADS // 2, nb), in_specs=_attn_specs(lambda i: i),
        out_specs=pl.BlockSpec((T_ATT, 128), lambda hp, i: (i, hp)),
        out_shape=jax.ShapeDtypeStruct((m, MIX), BF16),
        scratch_shapes=[pltpu.VMEM((4, T_ATT, NKEY), F32)],
        compiler_params=_cparams(("parallel", "arbitrary"), VMEM_BIG),
    )(h, h, h, h, h, vdiag, mask)


def _attn_bwd(dcat, h, vdiag, mask, *, name):
    m = h.shape[0]
    nb = m // T_ATT

    def body(do_ref, q_ref, k_ref, kp_ref, v_ref, vp_ref, vd_ref, mask_ref,
             dq_ref, dk_ref, dv_ref, dvd_ref, tile_ref, acc_ref, carry_ref):
        i = pl.program_id(1)

        @pl.when(i == 0)
        def _():
            _bias_tile(vd_ref, mask_ref, tile_ref)
            acc_ref[...] = jnp.zeros_like(acc_ref)

            carry_ref[...] = jnp.zeros_like(carry_ref)

        scale = HEAD_DIM ** -0.5
        first = jnp.where(i == nb - 1, 2, 0)
        dqs, dks, dvs = [], [], []
        for hh in range(2):
            sl = slice(hh * HEAD_DIM, (hh + 1) * HEAD_DIM)
            q = (q_ref[:, sl] * scale).astype(BF16)
            kc = jnp.concatenate([kp_ref[:, sl], k_ref[:, sl]], axis=0).astype(BF16)
            vc = jnp.concatenate([vp_ref[:, sl], v_ref[:, sl]], axis=0).astype(BF16)
            do = do_ref[:, sl].astype(BF16)
            bands = _attn_probs(q, kc, tile_ref, first + hh)
            p = _from_bands([b.astype(BF16) for b in bands])
            dvs.append(lax.dot_general(p, do, (((0,), (0,)), ((), ())), preferred_element_type=F32))
            dp = lax.dot_general(do, vc, (((1,), (1,)), ((), ())), preferred_element_type=F32)
            ds_bands = []
            for r, pb in enumerate(bands):
                dpb = _band(dp, r)
                dsb = pb * (dpb - jnp.sum(dpb * pb, axis=-1, keepdims=True))
                acc_ref[hh, BAND_ROWS * r:BAND_ROWS * (r + 1), BAND_ROWS * r:BAND_ROWS * r + BAND_COLS] += dsb
                ds_bands.append(dsb.astype(BF16))
            dsq = _from_bands(ds_bands)
            dqs.append(jnp.dot(dsq, kc, preferred_element_type=F32) * scale)
            dks.append(lax.dot_general(dsq, q, (((0,), (0,)), ((), ())), preferred_element_type=F32))
        dq_ref[...] = jnp.concatenate(dqs, axis=1)
        dk = jnp.concatenate(dks, axis=1)
        dv = jnp.concatenate(dvs, axis=1)
        dk_ref[...] = dk[T_ATT:] + carry_ref[0]
        dv_ref[...] = dv[T_ATT:] + carry_ref[1]
        carry_ref[0] = dk[:T_ATT]
        carry_ref[1] = dv[:T_ATT]

        @pl.when(i == nb - 1)
        def _():
            for hh in range(2):
                r8 = acc_ref[hh, 0:8, :]
                for mrow in range(1, T_ATT // 8):
                    r8 = r8 + pltpu.roll(acc_ref[hh, 8 * mrow:8 * mrow + 8, :], NKEY - 8 * mrow, 1)
                tot = r8[0:1, :]
                for s in range(1, 8):
                    tot = tot + pltpu.roll(r8[s:s + 1, :], NKEY - s, 1)
                dvd_ref[0, hh:hh + 1, :] = tot

    block = lambda i: nb - 1 - i
    out = pl.BlockSpec((T_ATT, 128), lambda hp, i: (block(i), hp))
    return pl.pallas_call(
        body, name=name, grid=(ATT_HEADS // 2, nb),
        in_specs=[out] + _attn_specs(block),
        out_specs=[out, out, out, pl.BlockSpec((1, 2, NKEY), lambda hp, i: (hp, 0, 0))],
        out_shape=[jax.ShapeDtypeStruct((m, MIX), F32)] * 3 + [jax.ShapeDtypeStruct((ATT_HEADS // 2, 2, NKEY), F32)],
        scratch_shapes=[pltpu.VMEM((4, T_ATT, NKEY), F32), pltpu.VMEM((2, T_ATT, NKEY), F32), pltpu.VMEM((2, T_ATT, 128), F32)],
        compiler_params=_cparams(("parallel", "arbitrary"), VMEM_BIG),
    )(dcat, h, h, h, h, h, vdiag, mask)


def _row_tile(rows):
    for t in (512, 256, 128, 64, 32, 16, 8):
        if rows % t == 0:
            return t
    return rows


def _adamw(w, g, mom, var, *, name):
    rows, cols = w.shape
    t = _row_tile(rows)

    def body(w_ref, g_ref, m_ref, v_ref, d_ref, mo_ref, vo_ref):
        g_ = g_ref[...]
        m_ = ADAM_B1 * m_ref[...] + (1.0 - ADAM_B1) * g_
        v_ = ADAM_B2 * v_ref[...] + (1.0 - ADAM_B2) * (g_ * g_)
        m_hat = m_ / (1.0 - ADAM_B1 ** ADAM_STEP)
        v_hat = v_ / (1.0 - ADAM_B2 ** ADAM_STEP)
        d_ref[...] = -ADAM_LR * (m_hat / (jnp.sqrt(v_hat) + ADAM_EPS) + ADAM_WD * w_ref[...])
        mo_ref[...] = m_
        vo_ref[...] = v_

    spec = pl.BlockSpec((t, cols), lambda i: (i, 0))
    return pl.pallas_call(
        body, name=name, grid=(rows // t,), in_specs=[spec] * 4, out_specs=[spec] * 3,
        out_shape=[jax.ShapeDtypeStruct((rows, cols), F32)] * 3, compiler_params=_cparams(("parallel",)),
    )(w, g, mom, var)


ANY = pl.BlockSpec(memory_space=pl.ANY)


def _place():
    x, y, c = lax.axis_index("x"), lax.axis_index("y"), lax.axis_index("c")
    chips = [(1 - x, y), (x, 1 - y), (1 - x, 1 - y)]
    return x, y, c, chips


class _GatherExchange:
    def __init__(self, ws):
        n = len(ws)
        self.ins = list(ws)
        self.out_shapes = [jax.ShapeDtypeStruct((N_CHIPS,) + w.shape, w.dtype) for w in ws]
        self.sems = [pltpu.SemaphoreType.DMA((6 * n,)), pltpu.SemaphoreType.DMA((6 * n,))]

    def _copies(self, ins, outs, sems, onward=True):
        send_sems, recv_sems = sems
        x, y, c, chips = _place()
        me = 2 * x + y

        def region(k, j, chip_index, rows, to):
            ref = outs[k].at[chip_index, rows]
            return pltpu.make_async_remote_copy(
                src_ref=ref, dst_ref=ref, send_sem=send_sems.at[6 * k + j], recv_sem=recv_sems.at[6 * k + j],
                device_id=to, device_id_type=MESH)

        first, landed, passed, handed = [], [], [], []
        for k in range(len(ins)):
            half = ins[k].shape[0] // 2
            mine, theirs = pl.ds(c * half, half), pl.ds((1 - c) * half, half)
            for j, chip in enumerate(chips):
                first.append(pltpu.make_async_remote_copy(
                    src_ref=ins[k].at[mine], dst_ref=outs[k].at[me, mine], send_sem=send_sems.at[6 * k + j],
                    recv_sem=recv_sems.at[6 * k + j], device_id=(*chip, c), device_id_type=MESH))
                if onward:
                    landed.append(region(k, j, 2 * chip[0] + chip[1], mine, (*chip, c)))
                    passed.append(region(k, 3 + j, 2 * chip[0] + chip[1], mine, (x, y, 1 - c)))
                    handed.append(region(k, 3 + j, 2 * chip[0] + chip[1], theirs, (x, y, 1 - c)))
        return first, landed, passed, handed

    def start(self, ins, outs, sems):
        for cp in self._copies(ins, outs, sems, onward=False)[0]:
            cp.start()

    def finish(self, ins, outs, sems):
        first, landed, passed, handed = self._copies(ins, outs, sems)
        for arrived, onward in zip(landed, passed):
            arrived.wait_recv()
            onward.start()
        for cp in handed:
            cp.wait_recv()
        for cp in first + passed:
            cp.wait_send()


class _ReduceExchange:
    def __init__(self, grads, axes):
        self.ins = list(grads)
        self.axes = list(axes)
        n = len(grads)
        self.out_shapes = [jax.ShapeDtypeStruct((N_DEV - 1,) + self._block(g, a), g.dtype) for g, a in zip(grads, axes)]
        self.sems = [pltpu.SemaphoreType.DMA((7 * n,)), pltpu.SemaphoreType.DMA((7 * n,))]

    @staticmethod
    def _block(g, axis):
        k, n = g.shape
        return (k // 2, n // N_CHIPS) if axis == 2 else (k // N_DEV, n)

    def _copies(self, ins, outs, sems):
        send_sems, recv_sems = sems
        x, y, c, _ = _place()
        cps = []
        for w, (g, axis) in enumerate(zip(ins, self.axes)):
            rows, cols = self._block(g, axis)
            for k in range(1, N_DEV):
                tx, ty, tc = (1 - x if k & 4 else x), (1 - y if k & 2 else y), (1 - c if k & 1 else c)
                chip = 2 * tx + ty
                if axis == 2:
                    src = g.at[pl.ds(tc * rows, rows), pl.ds(chip * cols, cols)]
                else:
                    src = g.at[pl.ds((2 * chip + tc) * rows, rows), :]
                cps.append(pltpu.make_async_remote_copy(
                    src_ref=src, dst_ref=outs[w].at[k - 1], send_sem=send_sems.at[7 * w + k - 1],
                    recv_sem=recv_sems.at[7 * w + k - 1], device_id=(tx, ty, tc), device_id_type=MESH))
        return cps

    def start(self, ins, outs, sems):
        for cp in self._copies(ins, outs, sems):
            cp.start()

    def finish(self, ins, outs, sems):
        cps = self._copies(ins, outs, sems)
        for cp in cps:
            cp.wait_recv()
        for cp in cps:
            cp.wait_send()


def _run_exchange(ex, *, name):
    n_in, n_out = len(ex.ins), len(ex.out_shapes)

    def body(*refs):
        ins, outs, sems = refs[:n_in], refs[n_in:n_in + n_out], refs[n_in + n_out:]
        ex.start(ins, outs, sems)
        ex.finish(ins, outs, sems)

    return pl.pallas_call(body, name=name, in_specs=[ANY] * n_in, out_specs=[ANY] * n_out, out_shape=ex.out_shapes,
                          scratch_shapes=ex.sems)(*ex.ins)


def _all_reduce_small(buf, *, name):
    rows = buf.shape[0]

    def body(x_ref, sum_ref, all_ref, send_sems, recv_sems, local_sem):
        x, y, c, chips = _place()
        me, sibling = (x, y, c), (x, y, 1 - c)

        def slab(px, py, pc):
            return all_ref.at[pl.ds((4 * px + 2 * py + pc) * rows, rows), :]

        def copy(k, block, to, src=None):
            return pltpu.make_async_remote_copy(
                src_ref=slab(*block) if src is None else src, dst_ref=slab(*block), send_sem=send_sems.at[k],
                recv_sem=recv_sems.at[k], device_id=to, device_id_type=MESH)

        mine = pltpu.make_async_copy(x_ref, slab(*me), local_sem)
        mine.start()
        first = [copy(0, me, sibling, src=x_ref)]
        first += [copy(1 + j, me, (*chip, c), src=x_ref) for j, chip in enumerate(chips)]
        for cp in first:
            cp.start()
        passed = [copy(4 + j, (*chip, c), sibling) for j, chip in enumerate(chips)]
        for j, chip in enumerate(chips):
            copy(1 + j, (*chip, c), me).wait_recv()
            passed[j].start()
        copy(0, sibling, me).wait_recv()
        for j, chip in enumerate(chips):
            copy(4 + j, (*chip, 1 - c), me).wait_recv()
        for cp in first + passed:
            cp.wait_send()
        mine.wait()
        acc = all_ref[0:rows, :]
        for d in range(1, N_DEV):
            acc = acc + all_ref[d * rows:(d + 1) * rows, :]
        sum_ref[...] = acc

    vmem = pl.BlockSpec(memory_space=pltpu.VMEM)
    return pl.pallas_call(
        body, name=name, in_specs=[vmem], out_specs=[vmem, vmem],
        out_shape=[jax.ShapeDtypeStruct((rows, 128), F32), jax.ShapeDtypeStruct((N_DEV * rows, 128), F32)],
        scratch_shapes=[pltpu.SemaphoreType.DMA((7,)), pltpu.SemaphoreType.DMA((7,)), pltpu.SemaphoreType.DMA],
        compiler_params=pltpu.CompilerParams(vmem_limit_bytes=VMEM_BIG),
    )(buf)[0]


WEIGHTS = ['ev_w_in', 'ev_lambda_re', 'ev_lambda_im', 'ev_log_dt', 'ev_b_re', 'ev_b_im', 'ev_c_re', 'ev_c_im', 'ev_d',
           'ev_w_glu', 'ev_b_glu', 'ev_conv_w', 'ev_w_out', 'od_w_in', 'od_rel_bias', 'od_pool_w', 'od_pool_scale',
           'od_w_out', 'ln_mix_g', 'ln_mix_b', 'ln_ffn_g', 'ln_ffn_b', 'ffn_w_up', 'ffn_w_down', 'ple_w_proj',
           'ple_w_gate', 'ple_b_gate']
INPUTS = ['x', 'p'] + WEIGHTS + ['loss_target'] + ['m_' + n for n in WEIGHTS] + ['v_' + n for n in WEIGHTS]

BIG = {
    'ev_w_in': (2, (2, 1024, 2048)), 'ev_w_glu': (1, (2, 512, 512)), 'ev_w_out': (1, (2, 1024, 1024)),
    'od_w_in': (2, (2, 1024, 2048)), 'od_w_out': (1, (2, 1024, 1024)), 'ffn_w_up': (2, (4, 1024, 5632)),
    'ffn_w_down': (1, (4, 2816, 1024)), 'ple_w_proj': (2, (4, 256, 1024)), 'ple_w_gate': (1, (4, 1024, 1024)),
}
SMALL_SHARDED = {'ev_conv_w': (2, 3, 512), 'od_pool_scale': (2, 512)}
REPLICATED = [n for n in WEIGHTS if n not in BIG and n not in SMALL_SHARDED]


def _shard_rows(name):
    axis, (nl, k, n) = BIG[name]
    return (nl * k, n // N_CHIPS) if axis == 2 else (nl * k // N_CHIPS, n)


def _pack(arrs):
    flat = jnp.concatenate([a.reshape(-1) for a in arrs])
    total = flat.shape[0]
    padded = -(-total // 1024) * 1024
    return jnp.pad(flat, (0, padded - total)).reshape(padded // 128, 128)


def _unpack(buf, shapes):
    flat = buf.reshape(-1)
    out, pos = [], 0
    for s in shapes:
        size = int(np.prod(s))
        out.append(flat[pos:pos + size].reshape(s))
        pos += size
    return out


def _s5_params(lam_re, lam_im, log_dt, b_re, b_im, c_re, c_im):
    dt = jnp.exp(log_dt)[:, None]
    mag = jnp.exp(lam_re * dt)
    ang = lam_im * dt
    lb_re = mag * jnp.cos(ang)
    lb_im = mag * jnp.sin(ang)
    den = lam_re * lam_re + lam_im * lam_im
    nr = lb_re - 1.0
    ni = lb_im
    r_re = (nr * lam_re + ni * lam_im) / den
    r_im = (ni * lam_re - nr * lam_im) / den
    bb_re = r_re[..., None] * b_re - r_im[..., None] * b_im
    bb_im = r_re[..., None] * b_im + r_im[..., None] * b_re
    per = S5_GROUPS // S5_SLABS
    eye = jnp.eye(per, dtype=F32)

    def block_diag(a):
        _, r, c = a.shape
        a = a.reshape(S5_SLABS, per, r, c)
        return (a[:, :, :, None, :] * eye[None, :, None, :, None]).reshape(S5_SLABS, per * r, per * c)

    bmat = jnp.concatenate([block_diag(bb_re.transpose(0, 2, 1)), block_diag(bb_im.transpose(0, 2, 1))], axis=2)
    cmat = jnp.concatenate([block_diag(c_re.transpose(0, 2, 1)), block_diag(-c_im.transpose(0, 2, 1))], axis=1)
    lam = jnp.stack([lb_re.reshape(S5_N), lb_im.reshape(S5_N)])
    return lam, bmat, cmat


def _lam_powers(lam):
    res, ims = [lam[0]], [lam[1]]
    for _ in range(7):
        res, ims = res + [res[-1] * lam[0] - ims[-1] * lam[1]], ims + [res[-1] * lam[1] + ims[-1] * lam[0]]
    sq_r, sq_i = [res[-1]], [ims[-1]]
    for _ in range(4):
        sq_r, sq_i = sq_r + [sq_r[-1] * sq_r[-1] - sq_i[-1] * sq_i[-1]], sq_i + [2.0 * sq_r[-1] * sq_i[-1]]
    return jnp.stack(res + ims + res[::-1] + ims[::-1] + sq_r[1:] + sq_i[1:])


def _layer_big(i):
    mixer = [('w_in', 'ev_w_in'), ('w_glu', 'ev_w_glu'), ('w_out', 'ev_w_out')] if i % 2 == 0 else \
        [('w_in', 'od_w_in'), ('w_out', 'od_w_out')]
    ffn = [('w_up', 'ffn_w_up'), ('w_down', 'ffn_w_down'), ('w_proj', 'ple_w_proj'), ('w_gate', 'ple_w_gate')]
    return [(k, n, i // 2) for k, n in mixer] + [(k, n, i) for k, n in ffn]


class _WholePlan:
    def __init__(self, whole):
        self.whole = whole
        self.grads = {n: {} for n in BIG}

    def layer_weights(self, i):
        return {k: self.whole[n][l] for k, n, l in _layer_big(i)}

    def forward_host(self, i):
        return None

    def backward_host(self, i):
        return None

    def early_host(self, i, g):
        return None

    def layer_grads(self, i, g):
        for k, n, l in _layer_big(i):
            self.grads[n][l] = g[k][0]


def _local_step(x, p, target, w, plan):
    mask = jnp.asarray(_band_mask())
    diag_idx = _diag_index()
    onehot = jnp.asarray(np.eye(2 * MAX_REL + 1, dtype=np.float32)[diag_idx])
    saved = []
    for i in range(DEPTH):
        li = i // 2
        lw = plan.layer_weights(i)
        s = {'x0': x, 'lw': lw}
        h = _mm([(x, 0, D_MODEL)], lw['w_in'], name=f"in_proj")
        if i % 2 == 0:
            (lam, bmat, cmat), s5_vjp = jax.vjp(
                _s5_params, w['ev_lambda_re'][li], w['ev_lambda_im'][li], w['ev_log_dt'][li], w['ev_b_re'][li],
                w['ev_b_im'][li], w['ev_c_re'][li], w['ev_c_im'][li])
            s5c = (bmat.astype(BF16), cmat.astype(BF16), w['ev_d'][li].reshape(1, MIX), lw['w_glu'],
                   w['ev_b_glu'][li].reshape(1, MIX), _lam_powers(lam))
            ya, ypre, hb = _s5_fwd(h, *s5c, name=f"s5_fwd")
            yb = _conv_fwd(h, w['ev_conv_w'][li], name=f"conv_fwd")
            s.update(s5_vjp=s5_vjp, s5c=s5c, ypre=ypre, hb=hb)
        else:
            vdiag = jnp.dot(w['od_rel_bias'][li], onehot.T, precision=HIGHEST).reshape(ATT_HEADS // 2, 2, NKEY)
            pw = w['od_pool_w'][li].astype(BF16)
            ps = w['od_pool_scale'][li].reshape(1, MIX)
            ya = _attn_fwd(h, vdiag, mask, name=f"attn_fwd")
            yb = _pool_fwd(h, pw, ps, name=f"pool_fwd")
            s.update(vdiag=vdiag, pw=pw, ps=ps)
        wout = lw['w_out']
        vec = lambda n: w[n][i].reshape(1, -1)

        def residual_ln(products, rows, vecs):
            r = ALPHA * rows[0] + products[0]
            return (r, _ln_apply(r, vecs[0], vecs[1])), ()

        def embed_gate(products, rows, vecs):
            gate = _sigmoid(products[0] + vecs[0])
            return (rows[0] + gate * products[1], gate, products[1]), ()

        two_f32 = [(D_MODEL, F32), (D_MODEL, F32)]
        r1, x1 = _mm_rows([([(ya, 0, MIX), (yb, 0, MIX)], wout, False)], [x], [vec('ln_mix_g'), vec('ln_mix_b')],
                          two_f32, [], residual_ln, name="out_proj_ln")
        hosted = plan.forward_host(i)
        if hosted is None:
            a, gg, uu = _ffn_up(x1, lw['w_up'], name=f"ffn_up")
        else:
            (a, gg, uu), arrived = _ffn_up(x1, lw['w_up'], exchange=hosted, name=f"ffn_up_gather")
            plan.forward_hosted(i, arrived)
        r2, x2 = _mm_rows([([(a, 0, D_FF)], lw['w_down'], False)], [x1], [vec('ln_ffn_g'), vec('ln_ffn_b')],
                          two_f32, [], residual_ln, name="ffn_down_ln")
        x3, gate, ppb = _mm_rows(
            [([(None, 0, D_MODEL)], lw['w_gate'], False), ([(p[i], 0, D_PLE)], lw['w_proj'], False)],
            [x2], [vec('ple_b_gate')], [(D_MODEL, F32), (D_MODEL, BF16), (D_MODEL, BF16)], [], embed_gate, name="ple")
        s.update(h=h, ya=ya, yb=yb, r1=r1, x1=x1, a=a, gg=gg, uu=uu, r2=r2, x2=x2, gate=gate, ppb=ppb)
        saved.append(s)
        x = x3

    loss, da = _loss_head(x, target, name="loss_head")
    grads = {n: [None] * (DEPTH if n.startswith(('ln_', 'ple_')) else DEPTH // 2) for n in WEIGHTS if n not in BIG}

    def both(pieces, axis):
        return tuple(jnp.concatenate([pc[k] for pc in pieces], axis=axis) for k in range(2))

    for i in reversed(range(DEPTH)):
        li = i // 2
        s = saved[i]
        lw = s['lw']
        big = {}
        dz, dpp, dr2, dbg, dg2, db2 = _ple_ln_bwd(da, s['gate'], s['ppb'], s['r2'], lw['w_gate'],
                                                  w['ln_ffn_g'][i].reshape(1, -1), name="ple_ln_bwd")
        grads['ple_b_gate'][i] = dbg.reshape(-1)
        big['w_gate'] = _mm_tn(s['x2'], 0, D_MODEL, dz, also_bf16=True, name=f"d_ple_gate")
        big['w_proj'] = _mm_tn(p[i], 0, D_PLE, dpp, also_bf16=True, name=f"d_ple_proj")
        grads['ln_ffn_g'][i] = dg2.reshape(-1)
        grads['ln_ffn_b'][i] = db2.reshape(-1)
        dhh = _ffn_down_bwd(dr2, lw['w_down'], s['gg'], s['uu'], name=f"ffn_down_bwd")
        big['w_down'] = _mm_tn(s['a'], 0, D_FF, dr2, tk=D_FF // 2, also_bf16=True, name=f"d_ffn_down")
        hosted = plan.backward_host(i)
        if hosted is None:
            big['w_up'] = _mm_tn(s['x1'], 0, D_MODEL, dhh, tn=D_FF // 2, also_bf16=True, name=f"d_ffn_up")
        else:
            big['w_up'], arrived = _mm_tn(s['x1'], 0, D_MODEL, dhh, tn=D_FF // 2, also_bf16=True, exchange=hosted,
                                          name=f"d_ffn_up_reduce_{i % 2}")
            plan.backward_hosted(i, arrived)

        def ln_mix_grad(products, rows, vecs):
            dr, dg, dbias = _ln_grad(rows[0], ALPHA * rows[1] + products[0], vecs[0])
            return (dr,), (dg, dbias)

        dr1, dg1, db1 = _mm_rows([([(dhh, 0, 2 * D_FF)], lw['w_up'], True)], [s['r1'], dr2],
                                 [w['ln_mix_g'][i].reshape(1, -1)], [(D_MODEL, F32)], [D_MODEL, D_MODEL], ln_mix_grad,
                                 tm=TM, vmem=VMEM_BIG, name="ffn_up_ln_bwd")
        grads['ln_mix_g'][i] = dg1.reshape(-1)
        grads['ln_mix_b'][i] = db1.reshape(-1)
        dcat = _mm([(dr1, 0, D_MODEL)], lw['w_out'], trans_b=True, name=f"out_proj_bwd")
        big['w_out'] = both([_mm_tn(s['ya'], 0, MIX, dr1, also_bf16=True, name=f"d_out_a"),
                             _mm_tn(s['yb'], 0, MIX, dr1, also_bf16=True, name=f"d_out_b")], 0)
        h = s['h']
        if i % 2 == 0:
            s5c = s['s5c']
            hosted = plan.early_host(i, big)
            if hosted is None:
                s5_out = _s5_bwd(dcat, s['ypre'], h, s['hb'], *s5c, name=f"s5_bwd")
            else:
                s5_out, arrived = _s5_bwd(dcat, s['ypre'], h, s['hb'], *s5c, exchange=hosted, name=f"s5_bwd_reduce")
                plan.early_hosted(i, arrived)
            du, xb, gb, gq, dzzq, dyq, dlam, dbglu, dd = s5_out
            dbmat = _mm_tn_slabs(h, 128, gb, SLAB_COLS, S5_SLABS, name=f"d_s5_b")
            dcmat = _mm_tn_slabs(xb, SLAB_COLS, dyq, 128, S5_SLABS, name=f"d_s5_c")
            s5g = s['s5_vjp']((dlam, dbmat, dcmat))
            for n, g_ in zip(['ev_lambda_re', 'ev_lambda_im', 'ev_log_dt', 'ev_b_re', 'ev_b_im', 'ev_c_re', 'ev_c_im'], s5g):
                grads[n][li] = g_
            big['w_glu'] = _mm_tn(gq, 0, MIX, dzzq, also_bf16=True, name=f"d_glu")
            grads['ev_b_glu'][li] = dbglu.reshape(-1)
            grads['ev_d'][li] = dd.reshape(-1)
            d3, dcw = _conv_bwd(dcat, h, w['ev_conv_w'][li], name=f"conv_bwd")
            grads['ev_conv_w'][li] = dcw[0:3]
            big['w_in'] = both([_mm_tn(s['x0'], 0, D_MODEL, du, also_bf16=True, name=f"d_in_a"),
                                _mm_tn(s['x0'], 0, D_MODEL, d3, tn=3 * MIX, also_bf16=True, name=f"d_in_b")], 1)
            dh_parts = [(du, 0, MIX), (d3, 0, 3 * MIX)]
        else:
            dq, dk, dv, dvd = _attn_bwd(dcat, h, s['vdiag'], mask, name=f"attn_bwd")
            dzp, dpw, dps = _pool_bwd(dcat, h, s['pw'], s['ps'], name=f"pool_bwd")
            parts = [dq, dk, dv, dzp]
            grads['od_rel_bias'][li] = jnp.dot(dvd.reshape(ATT_HEADS, NKEY), onehot, precision=HIGHEST)
            grads['od_pool_w'][li] = dpw
            grads['od_pool_scale'][li] = dps.reshape(-1)
            big['w_in'] = both([_mm_tn(s['x0'], 0, D_MODEL, d_, also_bf16=True, name=f"d_in_a") for d_ in parts], 1)
            dh_parts = [(d_, 0, MIX) for d_ in parts]
        plan.layer_grads(i, big)

        def layer_input_grad(products, rows, vecs):
            return (ALPHA * rows[0] + products[0],), ()

        (da,) = _mm_rows([(dh_parts, lw['w_in'], True)], [dr1], [], [(D_MODEL, F32)], [], layer_input_grad,
                         name=f"in_proj_bwd_{i % 2}")
    return loss, da, {n: jnp.stack(g) for n, g in grads.items()}


def _sum_blocks(own, others, *, name):
    rows, cols = own.shape
    t = _row_tile(rows)

    def body(own_ref, others_ref, o_ref):
        acc = own_ref[...]
        for k in range(N_DEV - 1):
            acc = acc + others_ref[k].astype(F32)
        o_ref[...] = acc

    return pl.pallas_call(
        body, name=name, grid=(rows // t,),
        in_specs=[pl.BlockSpec((t, cols), lambda i: (i, 0)), pl.BlockSpec((N_DEV - 1, t, cols), lambda i: (0, i, 0))],
        out_specs=pl.BlockSpec((t, cols), lambda i: (i, 0)), out_shape=jax.ShapeDtypeStruct((rows, cols), F32),
        compiler_params=_cparams(("parallel",)),
    )(own, others)


def _swap_sibling(arrs, *, name):
    n = len(arrs)

    def body(*refs):
        ins, outs = refs[:n], refs[n:2 * n]
        send_sems, recv_sems = refs[2 * n:]
        x, y, c, _ = _place()
        cps = [pltpu.make_async_remote_copy(src_ref=ins[k], dst_ref=outs[k], send_sem=send_sems.at[k], recv_sem=recv_sems.at[k],
                                            device_id=(x, y, 1 - c), device_id_type=MESH) for k in range(n)]
        for cp in cps:
            cp.start()
        for cp in cps:
            cp.wait_recv()
        for cp in cps:
            cp.wait_send()

    return pl.pallas_call(
        body, name=name, in_specs=[ANY] * n, out_specs=[ANY] * n,
        out_shape=[jax.ShapeDtypeStruct(a.shape, a.dtype) for a in arrs],
        scratch_shapes=[pltpu.SemaphoreType.DMA((n,)), pltpu.SemaphoreType.DMA((n,))],
    )(*arrs)


class _ShardedPlan:
    def __init__(self, a, c, me):
        self.a, self.c, self.me = a, c, me
        self.weights, self.pending, self.own, self.arrived = {}, None, {}, {}

    def _shards(self, i):
        return [self.a[n][l].astype(BF16) for _, n, l in _layer_big(i)]

    def _with_own(self, gathered, own):
        return lax.dynamic_update_index_in_dim(gathered, own, self.me, 0)

    def _set_weights(self, i, gathered):
        lw = {}
        for (k, n, _), g, own in zip(_layer_big(i), gathered, self._shards(i)):
            _, rows, cols = g.shape
            g = self._with_own(g, own)
            lw[k] = g.transpose(1, 0, 2).reshape(rows, N_CHIPS * cols) if BIG[n][0] == 2 else g.reshape(N_CHIPS * rows, cols)
        self.weights[i] = lw

    def gather_first(self, misc):
        gathered = _run_exchange(_GatherExchange(self._shards(0) + [misc]), name="weight_gather_0")
        self._set_weights(0, gathered[:-1])
        return self._with_own(gathered[-1], misc)

    def layer_weights(self, i):
        return self.weights.pop(i)

    def forward_host(self, i):
        return _GatherExchange(self._shards(i + 1)) if i + 1 < DEPTH else None

    def forward_hosted(self, i, arrived):
        self._set_weights(i + 1, arrived)

    EARLY = ('w_up', 'w_down', 'w_proj', 'w_gate')

    def _reduce_exchange(self, i, g, early=None):
        items = [(k, n, l) for k, n, l in _layer_big(i) if early is None or (k in self.EARLY) == early]
        return [(n, l) for _, n, l in items], _ReduceExchange([g[k][1] for k, _, _ in items], [BIG[n][0] for _, n, _ in items])

    def early_host(self, i, g):
        if i != 0:
            return None
        self.early_keys, exchange = self._reduce_exchange(0, g, early=True)
        return exchange

    def early_hosted(self, i, arrived):
        self.arrived.update(zip(self.early_keys, arrived))

    def layer_grads(self, i, g):
        for k, n, l in _layer_big(i):
            full = g[k][0]
            kk, nn = full.shape
            if BIG[n][0] == 2:
                self.own[n, l] = lax.dynamic_slice(full, (self.c * (kk // 2), self.me * (nn // N_CHIPS)), (kk // 2, nn // N_CHIPS))
            else:
                self.own[n, l] = lax.dynamic_slice_in_dim(full, (2 * self.me + self.c) * (kk // N_DEV), kk // N_DEV, axis=0)
        self.pending = (i, g)

    def backward_host(self, i):
        if i + 1 >= DEPTH:
            return None
        self.hosted_keys, exchange = self._reduce_exchange(*self.pending)
        return exchange

    def backward_hosted(self, i, arrived):
        self.arrived.update(zip(self.hosted_keys, arrived))

    def reduced(self):
        late_keys, exchange = self._reduce_exchange(*self.pending, early=False)
        self.arrived.update(zip(late_keys, _run_exchange(exchange, name="grad_reduce_0")))
        keys = [(n, l) for n in BIG for l in range(BIG[n][1][0])]
        mine = [_sum_blocks(self.own[k], self.arrived[k], name=f"grad_sum_{k[0]}") for k in keys]
        theirs = _swap_sibling(mine, name="grad_half_swap")
        out = {}
        for n in BIG:
            layers = []
            for l in range(BIG[n][1][0]):
                a_, b_ = mine[keys.index((n, l))], theirs[keys.index((n, l))]
                layers.append(jnp.where(self.c == 0, jnp.concatenate([a_, b_], axis=0), jnp.concatenate([b_, a_], axis=0)))
            out[n] = jnp.stack(layers)
        return out


def kernel(x, p, ev_w_in, ev_lambda_re, ev_lambda_im, ev_log_dt, ev_b_re, ev_b_im, ev_c_re, ev_c_im, ev_d, ev_w_glu, ev_b_glu, ev_conv_w, ev_w_out, od_w_in, od_rel_bias, od_pool_w, od_pool_scale, od_w_out, ln_mix_g, ln_mix_b, ln_ffn_g, ln_ffn_b, ffn_w_up, ffn_w_down, ple_w_proj, ple_w_gate, ple_b_gate, loss_target, m_ev_w_in, m_ev_lambda_re, m_ev_lambda_im, m_ev_log_dt, m_ev_b_re, m_ev_b_im, m_ev_c_re, m_ev_c_im, m_ev_d, m_ev_w_glu, m_ev_b_glu, m_ev_conv_w, m_ev_w_out, m_od_w_in, m_od_rel_bias, m_od_pool_w, m_od_pool_scale, m_od_w_out, m_ln_mix_g, m_ln_mix_b, m_ln_ffn_g, m_ln_ffn_b, m_ffn_w_up, m_ffn_w_down, m_ple_w_proj, m_ple_w_gate, m_ple_b_gate, v_ev_w_in, v_ev_lambda_re, v_ev_lambda_im, v_ev_log_dt, v_ev_b_re, v_ev_b_im, v_ev_c_re, v_ev_c_im, v_ev_d, v_ev_w_glu, v_ev_b_glu, v_ev_conv_w, v_ev_w_out, v_od_w_in, v_od_rel_bias, v_od_pool_w, v_od_pool_scale, v_od_w_out, v_ln_mix_g, v_ln_mix_b, v_ln_ffn_g, v_ln_ffn_b, v_ffn_w_up, v_ffn_w_down, v_ple_w_proj, v_ple_w_gate, v_ple_b_gate):
    given = locals()
    a = {n: given[n] for n in INPUTS}
    x, y, c = lax.axis_index("x"), lax.axis_index("y"), lax.axis_index("c")
    me = 2 * x + y

    plan = _ShardedPlan(a, c, me)
    misc = jnp.concatenate([a['ev_conv_w'].reshape(6, 128), a['od_pool_scale'], jnp.zeros((8, 128), F32)], axis=0)
    gm = plan.gather_first(misc)
    w = {n: a[n] for n in REPLICATED}
    w['ev_conv_w'] = gm[:, 0:6].reshape(N_CHIPS, 2, 3, 128).transpose(1, 2, 0, 3).reshape(2, 3, 512)
    w['od_pool_scale'] = gm[:, 6:8].transpose(1, 0, 2).reshape(2, 512)

    loss, grad_x, grads = _local_step(a['x'][0], a['p'][:, 0], a['loss_target'][0], w, plan)
    loss = lax.psum(loss[0, 0], ("x", "y", "c"))

    small_names = REPLICATED + list(SMALL_SHARDED)
    small = _all_reduce_small(_pack([grads[n] for n in small_names]), name="small_grad_all_reduce")
    small = dict(zip(small_names, _unpack(small, [grads[n].shape for n in small_names])))
    for n in SMALL_SHARDED:
        small[n] = lax.dynamic_slice_in_dim(small[n], me * 128, 128, axis=small[n].ndim - 1)
    big = plan.reduced()

    res = {}
    for n in BIG:
        shape = a[n].shape
        flat = _shard_rows(n)
        d, m_, v_ = _adamw(a[n].reshape(flat), big[n].reshape(flat), a['m_' + n].reshape(flat), a['v_' + n].reshape(flat),
                           name=f"adamw_{n}")
        res[n] = (big[n], d.reshape(shape), m_.reshape(shape), v_.reshape(shape))
    shapes = [a[n].shape for n in small_names]
    d, m_, v_ = _adamw(_pack([a[n] for n in small_names]), _pack([small[n] for n in small_names]),
                       _pack([a['m_' + n] for n in small_names]), _pack([a['v_' + n] for n in small_names]), name="adamw_small")
    for n, dd, mm, vv in zip(small_names, _unpack(d, shapes), _unpack(m_, shapes), _unpack(v_, shapes)):
        res[n] = (small[n], dd, mm, vv)

    outs = [loss, grad_x[None]]
    for part in range(4):
        outs += [res[n][part] for n in WEIGHTS]
    return tuple(outs)
```

```python
import functools
import math

import jax
import jax.numpy as jnp
import numpy as np
from jax import lax
from jax.experimental import pallas as pl
from jax.experimental.pallas import tpu as pltpu

F32 = jnp.float32
BF16 = jnp.bfloat16
MESH = pl.DeviceIdType.MESH
HIGHEST = lax.Precision.HIGHEST

D_MODEL = 1024
DEPTH = 4
MIX = 512
S5_GROUPS = 32
S5_GROUP = 16
S5_STATE = 64
S5_N = S5_GROUPS * S5_STATE
CHUNK = 64
LEFT_CHUNKS = 8
MAX_REL = 128
ATT_HEADS = 8
HEAD_DIM = 64
POOL_WINDOWS = (2, 4, 8, 16)
POOL_GROUP = 128
D_FF = 2816
D_PLE = 256
ALPHA = (2 * DEPTH) ** 0.25
LN_EPS = 1e-5
NEG_INF = -1e30
N_CHIPS = 4
N_DEV = 8

ADAM_LR = 0.001
ADAM_B1 = 0.9
ADAM_B2 = 0.999
ADAM_EPS = 1e-08
ADAM_WD = 0.01
ADAM_STEP = 10

TM = 512
T_S5 = 512
T_ATT = 512
VMEM_BIG = 56 * 1024 * 1024


VMEM_DEFAULT = 48 * 1024 * 1024


def _cparams(sem, vmem=None):
    return pltpu.CompilerParams(dimension_semantics=sem, vmem_limit_bytes=vmem or VMEM_DEFAULT)


def _sigmoid(x):
    return 0.5 + 0.5 * jnp.tanh(0.5 * x)


def _mm(a_parts, b, *, name, trans_b=False, out_dtype=F32, tm=TM, tn=1024, vmem=None):
    m = a_parts[0][0].shape[0]
    n = b.shape[0] if trans_b else b.shape[1]
    kk = b.shape[1] if trans_b else b.shape[0]
    tn = min(tn, n)
    widths = [w for _, _, w in a_parts]
    assert sum(widths) == kk and m % tm == 0 and n % tn == 0
    na = len(a_parts)

    def body(*refs):
        b_ref, o_ref = refs[na], refs[na + 1]
        acc = None
        k0 = 0
        for ar, w in zip(refs[:na], widths):
            a = ar[...].astype(BF16)
            if trans_b:
                part = lax.dot_general(a, b_ref[:, k0:k0 + w], (((1,), (1,)), ((), ())), preferred_element_type=F32)
            else:
                part = jnp.dot(a, b_ref[k0:k0 + w, :], preferred_element_type=F32)
            acc = part if acc is None else acc + part
            k0 += w
        o_ref[...] = acc.astype(o_ref.dtype)

    in_specs = [pl.BlockSpec((tm, w), functools.partial(lambda j, i, cb: (i, cb), cb=cb)) for _, cb, w in a_parts]
    if trans_b:
        in_specs.append(pl.BlockSpec((tn, kk), lambda j, i: (j, 0)))
    else:
        in_specs.append(pl.BlockSpec((kk, tn), lambda j, i: (0, j)))
    return pl.pallas_call(
        body, name=name, grid=(n // tn, m // tm), in_specs=in_specs,
        out_specs=pl.BlockSpec((tm, tn), lambda j, i: (i, j)),
        out_shape=jax.ShapeDtypeStruct((m, n), out_dtype),
        compiler_params=_cparams(("parallel", "parallel"), vmem),
    )(*[a for a, _, _ in a_parts], b)


def _host_parts(exchange):
    if exchange is None:
        return [], [], [], [], []
    any_space = pl.BlockSpec(memory_space=pl.ANY)
    return (exchange.ins, [any_space] * len(exchange.ins), [any_space] * len(exchange.out_shapes),
            list(exchange.out_shapes), list(exchange.sems))


def _host_run(exchange, refs, first, last):
    if exchange is None:
        return
    n_in, n_out = len(exchange.ins), len(exchange.out_shapes)
    ins, outs, sems = refs[:n_in], refs[n_in:n_in + n_out], refs[n_in + n_out:]

    @pl.when(first)
    def _():
        exchange.start(ins, outs, sems)

    @pl.when(last)
    def _():
        exchange.finish(ins, outs, sems)


def _mm_tn(a, a_cb, ka, b, *, name, tk=1024, tn=1024, tmr=2 * TM, vmem=None, also_bf16=False, exchange=None):
    m = a.shape[0]
    n = b.shape[1]
    tk = min(tk, ka)
    tn = min(tn, n)
    assert ka % tk == 0 and n % tn == 0 and m % tmr == 0
    kb = ka // tk
    grid = (kb, n // tn, m // tmr)
    ex_ops, ex_in_specs, ex_out_specs, ex_out_shapes, ex_scratch = _host_parts(exchange)
    n_own_out = 2 if also_bf16 else 1

    def body(*refs):
        a_ref, b_ref = refs[:2]
        hosted_in = refs[2:2 + len(ex_ops)]
        outs = refs[2 + len(ex_ops):]
        o_ref = outs[0]
        k, j, r = pl.program_id(0), pl.program_id(1), pl.program_id(2)
        _host_run(exchange, list(hosted_in) + list(outs[n_own_out:]),
                  (k == 0) & (j == 0) & (r == 0), (k == grid[0] - 1) & (j == grid[1] - 1) & (r == grid[2] - 1))

        @pl.when(r == 0)
        def _():
            o_ref[...] = jnp.zeros_like(o_ref)

        o_ref[...] += lax.dot_general(a_ref[...].astype(BF16), b_ref[...].astype(BF16), (((0,), (0,)), ((), ())),
                                      preferred_element_type=F32)
        if also_bf16:
            @pl.when(r == grid[2] - 1)
            def _():
                outs[1][...] = o_ref[...].astype(BF16)

    tile = pl.BlockSpec((tk, tn), lambda k, j, r: (k, j))
    res = pl.pallas_call(
        body, name=name, grid=grid,
        in_specs=[pl.BlockSpec((tmr, tk), lambda k, j, r: (r, a_cb * kb + k)),
                  pl.BlockSpec((tmr, tn), lambda k, j, r: (r, j))] + ex_in_specs,
        out_specs=[tile] * n_own_out + ex_out_specs,
        out_shape=[jax.ShapeDtypeStruct((ka, n), F32)] + ([jax.ShapeDtypeStruct((ka, n), BF16)] if also_bf16 else [])
        + ex_out_shapes,
        scratch_shapes=ex_scratch,
        compiler_params=_cparams(("arbitrary",) * 3 if exchange is not None else ("parallel", "parallel", "arbitrary"), vmem),
    )(a, b, *ex_ops)
    if exchange is None:
        return tuple(res) if also_bf16 else res[0]
    own = tuple(res[:n_own_out]) if also_bf16 else res[0]
    return own, list(res[n_own_out:])


def _mm_tn_slabs(a, ka, b, nbw, nslab, *, name, tmr=2 * TM):
    m = a.shape[0]
    assert m % tmr == 0

    def body(a_ref, b_ref, o_ref):
        @pl.when(pl.program_id(1) == 0)
        def _():
            o_ref[...] = jnp.zeros_like(o_ref)

        o_ref[0] += lax.dot_general(a_ref[...].astype(BF16), b_ref[...].astype(BF16), (((0,), (0,)), ((), ())),
                                    preferred_element_type=F32)

    return pl.pallas_call(
        body, name=name, grid=(nslab, m // tmr),
        in_specs=[pl.BlockSpec((tmr, ka), lambda s, r: (r, s)), pl.BlockSpec((tmr, nbw), lambda s, r: (r, s))],
        out_specs=pl.BlockSpec((1, ka, nbw), lambda s, r: (s, 0, 0)),
        out_shape=jax.ShapeDtypeStruct((nslab, ka, nbw), F32),
        compiler_params=_cparams(("parallel", "arbitrary")),
    )(a, b)


def _ln_stats(r):
    mu = jnp.mean(r, axis=-1, keepdims=True)
    xc = r - mu
    var = jnp.mean(xc * xc, axis=-1, keepdims=True)
    rstd = lax.rsqrt(var + LN_EPS)
    return xc * rstd, rstd


def _ln_apply(r, g, b):
    xhat, _ = _ln_stats(r)
    return xhat * g + b


def _ln_grad(r, dy, g):
    xhat, rstd = _ln_stats(r)
    dxh = dy * g
    m1 = jnp.mean(dxh, axis=-1, keepdims=True)
    m2 = jnp.mean(dxh * xhat, axis=-1, keepdims=True)
    return (rstd * (dxh - m1 - xhat * m2), jnp.sum(dy * xhat, axis=0, keepdims=True), jnp.sum(dy, axis=0, keepdims=True))


def _mm_rows(matmuls, rows_in, vecs_in, out_rows, acc_widths, fn, *, name, tm=TM, vmem=None):
    m = rows_in[0].shape[0]
    assert m % tm == 0
    flat, in_specs, layout = [], [], []
    for a_parts, b, trans_b in matmuls:
        own = [(arr, cb, w) for arr, cb, w in a_parts if arr is not None]
        for arr, cb, w in own:
            flat.append(arr)
            in_specs.append(pl.BlockSpec((tm, w), functools.partial(lambda i, cb: (i, cb), cb=cb)))
        flat.append(b)
        in_specs.append(pl.BlockSpec(b.shape, lambda i: (0, 0)))
        layout.append(([(arr is None, cb, w) for arr, cb, w in a_parts], len(own), trans_b))
    first_row = len(flat)
    for r in rows_in:
        flat.append(r)
        in_specs.append(pl.BlockSpec((tm, r.shape[1]), lambda i: (i, 0)))
    for v in vecs_in:
        flat.append(v)
        in_specs.append(pl.BlockSpec(v.shape, lambda i: (0, 0)))
    n_in = len(flat)
    n_rows_out = len(out_rows)

    def body(*refs):
        rows = [r[...] for r in refs[first_row:first_row + len(rows_in)]]
        vecs = [v[...] for v in refs[first_row + len(rows_in):n_in]]
        pos = 0
        products = []
        for parts, n_own, trans_b in layout:
            b_ref = refs[pos + n_own]
            own_refs = iter(refs[pos:pos + n_own])
            acc, k0 = None, 0
            for is_row, cb, w in parts:
                a = (rows[cb] if is_row else next(own_refs)[...]).astype(BF16)
                if trans_b:
                    part = lax.dot_general(a, b_ref[:, k0:k0 + w], (((1,), (1,)), ((), ())), preferred_element_type=F32)
                else:
                    part = jnp.dot(a, b_ref[k0:k0 + w, :], preferred_element_type=F32)
                acc = part if acc is None else acc + part
                k0 += w
            products.append(acc)
            pos += n_own + 1
        outs, sums = fn(products, rows, vecs)
        for o_ref, o in zip(refs[n_in:n_in + n_rows_out], outs):
            o_ref[...] = o.astype(o_ref.dtype)
        if acc_widths:
            acc_refs = refs[n_in + n_rows_out:]

            @pl.when(pl.program_id(0) == 0)
            def _():
                for a_ref in acc_refs:
                    a_ref[...] = jnp.zeros_like(a_ref)

            for a_ref, s_ in zip(acc_refs, sums):
                a_ref[...] += s_

    out_specs = [pl.BlockSpec((tm, n), lambda i: (i, 0)) for n, _ in out_rows]
    out_specs += [pl.BlockSpec((1, wd), lambda i: (0, 0)) for wd in acc_widths]
    out_shape = [jax.ShapeDtypeStruct((m, n), dt) for n, dt in out_rows]
    out_shape += [jax.ShapeDtypeStruct((1, wd), F32) for wd in acc_widths]
    return pl.pallas_call(
        body, name=name, grid=(m // tm,), in_specs=in_specs, out_specs=out_specs, out_shape=out_shape,
        compiler_params=_cparams(("arbitrary",) if acc_widths else ("parallel",), vmem),
    )(*flat)


def _ffn_up(x1, wup, *, name, exchange=None):
    m = x1.shape[0]
    tn = D_FF // 2
    grid = (2, m // TM)
    ex_ops, ex_in_specs, ex_out_specs, ex_out_shapes, ex_scratch = _host_parts(exchange)

    def body(*refs):
        x_ref, wg_ref, wu_ref = refs[:3]
        hosted_in = refs[3:3 + len(ex_ops)]
        a_ref, g_ref, u_ref = refs[3 + len(ex_ops):6 + len(ex_ops)]
        j, i = pl.program_id(0), pl.program_id(1)
        _host_run(exchange, list(hosted_in) + list(refs[6 + len(ex_ops):]),
                  (j == 0) & (i == 0), (j == grid[0] - 1) & (i == grid[1] - 1))
        x = x_ref[...].astype(BF16)
        g = jnp.dot(x, wg_ref[...], preferred_element_type=F32)
        u = jnp.dot(x, wu_ref[...], preferred_element_type=F32)
        a_ref[...] = (g * _sigmoid(g) * u).astype(BF16)
        g_ref[...] = g.astype(BF16)
        u_ref[...] = u.astype(BF16)

    out = pl.BlockSpec((TM, tn), lambda j, i: (i, j))
    res = pl.pallas_call(
        body, name=name, grid=grid,
        in_specs=[pl.BlockSpec((TM, D_MODEL), lambda j, i: (i, 0)),
                  pl.BlockSpec((D_MODEL, tn), lambda j, i: (0, j)),
                  pl.BlockSpec((D_MODEL, tn), lambda j, i: (0, j + 2))] + ex_in_specs,
        out_specs=[out, out, out] + ex_out_specs,
        out_shape=[jax.ShapeDtypeStruct((m, D_FF), BF16)] * 3 + ex_out_shapes,
        scratch_shapes=ex_scratch,
        compiler_params=_cparams(("arbitrary", "arbitrary") if exchange is not None else ("parallel", "parallel")),
    )(x1, wup, wup, *ex_ops)
    return (res[0], res[1], res[2]) if exchange is None else ((res[0], res[1], res[2]), list(res[3:]))


def _ffn_down_bwd(df, wdown, g, u, *, name):
    m = df.shape[0]
    tm = TM
    chunk = 256

    def body(df_ref, w_ref, g_ref, u_ref, o_ref):
        df = df_ref[...].astype(BF16)
        for part in range(D_FF // chunk):
            cols = slice(chunk * part, chunk * (part + 1))
            da = lax.dot_general(df, w_ref[cols, :], (((1,), (1,)), ((), ())), preferred_element_type=F32)
            gg = g_ref[:, cols].astype(F32)
            sg = _sigmoid(gg)
            o_ref[:, cols] = (da * u_ref[:, cols].astype(F32) * (sg * (1.0 + gg * (1.0 - sg)))).astype(BF16)
            o_ref[:, D_FF + chunk * part:D_FF + chunk * (part + 1)] = (da * (gg * sg)).astype(BF16)

    return pl.pallas_call(
        body, name=name, grid=(m // tm,),
        in_specs=[pl.BlockSpec((tm, D_MODEL), lambda i: (i, 0)), pl.BlockSpec((D_FF, D_MODEL), lambda i: (0, 0)),
                  pl.BlockSpec((tm, D_FF), lambda i: (i, 0)), pl.BlockSpec((tm, D_FF), lambda i: (i, 0))],
        out_specs=pl.BlockSpec((tm, 2 * D_FF), lambda i: (i, 0)),
        out_shape=jax.ShapeDtypeStruct((m, 2 * D_FF), BF16),
        compiler_params=_cparams(("parallel",), VMEM_BIG),
    )(df, wdown, g, u)


def _ple_ln_bwd(dx3, gate, pp, r2, wgate, g2, *, name):
    m, n = dx3.shape

    def body(dx3_ref, gate_ref, pp_ref, r_ref, w_ref, g_ref, dz_ref, dpp_ref, dr_ref, dbg_ref, dg_ref, dbias_ref):
        dx3 = dx3_ref[...]

        @pl.when(pl.program_id(0) == 0)
        def _():
            dbg_ref[...] = jnp.zeros_like(dbg_ref)
            dg_ref[...] = jnp.zeros_like(dg_ref)
            dbias_ref[...] = jnp.zeros_like(dbias_ref)

        gate = gate_ref[...].astype(F32)
        dz = dx3 * pp_ref[...].astype(F32) * gate * (1.0 - gate)
        dzq = dz.astype(BF16)
        dz_ref[...] = dzq
        dpp_ref[...] = (dx3 * gate).astype(BF16)
        dbg_ref[...] += jnp.sum(dz, axis=0, keepdims=True)
        dx2 = dx3 + lax.dot_general(dzq, w_ref[...], (((1,), (1,)), ((), ())), preferred_element_type=F32)
        dr, dg, dbias = _ln_grad(r_ref[...], dx2, g_ref[...])
        dr_ref[...] = dr
        dg_ref[...] += dg
        dbias_ref[...] += dbias

    row = pl.BlockSpec((TM, n), lambda i: (i, 0))
    vec = pl.BlockSpec((1, n), lambda i: (0, 0))
    in_specs = [row] * 4 + [pl.BlockSpec(wgate.shape, lambda i: (0, 0)), vec]
    return pl.pallas_call(
        body, name=name, grid=(m // TM,), in_specs=in_specs, out_specs=[row, row, row, vec, vec, vec],
        out_shape=[jax.ShapeDtypeStruct((m, n), BF16), jax.ShapeDtypeStruct((m, n), BF16), jax.ShapeDtypeStruct((m, n), F32)]
        + [jax.ShapeDtypeStruct((1, n), F32)] * 3,
        compiler_params=_cparams(("arbitrary",)),
    )(dx3, gate, pp, r2, wgate, g2)


def _loss_head(y, target, *, name):
    m, n = y.shape

    def body(y_ref, t_ref, loss_ref, dy_ref):
        @pl.when(pl.program_id(0) == 0)
        def _():
            loss_ref[...] = jnp.zeros_like(loss_ref)

        err = y_ref[...] - t_ref[...]
        dy_ref[...] = err * (1.0 / n)
        per_tok = jnp.mean(err * err, axis=-1, keepdims=True)
        loss_ref[...] += 0.5 * jnp.sum(per_tok, axis=0, keepdims=True)

    row = pl.BlockSpec((TM, n), lambda i: (i, 0))
    return pl.pallas_call(
        body, name=name, grid=(m // TM,), in_specs=[row, row],
        out_specs=[pl.BlockSpec((1, 1), lambda i: (0, 0)), row],
        out_shape=[jax.ShapeDtypeStruct((1, 1), F32), jax.ShapeDtypeStruct((m, n), F32)],
        compiler_params=_cparams(("arbitrary",)),
    )(y, target)


def _gelu(y):
    c = math.sqrt(2.0 / math.pi)
    return 0.5 * y * (1.0 + jnp.tanh(c * (y + 0.044715 * y * y * y)))


def _gelu_grad(y):
    c = math.sqrt(2.0 / math.pi)
    t = jnp.tanh(c * (y + 0.044715 * y * y * y))
    return 0.5 * (1.0 + t) + 0.5 * y * (1.0 - t * t) * c * (1.0 + 3.0 * 0.044715 * y * y)


STRIP = 128
S5_SLABS = 4
SLAB_COLS = 2 * S5_N // S5_SLABS
N_TILES = 2 * S5_N // STRIP
SLAB_TILES = SLAB_COLS // STRIP


def _strip_tiles(j):
    re_tile = (j // (SLAB_TILES // 2)) * SLAB_TILES + j % (SLAB_TILES // 2)
    return pl.multiple_of(j * STRIP, STRIP), re_tile, re_tile + SLAB_TILES // 2


def _store_tiles(ref, first_tile, value):
    for k in range(value.shape[1] // STRIP):
        ref[first_tile + k] = value[:, STRIP * k:STRIP * (k + 1)]


def _load_tiles(ref, first_tile, count):
    return jnp.concatenate([ref[first_tile + k] for k in range(count)], axis=1)


GROUPS = T_S5 // 8
N_SQUARES = GROUPS.bit_length() - 2
PTAB_ROWS = 32 + 2 * N_SQUARES


def _scan_cols(ref, hr, hi, ptab_ref, off, down, visit=None):
    sign = 1.0 if down else -1.0
    ref_r, ref_i = ref
    cols_p = pl.ds(off, STRIP)

    def power(row, im_offset=8):
        return ptab_ref[row:row + 1, cols_p], sign * ptab_ref[row + im_offset:row + im_offset + 1, cols_p]

    def rows(r):
        return pl.ds(r, GROUPS, stride=8)

    def mul_add(br, bi, qr, qi, vr, vi):
        return br + qr * vr - qi * vi, bi + qr * vi + qi * vr

    order = list(range(8)) if down else list(range(7, -1, -1))
    lam_r, lam_i = power(0)
    vr, vi = ref_r[rows(order[0]), :], ref_i[rows(order[0]), :]
    for r in order[1:]:
        vr, vi = mul_add(ref_r[rows(r), :], ref_i[rows(r), :], lam_r, lam_i, vr, vi)
        ref_r[rows(r), :] = vr
        ref_i[rows(r), :] = vi
    grow = lax.broadcasted_iota(jnp.int32, (GROUPS, STRIP), 0)
    edge = 0 if down else GROUPS - 1
    l8r, l8i = power(7)
    er = vr + jnp.where(grow == edge, l8r * hr - l8i * hi, 0.0)
    ei = vi + jnp.where(grow == edge, l8r * hi + l8i * hr, 0.0)
    k, step = 0, 1
    while step < GROUPS:
        qr, qi = (l8r, l8i) if k == 0 else power(32 + k - 1, N_SQUARES)
        if down:
            sr = jnp.where(grow >= step, pltpu.roll(er, step, 0), 0.0)
            si = jnp.where(grow >= step, pltpu.roll(ei, step, 0), 0.0)
        else:
            sr = jnp.where(grow < GROUPS - step, pltpu.roll(er, GROUPS - step, 0), 0.0)
            si = jnp.where(grow < GROUPS - step, pltpu.roll(ei, GROUPS - step, 0), 0.0)
        er, ei = mul_add(er, ei, qr, qi, sr, si)
        k, step = k + 1, 2 * step
    if down:
        cr = jnp.where(grow == 0, hr, pltpu.roll(er, 1, 0))
        ci = jnp.where(grow == 0, hi, pltpu.roll(ei, 1, 0))
    else:
        cr = jnp.where(grow == GROUPS - 1, hr, pltpu.roll(er, GROUPS - 1, 0))
        ci = jnp.where(grow == GROUPS - 1, hi, pltpu.roll(ei, GROUPS - 1, 0))
    for r in range(8):
        qr, qi = power(r if down else 16 + r)
        xr, xi = mul_add(ref_r[rows(r), :], ref_i[rows(r), :], qr, qi, cr, ci)
        ref_r[rows(r), :] = xr
        ref_i[rows(r), :] = xi
        if visit is not None:
            visit(r, xr, xi)
    last = GROUPS - 1 if down else 0
    return er[last:last + 1], ei[last:last + 1]


def _s5_fwd(h, bmat, cmat, dvec, wglu, bglu, ptab, *, name):
    m = h.shape[0]
    t = T_S5
    nb = m // t

    def body(u_ref, bmat_ref, cmat_ref, d_ref, wglu_ref, bglu_ref, ptab_ref,
             out_ref, y_ref, hb_ref, bu_ref, carry_ref):
        @pl.when(pl.program_id(0) == 0)
        def _():
            carry_ref[...] = jnp.zeros_like(carry_ref)

        hb_ref[0] = carry_ref[...]
        u = u_ref[...]
        ub = u.astype(BF16)
        for s in range(S5_SLABS):
            _store_tiles(bu_ref, SLAB_TILES * s,
                         jnp.dot(ub[:, 128 * s:128 * (s + 1)], bmat_ref[s], preferred_element_type=F32))

        def strip(j, c):
            off, tr, ti = _strip_tiles(j)
            cols_r, cols_i = pl.ds(pl.multiple_of(tr * STRIP, STRIP), STRIP), pl.ds(pl.multiple_of(ti * STRIP, STRIP), STRIP)
            er, ei = _scan_cols((bu_ref.at[tr], bu_ref.at[ti]), carry_ref[0:1, cols_r], carry_ref[0:1, cols_i],
                                ptab_ref, off, True)
            carry_ref[0:1, cols_r] = er
            carry_ref[0:1, cols_i] = ei
            return c

        lax.fori_loop(0, S5_N // STRIP, strip, 0)
        y = jnp.concatenate(
            [jnp.dot(_load_tiles(bu_ref, SLAB_TILES * s, SLAB_TILES).astype(BF16), cmat_ref[s], preferred_element_type=F32)
             for s in range(S5_SLABS)], axis=1) + d_ref[...] * u
        y_ref[...] = y
        g = _gelu(y)
        zz = jnp.dot(g.astype(BF16), wglu_ref[...], preferred_element_type=F32) + bglu_ref[...]
        out_ref[...] = (g * _sigmoid(zz)).astype(BF16)

    const = lambda shape: pl.BlockSpec(shape, lambda i: (0,) * len(shape))
    row_spec = pl.BlockSpec((t, MIX), lambda i: (i, 0))
    return pl.pallas_call(
        body, name=name, grid=(nb,),
        in_specs=[row_spec, const((S5_SLABS, 128, SLAB_COLS)), const((S5_SLABS, SLAB_COLS, 128)), const((1, MIX)),
                  const((MIX, MIX)), const((1, MIX)), const((PTAB_ROWS, S5_N))],
        out_specs=[row_spec, row_spec, pl.BlockSpec((1, 1, 2 * S5_N), lambda i: (i, 0, 0))],
        out_shape=[jax.ShapeDtypeStruct((m, MIX), BF16), jax.ShapeDtypeStruct((m, MIX), F32),
                   jax.ShapeDtypeStruct((nb, 1, 2 * S5_N), F32)],
        scratch_shapes=[pltpu.VMEM((N_TILES, t, STRIP), F32), pltpu.VMEM((1, 2 * S5_N), F32)],
        compiler_params=_cparams(("arbitrary",), VMEM_BIG),
    )(h, bmat, cmat, dvec, wglu, bglu, ptab)


def _s5_bwd(dcat, ypre, h, hb, bmat, cmat, dvec, wglu, bglu, ptab, *, name, exchange=None):
    m = h.shape[0]
    t = T_S5
    nb = m // t
    ex_ops, ex_in_specs, ex_out_specs, ex_out_shapes, ex_scratch = _host_parts(exchange)
    n_ex = len(ex_ops)

    def body(*refs):
        dya_ref, y_ref, u_ref, hb_ref, bmat_ref, cmat_ref, d_ref, wglu_ref, bglu_ref, ptab_ref = refs[:10]
        du_ref, xb_ref, gb_ref, gq_ref, dzz_ref, dyq_ref, dlam_ref, dbglu_ref, dd_ref = refs[10 + n_ex:19 + n_ex]
        hosted_out = refs[19 + n_ex:19 + n_ex + len(ex_out_shapes)]
        bu_ref, dx_ref, gcarry_ref = refs[19 + n_ex + len(ex_out_shapes):22 + n_ex + len(ex_out_shapes)]
        _host_run(exchange, list(refs[10:10 + n_ex]) + list(hosted_out) + list(refs[22 + n_ex + len(ex_out_shapes):]),
                  pl.program_id(0) == 0, pl.program_id(0) == nb - 1)

        @pl.when(pl.program_id(0) == 0)
        def _():
            gcarry_ref[...] = jnp.zeros_like(gcarry_ref)
            dlam_ref[...] = jnp.zeros_like(dlam_ref)
            dbglu_ref[...] = jnp.zeros_like(dbglu_ref)
            dd_ref[...] = jnp.zeros_like(dd_ref)

        u = u_ref[...]
        y = y_ref[...]
        g = _gelu(y)
        gq = g.astype(BF16)
        sg = _sigmoid(jnp.dot(gq, wglu_ref[...], preferred_element_type=F32) + bglu_ref[...])
        dout = dya_ref[...]
        dzz = dout * g * sg * (1.0 - sg)
        dzzq = dzz.astype(BF16)
        dg = dout * sg + lax.dot_general(dzzq, wglu_ref[...], (((1,), (1,)), ((), ())), preferred_element_type=F32)
        dy = dg * _gelu_grad(y)
        dyq = dy.astype(BF16)
        gq_ref[...] = gq
        dzz_ref[...] = dzzq
        dyq_ref[...] = dyq
        dbglu_ref[...] += jnp.sum(dzz, axis=0, keepdims=True)
        dd_ref[...] += jnp.sum(dy * u, axis=0, keepdims=True)

        ub = u.astype(BF16)
        nt = (((1,), (1,)), ((), ()))
        for s in range(S5_SLABS):
            _store_tiles(dx_ref, SLAB_TILES * s,
                         lax.dot_general(dyq[:, 128 * s:128 * (s + 1)], cmat_ref[s], nt, preferred_element_type=F32))
            _store_tiles(bu_ref, SLAB_TILES * s,
                         jnp.dot(ub[:, 128 * s:128 * (s + 1)], bmat_ref[s], preferred_element_type=F32))
        grow = lax.broadcasted_iota(jnp.int32, (GROUPS, STRIP), 0)

        def strip(j, c):
            off, tr, ti = _strip_tiles(j)
            cols_r, cols_i = pl.ds(pl.multiple_of(tr * STRIP, STRIP), STRIP), pl.ds(pl.multiple_of(ti * STRIP, STRIP), STRIP)
            x_r, x_i = bu_ref.at[tr], bu_ref.at[ti]
            hr = hb_ref[0, 0:1, cols_r]
            hi = hb_ref[0, 0:1, cols_i]
            _scan_cols((x_r, x_i), hr, hi, ptab_ref, off, True)
            xb_ref[:, cols_r] = x_r[...].astype(BF16)
            xb_ref[:, cols_i] = x_i[...].astype(BF16)
            sums = [jnp.zeros((1, STRIP), F32), jnp.zeros((1, STRIP), F32)]

            def d_lam(r, gr, gi):
                if r == 0:
                    pr_ = jnp.where(grow == 0, hr, pltpu.roll(x_r[pl.ds(7, GROUPS, stride=8), :], 1, 0))
                    pi_ = jnp.where(grow == 0, hi, pltpu.roll(x_i[pl.ds(7, GROUPS, stride=8), :], 1, 0))
                else:
                    pr_ = x_r[pl.ds(r - 1, GROUPS, stride=8), :]
                    pi_ = x_i[pl.ds(r - 1, GROUPS, stride=8), :]
                sums[0] = sums[0] + jnp.sum(pr_ * gr + pi_ * gi, axis=0, keepdims=True)
                sums[1] = sums[1] + jnp.sum(pr_ * gi - pi_ * gr, axis=0, keepdims=True)

            g_r, g_i = dx_ref.at[tr], dx_ref.at[ti]
            gr0, gi0 = _scan_cols((g_r, g_i), gcarry_ref[0:1, cols_r], gcarry_ref[0:1, cols_i], ptab_ref, off, False, d_lam)
            gb_ref[:, cols_r] = g_r[...].astype(BF16)
            gb_ref[:, cols_i] = g_i[...].astype(BF16)
            gcarry_ref[0:1, cols_r] = gr0
            gcarry_ref[0:1, cols_i] = gi0
            dlam_ref[0:1, pl.ds(off, STRIP)] += sums[0]
            dlam_ref[1:2, pl.ds(off, STRIP)] += sums[1]
            return c

        lax.fori_loop(0, S5_N // STRIP, strip, 0)
        du_ref[...] = dy * d_ref[...] + jnp.concatenate(
            [lax.dot_general(gb_ref[:, SLAB_COLS * s:SLAB_COLS * (s + 1)], bmat_ref[s], nt, preferred_element_type=F32)
             for s in range(S5_SLABS)], axis=1)

    const = lambda shape: pl.BlockSpec(shape, lambda i: (0,) * len(shape))
    rev = lambda i: (nb - 1 - i, 0)
    row_spec = pl.BlockSpec((t, MIX), rev)
    wide = pl.BlockSpec((t, 2 * S5_N), rev)
    res = pl.pallas_call(
        body, name=name, grid=(nb,),
        in_specs=[row_spec, row_spec, row_spec, pl.BlockSpec((1, 1, 2 * S5_N), lambda i: (nb - 1 - i, 0, 0)),
                  const((S5_SLABS, 128, SLAB_COLS)), const((S5_SLABS, SLAB_COLS, 128)), const((1, MIX)), const((MIX, MIX)),
                  const((1, MIX)), const((PTAB_ROWS, S5_N))] + ex_in_specs,
        out_specs=[row_spec, wide, wide, row_spec, row_spec, row_spec, const((2, S5_N)), const((1, MIX)), const((1, MIX))]
        + ex_out_specs,
        out_shape=[jax.ShapeDtypeStruct((m, MIX), F32), jax.ShapeDtypeStruct((m, 2 * S5_N), BF16),
                   jax.ShapeDtypeStruct((m, 2 * S5_N), BF16), jax.ShapeDtypeStruct((m, MIX), BF16),
                   jax.ShapeDtypeStruct((m, MIX), BF16), jax.ShapeDtypeStruct((m, MIX), BF16),
                   jax.ShapeDtypeStruct((2, S5_N), F32), jax.ShapeDtypeStruct((1, MIX), F32), jax.ShapeDtypeStruct((1, MIX), F32)]
        + ex_out_shapes,
        scratch_shapes=[pltpu.VMEM((N_TILES, t, STRIP), F32), pltpu.VMEM((N_TILES, t, STRIP), F32),
                        pltpu.VMEM((1, 2 * S5_N), F32)] + ex_scratch,
        compiler_params=_cparams(("arbitrary",), VMEM_BIG),
    )(dcat, ypre, h, hb, bmat, cmat, dvec, wglu, bglu, ptab, *ex_ops)
    return res[:9] if exchange is None else (res[:9], list(res[9:]))


HALO = 8


def _taps_down(zext, t):
    return pltpu.roll(zext, 1, 0)[HALO:HALO + t], pltpu.roll(zext, 2, 0)[HALO:HALO + t]


def _conv_z(c_ref, x_ref, cp_ref, xp_ref, first, t):
    z = c_ref[...] * x_ref[...]
    zp = jnp.where(first, 0.0, cp_ref[t - HALO:t, :] * xp_ref[t - HALO:t, :])
    z1, z2 = _taps_down(jnp.concatenate([zp, z], axis=0), t)
    return z, z1, z2


def _conv_fwd(h, cw, *, name):
    m = h.shape[0]
    t = TM
    nb = m // t

    def body(b_ref, c_ref, x_ref, cp_ref, xp_ref, w_ref, o_ref):
        z, z1, z2 = _conv_z(c_ref, x_ref, cp_ref, xp_ref, pl.program_id(0) == 0, t)
        o_ref[...] = (b_ref[...] * (w_ref[0:1, :] * z2 + w_ref[1:2, :] * z1 + w_ref[2:3, :] * z)).astype(BF16)

    cur = lambda cb: pl.BlockSpec((t, MIX), lambda i: (i, cb))
    prev = lambda cb: pl.BlockSpec((t, MIX), lambda i: (jnp.maximum(i - 1, 0), cb))
    return pl.pallas_call(
        body, name=name, grid=(nb,),
        in_specs=[cur(1), cur(2), cur(3), prev(2), prev(3), pl.BlockSpec((3, MIX), lambda i: (0, 0))],
        out_specs=pl.BlockSpec((t, MIX), lambda i: (i, 0)),
        out_shape=jax.ShapeDtypeStruct((m, MIX), BF16),
        compiler_params=_cparams(("parallel",)),
    )(h, h, h, h, h, cw)


def _conv_bwd(dcat, h, cw, *, name):
    m = h.shape[0]
    t = TM
    nb = m // t

    def body(dy_ref, dyn_ref, b_ref, c_ref, x_ref, cp_ref, xp_ref, bn_ref, w_ref, o_ref, dw_ref):
        i = pl.program_id(0)

        @pl.when(i == 0)
        def _():
            dw_ref[...] = jnp.zeros_like(dw_ref)

        z, z1, z2 = _conv_z(c_ref, x_ref, cp_ref, xp_ref, i == 0, t)
        w0, w1, w2 = w_ref[0:1, :], w_ref[1:2, :], w_ref[2:3, :]
        dy = dy_ref[...]
        dconv = dy * b_ref[...]
        dnext = jnp.where(i == nb - 1, 0.0, dyn_ref[0:HALO, :] * bn_ref[0:HALO, :])
        dext = jnp.concatenate([dconv, dnext], axis=0)
        d1 = pltpu.roll(dext, t + HALO - 1, 0)[0:t]
        d2 = pltpu.roll(dext, t + HALO - 2, 0)[0:t]
        dz = w2 * dconv + w1 * d1 + w0 * d2
        o_ref[:, 0:MIX] = dy * (w0 * z2 + w1 * z1 + w2 * z)
        o_ref[:, MIX:2 * MIX] = dz * x_ref[...]
        o_ref[:, 2 * MIX:3 * MIX] = dz * c_ref[...]
        dw_ref[0:1, :] += jnp.sum(dconv * z2, axis=0, keepdims=True)
        dw_ref[1:2, :] += jnp.sum(dconv * z1, axis=0, keepdims=True)
        dw_ref[2:3, :] += jnp.sum(dconv * z, axis=0, keepdims=True)

    cur = lambda cb: pl.BlockSpec((t, MIX), lambda i: (i, cb))
    prev = lambda cb: pl.BlockSpec((t, MIX), lambda i: (jnp.maximum(i - 1, 0), cb))
    nxt = lambda cb: pl.BlockSpec((t, MIX), lambda i: (jnp.minimum(i + 1, nb - 1), cb))
    return pl.pallas_call(
        body, name=name, grid=(nb,),
        in_specs=[cur(1), nxt(1), cur(1), cur(2), cur(3), prev(2), prev(3), nxt(1), pl.BlockSpec((3, MIX), lambda i: (0, 0))],
        out_specs=[pl.BlockSpec((t, 3 * MIX), lambda i: (i, 0)), pl.BlockSpec((8, MIX), lambda i: (0, 0))],
        out_shape=[jax.ShapeDtypeStruct((m, 3 * MIX), F32), jax.ShapeDtypeStruct((8, MIX), F32)],
        compiler_params=_cparams(("arbitrary",)),
    )(dcat, dcat, h, h, h, h, h, h, cw)


PHALO = 16


def _pool_pooled(z_ref, zp_ref, i, t):
    z = z_ref[...]
    zp = jnp.where(i == 0, 0.0, zp_ref[t - PHALO:t, :])
    s = jnp.concatenate([zp, z], axis=0)
    sums = {}
    width = 1
    while width < PHALO:
        s = s + pltpu.roll(s, width, 0)
        width *= 2
        sums[width] = s[PHALO:PHALO + t]
    tpos = i * t + lax.broadcasted_iota(jnp.int32, (t, 1), 0)
    outs = []
    for gi, w in enumerate(POOL_WINDOWS):
        lo = gi * POOL_GROUP
        count = jnp.minimum(tpos + 1, w).astype(F32)
        outs.append(sums[w][:, lo:lo + POOL_GROUP] / count - z[:, lo:lo + POOL_GROUP])
    return outs


def _pool_fwd(h, pw, ps, *, name):
    m = h.shape[0]
    t = TM
    nb = m // t

    def body(z_ref, zp_ref, pw_ref, ps_ref, o_ref):
        pooled = _pool_pooled(z_ref, zp_ref, pl.program_id(0), t)
        for gi in range(len(POOL_WINDOWS)):
            lo = gi * POOL_GROUP
            mixed = jnp.dot(pooled[gi].astype(BF16), pw_ref[gi], preferred_element_type=F32)
            o_ref[:, lo:lo + POOL_GROUP] = (mixed * ps_ref[:, lo:lo + POOL_GROUP]).astype(BF16)

    return pl.pallas_call(
        body, name=name, grid=(nb,),
        in_specs=[pl.BlockSpec((t, MIX), lambda i: (i, 3)), pl.BlockSpec((t, MIX), lambda i: (jnp.maximum(i - 1, 0), 3)),
                  pl.BlockSpec((4, POOL_GROUP, POOL_GROUP), lambda i: (0, 0, 0)), pl.BlockSpec((1, MIX), lambda i: (0, 0))],
        out_specs=pl.BlockSpec((t, MIX), lambda i: (i, 0)),
        out_shape=jax.ShapeDtypeStruct((m, MIX), BF16),
        compiler_params=_cparams(("parallel",)),
    )(h, h, pw, ps)


def _pool_bwd(dcat, h, pw, ps, *, name):
    m = h.shape[0]
    t = TM
    nb = m // t

    def body(dy_ref, dyn_ref, z_ref, zp_ref, pw_ref, ps_ref, dz_ref, dpw_ref, dps_ref):
        i = pl.program_id(0)

        @pl.when(i == 0)
        def _():
            dpw_ref[...] = jnp.zeros_like(dpw_ref)
            dps_ref[...] = jnp.zeros_like(dps_ref)

        pooled = _pool_pooled(z_ref, zp_ref, i, t)
        dy = dy_ref[...]
        tpos = i * t + lax.broadcasted_iota(jnp.int32, (t, 1), 0)
        for gi, w in enumerate(POOL_WINDOWS):
            lo = gi * POOL_GROUP
            sl = slice(lo, lo + POOL_GROUP)
            pq = pooled[gi].astype(BF16)
            mixed = jnp.dot(pq, pw_ref[gi], preferred_element_type=F32)
            dps_ref[:, sl] += jnp.sum(dy[:, sl] * mixed, axis=0, keepdims=True)
            dmix = (dy[:, sl] * ps_ref[:, sl]).astype(BF16)
            dpw_ref[gi] += lax.dot_general(pq, dmix, (((0,), (0,)), ((), ())), preferred_element_type=F32)
            dpool = lax.dot_general(dmix, pw_ref[gi], (((1,), (1,)), ((), ())), preferred_element_type=F32)
            dmix_n = (dyn_ref[0:PHALO, sl] * ps_ref[:, sl]).astype(BF16)
            dpool_n = lax.dot_general(dmix_n, pw_ref[gi], (((1,), (1,)), ((), ())), preferred_element_type=F32)
            e = dpool / jnp.minimum(tpos + 1, w).astype(F32)
            e_n = jnp.where(i == nb - 1, 0.0, dpool_n * (1.0 / w))
            f = jnp.concatenate([e, e_n], axis=0)
            width = 1
            while width < w:
                f = f + pltpu.roll(f, t + PHALO - width, 0)
                width *= 2
            dz_ref[:, sl] = f[0:t] - dpool

    return pl.pallas_call(
        body, name=name, grid=(nb,),
        in_specs=[pl.BlockSpec((t, MIX), lambda i: (i, 1)), pl.BlockSpec((t, MIX), lambda i: (jnp.minimum(i + 1, nb - 1), 1)),
                  pl.BlockSpec((t, MIX), lambda i: (i, 3)), pl.BlockSpec((t, MIX), lambda i: (jnp.maximum(i - 1, 0), 3)),
                  pl.BlockSpec((4, POOL_GROUP, POOL_GROUP), lambda i: (0, 0, 0)), pl.BlockSpec((1, MIX), lambda i: (0, 0))],
        out_specs=[pl.BlockSpec((t, MIX), lambda i: (i, 0)), pl.BlockSpec((4, POOL_GROUP, POOL_GROUP), lambda i: (0, 0, 0)),
                   pl.BlockSpec((1, MIX), lambda i: (0, 0))],
        out_shape=[jax.ShapeDtypeStruct((m, MIX), F32), jax.ShapeDtypeStruct((4, POOL_GROUP, POOL_GROUP), F32),
                   jax.ShapeDtypeStruct((1, MIX), F32)],
        compiler_params=_cparams(("arbitrary",)),
    )(dcat, dcat, h, h, pw, ps)


NKEY = 2 * T_ATT


def _band_mask():
    qc = np.arange(T_ATT)[:, None] // CHUNK
    kc = np.arange(NKEY)[None, :] // CHUNK - LEFT_CHUNKS
    return np.where((kc <= qc) & (kc >= qc - LEFT_CHUNKS), 0.0, NEG_INF).astype(np.float32)


def _diag_index():
    c = np.arange(NKEY)
    d = np.where(c <= NKEY // 2 + CHUNK, T_ATT - c, T_ATT + NKEY - c)
    return np.clip(d, -MAX_REL, MAX_REL) + MAX_REL


def _bias_tile(vd_ref, mask_ref, tile_ref):
    col = lax.broadcasted_iota(jnp.int32, (8, NKEY), 1)
    no_prev = jnp.where(col < T_ATT, NEG_INF, 0.0)
    for hh in range(2):
        v = vd_ref[0, hh:hh + 1, :]
        base = jnp.concatenate([v if s == 0 else pltpu.roll(v, s, 1) for s in range(8)], axis=0)
        for mrow in range(T_ATT // 8):
            rows = slice(8 * mrow, 8 * mrow + 8)
            blk = (base if mrow == 0 else pltpu.roll(base, 8 * mrow, 1)) + mask_ref[rows, :]
            tile_ref[hh, rows, :] = blk
            tile_ref[2 + hh, rows, :] = blk + no_prev


BAND_ROWS = 2 * CHUNK
BAND_COLS = (LEFT_CHUNKS + 2) * CHUNK
N_BANDS = T_ATT // BAND_ROWS


def _band(x, r):
    return x[BAND_ROWS * r:BAND_ROWS * (r + 1), BAND_ROWS * r:BAND_ROWS * r + BAND_COLS]


def _from_bands(parts):
    rows = []
    for r, part in enumerate(parts):
        right = NKEY - BAND_COLS - BAND_ROWS * r
        pieces = ([jnp.zeros((BAND_ROWS, BAND_ROWS * r), part.dtype)] if r else []) + [part]
        pieces += [jnp.zeros((BAND_ROWS, right), part.dtype)] if right else []
        rows.append(jnp.concatenate(pieces, axis=1))
    return jnp.concatenate(rows, axis=0)


def _attn_probs(q, kc, tile_ref, idx):
    s = lax.dot_general(q, kc, (((1,), (1,)), ((), ())), preferred_element_type=F32)
    parts = []
    for r in range(N_BANDS):
        sb = _band(s, r) + tile_ref[idx, BAND_ROWS * r:BAND_ROWS * (r + 1), BAND_ROWS * r:BAND_ROWS * r + BAND_COLS]
        p = jnp.exp(sb - jnp.max(sb, axis=-1, keepdims=True))
        parts.append(p * (1.0 / jnp.sum(p, axis=-1, keepdims=True)))
    return parts


def _attn_specs(block):
    cur = lambda base: pl.BlockSpec((T_ATT, 128), lambda hp, i: (block(i), base + hp))
    prev = lambda base: pl.BlockSpec((T_ATT, 128), lambda hp, i: (jnp.maximum(block(i) - 1, 0), base + hp))
    return [cur(0), cur(4), prev(4), cur(8), prev(8),
            pl.BlockSpec((1, 2, NKEY), lambda hp, i: (hp, 0, 0)), pl.BlockSpec((T_ATT, NKEY), lambda hp, i: (0, 0))]


def _attn_fwd(h, vdiag, mask, *, name):
    m = h.shape[0]
    nb = m // T_ATT

    def body(q_ref, k_ref, kp_ref, v_ref, vp_ref, vd_ref, mask_ref, o_ref, tile_ref):
        i = pl.program_id(1)

        @pl.when(i == 0)
        def _():
            _bias_tile(vd_ref, mask_ref, tile_ref)

        first = jnp.where(i == 0, 2, 0)
        outs = []
        for hh in range(2):
            sl = slice(hh * HEAD_DIM, (hh + 1) * HEAD_DIM)
            q = (q_ref[:, sl] * (HEAD_DIM ** -0.5)).astype(BF16)
            kc = jnp.concatenate([kp_ref[:, sl], k_ref[:, sl]], axis=0).astype(BF16)
            vc = jnp.concatenate([vp_ref[:, sl], v_ref[:, sl]], axis=0).astype(BF16)
            p = _from_bands([b.astype(BF16) for b in _attn_probs(q, kc, tile_ref, first + hh)])
            outs.append(jnp.dot(p, vc, preferred_element_type=F32))
        o_ref[...] = jnp.concatenate(outs, axis=1).astype(BF16)

    return pl.pallas_call(
        body, name=name, grid=(ATT_HEADS // 2, nb), in_specs=_attn_specs(lambda i: i),
        out_specs=pl.BlockSpec((T_ATT, 128), lambda hp, i: (i, hp)),
        out_shape=jax.ShapeDtypeStruct((m, MIX), BF16),
        scratch_shapes=[pltpu.VMEM((4, T_ATT, NKEY), F32)],
        compiler_params=_cparams(("parallel", "arbitrary"), VMEM_BIG),
    )(h, h, h, h, h, vdiag, mask)


def _attn_bwd(dcat, h, vdiag, mask, *, name):
    m = h.shape[0]
    nb = m // T_ATT

    def body(do_ref, q_ref, k_ref, kp_ref, v_ref, vp_ref, vd_ref, mask_ref,
             dq_ref, dk_ref, dv_ref, dvd_ref, tile_ref, acc_ref, carry_ref):
        i = pl.program_id(1)

        @pl.when(i == 0)
        def _():
            _bias_tile(vd_ref, mask_ref, tile_ref)
            acc_ref[...] = jnp.zeros_like(acc_ref)

            carry_ref[...] = jnp.zeros_like(carry_ref)

        scale = HEAD_DIM ** -0.5
        first = jnp.where(i == nb - 1, 2, 0)
        dqs, dks, dvs = [], [], []
        for hh in range(2):
            sl = slice(hh * HEAD_DIM, (hh + 1) * HEAD_DIM)
            q = (q_ref[:, sl] * scale).astype(BF16)
            kc = jnp.concatenate([kp_ref[:, sl], k_ref[:, sl]], axis=0).astype(BF16)
            vc = jnp.concatenate([vp_ref[:, sl], v_ref[:, sl]], axis=0).astype(BF16)
            do = do_ref[:, sl].astype(BF16)
            bands = _attn_probs(q, kc, tile_ref, first + hh)
            p = _from_bands([b.astype(BF16) for b in bands])
            dvs.append(lax.dot_general(p, do, (((0,), (0,)), ((), ())), preferred_element_type=F32))
            dp = lax.dot_general(do, vc, (((1,), (1,)), ((), ())), preferred_element_type=F32)
            ds_bands = []
            for r, pb in enumerate(bands):
                dpb = _band(dp, r)
                dsb = pb * (dpb - jnp.sum(dpb * pb, axis=-1, keepdims=True))
                acc_ref[hh, BAND_ROWS * r:BAND_ROWS * (r + 1), BAND_ROWS * r:BAND_ROWS * r + BAND_COLS] += dsb
                ds_bands.append(dsb.astype(BF16))
            dsq = _from_bands(ds_bands)
            dqs.append(jnp.dot(dsq, kc, preferred_element_type=F32) * scale)
            dks.append(lax.dot_general(dsq, q, (((0,), (0,)), ((), ())), preferred_element_type=F32))
        dq_ref[...] = jnp.concatenate(dqs, axis=1)
        dk = jnp.concatenate(dks, axis=1)
        dv = jnp.concatenate(dvs, axis=1)
        dk_ref[...] = dk[T_ATT:] + carry_ref[0]
        dv_ref[...] = dv[T_ATT:] + carry_ref[1]
        carry_ref[0] = dk[:T_ATT]
        carry_ref[1] = dv[:T_ATT]

        @pl.when(i == nb - 1)
        def _():
            for hh in range(2):
                r8 = acc_ref[hh, 0:8, :]
                for mrow in range(1, T_ATT // 8):
                    r8 = r8 + pltpu.roll(acc_ref[hh, 8 * mrow:8 * mrow + 8, :], NKEY - 8 * mrow, 1)
                tot = r8[0:1, :]
                for s in range(1, 8):
                    tot = tot + pltpu.roll(r8[s:s + 1, :], NKEY - s, 1)
                dvd_ref[0, hh:hh + 1, :] = tot

    block = lambda i: nb - 1 - i
    out = pl.BlockSpec((T_ATT, 128), lambda hp, i: (block(i), hp))
    return pl.pallas_call(
        body, name=name, grid=(ATT_HEADS // 2, nb),
        in_specs=[out] + _attn_specs(block),
        out_specs=[out, out, out, pl.BlockSpec((1, 2, NKEY), lambda hp, i: (hp, 0, 0))],
        out_shape=[jax.ShapeDtypeStruct((m, MIX), F32)] * 3 + [jax.ShapeDtypeStruct((ATT_HEADS // 2, 2, NKEY), F32)],
        scratch_shapes=[pltpu.VMEM((4, T_ATT, NKEY), F32), pltpu.VMEM((2, T_ATT, NKEY), F32), pltpu.VMEM((2, T_ATT, 128), F32)],
        compiler_params=_cparams(("parallel", "arbitrary"), VMEM_BIG),
    )(dcat, h, h, h, h, h, vdiag, mask)


def _row_tile(rows):
    for t in (512, 256, 128, 64, 32, 16, 8):
        if rows % t == 0:
            return t
    return rows


def _adamw(w, g, mom, var, *, name):
    rows, cols = w.shape
    t = _row_tile(rows)

    def body(w_ref, g_ref, m_ref, v_ref, d_ref, mo_ref, vo_ref):
        g_ = g_ref[...]
        m_ = ADAM_B1 * m_ref[...] + (1.0 - ADAM_B1) * g_
        v_ = ADAM_B2 * v_ref[...] + (1.0 - ADAM_B2) * (g_ * g_)
        m_hat = m_ / (1.0 - ADAM_B1 ** ADAM_STEP)
        v_hat = v_ / (1.0 - ADAM_B2 ** ADAM_STEP)
        d_ref[...] = -ADAM_LR * (m_hat / (jnp.sqrt(v_hat) + ADAM_EPS) + ADAM_WD * w_ref[...])
        mo_ref[...] = m_
        vo_ref[...] = v_

    spec = pl.BlockSpec((t, cols), lambda i: (i, 0))
    return pl.pallas_call(
        body, name=name, grid=(rows // t,), in_specs=[spec] * 4, out_specs=[spec] * 3,
        out_shape=[jax.ShapeDtypeStruct((rows, cols), F32)] * 3, compiler_params=_cparams(("parallel",)),
    )(w, g, mom, var)


ANY = pl.BlockSpec(memory_space=pl.ANY)


def _place():
    x, y, c = lax.axis_index("x"), lax.axis_index("y"), lax.axis_index("c")
    chips = [(1 - x, y), (x, 1 - y), (1 - x, 1 - y)]
    return x, y, c, chips


class _GatherExchange:
    def __init__(self, ws):
        n = len(ws)
        self.ins = list(ws)
        self.out_shapes = [jax.ShapeDtypeStruct((N_CHIPS,) + w.shape, w.dtype) for w in ws]
        self.sems = [pltpu.SemaphoreType.DMA((6 * n,)), pltpu.SemaphoreType.DMA((6 * n,))]

    def _copies(self, ins, outs, sems, onward=True):
        send_sems, recv_sems = sems
        x, y, c, chips = _place()
        me = 2 * x + y

        def region(k, j, chip_index, rows, to):
            ref = outs[k].at[chip_index, rows]
            return pltpu.make_async_remote_copy(
                src_ref=ref, dst_ref=ref, send_sem=send_sems.at[6 * k + j], recv_sem=recv_sems.at[6 * k + j],
                device_id=to, device_id_type=MESH)

        first, landed, passed, handed = [], [], [], []
        for k in range(len(ins)):
            half = ins[k].shape[0] // 2
            mine, theirs = pl.ds(c * half, half), pl.ds((1 - c) * half, half)
            for j, chip in enumerate(chips):
                first.append(pltpu.make_async_remote_copy(
                    src_ref=ins[k].at[mine], dst_ref=outs[k].at[me, mine], send_sem=send_sems.at[6 * k + j],
                    recv_sem=recv_sems.at[6 * k + j], device_id=(*chip, c), device_id_type=MESH))
                if onward:
                    landed.append(region(k, j, 2 * chip[0] + chip[1], mine, (*chip, c)))
                    passed.append(region(k, 3 + j, 2 * chip[0] + chip[1], mine, (x, y, 1 - c)))
                    handed.append(region(k, 3 + j, 2 * chip[0] + chip[1], theirs, (x, y, 1 - c)))
        return first, landed, passed, handed

    def start(self, ins, outs, sems):
        for cp in self._copies(ins, outs, sems, onward=False)[0]:
            cp.start()

    def finish(self, ins, outs, sems):
        first, landed, passed, handed = self._copies(ins, outs, sems)
        for arrived, onward in zip(landed, passed):
            arrived.wait_recv()
            onward.start()
        for cp in handed:
            cp.wait_recv()
        for cp in first + passed:
            cp.wait_send()


class _ReduceExchange:
    def __init__(self, grads, axes):
        self.ins = list(grads)
        self.axes = list(axes)
        n = len(grads)
        self.out_shapes = [jax.ShapeDtypeStruct((N_DEV - 1,) + self._block(g, a), g.dtype) for g, a in zip(grads, axes)]
        self.sems = [pltpu.SemaphoreType.DMA((7 * n,)), pltpu.SemaphoreType.DMA((7 * n,))]

    @staticmethod
    def _block(g, axis):
        k, n = g.shape
        return (k // 2, n // N_CHIPS) if axis == 2 else (k // N_DEV, n)

    def _copies(self, ins, outs, sems):
        send_sems, recv_sems = sems
        x, y, c, _ = _place()
        cps = []
        for w, (g, axis) in enumerate(zip(ins, self.axes)):
            rows, cols = self._block(g, axis)
            for k in range(1, N_DEV):
                tx, ty, tc = (1 - x if k & 4 else x), (1 - y if k & 2 else y), (1 - c if k & 1 else c)
                chip = 2 * tx + ty
                if axis == 2:
                    src = g.at[pl.ds(tc * rows, rows), pl.ds(chip * cols, cols)]
                else:
                    src = g.at[pl.ds((2 * chip + tc) * rows, rows), :]
                cps.append(pltpu.make_async_remote_copy(
                    src_ref=src, dst_ref=outs[w].at[k - 1], send_sem=send_sems.at[7 * w + k - 1],
                    recv_sem=recv_sems.at[7 * w + k - 1], device_id=(tx, ty, tc), device_id_type=MESH))
        return cps

    def start(self, ins, outs, sems):
        for cp in self._copies(ins, outs, sems):
            cp.start()

    def finish(self, ins, outs, sems):
        cps = self._copies(ins, outs, sems)
        for cp in cps:
            cp.wait_recv()
        for cp in cps:
            cp.wait_send()


def _run_exchange(ex, *, name):
    n_in, n_out = len(ex.ins), len(ex.out_shapes)

    def body(*refs):
        ins, outs, sems = refs[:n_in], refs[n_in:n_in + n_out], refs[n_in + n_out:]
        ex.start(ins, outs, sems)
        ex.finish(ins, outs, sems)

    return pl.pallas_call(body, name=name, in_specs=[ANY] * n_in, out_specs=[ANY] * n_out, out_shape=ex.out_shapes,
                          scratch_shapes=ex.sems)(*ex.ins)


def _all_reduce_small(buf, *, name):
    rows = buf.shape[0]

    def body(x_ref, sum_ref, all_ref, send_sems, recv_sems, local_sem):
        x, y, c, chips = _place()
        me, sibling = (x, y, c), (x, y, 1 - c)

        def slab(px, py, pc):
            return all_ref.at[pl.ds((4 * px + 2 * py + pc) * rows, rows), :]

        def copy(k, block, to, src=None):
            return pltpu.make_async_remote_copy(
                src_ref=slab(*block) if src is None else src, dst_ref=slab(*block), send_sem=send_sems.at[k],
                recv_sem=recv_sems.at[k], device_id=to, device_id_type=MESH)

        mine = pltpu.make_async_copy(x_ref, slab(*me), local_sem)
        mine.start()
        first = [copy(0, me, sibling, src=x_ref)]
        first += [copy(1 + j, me, (*chip, c), src=x_ref) for j, chip in enumerate(chips)]
        for cp in first:
            cp.start()
        passed = [copy(4 + j, (*chip, c), sibling) for j, chip in enumerate(chips)]
        for j, chip in enumerate(chips):
            copy(1 + j, (*chip, c), me).wait_recv()
            passed[j].start()
        copy(0, sibling, me).wait_recv()
        for j, chip in enumerate(chips):
            copy(4 + j, (*chip, 1 - c), me).wait_recv()
        for cp in first + passed:
            cp.wait_send()
        mine.wait()
        acc = all_ref[0:rows, :]
        for d in range(1, N_DEV):
            acc = acc + all_ref[d * rows:(d + 1) * rows, :]
        sum_ref[...] = acc

    vmem = pl.BlockSpec(memory_space=pltpu.VMEM)
    return pl.pallas_call(
        body, name=name, in_specs=[vmem], out_specs=[vmem, vmem],
        out_shape=[jax.ShapeDtypeStruct((rows, 128), F32), jax.ShapeDtypeStruct((N_DEV * rows, 128), F32)],
        scratch_shapes=[pltpu.SemaphoreType.DMA((7,)), pltpu.SemaphoreType.DMA((7,)), pltpu.SemaphoreType.DMA],
        compiler_params=pltpu.CompilerParams(vmem_limit_bytes=VMEM_BIG),
    )(buf)[0]


WEIGHTS = ['ev_w_in', 'ev_lambda_re', 'ev_lambda_im', 'ev_log_dt', 'ev_b_re', 'ev_b_im', 'ev_c_re', 'ev_c_im', 'ev_d',
           'ev_w_glu', 'ev_b_glu', 'ev_conv_w', 'ev_w_out', 'od_w_in', 'od_rel_bias', 'od_pool_w', 'od_pool_scale',
           'od_w_out', 'ln_mix_g', 'ln_mix_b', 'ln_ffn_g', 'ln_ffn_b', 'ffn_w_up', 'ffn_w_down', 'ple_w_proj',
           'ple_w_gate', 'ple_b_gate']
INPUTS = ['x', 'p'] + WEIGHTS + ['loss_target'] + ['m_' + n for n in WEIGHTS] + ['v_' + n for n in WEIGHTS]

BIG = {
    'ev_w_in': (2, (2, 1024, 2048)), 'ev_w_glu': (1, (2, 512, 512)), 'ev_w_out': (1, (2, 1024, 1024)),
    'od_w_in': (2, (2, 1024, 2048)), 'od_w_out': (1, (2, 1024, 1024)), 'ffn_w_up': (2, (4, 1024, 5632)),
    'ffn_w_down': (1, (4, 2816, 1024)), 'ple_w_proj': (2, (4, 256, 1024)), 'ple_w_gate': (1, (4, 1024, 1024)),
}
SMALL_SHARDED = {'ev_conv_w': (2, 3, 512), 'od_pool_scale': (2, 512)}
REPLICATED = [n for n in WEIGHTS if n not in BIG and n not in SMALL_SHARDED]


def _shard_rows(name):
    axis, (nl, k, n) = BIG[name]
    return (nl * k, n // N_CHIPS) if axis == 2 else (nl * k // N_CHIPS, n)


def _pack(arrs):
    flat = jnp.concatenate([a.reshape(-1) for a in arrs])
    total = flat.shape[0]
    padded = -(-total // 1024) * 1024
    return jnp.pad(flat, (0, padded - total)).reshape(padded // 128, 128)


def _unpack(buf, shapes):
    flat = buf.reshape(-1)
    out, pos = [], 0
    for s in shapes:
        size = int(np.prod(s))
        out.append(flat[pos:pos + size].reshape(s))
        pos += size
    return out


def _s5_params(lam_re, lam_im, log_dt, b_re, b_im, c_re, c_im):
    dt = jnp.exp(log_dt)[:, None]
    mag = jnp.exp(lam_re * dt)
    ang = lam_im * dt
    lb_re = mag * jnp.cos(ang)
    lb_im = mag * jnp.sin(ang)
    den = lam_re * lam_re + lam_im * lam_im
    nr = lb_re - 1.0
    ni = lb_im
    r_re = (nr * lam_re + ni * lam_im) / den
    r_im = (ni * lam_re - nr * lam_im) / den
    bb_re = r_re[..., None] * b_re - r_im[..., None] * b_im
    bb_im = r_re[..., None] * b_im + r_im[..., None] * b_re
    per = S5_GROUPS // S5_SLABS
    eye = jnp.eye(per, dtype=F32)

    def block_diag(a):
        _, r, c = a.shape
        a = a.reshape(S5_SLABS, per, r, c)
        return (a[:, :, :, None, :] * eye[None, :, None, :, None]).reshape(S5_SLABS, per * r, per * c)

    bmat = jnp.concatenate([block_diag(bb_re.transpose(0, 2, 1)), block_diag(bb_im.transpose(0, 2, 1))], axis=2)
    cmat = jnp.concatenate([block_diag(c_re.transpose(0, 2, 1)), block_diag(-c_im.transpose(0, 2, 1))], axis=1)
    lam = jnp.stack([lb_re.reshape(S5_N), lb_im.reshape(S5_N)])
    return lam, bmat, cmat


def _lam_powers(lam):
    res, ims = [lam[0]], [lam[1]]
    for _ in range(7):
        res, ims = res + [res[-1] * lam[0] - ims[-1] * lam[1]], ims + [res[-1] * lam[1] + ims[-1] * lam[0]]
    sq_r, sq_i = [res[-1]], [ims[-1]]
    for _ in range(N_SQUARES):
        sq_r, sq_i = sq_r + [sq_r[-1] * sq_r[-1] - sq_i[-1] * sq_i[-1]], sq_i + [2.0 * sq_r[-1] * sq_i[-1]]
    return jnp.stack(res + ims + res[::-1] + ims[::-1] + sq_r[1:] + sq_i[1:])


def _layer_big(i):
    mixer = [('w_in', 'ev_w_in'), ('w_glu', 'ev_w_glu'), ('w_out', 'ev_w_out')] if i % 2 == 0 else \
        [('w_in', 'od_w_in'), ('w_out', 'od_w_out')]
    ffn = [('w_up', 'ffn_w_up'), ('w_down', 'ffn_w_down'), ('w_proj', 'ple_w_proj'), ('w_gate', 'ple_w_gate')]
    return [(k, n, i // 2) for k, n in mixer] + [(k, n, i) for k, n in ffn]


class _WholePlan:
    def __init__(self, whole):
        self.whole = whole
        self.grads = {n: {} for n in BIG}

    def layer_weights(self, i):
        return {k: self.whole[n][l] for k, n, l in _layer_big(i)}

    def forward_host(self, i):
        return None

    def backward_host(self, i):
        return None

    def early_host(self, i, g):
        return None

    def layer_grads(self, i, g):
        for k, n, l in _layer_big(i):
            self.grads[n][l] = g[k][0]


def _local_step(x, p, target, w, plan):
    mask = jnp.asarray(_band_mask())
    diag_idx = _diag_index()
    onehot = jnp.asarray(np.eye(2 * MAX_REL + 1, dtype=np.float32)[diag_idx])
    saved = []
    for i in range(DEPTH):
        li = i // 2
        lw = plan.layer_weights(i)
        s = {'x0': x, 'lw': lw}
        h = _mm([(x, 0, D_MODEL)], lw['w_in'], tn=4 * MIX, name=f"in_proj")
        if i % 2 == 0:
            (lam, bmat, cmat), s5_vjp = jax.vjp(
                _s5_params, w['ev_lambda_re'][li], w['ev_lambda_im'][li], w['ev_log_dt'][li], w['ev_b_re'][li],
                w['ev_b_im'][li], w['ev_c_re'][li], w['ev_c_im'][li])
            s5c = (bmat.astype(BF16), cmat.astype(BF16), w['ev_d'][li].reshape(1, MIX), lw['w_glu'],
                   w['ev_b_glu'][li].reshape(1, MIX), _lam_powers(lam))
            ya, ypre, hb = _s5_fwd(h, *s5c, name=f"s5_fwd")
            yb = _conv_fwd(h, w['ev_conv_w'][li], name=f"conv_fwd")
            s.update(s5_vjp=s5_vjp, s5c=s5c, ypre=ypre, hb=hb)
        else:
            vdiag = jnp.dot(w['od_rel_bias'][li], onehot.T, precision=HIGHEST).reshape(ATT_HEADS // 2, 2, NKEY)
            pw = w['od_pool_w'][li].astype(BF16)
            ps = w['od_pool_scale'][li].reshape(1, MIX)
            ya = _attn_fwd(h, vdiag, mask, name=f"attn_fwd")
            yb = _pool_fwd(h, pw, ps, name=f"pool_fwd")
            s.update(vdiag=vdiag, pw=pw, ps=ps)
        wout = lw['w_out']
        vec = lambda n: w[n][i].reshape(1, -1)

        def residual_ln(products, rows, vecs):
            r = ALPHA * rows[0] + products[0]
            return (r, _ln_apply(r, vecs[0], vecs[1])), ()

        def embed_gate(products, rows, vecs):
            gate = _sigmoid(products[0] + vecs[0])
            return (rows[0] + gate * products[1], gate, products[1]), ()

        two_f32 = [(D_MODEL, F32), (D_MODEL, F32)]
        r1, x1 = _mm_rows([([(ya, 0, MIX), (yb, 0, MIX)], wout, False)], [x], [vec('ln_mix_g'), vec('ln_mix_b')],
                          two_f32, [], residual_ln, name="out_proj_ln")
        hosted = plan.forward_host(i)
        if hosted is None:
            a, gg, uu = _ffn_up(x1, lw['w_up'], name=f"ffn_up")
        else:
            (a, gg, uu), arrived = _ffn_up(x1, lw['w_up'], exchange=hosted, name=f"ffn_up_gather")
            plan.forward_hosted(i, arrived)
        r2, x2 = _mm_rows([([(a, 0, D_FF)], lw['w_down'], False)], [x1], [vec('ln_ffn_g'), vec('ln_ffn_b')],
                          two_f32, [], residual_ln, name="ffn_down_ln")
        x3, gate, ppb = _mm_rows(
            [([(None, 0, D_MODEL)], lw['w_gate'], False), ([(p[i], 0, D_PLE)], lw['w_proj'], False)],
            [x2], [vec('ple_b_gate')], [(D_MODEL, F32), (D_MODEL, BF16), (D_MODEL, BF16)], [], embed_gate, name="ple")
        s.update(h=h, ya=ya, yb=yb, r1=r1, x1=x1, a=a, gg=gg, uu=uu, r2=r2, x2=x2, gate=gate, ppb=ppb)
        saved.append(s)
        x = x3

    loss, da = _loss_head(x, target, name="loss_head")
    grads = {n: [None] * (DEPTH if n.startswith(('ln_', 'ple_')) else DEPTH // 2) for n in WEIGHTS if n not in BIG}

    def both(pieces, axis):
        return tuple(jnp.concatenate([pc[k] for pc in pieces], axis=axis) for k in range(2))

    for i in reversed(range(DEPTH)):
        li = i // 2
        s = saved[i]
        lw = s['lw']
        big = {}
        dz, dpp, dr2, dbg, dg2, db2 = _ple_ln_bwd(da, s['gate'], s['ppb'], s['r2'], lw['w_gate'],
                                                  w['ln_ffn_g'][i].reshape(1, -1), name="ple_ln_bwd")
        grads['ple_b_gate'][i] = dbg.reshape(-1)
        big['w_gate'] = _mm_tn(s['x2'], 0, D_MODEL, dz, also_bf16=True, name=f"d_ple_gate")
        big['w_proj'] = _mm_tn(p[i], 0, D_PLE, dpp, also_bf16=True, name=f"d_ple_proj")
        grads['ln_ffn_g'][i] = dg2.reshape(-1)
        grads['ln_ffn_b'][i] = db2.reshape(-1)
        dhh = _ffn_down_bwd(dr2, lw['w_down'], s['gg'], s['uu'], name=f"ffn_down_bwd")
        big['w_down'] = _mm_tn(s['a'], 0, D_FF, dr2, tk=D_FF // 2, also_bf16=True, name=f"d_ffn_down")
        hosted = plan.backward_host(i)
        if hosted is None:
            big['w_up'] = _mm_tn(s['x1'], 0, D_MODEL, dhh, tn=D_FF // 2, also_bf16=True, name=f"d_ffn_up")
        else:
            big['w_up'], arrived = _mm_tn(s['x1'], 0, D_MODEL, dhh, tn=D_FF // 2, also_bf16=True, exchange=hosted,
                                          name=f"d_ffn_up_reduce_{i % 2}")
            plan.backward_hosted(i, arrived)

        def ln_mix_grad(products, rows, vecs):
            dr, dg, dbias = _ln_grad(rows[0], ALPHA * rows[1] + products[0], vecs[0])
            return (dr,), (dg, dbias)

        dr1, dg1, db1 = _mm_rows([([(dhh, 0, 2 * D_FF)], lw['w_up'], True)], [s['r1'], dr2],
                                 [w['ln_mix_g'][i].reshape(1, -1)], [(D_MODEL, F32)], [D_MODEL, D_MODEL], ln_mix_grad,
                                 tm=TM, vmem=VMEM_BIG, name="ffn_up_ln_bwd")
        grads['ln_mix_g'][i] = dg1.reshape(-1)
        grads['ln_mix_b'][i] = db1.reshape(-1)
        dcat = _mm([(dr1, 0, D_MODEL)], lw['w_out'], trans_b=True, name=f"out_proj_bwd")
        big['w_out'] = both([_mm_tn(s['ya'], 0, MIX, dr1, also_bf16=True, name=f"d_out_a"),
                             _mm_tn(s['yb'], 0, MIX, dr1, also_bf16=True, name=f"d_out_b")], 0)
        h = s['h']
        if i % 2 == 0:
            s5c = s['s5c']
            hosted = plan.early_host(i, big)
            if hosted is None:
                s5_out = _s5_bwd(dcat, s['ypre'], h, s['hb'], *s5c, name=f"s5_bwd")
            else:
                s5_out, arrived = _s5_bwd(dcat, s['ypre'], h, s['hb'], *s5c, exchange=hosted, name=f"s5_bwd_reduce")
                plan.early_hosted(i, arrived)
            du, xb, gb, gq, dzzq, dyq, dlam, dbglu, dd = s5_out
            dbmat = _mm_tn_slabs(h, 128, gb, SLAB_COLS, S5_SLABS, name=f"d_s5_b")
            dcmat = _mm_tn_slabs(xb, SLAB_COLS, dyq, 128, S5_SLABS, name=f"d_s5_c")
            s5g = s['s5_vjp']((dlam, dbmat, dcmat))
            for n, g_ in zip(['ev_lambda_re', 'ev_lambda_im', 'ev_log_dt', 'ev_b_re', 'ev_b_im', 'ev_c_re', 'ev_c_im'], s5g):
                grads[n][li] = g_
            big['w_glu'] = _mm_tn(gq, 0, MIX, dzzq, also_bf16=True, name=f"d_glu")
            grads['ev_b_glu'][li] = dbglu.reshape(-1)
            grads['ev_d'][li] = dd.reshape(-1)
            d3, dcw = _conv_bwd(dcat, h, w['ev_conv_w'][li], name=f"conv_bwd")
            grads['ev_conv_w'][li] = dcw[0:3]
            big['w_in'] = both([_mm_tn(s['x0'], 0, D_MODEL, du, also_bf16=True, name=f"d_in_a"),
                                _mm_tn(s['x0'], 0, D_MODEL, d3, tn=3 * MIX, also_bf16=True, name=f"d_in_b")], 1)
            dh_parts = [(du, 0, MIX), (d3, 0, 3 * MIX)]
        else:
            dq, dk, dv, dvd = _attn_bwd(dcat, h, s['vdiag'], mask, name=f"attn_bwd")
            dzp, dpw, dps = _pool_bwd(dcat, h, s['pw'], s['ps'], name=f"pool_bwd")
            parts = [dq, dk, dv, dzp]
            grads['od_rel_bias'][li] = jnp.dot(dvd.reshape(ATT_HEADS, NKEY), onehot, precision=HIGHEST)
            grads['od_pool_w'][li] = dpw
            grads['od_pool_scale'][li] = dps.reshape(-1)
            big['w_in'] = both([_mm_tn(s['x0'], 0, D_MODEL, d_, also_bf16=True, name=f"d_in_a") for d_ in parts], 1)
            dh_parts = [(d_, 0, MIX) for d_ in parts]
        plan.layer_grads(i, big)

        def layer_input_grad(products, rows, vecs):
            return (ALPHA * rows[0] + products[0],), ()

        (da,) = _mm_rows([(dh_parts, lw['w_in'], True)], [dr1], [], [(D_MODEL, F32)], [], layer_input_grad,
                         name=f"in_proj_bwd_{i % 2}")
    return loss, da, {n: jnp.stack(g) for n, g in grads.items()}


def _sum_blocks(own, others, *, name):
    rows, cols = own.shape
    t = _row_tile(rows)

    def body(own_ref, others_ref, o_ref):
        acc = own_ref[...]
        for k in range(N_DEV - 1):
            acc = acc + others_ref[k].astype(F32)
        o_ref[...] = acc

    return pl.pallas_call(
        body, name=name, grid=(rows // t,),
        in_specs=[pl.BlockSpec((t, cols), lambda i: (i, 0)), pl.BlockSpec((N_DEV - 1, t, cols), lambda i: (0, i, 0))],
        out_specs=pl.BlockSpec((t, cols), lambda i: (i, 0)), out_shape=jax.ShapeDtypeStruct((rows, cols), F32),
        compiler_params=_cparams(("parallel",)),
    )(own, others)


def _swap_sibling(arrs, *, name):
    n = len(arrs)

    def body(*refs):
        ins, outs = refs[:n], refs[n:2 * n]
        send_sems, recv_sems = refs[2 * n:]
        x, y, c, _ = _place()
        cps = [pltpu.make_async_remote_copy(src_ref=ins[k], dst_ref=outs[k], send_sem=send_sems.at[k], recv_sem=recv_sems.at[k],
                                            device_id=(x, y, 1 - c), device_id_type=MESH) for k in range(n)]
        for cp in cps:
            cp.start()
        for cp in cps:
            cp.wait_recv()
        for cp in cps:
            cp.wait_send()

    return pl.pallas_call(
        body, name=name, in_specs=[ANY] * n, out_specs=[ANY] * n,
        out_shape=[jax.ShapeDtypeStruct(a.shape, a.dtype) for a in arrs],
        scratch_shapes=[pltpu.SemaphoreType.DMA((n,)), pltpu.SemaphoreType.DMA((n,))],
    )(*arrs)


class _ShardedPlan:
    def __init__(self, a, c, me):
        self.a, self.c, self.me = a, c, me
        self.weights, self.pending, self.own, self.arrived = {}, None, {}, {}

    def _shards(self, i):
        return [self.a[n][l].astype(BF16) for _, n, l in _layer_big(i)]

    def _with_own(self, gathered, own):
        return lax.dynamic_update_index_in_dim(gathered, own, self.me, 0)

    def _set_weights(self, i, gathered):
        lw = {}
        for (k, n, _), g, own in zip(_layer_big(i), gathered, self._shards(i)):
            _, rows, cols = g.shape
            g = self._with_own(g, own)
            lw[k] = g.transpose(1, 0, 2).reshape(rows, N_CHIPS * cols) if BIG[n][0] == 2 else g.reshape(N_CHIPS * rows, cols)
        self.weights[i] = lw

    def gather_first(self, misc):
        gathered = _run_exchange(_GatherExchange(self._shards(0) + [misc]), name="weight_gather_0")
        self._set_weights(0, gathered[:-1])
        return self._with_own(gathered[-1], misc)

    def layer_weights(self, i):
        return self.weights.pop(i)

    def forward_host(self, i):
        return _GatherExchange(self._shards(i + 1)) if i + 1 < DEPTH else None

    def forward_hosted(self, i, arrived):
        self._set_weights(i + 1, arrived)

    EARLY = ('w_up', 'w_down', 'w_proj', 'w_gate')

    def _reduce_exchange(self, i, g, early=None):
        items = [(k, n, l) for k, n, l in _layer_big(i) if early is None or (k in self.EARLY) == early]
        return [(n, l) for _, n, l in items], _ReduceExchange([g[k][1] for k, _, _ in items], [BIG[n][0] for _, n, _ in items])

    def early_host(self, i, g):
        if i != 0:
            return None
        self.early_keys, exchange = self._reduce_exchange(0, g, early=True)
        return exchange

    def early_hosted(self, i, arrived):
        self.arrived.update(zip(self.early_keys, arrived))

    def layer_grads(self, i, g):
        for k, n, l in _layer_big(i):
            full = g[k][0]
            kk, nn = full.shape
            if BIG[n][0] == 2:
                self.own[n, l] = lax.dynamic_slice(full, (self.c * (kk // 2), self.me * (nn // N_CHIPS)), (kk // 2, nn // N_CHIPS))
            else:
                self.own[n, l] = lax.dynamic_slice_in_dim(full, (2 * self.me + self.c) * (kk // N_DEV), kk // N_DEV, axis=0)
        self.pending = (i, g)

    def backward_host(self, i):
        if i + 1 >= DEPTH:
            return None
        self.hosted_keys, exchange = self._reduce_exchange(*self.pending)
        return exchange

    def backward_hosted(self, i, arrived):
        self.arrived.update(zip(self.hosted_keys, arrived))

    def reduced(self):
        late_keys, exchange = self._reduce_exchange(*self.pending, early=False)
        self.arrived.update(zip(late_keys, _run_exchange(exchange, name="grad_reduce_0")))
        keys = [(n, l) for n in BIG for l in range(BIG[n][1][0])]
        mine = [_sum_blocks(self.own[k], self.arrived[k], name=f"grad_sum_{k[0]}") for k in keys]
        theirs = _swap_sibling(mine, name="grad_half_swap")
        out = {}
        for n in BIG:
            layers = []
            for l in range(BIG[n][1][0]):
                a_, b_ = mine[keys.index((n, l))], theirs[keys.index((n, l))]
                layers.append(jnp.where(self.c == 0, jnp.concatenate([a_, b_], axis=0), jnp.concatenate([b_, a_], axis=0)))
            out[n] = jnp.stack(layers)
        return out


def kernel(x, p, ev_w_in, ev_lambda_re, ev_lambda_im, ev_log_dt, ev_b_re, ev_b_im, ev_c_re, ev_c_im, ev_d, ev_w_glu, ev_b_glu, ev_conv_w, ev_w_out, od_w_in, od_rel_bias, od_pool_w, od_pool_scale, od_w_out, ln_mix_g, ln_mix_b, ln_ffn_g, ln_ffn_b, ffn_w_up, ffn_w_down, ple_w_proj, ple_w_gate, ple_b_gate, loss_target, m_ev_w_in, m_ev_lambda_re, m_ev_lambda_im, m_ev_log_dt, m_ev_b_re, m_ev_b_im, m_ev_c_re, m_ev_c_im, m_ev_d, m_ev_w_glu, m_ev_b_glu, m_ev_conv_w, m_ev_w_out, m_od_w_in, m_od_rel_bias, m_od_pool_w, m_od_pool_scale, m_od_w_out, m_ln_mix_g, m_ln_mix_b, m_ln_ffn_g, m_ln_ffn_b, m_ffn_w_up, m_ffn_w_down, m_ple_w_proj, m_ple_w_gate, m_ple_b_gate, v_ev_w_in, v_ev_lambda_re, v_ev_lambda_im, v_ev_log_dt, v_ev_b_re, v_ev_b_im, v_ev_c_re, v_ev_c_im, v_ev_d, v_ev_w_glu, v_ev_b_glu, v_ev_conv_w, v_ev_w_out, v_od_w_in, v_od_rel_bias, v_od_pool_w, v_od_pool_scale, v_od_w_out, v_ln_mix_g, v_ln_mix_b, v_ln_ffn_g, v_ln_ffn_b, v_ffn_w_up, v_ffn_w_down, v_ple_w_proj, v_ple_w_gate, v_ple_b_gate):
    given = locals()
    a = {n: given[n] for n in INPUTS}
    x, y, c = lax.axis_index("x"), lax.axis_index("y"), lax.axis_index("c")
    me = 2 * x + y

    plan = _ShardedPlan(a, c, me)
    misc = jnp.concatenate([a['ev_conv_w'].reshape(6, 128), a['od_pool_scale'], jnp.zeros((8, 128), F32)], axis=0)
    gm = plan.gather_first(misc)
    w = {n: a[n] for n in REPLICATED}
    w['ev_conv_w'] = gm[:, 0:6].reshape(N_CHIPS, 2, 3, 128).transpose(1, 2, 0, 3).reshape(2, 3, 512)
    w['od_pool_scale'] = gm[:, 6:8].transpose(1, 0, 2).reshape(2, 512)

    loss, grad_x, grads = _local_step(a['x'][0], a['p'][:, 0], a['loss_target'][0], w, plan)
    loss = lax.psum(loss[0, 0], ("x", "y", "c"))

    small_names = REPLICATED + list(SMALL_SHARDED)
    small = _all_reduce_small(_pack([grads[n] for n in small_names]), name="small_grad_all_reduce")
    small = dict(zip(small_names, _unpack(small, [grads[n].shape for n in small_names])))
    for n in SMALL_SHARDED:
        small[n] = lax.dynamic_slice_in_dim(small[n], me * 128, 128, axis=small[n].ndim - 1)
    big = plan.reduced()

    res = {}
    for n in BIG:
        shape = a[n].shape
        flat = _shard_rows(n)
        d, m_, v_ = _adamw(a[n].reshape(flat), big[n].reshape(flat), a['m_' + n].reshape(flat), a['v_' + n].reshape(flat),
                           name=f"adamw_{n}")
        res[n] = (big[n], d.reshape(shape), m_.reshape(shape), v_.reshape(shape))
    shapes = [a[n].shape for n in small_names]
    d, m_, v_ = _adamw(_pack([a[n] for n in small_names]), _pack([small[n] for n in small_names]),
                       _pack([a['m_' + n] for n in small_names]), _pack([a['v_' + n] for n in small_names]), name="adamw_small")
    for n, dd, mm, vv in zip(small_names, _unpack(d, shapes), _unpack(m_, shapes), _unpack(v_, shapes)):
        res[n] = (small[n], dd, mm, vv)

    outs = [loss, grad_x[None]]
    for part in range(4):
        outs += [res[n][part] for n in WEIGHTS]
    return tuple(outs)
```

```python
import functools
import math

import jax
import jax.numpy as jnp
import numpy as np
from jax import lax
from jax.experimental import pallas as pl
from jax.experimental.pallas import tpu as pltpu

F32 = jnp.float32
BF16 = jnp.bfloat16
MESH = pl.DeviceIdType.MESH
HIGHEST = lax.Precision.HIGHEST

D_MODEL = 1024
DEPTH = 4
MIX = 512
S5_GROUPS = 32
S5_GROUP = 16
S5_STATE = 64
S5_N = S5_GROUPS * S5_STATE
CHUNK = 64
LEFT_CHUNKS = 8
MAX_REL = 128
ATT_HEADS = 8
HEAD_DIM = 64
POOL_WINDOWS = (2, 4, 8, 16)
POOL_GROUP = 128
D_FF = 2816
D_PLE = 256
ALPHA = (2 * DEPTH) ** 0.25
LN_EPS = 1e-5
NEG_INF = -1e30
N_CHIPS = 4
N_DEV = 8

ADAM_LR = 0.001
ADAM_B1 = 0.9
ADAM_B2 = 0.999
ADAM_EPS = 1e-08
ADAM_WD = 0.01
ADAM_STEP = 10

TM = 512
T_S5 = 512
T_ATT = 512
VMEM_BIG = 56 * 1024 * 1024


VMEM_DEFAULT = 48 * 1024 * 1024


def _cparams(sem, vmem=None):
    return pltpu.CompilerParams(dimension_semantics=sem, vmem_limit_bytes=vmem or VMEM_DEFAULT)


def _sigmoid(x):
    return 0.5 + 0.5 * jnp.tanh(0.5 * x)


def _mm(a_parts, b, *, name, trans_b=False, out_dtype=F32, tm=TM, tn=1024, vmem=None):
    m = a_parts[0][0].shape[0]
    n = b.shape[0] if trans_b else b.shape[1]
    kk = b.shape[1] if trans_b else b.shape[0]
    tn = min(tn, n)
    widths = [w for _, _, w in a_parts]
    assert sum(widths) == kk and m % tm == 0 and n % tn == 0
    na = len(a_parts)

    def body(*refs):
        b_ref, o_ref = refs[na], refs[na + 1]
        acc = None
        k0 = 0
        for ar, w in zip(refs[:na], widths):
            a = ar[...].astype(BF16)
            if trans_b:
                part = lax.dot_general(a, b_ref[:, k0:k0 + w], (((1,), (1,)), ((), ())), preferred_element_type=F32)
            else:
                part = jnp.dot(a, b_ref[k0:k0 + w, :], preferred_element_type=F32)
            acc = part if acc is None else acc + part
            k0 += w
        o_ref[...] = acc.astype(o_ref.dtype)

    in_specs = [pl.BlockSpec((tm, w), functools.partial(lambda j, i, cb: (i, cb), cb=cb)) for _, cb, w in a_parts]
    if trans_b:
        in_specs.append(pl.BlockSpec((tn, kk), lambda j, i: (j, 0)))
    else:
        in_specs.append(pl.BlockSpec((kk, tn), lambda j, i: (0, j)))
    return pl.pallas_call(
        body, name=name, grid=(n // tn, m // tm), in_specs=in_specs,
        out_specs=pl.BlockSpec((tm, tn), lambda j, i: (i, j)),
        out_shape=jax.ShapeDtypeStruct((m, n), out_dtype),
        compiler_params=_cparams(("parallel", "parallel"), vmem),
    )(*[a for a, _, _ in a_parts], b)


def _host_parts(exchange):
    if exchange is None:
        return [], [], [], [], []
    any_space = pl.BlockSpec(memory_space=pl.ANY)
    return (exchange.ins, [any_space] * len(exchange.ins), [any_space] * len(exchange.out_shapes),
            list(exchange.out_shapes), list(exchange.sems))


def _host_run(exchange, refs, first, last):
    if exchange is None:
        return
    n_in, n_out = len(exchange.ins), len(exchange.out_shapes)
    ins, outs, sems = refs[:n_in], refs[n_in:n_in + n_out], refs[n_in + n_out:]

    @pl.when(first)
    def _():
        exchange.start(ins, outs, sems)

    @pl.when(last)
    def _():
        exchange.finish(ins, outs, sems)


def _mm_tn(a, a_cb, ka, b, *, name, tk=1024, tn=1024, tmr=2 * TM, vmem=None, also_bf16=False, exchange=None):
    m = a.shape[0]
    n = b.shape[1]
    tk = min(tk, ka)
    tn = min(tn, n)
    assert ka % tk == 0 and n % tn == 0 and m % tmr == 0
    kb = ka // tk
    grid = (kb, n // tn, m // tmr)
    ex_ops, ex_in_specs, ex_out_specs, ex_out_shapes, ex_scratch = _host_parts(exchange)
    n_own_out = 2 if also_bf16 else 1

    def body(*refs):
        a_ref, b_ref = refs[:2]
        hosted_in = refs[2:2 + len(ex_ops)]
        outs = refs[2 + len(ex_ops):]
        o_ref = outs[0]
        k, j, r = pl.program_id(0), pl.program_id(1), pl.program_id(2)
        _host_run(exchange, list(hosted_in) + list(outs[n_own_out:]),
                  (k == 0) & (j == 0) & (r == 0), (k == grid[0] - 1) & (j == grid[1] - 1) & (r == grid[2] - 1))

        @pl.when(r == 0)
        def _():
            o_ref[...] = jnp.zeros_like(o_ref)

        o_ref[...] += lax.dot_general(a_ref[...].astype(BF16), b_ref[...].astype(BF16), (((0,), (0,)), ((), ())),
                                      preferred_element_type=F32)
        if also_bf16:
            @pl.when(r == grid[2] - 1)
            def _():
                outs[1][...] = o_ref[...].astype(BF16)

    tile = pl.BlockSpec((tk, tn), lambda k, j, r: (k, j))
    res = pl.pallas_call(
        body, name=name, grid=grid,
        in_specs=[pl.BlockSpec((tmr, tk), lambda k, j, r: (r, a_cb * kb + k)),
                  pl.BlockSpec((tmr, tn), lambda k, j, r: (r, j))] + ex_in_specs,
        out_specs=[tile] * n_own_out + ex_out_specs,
        out_shape=[jax.ShapeDtypeStruct((ka, n), F32)] + ([jax.ShapeDtypeStruct((ka, n), BF16)] if also_bf16 else [])
        + ex_out_shapes,
        scratch_shapes=ex_scratch,
        compiler_params=_cparams(("arbitrary",) * 3 if exchange is not None else ("parallel", "parallel", "arbitrary"), vmem),
    )(a, b, *ex_ops)
    if exchange is None:
        return tuple(res) if also_bf16 else res[0]
    own = tuple(res[:n_own_out]) if also_bf16 else res[0]
    return own, list(res[n_own_out:])


def _mm_tn_slabs(a, ka, b, nbw, nslab, *, name, tmr=2 * TM):
    m = a.shape[0]
    assert m % tmr == 0

    def body(a_ref, b_ref, o_ref):
        @pl.when(pl.program_id(1) == 0)
        def _():
            o_ref[...] = jnp.zeros_like(o_ref)

        o_ref[0] += lax.dot_general(a_ref[...].astype(BF16), b_ref[...].astype(BF16), (((0,), (0,)), ((), ())),
                                    preferred_element_type=F32)

    return pl.pallas_call(
        body, name=name, grid=(nslab, m // tmr),
        in_specs=[pl.BlockSpec((tmr, ka), lambda s, r: (r, s)), pl.BlockSpec((tmr, nbw), lambda s, r: (r, s))],
        out_specs=pl.BlockSpec((1, ka, nbw), lambda s, r: (s, 0, 0)),
        out_shape=jax.ShapeDtypeStruct((nslab, ka, nbw), F32),
        compiler_params=_cparams(("parallel", "arbitrary")),
    )(a, b)


def _ln_stats(r):
    mu = jnp.mean(r, axis=-1, keepdims=True)
    xc = r - mu
    var = jnp.mean(xc * xc, axis=-1, keepdims=True)
    rstd = lax.rsqrt(var + LN_EPS)
    return xc * rstd, rstd


def _ln_apply(r, g, b):
    xhat, _ = _ln_stats(r)
    return xhat * g + b


def _ln_grad(r, dy, g):
    xhat, rstd = _ln_stats(r)
    dxh = dy * g
    m1 = jnp.mean(dxh, axis=-1, keepdims=True)
    m2 = jnp.mean(dxh * xhat, axis=-1, keepdims=True)
    return (rstd * (dxh - m1 - xhat * m2), jnp.sum(dy * xhat, axis=0, keepdims=True), jnp.sum(dy, axis=0, keepdims=True))


def _mm_rows(matmuls, rows_in, vecs_in, out_rows, acc_widths, fn, *, name, tm=TM, vmem=None):
    m = rows_in[0].shape[0]
    assert m % tm == 0
    flat, in_specs, layout = [], [], []
    for a_parts, b, trans_b in matmuls:
        own = [(arr, cb, w) for arr, cb, w in a_parts if arr is not None]
        for arr, cb, w in own:
            flat.append(arr)
            in_specs.append(pl.BlockSpec((tm, w), functools.partial(lambda i, cb: (i, cb), cb=cb)))
        flat.append(b)
        in_specs.append(pl.BlockSpec(b.shape, lambda i: (0, 0)))
        layout.append(([(arr is None, cb, w) for arr, cb, w in a_parts], len(own), trans_b))
    first_row = len(flat)
    for r in rows_in:
        flat.append(r)
        in_specs.append(pl.BlockSpec((tm, r.shape[1]), lambda i: (i, 0)))
    for v in vecs_in:
        flat.append(v)
        in_specs.append(pl.BlockSpec(v.shape, lambda i: (0, 0)))
    n_in = len(flat)
    n_rows_out = len(out_rows)

    def body(*refs):
        rows = [r[...] for r in refs[first_row:first_row + len(rows_in)]]
        vecs = [v[...] for v in refs[first_row + len(rows_in):n_in]]
        pos = 0
        products = []
        for parts, n_own, trans_b in layout:
            b_ref = refs[pos + n_own]
            own_refs = iter(refs[pos:pos + n_own])
            acc, k0 = None, 0
            for is_row, cb, w in parts:
                a = (rows[cb] if is_row else next(own_refs)[...]).astype(BF16)
                if trans_b:
                    part = lax.dot_general(a, b_ref[:, k0:k0 + w], (((1,), (1,)), ((), ())), preferred_element_type=F32)
                else:
                    part = jnp.dot(a, b_ref[k0:k0 + w, :], preferred_element_type=F32)
                acc = part if acc is None else acc + part
                k0 += w
            products.append(acc)
            pos += n_own + 1
        outs, sums = fn(products, rows, vecs)
        for o_ref, o in zip(refs[n_in:n_in + n_rows_out], outs):
            o_ref[...] = o.astype(o_ref.dtype)
        if acc_widths:
            acc_refs = refs[n_in + n_rows_out:]

            @pl.when(pl.program_id(0) == 0)
            def _():
                for a_ref in acc_refs:
                    a_ref[...] = jnp.zeros_like(a_ref)

            for a_ref, s_ in zip(acc_refs, sums):
                a_ref[...] += s_

    out_specs = [pl.BlockSpec((tm, n), lambda i: (i, 0)) for n, _ in out_rows]
    out_specs += [pl.BlockSpec((1, wd), lambda i: (0, 0)) for wd in acc_widths]
    out_shape = [jax.ShapeDtypeStruct((m, n), dt) for n, dt in out_rows]
    out_shape += [jax.ShapeDtypeStruct((1, wd), F32) for wd in acc_widths]
    return pl.pallas_call(
        body, name=name, grid=(m // tm,), in_specs=in_specs, out_specs=out_specs, out_shape=out_shape,
        compiler_params=_cparams(("arbitrary",) if acc_widths else ("parallel",), vmem),
    )(*flat)


def _ffn_up(x1, wup, *, name, exchange=None):
    m = x1.shape[0]
    tn = D_FF // 2
    grid = (2, m // TM)
    ex_ops, ex_in_specs, ex_out_specs, ex_out_shapes, ex_scratch = _host_parts(exchange)

    def body(*refs):
        x_ref, wg_ref, wu_ref = refs[:3]
        hosted_in = refs[3:3 + len(ex_ops)]
        a_ref, g_ref, u_ref = refs[3 + len(ex_ops):6 + len(ex_ops)]
        j, i = pl.program_id(0), pl.program_id(1)
        _host_run(exchange, list(hosted_in) + list(refs[6 + len(ex_ops):]),
                  (j == 0) & (i == 0), (j == grid[0] - 1) & (i == grid[1] - 1))
        x = x_ref[...].astype(BF16)
        g = jnp.dot(x, wg_ref[...], preferred_element_type=F32)
        u = jnp.dot(x, wu_ref[...], preferred_element_type=F32)
        sg = _sigmoid(g)
        silu = g * sg
        a_ref[...] = (silu * u).astype(BF16)
        g_ref[...] = (u * (sg + silu * (1.0 - sg))).astype(BF16)
        u_ref[...] = silu.astype(BF16)

    out = pl.BlockSpec((TM, tn), lambda j, i: (i, j))
    res = pl.pallas_call(
        body, name=name, grid=grid,
        in_specs=[pl.BlockSpec((TM, D_MODEL), lambda j, i: (i, 0)),
                  pl.BlockSpec((D_MODEL, tn), lambda j, i: (0, j)),
                  pl.BlockSpec((D_MODEL, tn), lambda j, i: (0, j + 2))] + ex_in_specs,
        out_specs=[out, out, out] + ex_out_specs,
        out_shape=[jax.ShapeDtypeStruct((m, D_FF), BF16)] * 3 + ex_out_shapes,
        scratch_shapes=ex_scratch,
        compiler_params=_cparams(("arbitrary", "arbitrary") if exchange is not None else ("parallel", "parallel")),
    )(x1, wup, wup, *ex_ops)
    return (res[0], res[1], res[2]) if exchange is None else ((res[0], res[1], res[2]), list(res[3:]))


def _ffn_down_bwd(df, wdown, g, u, *, name):
    m = df.shape[0]
    tm = TM
    chunk = 256

    def body(df_ref, w_ref, g_ref, u_ref, o_ref):
        df = df_ref[...].astype(BF16)
        for part in range(D_FF // chunk):
            cols = slice(chunk * part, chunk * (part + 1))
            da = lax.dot_general(df, w_ref[cols, :], (((1,), (1,)), ((), ())), preferred_element_type=F32)
            o_ref[:, cols] = (da * g_ref[:, cols].astype(F32)).astype(BF16)
            o_ref[:, D_FF + chunk * part:D_FF + chunk * (part + 1)] = (da * u_ref[:, cols].astype(F32)).astype(BF16)

    return pl.pallas_call(
        body, name=name, grid=(m // tm,),
        in_specs=[pl.BlockSpec((tm, D_MODEL), lambda i: (i, 0)), pl.BlockSpec((D_FF, D_MODEL), lambda i: (0, 0)),
                  pl.BlockSpec((tm, D_FF), lambda i: (i, 0)), pl.BlockSpec((tm, D_FF), lambda i: (i, 0))],
        out_specs=pl.BlockSpec((tm, 2 * D_FF), lambda i: (i, 0)),
        out_shape=jax.ShapeDtypeStruct((m, 2 * D_FF), BF16),
        compiler_params=_cparams(("parallel",), VMEM_BIG),
    )(df, wdown, g, u)


def _ple_ln_bwd(dx3, gate, pp, r2, wgate, g2, *, name):
    m, n = dx3.shape

    def body(dx3_ref, gate_ref, pp_ref, r_ref, w_ref, g_ref, dz_ref, dpp_ref, dr_ref, dbg_ref, dg_ref, dbias_ref):
        dx3 = dx3_ref[...]

        @pl.when(pl.program_id(0) == 0)
        def _():
            dbg_ref[...] = jnp.zeros_like(dbg_ref)
            dg_ref[...] = jnp.zeros_like(dg_ref)
            dbias_ref[...] = jnp.zeros_like(dbias_ref)

        gate = gate_ref[...].astype(F32)
        dz = dx3 * pp_ref[...].astype(F32) * gate * (1.0 - gate)
        dzq = dz.astype(BF16)
        dz_ref[...] = dzq
        dpp_ref[...] = (dx3 * gate).astype(BF16)
        dbg_ref[...] += jnp.sum(dz, axis=0, keepdims=True)
        dx2 = dx3 + lax.dot_general(dzq, w_ref[...], (((1,), (1,)), ((), ())), preferred_element_type=F32)
        dr, dg, dbias = _ln_grad(r_ref[...], dx2, g_ref[...])
        dr_ref[...] = dr
        dg_ref[...] += dg
        dbias_ref[...] += dbias

    row = pl.BlockSpec((TM, n), lambda i: (i, 0))
    vec = pl.BlockSpec((1, n), lambda i: (0, 0))
    in_specs = [row] * 4 + [pl.BlockSpec(wgate.shape, lambda i: (0, 0)), vec]
    return pl.pallas_call(
        body, name=name, grid=(m // TM,), in_specs=in_specs, out_specs=[row, row, row, vec, vec, vec],
        out_shape=[jax.ShapeDtypeStruct((m, n), BF16), jax.ShapeDtypeStruct((m, n), BF16), jax.ShapeDtypeStruct((m, n), F32)]
        + [jax.ShapeDtypeStruct((1, n), F32)] * 3,
        compiler_params=_cparams(("arbitrary",)),
    )(dx3, gate, pp, r2, wgate, g2)


def _loss_head(y, target, *, name):
    m, n = y.shape

    def body(y_ref, t_ref, loss_ref, dy_ref):
        @pl.when(pl.program_id(0) == 0)
        def _():
            loss_ref[...] = jnp.zeros_like(loss_ref)

        err = y_ref[...] - t_ref[...]
        dy_ref[...] = err * (1.0 / n)
        per_tok = jnp.mean(err * err, axis=-1, keepdims=True)
        loss_ref[...] += 0.5 * jnp.sum(per_tok, axis=0, keepdims=True)

    row = pl.BlockSpec((TM, n), lambda i: (i, 0))
    return pl.pallas_call(
        body, name=name, grid=(m // TM,), in_specs=[row, row],
        out_specs=[pl.BlockSpec((1, 1), lambda i: (0, 0)), row],
        out_shape=[jax.ShapeDtypeStruct((1, 1), F32), jax.ShapeDtypeStruct((m, n), F32)],
        compiler_params=_cparams(("arbitrary",)),
    )(y, target)


def _gelu(y):
    c = math.sqrt(2.0 / math.pi)
    return 0.5 * y * (1.0 + jnp.tanh(c * (y + 0.044715 * y * y * y)))


def _gelu_grad(y):
    c = math.sqrt(2.0 / math.pi)
    t = jnp.tanh(c * (y + 0.044715 * y * y * y))
    return 0.5 * (1.0 + t) + 0.5 * y * (1.0 - t * t) * c * (1.0 + 3.0 * 0.044715 * y * y)


STRIP = 128
S5_SLABS = 4
SLAB_COLS = 2 * S5_N // S5_SLABS
N_TILES = 2 * S5_N // STRIP
SLAB_TILES = SLAB_COLS // STRIP


def _strip_tiles(j):
    re_tile = (j // (SLAB_TILES // 2)) * SLAB_TILES + j % (SLAB_TILES // 2)
    return pl.multiple_of(j * STRIP, STRIP), re_tile, re_tile + SLAB_TILES // 2


def _store_tiles(ref, first_tile, value):
    for k in range(value.shape[1] // STRIP):
        ref[first_tile + k] = value[:, STRIP * k:STRIP * (k + 1)]


def _load_tiles(ref, first_tile, count):
    return jnp.concatenate([ref[first_tile + k] for k in range(count)], axis=1)


GROUPS = T_S5 // 8
N_SQUARES = GROUPS.bit_length() - 2
PTAB_ROWS = 32 + 2 * N_SQUARES


def _scan_cols(ref, hr, hi, ptab_ref, off, down, visit=None):
    sign = 1.0 if down else -1.0
    ref_r, ref_i = ref
    cols_p = pl.ds(off, STRIP)

    def power(row, im_offset=8):
        return ptab_ref[row:row + 1, cols_p], sign * ptab_ref[row + im_offset:row + im_offset + 1, cols_p]

    def rows(r):
        return pl.ds(r, GROUPS, stride=8)

    def mul_add(br, bi, qr, qi, vr, vi):
        return br + qr * vr - qi * vi, bi + qr * vi + qi * vr

    order = list(range(8)) if down else list(range(7, -1, -1))
    lam_r, lam_i = power(0)
    vr, vi = ref_r[rows(order[0]), :], ref_i[rows(order[0]), :]
    for r in order[1:]:
        vr, vi = mul_add(ref_r[rows(r), :], ref_i[rows(r), :], lam_r, lam_i, vr, vi)
        ref_r[rows(r), :] = vr
        ref_i[rows(r), :] = vi
    grow = lax.broadcasted_iota(jnp.int32, (GROUPS, STRIP), 0)
    edge = 0 if down else GROUPS - 1
    l8r, l8i = power(7)
    er = vr + jnp.where(grow == edge, l8r * hr - l8i * hi, 0.0)
    ei = vi + jnp.where(grow == edge, l8r * hi + l8i * hr, 0.0)
    k, step = 0, 1
    while step < GROUPS:
        qr, qi = (l8r, l8i) if k == 0 else power(32 + k - 1, N_SQUARES)
        if down:
            sr = jnp.where(grow >= step, pltpu.roll(er, step, 0), 0.0)
            si = jnp.where(grow >= step, pltpu.roll(ei, step, 0), 0.0)
        else:
            sr = jnp.where(grow < GROUPS - step, pltpu.roll(er, GROUPS - step, 0), 0.0)
            si = jnp.where(grow < GROUPS - step, pltpu.roll(ei, GROUPS - step, 0), 0.0)
        er, ei = mul_add(er, ei, qr, qi, sr, si)
        k, step = k + 1, 2 * step
    if down:
        cr = jnp.where(grow == 0, hr, pltpu.roll(er, 1, 0))
        ci = jnp.where(grow == 0, hi, pltpu.roll(ei, 1, 0))
    else:
        cr = jnp.where(grow == GROUPS - 1, hr, pltpu.roll(er, GROUPS - 1, 0))
        ci = jnp.where(grow == GROUPS - 1, hi, pltpu.roll(ei, GROUPS - 1, 0))
    for r in range(8):
        qr, qi = power(r if down else 16 + r)
        xr, xi = mul_add(ref_r[rows(r), :], ref_i[rows(r), :], qr, qi, cr, ci)
        ref_r[rows(r), :] = xr
        ref_i[rows(r), :] = xi
        if visit is not None:
            visit(r, xr, xi)
    last = GROUPS - 1 if down else 0
    return er[last:last + 1], ei[last:last + 1]


def _s5_fwd(h, bmat, cmat, dvec, wglu, bglu, ptab, *, name):
    m = h.shape[0]
    t = T_S5
    nb = m // t

    def body(u_ref, bmat_ref, cmat_ref, d_ref, wglu_ref, bglu_ref, ptab_ref,
             out_ref, y_ref, hb_ref, bu_ref, carry_ref):
        @pl.when(pl.program_id(0) == 0)
        def _():
            carry_ref[...] = jnp.zeros_like(carry_ref)

        hb_ref[0] = carry_ref[...]
        u = u_ref[...]
        ub = u.astype(BF16)
        for s in range(S5_SLABS):
            _store_tiles(bu_ref, SLAB_TILES * s,
                         jnp.dot(ub[:, 128 * s:128 * (s + 1)], bmat_ref[s], preferred_element_type=F32))

        def strip(j, c):
            off, tr, ti = _strip_tiles(j)
            cols_r, cols_i = pl.ds(pl.multiple_of(tr * STRIP, STRIP), STRIP), pl.ds(pl.multiple_of(ti * STRIP, STRIP), STRIP)
            er, ei = _scan_cols((bu_ref.at[tr], bu_ref.at[ti]), carry_ref[0:1, cols_r], carry_ref[0:1, cols_i],
                                ptab_ref, off, True)
            carry_ref[0:1, cols_r] = er
            carry_ref[0:1, cols_i] = ei
            return c

        lax.fori_loop(0, S5_N // STRIP, strip, 0)
        y = jnp.concatenate(
            [jnp.dot(_load_tiles(bu_ref, SLAB_TILES * s, SLAB_TILES).astype(BF16), cmat_ref[s], preferred_element_type=F32)
             for s in range(S5_SLABS)], axis=1) + d_ref[...] * u
        y_ref[...] = y
        g = _gelu(y)
        zz = jnp.dot(g.astype(BF16), wglu_ref[...], preferred_element_type=F32) + bglu_ref[...]
        out_ref[...] = (g * _sigmoid(zz)).astype(BF16)

    const = lambda shape: pl.BlockSpec(shape, lambda i: (0,) * len(shape))
    row_spec = pl.BlockSpec((t, MIX), lambda i: (i, 0))
    return pl.pallas_call(
        body, name=name, grid=(nb,),
        in_specs=[row_spec, const((S5_SLABS, 128, SLAB_COLS)), const((S5_SLABS, SLAB_COLS, 128)), const((1, MIX)),
                  const((MIX, MIX)), const((1, MIX)), const((PTAB_ROWS, S5_N))],
        out_specs=[row_spec, row_spec, pl.BlockSpec((1, 1, 2 * S5_N), lambda i: (i, 0, 0))],
        out_shape=[jax.ShapeDtypeStruct((m, MIX), BF16), jax.ShapeDtypeStruct((m, MIX), F32),
                   jax.ShapeDtypeStruct((nb, 1, 2 * S5_N), F32)],
        scratch_shapes=[pltpu.VMEM((N_TILES, t, STRIP), F32), pltpu.VMEM((1, 2 * S5_N), F32)],
        compiler_params=_cparams(("arbitrary",), VMEM_BIG),
    )(h, bmat, cmat, dvec, wglu, bglu, ptab)


def _s5_bwd(dcat, ypre, h, hb, bmat, cmat, dvec, wglu, bglu, ptab, *, name, exchange=None):
    m = h.shape[0]
    t = T_S5
    nb = m // t
    ex_ops, ex_in_specs, ex_out_specs, ex_out_shapes, ex_scratch = _host_parts(exchange)
    n_ex = len(ex_ops)

    def body(*refs):
        dya_ref, y_ref, u_ref, hb_ref, bmat_ref, cmat_ref, d_ref, wglu_ref, bglu_ref, ptab_ref = refs[:10]
        du_ref, xb_ref, gb_ref, gq_ref, dzz_ref, dyq_ref, dlam_ref, dbglu_ref, dd_ref = refs[10 + n_ex:19 + n_ex]
        hosted_out = refs[19 + n_ex:19 + n_ex + len(ex_out_shapes)]
        bu_ref, dx_ref, gcarry_ref = refs[19 + n_ex + len(ex_out_shapes):22 + n_ex + len(ex_out_shapes)]
        _host_run(exchange, list(refs[10:10 + n_ex]) + list(hosted_out) + list(refs[22 + n_ex + len(ex_out_shapes):]),
                  pl.program_id(0) == 0, pl.program_id(0) == nb - 1)

        @pl.when(pl.program_id(0) == 0)
        def _():
            gcarry_ref[...] = jnp.zeros_like(gcarry_ref)
            dlam_ref[...] = jnp.zeros_like(dlam_ref)
            dbglu_ref[...] = jnp.zeros_like(dbglu_ref)
            dd_ref[...] = jnp.zeros_like(dd_ref)

        u = u_ref[...]
        y = y_ref[...]
        g = _gelu(y)
        gq = g.astype(BF16)
        sg = _sigmoid(jnp.dot(gq, wglu_ref[...], preferred_element_type=F32) + bglu_ref[...])
        dout = dya_ref[...]
        dzz = dout * g * sg * (1.0 - sg)
        dzzq = dzz.astype(BF16)
        dg = dout * sg + lax.dot_general(dzzq, wglu_ref[...], (((1,), (1,)), ((), ())), preferred_element_type=F32)
        dy = dg * _gelu_grad(y)
        dyq = dy.astype(BF16)
        gq_ref[...] = gq
        dzz_ref[...] = dzzq
        dyq_ref[...] = dyq
        dbglu_ref[...] += jnp.sum(dzz, axis=0, keepdims=True)
        dd_ref[...] += jnp.sum(dy * u, axis=0, keepdims=True)

        ub = u.astype(BF16)
        nt = (((1,), (1,)), ((), ()))
        for s in range(S5_SLABS):
            _store_tiles(dx_ref, SLAB_TILES * s,
                         lax.dot_general(dyq[:, 128 * s:128 * (s + 1)], cmat_ref[s], nt, preferred_element_type=F32))
            _store_tiles(bu_ref, SLAB_TILES * s,
                         jnp.dot(ub[:, 128 * s:128 * (s + 1)], bmat_ref[s], preferred_element_type=F32))
        grow = lax.broadcasted_iota(jnp.int32, (GROUPS, STRIP), 0)

        def strip(j, c):
            off, tr, ti = _strip_tiles(j)
            cols_r, cols_i = pl.ds(pl.multiple_of(tr * STRIP, STRIP), STRIP), pl.ds(pl.multiple_of(ti * STRIP, STRIP), STRIP)
            x_r, x_i = bu_ref.at[tr], bu_ref.at[ti]
            hr = hb_ref[0, 0:1, cols_r]
            hi = hb_ref[0, 0:1, cols_i]
            _scan_cols((x_r, x_i), hr, hi, ptab_ref, off, True)
            xb_ref[:, cols_r] = x_r[...].astype(BF16)
            xb_ref[:, cols_i] = x_i[...].astype(BF16)
            sums = [jnp.zeros((1, STRIP), F32), jnp.zeros((1, STRIP), F32)]

            def d_lam(r, gr, gi):
                if r == 0:
                    pr_ = jnp.where(grow == 0, hr, pltpu.roll(x_r[pl.ds(7, GROUPS, stride=8), :], 1, 0))
                    pi_ = jnp.where(grow == 0, hi, pltpu.roll(x_i[pl.ds(7, GROUPS, stride=8), :], 1, 0))
                else:
                    pr_ = x_r[pl.ds(r - 1, GROUPS, stride=8), :]
                    pi_ = x_i[pl.ds(r - 1, GROUPS, stride=8), :]
                sums[0] = sums[0] + jnp.sum(pr_ * gr + pi_ * gi, axis=0, keepdims=True)
                sums[1] = sums[1] + jnp.sum(pr_ * gi - pi_ * gr, axis=0, keepdims=True)

            g_r, g_i = dx_ref.at[tr], dx_ref.at[ti]
            gr0, gi0 = _scan_cols((g_r, g_i), gcarry_ref[0:1, cols_r], gcarry_ref[0:1, cols_i], ptab_ref, off, False, d_lam)
            gb_ref[:, cols_r] = g_r[...].astype(BF16)
            gb_ref[:, cols_i] = g_i[...].astype(BF16)
            gcarry_ref[0:1, cols_r] = gr0
            gcarry_ref[0:1, cols_i] = gi0
            dlam_ref[0:1, pl.ds(off, STRIP)] += sums[0]
            dlam_ref[1:2, pl.ds(off, STRIP)] += sums[1]
            return c

        lax.fori_loop(0, S5_N // STRIP, strip, 0)
        du_ref[...] = dy * d_ref[...] + jnp.concatenate(
            [lax.dot_general(gb_ref[:, SLAB_COLS * s:SLAB_COLS * (s + 1)], bmat_ref[s], nt, preferred_element_type=F32)
             for s in range(S5_SLABS)], axis=1)

    const = lambda shape: pl.BlockSpec(shape, lambda i: (0,) * len(shape))
    rev = lambda i: (nb - 1 - i, 0)
    row_spec = pl.BlockSpec((t, MIX), rev)
    wide = pl.BlockSpec((t, 2 * S5_N), rev)
    res = pl.pallas_call(
        body, name=name, grid=(nb,),
        in_specs=[row_spec, row_spec, row_spec, pl.BlockSpec((1, 1, 2 * S5_N), lambda i: (nb - 1 - i, 0, 0)),
                  const((S5_SLABS, 128, SLAB_COLS)), const((S5_SLABS, SLAB_COLS, 128)), const((1, MIX)), const((MIX, MIX)),
                  const((1, MIX)), const((PTAB_ROWS, S5_N))] + ex_in_specs,
        out_specs=[row_spec, wide, wide, row_spec, row_spec, row_spec, const((2, S5_N)), const((1, MIX)), const((1, MIX))]
        + ex_out_specs,
        out_shape=[jax.ShapeDtypeStruct((m, MIX), F32), jax.ShapeDtypeStruct((m, 2 * S5_N), BF16),
                   jax.ShapeDtypeStruct((m, 2 * S5_N), BF16), jax.ShapeDtypeStruct((m, MIX), BF16),
                   jax.ShapeDtypeStruct((m, MIX), BF16), jax.ShapeDtypeStruct((m, MIX), BF16),
                   jax.ShapeDtypeStruct((2, S5_N), F32), jax.ShapeDtypeStruct((1, MIX), F32), jax.ShapeDtypeStruct((1, MIX), F32)]
        + ex_out_shapes,
        scratch_shapes=[pltpu.VMEM((N_TILES, t, STRIP), F32), pltpu.VMEM((N_TILES, t, STRIP), F32),
                        pltpu.VMEM((1, 2 * S5_N), F32)] + ex_scratch,
        compiler_params=_cparams(("arbitrary",), VMEM_BIG),
    )(dcat, ypre, h, hb, bmat, cmat, dvec, wglu, bglu, ptab, *ex_ops)
    return res[:9] if exchange is None else (res[:9], list(res[9:]))


HALO = 8


def _taps_down(zext, t):
    return pltpu.roll(zext, 1, 0)[HALO:HALO + t], pltpu.roll(zext, 2, 0)[HALO:HALO + t]


def _conv_z(c_ref, x_ref, cp_ref, xp_ref, first, t):
    z = c_ref[...] * x_ref[...]
    zp = jnp.where(first, 0.0, cp_ref[t - HALO:t, :] * xp_ref[t - HALO:t, :])
    z1, z2 = _taps_down(jnp.concatenate([zp, z], axis=0), t)
    return z, z1, z2


def _conv_fwd(h, cw, *, name):
    m = h.shape[0]
    t = TM
    nb = m // t

    def body(b_ref, c_ref, x_ref, cp_ref, xp_ref, w_ref, o_ref):
        z, z1, z2 = _conv_z(c_ref, x_ref, cp_ref, xp_ref, pl.program_id(0) == 0, t)
        o_ref[...] = (b_ref[...] * (w_ref[0:1, :] * z2 + w_ref[1:2, :] * z1 + w_ref[2:3, :] * z)).astype(BF16)

    cur = lambda cb: pl.BlockSpec((t, MIX), lambda i: (i, cb))
    prev = lambda cb: pl.BlockSpec((t, MIX), lambda i: (jnp.maximum(i - 1, 0), cb))
    return pl.pallas_call(
        body, name=name, grid=(nb,),
        in_specs=[cur(1), cur(2), cur(3), prev(2), prev(3), pl.BlockSpec((3, MIX), lambda i: (0, 0))],
        out_specs=pl.BlockSpec((t, MIX), lambda i: (i, 0)),
        out_shape=jax.ShapeDtypeStruct((m, MIX), BF16),
        compiler_params=_cparams(("parallel",)),
    )(h, h, h, h, h, cw)


def _conv_bwd(dcat, h, cw, *, name):
    m = h.shape[0]
    t = TM
    nb = m // t

    def body(dy_ref, dyn_ref, b_ref, c_ref, x_ref, cp_ref, xp_ref, bn_ref, w_ref, o_ref, dw_ref):
        i = pl.program_id(0)

        @pl.when(i == 0)
        def _():
            dw_ref[...] = jnp.zeros_like(dw_ref)

        z, z1, z2 = _conv_z(c_ref, x_ref, cp_ref, xp_ref, i == 0, t)
        w0, w1, w2 = w_ref[0:1, :], w_ref[1:2, :], w_ref[2:3, :]
        dy = dy_ref[...]
        dconv = dy * b_ref[...]
        dnext = jnp.where(i == nb - 1, 0.0, dyn_ref[0:HALO, :] * bn_ref[0:HALO, :])
        dext = jnp.concatenate([dconv, dnext], axis=0)
        d1 = pltpu.roll(dext, t + HALO - 1, 0)[0:t]
        d2 = pltpu.roll(dext, t + HALO - 2, 0)[0:t]
        dz = w2 * dconv + w1 * d1 + w0 * d2
        o_ref[:, 0:MIX] = dy * (w0 * z2 + w1 * z1 + w2 * z)
        o_ref[:, MIX:2 * MIX] = dz * x_ref[...]
        o_ref[:, 2 * MIX:3 * MIX] = dz * c_ref[...]
        dw_ref[0:1, :] += jnp.sum(dconv * z2, axis=0, keepdims=True)
        dw_ref[1:2, :] += jnp.sum(dconv * z1, axis=0, keepdims=True)
        dw_ref[2:3, :] += jnp.sum(dconv * z, axis=0, keepdims=True)

    cur = lambda cb: pl.BlockSpec((t, MIX), lambda i: (i, cb))
    prev = lambda cb: pl.BlockSpec((t, MIX), lambda i: (jnp.maximum(i - 1, 0), cb))
    nxt = lambda cb: pl.BlockSpec((t, MIX), lambda i: (jnp.minimum(i + 1, nb - 1), cb))
    return pl.pallas_call(
        body, name=name, grid=(nb,),
        in_specs=[cur(1), nxt(1), cur(1), cur(2), cur(3), prev(2), prev(3), nxt(1), pl.BlockSpec((3, MIX), lambda i: (0, 0))],
        out_specs=[pl.BlockSpec((t, 3 * MIX), lambda i: (i, 0)), pl.BlockSpec((8, MIX), lambda i: (0, 0))],
        out_shape=[jax.ShapeDtypeStruct((m, 3 * MIX), F32), jax.ShapeDtypeStruct((8, MIX), F32)],
        compiler_params=_cparams(("arbitrary",)),
    )(dcat, dcat, h, h, h, h, h, h, cw)


PHALO = 16


def _pool_pooled(z_ref, zp_ref, i, t):
    z = z_ref[...]
    zp = jnp.where(i == 0, 0.0, zp_ref[t - PHALO:t, :])
    s = jnp.concatenate([zp, z], axis=0)
    sums = {}
    width = 1
    while width < PHALO:
        s = s + pltpu.roll(s, width, 0)
        width *= 2
        sums[width] = s[PHALO:PHALO + t]
    tpos = i * t + lax.broadcasted_iota(jnp.int32, (t, 1), 0)
    outs = []
    for gi, w in enumerate(POOL_WINDOWS):
        lo = gi * POOL_GROUP
        count = jnp.minimum(tpos + 1, w).astype(F32)
        outs.append(sums[w][:, lo:lo + POOL_GROUP] / count - z[:, lo:lo + POOL_GROUP])
    return outs


def _pool_fwd(h, pw, ps, *, name):
    m = h.shape[0]
    t = TM
    nb = m // t

    def body(z_ref, zp_ref, pw_ref, ps_ref, o_ref):
        pooled = _pool_pooled(z_ref, zp_ref, pl.program_id(0), t)
        for gi in range(len(POOL_WINDOWS)):
            lo = gi * POOL_GROUP
            mixed = jnp.dot(pooled[gi].astype(BF16), pw_ref[gi], preferred_element_type=F32)
            o_ref[:, lo:lo + POOL_GROUP] = (mixed * ps_ref[:, lo:lo + POOL_GROUP]).astype(BF16)

    return pl.pallas_call(
        body, name=name, grid=(nb,),
        in_specs=[pl.BlockSpec((t, MIX), lambda i: (i, 3)), pl.BlockSpec((t, MIX), lambda i: (jnp.maximum(i - 1, 0), 3)),
                  pl.BlockSpec((4, POOL_GROUP, POOL_GROUP), lambda i: (0, 0, 0)), pl.BlockSpec((1, MIX), lambda i: (0, 0))],
        out_specs=pl.BlockSpec((t, MIX), lambda i: (i, 0)),
        out_shape=jax.ShapeDtypeStruct((m, MIX), BF16),
        compiler_params=_cparams(("parallel",)),
    )(h, h, pw, ps)


def _pool_bwd(dcat, h, pw, ps, *, name):
    m = h.shape[0]
    t = TM
    nb = m // t

    def body(dy_ref, dyn_ref, z_ref, zp_ref, pw_ref, ps_ref, dz_ref, dpw_ref, dps_ref):
        i = pl.program_id(0)

        @pl.when(i == 0)
        def _():
            dpw_ref[...] = jnp.zeros_like(dpw_ref)
            dps_ref[...] = jnp.zeros_like(dps_ref)

        pooled = _pool_pooled(z_ref, zp_ref, i, t)
        dy = dy_ref[...]
        tpos = i * t + lax.broadcasted_iota(jnp.int32, (t, 1), 0)
        for gi, w in enumerate(POOL_WINDOWS):
            lo = gi * POOL_GROUP
            sl = slice(lo, lo + POOL_GROUP)
            pq = pooled[gi].astype(BF16)
            mixed = jnp.dot(pq, pw_ref[gi], preferred_element_type=F32)
            dps_ref[:, sl] += jnp.sum(dy[:, sl] * mixed, axis=0, keepdims=True)
            dmix = (dy[:, sl] * ps_ref[:, sl]).astype(BF16)
            dpw_ref[gi] += lax.dot_general(pq, dmix, (((0,), (0,)), ((), ())), preferred_element_type=F32)
            dpool = lax.dot_general(dmix, pw_ref[gi], (((1,), (1,)), ((), ())), preferred_element_type=F32)
            dmix_n = (dyn_ref[0:PHALO, sl] * ps_ref[:, sl]).astype(BF16)
            dpool_n = lax.dot_general(dmix_n, pw_ref[gi], (((1,), (1,)), ((), ())), preferred_element_type=F32)
            e = dpool / jnp.minimum(tpos + 1, w).astype(F32)
            e_n = jnp.where(i == nb - 1, 0.0, dpool_n * (1.0 / w))
            f = jnp.concatenate([e, e_n], axis=0)
            width = 1
            while width < w:
                f = f + pltpu.roll(f, t + PHALO - width, 0)
                width *= 2
            dz_ref[:, sl] = f[0:t] - dpool

    return pl.pallas_call(
        body, name=name, grid=(nb,),
        in_specs=[pl.BlockSpec((t, MIX), lambda i: (i, 1)), pl.BlockSpec((t, MIX), lambda i: (jnp.minimum(i + 1, nb - 1), 1)),
                  pl.BlockSpec((t, MIX), lambda i: (i, 3)), pl.BlockSpec((t, MIX), lambda i: (jnp.maximum(i - 1, 0), 3)),
                  pl.BlockSpec((4, POOL_GROUP, POOL_GROUP), lambda i: (0, 0, 0)), pl.BlockSpec((1, MIX), lambda i: (0, 0))],
        out_specs=[pl.BlockSpec((t, MIX), lambda i: (i, 0)), pl.BlockSpec((4, POOL_GROUP, POOL_GROUP), lambda i: (0, 0, 0)),
                   pl.BlockSpec((1, MIX), lambda i: (0, 0))],
        out_shape=[jax.ShapeDtypeStruct((m, MIX), F32), jax.ShapeDtypeStruct((4, POOL_GROUP, POOL_GROUP), F32),
                   jax.ShapeDtypeStruct((1, MIX), F32)],
        compiler_params=_cparams(("arbitrary",)),
    )(dcat, dcat, h, h, pw, ps)


NKEY = 2 * T_ATT


def _band_mask():
    qc = np.arange(T_ATT)[:, None] // CHUNK
    kc = np.arange(NKEY)[None, :] // CHUNK - LEFT_CHUNKS
    return np.where((kc <= qc) & (kc >= qc - LEFT_CHUNKS), 0.0, NEG_INF).astype(np.float32)


def _diag_index():
    c = np.arange(NKEY)
    d = np.where(c <= NKEY // 2 + CHUNK, T_ATT - c, T_ATT + NKEY - c)
    return np.clip(d, -MAX_REL, MAX_REL) + MAX_REL


def _bias_tile(vd_ref, mask_ref, tile_ref):
    col = lax.broadcasted_iota(jnp.int32, (8, NKEY), 1)
    no_prev = jnp.where(col < T_ATT, NEG_INF, 0.0)
    for hh in range(2):
        v = vd_ref[0, hh:hh + 1, :]
        base = jnp.concatenate([v if s == 0 else pltpu.roll(v, s, 1) for s in range(8)], axis=0)
        for mrow in range(T_ATT // 8):
            rows = slice(8 * mrow, 8 * mrow + 8)
            blk = (base if mrow == 0 else pltpu.roll(base, 8 * mrow, 1)) + mask_ref[rows, :]
            tile_ref[hh, rows, :] = blk
            tile_ref[2 + hh, rows, :] = blk + no_prev


BAND_ROWS = 2 * CHUNK
BAND_COLS = (LEFT_CHUNKS + 2) * CHUNK
N_BANDS = T_ATT // BAND_ROWS


def _band(x, r):
    return x[BAND_ROWS * r:BAND_ROWS * (r + 1), BAND_ROWS * r:BAND_ROWS * r + BAND_COLS]


def _from_bands(parts):
    rows = []
    for r, part in enumerate(parts):
        right = NKEY - BAND_COLS - BAND_ROWS * r
        pieces = ([jnp.zeros((BAND_ROWS, BAND_ROWS * r), part.dtype)] if r else []) + [part]
        pieces += [jnp.zeros((BAND_ROWS, right), part.dtype)] if right else []
        rows.append(jnp.concatenate(pieces, axis=1))
    return jnp.concatenate(rows, axis=0)


def _attn_probs(q, kc, tile_ref, idx):
    s = lax.dot_general(q, kc, (((1,), (1,)), ((), ())), preferred_element_type=F32)
    parts = []
    for r in range(N_BANDS):
        sb = _band(s, r) + tile_ref[idx, BAND_ROWS * r:BAND_ROWS * (r + 1), BAND_ROWS * r:BAND_ROWS * r + BAND_COLS]
        p = jnp.exp(sb - jnp.max(sb, axis=-1, keepdims=True))
        parts.append(p * (1.0 / jnp.sum(p, axis=-1, keepdims=True)))
    return parts


def _attn_specs(block):
    cur = lambda base: pl.BlockSpec((T_ATT, 128), lambda hp, i: (block(i), base + hp))
    prev = lambda base: pl.BlockSpec((T_ATT, 128), lambda hp, i: (jnp.maximum(block(i) - 1, 0), base + hp))
    return [cur(0), cur(4), prev(4), cur(8), prev(8),
            pl.BlockSpec((1, 2, NKEY), lambda hp, i: (hp, 0, 0)), pl.BlockSpec((T_ATT, NKEY), lambda hp, i: (0, 0))]


def _attn_fwd(h, vdiag, mask, *, name):
    m = h.shape[0]
    nb = m // T_ATT

    def body(q_ref, k_ref, kp_ref, v_ref, vp_ref, vd_ref, mask_ref, o_ref, tile_ref):
        i = pl.program_id(1)

        @pl.when(i == 0)
        def _():
            _bias_tile(vd_ref, mask_ref, tile_ref)

        first = jnp.where(i == 0, 2, 0)
        outs = []
        for hh in range(2):
            sl = slice(hh * HEAD_DIM, (hh + 1) * HEAD_DIM)
            q = (q_ref[:, sl] * (HEAD_DIM ** -0.5)).astype(BF16)
            kc = jnp.concatenate([kp_ref[:, sl], k_ref[:, sl]], axis=0).astype(BF16)
            vc = jnp.concatenate([vp_ref[:, sl], v_ref[:, sl]], axis=0).astype(BF16)
            p = _from_bands([b.astype(BF16) for b in _attn_probs(q, kc, tile_ref, first + hh)])
            outs.append(jnp.dot(p, vc, preferred_element_type=F32))
        o_ref[...] = jnp.concatenate(outs, axis=1).astype(BF16)

    return pl.pallas_call(
        body, name=name, grid=(ATT_HEADS // 2, nb), in_specs=_attn_specs(lambda i: i),
        out_specs=pl.BlockSpec((T_ATT, 128), lambda hp, i: (i, hp)),
        out_shape=jax.ShapeDtypeStruct((m, MIX), BF16),
        scratch_shapes=[pltpu.VMEM((4, T_ATT, NKEY), F32)],
        compiler_params=_cparams(("parallel", "arbitrary"), VMEM_BIG),
    )(h, h, h, h, h, vdiag, mask)


def _attn_bwd(dcat, h, vdiag, mask, *, name):
    m = h.shape[0]
    nb = m // T_ATT

    def body(do_ref, q_ref, k_ref, kp_ref, v_ref, vp_ref, vd_ref, mask_ref,
             dq_ref, dk_ref, dv_ref, dvd_ref, tile_ref, acc_ref, carry_ref):
        i = pl.program_id(1)

        @pl.when(i == 0)
        def _():
            _bias_tile(vd_ref, mask_ref, tile_ref)
            acc_ref[...] = jnp.zeros_like(acc_ref)

            carry_ref[...] = jnp.zeros_like(carry_ref)

        scale = HEAD_DIM ** -0.5
        first = jnp.where(i == nb - 1, 2, 0)
        dqs, dks, dvs = [], [], []
        for hh in range(2):
            sl = slice(hh * HEAD_DIM, (hh + 1) * HEAD_DIM)
            q = (q_ref[:, sl] * scale).astype(BF16)
            kc = jnp.concatenate([kp_ref[:, sl], k_ref[:, sl]], axis=0).astype(BF16)
            vc = jnp.concatenate([vp_ref[:, sl], v_ref[:, sl]], axis=0).astype(BF16)
            do = do_ref[:, sl].astype(BF16)
            bands = _attn_probs(q, kc, tile_ref, first + hh)
            p = _from_bands([b.astype(BF16) for b in bands])
            dvs.append(lax.dot_general(p, do, (((0,), (0,)), ((), ())), preferred_element_type=F32))
            dp = lax.dot_general(do, vc, (((1,), (1,)), ((), ())), preferred_element_type=F32)
            ds_bands = []
            for r, pb in enumerate(bands):
                dpb = _band(dp, r)
                dsb = pb * (dpb - jnp.sum(dpb * pb, axis=-1, keepdims=True))
                acc_ref[hh, BAND_ROWS * r:BAND_ROWS * (r + 1), BAND_ROWS * r:BAND_ROWS * r + BAND_COLS] += dsb
                ds_bands.append(dsb.astype(BF16))
            dsq = _from_bands(ds_bands)
            dqs.append(jnp.dot(dsq, kc, preferred_element_type=F32) * scale)
            dks.append(lax.dot_general(dsq, q, (((0,), (0,)), ((), ())), preferred_element_type=F32))
        dq_ref[...] = jnp.concatenate(dqs, axis=1)
        dk = jnp.concatenate(dks, axis=1)
        dv = jnp.concatenate(dvs, axis=1)
        dk_ref[...] = dk[T_ATT:] + carry_ref[0]
        dv_ref[...] = dv[T_ATT:] + carry_ref[1]
        carry_ref[0] = dk[:T_ATT]
        carry_ref[1] = dv[:T_ATT]

        @pl.when(i == nb - 1)
        def _():
            for hh in range(2):
                r8 = acc_ref[hh, 0:8, :]
                for mrow in range(1, T_ATT // 8):
                    r8 = r8 + pltpu.roll(acc_ref[hh, 8 * mrow:8 * mrow + 8, :], NKEY - 8 * mrow, 1)
                tot = r8[0:1, :]
                for s in range(1, 8):
                    tot = tot + pltpu.roll(r8[s:s + 1, :], NKEY - s, 1)
                dvd_ref[0, hh:hh + 1, :] = tot

    block = lambda i: nb - 1 - i
    out = pl.BlockSpec((T_ATT, 128), lambda hp, i: (block(i), hp))
    return pl.pallas_call(
        body, name=name, grid=(ATT_HEADS // 2, nb),
        in_specs=[out] + _attn_specs(block),
        out_specs=[out, out, out, pl.BlockSpec((1, 2, NKEY), lambda hp, i: (hp, 0, 0))],
        out_shape=[jax.ShapeDtypeStruct((m, MIX), F32)] * 3 + [jax.ShapeDtypeStruct((ATT_HEADS // 2, 2, NKEY), F32)],
        scratch_shapes=[pltpu.VMEM((4, T_ATT, NKEY), F32), pltpu.VMEM((2, T_ATT, NKEY), F32), pltpu.VMEM((2, T_ATT, 128), F32)],
        compiler_params=_cparams(("parallel", "arbitrary"), VMEM_BIG),
    )(dcat, h, h, h, h, h, vdiag, mask)


def _row_tile(rows):
    for t in (512, 256, 128, 64, 32, 16, 8):
        if rows % t == 0:
            return t
    return rows


def _adamw(w, g, mom, var, *, name):
    rows, cols = w.shape
    t = _row_tile(rows)

    def body(w_ref, g_ref, m_ref, v_ref, d_ref, mo_ref, vo_ref):
        g_ = g_ref[...]
        m_ = ADAM_B1 * m_ref[...] + (1.0 - ADAM_B1) * g_
        v_ = ADAM_B2 * v_ref[...] + (1.0 - ADAM_B2) * (g_ * g_)
        m_hat = m_ / (1.0 - ADAM_B1 ** ADAM_STEP)
        v_hat = v_ / (1.0 - ADAM_B2 ** ADAM_STEP)
        d_ref[...] = -ADAM_LR * (m_hat / (jnp.sqrt(v_hat) + ADAM_EPS) + ADAM_WD * w_ref[...])
        mo_ref[...] = m_
        vo_ref[...] = v_

    spec = pl.BlockSpec((t, cols), lambda i: (i, 0))
    return pl.pallas_call(
        body, name=name, grid=(rows // t,), in_specs=[spec] * 4, out_specs=[spec] * 3,
        out_shape=[jax.ShapeDtypeStruct((rows, cols), F32)] * 3, compiler_params=_cparams(("parallel",)),
    )(w, g, mom, var)


ANY = pl.BlockSpec(memory_space=pl.ANY)


def _place():
    x, y, c = lax.axis_index("x"), lax.axis_index("y"), lax.axis_index("c")
    chips = [(1 - x, y), (x, 1 - y), (1 - x, 1 - y)]
    return x, y, c, chips


class _GatherExchange:
    def __init__(self, ws):
        n = len(ws)
        self.ins = list(ws)
        self.out_shapes = [jax.ShapeDtypeStruct((N_CHIPS,) + w.shape, w.dtype) for w in ws]
        self.sems = [pltpu.SemaphoreType.DMA((6 * n,)), pltpu.SemaphoreType.DMA((6 * n,))]

    def _copies(self, ins, outs, sems, onward=True):
        send_sems, recv_sems = sems
        x, y, c, chips = _place()
        me = 2 * x + y

        def region(k, j, chip_index, rows, to):
            ref = outs[k].at[chip_index, rows]
            return pltpu.make_async_remote_copy(
                src_ref=ref, dst_ref=ref, send_sem=send_sems.at[6 * k + j], recv_sem=recv_sems.at[6 * k + j],
                device_id=to, device_id_type=MESH)

        first, landed, passed, handed = [], [], [], []
        for k in range(len(ins)):
            half = ins[k].shape[0] // 2
            mine, theirs = pl.ds(c * half, half), pl.ds((1 - c) * half, half)
            for j, chip in enumerate(chips):
                first.append(pltpu.make_async_remote_copy(
                    src_ref=ins[k].at[mine], dst_ref=outs[k].at[me, mine], send_sem=send_sems.at[6 * k + j],
                    recv_sem=recv_sems.at[6 * k + j], device_id=(*chip, c), device_id_type=MESH))
                if onward:
                    landed.append(region(k, j, 2 * chip[0] + chip[1], mine, (*chip, c)))
                    passed.append(region(k, 3 + j, 2 * chip[0] + chip[1], mine, (x, y, 1 - c)))
                    handed.append(region(k, 3 + j, 2 * chip[0] + chip[1], theirs, (x, y, 1 - c)))
        return first, landed, passed, handed

    def start(self, ins, outs, sems):
        for cp in self._copies(ins, outs, sems, onward=False)[0]:
            cp.start()

    def finish(self, ins, outs, sems):
        first, landed, passed, handed = self._copies(ins, outs, sems)
        for arrived, onward in zip(landed, passed):
            arrived.wait_recv()
            onward.start()
        for cp in handed:
            cp.wait_recv()
        for cp in first + passed:
            cp.wait_send()


class _ReduceExchange:
    def __init__(self, grads, axes):
        self.ins = list(grads)
        self.axes = list(axes)
        n = len(grads)
        self.out_shapes = [jax.ShapeDtypeStruct((N_DEV - 1,) + self._block(g, a), g.dtype) for g, a in zip(grads, axes)]
        self.sems = [pltpu.SemaphoreType.DMA((7 * n,)), pltpu.SemaphoreType.DMA((7 * n,))]

    @staticmethod
    def _block(g, axis):
        k, n = g.shape
        return (k // 2, n // N_CHIPS) if axis == 2 else (k // N_DEV, n)

    def _copies(self, ins, outs, sems):
        send_sems, recv_sems = sems
        x, y, c, _ = _place()
        cps = []
        for w, (g, axis) in enumerate(zip(ins, self.axes)):
            rows, cols = self._block(g, axis)
            for k in range(1, N_DEV):
                tx, ty, tc = (1 - x if k & 4 else x), (1 - y if k & 2 else y), (1 - c if k & 1 else c)
                chip = 2 * tx + ty
                if axis == 2:
                    src = g.at[pl.ds(tc * rows, rows), pl.ds(chip * cols, cols)]
                else:
                    src = g.at[pl.ds((2 * chip + tc) * rows, rows), :]
                cps.append(pltpu.make_async_remote_copy(
                    src_ref=src, dst_ref=outs[w].at[k - 1], send_sem=send_sems.at[7 * w + k - 1],
                    recv_sem=recv_sems.at[7 * w + k - 1], device_id=(tx, ty, tc), device_id_type=MESH))
        return cps

    def start(self, ins, outs, sems):
        for cp in self._copies(ins, outs, sems):
            cp.start()

    def finish(self, ins, outs, sems):
        cps = self._copies(ins, outs, sems)
        for cp in cps:
            cp.wait_recv()
        for cp in cps:
            cp.wait_send()


def _run_exchange(ex, *, name):
    n_in, n_out = len(ex.ins), len(ex.out_shapes)

    def body(*refs):
        ins, outs, sems = refs[:n_in], refs[n_in:n_in + n_out], refs[n_in + n_out:]
        ex.start(ins, outs, sems)
        ex.finish(ins, outs, sems)

    return pl.pallas_call(body, name=name, in_specs=[ANY] * n_in, out_specs=[ANY] * n_out, out_shape=ex.out_shapes,
                          scratch_shapes=ex.sems)(*ex.ins)


def _all_reduce_small(buf, *, name):
    rows = buf.shape[0]

    def body(x_ref, sum_ref, all_ref, send_sems, recv_sems, local_sem):
        x, y, c, chips = _place()
        me, sibling = (x, y, c), (x, y, 1 - c)

        def slab(px, py, pc):
            return all_ref.at[pl.ds((4 * px + 2 * py + pc) * rows, rows), :]

        def copy(k, block, to, src=None):
            return pltpu.make_async_remote_copy(
                src_ref=slab(*block) if src is None else src, dst_ref=slab(*block), send_sem=send_sems.at[k],
                recv_sem=recv_sems.at[k], device_id=to, device_id_type=MESH)

        mine = pltpu.make_async_copy(x_ref, slab(*me), local_sem)
        mine.start()
        first = [copy(0, me, sibling, src=x_ref)]
        first += [copy(1 + j, me, (*chip, c), src=x_ref) for j, chip in enumerate(chips)]
        for cp in first:
            cp.start()
        passed = [copy(4 + j, (*chip, c), sibling) for j, chip in enumerate(chips)]
        for j, chip in enumerate(chips):
            copy(1 + j, (*chip, c), me).wait_recv()
            passed[j].start()
        copy(0, sibling, me).wait_recv()
        for j, chip in enumerate(chips):
            copy(4 + j, (*chip, 1 - c), me).wait_recv()
        for cp in first + passed:
            cp.wait_send()
        mine.wait()
        acc = all_ref[0:rows, :]
        for d in range(1, N_DEV):
            acc = acc + all_ref[d * rows:(d + 1) * rows, :]
        sum_ref[...] = acc

    vmem = pl.BlockSpec(memory_space=pltpu.VMEM)
    return pl.pallas_call(
        body, name=name, in_specs=[vmem], out_specs=[vmem, vmem],
        out_shape=[jax.ShapeDtypeStruct((rows, 128), F32), jax.ShapeDtypeStruct((N_DEV * rows, 128), F32)],
        scratch_shapes=[pltpu.SemaphoreType.DMA((7,)), pltpu.SemaphoreType.DMA((7,)), pltpu.SemaphoreType.DMA],
        compiler_params=pltpu.CompilerParams(vmem_limit_bytes=VMEM_BIG),
    )(buf)[0]


WEIGHTS = ['ev_w_in', 'ev_lambda_re', 'ev_lambda_im', 'ev_log_dt', 'ev_b_re', 'ev_b_im', 'ev_c_re', 'ev_c_im', 'ev_d',
           'ev_w_glu', 'ev_b_glu', 'ev_conv_w', 'ev_w_out', 'od_w_in', 'od_rel_bias', 'od_pool_w', 'od_pool_scale',
           'od_w_out', 'ln_mix_g', 'ln_mix_b', 'ln_ffn_g', 'ln_ffn_b', 'ffn_w_up', 'ffn_w_down', 'ple_w_proj',
           'ple_w_gate', 'ple_b_gate']
INPUTS = ['x', 'p'] + WEIGHTS + ['loss_target'] + ['m_' + n for n in WEIGHTS] + ['v_' + n for n in WEIGHTS]

BIG = {
    'ev_w_in': (2, (2, 1024, 2048)), 'ev_w_glu': (1, (2, 512, 512)), 'ev_w_out': (1, (2, 1024, 1024)),
    'od_w_in': (2, (2, 1024, 2048)), 'od_w_out': (1, (2, 1024, 1024)), 'ffn_w_up': (2, (4, 1024, 5632)),
    'ffn_w_down': (1, (4, 2816, 1024)), 'ple_w_proj': (2, (4, 256, 1024)), 'ple_w_gate': (1, (4, 1024, 1024)),
}
SMALL_SHARDED = {'ev_conv_w': (2, 3, 512), 'od_pool_scale': (2, 512)}
REPLICATED = [n for n in WEIGHTS if n not in BIG and n not in SMALL_SHARDED]


def _shard_rows(name):
    axis, (nl, k, n) = BIG[name]
    return (nl * k, n // N_CHIPS) if axis == 2 else (nl * k // N_CHIPS, n)


def _pack(arrs):
    flat = jnp.concatenate([a.reshape(-1) for a in arrs])
    total = flat.shape[0]
    padded = -(-total // 1024) * 1024
    return jnp.pad(flat, (0, padded - total)).reshape(padded // 128, 128)


def _unpack(buf, shapes):
    flat = buf.reshape(-1)
    out, pos = [], 0
    for s in shapes:
        size = int(np.prod(s))
        out.append(flat[pos:pos + size].reshape(s))
        pos += size
    return out


def _s5_params(lam_re, lam_im, log_dt, b_re, b_im, c_re, c_im):
    dt = jnp.exp(log_dt)[:, None]
    mag = jnp.exp(lam_re * dt)
    ang = lam_im * dt
    lb_re = mag * jnp.cos(ang)
    lb_im = mag * jnp.sin(ang)
    den = lam_re * lam_re + lam_im * lam_im
    nr = lb_re - 1.0
    ni = lb_im
    r_re = (nr * lam_re + ni * lam_im) / den
    r_im = (ni * lam_re - nr * lam_im) / den
    bb_re = r_re[..., None] * b_re - r_im[..., None] * b_im
    bb_im = r_re[..., None] * b_im + r_im[..., None] * b_re
    per = S5_GROUPS // S5_SLABS
    eye = jnp.eye(per, dtype=F32)

    def block_diag(a):
        _, r, c = a.shape
        a = a.reshape(S5_SLABS, per, r, c)
        return (a[:, :, :, None, :] * eye[None, :, None, :, None]).reshape(S5_SLABS, per * r, per * c)

    bmat = jnp.concatenate([block_diag(bb_re.transpose(0, 2, 1)), block_diag(bb_im.transpose(0, 2, 1))], axis=2)
    cmat = jnp.concatenate([block_diag(c_re.transpose(0, 2, 1)), block_diag(-c_im.transpose(0, 2, 1))], axis=1)
    lam = jnp.stack([lb_re.reshape(S5_N), lb_im.reshape(S5_N)])
    return lam, bmat, cmat


def _lam_powers(lam):
    res, ims = [lam[0]], [lam[1]]
    for _ in range(7):
        res, ims = res + [res[-1] * lam[0] - ims[-1] * lam[1]], ims + [res[-1] * lam[1] + ims[-1] * lam[0]]
    sq_r, sq_i = [res[-1]], [ims[-1]]
    for _ in range(N_SQUARES):
        sq_r, sq_i = sq_r + [sq_r[-1] * sq_r[-1] - sq_i[-1] * sq_i[-1]], sq_i + [2.0 * sq_r[-1] * sq_i[-1]]
    return jnp.stack(res + ims + res[::-1] + ims[::-1] + sq_r[1:] + sq_i[1:])


def _layer_big(i):
    mixer = [('w_in', 'ev_w_in'), ('w_glu', 'ev_w_glu'), ('w_out', 'ev_w_out')] if i % 2 == 0 else \
        [('w_in', 'od_w_in'), ('w_out', 'od_w_out')]
    ffn = [('w_up', 'ffn_w_up'), ('w_down', 'ffn_w_down'), ('w_proj', 'ple_w_proj'), ('w_gate', 'ple_w_gate')]
    return [(k, n, i // 2) for k, n in mixer] + [(k, n, i) for k, n in ffn]


class _WholePlan:
    def __init__(self, whole):
        self.whole = whole
        self.grads = {n: {} for n in BIG}

    def layer_weights(self, i):
        return {k: self.whole[n][l] for k, n, l in _layer_big(i)}

    def forward_host(self, i):
        return None

    def backward_host(self, i):
        return None

    def early_host(self, i, g):
        return None

    def layer_grads(self, i, g):
        for k, n, l in _layer_big(i):
            self.grads[n][l] = g[k][0]


def _local_step(x, p, target, w, plan):
    mask = jnp.asarray(_band_mask())
    diag_idx = _diag_index()
    onehot = jnp.asarray(np.eye(2 * MAX_REL + 1, dtype=np.float32)[diag_idx])
    saved = []
    for i in range(DEPTH):
        li = i // 2
        lw = plan.layer_weights(i)
        s = {'x0': x, 'lw': lw}
        h = _mm([(x, 0, D_MODEL)], lw['w_in'], tn=4 * MIX, name=f"in_proj")
        if i % 2 == 0:
            (lam, bmat, cmat), s5_vjp = jax.vjp(
                _s5_params, w['ev_lambda_re'][li], w['ev_lambda_im'][li], w['ev_log_dt'][li], w['ev_b_re'][li],
                w['ev_b_im'][li], w['ev_c_re'][li], w['ev_c_im'][li])
            s5c = (bmat.astype(BF16), cmat.astype(BF16), w['ev_d'][li].reshape(1, MIX), lw['w_glu'],
                   w['ev_b_glu'][li].reshape(1, MIX), _lam_powers(lam))
            ya, ypre, hb = _s5_fwd(h, *s5c, name=f"s5_fwd")
            yb = _conv_fwd(h, w['ev_conv_w'][li], name=f"conv_fwd")
            s.update(s5_vjp=s5_vjp, s5c=s5c, ypre=ypre, hb=hb)
        else:
            vdiag = jnp.dot(w['od_rel_bias'][li], onehot.T, precision=HIGHEST).reshape(ATT_HEADS // 2, 2, NKEY)
            pw = w['od_pool_w'][li].astype(BF16)
            ps = w['od_pool_scale'][li].reshape(1, MIX)
            ya = _attn_fwd(h, vdiag, mask, name=f"attn_fwd")
            yb = _pool_fwd(h, pw, ps, name=f"pool_fwd")
            s.update(vdiag=vdiag, pw=pw, ps=ps)
        wout = lw['w_out']
        vec = lambda n: w[n][i].reshape(1, -1)

        def residual_ln(products, rows, vecs):
            r = ALPHA * rows[0] + products[0]
            return (r, _ln_apply(r, vecs[0], vecs[1])), ()

        def embed_gate(products, rows, vecs):
            gate = _sigmoid(products[0] + vecs[0])
            return (rows[0] + gate * products[1], gate, products[1]), ()

        two_f32 = [(D_MODEL, F32), (D_MODEL, F32)]
        r1, x1 = _mm_rows([([(ya, 0, MIX), (yb, 0, MIX)], wout, False)], [x], [vec('ln_mix_g'), vec('ln_mix_b')],
                          two_f32, [], residual_ln, name="out_proj_ln")
        hosted = plan.forward_host(i)
        if hosted is None:
            a, gg, uu = _ffn_up(x1, lw['w_up'], name=f"ffn_up")
        else:
            (a, gg, uu), arrived = _ffn_up(x1, lw['w_up'], exchange=hosted, name=f"ffn_up_gather")
            plan.forward_hosted(i, arrived)
        r2, x2 = _mm_rows([([(a, 0, D_FF)], lw['w_down'], False)], [x1], [vec('ln_ffn_g'), vec('ln_ffn_b')],
                          two_f32, [], residual_ln, name="ffn_down_ln")
        x3, gate, ppb = _mm_rows(
            [([(None, 0, D_MODEL)], lw['w_gate'], False), ([(p[i], 0, D_PLE)], lw['w_proj'], False)],
            [x2], [vec('ple_b_gate')], [(D_MODEL, F32), (D_MODEL, BF16), (D_MODEL, BF16)], [], embed_gate, name="ple")
        s.update(h=h, ya=ya, yb=yb, r1=r1, x1=x1, a=a, gg=gg, uu=uu, r2=r2, x2=x2, gate=gate, ppb=ppb)
        saved.append(s)
        x = x3

    loss, da = _loss_head(x, target, name="loss_head")
    grads = {n: [None] * (DEPTH if n.startswith(('ln_', 'ple_')) else DEPTH // 2) for n in WEIGHTS if n not in BIG}

    def both(pieces, axis):
        return tuple(jnp.concatenate([pc[k] for pc in pieces], axis=axis) for k in range(2))

    for i in reversed(range(DEPTH)):
        li = i // 2
        s = saved[i]
        lw = s['lw']
        big = {}
        dz, dpp, dr2, dbg, dg2, db2 = _ple_ln_bwd(da, s['gate'], s['ppb'], s['r2'], lw['w_gate'],
                                                  w['ln_ffn_g'][i].reshape(1, -1), name="ple_ln_bwd")
        grads['ple_b_gate'][i] = dbg.reshape(-1)
        big['w_gate'] = _mm_tn(s['x2'], 0, D_MODEL, dz, also_bf16=True, name=f"d_ple_gate")
        big['w_proj'] = _mm_tn(p[i], 0, D_PLE, dpp, also_bf16=True, name=f"d_ple_proj")
        grads['ln_ffn_g'][i] = dg2.reshape(-1)
        grads['ln_ffn_b'][i] = db2.reshape(-1)
        dhh = _ffn_down_bwd(dr2, lw['w_down'], s['gg'], s['uu'], name=f"ffn_down_bwd")
        big['w_down'] = _mm_tn(s['a'], 0, D_FF, dr2, tk=D_FF // 2, also_bf16=True, name=f"d_ffn_down")
        hosted = plan.backward_host(i)
        if hosted is None:
            big['w_up'] = _mm_tn(s['x1'], 0, D_MODEL, dhh, tn=D_FF // 2, also_bf16=True, name=f"d_ffn_up")
        else:
            big['w_up'], arrived = _mm_tn(s['x1'], 0, D_MODEL, dhh, tn=D_FF // 2, also_bf16=True, exchange=hosted,
                                          name=f"d_ffn_up_reduce_{i % 2}")
            plan.backward_hosted(i, arrived)

        def ln_mix_grad(products, rows, vecs):
            dr, dg, dbias = _ln_grad(rows[0], ALPHA * rows[1] + products[0], vecs[0])
            return (dr,), (dg, dbias)

        dr1, dg1, db1 = _mm_rows([([(dhh, 0, 2 * D_FF)], lw['w_up'], True)], [s['r1'], dr2],
                                 [w['ln_mix_g'][i].reshape(1, -1)], [(D_MODEL, F32)], [D_MODEL, D_MODEL], ln_mix_grad,
                                 tm=TM, vmem=VMEM_BIG, name="ffn_up_ln_bwd")
        grads['ln_mix_g'][i] = dg1.reshape(-1)
        grads['ln_mix_b'][i] = db1.reshape(-1)
        dcat = _mm([(dr1, 0, D_MODEL)], lw['w_out'], trans_b=True, name=f"out_proj_bwd")
        big['w_out'] = both([_mm_tn(s['ya'], 0, MIX, dr1, also_bf16=True, name=f"d_out_a"),
                             _mm_tn(s['yb'], 0, MIX, dr1, also_bf16=True, name=f"d_out_b")], 0)
        h = s['h']
        if i % 2 == 0:
            s5c = s['s5c']
            hosted = plan.early_host(i, big)
            if hosted is None:
                s5_out = _s5_bwd(dcat, s['ypre'], h, s['hb'], *s5c, name=f"s5_bwd")
            else:
                s5_out, arrived = _s5_bwd(dcat, s['ypre'], h, s['hb'], *s5c, exchange=hosted, name=f"s5_bwd_reduce")
                plan.early_hosted(i, arrived)
            du, xb, gb, gq, dzzq, dyq, dlam, dbglu, dd = s5_out
            dbmat = _mm_tn_slabs(h, 128, gb, SLAB_COLS, S5_SLABS, name=f"d_s5_b")
            dcmat = _mm_tn_slabs(xb, SLAB_COLS, dyq, 128, S5_SLABS, name=f"d_s5_c")
            s5g = s['s5_vjp']((dlam, dbmat, dcmat))
            for n, g_ in zip(['ev_lambda_re', 'ev_lambda_im', 'ev_log_dt', 'ev_b_re', 'ev_b_im', 'ev_c_re', 'ev_c_im'], s5g):
                grads[n][li] = g_
            big['w_glu'] = _mm_tn(gq, 0, MIX, dzzq, also_bf16=True, name=f"d_glu")
            grads['ev_b_glu'][li] = dbglu.reshape(-1)
            grads['ev_d'][li] = dd.reshape(-1)
            d3, dcw = _conv_bwd(dcat, h, w['ev_conv_w'][li], name=f"conv_bwd")
            grads['ev_conv_w'][li] = dcw[0:3]
            big['w_in'] = both([_mm_tn(s['x0'], 0, D_MODEL, du, also_bf16=True, name=f"d_in_a"),
                                _mm_tn(s['x0'], 0, D_MODEL, d3, tn=3 * MIX, also_bf16=True, name=f"d_in_b")], 1)
            dh_parts = [(du, 0, MIX), (d3, 0, 3 * MIX)]
        else:
            dq, dk, dv, dvd = _attn_bwd(dcat, h, s['vdiag'], mask, name=f"attn_bwd")
            dzp, dpw, dps = _pool_bwd(dcat, h, s['pw'], s['ps'], name=f"pool_bwd")
            parts = [dq, dk, dv, dzp]
            grads['od_rel_bias'][li] = jnp.dot(dvd.reshape(ATT_HEADS, NKEY), onehot, precision=HIGHEST)
            grads['od_pool_w'][li] = dpw
            grads['od_pool_scale'][li] = dps.reshape(-1)
            big['w_in'] = both([_mm_tn(s['x0'], 0, D_MODEL, d_, also_bf16=True, name=f"d_in_a") for d_ in parts], 1)
            dh_parts = [(d_, 0, MIX) for d_ in parts]
        plan.layer_grads(i, big)

        def layer_input_grad(products, rows, vecs):
            return (ALPHA * rows[0] + products[0],), ()

        (da,) = _mm_rows([(dh_parts, lw['w_in'], True)], [dr1], [], [(D_MODEL, F32)], [], layer_input_grad,
                         name=f"in_proj_bwd_{i % 2}")
    return loss, da, {n: jnp.stack(g) for n, g in grads.items()}


def _sum_blocks(own, others, *, name):
    rows, cols = own.shape
    t = _row_tile(rows)

    def body(own_ref, others_ref, o_ref):
        acc = own_ref[...]
        for k in range(N_DEV - 1):
            acc = acc + others_ref[k].astype(F32)
        o_ref[...] = acc

    return pl.pallas_call(
        body, name=name, grid=(rows // t,),
        in_specs=[pl.BlockSpec((t, cols), lambda i: (i, 0)), pl.BlockSpec((N_DEV - 1, t, cols), lambda i: (0, i, 0))],
        out_specs=pl.BlockSpec((t, cols), lambda i: (i, 0)), out_shape=jax.ShapeDtypeStruct((rows, cols), F32),
        compiler_params=_cparams(("parallel",)),
    )(own, others)


def _swap_sibling(arrs, *, name):
    n = len(arrs)

    def body(*refs):
        ins, outs = refs[:n], refs[n:2 * n]
        send_sems, recv_sems = refs[2 * n:]
        x, y, c, _ = _place()
        cps = [pltpu.make_async_remote_copy(src_ref=ins[k], dst_ref=outs[k], send_sem=send_sems.at[k], recv_sem=recv_sems.at[k],
                                            device_id=(x, y, 1 - c), device_id_type=MESH) for k in range(n)]
        for cp in cps:
            cp.start()
        for cp in cps:
            cp.wait_recv()
        for cp in cps:
            cp.wait_send()

    return pl.pallas_call(
        body, name=name, in_specs=[ANY] * n, out_specs=[ANY] * n,
        out_shape=[jax.ShapeDtypeStruct(a.shape, a.dtype) for a in arrs],
        scratch_shapes=[pltpu.SemaphoreType.DMA((n,)), pltpu.SemaphoreType.DMA((n,))],
    )(*arrs)


class _ShardedPlan:
    def __init__(self, a, c, me):
        self.a, self.c, self.me = a, c, me
        self.weights, self.pending, self.own, self.arrived = {}, None, {}, {}

    def _shards(self, i):
        return [self.a[n][l].astype(BF16) for _, n, l in _layer_big(i)]

    def _with_own(self, gathered, own):
        return lax.dynamic_update_index_in_dim(gathered, own, self.me, 0)

    def _set_weights(self, i, gathered):
        lw = {}
        for (k, n, _), g, own in zip(_layer_big(i), gathered, self._shards(i)):
            _, rows, cols = g.shape
            g = self._with_own(g, own)
            lw[k] = g.transpose(1, 0, 2).reshape(rows, N_CHIPS * cols) if BIG[n][0] == 2 else g.reshape(N_CHIPS * rows, cols)
        self.weights[i] = lw

    def gather_first(self, misc):
        gathered = _run_exchange(_GatherExchange(self._shards(0) + [misc]), name="weight_gather_0")
        self._set_weights(0, gathered[:-1])
        return self._with_own(gathered[-1], misc)

    def layer_weights(self, i):
        return self.weights.pop(i)

    def forward_host(self, i):
        return _GatherExchange(self._shards(i + 1)) if i + 1 < DEPTH else None

    def forward_hosted(self, i, arrived):
        self._set_weights(i + 1, arrived)

    EARLY = ('w_up', 'w_down', 'w_proj', 'w_gate')

    def _reduce_exchange(self, i, g, early=None):
        items = [(k, n, l) for k, n, l in _layer_big(i) if early is None or (k in self.EARLY) == early]
        return [(n, l) for _, n, l in items], _ReduceExchange([g[k][1] for k, _, _ in items], [BIG[n][0] for _, n, _ in items])

    def early_host(self, i, g):
        if i != 0:
            return None
        self.early_keys, exchange = self._reduce_exchange(0, g, early=True)
        return exchange

    def early_hosted(self, i, arrived):
        self.arrived.update(zip(self.early_keys, arrived))

    def layer_grads(self, i, g):
        for k, n, l in _layer_big(i):
            full = g[k][0]
            kk, nn = full.shape
            if BIG[n][0] == 2:
                self.own[n, l] = lax.dynamic_slice(full, (self.c * (kk // 2), self.me * (nn // N_CHIPS)), (kk // 2, nn // N_CHIPS))
            else:
                self.own[n, l] = lax.dynamic_slice_in_dim(full, (2 * self.me + self.c) * (kk // N_DEV), kk // N_DEV, axis=0)
        self.pending = (i, g)

    def backward_host(self, i):
        if i + 1 >= DEPTH:
            return None
        self.hosted_keys, exchange = self._reduce_exchange(*self.pending)
        return exchange

    def backward_hosted(self, i, arrived):
        self.arrived.update(zip(self.hosted_keys, arrived))

    def reduced(self):
        late_keys, exchange = self._reduce_exchange(*self.pending, early=False)
        self.arrived.update(zip(late_keys, _run_exchange(exchange, name="grad_reduce_0")))
        keys = [(n, l) for n in BIG for l in range(BIG[n][1][0])]
        mine = [_sum_blocks(self.own[k], self.arrived[k], name=f"grad_sum_{k[0]}") for k in keys]
        theirs = _swap_sibling(mine, name="grad_half_swap")
        out = {}
        for n in BIG:
            layers = []
            for l in range(BIG[n][1][0]):
                a_, b_ = mine[keys.index((n, l))], theirs[keys.index((n, l))]
                layers.append(jnp.where(self.c == 0, jnp.concatenate([a_, b_], axis=0), jnp.concatenate([b_, a_], axis=0)))
            out[n] = jnp.stack(layers)
        return out


def kernel(x, p, ev_w_in, ev_lambda_re, ev_lambda_im, ev_log_dt, ev_b_re, ev_b_im, ev_c_re, ev_c_im, ev_d, ev_w_glu, ev_b_glu, ev_conv_w, ev_w_out, od_w_in, od_rel_bias, od_pool_w, od_pool_scale, od_w_out, ln_mix_g, ln_mix_b, ln_ffn_g, ln_ffn_b, ffn_w_up, ffn_w_down, ple_w_proj, ple_w_gate, ple_b_gate, loss_target, m_ev_w_in, m_ev_lambda_re, m_ev_lambda_im, m_ev_log_dt, m_ev_b_re, m_ev_b_im, m_ev_c_re, m_ev_c_im, m_ev_d, m_ev_w_glu, m_ev_b_glu, m_ev_conv_w, m_ev_w_out, m_od_w_in, m_od_rel_bias, m_od_pool_w, m_od_pool_scale, m_od_w_out, m_ln_mix_g, m_ln_mix_b, m_ln_ffn_g, m_ln_ffn_b, m_ffn_w_up, m_ffn_w_down, m_ple_w_proj, m_ple_w_gate, m_ple_b_gate, v_ev_w_in, v_ev_lambda_re, v_ev_lambda_im, v_ev_log_dt, v_ev_b_re, v_ev_b_im, v_ev_c_re, v_ev_c_im, v_ev_d, v_ev_w_glu, v_ev_b_glu, v_ev_conv_w, v_ev_w_out, v_od_w_in, v_od_rel_bias, v_od_pool_w, v_od_pool_scale, v_od_w_out, v_ln_mix_g, v_ln_mix_b, v_ln_ffn_g, v_ln_ffn_b, v_ffn_w_up, v_ffn_w_down, v_ple_w_proj, v_ple_w_gate, v_ple_b_gate):
    given = locals()
    a = {n: given[n] for n in INPUTS}
    x, y, c = lax.axis_index("x"), lax.axis_index("y"), lax.axis_index("c")
    me = 2 * x + y

    plan = _ShardedPlan(a, c, me)
    misc = jnp.concatenate([a['ev_conv_w'].reshape(6, 128), a['od_pool_scale'], jnp.zeros((8, 128), F32)], axis=0)
    gm = plan.gather_first(misc)
    w = {n: a[n] for n in REPLICATED}
    w['ev_conv_w'] = gm[:, 0:6].reshape(N_CHIPS, 2, 3, 128).transpose(1, 2, 0, 3).reshape(2, 3, 512)
    w['od_pool_scale'] = gm[:, 6:8].transpose(1, 0, 2).reshape(2, 512)

    loss, grad_x, grads = _local_step(a['x'][0], a['p'][:, 0], a['loss_target'][0], w, plan)
    loss = lax.psum(loss[0, 0], ("x", "y", "c"))

    small_names = REPLICATED + list(SMALL_SHARDED)
    small = _all_reduce_small(_pack([grads[n] for n in small_names]), name="small_grad_all_reduce")
    small = dict(zip(small_names, _unpack(small, [grads[n].shape for n in small_names])))
    for n in SMALL_SHARDED:
        small[n] = lax.dynamic_slice_in_dim(small[n], me * 128, 128, axis=small[n].ndim - 1)
    big = plan.reduced()

    res = {}
    for n in BIG:
        shape = a[n].shape
        flat = _shard_rows(n)
        d, m_, v_ = _adamw(a[n].reshape(flat), big[n].reshape(flat), a['m_' + n].reshape(flat), a['v_' + n].reshape(flat),
                           name=f"adamw_{n}")
        res[n] = (big[n], d.reshape(shape), m_.reshape(shape), v_.reshape(shape))
    shapes = [a[n].shape for n in small_names]
    d, m_, v_ = _adamw(_pack([a[n] for n in small_names]), _pack([small[n] for n in small_names]),
                       _pack([a['m_' + n] for n in small_names]), _pack([a['v_' + n] for n in small_names]), name="adamw_small")
    for n, dd, mm, vv in zip(small_names, _unpack(d, shapes), _unpack(m_, shapes), _unpack(v_, shapes)):
        res[n] = (small[n], dd, mm, vv)

    outs = [loss, grad_x[None]]
    for part in range(4):
        outs += [res[n][part] for n in WEIGHTS]
    return tuple(outs)
```

```python
import functools
import math

import jax
import jax.numpy as jnp
import numpy as np
from jax import lax
from jax.experimental import pallas as pl
from jax.experimental.pallas import tpu as pltpu

F32 = jnp.float32
BF16 = jnp.bfloat16
MESH = pl.DeviceIdType.MESH
HIGHEST = lax.Precision.HIGHEST

D_MODEL = 1024
DEPTH = 4
MIX = 512
S5_GROUPS = 32
S5_GROUP = 16
S5_STATE = 64
S5_N = S5_GROUPS * S5_STATE
CHUNK = 64
LEFT_CHUNKS = 8
MAX_REL = 128
ATT_HEADS = 8
HEAD_DIM = 64
POOL_WINDOWS = (2, 4, 8, 16)
POOL_GROUP = 128
D_FF = 2816
D_PLE = 256
ALPHA = (2 * DEPTH) ** 0.25
LN_EPS = 1e-5
NEG_INF = -1e30
N_CHIPS = 4
N_DEV = 8

ADAM_LR = 0.001
ADAM_B1 = 0.9
ADAM_B2 = 0.999
ADAM_EPS = 1e-08
ADAM_WD = 0.01
ADAM_STEP = 10

TM = 512
T_S5 = 256
T_ATT = 512
VMEM_BIG = 56 * 1024 * 1024


VMEM_DEFAULT = 48 * 1024 * 1024


def _cparams(sem, vmem=None):
    return pltpu.CompilerParams(dimension_semantics=sem, vmem_limit_bytes=vmem or VMEM_DEFAULT)


def _sigmoid(x):
    return 0.5 + 0.5 * jnp.tanh(0.5 * x)


def _mm(a_parts, b, *, name, trans_b=False, out_dtype=F32, tm=TM, tn=1024, vmem=None):
    m = a_parts[0][0].shape[0]
    n = b.shape[0] if trans_b else b.shape[1]
    kk = b.shape[1] if trans_b else b.shape[0]
    tn = min(tn, n)
    widths = [w for _, _, w in a_parts]
    assert sum(widths) == kk and m % tm == 0 and n % tn == 0
    na = len(a_parts)

    def body(*refs):
        b_ref, o_ref = refs[na], refs[na + 1]
        acc = None
        k0 = 0
        for ar, w in zip(refs[:na], widths):
            a = ar[...].astype(BF16)
            if trans_b:
                part = lax.dot_general(a, b_ref[:, k0:k0 + w], (((1,), (1,)), ((), ())), preferred_element_type=F32)
            else:
                part = jnp.dot(a, b_ref[k0:k0 + w, :], preferred_element_type=F32)
            acc = part if acc is None else acc + part
            k0 += w
        o_ref[...] = acc.astype(o_ref.dtype)

    in_specs = [pl.BlockSpec((tm, w), functools.partial(lambda j, i, cb: (i, cb), cb=cb)) for _, cb, w in a_parts]
    if trans_b:
        in_specs.append(pl.BlockSpec((tn, kk), lambda j, i: (j, 0)))
    else:
        in_specs.append(pl.BlockSpec((kk, tn), lambda j, i: (0, j)))
    return pl.pallas_call(
        body, name=name, grid=(n // tn, m // tm), in_specs=in_specs,
        out_specs=pl.BlockSpec((tm, tn), lambda j, i: (i, j)),
        out_shape=jax.ShapeDtypeStruct((m, n), out_dtype),
        compiler_params=_cparams(("parallel", "parallel"), vmem),
    )(*[a for a, _, _ in a_parts], b)


def _host_parts(exchange):
    if exchange is None:
        return [], [], [], [], []
    any_space = pl.BlockSpec(memory_space=pl.ANY)
    return (exchange.ins, [any_space] * len(exchange.ins), [any_space] * len(exchange.out_shapes),
            list(exchange.out_shapes), list(exchange.sems))


def _host_run(exchange, refs, first, last):
    if exchange is None:
        return
    n_in, n_out = len(exchange.ins), len(exchange.out_shapes)
    ins, outs, sems = refs[:n_in], refs[n_in:n_in + n_out], refs[n_in + n_out:]

    @pl.when(first)
    def _():
        exchange.start(ins, outs, sems)

    @pl.when(last)
    def _():
        exchange.finish(ins, outs, sems)


def _mm_tn(a, a_cb, ka, b, *, name, tk=1024, tn=1024, tmr=2 * TM, vmem=None, also_bf16=False, exchange=None):
    m = a.shape[0]
    n = b.shape[1]
    tk = min(tk, ka)
    tn = min(tn, n)
    assert ka % tk == 0 and n % tn == 0 and m % tmr == 0
    kb = ka // tk
    grid = (kb, n // tn, m // tmr)
    ex_ops, ex_in_specs, ex_out_specs, ex_out_shapes, ex_scratch = _host_parts(exchange)
    n_own_out = 2 if also_bf16 else 1

    def body(*refs):
        a_ref, b_ref = refs[:2]
        hosted_in = refs[2:2 + len(ex_ops)]
        outs = refs[2 + len(ex_ops):]
        o_ref = outs[0]
        k, j, r = pl.program_id(0), pl.program_id(1), pl.program_id(2)
        _host_run(exchange, list(hosted_in) + list(outs[n_own_out:]),
                  (k == 0) & (j == 0) & (r == 0), (k == grid[0] - 1) & (j == grid[1] - 1) & (r == grid[2] - 1))

        @pl.when(r == 0)
        def _():
            o_ref[...] = jnp.zeros_like(o_ref)

        o_ref[...] += lax.dot_general(a_ref[...].astype(BF16), b_ref[...].astype(BF16), (((0,), (0,)), ((), ())),
                                      preferred_element_type=F32)
        if also_bf16:
            @pl.when(r == grid[2] - 1)
            def _():
                outs[1][...] = o_ref[...].astype(BF16)

    tile = pl.BlockSpec((tk, tn), lambda k, j, r: (k, j))
    res = pl.pallas_call(
        body, name=name, grid=grid,
        in_specs=[pl.BlockSpec((tmr, tk), lambda k, j, r: (r, a_cb * kb + k)),
                  pl.BlockSpec((tmr, tn), lambda k, j, r: (r, j))] + ex_in_specs,
        out_specs=[tile] * n_own_out + ex_out_specs,
        out_shape=[jax.ShapeDtypeStruct((ka, n), F32)] + ([jax.ShapeDtypeStruct((ka, n), BF16)] if also_bf16 else [])
        + ex_out_shapes,
        scratch_shapes=ex_scratch,
        compiler_params=_cparams(("arbitrary",) * 3 if exchange is not None else ("parallel", "parallel", "arbitrary"), vmem),
    )(a, b, *ex_ops)
    if exchange is None:
        return tuple(res) if also_bf16 else res[0]
    own = tuple(res[:n_own_out]) if also_bf16 else res[0]
    return own, list(res[n_own_out:])


def _mm_tn_slabs(a, ka, b, nbw, nslab, *, name, tmr=2 * TM):
    m = a.shape[0]
    assert m % tmr == 0

    def body(a_ref, b_ref, o_ref):
        @pl.when(pl.program_id(1) == 0)
        def _():
            o_ref[...] = jnp.zeros_like(o_ref)

        o_ref[0] += lax.dot_general(a_ref[...].astype(BF16), b_ref[...].astype(BF16), (((0,), (0,)), ((), ())),
                                    preferred_element_type=F32)

    return pl.pallas_call(
        body, name=name, grid=(nslab, m // tmr),
        in_specs=[pl.BlockSpec((tmr, ka), lambda s, r: (r, s)), pl.BlockSpec((tmr, nbw), lambda s, r: (r, s))],
        out_specs=pl.BlockSpec((1, ka, nbw), lambda s, r: (s, 0, 0)),
        out_shape=jax.ShapeDtypeStruct((nslab, ka, nbw), F32),
        compiler_params=_cparams(("parallel", "arbitrary")),
    )(a, b)


def _ln_stats(r):
    mu = jnp.mean(r, axis=-1, keepdims=True)
    xc = r - mu
    var = jnp.mean(xc * xc, axis=-1, keepdims=True)
    rstd = lax.rsqrt(var + LN_EPS)
    return xc * rstd, rstd


def _ln_apply(r, g, b):
    xhat, _ = _ln_stats(r)
    return xhat * g + b


def _ln_grad(r, dy, g):
    xhat, rstd = _ln_stats(r)
    dxh = dy * g
    m1 = jnp.mean(dxh, axis=-1, keepdims=True)
    m2 = jnp.mean(dxh * xhat, axis=-1, keepdims=True)
    return (rstd * (dxh - m1 - xhat * m2), jnp.sum(dy * xhat, axis=0, keepdims=True), jnp.sum(dy, axis=0, keepdims=True))


def _mm_rows(matmuls, rows_in, vecs_in, out_rows, acc_widths, fn, *, name, tm=TM, vmem=None):
    m = rows_in[0].shape[0]
    assert m % tm == 0
    flat, in_specs, layout = [], [], []
    for a_parts, b, trans_b in matmuls:
        own = [(arr, cb, w) for arr, cb, w in a_parts if arr is not None]
        for arr, cb, w in own:
            flat.append(arr)
            in_specs.append(pl.BlockSpec((tm, w), functools.partial(lambda i, cb: (i, cb), cb=cb)))
        flat.append(b)
        in_specs.append(pl.BlockSpec(b.shape, lambda i: (0, 0)))
        layout.append(([(arr is None, cb, w) for arr, cb, w in a_parts], len(own), trans_b))
    first_row = len(flat)
    for r in rows_in:
        flat.append(r)
        in_specs.append(pl.BlockSpec((tm, r.shape[1]), lambda i: (i, 0)))
    for v in vecs_in:
        flat.append(v)
        in_specs.append(pl.BlockSpec(v.shape, lambda i: (0, 0)))
    n_in = len(flat)
    n_rows_out = len(out_rows)

    def body(*refs):
        rows = [r[...] for r in refs[first_row:first_row + len(rows_in)]]
        vecs = [v[...] for v in refs[first_row + len(rows_in):n_in]]
        pos = 0
        products = []
        for parts, n_own, trans_b in layout:
            b_ref = refs[pos + n_own]
            own_refs = iter(refs[pos:pos + n_own])
            acc, k0 = None, 0
            for is_row, cb, w in parts:
                a = (rows[cb] if is_row else next(own_refs)[...]).astype(BF16)
                if trans_b:
                    part = lax.dot_general(a, b_ref[:, k0:k0 + w], (((1,), (1,)), ((), ())), preferred_element_type=F32)
                else:
                    part = jnp.dot(a, b_ref[k0:k0 + w, :], preferred_element_type=F32)
                acc = part if acc is None else acc + part
                k0 += w
            products.append(acc)
            pos += n_own + 1
        outs, sums = fn(products, rows, vecs)
        for o_ref, o in zip(refs[n_in:n_in + n_rows_out], outs):
            o_ref[...] = o.astype(o_ref.dtype)
        if acc_widths:
            acc_refs = refs[n_in + n_rows_out:]

            @pl.when(pl.program_id(0) == 0)
            def _():
                for a_ref in acc_refs:
                    a_ref[...] = jnp.zeros_like(a_ref)

            for a_ref, s_ in zip(acc_refs, sums):
                a_ref[...] += s_

    out_specs = [pl.BlockSpec((tm, n), lambda i: (i, 0)) for n, _ in out_rows]
    out_specs += [pl.BlockSpec((1, wd), lambda i: (0, 0)) for wd in acc_widths]
    out_shape = [jax.ShapeDtypeStruct((m, n), dt) for n, dt in out_rows]
    out_shape += [jax.ShapeDtypeStruct((1, wd), F32) for wd in acc_widths]
    return pl.pallas_call(
        body, name=name, grid=(m // tm,), in_specs=in_specs, out_specs=out_specs, out_shape=out_shape,
        compiler_params=_cparams(("arbitrary",) if acc_widths else ("parallel",), vmem),
    )(*flat)


def _ffn_up(x1, wup, *, name, exchange=None):
    m = x1.shape[0]
    tn = D_FF // 2
    grid = (2, m // TM)
    ex_ops, ex_in_specs, ex_out_specs, ex_out_shapes, ex_scratch = _host_parts(exchange)

    def body(*refs):
        x_ref, wg_ref, wu_ref = refs[:3]
        hosted_in = refs[3:3 + len(ex_ops)]
        a_ref, g_ref, u_ref = refs[3 + len(ex_ops):6 + len(ex_ops)]
        j, i = pl.program_id(0), pl.program_id(1)
        _host_run(exchange, list(hosted_in) + list(refs[6 + len(ex_ops):]),
                  (j == 0) & (i == 0), (j == grid[0] - 1) & (i == grid[1] - 1))
        x = x_ref[...].astype(BF16)
        g = jnp.dot(x, wg_ref[...], preferred_element_type=F32)
        u = jnp.dot(x, wu_ref[...], preferred_element_type=F32)
        sg = _sigmoid(g)
        silu = g * sg
        a_ref[...] = (silu * u).astype(BF16)
        g_ref[...] = (u * (sg + silu * (1.0 - sg))).astype(BF16)
        u_ref[...] = silu.astype(BF16)

    out = pl.BlockSpec((TM, tn), lambda j, i: (i, j))
    res = pl.pallas_call(
        body, name=name, grid=grid,
        in_specs=[pl.BlockSpec((TM, D_MODEL), lambda j, i: (i, 0)),
                  pl.BlockSpec((D_MODEL, tn), lambda j, i: (0, j)),
                  pl.BlockSpec((D_MODEL, tn), lambda j, i: (0, j + 2))] + ex_in_specs,
        out_specs=[out, out, out] + ex_out_specs,
        out_shape=[jax.ShapeDtypeStruct((m, D_FF), BF16)] * 3 + ex_out_shapes,
        scratch_shapes=ex_scratch,
        compiler_params=_cparams(("arbitrary", "arbitrary") if exchange is not None else ("parallel", "parallel")),
    )(x1, wup, wup, *ex_ops)
    return (res[0], res[1], res[2]) if exchange is None else ((res[0], res[1], res[2]), list(res[3:]))


def _ffn_down_bwd(df, wdown, g, u, *, name):
    m = df.shape[0]
    tm = TM
    chunk = 256

    def body(df_ref, w_ref, g_ref, u_ref, o_ref):
        df = df_ref[...].astype(BF16)
        for part in range(D_FF // chunk):
            cols = slice(chunk * part, chunk * (part + 1))
            da = lax.dot_general(df, w_ref[cols, :], (((1,), (1,)), ((), ())), preferred_element_type=F32)
            o_ref[:, cols] = (da * g_ref[:, cols].astype(F32)).astype(BF16)
            o_ref[:, D_FF + chunk * part:D_FF + chunk * (part + 1)] = (da * u_ref[:, cols].astype(F32)).astype(BF16)

    return pl.pallas_call(
        body, name=name, grid=(m // tm,),
        in_specs=[pl.BlockSpec((tm, D_MODEL), lambda i: (i, 0)), pl.BlockSpec((D_FF, D_MODEL), lambda i: (0, 0)),
                  pl.BlockSpec((tm, D_FF), lambda i: (i, 0)), pl.BlockSpec((tm, D_FF), lambda i: (i, 0))],
        out_specs=pl.BlockSpec((tm, 2 * D_FF), lambda i: (i, 0)),
        out_shape=jax.ShapeDtypeStruct((m, 2 * D_FF), BF16),
        compiler_params=_cparams(("parallel",), VMEM_BIG),
    )(df, wdown, g, u)


def _ple_ln_bwd(dx3, gate, pp, r2, wgate, g2, *, name):
    m, n = dx3.shape

    def body(dx3_ref, gate_ref, pp_ref, r_ref, w_ref, g_ref, dz_ref, dpp_ref, dr_ref, dbg_ref, dg_ref, dbias_ref):
        dx3 = dx3_ref[...]

        @pl.when(pl.program_id(0) == 0)
        def _():
            dbg_ref[...] = jnp.zeros_like(dbg_ref)
            dg_ref[...] = jnp.zeros_like(dg_ref)
            dbias_ref[...] = jnp.zeros_like(dbias_ref)

        gate = gate_ref[...].astype(F32)
        dz = dx3 * pp_ref[...].astype(F32) * gate * (1.0 - gate)
        dzq = dz.astype(BF16)
        dz_ref[...] = dzq
        dpp_ref[...] = (dx3 * gate).astype(BF16)
        dbg_ref[...] += jnp.sum(dz, axis=0, keepdims=True)
        dx2 = dx3 + lax.dot_general(dzq, w_ref[...], (((1,), (1,)), ((), ())), preferred_element_type=F32)
        dr, dg, dbias = _ln_grad(r_ref[...], dx2, g_ref[...])
        dr_ref[...] = dr
        dg_ref[...] += dg
        dbias_ref[...] += dbias

    row = pl.BlockSpec((TM, n), lambda i: (i, 0))
    vec = pl.BlockSpec((1, n), lambda i: (0, 0))
    in_specs = [row] * 4 + [pl.BlockSpec(wgate.shape, lambda i: (0, 0)), vec]
    return pl.pallas_call(
        body, name=name, grid=(m // TM,), in_specs=in_specs, out_specs=[row, row, row, vec, vec, vec],
        out_shape=[jax.ShapeDtypeStruct((m, n), BF16), jax.ShapeDtypeStruct((m, n), BF16), jax.ShapeDtypeStruct((m, n), F32)]
        + [jax.ShapeDtypeStruct((1, n), F32)] * 3,
        compiler_params=_cparams(("arbitrary",)),
    )(dx3, gate, pp, r2, wgate, g2)


def _loss_head(y, target, *, name):
    m, n = y.shape

    def body(y_ref, t_ref, loss_ref, dy_ref):
        @pl.when(pl.program_id(0) == 0)
        def _():
            loss_ref[...] = jnp.zeros_like(loss_ref)

        err = y_ref[...] - t_ref[...]
        dy_ref[...] = err * (1.0 / n)
        per_tok = jnp.mean(err * err, axis=-1, keepdims=True)
        loss_ref[...] += 0.5 * jnp.sum(per_tok, axis=0, keepdims=True)

    row = pl.BlockSpec((TM, n), lambda i: (i, 0))
    return pl.pallas_call(
        body, name=name, grid=(m // TM,), in_specs=[row, row],
        out_specs=[pl.BlockSpec((1, 1), lambda i: (0, 0)), row],
        out_shape=[jax.ShapeDtypeStruct((1, 1), F32), jax.ShapeDtypeStruct((m, n), F32)],
        compiler_params=_cparams(("arbitrary",)),
    )(y, target)


def _gelu(y):
    c = math.sqrt(2.0 / math.pi)
    return 0.5 * y * (1.0 + jnp.tanh(c * (y + 0.044715 * y * y * y)))


def _gelu_grad(y):
    c = math.sqrt(2.0 / math.pi)
    t = jnp.tanh(c * (y + 0.044715 * y * y * y))
    return 0.5 * (1.0 + t) + 0.5 * y * (1.0 - t * t) * c * (1.0 + 3.0 * 0.044715 * y * y)


STRIP = 128
S5_SLABS = 4
SLAB_COLS = 2 * S5_N // S5_SLABS
N_TILES = 2 * S5_N // STRIP
SLAB_TILES = SLAB_COLS // STRIP


def _strip_tiles(j):
    re_tile = (j // (SLAB_TILES // 2)) * SLAB_TILES + j % (SLAB_TILES // 2)
    return pl.multiple_of(j * STRIP, STRIP), re_tile, re_tile + SLAB_TILES // 2


def _store_tiles(ref, first_tile, value):
    for k in range(value.shape[1] // STRIP):
        ref[first_tile + k] = value[:, STRIP * k:STRIP * (k + 1)]


def _load_tiles(ref, first_tile, count):
    return jnp.concatenate([ref[first_tile + k] for k in range(count)], axis=1)


GROUPS = T_S5 // 8
N_SQUARES = GROUPS.bit_length() - 2
PTAB_ROWS = 32 + 2 * N_SQUARES


def _scan_cols(ref, hr, hi, ptab_ref, off, down, visit=None):
    sign = 1.0 if down else -1.0
    ref_r, ref_i = ref
    cols_p = pl.ds(off, STRIP)

    def power(row, im_offset=8):
        return ptab_ref[row:row + 1, cols_p], sign * ptab_ref[row + im_offset:row + im_offset + 1, cols_p]

    def rows(r):
        return pl.ds(r, GROUPS, stride=8)

    def mul_add(br, bi, qr, qi, vr, vi):
        return br + qr * vr - qi * vi, bi + qr * vi + qi * vr

    order = list(range(8)) if down else list(range(7, -1, -1))
    lam_r, lam_i = power(0)
    vr, vi = ref_r[rows(order[0]), :], ref_i[rows(order[0]), :]
    for r in order[1:]:
        vr, vi = mul_add(ref_r[rows(r), :], ref_i[rows(r), :], lam_r, lam_i, vr, vi)
        ref_r[rows(r), :] = vr
        ref_i[rows(r), :] = vi
    grow = lax.broadcasted_iota(jnp.int32, (GROUPS, STRIP), 0)
    edge = 0 if down else GROUPS - 1
    l8r, l8i = power(7)
    er = vr + jnp.where(grow == edge, l8r * hr - l8i * hi, 0.0)
    ei = vi + jnp.where(grow == edge, l8r * hi + l8i * hr, 0.0)
    k, step = 0, 1
    while step < GROUPS:
        qr, qi = (l8r, l8i) if k == 0 else power(32 + k - 1, N_SQUARES)
        if down:
            sr = jnp.where(grow >= step, pltpu.roll(er, step, 0), 0.0)
            si = jnp.where(grow >= step, pltpu.roll(ei, step, 0), 0.0)
        else:
            sr = jnp.where(grow < GROUPS - step, pltpu.roll(er, GROUPS - step, 0), 0.0)
            si = jnp.where(grow < GROUPS - step, pltpu.roll(ei, GROUPS - step, 0), 0.0)
        er, ei = mul_add(er, ei, qr, qi, sr, si)
        k, step = k + 1, 2 * step
    if down:
        cr = jnp.where(grow == 0, hr, pltpu.roll(er, 1, 0))
        ci = jnp.where(grow == 0, hi, pltpu.roll(ei, 1, 0))
    else:
        cr = jnp.where(grow == GROUPS - 1, hr, pltpu.roll(er, GROUPS - 1, 0))
        ci = jnp.where(grow == GROUPS - 1, hi, pltpu.roll(ei, GROUPS - 1, 0))
    for r in range(8):
        qr, qi = power(r if down else 16 + r)
        xr, xi = mul_add(ref_r[rows(r), :], ref_i[rows(r), :], qr, qi, cr, ci)
        ref_r[rows(r), :] = xr
        ref_i[rows(r), :] = xi
        if visit is not None:
            visit(r, xr, xi)
    last = GROUPS - 1 if down else 0
    return er[last:last + 1], ei[last:last + 1]


def _s5_fwd(h, bmat, cmat, dvec, wglu, bglu, ptab, *, name):
    m = h.shape[0]
    t = T_S5
    nb = m // t

    def body(u_ref, bmat_ref, cmat_ref, d_ref, wglu_ref, bglu_ref, ptab_ref,
             out_ref, y_ref, hb_ref, bu_ref, carry_ref):
        @pl.when(pl.program_id(0) == 0)
        def _():
            carry_ref[...] = jnp.zeros_like(carry_ref)

        hb_ref[0] = carry_ref[...]
        u = u_ref[...]
        ub = u.astype(BF16)
        for s in range(S5_SLABS):
            _store_tiles(bu_ref, SLAB_TILES * s,
                         jnp.dot(ub[:, 128 * s:128 * (s + 1)], bmat_ref[s], preferred_element_type=F32))

        def strip(j, c):
            off, tr, ti = _strip_tiles(j)
            cols_r, cols_i = pl.ds(pl.multiple_of(tr * STRIP, STRIP), STRIP), pl.ds(pl.multiple_of(ti * STRIP, STRIP), STRIP)
            er, ei = _scan_cols((bu_ref.at[tr], bu_ref.at[ti]), carry_ref[0:1, cols_r], carry_ref[0:1, cols_i],
                                ptab_ref, off, True)
            carry_ref[0:1, cols_r] = er
            carry_ref[0:1, cols_i] = ei
            return c

        lax.fori_loop(0, S5_N // STRIP, strip, 0)
        y = jnp.concatenate(
            [jnp.dot(_load_tiles(bu_ref, SLAB_TILES * s, SLAB_TILES).astype(BF16), cmat_ref[s], preferred_element_type=F32)
             for s in range(S5_SLABS)], axis=1) + d_ref[...] * u
        y_ref[...] = y
        g = _gelu(y)
        zz = jnp.dot(g.astype(BF16), wglu_ref[...], preferred_element_type=F32) + bglu_ref[...]
        out_ref[...] = (g * _sigmoid(zz)).astype(BF16)

    const = lambda shape: pl.BlockSpec(shape, lambda i: (0,) * len(shape))
    row_spec = pl.BlockSpec((t, MIX), lambda i: (i, 0))
    return pl.pallas_call(
        body, name=name, grid=(nb,),
        in_specs=[row_spec, const((S5_SLABS, 128, SLAB_COLS)), const((S5_SLABS, SLAB_COLS, 128)), const((1, MIX)),
                  const((MIX, MIX)), const((1, MIX)), const((PTAB_ROWS, S5_N))],
        out_specs=[row_spec, row_spec, pl.BlockSpec((1, 1, 2 * S5_N), lambda i: (i, 0, 0))],
        out_shape=[jax.ShapeDtypeStruct((m, MIX), BF16), jax.ShapeDtypeStruct((m, MIX), F32),
                   jax.ShapeDtypeStruct((nb, 1, 2 * S5_N), F32)],
        scratch_shapes=[pltpu.VMEM((N_TILES, t, STRIP), F32), pltpu.VMEM((1, 2 * S5_N), F32)],
        compiler_params=_cparams(("arbitrary",), VMEM_BIG),
    )(h, bmat, cmat, dvec, wglu, bglu, ptab)


def _s5_bwd(dcat, ypre, h, hb, bmat, cmat, dvec, wglu, bglu, ptab, *, name, exchange=None):
    m = h.shape[0]
    t = T_S5
    nb = m // t
    ex_ops, ex_in_specs, ex_out_specs, ex_out_shapes, ex_scratch = _host_parts(exchange)
    n_ex = len(ex_ops)

    def body(*refs):
        dya_ref, y_ref, u_ref, hb_ref, bmat_ref, cmat_ref, d_ref, wglu_ref, bglu_ref, ptab_ref = refs[:10]
        du_ref, xb_ref, gb_ref, gq_ref, dzz_ref, dyq_ref, dlam_ref, dbglu_ref, dd_ref = refs[10 + n_ex:19 + n_ex]
        hosted_out = refs[19 + n_ex:19 + n_ex + len(ex_out_shapes)]
        bu_ref, dx_ref, gcarry_ref = refs[19 + n_ex + len(ex_out_shapes):22 + n_ex + len(ex_out_shapes)]
        _host_run(exchange, list(refs[10:10 + n_ex]) + list(hosted_out) + list(refs[22 + n_ex + len(ex_out_shapes):]),
                  pl.program_id(0) == 0, pl.program_id(0) == nb - 1)

        @pl.when(pl.program_id(0) == 0)
        def _():
            gcarry_ref[...] = jnp.zeros_like(gcarry_ref)
            dlam_ref[...] = jnp.zeros_like(dlam_ref)
            dbglu_ref[...] = jnp.zeros_like(dbglu_ref)
            dd_ref[...] = jnp.zeros_like(dd_ref)

        u = u_ref[...]
        y = y_ref[...]
        g = _gelu(y)
        gq = g.astype(BF16)
        sg = _sigmoid(jnp.dot(gq, wglu_ref[...], preferred_element_type=F32) + bglu_ref[...])
        dout = dya_ref[...]
        dzz = dout * g * sg * (1.0 - sg)
        dzzq = dzz.astype(BF16)
        dg = dout * sg + lax.dot_general(dzzq, wglu_ref[...], (((1,), (1,)), ((), ())), preferred_element_type=F32)
        dy = dg * _gelu_grad(y)
        dyq = dy.astype(BF16)
        gq_ref[...] = gq
        dzz_ref[...] = dzzq
        dyq_ref[...] = dyq
        dbglu_ref[...] += jnp.sum(dzz, axis=0, keepdims=True)
        dd_ref[...] += jnp.sum(dy * u, axis=0, keepdims=True)

        ub = u.astype(BF16)
        nt = (((1,), (1,)), ((), ()))
        for s in range(S5_SLABS):
            _store_tiles(dx_ref, SLAB_TILES * s,
                         lax.dot_general(dyq[:, 128 * s:128 * (s + 1)], cmat_ref[s], nt, preferred_element_type=F32))
            _store_tiles(bu_ref, SLAB_TILES * s,
                         jnp.dot(ub[:, 128 * s:128 * (s + 1)], bmat_ref[s], preferred_element_type=F32))
        grow = lax.broadcasted_iota(jnp.int32, (GROUPS, STRIP), 0)

        def strip(j, c):
            off, tr, ti = _strip_tiles(j)
            cols_r, cols_i = pl.ds(pl.multiple_of(tr * STRIP, STRIP), STRIP), pl.ds(pl.multiple_of(ti * STRIP, STRIP), STRIP)
            x_r, x_i = bu_ref.at[tr], bu_ref.at[ti]
            hr = hb_ref[0, 0:1, cols_r]
            hi = hb_ref[0, 0:1, cols_i]
            _scan_cols((x_r, x_i), hr, hi, ptab_ref, off, True)
            xb_ref[:, cols_r] = x_r[...].astype(BF16)
            xb_ref[:, cols_i] = x_i[...].astype(BF16)
            sums = [jnp.zeros((1, STRIP), F32), jnp.zeros((1, STRIP), F32)]

            def d_lam(r, gr, gi):
                if r == 0:
                    pr_ = jnp.where(grow == 0, hr, pltpu.roll(x_r[pl.ds(7, GROUPS, stride=8), :], 1, 0))
                    pi_ = jnp.where(grow == 0, hi, pltpu.roll(x_i[pl.ds(7, GROUPS, stride=8), :], 1, 0))
                else:
                    pr_ = x_r[pl.ds(r - 1, GROUPS, stride=8), :]
                    pi_ = x_i[pl.ds(r - 1, GROUPS, stride=8), :]
                sums[0] = sums[0] + jnp.sum(pr_ * gr + pi_ * gi, axis=0, keepdims=True)
                sums[1] = sums[1] + jnp.sum(pr_ * gi - pi_ * gr, axis=0, keepdims=True)

            g_r, g_i = dx_ref.at[tr], dx_ref.at[ti]
            gr0, gi0 = _scan_cols((g_r, g_i), gcarry_ref[0:1, cols_r], gcarry_ref[0:1, cols_i], ptab_ref, off, False, d_lam)
            gb_ref[:, cols_r] = g_r[...].astype(BF16)
            gb_ref[:, cols_i] = g_i[...].astype(BF16)
            gcarry_ref[0:1, cols_r] = gr0
            gcarry_ref[0:1, cols_i] = gi0
            dlam_ref[0:1, pl.ds(off, STRIP)] += sums[0]
            dlam_ref[1:2, pl.ds(off, STRIP)] += sums[1]
            return c

        lax.fori_loop(0, S5_N // STRIP, strip, 0)
        du_ref[...] = dy * d_ref[...] + jnp.concatenate(
            [lax.dot_general(gb_ref[:, SLAB_COLS * s:SLAB_COLS * (s + 1)], bmat_ref[s], nt, preferred_element_type=F32)
             for s in range(S5_SLABS)], axis=1)

    const = lambda shape: pl.BlockSpec(shape, lambda i: (0,) * len(shape))
    rev = lambda i: (nb - 1 - i, 0)
    row_spec = pl.BlockSpec((t, MIX), rev)
    wide = pl.BlockSpec((t, 2 * S5_N), rev)
    res = pl.pallas_call(
        body, name=name, grid=(nb,),
        in_specs=[row_spec, row_spec, row_spec, pl.BlockSpec((1, 1, 2 * S5_N), lambda i: (nb - 1 - i, 0, 0)),
                  const((S5_SLABS, 128, SLAB_COLS)), const((S5_SLABS, SLAB_COLS, 128)), const((1, MIX)), const((MIX, MIX)),
                  const((1, MIX)), const((PTAB_ROWS, S5_N))] + ex_in_specs,
        out_specs=[row_spec, wide, wide, row_spec, row_spec, row_spec, const((2, S5_N)), const((1, MIX)), const((1, MIX))]
        + ex_out_specs,
        out_shape=[jax.ShapeDtypeStruct((m, MIX), F32), jax.ShapeDtypeStruct((m, 2 * S5_N), BF16),
                   jax.ShapeDtypeStruct((m, 2 * S5_N), BF16), jax.ShapeDtypeStruct((m, MIX), BF16),
                   jax.ShapeDtypeStruct((m, MIX), BF16), jax.ShapeDtypeStruct((m, MIX), BF16),
                   jax.ShapeDtypeStruct((2, S5_N), F32), jax.ShapeDtypeStruct((1, MIX), F32), jax.ShapeDtypeStruct((1, MIX), F32)]
        + ex_out_shapes,
        scratch_shapes=[pltpu.VMEM((N_TILES, t, STRIP), F32), pltpu.VMEM((N_TILES, t, STRIP), F32),
                        pltpu.VMEM((1, 2 * S5_N), F32)] + ex_scratch,
        compiler_params=_cparams(("arbitrary",), VMEM_BIG),
    )(dcat, ypre, h, hb, bmat, cmat, dvec, wglu, bglu, ptab, *ex_ops)
    return res[:9] if exchange is None else (res[:9], list(res[9:]))


HALO = 8


def _taps_down(zext, t):
    return pltpu.roll(zext, 1, 0)[HALO:HALO + t], pltpu.roll(zext, 2, 0)[HALO:HALO + t]


def _conv_z(c_ref, x_ref, cp_ref, xp_ref, first, t):
    z = c_ref[...] * x_ref[...]
    zp = jnp.where(first, 0.0, cp_ref[t - HALO:t, :] * xp_ref[t - HALO:t, :])
    z1, z2 = _taps_down(jnp.concatenate([zp, z], axis=0), t)
    return z, z1, z2


def _conv_fwd(h, cw, *, name):
    m = h.shape[0]
    t = TM
    nb = m // t

    def body(b_ref, c_ref, x_ref, cp_ref, xp_ref, w_ref, o_ref):
        z, z1, z2 = _conv_z(c_ref, x_ref, cp_ref, xp_ref, pl.program_id(0) == 0, t)
        o_ref[...] = (b_ref[...] * (w_ref[0:1, :] * z2 + w_ref[1:2, :] * z1 + w_ref[2:3, :] * z)).astype(BF16)

    cur = lambda cb: pl.BlockSpec((t, MIX), lambda i: (i, cb))
    prev = lambda cb: pl.BlockSpec((t, MIX), lambda i: (jnp.maximum(i - 1, 0), cb))
    return pl.pallas_call(
        body, name=name, grid=(nb,),
        in_specs=[cur(1), cur(2), cur(3), prev(2), prev(3), pl.BlockSpec((3, MIX), lambda i: (0, 0))],
        out_specs=pl.BlockSpec((t, MIX), lambda i: (i, 0)),
        out_shape=jax.ShapeDtypeStruct((m, MIX), BF16),
        compiler_params=_cparams(("parallel",)),
    )(h, h, h, h, h, cw)


def _conv_bwd(dcat, h, cw, *, name):
    m = h.shape[0]
    t = TM
    nb = m // t

    def body(dy_ref, dyn_ref, b_ref, c_ref, x_ref, cp_ref, xp_ref, bn_ref, w_ref, o_ref, dw_ref):
        i = pl.program_id(0)

        @pl.when(i == 0)
        def _():
            dw_ref[...] = jnp.zeros_like(dw_ref)

        z, z1, z2 = _conv_z(c_ref, x_ref, cp_ref, xp_ref, i == 0, t)
        w0, w1, w2 = w_ref[0:1, :], w_ref[1:2, :], w_ref[2:3, :]
        dy = dy_ref[...]
        dconv = dy * b_ref[...]
        dnext = jnp.where(i == nb - 1, 0.0, dyn_ref[0:HALO, :] * bn_ref[0:HALO, :])
        dext = jnp.concatenate([dconv, dnext], axis=0)
        d1 = pltpu.roll(dext, t + HALO - 1, 0)[0:t]
        d2 = pltpu.roll(dext, t + HALO - 2, 0)[0:t]
        dz = w2 * dconv + w1 * d1 + w0 * d2
        o_ref[:, 0:MIX] = dy * (w0 * z2 + w1 * z1 + w2 * z)
        o_ref[:, MIX:2 * MIX] = dz * x_ref[...]
        o_ref[:, 2 * MIX:3 * MIX] = dz * c_ref[...]
        dw_ref[0:1, :] += jnp.sum(dconv * z2, axis=0, keepdims=True)
        dw_ref[1:2, :] += jnp.sum(dconv * z1, axis=0, keepdims=True)
        dw_ref[2:3, :] += jnp.sum(dconv * z, axis=0, keepdims=True)

    cur = lambda cb: pl.BlockSpec((t, MIX), lambda i: (i, cb))
    prev = lambda cb: pl.BlockSpec((t, MIX), lambda i: (jnp.maximum(i - 1, 0), cb))
    nxt = lambda cb: pl.BlockSpec((t, MIX), lambda i: (jnp.minimum(i + 1, nb - 1), cb))
    return pl.pallas_call(
        body, name=name, grid=(nb,),
        in_specs=[cur(1), nxt(1), cur(1), cur(2), cur(3), prev(2), prev(3), nxt(1), pl.BlockSpec((3, MIX), lambda i: (0, 0))],
        out_specs=[pl.BlockSpec((t, 3 * MIX), lambda i: (i, 0)), pl.BlockSpec((8, MIX), lambda i: (0, 0))],
        out_shape=[jax.ShapeDtypeStruct((m, 3 * MIX), F32), jax.ShapeDtypeStruct((8, MIX), F32)],
        compiler_params=_cparams(("arbitrary",)),
    )(dcat, dcat, h, h, h, h, h, h, cw)


PHALO = 16


def _pool_pooled(z_ref, zp_ref, i, t):
    z = z_ref[...]
    zp = jnp.where(i == 0, 0.0, zp_ref[t - PHALO:t, :])
    s = jnp.concatenate([zp, z], axis=0)
    sums = {}
    width = 1
    while width < PHALO:
        s = s + pltpu.roll(s, width, 0)
        width *= 2
        sums[width] = s[PHALO:PHALO + t]
    tpos = i * t + lax.broadcasted_iota(jnp.int32, (t, 1), 0)
    outs = []
    for gi, w in enumerate(POOL_WINDOWS):
        lo = gi * POOL_GROUP
        count = jnp.minimum(tpos + 1, w).astype(F32)
        outs.append(sums[w][:, lo:lo + POOL_GROUP] / count - z[:, lo:lo + POOL_GROUP])
    return outs


def _pool_fwd(h, pw, ps, *, name):
    m = h.shape[0]
    t = TM
    nb = m // t

    def body(z_ref, zp_ref, pw_ref, ps_ref, o_ref):
        pooled = _pool_pooled(z_ref, zp_ref, pl.program_id(0), t)
        for gi in range(len(POOL_WINDOWS)):
            lo = gi * POOL_GROUP
            mixed = jnp.dot(pooled[gi].astype(BF16), pw_ref[gi], preferred_element_type=F32)
            o_ref[:, lo:lo + POOL_GROUP] = (mixed * ps_ref[:, lo:lo + POOL_GROUP]).astype(BF16)

    return pl.pallas_call(
        body, name=name, grid=(nb,),
        in_specs=[pl.BlockSpec((t, MIX), lambda i: (i, 3)), pl.BlockSpec((t, MIX), lambda i: (jnp.maximum(i - 1, 0), 3)),
                  pl.BlockSpec((4, POOL_GROUP, POOL_GROUP), lambda i: (0, 0, 0)), pl.BlockSpec((1, MIX), lambda i: (0, 0))],
        out_specs=pl.BlockSpec((t, MIX), lambda i: (i, 0)),
        out_shape=jax.ShapeDtypeStruct((m, MIX), BF16),
        compiler_params=_cparams(("parallel",)),
    )(h, h, pw, ps)


def _pool_bwd(dcat, h, pw, ps, *, name):
    m = h.shape[0]
    t = TM
    nb = m // t

    def body(dy_ref, dyn_ref, z_ref, zp_ref, pw_ref, ps_ref, dz_ref, dpw_ref, dps_ref):
        i = pl.program_id(0)

        @pl.when(i == 0)
        def _():
            dpw_ref[...] = jnp.zeros_like(dpw_ref)
            dps_ref[...] = jnp.zeros_like(dps_ref)

        pooled = _pool_pooled(z_ref, zp_ref, i, t)
        dy = dy_ref[...]
        tpos = i * t + lax.broadcasted_iota(jnp.int32, (t, 1), 0)
        for gi, w in enumerate(POOL_WINDOWS):
            lo = gi * POOL_GROUP
            sl = slice(lo, lo + POOL_GROUP)
            pq = pooled[gi].astype(BF16)
            mixed = jnp.dot(pq, pw_ref[gi], preferred_element_type=F32)
            dps_ref[:, sl] += jnp.sum(dy[:, sl] * mixed, axis=0, keepdims=True)
            dmix = (dy[:, sl] * ps_ref[:, sl]).astype(BF16)
            dpw_ref[gi] += lax.dot_general(pq, dmix, (((0,), (0,)), ((), ())), preferred_element_type=F32)
            dpool = lax.dot_general(dmix, pw_ref[gi], (((1,), (1,)), ((), ())), preferred_element_type=F32)
            dmix_n = (dyn_ref[0:PHALO, sl] * ps_ref[:, sl]).astype(BF16)
            dpool_n = lax.dot_general(dmix_n, pw_ref[gi], (((1,), (1,)), ((), ())), preferred_element_type=F32)
            e = dpool / jnp.minimum(tpos + 1, w).astype(F32)
            e_n = jnp.where(i == nb - 1, 0.0, dpool_n * (1.0 / w))
            f = jnp.concatenate([e, e_n], axis=0)
            width = 1
            while width < w:
                f = f + pltpu.roll(f, t + PHALO - width, 0)
                width *= 2
            dz_ref[:, sl] = f[0:t] - dpool

    return pl.pallas_call(
        body, name=name, grid=(nb,),
        in_specs=[pl.BlockSpec((t, MIX), lambda i: (i, 1)), pl.BlockSpec((t, MIX), lambda i: (jnp.minimum(i + 1, nb - 1), 1)),
                  pl.BlockSpec((t, MIX), lambda i: (i, 3)), pl.BlockSpec((t, MIX), lambda i: (jnp.maximum(i - 1, 0), 3)),
                  pl.BlockSpec((4, POOL_GROUP, POOL_GROUP), lambda i: (0, 0, 0)), pl.BlockSpec((1, MIX), lambda i: (0, 0))],
        out_specs=[pl.BlockSpec((t, MIX), lambda i: (i, 0)), pl.BlockSpec((4, POOL_GROUP, POOL_GROUP), lambda i: (0, 0, 0)),
                   pl.BlockSpec((1, MIX), lambda i: (0, 0))],
        out_shape=[jax.ShapeDtypeStruct((m, MIX), F32), jax.ShapeDtypeStruct((4, POOL_GROUP, POOL_GROUP), F32),
                   jax.ShapeDtypeStruct((1, MIX), F32)],
        compiler_params=_cparams(("arbitrary",)),
    )(dcat, dcat, h, h, pw, ps)


NKEY = 2 * T_ATT


def _band_mask():
    qc = np.arange(T_ATT)[:, None] // CHUNK
    kc = np.arange(NKEY)[None, :] // CHUNK - LEFT_CHUNKS
    return np.where((kc <= qc) & (kc >= qc - LEFT_CHUNKS), 0.0, NEG_INF).astype(np.float32)


def _diag_index():
    c = np.arange(NKEY)
    d = np.where(c <= NKEY // 2 + CHUNK, T_ATT - c, T_ATT + NKEY - c)
    return np.clip(d, -MAX_REL, MAX_REL) + MAX_REL


def _bias_tile(vd_ref, mask_ref, tile_ref):
    col = lax.broadcasted_iota(jnp.int32, (8, NKEY), 1)
    no_prev = jnp.where(col < T_ATT, NEG_INF, 0.0)
    for hh in range(2):
        v = vd_ref[0, hh:hh + 1, :]
        base = jnp.concatenate([v if s == 0 else pltpu.roll(v, s, 1) for s in range(8)], axis=0)
        for mrow in range(T_ATT // 8):
            rows = slice(8 * mrow, 8 * mrow + 8)
            blk = (base if mrow == 0 else pltpu.roll(base, 8 * mrow, 1)) + mask_ref[rows, :]
            tile_ref[hh, rows, :] = blk
            tile_ref[2 + hh, rows, :] = blk + no_prev


BAND_ROWS = 2 * CHUNK
BAND_COLS = (LEFT_CHUNKS + 2) * CHUNK
N_BANDS = T_ATT // BAND_ROWS


def _band(x, r):
    return x[BAND_ROWS * r:BAND_ROWS * (r + 1), BAND_ROWS * r:BAND_ROWS * r + BAND_COLS]


def _from_bands(parts):
    rows = []
    for r, part in enumerate(parts):
        right = NKEY - BAND_COLS - BAND_ROWS * r
        pieces = ([jnp.zeros((BAND_ROWS, BAND_ROWS * r), part.dtype)] if r else []) + [part]
        pieces += [jnp.zeros((BAND_ROWS, right), part.dtype)] if right else []
        rows.append(jnp.concatenate(pieces, axis=1))
    return jnp.concatenate(rows, axis=0)


def _attn_probs(q, kc, tile_ref, idx):
    s = lax.dot_general(q, kc, (((1,), (1,)), ((), ())), preferred_element_type=F32)
    parts = []
    for r in range(N_BANDS):
        sb = _band(s, r) + tile_ref[idx, BAND_ROWS * r:BAND_ROWS * (r + 1), BAND_ROWS * r:BAND_ROWS * r + BAND_COLS]
        p = jnp.exp(sb - jnp.max(sb, axis=-1, keepdims=True))
        parts.append(p * (1.0 / jnp.sum(p, axis=-1, keepdims=True)))
    return parts


def _attn_specs(block):
    cur = lambda base: pl.BlockSpec((T_ATT, 128), lambda hp, i: (block(i), base + hp))
    prev = lambda base: pl.BlockSpec((T_ATT, 128), lambda hp, i: (jnp.maximum(block(i) - 1, 0), base + hp))
    return [cur(0), cur(4), prev(4), cur(8), prev(8),
            pl.BlockSpec((1, 2, NKEY), lambda hp, i: (hp, 0, 0)), pl.BlockSpec((T_ATT, NKEY), lambda hp, i: (0, 0))]


def _attn_fwd(h, vdiag, mask, *, name):
    m = h.shape[0]
    nb = m // T_ATT

    def body(q_ref, k_ref, kp_ref, v_ref, vp_ref, vd_ref, mask_ref, o_ref, tile_ref):
        i = pl.program_id(1)

        @pl.when(i == 0)
        def _():
            _bias_tile(vd_ref, mask_ref, tile_ref)

        first = jnp.where(i == 0, 2, 0)
        outs = []
        for hh in range(2):
            sl = slice(hh * HEAD_DIM, (hh + 1) * HEAD_DIM)
            q = (q_ref[:, sl] * (HEAD_DIM ** -0.5)).astype(BF16)
            kc = jnp.concatenate([kp_ref[:, sl], k_ref[:, sl]], axis=0).astype(BF16)
            vc = jnp.concatenate([vp_ref[:, sl], v_ref[:, sl]], axis=0).astype(BF16)
            p = _from_bands([b.astype(BF16) for b in _attn_probs(q, kc, tile_ref, first + hh)])
            outs.append(jnp.dot(p, vc, preferred_element_type=F32))
        o_ref[...] = jnp.concatenate(outs, axis=1).astype(BF16)

    return pl.pallas_call(
        body, name=name, grid=(ATT_HEADS // 2, nb), in_specs=_attn_specs(lambda i: i),
        out_specs=pl.BlockSpec((T_ATT, 128), lambda hp, i: (i, hp)),
        out_shape=jax.ShapeDtypeStruct((m, MIX), BF16),
        scratch_shapes=[pltpu.VMEM((4, T_ATT, NKEY), F32)],
        compiler_params=_cparams(("parallel", "arbitrary"), VMEM_BIG),
    )(h, h, h, h, h, vdiag, mask)


def _attn_bwd(dcat, h, vdiag, mask, *, name):
    m = h.shape[0]
    nb = m // T_ATT

    def body(do_ref, q_ref, k_ref, kp_ref, v_ref, vp_ref, vd_ref, mask_ref,
             dq_ref, dk_ref, dv_ref, dvd_ref, tile_ref, acc_ref, carry_ref):
        i = pl.program_id(1)

        @pl.when(i == 0)
        def _():
            _bias_tile(vd_ref, mask_ref, tile_ref)
            acc_ref[...] = jnp.zeros_like(acc_ref)

            carry_ref[...] = jnp.zeros_like(carry_ref)

        scale = HEAD_DIM ** -0.5
        first = jnp.where(i == nb - 1, 2, 0)
        dqs, dks, dvs = [], [], []
        for hh in range(2):
            sl = slice(hh * HEAD_DIM, (hh + 1) * HEAD_DIM)
            q = (q_ref[:, sl] * scale).astype(BF16)
            kc = jnp.concatenate([kp_ref[:, sl], k_ref[:, sl]], axis=0).astype(BF16)
            vc = jnp.concatenate([vp_ref[:, sl], v_ref[:, sl]], axis=0).astype(BF16)
            do = do_ref[:, sl].astype(BF16)
            bands = _attn_probs(q, kc, tile_ref, first + hh)
            p = _from_bands([b.astype(BF16) for b in bands])
            dvs.append(lax.dot_general(p, do, (((0,), (0,)), ((), ())), preferred_element_type=F32))
            dp = lax.dot_general(do, vc, (((1,), (1,)), ((), ())), preferred_element_type=F32)
            ds_bands = []
            for r, pb in enumerate(bands):
                dpb = _band(dp, r)
                dsb = pb * (dpb - jnp.sum(dpb * pb, axis=-1, keepdims=True))
                acc_ref[hh, BAND_ROWS * r:BAND_ROWS * (r + 1), BAND_ROWS * r:BAND_ROWS * r + BAND_COLS] += dsb
                ds_bands.append(dsb.astype(BF16))
            dsq = _from_bands(ds_bands)
            dqs.append(jnp.dot(dsq, kc, preferred_element_type=F32) * scale)
            dks.append(lax.dot_general(dsq, q, (((0,), (0,)), ((), ())), preferred_element_type=F32))
        dq_ref[...] = jnp.concatenate(dqs, axis=1)
        dk = jnp.concatenate(dks, axis=1)
        dv = jnp.concatenate(dvs, axis=1)
        dk_ref[...] = dk[T_ATT:] + carry_ref[0]
        dv_ref[...] = dv[T_ATT:] + carry_ref[1]
        carry_ref[0] = dk[:T_ATT]
        carry_ref[1] = dv[:T_ATT]

        @pl.when(i == nb - 1)
        def _():
            for hh in range(2):
                r8 = acc_ref[hh, 0:8, :]
                for mrow in range(1, T_ATT // 8):
                    r8 = r8 + pltpu.roll(acc_ref[hh, 8 * mrow:8 * mrow + 8, :], NKEY - 8 * mrow, 1)
                tot = r8[0:1, :]
                for s in range(1, 8):
                    tot = tot + pltpu.roll(r8[s:s + 1, :], NKEY - s, 1)
                dvd_ref[0, hh:hh + 1, :] = tot

    block = lambda i: nb - 1 - i
    out = pl.BlockSpec((T_ATT, 128), lambda hp, i: (block(i), hp))
    return pl.pallas_call(
        body, name=name, grid=(ATT_HEADS // 2, nb),
        in_specs=[out] + _attn_specs(block),
        out_specs=[out, out, out, pl.BlockSpec((1, 2, NKEY), lambda hp, i: (hp, 0, 0))],
        out_shape=[jax.ShapeDtypeStruct((m, MIX), F32)] * 3 + [jax.ShapeDtypeStruct((ATT_HEADS // 2, 2, NKEY), F32)],
        scratch_shapes=[pltpu.VMEM((4, T_ATT, NKEY), F32), pltpu.VMEM((2, T_ATT, NKEY), F32), pltpu.VMEM((2, T_ATT, 128), F32)],
        compiler_params=_cparams(("parallel", "arbitrary"), VMEM_BIG),
    )(dcat, h, h, h, h, h, vdiag, mask)


def _row_tile(rows):
    for t in (512, 256, 128, 64, 32, 16, 8):
        if rows % t == 0:
            return t
    return rows


def _adamw(w, g, mom, var, *, name):
    rows, cols = w.shape
    t = _row_tile(rows)

    def body(w_ref, g_ref, m_ref, v_ref, d_ref, mo_ref, vo_ref):
        g_ = g_ref[...]
        m_ = ADAM_B1 * m_ref[...] + (1.0 - ADAM_B1) * g_
        v_ = ADAM_B2 * v_ref[...] + (1.0 - ADAM_B2) * (g_ * g_)
        m_hat = m_ / (1.0 - ADAM_B1 ** ADAM_STEP)
        v_hat = v_ / (1.0 - ADAM_B2 ** ADAM_STEP)
        d_ref[...] = -ADAM_LR * (m_hat / (jnp.sqrt(v_hat) + ADAM_EPS) + ADAM_WD * w_ref[...])
        mo_ref[...] = m_
        vo_ref[...] = v_

    spec = pl.BlockSpec((t, cols), lambda i: (i, 0))
    return pl.pallas_call(
        body, name=name, grid=(rows // t,), in_specs=[spec] * 4, out_specs=[spec] * 3,
        out_shape=[jax.ShapeDtypeStruct((rows, cols), F32)] * 3, compiler_params=_cparams(("parallel",)),
    )(w, g, mom, var)


ANY = pl.BlockSpec(memory_space=pl.ANY)


def _place():
    x, y, c = lax.axis_index("x"), lax.axis_index("y"), lax.axis_index("c")
    chips = [(1 - x, y), (x, 1 - y), (1 - x, 1 - y)]
    return x, y, c, chips


class _GatherExchange:
    def __init__(self, ws):
        n = len(ws)
        self.ins = list(ws)
        self.out_shapes = [jax.ShapeDtypeStruct((N_CHIPS,) + w.shape, w.dtype) for w in ws]
        self.sems = [pltpu.SemaphoreType.DMA((6 * n,)), pltpu.SemaphoreType.DMA((6 * n,))]

    def _copies(self, ins, outs, sems, onward=True):
        send_sems, recv_sems = sems
        x, y, c, chips = _place()
        me = 2 * x + y

        def region(k, j, chip_index, rows, to):
            ref = outs[k].at[chip_index, rows]
            return pltpu.make_async_remote_copy(
                src_ref=ref, dst_ref=ref, send_sem=send_sems.at[6 * k + j], recv_sem=recv_sems.at[6 * k + j],
                device_id=to, device_id_type=MESH)

        first, landed, passed, handed = [], [], [], []
        for k in range(len(ins)):
            half = ins[k].shape[0] // 2
            mine, theirs = pl.ds(c * half, half), pl.ds((1 - c) * half, half)
            for j, chip in enumerate(chips):
                first.append(pltpu.make_async_remote_copy(
                    src_ref=ins[k].at[mine], dst_ref=outs[k].at[me, mine], send_sem=send_sems.at[6 * k + j],
                    recv_sem=recv_sems.at[6 * k + j], device_id=(*chip, c), device_id_type=MESH))
                if onward:
                    landed.append(region(k, j, 2 * chip[0] + chip[1], mine, (*chip, c)))
                    passed.append(region(k, 3 + j, 2 * chip[0] + chip[1], mine, (x, y, 1 - c)))
                    handed.append(region(k, 3 + j, 2 * chip[0] + chip[1], theirs, (x, y, 1 - c)))
        return first, landed, passed, handed

    def start(self, ins, outs, sems):
        for cp in self._copies(ins, outs, sems, onward=False)[0]:
            cp.start()

    def finish(self, ins, outs, sems):
        first, landed, passed, handed = self._copies(ins, outs, sems)
        for arrived, onward in zip(landed, passed):
            arrived.wait_recv()
            onward.start()
        for cp in handed:
            cp.wait_recv()
        for cp in first + passed:
            cp.wait_send()


class _ReduceExchange:
    def __init__(self, grads, axes):
        self.ins = list(grads)
        self.axes = list(axes)
        n = len(grads)
        self.out_shapes = [jax.ShapeDtypeStruct((N_DEV - 1,) + self._block(g, a), g.dtype) for g, a in zip(grads, axes)]
        self.sems = [pltpu.SemaphoreType.DMA((7 * n,)), pltpu.SemaphoreType.DMA((7 * n,))]

    @staticmethod
    def _block(g, axis):
        k, n = g.shape
        return (k // 2, n // N_CHIPS) if axis == 2 else (k // N_DEV, n)

    def _copies(self, ins, outs, sems):
        send_sems, recv_sems = sems
        x, y, c, _ = _place()
        cps = []
        for w, (g, axis) in enumerate(zip(ins, self.axes)):
            rows, cols = self._block(g, axis)
            for k in range(1, N_DEV):
                tx, ty, tc = (1 - x if k & 4 else x), (1 - y if k & 2 else y), (1 - c if k & 1 else c)
                chip = 2 * tx + ty
                if axis == 2:
                    src = g.at[pl.ds(tc * rows, rows), pl.ds(chip * cols, cols)]
                else:
                    src = g.at[pl.ds((2 * chip + tc) * rows, rows), :]
                cps.append(pltpu.make_async_remote_copy(
                    src_ref=src, dst_ref=outs[w].at[k - 1], send_sem=send_sems.at[7 * w + k - 1],
                    recv_sem=recv_sems.at[7 * w + k - 1], device_id=(tx, ty, tc), device_id_type=MESH))
        return cps

    def start(self, ins, outs, sems):
        for cp in self._copies(ins, outs, sems):
            cp.start()

    def finish(self, ins, outs, sems):
        cps = self._copies(ins, outs, sems)
        for cp in cps:
            cp.wait_recv()
        for cp in cps:
            cp.wait_send()


def _run_exchange(ex, *, name):
    n_in, n_out = len(ex.ins), len(ex.out_shapes)

    def body(*refs):
        ins, outs, sems = refs[:n_in], refs[n_in:n_in + n_out], refs[n_in + n_out:]
        ex.start(ins, outs, sems)
        ex.finish(ins, outs, sems)

    return pl.pallas_call(body, name=name, in_specs=[ANY] * n_in, out_specs=[ANY] * n_out, out_shape=ex.out_shapes,
                          scratch_shapes=ex.sems)(*ex.ins)


def _all_reduce_small(buf, *, name):
    rows = buf.shape[0]

    def body(x_ref, sum_ref, all_ref, send_sems, recv_sems, local_sem):
        x, y, c, chips = _place()
        me, sibling = (x, y, c), (x, y, 1 - c)

        def slab(px, py, pc):
            return all_ref.at[pl.ds((4 * px + 2 * py + pc) * rows, rows), :]

        def copy(k, block, to, src=None):
            return pltpu.make_async_remote_copy(
                src_ref=slab(*block) if src is None else src, dst_ref=slab(*block), send_sem=send_sems.at[k],
                recv_sem=recv_sems.at[k], device_id=to, device_id_type=MESH)

        mine = pltpu.make_async_copy(x_ref, slab(*me), local_sem)
        mine.start()
        first = [copy(0, me, sibling, src=x_ref)]
        first += [copy(1 + j, me, (*chip, c), src=x_ref) for j, chip in enumerate(chips)]
        for cp in first:
            cp.start()
        passed = [copy(4 + j, (*chip, c), sibling) for j, chip in enumerate(chips)]
        for j, chip in enumerate(chips):
            copy(1 + j, (*chip, c), me).wait_recv()
            passed[j].start()
        copy(0, sibling, me).wait_recv()
        for j, chip in enumerate(chips):
            copy(4 + j, (*chip, 1 - c), me).wait_recv()
        for cp in first + passed:
            cp.wait_send()
        mine.wait()
        acc = all_ref[0:rows, :]
        for d in range(1, N_DEV):
            acc = acc + all_ref[d * rows:(d + 1) * rows, :]
        sum_ref[...] = acc

    vmem = pl.BlockSpec(memory_space=pltpu.VMEM)
    return pl.pallas_call(
        body, name=name, in_specs=[vmem], out_specs=[vmem, vmem],
        out_shape=[jax.ShapeDtypeStruct((rows, 128), F32), jax.ShapeDtypeStruct((N_DEV * rows, 128), F32)],
        scratch_shapes=[pltpu.SemaphoreType.DMA((7,)), pltpu.SemaphoreType.DMA((7,)), pltpu.SemaphoreType.DMA],
        compiler_params=pltpu.CompilerParams(vmem_limit_bytes=VMEM_BIG),
    )(buf)[0]


WEIGHTS = ['ev_w_in', 'ev_lambda_re', 'ev_lambda_im', 'ev_log_dt', 'ev_b_re', 'ev_b_im', 'ev_c_re', 'ev_c_im', 'ev_d',
           'ev_w_glu', 'ev_b_glu', 'ev_conv_w', 'ev_w_out', 'od_w_in', 'od_rel_bias', 'od_pool_w', 'od_pool_scale',
           'od_w_out', 'ln_mix_g', 'ln_mix_b', 'ln_ffn_g', 'ln_ffn_b', 'ffn_w_up', 'ffn_w_down', 'ple_w_proj',
           'ple_w_gate', 'ple_b_gate']
INPUTS = ['x', 'p'] + WEIGHTS + ['loss_target'] + ['m_' + n for n in WEIGHTS] + ['v_' + n for n in WEIGHTS]

BIG = {
    'ev_w_in': (2, (2, 1024, 2048)), 'ev_w_glu': (1, (2, 512, 512)), 'ev_w_out': (1, (2, 1024, 1024)),
    'od_w_in': (2, (2, 1024, 2048)), 'od_w_out': (1, (2, 1024, 1024)), 'ffn_w_up': (2, (4, 1024, 5632)),
    'ffn_w_down': (1, (4, 2816, 1024)), 'ple_w_proj': (2, (4, 256, 1024)), 'ple_w_gate': (1, (4, 1024, 1024)),
}
SMALL_SHARDED = {'ev_conv_w': (2, 3, 512), 'od_pool_scale': (2, 512)}
REPLICATED = [n for n in WEIGHTS if n not in BIG and n not in SMALL_SHARDED]


def _shard_rows(name):
    axis, (nl, k, n) = BIG[name]
    return (nl * k, n // N_CHIPS) if axis == 2 else (nl * k // N_CHIPS, n)


def _pack(arrs):
    flat = jnp.concatenate([a.reshape(-1) for a in arrs])
    total = flat.shape[0]
    padded = -(-total // 1024) * 1024
    return jnp.pad(flat, (0, padded - total)).reshape(padded // 128, 128)


def _unpack(buf, shapes):
    flat = buf.reshape(-1)
    out, pos = [], 0
    for s in shapes:
        size = int(np.prod(s))
        out.append(flat[pos:pos + size].reshape(s))
        pos += size
    return out


def _s5_params(lam_re, lam_im, log_dt, b_re, b_im, c_re, c_im):
    dt = jnp.exp(log_dt)[:, None]
    mag = jnp.exp(lam_re * dt)
    ang = lam_im * dt
    lb_re = mag * jnp.cos(ang)
    lb_im = mag * jnp.sin(ang)
    den = lam_re * lam_re + lam_im * lam_im
    nr = lb_re - 1.0
    ni = lb_im
    r_re = (nr * lam_re + ni * lam_im) / den
    r_im = (ni * lam_re - nr * lam_im) / den
    bb_re = r_re[..., None] * b_re - r_im[..., None] * b_im
    bb_im = r_re[..., None] * b_im + r_im[..., None] * b_re
    per = S5_GROUPS // S5_SLABS
    eye = jnp.eye(per, dtype=F32)

    def block_diag(a):
        _, r, c = a.shape
        a = a.reshape(S5_SLABS, per, r, c)
        return (a[:, :, :, None, :] * eye[None, :, None, :, None]).reshape(S5_SLABS, per * r, per * c)

    bmat = jnp.concatenate([block_diag(bb_re.transpose(0, 2, 1)), block_diag(bb_im.transpose(0, 2, 1))], axis=2)
    cmat = jnp.concatenate([block_diag(c_re.transpose(0, 2, 1)), block_diag(-c_im.transpose(0, 2, 1))], axis=1)
    lam = jnp.stack([lb_re.reshape(S5_N), lb_im.reshape(S5_N)])
    return lam, bmat, cmat


def _lam_powers(lam):
    res, ims = [lam[0]], [lam[1]]
    for _ in range(7):
        res, ims = res + [res[-1] * lam[0] - ims[-1] * lam[1]], ims + [res[-1] * lam[1] + ims[-1] * lam[0]]
    sq_r, sq_i = [res[-1]], [ims[-1]]
    for _ in range(N_SQUARES):
        sq_r, sq_i = sq_r + [sq_r[-1] * sq_r[-1] - sq_i[-1] * sq_i[-1]], sq_i + [2.0 * sq_r[-1] * sq_i[-1]]
    return jnp.stack(res + ims + res[::-1] + ims[::-1] + sq_r[1:] + sq_i[1:])


def _layer_big(i):
    mixer = [('w_in', 'ev_w_in'), ('w_glu', 'ev_w_glu'), ('w_out', 'ev_w_out')] if i % 2 == 0 else \
        [('w_in', 'od_w_in'), ('w_out', 'od_w_out')]
    ffn = [('w_up', 'ffn_w_up'), ('w_down', 'ffn_w_down'), ('w_proj', 'ple_w_proj'), ('w_gate', 'ple_w_gate')]
    return [(k, n, i // 2) for k, n in mixer] + [(k, n, i) for k, n in ffn]


class _WholePlan:
    def __init__(self, whole):
        self.whole = whole
        self.grads = {n: {} for n in BIG}

    def layer_weights(self, i):
        return {k: self.whole[n][l] for k, n, l in _layer_big(i)}

    def forward_host(self, i):
        return None

    def backward_host(self, i):
        return None

    def early_host(self, i, g):
        return None

    def layer_grads(self, i, g):
        for k, n, l in _layer_big(i):
            self.grads[n][l] = g[k][0]


def _local_step(x, p, target, w, plan):
    mask = jnp.asarray(_band_mask())
    diag_idx = _diag_index()
    onehot = jnp.asarray(np.eye(2 * MAX_REL + 1, dtype=np.float32)[diag_idx])
    saved = []
    for i in range(DEPTH):
        li = i // 2
        lw = plan.layer_weights(i)
        s = {'x0': x, 'lw': lw}
        h = _mm([(x, 0, D_MODEL)], lw['w_in'], tn=4 * MIX, name=f"in_proj")
        if i % 2 == 0:
            (lam, bmat, cmat), s5_vjp = jax.vjp(
                _s5_params, w['ev_lambda_re'][li], w['ev_lambda_im'][li], w['ev_log_dt'][li], w['ev_b_re'][li],
                w['ev_b_im'][li], w['ev_c_re'][li], w['ev_c_im'][li])
            s5c = (bmat.astype(BF16), cmat.astype(BF16), w['ev_d'][li].reshape(1, MIX), lw['w_glu'],
                   w['ev_b_glu'][li].reshape(1, MIX), _lam_powers(lam))
            ya, ypre, hb = _s5_fwd(h, *s5c, name=f"s5_fwd")
            yb = _conv_fwd(h, w['ev_conv_w'][li], name=f"conv_fwd")
            s.update(s5_vjp=s5_vjp, s5c=s5c, ypre=ypre, hb=hb)
        else:
            vdiag = jnp.dot(w['od_rel_bias'][li], onehot.T, precision=HIGHEST).reshape(ATT_HEADS // 2, 2, NKEY)
            pw = w['od_pool_w'][li].astype(BF16)
            ps = w['od_pool_scale'][li].reshape(1, MIX)
            ya = _attn_fwd(h, vdiag, mask, name=f"attn_fwd")
            yb = _pool_fwd(h, pw, ps, name=f"pool_fwd")
            s.update(vdiag=vdiag, pw=pw, ps=ps)
        wout = lw['w_out']
        vec = lambda n: w[n][i].reshape(1, -1)

        def residual_ln(products, rows, vecs):
            r = ALPHA * rows[0] + products[0]
            return (r, _ln_apply(r, vecs[0], vecs[1])), ()

        def embed_gate(products, rows, vecs):
            gate = _sigmoid(products[0] + vecs[0])
            return (rows[0] + gate * products[1], gate, products[1]), ()

        two_f32 = [(D_MODEL, F32), (D_MODEL, F32)]
        r1, x1 = _mm_rows([([(ya, 0, MIX), (yb, 0, MIX)], wout, False)], [x], [vec('ln_mix_g'), vec('ln_mix_b')],
                          two_f32, [], residual_ln, name="out_proj_ln")
        hosted = plan.forward_host(i)
        if hosted is None:
            a, gg, uu = _ffn_up(x1, lw['w_up'], name=f"ffn_up")
        else:
            (a, gg, uu), arrived = _ffn_up(x1, lw['w_up'], exchange=hosted, name=f"ffn_up_gather")
            plan.forward_hosted(i, arrived)
        r2, x2 = _mm_rows([([(a, 0, D_FF)], lw['w_down'], False)], [x1], [vec('ln_ffn_g'), vec('ln_ffn_b')],
                          two_f32, [], residual_ln, name="ffn_down_ln")
        x3, gate, ppb = _mm_rows(
            [([(None, 0, D_MODEL)], lw['w_gate'], False), ([(p[i], 0, D_PLE)], lw['w_proj'], False)],
            [x2], [vec('ple_b_gate')], [(D_MODEL, F32), (D_MODEL, BF16), (D_MODEL, BF16)], [], embed_gate, name="ple")
        s.update(h=h, ya=ya, yb=yb, r1=r1, x1=x1, a=a, gg=gg, uu=uu, r2=r2, x2=x2, gate=gate, ppb=ppb)
        saved.append(s)
        x = x3

    loss, da = _loss_head(x, target, name="loss_head")
    grads = {n: [None] * (DEPTH if n.startswith(('ln_', 'ple_')) else DEPTH // 2) for n in WEIGHTS if n not in BIG}

    def both(pieces, axis):
        return tuple(jnp.concatenate([pc[k] for pc in pieces], axis=axis) for k in range(2))

    for i in reversed(range(DEPTH)):
        li = i // 2
        s = saved[i]
        lw = s['lw']
        big = {}
        dz, dpp, dr2, dbg, dg2, db2 = _ple_ln_bwd(da, s['gate'], s['ppb'], s['r2'], lw['w_gate'],
                                                  w['ln_ffn_g'][i].reshape(1, -1), name="ple_ln_bwd")
        grads['ple_b_gate'][i] = dbg.reshape(-1)
        big['w_gate'] = _mm_tn(s['x2'], 0, D_MODEL, dz, also_bf16=True, name=f"d_ple_gate")
        big['w_proj'] = _mm_tn(p[i], 0, D_PLE, dpp, also_bf16=True, name=f"d_ple_proj")
        grads['ln_ffn_g'][i] = dg2.reshape(-1)
        grads['ln_ffn_b'][i] = db2.reshape(-1)
        dhh = _ffn_down_bwd(dr2, lw['w_down'], s['gg'], s['uu'], name=f"ffn_down_bwd")
        big['w_down'] = _mm_tn(s['a'], 0, D_FF, dr2, tk=D_FF // 2, also_bf16=True, name=f"d_ffn_down")
        hosted = plan.backward_host(i)
        if hosted is None:
            big['w_up'] = _mm_tn(s['x1'], 0, D_MODEL, dhh, tn=D_FF // 2, also_bf16=True, name=f"d_ffn_up")
        else:
            big['w_up'], arrived = _mm_tn(s['x1'], 0, D_MODEL, dhh, tn=D_FF // 2, also_bf16=True, exchange=hosted,
                                          name=f"d_ffn_up_reduce_{i % 2}")
            plan.backward_hosted(i, arrived)

        def ln_mix_grad(products, rows, vecs):
            dr, dg, dbias = _ln_grad(rows[0], ALPHA * rows[1] + products[0], vecs[0])
            return (dr,), (dg, dbias)

        dr1, dg1, db1 = _mm_rows([([(dhh, 0, 2 * D_FF)], lw['w_up'], True)], [s['r1'], dr2],
                                 [w['ln_mix_g'][i].reshape(1, -1)], [(D_MODEL, F32)], [D_MODEL, D_MODEL], ln_mix_grad,
                                 tm=TM, vmem=VMEM_BIG, name="ffn_up_ln_bwd")
        grads['ln_mix_g'][i] = dg1.reshape(-1)
        grads['ln_mix_b'][i] = db1.reshape(-1)
        dcat = _mm([(dr1, 0, D_MODEL)], lw['w_out'], trans_b=True, name=f"out_proj_bwd")
        big['w_out'] = both([_mm_tn(s['ya'], 0, MIX, dr1, also_bf16=True, name=f"d_out_a"),
                             _mm_tn(s['yb'], 0, MIX, dr1, also_bf16=True, name=f"d_out_b")], 0)
        h = s['h']
        if i % 2 == 0:
            s5c = s['s5c']
            hosted = plan.early_host(i, big)
            if hosted is None:
                s5_out = _s5_bwd(dcat, s['ypre'], h, s['hb'], *s5c, name=f"s5_bwd")
            else:
                s5_out, arrived = _s5_bwd(dcat, s['ypre'], h, s['hb'], *s5c, exchange=hosted, name=f"s5_bwd_reduce")
                plan.early_hosted(i, arrived)
            du, xb, gb, gq, dzzq, dyq, dlam, dbglu, dd = s5_out
            dbmat = _mm_tn_slabs(h, 128, gb, SLAB_COLS, S5_SLABS, name=f"d_s5_b")
            dcmat = _mm_tn_slabs(xb, SLAB_COLS, dyq, 128, S5_SLABS, name=f"d_s5_c")
            s5g = s['s5_vjp']((dlam, dbmat, dcmat))
            for n, g_ in zip(['ev_lambda_re', 'ev_lambda_im', 'ev_log_dt', 'ev_b_re', 'ev_b_im', 'ev_c_re', 'ev_c_im'], s5g):
                grads[n][li] = g_
            big['w_glu'] = _mm_tn(gq, 0, MIX, dzzq, also_bf16=True, name=f"d_glu")
            grads['ev_b_glu'][li] = dbglu.reshape(-1)
            grads['ev_d'][li] = dd.reshape(-1)
            d3, dcw = _conv_bwd(dcat, h, w['ev_conv_w'][li], name=f"conv_bwd")
            grads['ev_conv_w'][li] = dcw[0:3]
            big['w_in'] = both([_mm_tn(s['x0'], 0, D_MODEL, du, also_bf16=True, name=f"d_in_a"),
                                _mm_tn(s['x0'], 0, D_MODEL, d3, tn=3 * MIX, also_bf16=True, name=f"d_in_b")], 1)
            dh_parts = [(du, 0, MIX), (d3, 0, 3 * MIX)]
        else:
            dq, dk, dv, dvd = _attn_bwd(dcat, h, s['vdiag'], mask, name=f"attn_bwd")
            dzp, dpw, dps = _pool_bwd(dcat, h, s['pw'], s['ps'], name=f"pool_bwd")
            parts = [dq, dk, dv, dzp]
            grads['od_rel_bias'][li] = jnp.dot(dvd.reshape(ATT_HEADS, NKEY), onehot, precision=HIGHEST)
            grads['od_pool_w'][li] = dpw
            grads['od_pool_scale'][li] = dps.reshape(-1)
            big['w_in'] = both([_mm_tn(s['x0'], 0, D_MODEL, d_, also_bf16=True, name=f"d_in_a") for d_ in parts], 1)
            dh_parts = [(d_, 0, MIX) for d_ in parts]
        plan.layer_grads(i, big)

        def layer_input_grad(products, rows, vecs):
            return (ALPHA * rows[0] + products[0],), ()

        (da,) = _mm_rows([(dh_parts, lw['w_in'], True)], [dr1], [], [(D_MODEL, F32)], [], layer_input_grad,
                         name=f"in_proj_bwd_{i % 2}")
    return loss, da, {n: jnp.stack(g) for n, g in grads.items()}


def _sum_blocks(own, others, *, name):
    rows, cols = own.shape
    t = _row_tile(rows)

    def body(own_ref, others_ref, o_ref):
        acc = own_ref[...]
        for k in range(N_DEV - 1):
            acc = acc + others_ref[k].astype(F32)
        o_ref[...] = acc

    return pl.pallas_call(
        body, name=name, grid=(rows // t,),
        in_specs=[pl.BlockSpec((t, cols), lambda i: (i, 0)), pl.BlockSpec((N_DEV - 1, t, cols), lambda i: (0, i, 0))],
        out_specs=pl.BlockSpec((t, cols), lambda i: (i, 0)), out_shape=jax.ShapeDtypeStruct((rows, cols), F32),
        compiler_params=_cparams(("parallel",)),
    )(own, others)


def _swap_sibling(arrs, *, name):
    n = len(arrs)

    def body(*refs):
        ins, outs = refs[:n], refs[n:2 * n]
        send_sems, recv_sems = refs[2 * n:]
        x, y, c, _ = _place()
        cps = [pltpu.make_async_remote_copy(src_ref=ins[k], dst_ref=outs[k], send_sem=send_sems.at[k], recv_sem=recv_sems.at[k],
                                            device_id=(x, y, 1 - c), device_id_type=MESH) for k in range(n)]
        for cp in cps:
            cp.start()
        for cp in cps:
            cp.wait_recv()
        for cp in cps:
            cp.wait_send()

    return pl.pallas_call(
        body, name=name, in_specs=[ANY] * n, out_specs=[ANY] * n,
        out_shape=[jax.ShapeDtypeStruct(a.shape, a.dtype) for a in arrs],
        scratch_shapes=[pltpu.SemaphoreType.DMA((n,)), pltpu.SemaphoreType.DMA((n,))],
    )(*arrs)


class _ShardedPlan:
    def __init__(self, a, c, me):
        self.a, self.c, self.me = a, c, me
        self.weights, self.pending, self.own, self.arrived = {}, None, {}, {}

    def _shards(self, i):
        return [self.a[n][l].astype(BF16) for _, n, l in _layer_big(i)]

    def _with_own(self, gathered, own):
        return lax.dynamic_update_index_in_dim(gathered, own, self.me, 0)

    def _set_weights(self, i, gathered):
        lw = {}
        for (k, n, _), g, own in zip(_layer_big(i), gathered, self._shards(i)):
            _, rows, cols = g.shape
            g = self._with_own(g, own)
            lw[k] = g.transpose(1, 0, 2).reshape(rows, N_CHIPS * cols) if BIG[n][0] == 2 else g.reshape(N_CHIPS * rows, cols)
        self.weights[i] = lw

    def gather_first(self, misc):
        gathered = _run_exchange(_GatherExchange(self._shards(0) + [misc]), name="weight_gather_0")
        self._set_weights(0, gathered[:-1])
        return self._with_own(gathered[-1], misc)

    def layer_weights(self, i):
        return self.weights.pop(i)

    def forward_host(self, i):
        return _GatherExchange(self._shards(i + 1)) if i + 1 < DEPTH else None

    def forward_hosted(self, i, arrived):
        self._set_weights(i + 1, arrived)

    EARLY = ('w_up', 'w_down', 'w_proj', 'w_gate')

    def _reduce_exchange(self, i, g, early=None):
        items = [(k, n, l) for k, n, l in _layer_big(i) if early is None or (k in self.EARLY) == early]
        return [(n, l) for _, n, l in items], _ReduceExchange([g[k][1] for k, _, _ in items], [BIG[n][0] for _, n, _ in items])

    def early_host(self, i, g):
        if i != 0:
            return None
        self.early_keys, exchange = self._reduce_exchange(0, g, early=True)
        return exchange

    def early_hosted(self, i, arrived):
        self.arrived.update(zip(self.early_keys, arrived))

    def layer_grads(self, i, g):
        for k, n, l in _layer_big(i):
            full = g[k][0]
            kk, nn = full.shape
            if BIG[n][0] == 2:
                self.own[n, l] = lax.dynamic_slice(full, (self.c * (kk // 2), self.me * (nn // N_CHIPS)), (kk // 2, nn // N_CHIPS))
            else:
                self.own[n, l] = lax.dynamic_slice_in_dim(full, (2 * self.me + self.c) * (kk // N_DEV), kk // N_DEV, axis=0)
        self.pending = (i, g)

    def backward_host(self, i):
        if i + 1 >= DEPTH:
            return None
        self.hosted_keys, exchange = self._reduce_exchange(*self.pending)
        return exchange

    def backward_hosted(self, i, arrived):
        self.arrived.update(zip(self.hosted_keys, arrived))

    def reduced(self):
        late_keys, exchange = self._reduce_exchange(*self.pending, early=False)
        self.arrived.update(zip(late_keys, _run_exchange(exchange, name="grad_reduce_0")))
        keys = [(n, l) for n in BIG for l in range(BIG[n][1][0])]
        mine = [_sum_blocks(self.own[k], self.arrived[k], name=f"grad_sum_{k[0]}") for k in keys]
        theirs = _swap_sibling(mine, name="grad_half_swap")
        out = {}
        for n in BIG:
            layers = []
            for l in range(BIG[n][1][0]):
                a_, b_ = mine[keys.index((n, l))], theirs[keys.index((n, l))]
                layers.append(jnp.where(self.c == 0, jnp.concatenate([a_, b_], axis=0), jnp.concatenate([b_, a_], axis=0)))
            out[n] = jnp.stack(layers)
        return out


def kernel(x, p, ev_w_in, ev_lambda_re, ev_lambda_im, ev_log_dt, ev_b_re, ev_b_im, ev_c_re, ev_c_im, ev_d, ev_w_glu, ev_b_glu, ev_conv_w, ev_w_out, od_w_in, od_rel_bias, od_pool_w, od_pool_scale, od_w_out, ln_mix_g, ln_mix_b, ln_ffn_g, ln_ffn_b, ffn_w_up, ffn_w_down, ple_w_proj, ple_w_gate, ple_b_gate, loss_target, m_ev_w_in, m_ev_lambda_re, m_ev_lambda_im, m_ev_log_dt, m_ev_b_re, m_ev_b_im, m_ev_c_re, m_ev_c_im, m_ev_d, m_ev_w_glu, m_ev_b_glu, m_ev_conv_w, m_ev_w_out, m_od_w_in, m_od_rel_bias, m_od_pool_w, m_od_pool_scale, m_od_w_out, m_ln_mix_g, m_ln_mix_b, m_ln_ffn_g, m_ln_ffn_b, m_ffn_w_up, m_ffn_w_down, m_ple_w_proj, m_ple_w_gate, m_ple_b_gate, v_ev_w_in, v_ev_lambda_re, v_ev_lambda_im, v_ev_log_dt, v_ev_b_re, v_ev_b_im, v_ev_c_re, v_ev_c_im, v_ev_d, v_ev_w_glu, v_ev_b_glu, v_ev_conv_w, v_ev_w_out, v_od_w_in, v_od_rel_bias, v_od_pool_w, v_od_pool_scale, v_od_w_out, v_ln_mix_g, v_ln_mix_b, v_ln_ffn_g, v_ln_ffn_b, v_ffn_w_up, v_ffn_w_down, v_ple_w_proj, v_ple_w_gate, v_ple_b_gate):
    given = locals()
    a = {n: given[n] for n in INPUTS}
    x, y, c = lax.axis_index("x"), lax.axis_index("y"), lax.axis_index("c")
    me = 2 * x + y

    plan = _ShardedPlan(a, c, me)
    misc = jnp.concatenate([a['ev_conv_w'].reshape(6, 128), a['od_pool_scale'], jnp.zeros((8, 128), F32)], axis=0)
    gm = plan.gather_first(misc)
    w = {n: a[n] for n in REPLICATED}
    w['ev_conv_w'] = gm[:, 0:6].reshape(N_CHIPS, 2, 3, 128).transpose(1, 2, 0, 3).reshape(2, 3, 512)
    w['od_pool_scale'] = gm[:, 6:8].transpose(1, 0, 2).reshape(2, 512)

    loss, grad_x, grads = _local_step(a['x'][0], a['p'][:, 0], a['loss_target'][0], w, plan)
    loss = lax.psum(loss[0, 0], ("x", "y", "c"))

    small_names = REPLICATED + list(SMALL_SHARDED)
    small = _all_reduce_small(_pack([grads[n] for n in small_names]), name="small_grad_all_reduce")
    small = dict(zip(small_names, _unpack(small, [grads[n].shape for n in small_names])))
    for n in SMALL_SHARDED:
        small[n] = lax.dynamic_slice_in_dim(small[n], me * 128, 128, axis=small[n].ndim - 1)
    big = plan.reduced()

    res = {}
    for n in BIG:
        shape = a[n].shape
        flat = _shard_rows(n)
        d, m_, v_ = _adamw(a[n].reshape(flat), big[n].reshape(flat), a['m_' + n].reshape(flat), a['v_' + n].reshape(flat),
                           name=f"adamw_{n}")
        res[n] = (big[n], d.reshape(shape), m_.reshape(shape), v_.reshape(shape))
    shapes = [a[n].shape for n in small_names]
    d, m_, v_ = _adamw(_pack([a[n] for n in small_names]), _pack([small[n] for n in small_names]),
                       _pack([a['m_' + n] for n in small_names]), _pack([a['v_' + n] for n in small_names]), name="adamw_small")
    for n, dd, mm, vv in zip(small_names, _unpack(d, shapes), _unpack(m_, shapes), _unpack(v_, shapes)):
        res[n] = (small[n], dd, mm, vv)

    outs = [loss, grad_x[None]]
    for part in range(4):
        outs += [res[n][part] for n in WEIGHTS]
    return tuple(outs)
```
